```python
import math
import jax, jax.numpy as jnp
from jax import lax
import numpy as np

D_MODEL = 1024
BATCH = 4
SEQ = 8192
DEPTH = 1

ATT_WIDTH = D_MODEL // 2
HEAD_DIM = 64
N_ATT_HEADS = ATT_WIDTH // HEAD_DIM
SSM_WIDTH = D_MODEL - ATT_WIDTH
SSM_GROUP = 16
N_SSM_GROUPS = SSM_WIDTH // SSM_GROUP
SSM_STATE = 64
MIX_WIDTH = ATT_WIDTH + SSM_WIDTH
DILATED_PATTERNS = ((128, 1), (512, 4), (2048, 16))
ATT_BLOCK = 128
N_BUCKETS = 32
MAX_DISTANCE = 2048
N_EXPERTS = 256
TOP_K = 8
N_EXPERT_GROUPS = 8
TOPK_GROUPS = 4
EXPERT_FF = 256
SHARED_FF = 256
ROUTED_SCALE = 2.5
MOE_BLOCK = 128
ALPHA = (2 * DEPTH) ** 0.25
BETA = (8 * DEPTH) ** -0.25
EPS = 1e-5
NEG_INF = -1e30

kernel_name = 'hybrid_dilated_s5_moe_block'


def layer_norm(x, g, b):
    xf = x.astype(jnp.float32)
    mu = jnp.mean(xf, -1, keepdims=True)
    var = jnp.mean(jnp.square(xf - mu), -1, keepdims=True)
    return ((xf - mu) * lax.rsqrt(var + EPS) * g + b).astype(x.dtype)


def rms_norm(x, g):
    xf = x.astype(jnp.float32)
    return (xf * lax.rsqrt(jnp.mean(jnp.square(xf), -1, keepdims=True) + EPS) * g).astype(x.dtype)


def t5_bucket(dist):
    exact = N_BUCKETS // 2
    large = exact + (jnp.log(jnp.maximum(dist, 1).astype(jnp.float32) / exact)
                     / math.log(MAX_DISTANCE / exact) * (N_BUCKETS - exact)).astype(jnp.int32)
    return jnp.where(dist < exact, dist, jnp.minimum(large, N_BUCKETS - 1))


def dilated_window_pattern(q, k, v, rel_bias, window, dilation):
    bsz, seq, heads, hd = q.shape
    n_strided = seq // dilation
    n_blocks = -(-n_strided // ATT_BLOCK)
    padded = n_blocks * ATT_BLOCK

    def strided_blocks(t):
        t = t.reshape(bsz, n_strided, dilation, heads, hd)
        t = jnp.pad(t, ((0, 0), (0, padded - n_strided), (0, 0), (0, 0), (0, 0)))
        return t.reshape(bsz, n_blocks, ATT_BLOCK, dilation, heads, hd)

    def with_previous(t):
        prev = jnp.pad(t[:, :-1], ((0, 0), (1, 0), (0, 0), (0, 0), (0, 0), (0, 0)))
        return jnp.concatenate([prev, t], axis=2)

    qb = strided_blocks(q)
    kb = with_previous(strided_blocks(k))
    vb = with_previous(strided_blocks(v))
    qi = jnp.arange(ATT_BLOCK)[:, None]
    ki = jnp.arange(2 * ATT_BLOCK)[None, :]
    rel = qi + ATT_BLOCK - ki
    band = (rel >= 0) & (rel <= window // dilation)
    key_ok = (jnp.arange(n_blocks)[:, None] * ATT_BLOCK - ATT_BLOCK + ki) >= 0
    mask = band[None] & key_ok[:, None, :]
    bias = jnp.transpose(rel_bias[t5_bucket(jnp.maximum(rel, 0) * dilation)], (2, 0, 1)).astype(jnp.float32)
    s = jnp.einsum('bnqrhd,bnkrhd->bnrhqk', qb, kb).astype(jnp.float32) * HEAD_DIM ** -0.5 + bias
    s = jnp.where(mask[None, :, None, None], s, NEG_INF)
    m = jnp.max(s, -1, keepdims=True)
    p = jnp.exp(s - m)
    l = jnp.sum(p, -1)
    o = jnp.einsum('bnrhqk,bnkrhd->bnqrhd', p, vb.astype(jnp.float32))
    l_t = jnp.transpose(l, (0, 1, 4, 2, 3))
    o = o / l_t[..., None]
    lse = jnp.transpose(m[..., 0], (0, 1, 4, 2, 3)) + jnp.log(l_t)
    o = o.reshape(bsz, padded, dilation, heads, hd)[:, :n_strided].reshape(bsz, seq, heads, hd)
    lse = lse.reshape(bsz, padded, dilation, heads)[:, :n_strided].reshape(bsz, seq, heads)
    return o, lse


def dilated_attention(q, k, v, rel_bias):
    outs, lses = zip(*[dilated_window_pattern(q, k, v, rel_bias, w, d) for w, d in DILATED_PATTERNS])
    wts = jax.nn.softmax(jnp.stack(lses), axis=0)
    return jnp.sum(wts[..., None] * jnp.stack(outs), axis=0)


def s5_layer(u, a_re, a_im, b_re, b_im, c_re, c_im, d_skip, log_dt, w_glu, b_glu):
    bsz, seq, _ = u.shape
    f32 = jnp.float32
    uf = u.astype(f32).reshape(bsz, seq, N_SSM_GROUPS, SSM_GROUP)
    a = lax.complex(a_re.astype(f32), a_im.astype(f32))
    dt = jnp.exp(log_dt.astype(f32))[:, None]
    a_bar = jnp.exp(dt * a)
    b_bar = ((a_bar - 1.0) / a)[:, :, None] * lax.complex(b_re.astype(f32), b_im.astype(f32))
    c_mat = lax.complex(c_re.astype(f32), c_im.astype(f32))
    bu = jnp.einsum('bsgc,gpc->bsgp', uf.astype(jnp.complex64), b_bar)
    decay = jnp.broadcast_to(a_bar, bu.shape)

    def combine(e1, e2):
        a1, h1 = e1
        a2, h2 = e2
        return a1 * a2, a2 * h1 + h2

    _, state = lax.associative_scan(combine, (decay, bu), axis=1)
    y = jnp.einsum('bsgp,gcp->bsgc', state, c_mat).real + d_skip.astype(f32) * uf
    y = jax.nn.gelu(y.reshape(bsz, seq, SSM_WIDTH))
    y = y * jax.nn.sigmoid(y @ w_glu.astype(f32) + b_glu.astype(f32))
    return y.astype(u.dtype)


def parallel_mixer(h, w_in, rel_bias, a_re, a_im, b_re, b_im, c_re, c_im, d_skip, log_dt,
                   w_glu, b_glu, g_att, g_ssm, w_out):
    bsz, seq, _ = h.shape
    proj = h @ w_in
    q, k, v, u = jnp.split(proj, [ATT_WIDTH, 2 * ATT_WIDTH, 3 * ATT_WIDTH], axis=-1)
    heads = lambda t: t.reshape(bsz, seq, N_ATT_HEADS, HEAD_DIM)
    att = dilated_attention(heads(q), heads(k), heads(v), rel_bias).reshape(bsz, seq, ATT_WIDTH).astype(h.dtype)
    ssm = s5_layer(u, a_re, a_im, b_re, b_im, c_re, c_im, d_skip, log_dt, w_glu, b_glu)
    merged = jnp.concatenate([rms_norm(att, g_att), rms_norm(ssm, g_ssm)], axis=-1)
    return merged @ w_out


def moe_ffn(h, w_router, router_bias, w_e_gate, w_e_up, w_e_down, w_s_gate, w_s_up, w_s_down):
    bsz, seq, d = h.shape
    n_tok = bsz * seq
    xt = h.reshape(n_tok, d)
    scores = jax.nn.sigmoid((xt @ w_router).astype(jnp.float32))
    biased = scores + router_bias.astype(jnp.float32)
    grouped = biased.reshape(n_tok, N_EXPERT_GROUPS, N_EXPERTS // N_EXPERT_GROUPS)
    group_score = jnp.sum(lax.top_k(grouped, 2)[0], -1)
    _, top_groups = lax.top_k(group_score, TOPK_GROUPS)
    group_ok = jnp.any(top_groups[:, :, None] == jnp.arange(N_EXPERT_GROUPS)[None, None, :], axis=1)
    masked = jnp.where(group_ok[:, :, None], grouped, -jnp.inf).reshape(n_tok, N_EXPERTS)
    _, expert_idx = lax.top_k(masked, TOP_K)
    gate = jnp.take_along_axis(scores, expert_idx, axis=1)
    gate = gate / jnp.sum(gate, -1, keepdims=True) * ROUTED_SCALE
    n_assign = n_tok * TOP_K
    flat_e = expert_idx.reshape(-1)
    order = jnp.argsort(flat_e)
    sorted_e = flat_e[order]
    sorted_tok = (order // TOP_K).astype(jnp.int32)
    sorted_gate = gate.reshape(-1)[order]
    counts = jnp.bincount(flat_e, length=N_EXPERTS)
    padded_counts = (counts + MOE_BLOCK - 1) // MOE_BLOCK * MOE_BLOCK
    padded_end = jnp.cumsum(padded_counts)
    padded_start = padded_end - padded_counts
    start = jnp.cumsum(counts) - counts
    dest = padded_start[sorted_e] + jnp.arange(n_assign) - start[sorted_e]
    n_blocks = -(-n_assign // MOE_BLOCK) + N_EXPERTS
    n_rows = n_blocks * MOE_BLOCK
    row_tok = jnp.full((n_rows,), n_tok, jnp.int32).at[dest].set(sorted_tok)
    row_gate = jnp.zeros((n_rows,), jnp.float32).at[dest].set(sorted_gate)
    block_expert = jnp.minimum(jnp.searchsorted(padded_end, jnp.arange(n_blocks) * MOE_BLOCK, side='right'),
                               N_EXPERTS - 1)
    x_pad = jnp.concatenate([xt, jnp.zeros((1, d), xt.dtype)], axis=0)

    def expert_block(args):
        rows, g, e = args
        xb = x_pad[rows]
        hid = jax.nn.silu(xb @ w_e_gate[e]) * (xb @ w_e_up[e])
        return (hid @ w_e_down[e]) * g[:, None].astype(xb.dtype)

    routed = lax.map(expert_block, (row_tok.reshape(n_blocks, MOE_BLOCK),
                                    row_gate.reshape(n_blocks, MOE_BLOCK), block_expert))
    routed = jax.ops.segment_sum(routed.reshape(n_rows, d), row_tok, num_segments=n_tok + 1)[:n_tok]
    shared = (jax.nn.silu(xt @ w_s_gate) * (xt @ w_s_up)) @ w_s_down
    return (routed + shared).reshape(bsz, seq, d)


def setup_inputs(seed: int = 0) -> dict:
    key = jax.random.key(seed)
    ks = jax.random.split(key, 32)
    f32 = jnp.float32

    def nrm(k, shape, scale):
        return jax.random.normal(k, shape, f32) * scale

    L, G, P, GC = DEPTH, N_SSM_GROUPS, SSM_STATE, SSM_GROUP
    n_idx = jnp.arange(P, dtype=f32)
    return {
        'x': nrm(ks[0], (BATCH, SEQ, D_MODEL), 1.0),
        'c': nrm(ks[1], (BATCH, D_MODEL), 1.0),
        'rel_bias': nrm(ks[2], (N_BUCKETS, N_ATT_HEADS), 0.5),
        'w_ada': nrm(ks[3], (L, D_MODEL, 6 * D_MODEL), 0.5 * D_MODEL ** -0.5),
        'b_ada': nrm(ks[4], (L, 6 * D_MODEL), 0.02),
        'w_in': nrm(ks[5], (L, D_MODEL, 3 * ATT_WIDTH + SSM_WIDTH), D_MODEL ** -0.5),
        'ssm_a_re': -0.5 + nrm(ks[6], (L, G, P), 0.01),
        'ssm_a_im': math.pi * n_idx + nrm(ks[7], (L, G, P), 0.01),
        'ssm_b_re': nrm(ks[8], (L, G, P, GC), (2 * GC) ** -0.5),
        'ssm_b_im': nrm(ks[9], (L, G, P, GC), (2 * GC) ** -0.5),
        'ssm_c_re': nrm(ks[10], (L, G, GC, P), (2 * P) ** -0.5),
        'ssm_c_im': nrm(ks[11], (L, G, GC, P), (2 * P) ** -0.5),
        'ssm_d': nrm(ks[12], (L, G, GC), 0.5),
        'ssm_log_dt': jax.random.uniform(ks[13], (L, G), f32, math.log(0.001), math.log(0.1)),
        'w_glu': nrm(ks[14], (L, SSM_WIDTH, SSM_WIDTH), SSM_WIDTH ** -0.5),
        'b_glu': nrm(ks[15], (L, SSM_WIDTH), 0.02),
        'g_att': 1.0 + nrm(ks[16], (L, ATT_WIDTH), 0.02),
        'g_ssm': 1.0 + nrm(ks[17], (L, SSM_WIDTH), 0.02),
        'w_out': nrm(ks[18], (L, MIX_WIDTH, D_MODEL), BETA * MIX_WIDTH ** -0.5),
        'ln1_g': 1.0 + nrm(ks[19], (L, D_MODEL), 0.02),
        'ln1_b': nrm(ks[20], (L, D_MODEL), 0.02),
        'w_router': nrm(ks[21], (L, D_MODEL, N_EXPERTS), D_MODEL ** -0.5),
        'router_bias': nrm(ks[22], (L, N_EXPERTS), 0.01),
        'w_e_gate': nrm(ks[23], (L, N_EXPERTS, D_MODEL, EXPERT_FF), D_MODEL ** -0.5),
        'w_e_up': nrm(ks[24], (L, N_EXPERTS, D_MODEL, EXPERT_FF), D_MODEL ** -0.5),
        'w_e_down': nrm(ks[25], (L, N_EXPERTS, EXPERT_FF, D_MODEL), BETA * EXPERT_FF ** -0.5),
        'w_s_gate': nrm(ks[26], (L, D_MODEL, SHARED_FF), D_MODEL ** -0.5),
        'w_s_up': nrm(ks[27], (L, D_MODEL, SHARED_FF), D_MODEL ** -0.5),
        'w_s_down': nrm(ks[28], (L, SHARED_FF, D_MODEL), BETA * SHARED_FF ** -0.5),
        'ln2_g': 1.0 + nrm(ks[29], (L, D_MODEL), 0.02),
        'ln2_b': nrm(ks[30], (L, D_MODEL), 0.02),
    }


def reference(x, c, rel_bias, w_ada, b_ada, w_in, ssm_a_re, ssm_a_im, ssm_b_re, ssm_b_im,
              ssm_c_re, ssm_c_im, ssm_d, ssm_log_dt, w_glu, b_glu, g_att, g_ssm, w_out,
              ln1_g, ln1_b, w_router, router_bias, w_e_gate, w_e_up, w_e_down,
              w_s_gate, w_s_up, w_s_down, ln2_g, ln2_b):
    for layer in range(DEPTH):
        ada = jax.nn.silu(c) @ w_ada[layer] + b_ada[layer]
        shift1, scale1, gate1, shift2, scale2, gate2 = jnp.split(ada[:, None, :], 6, axis=-1)
        h = x * (1.0 + scale1) + shift1
        mix = parallel_mixer(h, w_in[layer], rel_bias, ssm_a_re[layer], ssm_a_im[layer],
                             ssm_b_re[layer], ssm_b_im[layer], ssm_c_re[layer], ssm_c_im[layer],
                             ssm_d[layer], ssm_log_dt[layer], w_glu[layer], b_glu[layer],
                             g_att[layer], g_ssm[layer], w_out[layer])
        x = layer_norm(ALPHA * x + gate1 * mix, ln1_g[layer], ln1_b[layer])
        h = x * (1.0 + scale2) + shift2
        ffn = moe_ffn(h, w_router[layer], router_bias[layer], w_e_gate[layer], w_e_up[layer],
                      w_e_down[layer], w_s_gate[layer], w_s_up[layer], w_s_down[layer])
        x = layer_norm(ALPHA * x + gate2 * ffn, ln2_g[layer], ln2_b[layer])
    return x
```

```python
import functools
import math

import jax
import jax.numpy as jnp
import numpy as np
from jax import lax
from jax.experimental import pallas as pl
from jax.experimental.pallas import tpu as pltpu

F32 = jnp.float32
BF16 = jnp.bfloat16

HEAD_DIM = 64
ATT_BLOCK = 128
PATTERNS = ((128, 1), (512, 4), (2048, 16))
N_BUCKETS = 32
MAX_DISTANCE = 2048
SSM_GROUP = 16
SSM_STATE = 64
N_EXPERTS = 256
TOP_K = 8
N_EXPERT_GROUPS = 8
TOPK_GROUPS = 4
ROUTED_SCALE = 2.5
EPS = 1e-5
NEG_INF = -1e30

LANES = 128
SUBLANES = 8
VMEM_LIMIT_BYTES = 56 * 1024 * 1024

PROJ_ROWS = 512
SSM_ROWS = 256
MIX_ROWS = 512
ROUTE_COLS = 512
MOE_ROWS = 256
DISPATCH_ROWS = 256


def _cparams(sem, vmem=VMEM_LIMIT_BYTES):
    return pltpu.CompilerParams(dimension_semantics=sem, vmem_limit_bytes=vmem)


def _dot(a, b):
    return jnp.dot(a, b, preferred_element_type=F32)


def _dot_nt(a, b):
    return lax.dot_general(a, b, (((1,), (1,)), ((), ())), preferred_element_type=F32)


def _silu(x):
    return x * jax.nn.sigmoid(x)


def _split_bf16(x):
    hi = x.astype(BF16)
    lo = (x - hi.astype(F32)).astype(BF16)
    return hi, lo


def _ada_kernel(c_ref, w_ref, b_ref, o_ref):
    c = c_ref[...]
    a_hi, a_lo = _split_bf16(_silu(c))
    w_hi, w_lo = _split_bf16(w_ref[...])
    acc = _dot(a_hi, w_hi) + _dot(a_hi, w_lo) + _dot(a_lo, w_hi)
    o_ref[...] = acc + b_ref[...]


def _ada(c, w_ada, b_ada):
    bsz, d = c.shape
    n = w_ada.shape[1]
    rows = SUBLANES
    c_pad = jnp.zeros((rows, d), F32).at[:bsz].set(c)
    tn = 1024
    out = pl.pallas_call(
        _ada_kernel,
        grid=(n // tn,),
        in_specs=[pl.BlockSpec((rows, d), lambda j: (0, 0)),
                  pl.BlockSpec((d, tn), lambda j: (0, j)),
                  pl.BlockSpec((1, tn), lambda j: (0, j))],
        out_specs=pl.BlockSpec((rows, tn), lambda j: (0, j)),
        out_shape=jax.ShapeDtypeStruct((rows, n), F32),
        compiler_params=_cparams(("parallel",)),
        name="ada",
    )(c_pad, w_ada, b_ada.reshape(1, n))
    return out[:bsz]


def _inproj_kernel(x_ref, sc_ref, sh_ref, w_ref, q_ref, k_ref, v_ref, u_ref, *, att_w, q_scale):
    h = (x_ref[0] * (1.0 + sc_ref[0]) + sh_ref[0]).astype(BF16)
    q_ref[0] = _dot(h, w_ref[:, 0:att_w]) * q_scale
    k_ref[0] = _dot(h, w_ref[:, att_w:2 * att_w])
    v_ref[0] = _dot(h, w_ref[:, 2 * att_w:3 * att_w])
    u_ref[0] = _dot(h, w_ref[:, 3 * att_w:])


def _inproj(x, scale, shift, w_in, att_w):
    bsz, seq, d = x.shape
    n = w_in.shape[1]
    ssm_w = n - 3 * att_w
    ts = min(PROJ_ROWS, seq)
    row = lambda b, i: (b, i, 0)
    per_b = lambda b, i: (b, 0, 0)
    kern = functools.partial(_inproj_kernel, att_w=att_w, q_scale=HEAD_DIM ** -0.5)
    return pl.pallas_call(
        kern,
        grid=(bsz, seq // ts),
        in_specs=[pl.BlockSpec((1, ts, d), row),
                  pl.BlockSpec((1, 1, d), per_b),
                  pl.BlockSpec((1, 1, d), per_b),
                  pl.BlockSpec((d, n), lambda b, i: (0, 0))],
        out_specs=[pl.BlockSpec((1, ts, att_w), row)] * 3 + [pl.BlockSpec((1, ts, ssm_w), row)],
        out_shape=[jax.ShapeDtypeStruct((bsz, seq, att_w), F32)] * 3
                  + [jax.ShapeDtypeStruct((bsz, seq, ssm_w), F32)],
        compiler_params=_cparams(("parallel", "parallel")),
        name="inproj",
    )(x, scale.reshape(bsz, 1, d), shift.reshape(bsz, 1, d), w_in.astype(BF16))


def _t5_bucket_np(dist):
    exact = N_BUCKETS // 2
    large = exact + (np.log(np.maximum(dist, 1).astype(np.float64) / exact)
                     / math.log(MAX_DISTANCE / exact) * (N_BUCKETS - exact)).astype(np.int64)
    return np.where(dist < exact, dist, np.minimum(large, N_BUCKETS - 1))


def _att_tables(rel_bias):
    qi = np.arange(ATT_BLOCK)[:, None]
    ki = np.arange(2 * ATT_BLOCK)[None, :]
    rel = qi + ATT_BLOCK - ki
    tabs = []
    for window, dil in PATTERNS:
        band = (rel >= 0) & (rel <= window // dil)
        bucket = _t5_bucket_np(np.maximum(rel, 0) * dil)
        bias = jnp.transpose(rel_bias[bucket], (2, 0, 1)).astype(F32)
        full = jnp.where(band[None], bias, NEG_INF)
        first = jnp.concatenate([full[:, :, ATT_BLOCK:], jnp.full_like(full[:, :, ATT_BLOCK:], NEG_INF)], axis=-1)
        tabs.append(jnp.stack([full, first]))
    return jnp.stack(tabs)


def _att_kernel(q_ref, k_ref, v_ref, tab_ref, o_ref, m_ref, l_ref, *, seq):
    lane = lax.broadcasted_iota(jnp.int32, (ATT_BLOCK, LANES), 1)
    head0 = lane < HEAD_DIM

    def rows(ref, start, n, dil):
        if dil == 1:
            return ref[0, pl.ds(pl.multiple_of(start, ATT_BLOCK), n), :]
        return ref[0, pl.ds(start, n, stride=dil), :]

    def unit(pi, dil, u):
        log_d = dil.bit_length() - 1
        res = u & (dil - 1)
        blk = u >> log_d
        first = 1 - jnp.minimum(blk, 1)
        q_start = res + dil * ATT_BLOCK * blk
        kv_start = res + dil * ATT_BLOCK * jnp.maximum(blk - 1, 0)
        q2 = rows(q_ref, q_start, ATT_BLOCK, dil)
        k2 = rows(k_ref, kv_start, 2 * ATT_BLOCK, dil).astype(BF16)
        v2 = rows(v_ref, kv_start, 2 * ATT_BLOCK, dil).astype(BF16)
        outs = []
        for h in range(2):
            qh = jnp.where(head0 if h == 0 else ~head0, q2, 0.0).astype(BF16)
            s = _dot_nt(qh, k2) + tab_ref[pi, first, h]
            m = jnp.max(s, axis=1, keepdims=True)
            p = jnp.exp(s - m)
            l = jnp.sum(p, axis=1, keepdims=True)
            o = _dot(p.astype(BF16), v2)
            outs.append((o, m, l))
        o = jnp.where(head0, outs[0][0], outs[1][0])
        m = jnp.where(head0, outs[0][1], outs[1][1])
        l = jnp.where(head0, outs[0][2], outs[1][2])
        if dil == 1:
            sl = pl.ds(pl.multiple_of(q_start, ATT_BLOCK), ATT_BLOCK)
        else:
            sl = pl.ds(q_start, ATT_BLOCK, stride=dil)
        if pi == 0:
            o_ref[0, sl, :] = o
            m_ref[sl, :] = m
            l_ref[sl, :] = l
        else:
            m_old = m_ref[sl, :]
            m_new = jnp.maximum(m_old, m)
            a_old = jnp.exp(m_old - m_new)
            a_cur = jnp.exp(m - m_new)
            o_ref[0, sl, :] = o_ref[0, sl, :] * a_old + o * a_cur
            l_ref[sl, :] = l_ref[sl, :] * a_old + l * a_cur
            m_ref[sl, :] = m_new

    n_units = seq // ATT_BLOCK
    for pi, (_, dil) in enumerate(PATTERNS):
        def body(u, carry, pi=pi, dil=dil):
            unit(pi, dil, u)
            return carry
        lax.fori_loop(0, n_units, body, 0)

    def finish(i, carry):
        sl = pl.ds(pl.multiple_of(i * ATT_BLOCK, ATT_BLOCK), ATT_BLOCK)
        o_ref[0, sl, :] = o_ref[0, sl, :] / l_ref[sl, :]
        return carry
    lax.fori_loop(0, n_units, finish, 0)


def _attention(q, k, v, tables):
    bsz, seq, att_w = q.shape
    n_pairs = att_w // LANES
    blk = lambda b, p: (b, 0, p)
    qkv_spec = pl.BlockSpec((1, seq, LANES), blk)
    n_pat = tables.shape[0]
    tab_spec = pl.BlockSpec((n_pat, 2, 2, ATT_BLOCK, 2 * ATT_BLOCK), lambda b, p: (0, 0, p, 0, 0))
    return pl.pallas_call(
        functools.partial(_att_kernel, seq=seq),
        grid=(bsz, n_pairs),
        in_specs=[qkv_spec, qkv_spec, qkv_spec, tab_spec],
        out_specs=pl.BlockSpec((1, seq, LANES), blk),
        out_shape=jax.ShapeDtypeStruct((bsz, seq, att_w), F32),
        scratch_shapes=[pltpu.VMEM((seq, LANES), F32), pltpu.VMEM((seq, LANES), F32)],
        compiler_params=_cparams(("parallel", "parallel")),
        name="dilated_attention",
    )(q, k, v, tables)


def _ssm_params(a_re, a_im, b_re, b_im, c_re, c_im, d_skip, log_dt):
    n_g, n_p = a_re.shape
    n_c = b_re.shape[-1]
    n_state = n_g * n_p
    dt = jnp.exp(log_dt.astype(F32))[:, None]

    def a_pow(kk):
        mag = jnp.exp(kk * dt * a_re)
        ph = kk * dt * a_im
        return mag * jnp.cos(ph), mag * jnp.sin(ph)

    ab_re, ab_im = a_pow(1.0)
    nr, ni = ab_re - 1.0, ab_im
    den = a_re * a_re + a_im * a_im
    f_re = (nr * a_re + ni * a_im) / den
    f_im = (ni * a_re - nr * a_im) / den
    bb_re = f_re[:, :, None] * b_re - f_im[:, :, None] * b_im
    bb_im = f_re[:, :, None] * b_im + f_im[:, :, None] * b_re
    eye = jnp.eye(n_g, dtype=F32)
    w_re = jnp.einsum('gpc,gh->gchp', bb_re, eye).reshape(n_g * n_c, n_state)
    w_im = jnp.einsum('gpc,gh->gchp', bb_im, eye).reshape(n_g * n_c, n_state)
    o_re = jnp.einsum('gcp,gh->gphc', c_re, eye).reshape(n_state, n_g * n_c)
    o_im = jnp.einsum('gcp,gh->gphc', c_im, eye).reshape(n_state, n_g * n_c)
    n_kb = (n_g * n_c) // LANES
    sp = n_state // n_kb
    w_in = jnp.stack([jnp.concatenate([w_re[kb * LANES:(kb + 1) * LANES, kb * sp:(kb + 1) * sp],
                                       w_im[kb * LANES:(kb + 1) * LANES, kb * sp:(kb + 1) * sp]], axis=1)
                      for kb in range(n_kb)])
    w_out = jnp.stack([jnp.concatenate([o_re[kb * sp:(kb + 1) * sp, kb * LANES:(kb + 1) * LANES],
                                        -o_im[kb * sp:(kb + 1) * sp, kb * LANES:(kb + 1) * LANES]], axis=0)
                       for kb in range(n_kb)])
    j = jnp.arange(SUBLANES, dtype=F32)[:, None]
    flat = lambda t: jnp.broadcast_to(t.reshape(1, n_state), (SUBLANES, n_state))
    coef = []
    for sh in (1, 2, 4):
        pr, pim = a_pow(float(sh))
        keep = (j >= sh).astype(F32)
        coef += [flat(pr) * keep, flat(pim) * keep]
    a_re_f = jnp.broadcast_to(a_re.reshape(1, n_state), (SUBLANES, n_state))
    a_im_f = jnp.broadcast_to(a_im.reshape(1, n_state), (SUBLANES, n_state))
    dt_f = jnp.broadcast_to(jnp.repeat(dt[:, 0], n_p).reshape(1, n_state), (SUBLANES, n_state))
    mag = jnp.exp((j + 1.0) * dt_f * a_re_f)
    ph = (j + 1.0) * dt_f * a_im_f
    coef += [mag * jnp.cos(ph), mag * jnp.sin(ph)]
    coef = jnp.stack(coef)
    return w_in.astype(BF16), w_out.astype(BF16), coef, d_skip.reshape(1, n_g * n_c).astype(F32)


def _gelu_tanh(x):
    return 0.5 * x * (1.0 + jnp.tanh(math.sqrt(2.0 / math.pi) * (x + 0.044715 * (x * x * x))))


def _ssm_kernel(u_ref, win_ref, wout_ref, coef_ref, d_ref, wglu_ref, bglu_ref, o_ref,
                hre_ref, him_ref, cre_ref, cim_ref, *, rows, n_state, col_w):
    n_kb = win_ref.shape[0]
    sp = n_state // n_kb

    @pl.when(pl.program_id(1) == 0)
    def _():
        cre_ref[...] = jnp.zeros_like(cre_ref)
        cim_ref[...] = jnp.zeros_like(cim_ref)

    u = u_ref[0]
    ub = u.astype(BF16)
    for kb in range(n_kb):
        bu = _dot(ub[:, kb * LANES:(kb + 1) * LANES], win_ref[kb])
        hre_ref[:, kb * sp:(kb + 1) * sp] = bu[:, :sp]
        him_ref[:, kb * sp:(kb + 1) * sp] = bu[:, sp:]

    n_slab = rows // SUBLANES
    for c0 in range(0, n_state, col_w):
        cs = slice(c0, c0 + col_w)

        def body(i, carry, cs=cs):
            c_re, c_im = carry
            sl = pl.ds(pl.multiple_of(i * SUBLANES, SUBLANES), SUBLANES)
            x_re = hre_ref[sl, cs]
            x_im = him_ref[sl, cs]
            for si, sh in enumerate((1, 2, 4)):
                p_re = coef_ref[2 * si, :, cs]
                p_im = coef_ref[2 * si + 1, :, cs]
                s_re = pltpu.roll(x_re, sh, axis=0)
                s_im = pltpu.roll(x_im, sh, axis=0)
                x_re, x_im = (x_re + p_re * s_re - p_im * s_im,
                              x_im + p_re * s_im + p_im * s_re)
            p_re = coef_ref[6, :, cs]
            p_im = coef_ref[7, :, cs]
            h_re = x_re + p_re * c_re - p_im * c_im
            h_im = x_im + p_re * c_im + p_im * c_re
            hre_ref[sl, cs] = h_re
            him_ref[sl, cs] = h_im
            last = slice(SUBLANES - 1, SUBLANES)
            return (jnp.broadcast_to(h_re[last, :], h_re.shape),
                    jnp.broadcast_to(h_im[last, :], h_im.shape))

        c_re, c_im = lax.fori_loop(0, n_slab, body, (cre_ref[:, cs], cim_ref[:, cs]))
        cre_ref[:, cs] = c_re
        cim_ref[:, cs] = c_im

    ys = []
    for kb in range(n_kb):
        hcat = jnp.concatenate([hre_ref[:, kb * sp:(kb + 1) * sp].astype(BF16),
                                him_ref[:, kb * sp:(kb + 1) * sp].astype(BF16)], axis=1)
        ys.append(_dot(hcat, wout_ref[kb]))
    y = jnp.concatenate(ys, axis=1) + d_ref[...] * u
    y = _gelu_tanh(y)
    z = _dot(y.astype(BF16), wglu_ref[...]) + bglu_ref[...]
    o_ref[0] = y * jax.nn.sigmoid(z)


def _ssm(u, w_in, w_out, coef, d_flat, w_glu, b_glu):
    bsz, seq, ssm_w = u.shape
    n_state = coef.shape[-1]
    rows = min(SSM_ROWS, seq)
    full = lambda *shape: pl.BlockSpec(shape, lambda b, i: (0,) * len(shape))
    kern = functools.partial(_ssm_kernel, rows=rows, n_state=n_state, col_w=4 * LANES)
    return pl.pallas_call(
        kern,
        grid=(bsz, seq // rows),
        in_specs=[pl.BlockSpec((1, rows, ssm_w), lambda b, i: (b, i, 0)),
                  full(*w_in.shape), full(*w_out.shape), full(*coef.shape), full(1, ssm_w),
                  full(ssm_w, ssm_w), full(1, ssm_w)],
        out_specs=pl.BlockSpec((1, rows, ssm_w), lambda b, i: (b, i, 0)),
        out_shape=jax.ShapeDtypeStruct((bsz, seq, ssm_w), F32),
        scratch_shapes=[pltpu.VMEM((rows, n_state), F32), pltpu.VMEM((rows, n_state), F32),
                        pltpu.VMEM((SUBLANES, n_state), F32), pltpu.VMEM((SUBLANES, n_state), F32)],
        compiler_params=_cparams(("parallel", "arbitrary")),
        name="s5_glu",
    )(u, w_in, w_out, coef, d_flat, w_glu.astype(BF16), b_glu.reshape(1, ssm_w))


def _layer_norm(y, g, b):
    mu = jnp.mean(y, axis=-1, keepdims=True)
    yc = y - mu
    var = jnp.mean(yc * yc, axis=-1, keepdims=True)
    return yc * lax.rsqrt(var + EPS) * g + b


def _rms_norm(y, g):
    return y * lax.rsqrt(jnp.mean(y * y, axis=-1, keepdims=True) + EPS) * g


def _mix_kernel(x_ref, att_ref, ssm_ref, gatt_ref, gssm_ref, wout_ref, g1_ref, ln_g_ref, ln_b_ref,
                sc_ref, sh_ref, g2_ref, wsg_ref, wsu_ref, wsd_ref, h_ref, pre_ref, *, alpha, att_w):
    a_n = _rms_norm(att_ref[0], gatt_ref[...]).astype(BF16)
    s_n = _rms_norm(ssm_ref[0], gssm_ref[...]).astype(BF16)
    mix = _dot(a_n, wout_ref[0:att_w, :]) + _dot(s_n, wout_ref[att_w:, :])
    x1 = _layer_norm(alpha * x_ref[0] + g1_ref[0] * mix, ln_g_ref[...], ln_b_ref[...])
    h = x1 * (1.0 + sc_ref[0]) + sh_ref[0]
    h_ref[0] = h
    hb = h.astype(BF16)
    hid = _silu(_dot(hb, wsg_ref[...])) * _dot(hb, wsu_ref[...])
    shared = _dot(hid.astype(BF16), wsd_ref[...])
    pre_ref[0] = alpha * x1 + g2_ref[0] * shared


def _mix(x, att, ssm, g_att, g_ssm, w_out, gate1, ln_g, ln_b, scale2, shift2, gate2,
         w_s_gate, w_s_up, w_s_down, alpha):
    bsz, seq, d = x.shape
    att_w = att.shape[-1]
    ssm_w = ssm.shape[-1]
    ff = w_s_gate.shape[1]
    tm = min(MIX_ROWS, seq)
    row = lambda b, i: (b, i, 0)
    per_b = lambda b, i: (b, 0, 0)
    full = lambda *shape: pl.BlockSpec(shape, lambda b, i: (0,) * len(shape))
    vec = lambda t: t.reshape(bsz, 1, d)
    return pl.pallas_call(
        functools.partial(_mix_kernel, alpha=alpha, att_w=att_w),
        grid=(bsz, seq // tm),
        in_specs=[pl.BlockSpec((1, tm, d), row), pl.BlockSpec((1, tm, att_w), row),
                  pl.BlockSpec((1, tm, ssm_w), row),
                  full(1, att_w), full(1, ssm_w), full(att_w + ssm_w, d),
                  pl.BlockSpec((1, 1, d), per_b), full(1, d), full(1, d),
                  pl.BlockSpec((1, 1, d), per_b), pl.BlockSpec((1, 1, d), per_b),
                  pl.BlockSpec((1, 1, d), per_b),
                  full(d, ff), full(d, ff), full(ff, d)],
        out_specs=[pl.BlockSpec((1, tm, d), row), pl.BlockSpec((1, tm, d), row)],
        out_shape=[jax.ShapeDtypeStruct((bsz, seq, d), F32)] * 2,
        compiler_params=_cparams(("parallel", "parallel")),
        name="mix_ln1_shared",
    )(x, att, ssm, g_att.reshape(1, att_w), g_ssm.reshape(1, ssm_w), w_out.astype(BF16),
      vec(gate1), ln_g.reshape(1, d), ln_b.reshape(1, d), vec(scale2), vec(shift2), vec(gate2),
      w_s_gate.astype(BF16), w_s_up.astype(BF16), w_s_down.astype(BF16))


def _route_kernel(h_ref, wrt_ref, wrt_lo_ref, bias_ref, tri_ref, idx_ref, gate_ref, rank_ref, cnt_ref, carry_ref):
    n_e = wrt_ref.shape[0]
    tm = h_ref.shape[0]
    per_group = n_e // N_EXPERT_GROUPS
    neg = -jnp.inf

    @pl.when(pl.program_id(0) == 0)
    def _():
        carry_ref[...] = jnp.zeros_like(carry_ref)

    h_hi, h_lo = _split_bf16(h_ref[...])
    logits = (_dot_nt(wrt_ref[...], h_hi) + _dot_nt(wrt_ref[...], h_lo)
              + _dot_nt(wrt_lo_ref[...], h_hi))
    scores = jax.nn.sigmoid(logits)
    biased = scores + bias_ref[...]

    g3 = biased.reshape(N_EXPERT_GROUPS, per_group, tm)
    ridx = lax.broadcasted_iota(jnp.int32, g3.shape, 1).astype(F32)
    m1 = jnp.max(g3, axis=1, keepdims=True)
    first = jnp.min(jnp.where(g3 == m1, ridx, float(per_group)), axis=1, keepdims=True)
    m2 = jnp.max(jnp.where(ridx == first, neg, g3), axis=1, keepdims=True)
    gs = m1 + m2

    gidx = lax.broadcasted_iota(jnp.int32, gs.shape, 0).astype(F32)
    ok = jnp.zeros_like(gs)
    cur = gs
    for _ in range(TOPK_GROUPS):
        mx = jnp.max(cur, axis=0, keepdims=True)
        fi = jnp.min(jnp.where(cur == mx, gidx, float(N_EXPERT_GROUPS)), axis=0, keepdims=True)
        hit = gidx == fi
        ok = jnp.where(hit, 1.0, ok)
        cur = jnp.where(hit, neg, cur)
    masked = jnp.where(ok > 0.5, g3, neg).reshape(n_e, tm)

    eidx = lax.broadcasted_iota(jnp.int32, (n_e, tm), 0).astype(F32)
    onehot = jnp.zeros((n_e, tm), F32)
    cur = masked
    sel_idx = []
    sel_gate = []
    for _ in range(TOP_K):
        mx = jnp.max(cur, axis=0, keepdims=True)
        fi = jnp.min(jnp.where(cur == mx, eidx, float(n_e)), axis=0, keepdims=True)
        hit = eidx == fi
        sel_idx.append(fi)
        sel_gate.append(jnp.sum(jnp.where(hit, scores, 0.0), axis=0, keepdims=True))
        onehot = jnp.where(hit, 1.0, onehot)
        cur = jnp.where(hit, neg, cur)
    idx = jnp.concatenate(sel_idx, axis=0)
    gate = jnp.concatenate(sel_gate, axis=0)
    gate = gate / jnp.sum(gate, axis=0, keepdims=True) * ROUTED_SCALE

    prior = _dot(onehot.astype(BF16), tri_ref[...]) + carry_ref[:, 0:1]
    ranks = [jnp.sum(jnp.where(eidx == sel_idx[k], prior, 0.0), axis=0, keepdims=True)
             for k in range(TOP_K)]
    rank = jnp.concatenate(ranks, axis=0)
    carry = carry_ref[...] + jnp.sum(onehot, axis=1, keepdims=True)
    carry_ref[...] = carry
    cnt_ref[...] = carry

    idx_ref[...] = idx.astype(jnp.int32)
    gate_ref[...] = gate
    rank_ref[...] = rank.astype(jnp.int32)


def _route(h2d, w_router, router_bias):
    n_tok, d = h2d.shape
    n_e = w_router.shape[1]
    tm = min(ROUTE_COLS, n_tok)
    tri = (np.arange(tm)[:, None] < np.arange(tm)[None, :]).astype(np.float32)
    w_hi, w_lo = _split_bf16(w_router.T.astype(F32))
    col = lambda i: (0, i)
    full = lambda *shape: pl.BlockSpec(shape, lambda i: (0,) * len(shape))
    idx, gate, rank, cnt = pl.pallas_call(
        _route_kernel,
        grid=(n_tok // tm,),
        in_specs=[pl.BlockSpec((tm, d), lambda i: (i, 0)), full(n_e, d), full(n_e, d), full(n_e, 1),
                  full(tm, tm)],
        out_specs=[pl.BlockSpec((TOP_K, tm), col), pl.BlockSpec((TOP_K, tm), col),
                   pl.BlockSpec((TOP_K, tm), col), full(n_e, LANES)],
        out_shape=[jax.ShapeDtypeStruct((TOP_K, n_tok), jnp.int32),
                   jax.ShapeDtypeStruct((TOP_K, n_tok), F32),
                   jax.ShapeDtypeStruct((TOP_K, n_tok), jnp.int32),
                   jax.ShapeDtypeStruct((n_e, LANES), F32)],
        scratch_shapes=[pltpu.VMEM((n_e, LANES), F32)],
        compiler_params=_cparams(("arbitrary",)),
        name="router_topk",
    )(h2d, w_hi, w_lo, router_bias.reshape(n_e, 1).astype(F32), jnp.asarray(tri, BF16))
    return idx, gate, rank, cnt[:, 0].astype(jnp.int32)


def _dispatch_kernel(pend_ref, cnt_ref, dest_ref, h_ref, xs_ref, zero_ref, sem_ref, *, n_e, blk):
    tm = h_ref.shape[0]

    def zero_copy(e):
        start = pl.multiple_of(pend_ref[e] - blk, blk)
        return pltpu.make_async_copy(zero_ref, xs_ref.at[pl.ds(start, blk)], sem_ref.at[0])

    @pl.when(pl.program_id(0) == 0)
    def _():
        zero_ref[...] = jnp.zeros_like(zero_ref)

        def start(e, c):
            @pl.when(cnt_ref[e] > 0)
            def _():
                zero_copy(e).start()
            return c
        lax.fori_loop(0, n_e, start, 0)

        def wait(e, c):
            @pl.when(cnt_ref[e] > 0)
            def _():
                zero_copy(e).wait()
            return c
        lax.fori_loop(0, n_e, wait, 0)

    def row_copy(t, k):
        return pltpu.make_async_copy(h_ref.at[pl.ds(t, 1)], xs_ref.at[pl.ds(dest_ref[k, t], 1)],
                                     sem_ref.at[1])

    def issue(t, c):
        for k in range(TOP_K):
            row_copy(t, k).start()
        return c
    lax.fori_loop(0, tm, issue, 0)

    def drain(t, c):
        for k in range(TOP_K):
            row_copy(t, k).wait()
        return c
    lax.fori_loop(0, tm, drain, 0)


def _dispatch(h2d, dest, pend, cnt, n_rows):
    n_tok, d = h2d.shape
    n_e = cnt.shape[0]
    tm = min(DISPATCH_ROWS, n_tok)
    grid_spec = pltpu.PrefetchScalarGridSpec(
        num_scalar_prefetch=2,
        grid=(n_tok // tm,),
        in_specs=[pl.BlockSpec((TOP_K, tm), lambda i, *_: (0, i), memory_space=pltpu.SMEM),
                  pl.BlockSpec((tm, d), lambda i, *_: (i, 0))],
        out_specs=pl.BlockSpec(memory_space=pl.ANY),
        scratch_shapes=[pltpu.VMEM((MOE_ROWS, d), F32), pltpu.SemaphoreType.DMA((2,))],
    )
    return pl.pallas_call(
        functools.partial(_dispatch_kernel, n_e=n_e, blk=MOE_ROWS),
        grid_spec=grid_spec,
        out_shape=jax.ShapeDtypeStruct((n_rows, d), F32),
        compiler_params=_cparams(("arbitrary",)),
        name="moe_dispatch",
    )(pend, cnt, dest, h2d)


def _expert_kernel(be_ref, nused_ref, x_ref, wg_ref, wu_ref, wd_ref, y_ref, wgb_ref, wub_ref, wdb_ref):
    i = pl.program_id(0)
    prev = be_ref[jnp.maximum(i - 1, 0)]
    fresh = jnp.logical_or(i == 0, be_ref[i] != prev)

    @pl.when(jnp.logical_and(i < nused_ref[0], fresh))
    def _():
        wgb_ref[...] = wg_ref[0].astype(BF16)
        wub_ref[...] = wu_ref[0].astype(BF16)
        wdb_ref[...] = wd_ref[0].astype(BF16)

    @pl.when(i < nused_ref[0])
    def _():
        xb = x_ref[...].astype(BF16)
        hid = _silu(_dot(xb, wgb_ref[...])) * _dot(xb, wub_ref[...])
        y_ref[...] = _dot(hid.astype(BF16), wdb_ref[...])


def _experts(xs, block_expert, n_used, w_gate, w_up, w_down):
    n_rows, d = xs.shape
    n_blk = n_rows // MOE_ROWS
    ff = w_gate.shape[-1]
    clamp = lambda i, be, nu: jnp.minimum(i, nu[0] - 1)
    grid_spec = pltpu.PrefetchScalarGridSpec(
        num_scalar_prefetch=2,
        grid=(n_blk,),
        in_specs=[pl.BlockSpec((MOE_ROWS, d), lambda i, be, nu: (clamp(i, be, nu), 0)),
                  pl.BlockSpec((1, d, ff), lambda i, be, nu: (be[clamp(i, be, nu)], 0, 0)),
                  pl.BlockSpec((1, d, ff), lambda i, be, nu: (be[clamp(i, be, nu)], 0, 0)),
                  pl.BlockSpec((1, ff, d), lambda i, be, nu: (be[clamp(i, be, nu)], 0, 0))],
        out_specs=pl.BlockSpec((MOE_ROWS, d), lambda i, be, nu: (clamp(i, be, nu), 0)),
        scratch_shapes=[pltpu.VMEM((d, ff), BF16), pltpu.VMEM((d, ff), BF16), pltpu.VMEM((ff, d), BF16)],
    )
    return pl.pallas_call(
        _expert_kernel,
        grid_spec=grid_spec,
        out_shape=jax.ShapeDtypeStruct((n_rows, d), F32),
        compiler_params=_cparams(("arbitrary",)),
        name="moe_experts",
    )(block_expert, n_used, xs, w_gate, w_up, w_down)


def _combine_kernel(dest_ref, ys_ref, gate_ref, pre_ref, g2_ref, ln_g_ref, ln_b_ref, o_ref, buf_ref, sem_ref):
    tm = pre_ref.shape[1]

    def row_copy(t, k):
        return pltpu.make_async_copy(ys_ref.at[pl.ds(dest_ref[k, t], 1)], buf_ref.at[k, pl.ds(t, 1)],
                                     sem_ref.at[0])

    def issue(t, c):
        for k in range(TOP_K):
            row_copy(t, k).start()
        return c
    lax.fori_loop(0, tm, issue, 0)

    def drain(t, c):
        for k in range(TOP_K):
            row_copy(t, k).wait()
        return c
    lax.fori_loop(0, tm, drain, 0)

    gates = gate_ref[...]
    routed = buf_ref[0] * gates[:, 0:1]
    for k in range(1, TOP_K):
        routed = routed + buf_ref[k] * gates[:, k:k + 1]
    y = pre_ref[0] + g2_ref[0] * routed
    o_ref[0] = _layer_norm(y, ln_g_ref[...], ln_b_ref[...])


def _combine(ys, dest, gates_t, pre, gate2, ln_g, ln_b):
    bsz, seq, d = pre.shape
    tm = min(DISPATCH_ROWS, seq)
    per_seq = seq // tm
    row = lambda i, *_: (i // per_seq, i % per_seq, 0)
    per_b = lambda i, *_: (i // per_seq, 0, 0)
    full = lambda *shape: pl.BlockSpec(shape, lambda i, *_: (0,) * len(shape))
    grid_spec = pltpu.PrefetchScalarGridSpec(
        num_scalar_prefetch=0,
        grid=(bsz * per_seq,),
        in_specs=[pl.BlockSpec((TOP_K, tm), lambda i: (0, i), memory_space=pltpu.SMEM),
                  pl.BlockSpec(memory_space=pl.ANY),
                  pl.BlockSpec((tm, TOP_K), lambda i: (i, 0)),
                  pl.BlockSpec((1, tm, d), row), pl.BlockSpec((1, 1, d), per_b), full(1, d), full(1, d)],
        out_specs=pl.BlockSpec((1, tm, d), row),
        scratch_shapes=[pltpu.VMEM((TOP_K, tm, d), F32), pltpu.SemaphoreType.DMA((1,))],
    )
    return pl.pallas_call(
        _combine_kernel,
        grid_spec=grid_spec,
        out_shape=jax.ShapeDtypeStruct((bsz, seq, d), F32),
        compiler_params=_cparams(("arbitrary",)),
        name="moe_combine_ln2",
    )(dest, ys, gates_t, pre, gate2.reshape(bsz, 1, d), ln_g.reshape(1, d), ln_b.reshape(1, d))


def _moe_plan(idx, rank, cnt, n_tok):
    n_e = cnt.shape[0]
    padded = (cnt + MOE_ROWS - 1) // MOE_ROWS * MOE_ROWS
    pend = jnp.cumsum(padded).astype(jnp.int32)
    pstart = pend - padded
    dest = (jnp.take(pstart, idx) + rank).astype(jnp.int32)
    n_blk = n_tok * TOP_K // MOE_ROWS + n_e
    block_expert = jnp.minimum(
        jnp.searchsorted(pend, jnp.arange(n_blk, dtype=jnp.int32) * MOE_ROWS, side='right'),
        n_e - 1).astype(jnp.int32)
    n_used = (pend[-1:] // MOE_ROWS).astype(jnp.int32)
    return dest, pend, block_expert, n_used, n_blk * MOE_ROWS


def kernel(x, c, rel_bias, w_ada, b_ada, w_in, ssm_a_re, ssm_a_im, ssm_b_re, ssm_b_im, ssm_c_re, ssm_c_im, ssm_d, ssm_log_dt, w_glu, b_glu, g_att, g_ssm, w_out, ln1_g, ln1_b, w_router, router_bias, w_e_gate, w_e_up, w_e_down, w_s_gate, w_s_up, w_s_down, ln2_g, ln2_b):
    bsz, seq, d = x.shape
    depth = w_ada.shape[0]
    alpha = (2 * depth) ** 0.25
    att_w = g_att.shape[-1]
    tables = _att_tables(rel_bias)
    for layer in range(depth):
        ada = _ada(c, w_ada[layer], b_ada[layer])
        shift1, scale1, gate1, shift2, scale2, gate2 = jnp.split(ada, 6, axis=-1)
        q, k, v, u = _inproj(x, scale1, shift1, w_in[layer], att_w)
        att = _attention(q, k, v, tables)
        ssm_prm = _ssm_params(ssm_a_re[layer], ssm_a_im[layer], ssm_b_re[layer], ssm_b_im[layer],
                              ssm_c_re[layer], ssm_c_im[layer], ssm_d[layer], ssm_log_dt[layer])
        ssm = _ssm(u, *ssm_prm, w_glu[layer], b_glu[layer])
        h2, pre = _mix(x, att, ssm, g_att[layer], g_ssm[layer], w_out[layer], gate1, ln1_g[layer],
                       ln1_b[layer], scale2, shift2, gate2, w_s_gate[layer], w_s_up[layer],
                       w_s_down[layer], alpha)
        h2d = h2.reshape(bsz * seq, d)
        idx, gate, rank, cnt = _route(h2d, w_router[layer], router_bias[layer])
        dest, pend, block_expert, n_used, n_rows = _moe_plan(idx, rank, cnt, bsz * seq)
        xs = _dispatch(h2d, dest, pend, cnt, n_rows)
        ys = _experts(xs, block_expert, n_used, w_e_gate[layer], w_e_up[layer], w_e_down[layer])
        x = _combine(ys, dest, gate.T, pre, gate2, ln2_g[layer], ln2_b[layer])
    return x
```

```python
import functools
import math

import jax
import jax.numpy as jnp
import numpy as np
from jax import lax
from jax.experimental import pallas as pl
from jax.experimental.pallas import tpu as pltpu

F32 = jnp.float32
BF16 = jnp.bfloat16

HEAD_DIM = 64
ATT_BLOCK = 128
PATTERNS = ((128, 1), (512, 4), (2048, 16))
N_BUCKETS = 32
MAX_DISTANCE = 2048
SSM_GROUP = 16
SSM_STATE = 64
N_EXPERTS = 256
TOP_K = 8
N_EXPERT_GROUPS = 8
TOPK_GROUPS = 4
ROUTED_SCALE = 2.5
EPS = 1e-5
NEG_INF = -1e30

LANES = 128
SUBLANES = 8
VMEM_LIMIT_BYTES = 56 * 1024 * 1024

PROJ_ROWS = 512
SSM_ROWS = 256
MIX_ROWS = 512
ROUTE_COLS = 512
MOE_ROWS = 256
DISPATCH_ROWS = 256
ATT_UNITS_PER_STEP = 4


def _cparams(sem, vmem=VMEM_LIMIT_BYTES):
    return pltpu.CompilerParams(dimension_semantics=sem, vmem_limit_bytes=vmem)


def _dot(a, b):
    return jnp.dot(a, b, preferred_element_type=F32)


def _dot_nt(a, b):
    return lax.dot_general(a, b, (((1,), (1,)), ((), ())), preferred_element_type=F32)


def _silu(x):
    return x * jax.nn.sigmoid(x)


def _split_bf16(x):
    hi = x.astype(BF16)
    lo = (x - hi.astype(F32)).astype(BF16)
    return hi, lo


def _ada_kernel(c_ref, w_ref, b_ref, o_ref):
    c = c_ref[...]
    a_hi, a_lo = _split_bf16(_silu(c))
    w_hi, w_lo = _split_bf16(w_ref[...])
    acc = _dot(a_hi, w_hi) + _dot(a_hi, w_lo) + _dot(a_lo, w_hi)
    o_ref[...] = acc + b_ref[...]


def _ada(c, w_ada, b_ada):
    bsz, d = c.shape
    n = w_ada.shape[1]
    rows = SUBLANES
    c_pad = jnp.zeros((rows, d), F32).at[:bsz].set(c)
    tn = 1024
    out = pl.pallas_call(
        _ada_kernel,
        grid=(n // tn,),
        in_specs=[pl.BlockSpec((rows, d), lambda j: (0, 0)),
                  pl.BlockSpec((d, tn), lambda j: (0, j)),
                  pl.BlockSpec((1, tn), lambda j: (0, j))],
        out_specs=pl.BlockSpec((rows, tn), lambda j: (0, j)),
        out_shape=jax.ShapeDtypeStruct((rows, n), F32),
        compiler_params=_cparams(("parallel",)),
        name="ada",
    )(c_pad, w_ada, b_ada.reshape(1, n))
    return out[:bsz]


def _inproj_kernel(x_ref, sc_ref, sh_ref, w_ref, q_ref, k_ref, v_ref, u_ref, *, att_w, q_scale):
    h = (x_ref[0] * (1.0 + sc_ref[0]) + sh_ref[0]).astype(BF16)
    q_ref[0] = _dot(h, w_ref[:, 0:att_w]) * q_scale
    k_ref[0] = _dot(h, w_ref[:, att_w:2 * att_w])
    v_ref[0] = _dot(h, w_ref[:, 2 * att_w:3 * att_w])
    u_ref[0] = _dot(h, w_ref[:, 3 * att_w:])


def _inproj(x, scale, shift, w_in, att_w):
    bsz, seq, d = x.shape
    n = w_in.shape[1]
    ssm_w = n - 3 * att_w
    ts = min(PROJ_ROWS, seq)
    row = lambda b, i: (b, i, 0)
    per_b = lambda b, i: (b, 0, 0)
    kern = functools.partial(_inproj_kernel, att_w=att_w, q_scale=HEAD_DIM ** -0.5)
    return pl.pallas_call(
        kern,
        grid=(bsz, seq // ts),
        in_specs=[pl.BlockSpec((1, ts, d), row),
                  pl.BlockSpec((1, 1, d), per_b),
                  pl.BlockSpec((1, 1, d), per_b),
                  pl.BlockSpec((d, n), lambda b, i: (0, 0))],
        out_specs=[pl.BlockSpec((1, ts, att_w), row)] * 3 + [pl.BlockSpec((1, ts, ssm_w), row)],
        out_shape=[jax.ShapeDtypeStruct((bsz, seq, att_w), F32)] * 3
                  + [jax.ShapeDtypeStruct((bsz, seq, ssm_w), F32)],
        compiler_params=_cparams(("parallel", "parallel")),
        name="inproj",
    )(x, scale.reshape(bsz, 1, d), shift.reshape(bsz, 1, d), w_in.astype(BF16))


def _t5_bucket_np(dist):
    exact = N_BUCKETS // 2
    large = exact + (np.log(np.maximum(dist, 1).astype(np.float64) / exact)
                     / math.log(MAX_DISTANCE / exact) * (N_BUCKETS - exact)).astype(np.int64)
    return np.where(dist < exact, dist, np.minimum(large, N_BUCKETS - 1))


def _att_tables(rel_bias):
    qi = np.arange(ATT_BLOCK)[:, None]
    ki = np.arange(2 * ATT_BLOCK)[None, :]
    rel = qi + ATT_BLOCK - ki
    tabs = []
    for window, dil in PATTERNS:
        band = (rel >= 0) & (rel <= window // dil)
        bucket = _t5_bucket_np(np.maximum(rel, 0) * dil)
        onehot = (bucket[:, :, None] == np.arange(N_BUCKETS)[None, None, :]).astype(np.float32)
        bias = jnp.einsum('qkb,bh->hqk', onehot, rel_bias.astype(F32),
                          precision=lax.Precision.HIGHEST)
        full = jnp.where(band[None], bias, NEG_INF)
        first = jnp.concatenate([full[:, :, ATT_BLOCK:], jnp.full_like(full[:, :, ATT_BLOCK:], NEG_INF)], axis=-1)
        tabs.append(jnp.stack([full, first]))
    return jnp.stack(tabs)


def _att_kernel(q_ref, k_ref, v_ref, tab_ref, o_ref, m_ref, l_ref, *, seq):
    lane = lax.broadcasted_iota(jnp.int32, (ATT_BLOCK, LANES), 1)
    head0 = lane < HEAD_DIM

    def rows(ref, start, n, dil):
        if dil == 1:
            return ref[0, pl.ds(pl.multiple_of(start, ATT_BLOCK), n), :]
        return ref[0, pl.ds(start, n, stride=dil), :]

    nb = ATT_UNITS_PER_STEP
    q_rows = nb * ATT_BLOCK

    def lane_mask(n_rows, h):
        m = lax.broadcasted_iota(jnp.int32, (n_rows, LANES), 1) < HEAD_DIM
        return m if h == 0 else ~m

    def batch(pi, dil, res, blk0, head, init):
        kv_blocks = nb if head else nb + 1
        q_start = res + dil * ATT_BLOCK * blk0
        kv_start = q_start if head else q_start - dil * ATT_BLOCK
        if dil == 1:
            sl = pl.ds(pl.multiple_of(q_start, ATT_BLOCK), q_rows)
        else:
            sl = pl.ds(q_start, q_rows, stride=dil)
        q = rows(q_ref, q_start, q_rows, dil)
        k = rows(k_ref, kv_start, kv_blocks * ATT_BLOCK, dil).astype(BF16)
        v = rows(v_ref, kv_start, kv_blocks * ATT_BLOCK, dil)
        if not init:
            m_old, l_old, o_old = m_ref[sl, :], l_ref[sl, :], o_ref[0, sl, :]
        qh = [jnp.where(lane_mask(q_rows, h), q, 0.0).astype(BF16) for h in range(2)]
        vh = [jnp.where(lane_mask(kv_blocks * ATT_BLOCK, h), v, 1.0).astype(BF16) for h in range(2)]
        o_parts, l_parts, m_parts = [], [], []
        for j in range(nb):
            first = 1 if (head and j == 0) else 0
            kb = j if (not head or j == 0) else j - 1
            ksl = slice(kb * ATT_BLOCK, (kb + 2) * ATT_BLOCK)
            qsl = slice(j * ATT_BLOCK, (j + 1) * ATT_BLOCK)
            outs, ms = [], []
            for h in range(2):
                s = _dot_nt(qh[h][qsl], k[ksl]) + tab_ref[pi, first, h]
                m = jnp.max(s, axis=1, keepdims=True)
                p = jnp.exp(s - m).astype(BF16)
                outs.append(_dot(p, vh[h][ksl]))
                ms.append(m)
            o_parts.append(jnp.where(head0, outs[0], outs[1]))
            l_parts.append(pltpu.roll(jnp.where(head0, outs[1], outs[0]), HEAD_DIM, axis=1))
            m_parts.append(jnp.where(head0, ms[0], ms[1]))
        o = jnp.concatenate(o_parts, axis=0)
        l = jnp.concatenate(l_parts, axis=0)
        m = jnp.concatenate(m_parts, axis=0)
        if not init:
            m_new = jnp.maximum(m_old, m)
            a_old = jnp.exp(m_old - m_new)
            a_cur = jnp.exp(m - m_new)
            o = o_old * a_old + o * a_cur
            l = l_old * a_old + l * a_cur
            m = m_new
        o_ref[0, sl, :] = o
        m_ref[sl, :] = m
        l_ref[sl, :] = l

    n_units = seq // ATT_BLOCK
    order = sorted(range(len(PATTERNS)), key=lambda p: -PATTERNS[p][1])
    for pos, pi in enumerate(order):
        dil = PATTERNS[pi][1]
        log_d = dil.bit_length() - 1
        per_res = seq // (dil * ATT_BLOCK) // nb
        init = pos == 0

        def head_body(res, carry, pi=pi, dil=dil, init=init):
            batch(pi, dil, res, 0, True, init)
            return carry
        lax.fori_loop(0, dil, head_body, 0)

        if per_res > 1:
            def tail_body(i, carry, pi=pi, dil=dil, log_d=log_d, init=init):
                res = i & (dil - 1)
                blk0 = nb * (1 + (i >> log_d))
                batch(pi, dil, res, blk0, False, init)
                return carry
            lax.fori_loop(0, dil * (per_res - 1), tail_body, 0)

    def finish(i, carry):
        sl = pl.ds(pl.multiple_of(i * ATT_BLOCK, ATT_BLOCK), ATT_BLOCK)
        o_ref[0, sl, :] = o_ref[0, sl, :] / l_ref[sl, :]
        return carry
    lax.fori_loop(0, n_units, finish, 0)


def _attention(q, k, v, tables):
    bsz, seq, att_w = q.shape
    n_pairs = att_w // LANES
    blk = lambda b, p: (b, 0, p)
    qkv_spec = pl.BlockSpec((1, seq, LANES), blk)
    n_pat = tables.shape[0]
    tab_spec = pl.BlockSpec((n_pat, 2, 2, ATT_BLOCK, 2 * ATT_BLOCK), lambda b, p: (0, 0, p, 0, 0))
    return pl.pallas_call(
        functools.partial(_att_kernel, seq=seq),
        grid=(bsz, n_pairs),
        in_specs=[qkv_spec, qkv_spec, qkv_spec, tab_spec],
        out_specs=pl.BlockSpec((1, seq, LANES), blk),
        out_shape=jax.ShapeDtypeStruct((bsz, seq, att_w), F32),
        scratch_shapes=[pltpu.VMEM((seq, LANES), F32), pltpu.VMEM((seq, LANES), F32)],
        compiler_params=_cparams(("parallel", "parallel")),
        name="dilated_attention",
    )(q, k, v, tables)


def _ssm_params(a_re, a_im, b_re, b_im, c_re, c_im, d_skip, log_dt):
    n_g, n_p = a_re.shape
    n_c = b_re.shape[-1]
    n_state = n_g * n_p
    dt = jnp.exp(log_dt.astype(F32))[:, None]

    def a_pow(kk):
        mag = jnp.exp(kk * dt * a_re)
        ph = kk * dt * a_im
        return mag * jnp.cos(ph), mag * jnp.sin(ph)

    ab_re, ab_im = a_pow(1.0)
    nr, ni = ab_re - 1.0, ab_im
    den = a_re * a_re + a_im * a_im
    f_re = (nr * a_re + ni * a_im) / den
    f_im = (ni * a_re - nr * a_im) / den
    bb_re = f_re[:, :, None] * b_re - f_im[:, :, None] * b_im
    bb_im = f_re[:, :, None] * b_im + f_im[:, :, None] * b_re
    eye = jnp.eye(n_g, dtype=F32)
    w_re = jnp.einsum('gpc,gh->gchp', bb_re, eye).reshape(n_g * n_c, n_state)
    w_im = jnp.einsum('gpc,gh->gchp', bb_im, eye).reshape(n_g * n_c, n_state)
    o_re = jnp.einsum('gcp,gh->gphc', c_re, eye).reshape(n_state, n_g * n_c)
    o_im = jnp.einsum('gcp,gh->gphc', c_im, eye).reshape(n_state, n_g * n_c)
    n_kb = (n_g * n_c) // LANES
    sp = n_state // n_kb
    w_in = jnp.stack([jnp.concatenate([w_re[kb * LANES:(kb + 1) * LANES, kb * sp:(kb + 1) * sp],
                                       w_im[kb * LANES:(kb + 1) * LANES, kb * sp:(kb + 1) * sp]], axis=1)
                      for kb in range(n_kb)])
    w_out = jnp.stack([jnp.concatenate([o_re[kb * sp:(kb + 1) * sp, kb * LANES:(kb + 1) * LANES],
                                        -o_im[kb * sp:(kb + 1) * sp, kb * LANES:(kb + 1) * LANES]], axis=0)
                       for kb in range(n_kb)])
    j = jnp.arange(SUBLANES, dtype=F32)[:, None]
    flat = lambda t: jnp.broadcast_to(t.reshape(1, n_state), (SUBLANES, n_state))
    coef = []
    for sh in (1, 2, 4):
        pr, pim = a_pow(float(sh))
        keep = (j >= sh).astype(F32)
        coef += [flat(pr) * keep, flat(pim) * keep]
    a_re_f = jnp.broadcast_to(a_re.reshape(1, n_state), (SUBLANES, n_state))
    a_im_f = jnp.broadcast_to(a_im.reshape(1, n_state), (SUBLANES, n_state))
    dt_f = jnp.broadcast_to(jnp.repeat(dt[:, 0], n_p).reshape(1, n_state), (SUBLANES, n_state))
    mag = jnp.exp((j + 1.0) * dt_f * a_re_f)
    ph = (j + 1.0) * dt_f * a_im_f
    coef += [mag * jnp.cos(ph), mag * jnp.sin(ph)]
    coef = jnp.stack(coef)
    return w_in.astype(BF16), w_out.astype(BF16), coef, d_skip.reshape(1, n_g * n_c).astype(F32)


def _gelu_tanh(x):
    return 0.5 * x * (1.0 + jnp.tanh(math.sqrt(2.0 / math.pi) * (x + 0.044715 * (x * x * x))))


def _ssm_kernel(u_ref, win_ref, wout_ref, coef_ref, d_ref, wglu_ref, bglu_ref, o_ref,
                hre_ref, him_ref, cre_ref, cim_ref, *, rows, n_state, col_w):
    n_kb = win_ref.shape[0]
    sp = n_state // n_kb

    @pl.when(pl.program_id(1) == 0)
    def _():
        cre_ref[...] = jnp.zeros_like(cre_ref)
        cim_ref[...] = jnp.zeros_like(cim_ref)

    u = u_ref[0]
    ub = u.astype(BF16)
    for kb in range(n_kb):
        bu = _dot(ub[:, kb * LANES:(kb + 1) * LANES], win_ref[kb])
        hre_ref[:, kb * sp:(kb + 1) * sp] = bu[:, :sp]
        him_ref[:, kb * sp:(kb + 1) * sp] = bu[:, sp:]

    n_slab = rows // SUBLANES
    for c0 in range(0, n_state, col_w):
        cs = slice(c0, c0 + col_w)

        def body(i, carry, cs=cs):
            c_re, c_im = carry
            sl = pl.ds(pl.multiple_of(i * SUBLANES, SUBLANES), SUBLANES)
            x_re = hre_ref[sl, cs]
            x_im = him_ref[sl, cs]
            for si, sh in enumerate((1, 2, 4)):
                p_re = coef_ref[2 * si, :, cs]
                p_im = coef_ref[2 * si + 1, :, cs]
                s_re = pltpu.roll(x_re, sh, axis=0)
                s_im = pltpu.roll(x_im, sh, axis=0)
                x_re, x_im = (x_re + p_re * s_re - p_im * s_im,
                              x_im + p_re * s_im + p_im * s_re)
            p_re = coef_ref[6, :, cs]
            p_im = coef_ref[7, :, cs]
            h_re = x_re + p_re * c_re - p_im * c_im
            h_im = x_im + p_re * c_im + p_im * c_re
            hre_ref[sl, cs] = h_re
            him_ref[sl, cs] = h_im
            last = slice(SUBLANES - 1, SUBLANES)
            return (jnp.broadcast_to(h_re[last, :], h_re.shape),
                    jnp.broadcast_to(h_im[last, :], h_im.shape))

        c_re, c_im = lax.fori_loop(0, n_slab, body, (cre_ref[:, cs], cim_ref[:, cs]))
        cre_ref[:, cs] = c_re
        cim_ref[:, cs] = c_im

    ys = []
    for kb in range(n_kb):
        hcat = jnp.concatenate([hre_ref[:, kb * sp:(kb + 1) * sp].astype(BF16),
                                him_ref[:, kb * sp:(kb + 1) * sp].astype(BF16)], axis=1)
        ys.append(_dot(hcat, wout_ref[kb]))
    y = jnp.concatenate(ys, axis=1) + d_ref[...] * u
    y = _gelu_tanh(y)
    z = _dot(y.astype(BF16), wglu_ref[...]) + bglu_ref[...]
    o_ref[0] = y * jax.nn.sigmoid(z)


def _ssm(u, w_in, w_out, coef, d_flat, w_glu, b_glu):
    bsz, seq, ssm_w = u.shape
    n_state = coef.shape[-1]
    rows = min(SSM_ROWS, seq)
    full = lambda *shape: pl.BlockSpec(shape, lambda b, i: (0,) * len(shape))
    kern = functools.partial(_ssm_kernel, rows=rows, n_state=n_state, col_w=4 * LANES)
    return pl.pallas_call(
        kern,
        grid=(bsz, seq // rows),
        in_specs=[pl.BlockSpec((1, rows, ssm_w), lambda b, i: (b, i, 0)),
                  full(*w_in.shape), full(*w_out.shape), full(*coef.shape), full(1, ssm_w),
                  full(ssm_w, ssm_w), full(1, ssm_w)],
        out_specs=pl.BlockSpec((1, rows, ssm_w), lambda b, i: (b, i, 0)),
        out_shape=jax.ShapeDtypeStruct((bsz, seq, ssm_w), F32),
        scratch_shapes=[pltpu.VMEM((rows, n_state), F32), pltpu.VMEM((rows, n_state), F32),
                        pltpu.VMEM((SUBLANES, n_state), F32), pltpu.VMEM((SUBLANES, n_state), F32)],
        compiler_params=_cparams(("parallel", "arbitrary")),
        name="s5_glu",
    )(u, w_in, w_out, coef, d_flat, w_glu.astype(BF16), b_glu.reshape(1, ssm_w))


def _layer_norm(y, g, b):
    mu = jnp.mean(y, axis=-1, keepdims=True)
    yc = y - mu
    var = jnp.mean(yc * yc, axis=-1, keepdims=True)
    return yc * lax.rsqrt(var + EPS) * g + b


def _rms_norm(y, g):
    return y * lax.rsqrt(jnp.mean(y * y, axis=-1, keepdims=True) + EPS) * g


def _mix_kernel(x_ref, att_ref, ssm_ref, gatt_ref, gssm_ref, wout_ref, g1_ref, ln_g_ref, ln_b_ref,
                sc_ref, sh_ref, g2_ref, wsg_ref, wsu_ref, wsd_ref, h_ref, pre_ref, *, alpha, att_w):
    a_n = _rms_norm(att_ref[0], gatt_ref[...]).astype(BF16)
    s_n = _rms_norm(ssm_ref[0], gssm_ref[...]).astype(BF16)
    mix = _dot(a_n, wout_ref[0:att_w, :]) + _dot(s_n, wout_ref[att_w:, :])
    x1 = _layer_norm(alpha * x_ref[0] + g1_ref[0] * mix, ln_g_ref[...], ln_b_ref[...])
    h = x1 * (1.0 + sc_ref[0]) + sh_ref[0]
    h_ref[0] = h
    hb = h.astype(BF16)
    hid = _silu(_dot(hb, wsg_ref[...])) * _dot(hb, wsu_ref[...])
    shared = _dot(hid.astype(BF16), wsd_ref[...])
    pre_ref[0] = alpha * x1 + g2_ref[0] * shared


def _mix(x, att, ssm, g_att, g_ssm, w_out, gate1, ln_g, ln_b, scale2, shift2, gate2,
         w_s_gate, w_s_up, w_s_down, alpha):
    bsz, seq, d = x.shape
    att_w = att.shape[-1]
    ssm_w = ssm.shape[-1]
    ff = w_s_gate.shape[1]
    tm = min(MIX_ROWS, seq)
    row = lambda b, i: (b, i, 0)
    per_b = lambda b, i: (b, 0, 0)
    full = lambda *shape: pl.BlockSpec(shape, lambda b, i: (0,) * len(shape))
    vec = lambda t: t.reshape(bsz, 1, d)
    return pl.pallas_call(
        functools.partial(_mix_kernel, alpha=alpha, att_w=att_w),
        grid=(bsz, seq // tm),
        in_specs=[pl.BlockSpec((1, tm, d), row), pl.BlockSpec((1, tm, att_w), row),
                  pl.BlockSpec((1, tm, ssm_w), row),
                  full(1, att_w), full(1, ssm_w), full(att_w + ssm_w, d),
                  pl.BlockSpec((1, 1, d), per_b), full(1, d), full(1, d),
                  pl.BlockSpec((1, 1, d), per_b), pl.BlockSpec((1, 1, d), per_b),
                  pl.BlockSpec((1, 1, d), per_b),
                  full(d, ff), full(d, ff), full(ff, d)],
        out_specs=[pl.BlockSpec((1, tm, d), row), pl.BlockSpec((1, tm, d), row)],
        out_shape=[jax.ShapeDtypeStruct((bsz, seq, d), F32)] * 2,
        compiler_params=_cparams(("parallel", "parallel")),
        name="mix_ln1_shared",
    )(x, att, ssm, g_att.reshape(1, att_w), g_ssm.reshape(1, ssm_w), w_out.astype(BF16),
      vec(gate1), ln_g.reshape(1, d), ln_b.reshape(1, d), vec(scale2), vec(shift2), vec(gate2),
      w_s_gate.astype(BF16), w_s_up.astype(BF16), w_s_down.astype(BF16))


def _route_kernel(h_ref, wrt_ref, wrt_lo_ref, bias_ref, tri_ref, idx_ref, gate_ref, rank_ref, cnt_ref, carry_ref):
    n_e = wrt_ref.shape[0]
    tm = h_ref.shape[0]
    per_group = n_e // N_EXPERT_GROUPS
    neg = -jnp.inf

    @pl.when(pl.program_id(0) == 0)
    def _():
        carry_ref[...] = jnp.zeros_like(carry_ref)

    h_hi, h_lo = _split_bf16(h_ref[...])
    logits = (_dot_nt(wrt_ref[...], h_hi) + _dot_nt(wrt_ref[...], h_lo)
              + _dot_nt(wrt_lo_ref[...], h_hi))
    scores = jax.nn.sigmoid(logits)
    biased = scores + bias_ref[...]

    g3 = biased.reshape(N_EXPERT_GROUPS, per_group, tm)
    ridx = lax.broadcasted_iota(jnp.int32, g3.shape, 1).astype(F32)
    m1 = jnp.max(g3, axis=1, keepdims=True)
    first = jnp.min(jnp.where(g3 == m1, ridx, float(per_group)), axis=1, keepdims=True)
    m2 = jnp.max(jnp.where(ridx == first, neg, g3), axis=1, keepdims=True)
    gs = m1 + m2

    gidx = lax.broadcasted_iota(jnp.int32, gs.shape, 0).astype(F32)
    ok = jnp.zeros_like(gs)
    cur = gs
    for _ in range(TOPK_GROUPS):
        mx = jnp.max(cur, axis=0, keepdims=True)
        fi = jnp.min(jnp.where(cur == mx, gidx, float(N_EXPERT_GROUPS)), axis=0, keepdims=True)
        hit = gidx == fi
        ok = jnp.where(hit, 1.0, ok)
        cur = jnp.where(hit, neg, cur)
    masked = jnp.where(ok > 0.5, g3, neg).reshape(n_e, tm)

    eidx = lax.broadcasted_iota(jnp.int32, (n_e, tm), 0).astype(F32)
    onehot = jnp.zeros((n_e, tm), F32)
    cur = masked
    sel_idx = []
    sel_gate = []
    for _ in range(TOP_K):
        mx = jnp.max(cur, axis=0, keepdims=True)
        fi = jnp.min(jnp.where(cur == mx, eidx, float(n_e)), axis=0, keepdims=True)
        hit = eidx == fi
        sel_idx.append(fi)
        sel_gate.append(jnp.sum(jnp.where(hit, scores, 0.0), axis=0, keepdims=True))
        onehot = jnp.where(hit, 1.0, onehot)
        cur = jnp.where(hit, neg, cur)
    idx = jnp.concatenate(sel_idx, axis=0)
    gate = jnp.concatenate(sel_gate, axis=0)
    gate = gate / jnp.sum(gate, axis=0, keepdims=True) * ROUTED_SCALE

    prior = _dot(onehot.astype(BF16), tri_ref[...]) + carry_ref[:, 0:1]
    ranks = [jnp.sum(jnp.where(eidx == sel_idx[k], prior, 0.0), axis=0, keepdims=True)
             for k in range(TOP_K)]
    rank = jnp.concatenate(ranks, axis=0)
    carry = carry_ref[...] + jnp.sum(onehot, axis=1, keepdims=True)
    carry_ref[...] = carry
    cnt_ref[...] = carry

    idx_ref[...] = idx.astype(jnp.int32)
    gate_ref[...] = gate
    rank_ref[...] = rank.astype(jnp.int32)


def _route(h2d, w_router, router_bias):
    n_tok, d = h2d.shape
    n_e = w_router.shape[1]
    tm = min(ROUTE_COLS, n_tok)
    tri = (np.arange(tm)[:, None] < np.arange(tm)[None, :]).astype(np.float32)
    w_hi, w_lo = _split_bf16(w_router.T.astype(F32))
    col = lambda i: (0, i)
    full = lambda *shape: pl.BlockSpec(shape, lambda i: (0,) * len(shape))
    idx, gate, rank, cnt = pl.pallas_call(
        _route_kernel,
        grid=(n_tok // tm,),
        in_specs=[pl.BlockSpec((tm, d), lambda i: (i, 0)), full(n_e, d), full(n_e, d), full(n_e, 1),
                  full(tm, tm)],
        out_specs=[pl.BlockSpec((TOP_K, tm), col), pl.BlockSpec((TOP_K, tm), col),
                   pl.BlockSpec((TOP_K, tm), col), full(n_e, LANES)],
        out_shape=[jax.ShapeDtypeStruct((TOP_K, n_tok), jnp.int32),
                   jax.ShapeDtypeStruct((TOP_K, n_tok), F32),
                   jax.ShapeDtypeStruct((TOP_K, n_tok), jnp.int32),
                   jax.ShapeDtypeStruct((n_e, LANES), F32)],
        scratch_shapes=[pltpu.VMEM((n_e, LANES), F32)],
        compiler_params=_cparams(("arbitrary",)),
        name="router_topk",
    )(h2d, w_hi, w_lo, router_bias.reshape(n_e, 1).astype(F32), jnp.asarray(tri, BF16))
    return idx, gate, rank, cnt[:, 0].astype(jnp.int32)


def _dispatch_kernel(pend_ref, cnt_ref, dest_ref, h_ref, xs_ref, zero_ref, sem_ref, *, n_e, blk):
    tm = h_ref.shape[0]

    def zero_copy(e):
        start = pl.multiple_of(pend_ref[e] - blk, blk)
        return pltpu.make_async_copy(zero_ref, xs_ref.at[pl.ds(start, blk)], sem_ref.at[0])

    @pl.when(pl.program_id(0) == 0)
    def _():
        zero_ref[...] = jnp.zeros_like(zero_ref)

        def start(e, c):
            @pl.when(cnt_ref[e] > 0)
            def _():
                zero_copy(e).start()
            return c
        lax.fori_loop(0, n_e, start, 0)

        def wait(e, c):
            @pl.when(cnt_ref[e] > 0)
            def _():
                zero_copy(e).wait()
            return c
        lax.fori_loop(0, n_e, wait, 0)

    def row_copy(t, k):
        return pltpu.make_async_copy(h_ref.at[pl.ds(t, 1)], xs_ref.at[pl.ds(dest_ref[k, t], 1)],
                                     sem_ref.at[1])

    def issue(t, c):
        for k in range(TOP_K):
            row_copy(t, k).start()
        return c
    lax.fori_loop(0, tm, issue, 0)

    def drain(t, c):
        for k in range(TOP_K):
            row_copy(t, k).wait()
        return c
    lax.fori_loop(0, tm, drain, 0)


def _dispatch(h2d, dest, pend, cnt, n_rows):
    n_tok, d = h2d.shape
    n_e = cnt.shape[0]
    tm = min(DISPATCH_ROWS, n_tok)
    grid_spec = pltpu.PrefetchScalarGridSpec(
        num_scalar_prefetch=2,
        grid=(n_tok // tm,),
        in_specs=[pl.BlockSpec((TOP_K, tm), lambda i, *_: (0, i), memory_space=pltpu.SMEM),
                  pl.BlockSpec((tm, d), lambda i, *_: (i, 0))],
        out_specs=pl.BlockSpec(memory_space=pl.ANY),
        scratch_shapes=[pltpu.VMEM((MOE_ROWS, d), F32), pltpu.SemaphoreType.DMA((2,))],
    )
    return pl.pallas_call(
        functools.partial(_dispatch_kernel, n_e=n_e, blk=MOE_ROWS),
        grid_spec=grid_spec,
        out_shape=jax.ShapeDtypeStruct((n_rows, d), F32),
        compiler_params=_cparams(("arbitrary",)),
        name="moe_dispatch",
    )(pend, cnt, dest, h2d)


def _expert_kernel(be_ref, nused_ref, x_ref, wg_ref, wu_ref, wd_ref, y_ref, wgb_ref, wub_ref, wdb_ref):
    i = pl.program_id(0)
    prev = be_ref[jnp.maximum(i - 1, 0)]
    fresh = jnp.logical_or(i == 0, be_ref[i] != prev)

    @pl.when(jnp.logical_and(i < nused_ref[0], fresh))
    def _():
        wgb_ref[...] = wg_ref[0].astype(BF16)
        wub_ref[...] = wu_ref[0].astype(BF16)
        wdb_ref[...] = wd_ref[0].astype(BF16)

    @pl.when(i < nused_ref[0])
    def _():
        xb = x_ref[...].astype(BF16)
        hid = _silu(_dot(xb, wgb_ref[...])) * _dot(xb, wub_ref[...])
        y_ref[...] = _dot(hid.astype(BF16), wdb_ref[...])


def _experts(xs, block_expert, n_used, w_gate, w_up, w_down):
    n_rows, d = xs.shape
    n_blk = n_rows // MOE_ROWS
    ff = w_gate.shape[-1]
    clamp = lambda i, be, nu: jnp.minimum(i, nu[0] - 1)
    grid_spec = pltpu.PrefetchScalarGridSpec(
        num_scalar_prefetch=2,
        grid=(n_blk,),
        in_specs=[pl.BlockSpec((MOE_ROWS, d), lambda i, be, nu: (clamp(i, be, nu), 0)),
                  pl.BlockSpec((1, d, ff), lambda i, be, nu: (be[clamp(i, be, nu)], 0, 0)),
                  pl.BlockSpec((1, d, ff), lambda i, be, nu: (be[clamp(i, be, nu)], 0, 0)),
                  pl.BlockSpec((1, ff, d), lambda i, be, nu: (be[clamp(i, be, nu)], 0, 0))],
        out_specs=pl.BlockSpec((MOE_ROWS, d), lambda i, be, nu: (clamp(i, be, nu), 0)),
        scratch_shapes=[pltpu.VMEM((d, ff), BF16), pltpu.VMEM((d, ff), BF16), pltpu.VMEM((ff, d), BF16)],
    )
    return pl.pallas_call(
        _expert_kernel,
        grid_spec=grid_spec,
        out_shape=jax.ShapeDtypeStruct((n_rows, d), F32),
        compiler_params=_cparams(("arbitrary",)),
        name="moe_experts",
    )(block_expert, n_used, xs, w_gate, w_up, w_down)


def _combine_kernel(dest_ref, ys_ref, gate_ref, pre_ref, g2_ref, ln_g_ref, ln_b_ref, o_ref, buf_ref, sem_ref):
    tm = pre_ref.shape[1]

    def row_copy(t, k):
        return pltpu.make_async_copy(ys_ref.at[pl.ds(dest_ref[k, t], 1)], buf_ref.at[k, pl.ds(t, 1)],
                                     sem_ref.at[0])

    def issue(t, c):
        for k in range(TOP_K):
            row_copy(t, k).start()
        return c
    lax.fori_loop(0, tm, issue, 0)

    def drain(t, c):
        for k in range(TOP_K):
            row_copy(t, k).wait()
        return c
    lax.fori_loop(0, tm, drain, 0)

    gates = gate_ref[...]
    routed = buf_ref[0] * gates[:, 0:1]
    for k in range(1, TOP_K):
        routed = routed + buf_ref[k] * gates[:, k:k + 1]
    y = pre_ref[0] + g2_ref[0] * routed
    o_ref[0] = _layer_norm(y, ln_g_ref[...], ln_b_ref[...])


def _combine(ys, dest, gates_t, pre, gate2, ln_g, ln_b):
    bsz, seq, d = pre.shape
    tm = min(DISPATCH_ROWS, seq)
    per_seq = seq // tm
    row = lambda i, *_: (i // per_seq, i % per_seq, 0)
    per_b = lambda i, *_: (i // per_seq, 0, 0)
    full = lambda *shape: pl.BlockSpec(shape, lambda i, *_: (0,) * len(shape))
    grid_spec = pltpu.PrefetchScalarGridSpec(
        num_scalar_prefetch=0,
        grid=(bsz * per_seq,),
        in_specs=[pl.BlockSpec((TOP_K, tm), lambda i: (0, i), memory_space=pltpu.SMEM),
                  pl.BlockSpec(memory_space=pl.ANY),
                  pl.BlockSpec((tm, TOP_K), lambda i: (i, 0)),
                  pl.BlockSpec((1, tm, d), row), pl.BlockSpec((1, 1, d), per_b), full(1, d), full(1, d)],
        out_specs=pl.BlockSpec((1, tm, d), row),
        scratch_shapes=[pltpu.VMEM((TOP_K, tm, d), F32), pltpu.SemaphoreType.DMA((1,))],
    )
    return pl.pallas_call(
        _combine_kernel,
        grid_spec=grid_spec,
        out_shape=jax.ShapeDtypeStruct((bsz, seq, d), F32),
        compiler_params=_cparams(("arbitrary",)),
        name="moe_combine_ln2",
    )(dest, ys, gates_t, pre, gate2.reshape(bsz, 1, d), ln_g.reshape(1, d), ln_b.reshape(1, d))


def _dest_kernel(pstart_ref, idx_ref, rank_ref, dest_ref):
    idx = idx_ref[...]

    def body(e, acc):
        return acc + jnp.where(idx == e, pstart_ref[e], 0)
    dest_ref[...] = lax.fori_loop(0, pstart_ref.shape[0], body, rank_ref[...], unroll=8)


def _dest_rows(pstart, idx, rank):
    n_k, n_tok = idx.shape
    tn = min(4096, n_tok)
    grid_spec = pltpu.PrefetchScalarGridSpec(
        num_scalar_prefetch=1,
        grid=(n_tok // tn,),
        in_specs=[pl.BlockSpec((n_k, tn), lambda i, *_: (0, i))] * 2,
        out_specs=pl.BlockSpec((n_k, tn), lambda i, *_: (0, i)),
    )
    return pl.pallas_call(
        _dest_kernel,
        grid_spec=grid_spec,
        out_shape=jax.ShapeDtypeStruct((n_k, n_tok), jnp.int32),
        compiler_params=_cparams(("arbitrary",)),
        name="moe_dest",
    )(pstart, idx, rank)


def _moe_plan(idx, rank, cnt, n_tok):
    n_e = cnt.shape[0]
    padded = (cnt + MOE_ROWS - 1) // MOE_ROWS * MOE_ROWS
    pend = jnp.cumsum(padded).astype(jnp.int32)
    pstart = pend - padded
    dest = _dest_rows(pstart, idx, rank)
    n_blk = n_tok * TOP_K // MOE_ROWS + n_e
    starts = jnp.arange(n_blk, dtype=jnp.int32) * MOE_ROWS
    block_expert = jnp.minimum(jnp.sum((pend[None, :] <= starts[:, None]).astype(jnp.int32), axis=1),
                               n_e - 1).astype(jnp.int32)
    n_used = (pend[-1:] // MOE_ROWS).astype(jnp.int32)
    return dest, pend, block_expert, n_used, n_blk * MOE_ROWS


def kernel(x, c, rel_bias, w_ada, b_ada, w_in, ssm_a_re, ssm_a_im, ssm_b_re, ssm_b_im, ssm_c_re, ssm_c_im, ssm_d, ssm_log_dt, w_glu, b_glu, g_att, g_ssm, w_out, ln1_g, ln1_b, w_router, router_bias, w_e_gate, w_e_up, w_e_down, w_s_gate, w_s_up, w_s_down, ln2_g, ln2_b):
    bsz, seq, d = x.shape
    depth = w_ada.shape[0]
    alpha = (2 * depth) ** 0.25
    att_w = g_att.shape[-1]
    tables = _att_tables(rel_bias)
    for layer in range(depth):
        ada = _ada(c, w_ada[layer], b_ada[layer])
        shift1, scale1, gate1, shift2, scale2, gate2 = jnp.split(ada, 6, axis=-1)
        q, k, v, u = _inproj(x, scale1, shift1, w_in[layer], att_w)
        att = _attention(q, k, v, tables)
        ssm_prm = _ssm_params(ssm_a_re[layer], ssm_a_im[layer], ssm_b_re[layer], ssm_b_im[layer],
                              ssm_c_re[layer], ssm_c_im[layer], ssm_d[layer], ssm_log_dt[layer])
        ssm = _ssm(u, *ssm_prm, w_glu[layer], b_glu[layer])
        h2, pre = _mix(x, att, ssm, g_att[layer], g_ssm[layer], w_out[layer], gate1, ln1_g[layer],
                       ln1_b[layer], scale2, shift2, gate2, w_s_gate[layer], w_s_up[layer],
                       w_s_down[layer], alpha)
        h2d = h2.reshape(bsz * seq, d)
        idx, gate, rank, cnt = _route(h2d, w_router[layer], router_bias[layer])
        dest, pend, block_expert, n_used, n_rows = _moe_plan(idx, rank, cnt, bsz * seq)
        xs = _dispatch(h2d, dest, pend, cnt, n_rows)
        ys = _experts(xs, block_expert, n_used, w_e_gate[layer], w_e_up[layer], w_e_down[layer])
        x = _combine(ys, dest, gate.T, pre, gate2, ln2_g[layer], ln2_b[layer])
    return x
```

```python
import functools
import math

import jax
import jax.numpy as jnp
import numpy as np
from jax import lax
from jax.experimental import pallas as pl
from jax.experimental.pallas import tpu as pltpu

F32 = jnp.float32
BF16 = jnp.bfloat16

HEAD_DIM = 64
ATT_BLOCK = 128
PATTERNS = ((128, 1), (512, 4), (2048, 16))
N_BUCKETS = 32
MAX_DISTANCE = 2048
SSM_GROUP = 16
SSM_STATE = 64
N_EXPERTS = 256
TOP_K = 8
N_EXPERT_GROUPS = 8
TOPK_GROUPS = 4
ROUTED_SCALE = 2.5
EPS = 1e-5
NEG_INF = -1e30

LANES = 128
SUBLANES = 8
VMEM_LIMIT_BYTES = 56 * 1024 * 1024

PROJ_ROWS = 512
SSM_ROWS = 256
MIX_ROWS = 512
ROUTE_COLS = 512
MOE_ROWS = 256
DISPATCH_ROWS = 256
ATT_UNITS_PER_STEP = 4
DMA_ISSUE_UNROLL = 4


def _cparams(sem, vmem=VMEM_LIMIT_BYTES):
    return pltpu.CompilerParams(dimension_semantics=sem, vmem_limit_bytes=vmem)


def _dot(a, b):
    return jnp.dot(a, b, preferred_element_type=F32)


def _dot_nt(a, b):
    return lax.dot_general(a, b, (((1,), (1,)), ((), ())), preferred_element_type=F32)


def _silu(x):
    return x * jax.nn.sigmoid(x)


def _store_token_tiles(ref, x):
    m, d = x.shape
    n_sub = d // LANES
    for s in range(n_sub):
        ref[pl.ds(s, m, stride=n_sub), :] = x[:, s * LANES:(s + 1) * LANES]


def _load_token_tiles(ref, m, d, row0=0):
    n_sub = d // LANES
    return [ref[pl.ds(row0 + s, m, stride=n_sub), :] for s in range(n_sub)]


def _split_bf16(x):
    hi = x.astype(BF16)
    lo = (x - hi.astype(F32)).astype(BF16)
    return hi, lo


def _ada_kernel(c_ref, w_ref, b_ref, o_ref):
    c = c_ref[...]
    a_hi, a_lo = _split_bf16(_silu(c))
    w_hi, w_lo = _split_bf16(w_ref[...])
    acc = _dot(a_hi, w_hi) + _dot(a_hi, w_lo) + _dot(a_lo, w_hi)
    o_ref[...] = acc + b_ref[...]


def _ada(c, w_ada, b_ada):
    bsz, d = c.shape
    n = w_ada.shape[1]
    rows = SUBLANES
    c_pad = jnp.zeros((rows, d), F32).at[:bsz].set(c)
    tn = 1024
    out = pl.pallas_call(
        _ada_kernel,
        grid=(n // tn,),
        in_specs=[pl.BlockSpec((rows, d), lambda j: (0, 0)),
                  pl.BlockSpec((d, tn), lambda j: (0, j)),
                  pl.BlockSpec((1, tn), lambda j: (0, j))],
        out_specs=pl.BlockSpec((rows, tn), lambda j: (0, j)),
        out_shape=jax.ShapeDtypeStruct((rows, n), F32),
        compiler_params=_cparams(("parallel",)),
        name="ada",
    )(c_pad, w_ada, b_ada.reshape(1, n))
    return out[:bsz]


def _inproj_kernel(x_ref, sc_ref, sh_ref, w_ref, q_ref, k_ref, v_ref, u_ref, *, att_w, q_scale):
    h = (x_ref[0] * (1.0 + sc_ref[0]) + sh_ref[0]).astype(BF16)
    q_ref[0] = _dot(h, w_ref[:, 0:att_w]) * q_scale
    k_ref[0] = _dot(h, w_ref[:, att_w:2 * att_w])
    v_ref[0] = _dot(h, w_ref[:, 2 * att_w:3 * att_w])
    u_ref[0] = _dot(h, w_ref[:, 3 * att_w:])


def _inproj(x, scale, shift, w_in, att_w):
    bsz, seq, d = x.shape
    n = w_in.shape[1]
    ssm_w = n - 3 * att_w
    ts = min(PROJ_ROWS, seq)
    row = lambda b, i: (b, i, 0)
    per_b = lambda b, i: (b, 0, 0)
    kern = functools.partial(_inproj_kernel, att_w=att_w, q_scale=HEAD_DIM ** -0.5)
    return pl.pallas_call(
        kern,
        grid=(bsz, seq // ts),
        in_specs=[pl.BlockSpec((1, ts, d), row),
                  pl.BlockSpec((1, 1, d), per_b),
                  pl.BlockSpec((1, 1, d), per_b),
                  pl.BlockSpec((d, n), lambda b, i: (0, 0))],
        out_specs=[pl.BlockSpec((1, ts, att_w), row)] * 3 + [pl.BlockSpec((1, ts, ssm_w), row)],
        out_shape=[jax.ShapeDtypeStruct((bsz, seq, att_w), F32)] * 3
                  + [jax.ShapeDtypeStruct((bsz, seq, ssm_w), F32)],
        compiler_params=_cparams(("parallel", "parallel")),
        name="inproj",
    )(x, scale.reshape(bsz, 1, d), shift.reshape(bsz, 1, d), w_in.astype(BF16))


def _t5_bucket_np(dist):
    exact = N_BUCKETS // 2
    large = exact + (np.log(np.maximum(dist, 1).astype(np.float64) / exact)
                     / math.log(MAX_DISTANCE / exact) * (N_BUCKETS - exact)).astype(np.int64)
    return np.where(dist < exact, dist, np.minimum(large, N_BUCKETS - 1))


def _att_tables(rel_bias):
    qi = np.arange(ATT_BLOCK)[:, None]
    ki = np.arange(2 * ATT_BLOCK)[None, :]
    rel = qi + ATT_BLOCK - ki
    tabs = []
    for window, dil in PATTERNS:
        band = (rel >= 0) & (rel <= window // dil)
        bucket = _t5_bucket_np(np.maximum(rel, 0) * dil)
        onehot = (bucket[:, :, None] == np.arange(N_BUCKETS)[None, None, :]).astype(np.float32)
        bias = jnp.einsum('qkb,bh->hqk', onehot, rel_bias.astype(F32),
                          precision=lax.Precision.HIGHEST)
        full = jnp.where(band[None], bias, NEG_INF)
        first = jnp.concatenate([full[:, :, ATT_BLOCK:], jnp.full_like(full[:, :, ATT_BLOCK:], NEG_INF)], axis=-1)
        tabs.append(jnp.stack([full, first]))
    return jnp.stack(tabs)


def _att_kernel(q_ref, k_ref, v_ref, tab_ref, o_ref, m_ref, l_ref, *, seq):
    lane = lax.broadcasted_iota(jnp.int32, (ATT_BLOCK, LANES), 1)
    head0 = lane < HEAD_DIM

    def rows(ref, start, n, dil):
        if dil == 1:
            return ref[0, pl.ds(pl.multiple_of(start, ATT_BLOCK), n), :]
        return ref[0, pl.ds(start, n, stride=dil), :]

    nb = ATT_UNITS_PER_STEP
    q_rows = nb * ATT_BLOCK

    def lane_mask(n_rows, h):
        m = lax.broadcasted_iota(jnp.int32, (n_rows, LANES), 1) < HEAD_DIM
        return m if h == 0 else ~m

    def batch(pi, dil, res, blk0, head, init):
        kv_blocks = nb if head else nb + 1
        q_start = res + dil * ATT_BLOCK * blk0
        kv_start = q_start if head else q_start - dil * ATT_BLOCK
        if dil == 1:
            sl = pl.ds(pl.multiple_of(q_start, ATT_BLOCK), q_rows)
        else:
            sl = pl.ds(q_start, q_rows, stride=dil)
        q = rows(q_ref, q_start, q_rows, dil)
        k = rows(k_ref, kv_start, kv_blocks * ATT_BLOCK, dil).astype(BF16)
        v = rows(v_ref, kv_start, kv_blocks * ATT_BLOCK, dil)
        if not init:
            m_old, l_old, o_old = m_ref[sl, :], l_ref[sl, :], o_ref[0, sl, :]
        qh = [jnp.where(lane_mask(q_rows, h), q, 0.0).astype(BF16) for h in range(2)]
        vh = [jnp.where(lane_mask(kv_blocks * ATT_BLOCK, h), v, 1.0).astype(BF16) for h in range(2)]
        o_parts, l_parts, m_parts = [], [], []
        for j in range(nb):
            first = 1 if (head and j == 0) else 0
            kb = j if (not head or j == 0) else j - 1
            ksl = slice(kb * ATT_BLOCK, (kb + 2) * ATT_BLOCK)
            qsl = slice(j * ATT_BLOCK, (j + 1) * ATT_BLOCK)
            outs, ms = [], []
            for h in range(2):
                s = _dot_nt(qh[h][qsl], k[ksl]) + tab_ref[pi, first, h]
                m = jnp.max(s, axis=1, keepdims=True)
                p = jnp.exp(s - m).astype(BF16)
                outs.append(_dot(p, vh[h][ksl]))
                ms.append(m)
            o_parts.append(jnp.where(head0, outs[0], outs[1]))
            l_parts.append(pltpu.roll(jnp.where(head0, outs[1], outs[0]), HEAD_DIM, axis=1))
            m_parts.append(jnp.where(head0, ms[0], ms[1]))
        o = jnp.concatenate(o_parts, axis=0)
        l = jnp.concatenate(l_parts, axis=0)
        m = jnp.concatenate(m_parts, axis=0)
        if not init:
            m_new = jnp.maximum(m_old, m)
            a_old = jnp.exp(m_old - m_new)
            a_cur = jnp.exp(m - m_new)
            o = o_old * a_old + o * a_cur
            l = l_old * a_old + l * a_cur
            m = m_new
        o_ref[0, sl, :] = o
        m_ref[sl, :] = m
        l_ref[sl, :] = l

    n_units = seq // ATT_BLOCK
    order = sorted(range(len(PATTERNS)), key=lambda p: -PATTERNS[p][1])
    for pos, pi in enumerate(order):
        dil = PATTERNS[pi][1]
        log_d = dil.bit_length() - 1
        per_res = seq // (dil * ATT_BLOCK) // nb
        init = pos == 0

        def head_body(res, carry, pi=pi, dil=dil, init=init):
            batch(pi, dil, res, 0, True, init)
            return carry
        lax.fori_loop(0, dil, head_body, 0)

        if per_res > 1:
            def tail_body(i, carry, pi=pi, dil=dil, log_d=log_d, init=init):
                res = i & (dil - 1)
                blk0 = nb * (1 + (i >> log_d))
                batch(pi, dil, res, blk0, False, init)
                return carry
            lax.fori_loop(0, dil * (per_res - 1), tail_body, 0)

    def finish(i, carry):
        sl = pl.ds(pl.multiple_of(i * ATT_BLOCK, ATT_BLOCK), ATT_BLOCK)
        o_ref[0, sl, :] = o_ref[0, sl, :] / l_ref[sl, :]
        return carry
    lax.fori_loop(0, n_units, finish, 0)


def _attention(q, k, v, tables):
    bsz, seq, att_w = q.shape
    n_pairs = att_w // LANES
    blk = lambda b, p: (b, 0, p)
    qkv_spec = pl.BlockSpec((1, seq, LANES), blk)
    n_pat = tables.shape[0]
    tab_spec = pl.BlockSpec((n_pat, 2, 2, ATT_BLOCK, 2 * ATT_BLOCK), lambda b, p: (0, 0, p, 0, 0))
    return pl.pallas_call(
        functools.partial(_att_kernel, seq=seq),
        grid=(bsz, n_pairs),
        in_specs=[qkv_spec, qkv_spec, qkv_spec, tab_spec],
        out_specs=pl.BlockSpec((1, seq, LANES), blk),
        out_shape=jax.ShapeDtypeStruct((bsz, seq, att_w), F32),
        scratch_shapes=[pltpu.VMEM((seq, LANES), F32), pltpu.VMEM((seq, LANES), F32)],
        compiler_params=_cparams(("parallel", "parallel")),
        name="dilated_attention",
    )(q, k, v, tables)


def _ssm_params(a_re, a_im, b_re, b_im, c_re, c_im, d_skip, log_dt):
    n_g, n_p = a_re.shape
    n_c = b_re.shape[-1]
    n_state = n_g * n_p
    dt = jnp.exp(log_dt.astype(F32))[:, None]

    def a_pow(kk):
        mag = jnp.exp(kk * dt * a_re)
        ph = kk * dt * a_im
        return mag * jnp.cos(ph), mag * jnp.sin(ph)

    ab_re, ab_im = a_pow(1.0)
    nr, ni = ab_re - 1.0, ab_im
    den = a_re * a_re + a_im * a_im
    f_re = (nr * a_re + ni * a_im) / den
    f_im = (ni * a_re - nr * a_im) / den
    bb_re = f_re[:, :, None] * b_re - f_im[:, :, None] * b_im
    bb_im = f_re[:, :, None] * b_im + f_im[:, :, None] * b_re
    eye = jnp.eye(n_g, dtype=F32)
    w_re = jnp.einsum('gpc,gh->gchp', bb_re, eye).reshape(n_g * n_c, n_state)
    w_im = jnp.einsum('gpc,gh->gchp', bb_im, eye).reshape(n_g * n_c, n_state)
    o_re = jnp.einsum('gcp,gh->gphc', c_re, eye).reshape(n_state, n_g * n_c)
    o_im = jnp.einsum('gcp,gh->gphc', c_im, eye).reshape(n_state, n_g * n_c)
    n_kb = (n_g * n_c) // LANES
    sp = n_state // n_kb
    w_in = jnp.stack([jnp.concatenate([w_re[kb * LANES:(kb + 1) * LANES, kb * sp:(kb + 1) * sp],
                                       w_im[kb * LANES:(kb + 1) * LANES, kb * sp:(kb + 1) * sp]], axis=1)
                      for kb in range(n_kb)])
    w_out = jnp.stack([jnp.concatenate([o_re[kb * sp:(kb + 1) * sp, kb * LANES:(kb + 1) * LANES],
                                        -o_im[kb * sp:(kb + 1) * sp, kb * LANES:(kb + 1) * LANES]], axis=0)
                       for kb in range(n_kb)])
    j = jnp.arange(SUBLANES, dtype=F32)[:, None]
    flat = lambda t: jnp.broadcast_to(t.reshape(1, n_state), (SUBLANES, n_state))
    coef = []
    for sh in (1, 2, 4):
        pr, pim = a_pow(float(sh))
        keep = (j >= sh).astype(F32)
        coef += [flat(pr) * keep, flat(pim) * keep]
    a_re_f = jnp.broadcast_to(a_re.reshape(1, n_state), (SUBLANES, n_state))
    a_im_f = jnp.broadcast_to(a_im.reshape(1, n_state), (SUBLANES, n_state))
    dt_f = jnp.broadcast_to(jnp.repeat(dt[:, 0], n_p).reshape(1, n_state), (SUBLANES, n_state))
    mag = jnp.exp((j + 1.0) * dt_f * a_re_f)
    ph = (j + 1.0) * dt_f * a_im_f
    coef += [mag * jnp.cos(ph), mag * jnp.sin(ph)]
    coef = jnp.stack(coef)
    return w_in.astype(BF16), w_out.astype(BF16), coef, d_skip.reshape(1, n_g * n_c).astype(F32)


def _gelu_tanh(x):
    return 0.5 * x * (1.0 + jnp.tanh(math.sqrt(2.0 / math.pi) * (x + 0.044715 * (x * x * x))))


def _ssm_kernel(u_ref, win_ref, wout_ref, coef_ref, d_ref, wglu_ref, bglu_ref, o_ref,
                hre_ref, him_ref, cre_ref, cim_ref, *, rows, n_state, col_w):
    n_kb = win_ref.shape[0]
    sp = n_state // n_kb

    @pl.when(pl.program_id(1) == 0)
    def _():
        cre_ref[...] = jnp.zeros_like(cre_ref)
        cim_ref[...] = jnp.zeros_like(cim_ref)

    u = u_ref[0]
    ub = u.astype(BF16)
    for kb in range(n_kb):
        bu = _dot(ub[:, kb * LANES:(kb + 1) * LANES], win_ref[kb])
        hre_ref[:, kb * sp:(kb + 1) * sp] = bu[:, :sp]
        him_ref[:, kb * sp:(kb + 1) * sp] = bu[:, sp:]

    n_slab = rows // SUBLANES
    for c0 in range(0, n_state, col_w):
        cs = slice(c0, c0 + col_w)

        def body(i, carry, cs=cs):
            c_re, c_im = carry
            sl = pl.ds(pl.multiple_of(i * SUBLANES, SUBLANES), SUBLANES)
            x_re = hre_ref[sl, cs]
            x_im = him_ref[sl, cs]
            for si, sh in enumerate((1, 2, 4)):
                p_re = coef_ref[2 * si, :, cs]
                p_im = coef_ref[2 * si + 1, :, cs]
                s_re = pltpu.roll(x_re, sh, axis=0)
                s_im = pltpu.roll(x_im, sh, axis=0)
                x_re, x_im = (x_re + p_re * s_re - p_im * s_im,
                              x_im + p_re * s_im + p_im * s_re)
            p_re = coef_ref[6, :, cs]
            p_im = coef_ref[7, :, cs]
            h_re = x_re + p_re * c_re - p_im * c_im
            h_im = x_im + p_re * c_im + p_im * c_re
            hre_ref[sl, cs] = h_re
            him_ref[sl, cs] = h_im
            last = slice(SUBLANES - 1, SUBLANES)
            return (jnp.broadcast_to(h_re[last, :], h_re.shape),
                    jnp.broadcast_to(h_im[last, :], h_im.shape))

        c_re, c_im = lax.fori_loop(0, n_slab, body, (cre_ref[:, cs], cim_ref[:, cs]))
        cre_ref[:, cs] = c_re
        cim_ref[:, cs] = c_im

    ys = []
    for kb in range(n_kb):
        hcat = jnp.concatenate([hre_ref[:, kb * sp:(kb + 1) * sp].astype(BF16),
                                him_ref[:, kb * sp:(kb + 1) * sp].astype(BF16)], axis=1)
        ys.append(_dot(hcat, wout_ref[kb]))
    y = jnp.concatenate(ys, axis=1) + d_ref[...] * u
    y = _gelu_tanh(y)
    z = _dot(y.astype(BF16), wglu_ref[...]) + bglu_ref[...]
    o_ref[0] = y * jax.nn.sigmoid(z)


def _ssm(u, w_in, w_out, coef, d_flat, w_glu, b_glu):
    bsz, seq, ssm_w = u.shape
    n_state = coef.shape[-1]
    rows = min(SSM_ROWS, seq)
    full = lambda *shape: pl.BlockSpec(shape, lambda b, i: (0,) * len(shape))
    kern = functools.partial(_ssm_kernel, rows=rows, n_state=n_state, col_w=4 * LANES)
    return pl.pallas_call(
        kern,
        grid=(bsz, seq // rows),
        in_specs=[pl.BlockSpec((1, rows, ssm_w), lambda b, i: (b, i, 0)),
                  full(*w_in.shape), full(*w_out.shape), full(*coef.shape), full(1, ssm_w),
                  full(ssm_w, ssm_w), full(1, ssm_w)],
        out_specs=pl.BlockSpec((1, rows, ssm_w), lambda b, i: (b, i, 0)),
        out_shape=jax.ShapeDtypeStruct((bsz, seq, ssm_w), F32),
        scratch_shapes=[pltpu.VMEM((rows, n_state), F32), pltpu.VMEM((rows, n_state), F32),
                        pltpu.VMEM((SUBLANES, n_state), F32), pltpu.VMEM((SUBLANES, n_state), F32)],
        compiler_params=_cparams(("parallel", "arbitrary")),
        name="s5_glu",
    )(u, w_in, w_out, coef, d_flat, w_glu.astype(BF16), b_glu.reshape(1, ssm_w))


def _layer_norm(y, g, b):
    mu = jnp.mean(y, axis=-1, keepdims=True)
    yc = y - mu
    var = jnp.mean(yc * yc, axis=-1, keepdims=True)
    return yc * lax.rsqrt(var + EPS) * g + b


def _rms_norm(y, g):
    return y * lax.rsqrt(jnp.mean(y * y, axis=-1, keepdims=True) + EPS) * g


def _mix_kernel(x_ref, att_ref, ssm_ref, gatt_ref, gssm_ref, wout_ref, g1_ref, ln_g_ref, ln_b_ref,
                sc_ref, sh_ref, g2_ref, wsg_ref, wsu_ref, wsd_ref, h_ref, ht_ref, pre_ref, *, alpha, att_w):
    a_n = _rms_norm(att_ref[0], gatt_ref[...]).astype(BF16)
    s_n = _rms_norm(ssm_ref[0], gssm_ref[...]).astype(BF16)
    mix = _dot(a_n, wout_ref[0:att_w, :]) + _dot(s_n, wout_ref[att_w:, :])
    x1 = _layer_norm(alpha * x_ref[0] + g1_ref[0] * mix, ln_g_ref[...], ln_b_ref[...])
    h = x1 * (1.0 + sc_ref[0]) + sh_ref[0]
    h_ref[0] = h
    _store_token_tiles(ht_ref, h)
    hb = h.astype(BF16)
    hid = _silu(_dot(hb, wsg_ref[...])) * _dot(hb, wsu_ref[...])
    shared = _dot(hid.astype(BF16), wsd_ref[...])
    pre_ref[0] = alpha * x1 + g2_ref[0] * shared


def _mix(x, att, ssm, g_att, g_ssm, w_out, gate1, ln_g, ln_b, scale2, shift2, gate2,
         w_s_gate, w_s_up, w_s_down, alpha):
    bsz, seq, d = x.shape
    att_w = att.shape[-1]
    ssm_w = ssm.shape[-1]
    ff = w_s_gate.shape[1]
    tm = min(MIX_ROWS, seq)
    n_sub = d // LANES
    row = lambda b, i: (b, i, 0)
    per_b = lambda b, i: (b, 0, 0)
    full = lambda *shape: pl.BlockSpec(shape, lambda b, i: (0,) * len(shape))
    vec = lambda t: t.reshape(bsz, 1, d)
    return pl.pallas_call(
        functools.partial(_mix_kernel, alpha=alpha, att_w=att_w),
        grid=(bsz, seq // tm),
        in_specs=[pl.BlockSpec((1, tm, d), row), pl.BlockSpec((1, tm, att_w), row),
                  pl.BlockSpec((1, tm, ssm_w), row),
                  full(1, att_w), full(1, ssm_w), full(att_w + ssm_w, d),
                  pl.BlockSpec((1, 1, d), per_b), full(1, d), full(1, d),
                  pl.BlockSpec((1, 1, d), per_b), pl.BlockSpec((1, 1, d), per_b),
                  pl.BlockSpec((1, 1, d), per_b),
                  full(d, ff), full(d, ff), full(ff, d)],
        out_specs=[pl.BlockSpec((1, tm, d), row),
                   pl.BlockSpec((tm * n_sub, LANES), lambda b, i: (b * (seq // tm) + i, 0)),
                   pl.BlockSpec((1, tm, d), row)],
        out_shape=[jax.ShapeDtypeStruct((bsz, seq, d), F32),
                   jax.ShapeDtypeStruct((bsz * seq * n_sub, LANES), F32),
                   jax.ShapeDtypeStruct((bsz, seq, d), F32)],
        compiler_params=_cparams(("parallel", "parallel")),
        name="mix_ln1_shared",
    )(x, att, ssm, g_att.reshape(1, att_w), g_ssm.reshape(1, ssm_w), w_out.astype(BF16),
      vec(gate1), ln_g.reshape(1, d), ln_b.reshape(1, d), vec(scale2), vec(shift2), vec(gate2),
      w_s_gate.astype(BF16), w_s_up.astype(BF16), w_s_down.astype(BF16))


def _route_kernel(h_ref, wrt_ref, wrt_lo_ref, bias_ref, tri_ref, idx_ref, gate_ref, rank_ref, cnt_ref, carry_ref):
    n_e = wrt_ref.shape[0]
    tm = h_ref.shape[0]
    per_group = n_e // N_EXPERT_GROUPS
    neg = -jnp.inf

    @pl.when(pl.program_id(0) == 0)
    def _():
        carry_ref[...] = jnp.zeros_like(carry_ref)

    h_hi, h_lo = _split_bf16(h_ref[...])
    logits = (_dot_nt(wrt_ref[...], h_hi) + _dot_nt(wrt_ref[...], h_lo)
              + _dot_nt(wrt_lo_ref[...], h_hi))
    scores = jax.nn.sigmoid(logits)
    biased = scores + bias_ref[...]

    g3 = biased.reshape(N_EXPERT_GROUPS, per_group, tm)
    ridx = lax.broadcasted_iota(jnp.int32, g3.shape, 1).astype(F32)
    m1 = jnp.max(g3, axis=1, keepdims=True)
    first = jnp.min(jnp.where(g3 == m1, ridx, float(per_group)), axis=1, keepdims=True)
    m2 = jnp.max(jnp.where(ridx == first, neg, g3), axis=1, keepdims=True)
    gs = m1 + m2

    gidx = lax.broadcasted_iota(jnp.int32, gs.shape, 0).astype(F32)
    ok = jnp.zeros_like(gs)
    cur = gs
    for _ in range(TOPK_GROUPS):
        mx = jnp.max(cur, axis=0, keepdims=True)
        fi = jnp.min(jnp.where(cur == mx, gidx, float(N_EXPERT_GROUPS)), axis=0, keepdims=True)
        hit = gidx == fi
        ok = jnp.where(hit, 1.0, ok)
        cur = jnp.where(hit, neg, cur)
    masked = jnp.where(ok > 0.5, g3, neg).reshape(n_e, tm)

    eidx = lax.broadcasted_iota(jnp.int32, (n_e, tm), 0).astype(F32)
    onehot = jnp.zeros((n_e, tm), F32)
    cur = masked
    sel_idx = []
    sel_gate = []
    for _ in range(TOP_K):
        mx = jnp.max(cur, axis=0, keepdims=True)
        fi = jnp.min(jnp.where(cur == mx, eidx, float(n_e)), axis=0, keepdims=True)
        hit = eidx == fi
        sel_idx.append(fi)
        sel_gate.append(jnp.sum(jnp.where(hit, scores, 0.0), axis=0, keepdims=True))
        onehot = jnp.where(hit, 1.0, onehot)
        cur = jnp.where(hit, neg, cur)
    idx = jnp.concatenate(sel_idx, axis=0)
    gate = jnp.concatenate(sel_gate, axis=0)
    gate = gate / jnp.sum(gate, axis=0, keepdims=True) * ROUTED_SCALE

    prior = _dot(onehot.astype(BF16), tri_ref[...]) + carry_ref[:, 0:1]
    ranks = [jnp.sum(jnp.where(eidx == sel_idx[k], prior, 0.0), axis=0, keepdims=True)
             for k in range(TOP_K)]
    rank = jnp.concatenate(ranks, axis=0)
    carry = carry_ref[...] + jnp.sum(onehot, axis=1, keepdims=True)
    carry_ref[...] = carry
    cnt_ref[...] = carry

    idx_ref[...] = idx.astype(jnp.int32)
    gate_ref[...] = gate
    rank_ref[...] = rank.astype(jnp.int32)


def _route(h2d, w_router, router_bias):
    n_tok, d = h2d.shape
    n_e = w_router.shape[1]
    tm = min(ROUTE_COLS, n_tok)
    tri = (np.arange(tm)[:, None] < np.arange(tm)[None, :]).astype(np.float32)
    w_hi, w_lo = _split_bf16(w_router.T.astype(F32))
    col = lambda i: (0, i)
    full = lambda *shape: pl.BlockSpec(shape, lambda i: (0,) * len(shape))
    idx, gate, rank, cnt = pl.pallas_call(
        _route_kernel,
        grid=(n_tok // tm,),
        in_specs=[pl.BlockSpec((tm, d), lambda i: (i, 0)), full(n_e, d), full(n_e, d), full(n_e, 1),
                  full(tm, tm)],
        out_specs=[pl.BlockSpec((TOP_K, tm), col), pl.BlockSpec((TOP_K, tm), col),
                   pl.BlockSpec((TOP_K, tm), col), full(n_e, LANES)],
        out_shape=[jax.ShapeDtypeStruct((TOP_K, n_tok), jnp.int32),
                   jax.ShapeDtypeStruct((TOP_K, n_tok), F32),
                   jax.ShapeDtypeStruct((TOP_K, n_tok), jnp.int32),
                   jax.ShapeDtypeStruct((n_e, LANES), F32)],
        scratch_shapes=[pltpu.VMEM((n_e, LANES), F32)],
        compiler_params=_cparams(("arbitrary",)),
        name="router_topk",
    )(h2d, w_hi, w_lo, router_bias.reshape(n_e, 1).astype(F32), jnp.asarray(tri, BF16))
    return idx, gate, rank, cnt[:, 0].astype(jnp.int32)


def _dispatch_kernel(pend_ref, cnt_ref, dest_ref, ht_ref, xs_ref, zero_ref, sem_ref, *, n_e, blk, tm):
    base = pl.program_id(0) * tm

    def zero_copy(e):
        start = pl.multiple_of(pend_ref[e] - blk, blk)
        return pltpu.make_async_copy(zero_ref, xs_ref.at[pl.ds(start, blk)], sem_ref.at[0])

    @pl.when(pl.program_id(0) == 0)
    def _():
        zero_ref[...] = jnp.zeros_like(zero_ref)

        def start(e, c):
            @pl.when(cnt_ref[e] > 0)
            def _():
                zero_copy(e).start()
            return c
        lax.fori_loop(0, n_e, start, 0)

        def wait(e, c):
            @pl.when(cnt_ref[e] > 0)
            def _():
                zero_copy(e).wait()
            return c
        lax.fori_loop(0, n_e, wait, 0)

    def row_copy(t, dst_row):
        return pltpu.make_async_copy(ht_ref.at[base + t], xs_ref.at[dst_row], sem_ref.at[1])

    def issue(g, c):
        t0 = g * DMA_ISSUE_UNROLL
        rows = [[dest_ref[(t0 + j) * TOP_K + k] for k in range(TOP_K)] for j in range(DMA_ISSUE_UNROLL)]
        for j in range(DMA_ISSUE_UNROLL):
            for k in range(TOP_K):
                row_copy(t0 + j, rows[j][k]).start(priority=k % 2)
        return c
    lax.fori_loop(0, tm // DMA_ISSUE_UNROLL, issue, 0)

    def drain(t, c):
        for k in range(TOP_K):
            row_copy(t, dest_ref[t * TOP_K + k]).wait()
        return c
    lax.fori_loop(0, tm, drain, 0)


def _dispatch(h_tiles, dest, pend, cnt, n_rows):
    n_tok, n_sub, _ = h_tiles.shape
    n_e = cnt.shape[0]
    tm = min(DISPATCH_ROWS, n_tok)
    grid_spec = pltpu.PrefetchScalarGridSpec(
        num_scalar_prefetch=2,
        grid=(n_tok // tm,),
        in_specs=[pl.BlockSpec((TOP_K * tm,), lambda i, *_: (i,), memory_space=pltpu.SMEM),
                  pl.BlockSpec(memory_space=pl.ANY)],
        out_specs=pl.BlockSpec(memory_space=pl.ANY),
        scratch_shapes=[pltpu.VMEM((MOE_ROWS, n_sub, LANES), F32), pltpu.SemaphoreType.DMA((2,))],
    )
    return pl.pallas_call(
        functools.partial(_dispatch_kernel, n_e=n_e, blk=MOE_ROWS, tm=tm),
        grid_spec=grid_spec,
        out_shape=jax.ShapeDtypeStruct((n_rows, n_sub, LANES), F32),
        compiler_params=_cparams(("arbitrary",)),
        name="moe_dispatch",
    )(pend, cnt, dest, h_tiles)


def _expert_kernel(be_ref, nused_ref, x_ref, wg_ref, wu_ref, wd_ref, y_ref, wgb_ref, wub_ref, wdb_ref):
    i = pl.program_id(0)
    prev = be_ref[jnp.maximum(i - 1, 0)]
    fresh = jnp.logical_or(i == 0, be_ref[i] != prev)

    @pl.when(jnp.logical_and(i < nused_ref[0], fresh))
    def _():
        wgb_ref[...] = wg_ref[0].astype(BF16)
        wub_ref[...] = wu_ref[0].astype(BF16)
        wdb_ref[...] = wd_ref[0].astype(BF16)

    @pl.when(i < nused_ref[0])
    def _():
        d = wgb_ref.shape[0]
        xb = jnp.concatenate(_load_token_tiles(x_ref, MOE_ROWS, d), axis=1).astype(BF16)
        hid = _silu(_dot(xb, wgb_ref[...])) * _dot(xb, wub_ref[...])
        _store_token_tiles(y_ref, _dot(hid.astype(BF16), wdb_ref[...]))


def _experts(xs_tiles, block_expert, n_used, w_gate, w_up, w_down):
    n_rows, n_sub, _ = xs_tiles.shape
    d = n_sub * LANES
    n_blk = n_rows // MOE_ROWS
    ff = w_gate.shape[-1]
    clamp = lambda i, be, nu: jnp.minimum(i, nu[0] - 1)
    rows_spec = pl.BlockSpec((MOE_ROWS * n_sub, LANES), lambda i, be, nu: (clamp(i, be, nu), 0))
    grid_spec = pltpu.PrefetchScalarGridSpec(
        num_scalar_prefetch=2,
        grid=(n_blk,),
        in_specs=[rows_spec,
                  pl.BlockSpec((1, d, ff), lambda i, be, nu: (be[clamp(i, be, nu)], 0, 0)),
                  pl.BlockSpec((1, d, ff), lambda i, be, nu: (be[clamp(i, be, nu)], 0, 0)),
                  pl.BlockSpec((1, ff, d), lambda i, be, nu: (be[clamp(i, be, nu)], 0, 0))],
        out_specs=rows_spec,
        scratch_shapes=[pltpu.VMEM((d, ff), BF16), pltpu.VMEM((d, ff), BF16), pltpu.VMEM((ff, d), BF16)],
    )
    ys = pl.pallas_call(
        _expert_kernel,
        grid_spec=grid_spec,
        out_shape=jax.ShapeDtypeStruct((n_rows * n_sub, LANES), F32),
        compiler_params=_cparams(("arbitrary",)),
        name="moe_experts",
    )(block_expert, n_used, xs_tiles.reshape(n_rows * n_sub, LANES), w_gate, w_up, w_down)
    return ys.reshape(n_rows, n_sub, LANES)


def _combine_kernel(dest_ref, ys_ref, gate_ref, pre_ref, g2_ref, ln_g_ref, ln_b_ref, o_ref, buf_ref, sem_ref):
    tm, d = pre_ref.shape[1], pre_ref.shape[2]
    n_sub = d // LANES

    def row_copy(t, k, src_row):
        slot = pl.multiple_of((k * tm + t) * n_sub, n_sub)
        return pltpu.make_async_copy(ys_ref.at[src_row], buf_ref.at[pl.ds(slot, n_sub)], sem_ref.at[0])

    def issue(g, c):
        t0 = g * DMA_ISSUE_UNROLL
        rows = [[dest_ref[(t0 + j) * TOP_K + k] for k in range(TOP_K)] for j in range(DMA_ISSUE_UNROLL)]
        for j in range(DMA_ISSUE_UNROLL):
            for k in range(TOP_K):
                row_copy(t0 + j, k, rows[j][k]).start(priority=k % 2)
        return c
    lax.fori_loop(0, tm // DMA_ISSUE_UNROLL, issue, 0)

    def drain(t, c):
        for k in range(TOP_K):
            row_copy(t, k, dest_ref[t * TOP_K + k]).wait()
        return c
    lax.fori_loop(0, tm, drain, 0)

    gates = gate_ref[...]
    gate_cols = [jnp.broadcast_to(gates[:, k:k + 1], (tm, LANES)) for k in range(TOP_K)]
    pieces = []
    for s in range(n_sub):
        acc = None
        for k in range(TOP_K):
            term = buf_ref[pl.ds(k * tm * n_sub + s, tm, stride=n_sub), :] * gate_cols[k]
            acc = term if acc is None else acc + term
        pieces.append(acc)
    routed = jnp.concatenate(pieces, axis=1)
    y = pre_ref[0] + g2_ref[0] * routed
    o_ref[0] = _layer_norm(y, ln_g_ref[...], ln_b_ref[...])


def _combine(ys, dest, gates_t, pre, gate2, ln_g, ln_b):
    bsz, seq, d = pre.shape
    tm = min(DISPATCH_ROWS, seq)
    per_seq = seq // tm
    row = lambda i, *_: (i // per_seq, i % per_seq, 0)
    per_b = lambda i, *_: (i // per_seq, 0, 0)
    full = lambda *shape: pl.BlockSpec(shape, lambda i, *_: (0,) * len(shape))
    grid_spec = pltpu.PrefetchScalarGridSpec(
        num_scalar_prefetch=0,
        grid=(bsz * per_seq,),
        in_specs=[pl.BlockSpec((TOP_K * tm,), lambda i: (i,), memory_space=pltpu.SMEM),
                  pl.BlockSpec(memory_space=pl.ANY),
                  pl.BlockSpec((tm, TOP_K), lambda i: (i, 0)),
                  pl.BlockSpec((1, tm, d), row), pl.BlockSpec((1, 1, d), per_b), full(1, d), full(1, d)],
        out_specs=pl.BlockSpec((1, tm, d), row),
        scratch_shapes=[pltpu.VMEM((TOP_K * tm * (d // LANES), LANES), F32), pltpu.SemaphoreType.DMA((1,))],
    )
    return pl.pallas_call(
        _combine_kernel,
        grid_spec=grid_spec,
        out_shape=jax.ShapeDtypeStruct((bsz, seq, d), F32),
        compiler_params=_cparams(("arbitrary",)),
        name="moe_combine_ln2",
    )(dest, ys, gates_t, pre, gate2.reshape(bsz, 1, d), ln_g.reshape(1, d), ln_b.reshape(1, d))


def _dest_kernel(pstart_ref, idx_ref, rank_ref, dest_ref):
    idx = idx_ref[...]

    def body(e, acc):
        return acc + jnp.where(idx == e, pstart_ref[e], 0)
    dest_ref[...] = lax.fori_loop(0, pstart_ref.shape[0], body, rank_ref[...], unroll=8)


def _dest_rows(pstart, idx, rank):
    n_k, n_tok = idx.shape
    tn = min(4096, n_tok)
    grid_spec = pltpu.PrefetchScalarGridSpec(
        num_scalar_prefetch=1,
        grid=(n_tok // tn,),
        in_specs=[pl.BlockSpec((n_k, tn), lambda i, *_: (0, i))] * 2,
        out_specs=pl.BlockSpec((n_k, tn), lambda i, *_: (0, i)),
    )
    return pl.pallas_call(
        _dest_kernel,
        grid_spec=grid_spec,
        out_shape=jax.ShapeDtypeStruct((n_k, n_tok), jnp.int32),
        compiler_params=_cparams(("arbitrary",)),
        name="moe_dest",
    )(pstart, idx, rank)


def _moe_plan(idx, rank, cnt, n_tok):
    n_e = cnt.shape[0]
    padded = (cnt + MOE_ROWS - 1) // MOE_ROWS * MOE_ROWS
    pend = jnp.cumsum(padded).astype(jnp.int32)
    pstart = pend - padded
    dest = _dest_rows(pstart, idx, rank).T.reshape(-1)
    n_blk = n_tok * TOP_K // MOE_ROWS + n_e
    starts = jnp.arange(n_blk, dtype=jnp.int32) * MOE_ROWS
    block_expert = jnp.minimum(jnp.sum((pend[None, :] <= starts[:, None]).astype(jnp.int32), axis=1),
                               n_e - 1).astype(jnp.int32)
    n_used = (pend[-1:] // MOE_ROWS).astype(jnp.int32)
    return dest, pend, block_expert, n_used, n_blk * MOE_ROWS


def kernel(x, c, rel_bias, w_ada, b_ada, w_in, ssm_a_re, ssm_a_im, ssm_b_re, ssm_b_im, ssm_c_re, ssm_c_im, ssm_d, ssm_log_dt, w_glu, b_glu, g_att, g_ssm, w_out, ln1_g, ln1_b, w_router, router_bias, w_e_gate, w_e_up, w_e_down, w_s_gate, w_s_up, w_s_down, ln2_g, ln2_b):
    bsz, seq, d = x.shape
    depth = w_ada.shape[0]
    alpha = (2 * depth) ** 0.25
    att_w = g_att.shape[-1]
    tables = _att_tables(rel_bias)
    for layer in range(depth):
        ada = _ada(c, w_ada[layer], b_ada[layer])
        shift1, scale1, gate1, shift2, scale2, gate2 = jnp.split(ada, 6, axis=-1)
        q, k, v, u = _inproj(x, scale1, shift1, w_in[layer], att_w)
        att = _attention(q, k, v, tables)
        ssm_prm = _ssm_params(ssm_a_re[layer], ssm_a_im[layer], ssm_b_re[layer], ssm_b_im[layer],
                              ssm_c_re[layer], ssm_c_im[layer], ssm_d[layer], ssm_log_dt[layer])
        ssm = _ssm(u, *ssm_prm, w_glu[layer], b_glu[layer])
        h2, h2_tiles, pre = _mix(x, att, ssm, g_att[layer], g_ssm[layer], w_out[layer], gate1,
                                 ln1_g[layer], ln1_b[layer], scale2, shift2, gate2, w_s_gate[layer],
                                 w_s_up[layer], w_s_down[layer], alpha)
        h2d = h2.reshape(bsz * seq, d)
        idx, gate, rank, cnt = _route(h2d, w_router[layer], router_bias[layer])
        dest, pend, block_expert, n_used, n_rows = _moe_plan(idx, rank, cnt, bsz * seq)
        xs = _dispatch(h2_tiles.reshape(bsz * seq, d // LANES, LANES), dest, pend, cnt, n_rows)
        ys = _experts(xs, block_expert, n_used, w_e_gate[layer], w_e_up[layer], w_e_down[layer])
        x = _combine(ys, dest, gate.T, pre, gate2, ln2_g[layer], ln2_b[layer])
    return x
```

```python
import functools
import math

import jax
import jax.numpy as jnp
import numpy as np
from jax import lax
from jax.experimental import pallas as pl
from jax.experimental.pallas import tpu as pltpu

F32 = jnp.float32
BF16 = jnp.bfloat16

HEAD_DIM = 64
ATT_BLOCK = 128
PATTERNS = ((128, 1), (512, 4), (2048, 16))
N_BUCKETS = 32
MAX_DISTANCE = 2048
SSM_GROUP = 16
SSM_STATE = 64
N_EXPERTS = 256
TOP_K = 8
N_EXPERT_GROUPS = 8
TOPK_GROUPS = 4
ROUTED_SCALE = 2.5
EPS = 1e-5
NEG_INF = -1e30

LANES = 128
SUBLANES = 8
VMEM_LIMIT_BYTES = 56 * 1024 * 1024

PROJ_ROWS = 512
SSM_ROWS = 256
MIX_ROWS = 512
ROUTE_COLS = 512
MOE_ROWS = 256
DISPATCH_ROWS = 256
ATT_UNITS_PER_STEP = 4
DMA_ISSUE_UNROLL = 4


def _cparams(sem, vmem=VMEM_LIMIT_BYTES):
    return pltpu.CompilerParams(dimension_semantics=sem, vmem_limit_bytes=vmem)


def _dot(a, b):
    return jnp.dot(a, b, preferred_element_type=F32)


def _dot_nt(a, b):
    return lax.dot_general(a, b, (((1,), (1,)), ((), ())), preferred_element_type=F32)


def _silu(x):
    return x * jax.nn.sigmoid(x)


def _store_token_tiles(ref, x):
    m, d = x.shape
    n_sub = d // LANES
    for s in range(n_sub):
        ref[pl.ds(s, m, stride=n_sub), :] = x[:, s * LANES:(s + 1) * LANES]


def _load_token_tiles(ref, m, d, row0=0):
    n_sub = d // LANES
    return [ref[pl.ds(row0 + s, m, stride=n_sub), :] for s in range(n_sub)]


def _split_bf16(x):
    hi = x.astype(BF16)
    lo = (x - hi.astype(F32)).astype(BF16)
    return hi, lo


def _ada_kernel(c_ref, w_ref, b_ref, o_ref):
    c = c_ref[...]
    a_hi, a_lo = _split_bf16(_silu(c))
    w_hi, w_lo = _split_bf16(w_ref[...])
    acc = _dot(a_hi, w_hi) + _dot(a_hi, w_lo) + _dot(a_lo, w_hi)
    o_ref[...] = acc + b_ref[...]


def _ada(c, w_ada, b_ada):
    bsz, d = c.shape
    n = w_ada.shape[1]
    rows = SUBLANES
    c_pad = jnp.zeros((rows, d), F32).at[:bsz].set(c)
    tn = 1024
    out = pl.pallas_call(
        _ada_kernel,
        grid=(n // tn,),
        in_specs=[pl.BlockSpec((rows, d), lambda j: (0, 0)),
                  pl.BlockSpec((d, tn), lambda j: (0, j)),
                  pl.BlockSpec((1, tn), lambda j: (0, j))],
        out_specs=pl.BlockSpec((rows, tn), lambda j: (0, j)),
        out_shape=jax.ShapeDtypeStruct((rows, n), F32),
        compiler_params=_cparams(("parallel",)),
        name="ada",
    )(c_pad, w_ada, b_ada.reshape(1, n))
    return out[:bsz]


def _inproj_kernel(x_ref, sc_ref, sh_ref, w_ref, q_ref, k_ref, v_ref, u_ref, *, att_w, q_scale):
    h = (x_ref[0] * (1.0 + sc_ref[0]) + sh_ref[0]).astype(BF16)
    q_ref[0] = _dot(h, w_ref[:, 0:att_w]) * q_scale
    k_ref[0] = _dot(h, w_ref[:, att_w:2 * att_w])
    v_ref[0] = _dot(h, w_ref[:, 2 * att_w:3 * att_w])
    u_ref[0] = _dot(h, w_ref[:, 3 * att_w:])


def _inproj(x, scale, shift, w_in, att_w):
    bsz, seq, d = x.shape
    n = w_in.shape[1]
    ssm_w = n - 3 * att_w
    ts = min(PROJ_ROWS, seq)
    row = lambda b, i: (b, i, 0)
    per_b = lambda b, i: (b, 0, 0)
    kern = functools.partial(_inproj_kernel, att_w=att_w, q_scale=HEAD_DIM ** -0.5)
    return pl.pallas_call(
        kern,
        grid=(bsz, seq // ts),
        in_specs=[pl.BlockSpec((1, ts, d), row),
                  pl.BlockSpec((1, 1, d), per_b),
                  pl.BlockSpec((1, 1, d), per_b),
                  pl.BlockSpec((d, n), lambda b, i: (0, 0))],
        out_specs=[pl.BlockSpec((1, ts, att_w), row)] * 3 + [pl.BlockSpec((1, ts, ssm_w), row)],
        out_shape=[jax.ShapeDtypeStruct((bsz, seq, att_w), F32)] * 3
                  + [jax.ShapeDtypeStruct((bsz, seq, ssm_w), F32)],
        compiler_params=_cparams(("parallel", "parallel")),
        name="inproj",
    )(x, scale.reshape(bsz, 1, d), shift.reshape(bsz, 1, d), w_in.astype(BF16))


def _t5_bucket_np(dist):
    exact = N_BUCKETS // 2
    large = exact + (np.log(np.maximum(dist, 1).astype(np.float64) / exact)
                     / math.log(MAX_DISTANCE / exact) * (N_BUCKETS - exact)).astype(np.int64)
    return np.where(dist < exact, dist, np.minimum(large, N_BUCKETS - 1))


def _att_tables(rel_bias):
    qi = np.arange(ATT_BLOCK)[:, None]
    ki = np.arange(2 * ATT_BLOCK)[None, :]
    rel = qi + ATT_BLOCK - ki
    tabs = []
    for window, dil in PATTERNS:
        band = (rel >= 0) & (rel <= window // dil)
        bucket = _t5_bucket_np(np.maximum(rel, 0) * dil)
        onehot = (bucket[:, :, None] == np.arange(N_BUCKETS)[None, None, :]).astype(np.float32)
        bias = jnp.einsum('qkb,bh->hqk', onehot, rel_bias.astype(F32),
                          precision=lax.Precision.HIGHEST)
        full = jnp.where(band[None], bias, NEG_INF)
        first = jnp.concatenate([full[:, :, ATT_BLOCK:], jnp.full_like(full[:, :, ATT_BLOCK:], NEG_INF)], axis=-1)
        tabs.append(jnp.stack([full, first]))
    return jnp.stack(tabs)


def _att_kernel(q_ref, k_ref, v_ref, tab_ref, o_ref, m_ref, l_ref, *, seq):
    lane = lax.broadcasted_iota(jnp.int32, (ATT_BLOCK, LANES), 1)
    head0 = lane < HEAD_DIM

    def rows(ref, start, n, dil):
        if dil == 1:
            return ref[0, pl.ds(pl.multiple_of(start, ATT_BLOCK), n), :]
        return ref[0, pl.ds(start, n, stride=dil), :]

    nb = ATT_UNITS_PER_STEP
    q_rows = nb * ATT_BLOCK

    def lane_mask(n_rows, h):
        m = lax.broadcasted_iota(jnp.int32, (n_rows, LANES), 1) < HEAD_DIM
        return m if h == 0 else ~m

    def batch(pi, dil, res, blk0, head, init):
        kv_blocks = nb if head else nb + 1
        q_start = res + dil * ATT_BLOCK * blk0
        kv_start = q_start if head else q_start - dil * ATT_BLOCK
        if dil == 1:
            sl = pl.ds(pl.multiple_of(q_start, ATT_BLOCK), q_rows)
        else:
            sl = pl.ds(q_start, q_rows, stride=dil)
        q = rows(q_ref, q_start, q_rows, dil)
        k = rows(k_ref, kv_start, kv_blocks * ATT_BLOCK, dil).astype(BF16)
        v = rows(v_ref, kv_start, kv_blocks * ATT_BLOCK, dil)
        if not init:
            m_old, l_old, o_old = m_ref[sl, :], l_ref[sl, :], o_ref[0, sl, :]
        qh = [jnp.where(lane_mask(q_rows, h), q, 0.0).astype(BF16) for h in range(2)]
        vh = [jnp.where(lane_mask(kv_blocks * ATT_BLOCK, h), v, 1.0).astype(BF16) for h in range(2)]
        o_parts, l_parts, m_parts = [], [], []
        for j in range(nb):
            first = 1 if (head and j == 0) else 0
            kb = j if (not head or j == 0) else j - 1
            ksl = slice(kb * ATT_BLOCK, (kb + 2) * ATT_BLOCK)
            qsl = slice(j * ATT_BLOCK, (j + 1) * ATT_BLOCK)
            outs, ms = [], []
            for h in range(2):
                s = _dot_nt(qh[h][qsl], k[ksl]) + tab_ref[pi, first, h]
                m = jnp.max(s, axis=1, keepdims=True)
                p = jnp.exp(s - m).astype(BF16)
                outs.append(_dot(p, vh[h][ksl]))
                ms.append(m)
            o_parts.append(jnp.where(head0, outs[0], outs[1]))
            l_parts.append(pltpu.roll(jnp.where(head0, outs[1], outs[0]), HEAD_DIM, axis=1))
            m_parts.append(jnp.where(head0, ms[0], ms[1]))
        o = jnp.concatenate(o_parts, axis=0)
        l = jnp.concatenate(l_parts, axis=0)
        m = jnp.concatenate(m_parts, axis=0)
        if not init:
            m_new = jnp.maximum(m_old, m)
            a_old = jnp.exp(m_old - m_new)
            a_cur = jnp.exp(m - m_new)
            o = o_old * a_old + o * a_cur
            l = l_old * a_old + l * a_cur
            m = m_new
        o_ref[0, sl, :] = o
        m_ref[sl, :] = m
        l_ref[sl, :] = l

    n_units = seq // ATT_BLOCK
    order = sorted(range(len(PATTERNS)), key=lambda p: -PATTERNS[p][1])
    for pos, pi in enumerate(order):
        dil = PATTERNS[pi][1]
        log_d = dil.bit_length() - 1
        per_res = seq // (dil * ATT_BLOCK) // nb
        init = pos == 0

        def head_body(res, carry, pi=pi, dil=dil, init=init):
            batch(pi, dil, res, 0, True, init)
            return carry
        lax.fori_loop(0, dil, head_body, 0)

        if per_res > 1:
            def tail_body(i, carry, pi=pi, dil=dil, log_d=log_d, init=init):
                res = i & (dil - 1)
                blk0 = nb * (1 + (i >> log_d))
                batch(pi, dil, res, blk0, False, init)
                return carry
            lax.fori_loop(0, dil * (per_res - 1), tail_body, 0)

    def finish(i, carry):
        sl = pl.ds(pl.multiple_of(i * ATT_BLOCK, ATT_BLOCK), ATT_BLOCK)
        o_ref[0, sl, :] = o_ref[0, sl, :] / l_ref[sl, :]
        return carry
    lax.fori_loop(0, n_units, finish, 0)


def _attention(q, k, v, tables):
    bsz, seq, att_w = q.shape
    n_pairs = att_w // LANES
    blk = lambda b, p: (b, 0, p)
    qkv_spec = pl.BlockSpec((1, seq, LANES), blk)
    n_pat = tables.shape[0]
    tab_spec = pl.BlockSpec((n_pat, 2, 2, ATT_BLOCK, 2 * ATT_BLOCK), lambda b, p: (0, 0, p, 0, 0))
    return pl.pallas_call(
        functools.partial(_att_kernel, seq=seq),
        grid=(bsz, n_pairs),
        in_specs=[qkv_spec, qkv_spec, qkv_spec, tab_spec],
        out_specs=pl.BlockSpec((1, seq, LANES), blk),
        out_shape=jax.ShapeDtypeStruct((bsz, seq, att_w), F32),
        scratch_shapes=[pltpu.VMEM((seq, LANES), F32), pltpu.VMEM((seq, LANES), F32)],
        compiler_params=_cparams(("parallel", "parallel")),
        name="dilated_attention",
    )(q, k, v, tables)


def _ssm_params(a_re, a_im, b_re, b_im, c_re, c_im, d_skip, log_dt):
    n_g, n_p = a_re.shape
    n_c = b_re.shape[-1]
    n_state = n_g * n_p
    dt = jnp.exp(log_dt.astype(F32))[:, None]

    def a_pow(kk):
        mag = jnp.exp(kk * dt * a_re)
        ph = kk * dt * a_im
        return mag * jnp.cos(ph), mag * jnp.sin(ph)

    ab_re, ab_im = a_pow(1.0)
    nr, ni = ab_re - 1.0, ab_im
    den = a_re * a_re + a_im * a_im
    f_re = (nr * a_re + ni * a_im) / den
    f_im = (ni * a_re - nr * a_im) / den
    bb_re = f_re[:, :, None] * b_re - f_im[:, :, None] * b_im
    bb_im = f_re[:, :, None] * b_im + f_im[:, :, None] * b_re
    eye = jnp.eye(n_g, dtype=F32)
    w_re = jnp.einsum('gpc,gh->gchp', bb_re, eye).reshape(n_g * n_c, n_state)
    w_im = jnp.einsum('gpc,gh->gchp', bb_im, eye).reshape(n_g * n_c, n_state)
    o_re = jnp.einsum('gcp,gh->gphc', c_re, eye).reshape(n_state, n_g * n_c)
    o_im = jnp.einsum('gcp,gh->gphc', c_im, eye).reshape(n_state, n_g * n_c)
    n_kb = (n_g * n_c) // LANES
    sp = n_state // n_kb
    w_in = jnp.stack([jnp.concatenate([w_re[kb * LANES:(kb + 1) * LANES, kb * sp:(kb + 1) * sp],
                                       w_im[kb * LANES:(kb + 1) * LANES, kb * sp:(kb + 1) * sp]], axis=1)
                      for kb in range(n_kb)])
    w_out = jnp.stack([jnp.concatenate([o_re[kb * sp:(kb + 1) * sp, kb * LANES:(kb + 1) * LANES],
                                        -o_im[kb * sp:(kb + 1) * sp, kb * LANES:(kb + 1) * LANES]], axis=0)
                       for kb in range(n_kb)])
    j = jnp.arange(SUBLANES, dtype=F32)[:, None]
    flat = lambda t: jnp.broadcast_to(t.reshape(1, n_state), (SUBLANES, n_state))
    coef = []
    for sh in (1, 2, 4):
        pr, pim = a_pow(float(sh))
        keep = (j >= sh).astype(F32)
        coef += [flat(pr) * keep, flat(pim) * keep]
    a_re_f = jnp.broadcast_to(a_re.reshape(1, n_state), (SUBLANES, n_state))
    a_im_f = jnp.broadcast_to(a_im.reshape(1, n_state), (SUBLANES, n_state))
    dt_f = jnp.broadcast_to(jnp.repeat(dt[:, 0], n_p).reshape(1, n_state), (SUBLANES, n_state))
    mag = jnp.exp((j + 1.0) * dt_f * a_re_f)
    ph = (j + 1.0) * dt_f * a_im_f
    coef += [mag * jnp.cos(ph), mag * jnp.sin(ph)]
    coef = jnp.stack(coef)
    return w_in.astype(BF16), w_out.astype(BF16), coef, d_skip.reshape(1, n_g * n_c).astype(F32)


def _gelu_tanh(x):
    return 0.5 * x * (1.0 + jnp.tanh(math.sqrt(2.0 / math.pi) * (x + 0.044715 * (x * x * x))))


def _ssm_kernel(u_ref, win_ref, wout_ref, coef_ref, d_ref, wglu_ref, bglu_ref, o_ref,
                hre_ref, him_ref, cre_ref, cim_ref, *, rows, n_state, col_w):
    n_kb = win_ref.shape[0]
    sp = n_state // n_kb

    @pl.when(pl.program_id(1) == 0)
    def _():
        cre_ref[...] = jnp.zeros_like(cre_ref)
        cim_ref[...] = jnp.zeros_like(cim_ref)

    u = u_ref[0]
    ub = u.astype(BF16)
    for kb in range(n_kb):
        bu = _dot(ub[:, kb * LANES:(kb + 1) * LANES], win_ref[kb])
        hre_ref[:, kb * sp:(kb + 1) * sp] = bu[:, :sp]
        him_ref[:, kb * sp:(kb + 1) * sp] = bu[:, sp:]

    n_slab = rows // SUBLANES
    for c0 in range(0, n_state, col_w):
        cs = slice(c0, c0 + col_w)

        def body(i, carry, cs=cs):
            c_re, c_im = carry
            sl = pl.ds(pl.multiple_of(i * SUBLANES, SUBLANES), SUBLANES)
            x_re = hre_ref[sl, cs]
            x_im = him_ref[sl, cs]
            for si, sh in enumerate((1, 2, 4)):
                p_re = coef_ref[2 * si, :, cs]
                p_im = coef_ref[2 * si + 1, :, cs]
                s_re = pltpu.roll(x_re, sh, axis=0)
                s_im = pltpu.roll(x_im, sh, axis=0)
                x_re, x_im = (x_re + p_re * s_re - p_im * s_im,
                              x_im + p_re * s_im + p_im * s_re)
            p_re = coef_ref[6, :, cs]
            p_im = coef_ref[7, :, cs]
            h_re = x_re + p_re * c_re - p_im * c_im
            h_im = x_im + p_re * c_im + p_im * c_re
            hre_ref[sl, cs] = h_re
            him_ref[sl, cs] = h_im
            last = slice(SUBLANES - 1, SUBLANES)
            return (jnp.broadcast_to(h_re[last, :], h_re.shape),
                    jnp.broadcast_to(h_im[last, :], h_im.shape))

        c_re, c_im = lax.fori_loop(0, n_slab, body, (cre_ref[:, cs], cim_ref[:, cs]))
        cre_ref[:, cs] = c_re
        cim_ref[:, cs] = c_im

    ys = []
    for kb in range(n_kb):
        hcat = jnp.concatenate([hre_ref[:, kb * sp:(kb + 1) * sp].astype(BF16),
                                him_ref[:, kb * sp:(kb + 1) * sp].astype(BF16)], axis=1)
        ys.append(_dot(hcat, wout_ref[kb]))
    y = jnp.concatenate(ys, axis=1) + d_ref[...] * u
    y = _gelu_tanh(y)
    z = _dot(y.astype(BF16), wglu_ref[...]) + bglu_ref[...]
    o_ref[0] = y * jax.nn.sigmoid(z)


def _ssm(u, w_in, w_out, coef, d_flat, w_glu, b_glu):
    bsz, seq, ssm_w = u.shape
    n_state = coef.shape[-1]
    rows = min(SSM_ROWS, seq)
    full = lambda *shape: pl.BlockSpec(shape, lambda b, i: (0,) * len(shape))
    kern = functools.partial(_ssm_kernel, rows=rows, n_state=n_state, col_w=4 * LANES)
    return pl.pallas_call(
        kern,
        grid=(bsz, seq // rows),
        in_specs=[pl.BlockSpec((1, rows, ssm_w), lambda b, i: (b, i, 0)),
                  full(*w_in.shape), full(*w_out.shape), full(*coef.shape), full(1, ssm_w),
                  full(ssm_w, ssm_w), full(1, ssm_w)],
        out_specs=pl.BlockSpec((1, rows, ssm_w), lambda b, i: (b, i, 0)),
        out_shape=jax.ShapeDtypeStruct((bsz, seq, ssm_w), F32),
        scratch_shapes=[pltpu.VMEM((rows, n_state), F32), pltpu.VMEM((rows, n_state), F32),
                        pltpu.VMEM((SUBLANES, n_state), F32), pltpu.VMEM((SUBLANES, n_state), F32)],
        compiler_params=_cparams(("parallel", "arbitrary")),
        name="s5_glu",
    )(u, w_in, w_out, coef, d_flat, w_glu.astype(BF16), b_glu.reshape(1, ssm_w))


def _layer_norm(y, g, b):
    mu = jnp.mean(y, axis=-1, keepdims=True)
    yc = y - mu
    var = jnp.mean(yc * yc, axis=-1, keepdims=True)
    return yc * lax.rsqrt(var + EPS) * g + b


def _rms_norm(y, g):
    return y * lax.rsqrt(jnp.mean(y * y, axis=-1, keepdims=True) + EPS) * g


def _mix_kernel(x_ref, att_ref, ssm_ref, gatt_ref, gssm_ref, wout_ref, g1_ref, ln_g_ref, ln_b_ref,
                sc_ref, sh_ref, g2_ref, wsg_ref, wsu_ref, wsd_ref, h_ref, ht_ref, pre_ref, *, alpha, att_w):
    a_n = _rms_norm(att_ref[0], gatt_ref[...]).astype(BF16)
    s_n = _rms_norm(ssm_ref[0], gssm_ref[...]).astype(BF16)
    mix = _dot(a_n, wout_ref[0:att_w, :]) + _dot(s_n, wout_ref[att_w:, :])
    x1 = _layer_norm(alpha * x_ref[0] + g1_ref[0] * mix, ln_g_ref[...], ln_b_ref[...])
    h = x1 * (1.0 + sc_ref[0]) + sh_ref[0]
    h_ref[0] = h
    _store_token_tiles(ht_ref, h)
    hb = h.astype(BF16)
    hid = _silu(_dot(hb, wsg_ref[...])) * _dot(hb, wsu_ref[...])
    shared = _dot(hid.astype(BF16), wsd_ref[...])
    pre_ref[0] = alpha * x1 + g2_ref[0] * shared


def _mix(x, att, ssm, g_att, g_ssm, w_out, gate1, ln_g, ln_b, scale2, shift2, gate2,
         w_s_gate, w_s_up, w_s_down, alpha):
    bsz, seq, d = x.shape
    att_w = att.shape[-1]
    ssm_w = ssm.shape[-1]
    ff = w_s_gate.shape[1]
    tm = min(MIX_ROWS, seq)
    n_sub = d // LANES
    row = lambda b, i: (b, i, 0)
    per_b = lambda b, i: (b, 0, 0)
    full = lambda *shape: pl.BlockSpec(shape, lambda b, i: (0,) * len(shape))
    vec = lambda t: t.reshape(bsz, 1, d)
    return pl.pallas_call(
        functools.partial(_mix_kernel, alpha=alpha, att_w=att_w),
        grid=(bsz, seq // tm),
        in_specs=[pl.BlockSpec((1, tm, d), row), pl.BlockSpec((1, tm, att_w), row),
                  pl.BlockSpec((1, tm, ssm_w), row),
                  full(1, att_w), full(1, ssm_w), full(att_w + ssm_w, d),
                  pl.BlockSpec((1, 1, d), per_b), full(1, d), full(1, d),
                  pl.BlockSpec((1, 1, d), per_b), pl.BlockSpec((1, 1, d), per_b),
                  pl.BlockSpec((1, 1, d), per_b),
                  full(d, ff), full(d, ff), full(ff, d)],
        out_specs=[pl.BlockSpec((1, tm, d), row),
                   pl.BlockSpec((tm * n_sub, LANES), lambda b, i: (b * (seq // tm) + i, 0)),
                   pl.BlockSpec((1, tm, d), row)],
        out_shape=[jax.ShapeDtypeStruct((bsz, seq, d), F32),
                   jax.ShapeDtypeStruct((bsz * seq * n_sub, LANES), F32),
                   jax.ShapeDtypeStruct((bsz, seq, d), F32)],
        compiler_params=_cparams(("parallel", "parallel")),
        name="mix_ln1_shared",
    )(x, att, ssm, g_att.reshape(1, att_w), g_ssm.reshape(1, ssm_w), w_out.astype(BF16),
      vec(gate1), ln_g.reshape(1, d), ln_b.reshape(1, d), vec(scale2), vec(shift2), vec(gate2),
      w_s_gate.astype(BF16), w_s_up.astype(BF16), w_s_down.astype(BF16))


def _route_kernel(h_ref, wrt_ref, wrt_lo_ref, bias_ref, tri_ref, idx_ref, gate_ref, rank_ref, cnt_ref, carry_ref):
    n_e = wrt_ref.shape[0]
    tm = h_ref.shape[0]
    per_group = n_e // N_EXPERT_GROUPS
    neg = -jnp.inf

    @pl.when(pl.program_id(0) == 0)
    def _():
        carry_ref[...] = jnp.zeros_like(carry_ref)

    h_hi, h_lo = _split_bf16(h_ref[...])
    logits = (_dot_nt(wrt_ref[...], h_hi) + _dot_nt(wrt_ref[...], h_lo)
              + _dot_nt(wrt_lo_ref[...], h_hi))
    scores = jax.nn.sigmoid(logits)
    biased = scores + bias_ref[...]

    g3 = biased.reshape(N_EXPERT_GROUPS, per_group, tm)
    ridx = lax.broadcasted_iota(jnp.int32, g3.shape, 1).astype(F32)
    m1 = jnp.max(g3, axis=1, keepdims=True)
    first = jnp.min(jnp.where(g3 == m1, ridx, float(per_group)), axis=1, keepdims=True)
    m2 = jnp.max(jnp.where(ridx == first, neg, g3), axis=1, keepdims=True)
    gs = m1 + m2

    gidx = lax.broadcasted_iota(jnp.int32, gs.shape, 0).astype(F32)
    ok = jnp.zeros_like(gs)
    cur = gs
    for _ in range(TOPK_GROUPS):
        mx = jnp.max(cur, axis=0, keepdims=True)
        fi = jnp.min(jnp.where(cur == mx, gidx, float(N_EXPERT_GROUPS)), axis=0, keepdims=True)
        hit = gidx == fi
        ok = jnp.where(hit, 1.0, ok)
        cur = jnp.where(hit, neg, cur)
    masked = jnp.where(ok > 0.5, g3, neg).reshape(n_e, tm)

    eidx = lax.broadcasted_iota(jnp.int32, (n_e, tm), 0).astype(F32)
    onehot = jnp.zeros((n_e, tm), F32)
    cur = masked
    sel_idx = []
    sel_gate = []
    for _ in range(TOP_K):
        mx = jnp.max(cur, axis=0, keepdims=True)
        fi = jnp.min(jnp.where(cur == mx, eidx, float(n_e)), axis=0, keepdims=True)
        hit = eidx == fi
        sel_idx.append(fi)
        sel_gate.append(jnp.sum(jnp.where(hit, scores, 0.0), axis=0, keepdims=True))
        onehot = jnp.where(hit, 1.0, onehot)
        cur = jnp.where(hit, neg, cur)
    idx = jnp.concatenate(sel_idx, axis=0)
    gate = jnp.concatenate(sel_gate, axis=0)
    gate = gate / jnp.sum(gate, axis=0, keepdims=True) * ROUTED_SCALE

    prior = _dot(onehot.astype(BF16), tri_ref[...]) + carry_ref[:, 0:1]
    ranks = [jnp.sum(jnp.where(eidx == sel_idx[k], prior, 0.0), axis=0, keepdims=True)
             for k in range(TOP_K)]
    rank = jnp.concatenate(ranks, axis=0)
    carry = carry_ref[...] + jnp.sum(onehot, axis=1, keepdims=True)
    carry_ref[...] = carry
    cnt_ref[...] = carry

    idx_ref[...] = idx.astype(jnp.int32)
    gate_ref[...] = gate
    rank_ref[...] = rank.astype(jnp.int32)


def _route(h2d, w_router, router_bias):
    n_tok, d = h2d.shape
    n_e = w_router.shape[1]
    tm = min(ROUTE_COLS, n_tok)
    tri = (np.arange(tm)[:, None] < np.arange(tm)[None, :]).astype(np.float32)
    w_hi, w_lo = _split_bf16(w_router.T.astype(F32))
    col = lambda i: (0, i)
    full = lambda *shape: pl.BlockSpec(shape, lambda i: (0,) * len(shape))
    idx, gate, rank, cnt = pl.pallas_call(
        _route_kernel,
        grid=(n_tok // tm,),
        in_specs=[pl.BlockSpec((tm, d), lambda i: (i, 0)), full(n_e, d), full(n_e, d), full(n_e, 1),
                  full(tm, tm)],
        out_specs=[pl.BlockSpec((TOP_K, tm), col), pl.BlockSpec((TOP_K, tm), col),
                   pl.BlockSpec((TOP_K, tm), col), full(n_e, LANES)],
        out_shape=[jax.ShapeDtypeStruct((TOP_K, n_tok), jnp.int32),
                   jax.ShapeDtypeStruct((TOP_K, n_tok), F32),
                   jax.ShapeDtypeStruct((TOP_K, n_tok), jnp.int32),
                   jax.ShapeDtypeStruct((n_e, LANES), F32)],
        scratch_shapes=[pltpu.VMEM((n_e, LANES), F32)],
        compiler_params=_cparams(("arbitrary",)),
        name="router_topk",
    )(h2d, w_hi, w_lo, router_bias.reshape(n_e, 1).astype(F32), jnp.asarray(tri, BF16))
    return idx, gate, rank, cnt[:, 0].astype(jnp.int32)


def _dispatch_kernel(pend_ref, cnt_ref, dest_ref, ht_ref, xs_ref, zero_ref, sem_ref, *, n_e, blk, tm):

    def zero_copy(e):
        start = pl.multiple_of(pend_ref[e] - blk, blk)
        return pltpu.make_async_copy(zero_ref, xs_ref.at[pl.ds(start, blk)], sem_ref.at[0])

    @pl.when(pl.program_id(0) == 0)
    def _():
        zero_ref[...] = jnp.zeros_like(zero_ref)

        def start(e, c):
            @pl.when(cnt_ref[e] > 0)
            def _():
                zero_copy(e).start()
            return c
        lax.fori_loop(0, n_e, start, 0)

        def wait(e, c):
            @pl.when(cnt_ref[e] > 0)
            def _():
                zero_copy(e).wait()
            return c
        lax.fori_loop(0, n_e, wait, 0)

    n_sub = ht_ref.shape[0] // tm

    def row_copy(t, dst_row):
        src = ht_ref.at[pl.ds(pl.multiple_of(t * n_sub, n_sub), n_sub)]
        return pltpu.make_async_copy(src, xs_ref.at[dst_row], sem_ref.at[1])

    def issue(g, c):
        t0 = g * DMA_ISSUE_UNROLL
        rows = [[dest_ref[(t0 + j) * TOP_K + k] for k in range(TOP_K)] for j in range(DMA_ISSUE_UNROLL)]
        for j in range(DMA_ISSUE_UNROLL):
            for k in range(TOP_K):
                row_copy(t0 + j, rows[j][k]).start(priority=k % 2)
        return c
    lax.fori_loop(0, tm // DMA_ISSUE_UNROLL, issue, 0)

    def drain(t, c):
        for k in range(TOP_K):
            row_copy(t, dest_ref[t * TOP_K + k]).wait()
        return c
    lax.fori_loop(0, tm, drain, 0)


def _dispatch(h_tiles, n_sub, dest, pend, cnt, n_rows):
    n_tok = h_tiles.shape[0] // n_sub
    n_e = cnt.shape[0]
    tm = min(DISPATCH_ROWS, n_tok)
    grid_spec = pltpu.PrefetchScalarGridSpec(
        num_scalar_prefetch=2,
        grid=(n_tok // tm,),
        in_specs=[pl.BlockSpec((TOP_K * tm,), lambda i, *_: (i,), memory_space=pltpu.SMEM),
                  pl.BlockSpec((tm * n_sub, LANES), lambda i, *_: (i, 0))],
        out_specs=pl.BlockSpec(memory_space=pl.ANY),
        scratch_shapes=[pltpu.VMEM((MOE_ROWS, n_sub, LANES), F32), pltpu.SemaphoreType.DMA((2,))],
    )
    return pl.pallas_call(
        functools.partial(_dispatch_kernel, n_e=n_e, blk=MOE_ROWS, tm=tm),
        grid_spec=grid_spec,
        out_shape=jax.ShapeDtypeStruct((n_rows, n_sub, LANES), F32),
        compiler_params=_cparams(("arbitrary",)),
        name="moe_dispatch",
    )(pend, cnt, dest, h_tiles)


def _expert_kernel(be_ref, nused_ref, x_ref, wg_ref, wu_ref, wd_ref, y_ref, wgb_ref, wub_ref, wdb_ref):
    i = pl.program_id(0)
    prev = be_ref[jnp.maximum(i - 1, 0)]
    fresh = jnp.logical_or(i == 0, be_ref[i] != prev)

    @pl.when(jnp.logical_and(i < nused_ref[0], fresh))
    def _():
        wgb_ref[...] = wg_ref[0].astype(BF16)
        wub_ref[...] = wu_ref[0].astype(BF16)
        wdb_ref[...] = wd_ref[0].astype(BF16)

    @pl.when(i < nused_ref[0])
    def _():
        d = wgb_ref.shape[0]
        xb = jnp.concatenate(_load_token_tiles(x_ref, MOE_ROWS, d), axis=1).astype(BF16)
        hid = _silu(_dot(xb, wgb_ref[...])) * _dot(xb, wub_ref[...])
        _store_token_tiles(y_ref, _dot(hid.astype(BF16), wdb_ref[...]))


def _experts(xs_tiles, block_expert, n_used, w_gate, w_up, w_down):
    n_rows, n_sub, _ = xs_tiles.shape
    d = n_sub * LANES
    n_blk = n_rows // MOE_ROWS
    ff = w_gate.shape[-1]
    clamp = lambda i, be, nu: jnp.minimum(i, nu[0] - 1)
    rows_spec = pl.BlockSpec((MOE_ROWS * n_sub, LANES), lambda i, be, nu: (clamp(i, be, nu), 0))
    grid_spec = pltpu.PrefetchScalarGridSpec(
        num_scalar_prefetch=2,
        grid=(n_blk,),
        in_specs=[rows_spec,
                  pl.BlockSpec((1, d, ff), lambda i, be, nu: (be[clamp(i, be, nu)], 0, 0)),
                  pl.BlockSpec((1, d, ff), lambda i, be, nu: (be[clamp(i, be, nu)], 0, 0)),
                  pl.BlockSpec((1, ff, d), lambda i, be, nu: (be[clamp(i, be, nu)], 0, 0))],
        out_specs=rows_spec,
        scratch_shapes=[pltpu.VMEM((d, ff), BF16), pltpu.VMEM((d, ff), BF16), pltpu.VMEM((ff, d), BF16)],
    )
    ys = pl.pallas_call(
        _expert_kernel,
        grid_spec=grid_spec,
        out_shape=jax.ShapeDtypeStruct((n_rows * n_sub, LANES), F32),
        compiler_params=_cparams(("arbitrary",)),
        name="moe_experts",
    )(block_expert, n_used, xs_tiles.reshape(n_rows * n_sub, LANES), w_gate, w_up, w_down)
    return ys.reshape(n_rows, n_sub, LANES)


def _combine_kernel(dest_ref, ys_ref, gate_ref, pre_ref, g2_ref, ln_g_ref, ln_b_ref, o_ref, buf_ref, sem_ref):
    tm, d = pre_ref.shape[1], pre_ref.shape[2]
    n_sub = d // LANES

    def row_copy(t, k, src_row):
        slot = pl.multiple_of((k * tm + t) * n_sub, n_sub)
        return pltpu.make_async_copy(ys_ref.at[src_row], buf_ref.at[pl.ds(slot, n_sub)], sem_ref.at[0])

    def issue(g, c):
        t0 = g * DMA_ISSUE_UNROLL
        rows = [[dest_ref[(t0 + j) * TOP_K + k] for k in range(TOP_K)] for j in range(DMA_ISSUE_UNROLL)]
        for j in range(DMA_ISSUE_UNROLL):
            for k in range(TOP_K):
                row_copy(t0 + j, k, rows[j][k]).start(priority=k % 2)
        return c
    lax.fori_loop(0, tm // DMA_ISSUE_UNROLL, issue, 0)

    def drain(t, c):
        for k in range(TOP_K):
            row_copy(t, k, dest_ref[t * TOP_K + k]).wait()
        return c
    lax.fori_loop(0, tm, drain, 0)

    gates = gate_ref[...]
    gate_cols = [jnp.broadcast_to(gates[:, k:k + 1], (tm, LANES)) for k in range(TOP_K)]
    pieces = []
    for s in range(n_sub):
        acc = None
        for k in range(TOP_K):
            term = buf_ref[pl.ds(k * tm * n_sub + s, tm, stride=n_sub), :] * gate_cols[k]
            acc = term if acc is None else acc + term
        pieces.append(acc)
    routed = jnp.concatenate(pieces, axis=1)
    y = pre_ref[0] + g2_ref[0] * routed
    o_ref[0] = _layer_norm(y, ln_g_ref[...], ln_b_ref[...])


def _combine(ys, dest, gates_t, pre, gate2, ln_g, ln_b):
    bsz, seq, d = pre.shape
    tm = min(DISPATCH_ROWS, seq)
    per_seq = seq // tm
    row = lambda i, *_: (i // per_seq, i % per_seq, 0)
    per_b = lambda i, *_: (i // per_seq, 0, 0)
    full = lambda *shape: pl.BlockSpec(shape, lambda i, *_: (0,) * len(shape))
    grid_spec = pltpu.PrefetchScalarGridSpec(
        num_scalar_prefetch=0,
        grid=(bsz * per_seq,),
        in_specs=[pl.BlockSpec((TOP_K * tm,), lambda i: (i,), memory_space=pltpu.SMEM),
                  pl.BlockSpec(memory_space=pl.ANY),
                  pl.BlockSpec((tm, TOP_K), lambda i: (i, 0)),
                  pl.BlockSpec((1, tm, d), row), pl.BlockSpec((1, 1, d), per_b), full(1, d), full(1, d)],
        out_specs=pl.BlockSpec((1, tm, d), row),
        scratch_shapes=[pltpu.VMEM((TOP_K * tm * (d // LANES), LANES), F32), pltpu.SemaphoreType.DMA((1,))],
    )
    return pl.pallas_call(
        _combine_kernel,
        grid_spec=grid_spec,
        out_shape=jax.ShapeDtypeStruct((bsz, seq, d), F32),
        compiler_params=_cparams(("arbitrary",)),
        name="moe_combine_ln2",
    )(dest, ys, gates_t, pre, gate2.reshape(bsz, 1, d), ln_g.reshape(1, d), ln_b.reshape(1, d))


def _dest_kernel(pstart_ref, idx_ref, rank_ref, dest_ref):
    idx = idx_ref[...]

    def body(e, acc):
        return acc + jnp.where(idx == e, pstart_ref[e], 0)
    dest_ref[...] = lax.fori_loop(0, pstart_ref.shape[0], body, rank_ref[...], unroll=8)


def _dest_rows(pstart, idx, rank):
    n_k, n_tok = idx.shape
    tn = min(4096, n_tok)
    grid_spec = pltpu.PrefetchScalarGridSpec(
        num_scalar_prefetch=1,
        grid=(n_tok // tn,),
        in_specs=[pl.BlockSpec((n_k, tn), lambda i, *_: (0, i))] * 2,
        out_specs=pl.BlockSpec((n_k, tn), lambda i, *_: (0, i)),
    )
    return pl.pallas_call(
        _dest_kernel,
        grid_spec=grid_spec,
        out_shape=jax.ShapeDtypeStruct((n_k, n_tok), jnp.int32),
        compiler_params=_cparams(("arbitrary",)),
        name="moe_dest",
    )(pstart, idx, rank)


def _moe_plan(idx, rank, cnt, n_tok):
    n_e = cnt.shape[0]
    padded = (cnt + MOE_ROWS - 1) // MOE_ROWS * MOE_ROWS
    pend = jnp.cumsum(padded).astype(jnp.int32)
    pstart = pend - padded
    dest = _dest_rows(pstart, idx, rank).T.reshape(-1)
    n_blk = n_tok * TOP_K // MOE_ROWS + n_e
    starts = jnp.arange(n_blk, dtype=jnp.int32) * MOE_ROWS
    block_expert = jnp.minimum(jnp.sum((pend[None, :] <= starts[:, None]).astype(jnp.int32), axis=1),
                               n_e - 1).astype(jnp.int32)
    n_used = (pend[-1:] // MOE_ROWS).astype(jnp.int32)
    return dest, pend, block_expert, n_used, n_blk * MOE_ROWS


def kernel(x, c, rel_bias, w_ada, b_ada, w_in, ssm_a_re, ssm_a_im, ssm_b_re, ssm_b_im, ssm_c_re, ssm_c_im, ssm_d, ssm_log_dt, w_glu, b_glu, g_att, g_ssm, w_out, ln1_g, ln1_b, w_router, router_bias, w_e_gate, w_e_up, w_e_down, w_s_gate, w_s_up, w_s_down, ln2_g, ln2_b):
    bsz, seq, d = x.shape
    depth = w_ada.shape[0]
    alpha = (2 * depth) ** 0.25
    att_w = g_att.shape[-1]
    tables = _att_tables(rel_bias)
    for layer in range(depth):
        ada = _ada(c, w_ada[layer], b_ada[layer])
        shift1, scale1, gate1, shift2, scale2, gate2 = jnp.split(ada, 6, axis=-1)
        q, k, v, u = _inproj(x, scale1, shift1, w_in[layer], att_w)
        att = _attention(q, k, v, tables)
        ssm_prm = _ssm_params(ssm_a_re[layer], ssm_a_im[layer], ssm_b_re[layer], ssm_b_im[layer],
                              ssm_c_re[layer], ssm_c_im[layer], ssm_d[layer], ssm_log_dt[layer])
        ssm = _ssm(u, *ssm_prm, w_glu[layer], b_glu[layer])
        h2, h2_tiles, pre = _mix(x, att, ssm, g_att[layer], g_ssm[layer], w_out[layer], gate1,
                                 ln1_g[layer], ln1_b[layer], scale2, shift2, gate2, w_s_gate[layer],
                                 w_s_up[layer], w_s_down[layer], alpha)
        h2d = h2.reshape(bsz * seq, d)
        idx, gate, rank, cnt = _route(h2d, w_router[layer], router_bias[layer])
        dest, pend, block_expert, n_used, n_rows = _moe_plan(idx, rank, cnt, bsz * seq)
        xs = _dispatch(h2_tiles, d // LANES, dest, pend, cnt, n_rows)
        ys = _experts(xs, block_expert, n_used, w_e_gate[layer], w_e_up[layer], w_e_down[layer])
        x = _combine(ys, dest, gate.T, pre, gate2, ln2_g[layer], ln2_b[layer])
    return x
```

```python
import functools
import math

import jax
import jax.numpy as jnp
import numpy as np
from jax import lax
from jax.experimental import pallas as pl
from jax.experimental.pallas import tpu as pltpu

F32 = jnp.float32
BF16 = jnp.bfloat16

HEAD_DIM = 64
ATT_BLOCK = 128
PATTERNS = ((128, 1), (512, 4), (2048, 16))
N_BUCKETS = 32
MAX_DISTANCE = 2048
SSM_GROUP = 16
SSM_STATE = 64
N_EXPERTS = 256
TOP_K = 8
N_EXPERT_GROUPS = 8
TOPK_GROUPS = 4
ROUTED_SCALE = 2.5
EPS = 1e-5
NEG_INF = -1e30

LANES = 128
SUBLANES = 8
VMEM_LIMIT_BYTES = 56 * 1024 * 1024

PROJ_ROWS = 512
SSM_ROWS = 256
MIX_ROWS = 512
ROUTE_COLS = 512
MOE_ROWS = 256
DISPATCH_ROWS = 256
ATT_UNITS_PER_STEP = 4
DMA_ISSUE_UNROLL = 4


def _cparams(sem, vmem=VMEM_LIMIT_BYTES):
    return pltpu.CompilerParams(dimension_semantics=sem, vmem_limit_bytes=vmem)


def _dot(a, b):
    return jnp.dot(a, b, preferred_element_type=F32)


def _dot_nt(a, b):
    return lax.dot_general(a, b, (((1,), (1,)), ((), ())), preferred_element_type=F32)


def _silu(x):
    return x * jax.nn.sigmoid(x)


def _store_token_tiles(ref, x):
    m, d = x.shape
    n_sub = d // LANES
    for s in range(n_sub):
        ref[pl.ds(s, m, stride=n_sub), :] = x[:, s * LANES:(s + 1) * LANES]


def _load_token_tiles(ref, m, d, row0=0):
    n_sub = d // LANES
    return [ref[pl.ds(row0 + s, m, stride=n_sub), :] for s in range(n_sub)]


def _split_bf16(x):
    hi = x.astype(BF16)
    lo = (x - hi.astype(F32)).astype(BF16)
    return hi, lo


def _ada_kernel(c_ref, w_ref, b_ref, o_ref):
    c = c_ref[...]
    a_hi, a_lo = _split_bf16(_silu(c))
    w_hi, w_lo = _split_bf16(w_ref[...])
    acc = _dot(a_hi, w_hi) + _dot(a_hi, w_lo) + _dot(a_lo, w_hi)
    o_ref[...] = acc + b_ref[...]


def _ada(c, w_ada, b_ada):
    bsz, d = c.shape
    n = w_ada.shape[1]
    rows = SUBLANES
    c_pad = jnp.zeros((rows, d), F32).at[:bsz].set(c)
    tn = 1024
    out = pl.pallas_call(
        _ada_kernel,
        grid=(n // tn,),
        in_specs=[pl.BlockSpec((rows, d), lambda j: (0, 0)),
                  pl.BlockSpec((d, tn), lambda j: (0, j)),
                  pl.BlockSpec((1, tn), lambda j: (0, j))],
        out_specs=pl.BlockSpec((rows, tn), lambda j: (0, j)),
        out_shape=jax.ShapeDtypeStruct((rows, n), F32),
        compiler_params=_cparams(("parallel",)),
        name="ada",
    )(c_pad, w_ada, b_ada.reshape(1, n))
    return out[:bsz]


def _inproj_kernel(x_ref, sc_ref, sh_ref, w_ref, q_ref, k_ref, v_ref, u_ref, *, att_w, q_scale):
    h = (x_ref[0] * (1.0 + sc_ref[0]) + sh_ref[0]).astype(BF16)
    q_ref[0] = _dot(h, w_ref[:, 0:att_w]) * q_scale
    k_ref[0] = _dot(h, w_ref[:, att_w:2 * att_w])
    v_ref[0] = _dot(h, w_ref[:, 2 * att_w:3 * att_w])
    u_ref[0] = _dot(h, w_ref[:, 3 * att_w:])


def _inproj(x, scale, shift, w_in, att_w):
    bsz, seq, d = x.shape
    n = w_in.shape[1]
    ssm_w = n - 3 * att_w
    ts = min(PROJ_ROWS, seq)
    row = lambda b, i: (b, i, 0)
    per_b = lambda b, i: (b, 0, 0)
    kern = functools.partial(_inproj_kernel, att_w=att_w, q_scale=HEAD_DIM ** -0.5)
    return pl.pallas_call(
        kern,
        grid=(bsz, seq // ts),
        in_specs=[pl.BlockSpec((1, ts, d), row),
                  pl.BlockSpec((1, 1, d), per_b),
                  pl.BlockSpec((1, 1, d), per_b),
                  pl.BlockSpec((d, n), lambda b, i: (0, 0))],
        out_specs=[pl.BlockSpec((1, ts, att_w), row)] * 3 + [pl.BlockSpec((1, ts, ssm_w), row)],
        out_shape=[jax.ShapeDtypeStruct((bsz, seq, att_w), F32)] * 3
                  + [jax.ShapeDtypeStruct((bsz, seq, ssm_w), F32)],
        compiler_params=_cparams(("parallel", "parallel")),
        name="inproj",
    )(x, scale.reshape(bsz, 1, d), shift.reshape(bsz, 1, d), w_in.astype(BF16))


def _t5_bucket_np(dist):
    exact = N_BUCKETS // 2
    large = exact + (np.log(np.maximum(dist, 1).astype(np.float64) / exact)
                     / math.log(MAX_DISTANCE / exact) * (N_BUCKETS - exact)).astype(np.int64)
    return np.where(dist < exact, dist, np.minimum(large, N_BUCKETS - 1))


def _att_tables(rel_bias):
    qi = np.arange(ATT_BLOCK)[:, None]
    ki = np.arange(2 * ATT_BLOCK)[None, :]
    rel = qi + ATT_BLOCK - ki
    tabs = []
    for window, dil in PATTERNS:
        band = (rel >= 0) & (rel <= window // dil)
        bucket = _t5_bucket_np(np.maximum(rel, 0) * dil)
        onehot = (bucket[:, :, None] == np.arange(N_BUCKETS)[None, None, :]).astype(np.float32)
        bias = jnp.einsum('qkb,bh->hqk', onehot, rel_bias.astype(F32),
                          precision=lax.Precision.HIGHEST)
        full = jnp.where(band[None], bias, NEG_INF)
        first = jnp.concatenate([full[:, :, ATT_BLOCK:], jnp.full_like(full[:, :, ATT_BLOCK:], NEG_INF)], axis=-1)
        tabs.append(jnp.stack([full, first]))
    return jnp.stack(tabs)


def _att_kernel(q_ref, k_ref, v_ref, tab_ref, o_ref, m_ref, l_ref, *, seq):
    lane = lax.broadcasted_iota(jnp.int32, (ATT_BLOCK, LANES), 1)
    head0 = lane < HEAD_DIM

    def rows(ref, start, n, dil):
        if dil == 1:
            return ref[0, pl.ds(pl.multiple_of(start, ATT_BLOCK), n), :]
        return ref[0, pl.ds(start, n, stride=dil), :]

    nb = ATT_UNITS_PER_STEP
    q_rows = nb * ATT_BLOCK

    def lane_mask(n_rows, h):
        m = lax.broadcasted_iota(jnp.int32, (n_rows, LANES), 1) < HEAD_DIM
        return m if h == 0 else ~m

    def batch(pi, dil, res, blk0, head, init):
        kv_blocks = nb if head else nb + 1
        q_start = res + dil * ATT_BLOCK * blk0
        kv_start = q_start if head else q_start - dil * ATT_BLOCK
        if dil == 1:
            sl = pl.ds(pl.multiple_of(q_start, ATT_BLOCK), q_rows)
        else:
            sl = pl.ds(q_start, q_rows, stride=dil)
        q = rows(q_ref, q_start, q_rows, dil)
        k = rows(k_ref, kv_start, kv_blocks * ATT_BLOCK, dil).astype(BF16)
        v = rows(v_ref, kv_start, kv_blocks * ATT_BLOCK, dil)
        if not init:
            m_old, l_old, o_old = m_ref[sl, :], l_ref[sl, :], o_ref[0, sl, :]
        qh = [jnp.where(lane_mask(q_rows, h), q, 0.0).astype(BF16) for h in range(2)]
        vh = [jnp.where(lane_mask(kv_blocks * ATT_BLOCK, h), v, 1.0).astype(BF16) for h in range(2)]
        o_parts, l_parts, m_parts = [], [], []
        for j in range(nb):
            first = 1 if (head and j == 0) else 0
            kb = j if (not head or j == 0) else j - 1
            ksl = slice(kb * ATT_BLOCK, (kb + 2) * ATT_BLOCK)
            qsl = slice(j * ATT_BLOCK, (j + 1) * ATT_BLOCK)
            outs, ms = [], []
            for h in range(2):
                s = _dot_nt(qh[h][qsl], k[ksl]) + tab_ref[pi, first, h]
                m = jnp.max(s, axis=1, keepdims=True)
                p = jnp.exp(s - m).astype(BF16)
                outs.append(_dot(p, vh[h][ksl]))
                ms.append(m)
            o_parts.append(jnp.where(head0, outs[0], outs[1]))
            l_parts.append(pltpu.roll(jnp.where(head0, outs[1], outs[0]), HEAD_DIM, axis=1))
            m_parts.append(jnp.where(head0, ms[0], ms[1]))
        o = jnp.concatenate(o_parts, axis=0)
        l = jnp.concatenate(l_parts, axis=0)
        m = jnp.concatenate(m_parts, axis=0)
        if not init:
            m_new = jnp.maximum(m_old, m)
            a_old = jnp.exp(m_old - m_new)
            a_cur = jnp.exp(m - m_new)
            o = o_old * a_old + o * a_cur
            l = l_old * a_old + l * a_cur
            m = m_new
        o_ref[0, sl, :] = o
        m_ref[sl, :] = m
        l_ref[sl, :] = l

    n_units = seq // ATT_BLOCK
    order = sorted(range(len(PATTERNS)), key=lambda p: -PATTERNS[p][1])
    for pos, pi in enumerate(order):
        dil = PATTERNS[pi][1]
        log_d = dil.bit_length() - 1
        per_res = seq // (dil * ATT_BLOCK) // nb
        init = pos == 0

        def head_body(res, carry, pi=pi, dil=dil, init=init):
            batch(pi, dil, res, 0, True, init)
            return carry
        lax.fori_loop(0, dil, head_body, 0)

        if per_res > 1:
            def tail_body(i, carry, pi=pi, dil=dil, log_d=log_d, init=init):
                res = i & (dil - 1)
                blk0 = nb * (1 + (i >> log_d))
                batch(pi, dil, res, blk0, False, init)
                return carry
            lax.fori_loop(0, dil * (per_res - 1), tail_body, 0)

    def finish(i, carry):
        sl = pl.ds(pl.multiple_of(i * ATT_BLOCK, ATT_BLOCK), ATT_BLOCK)
        o_ref[0, sl, :] = o_ref[0, sl, :] / l_ref[sl, :]
        return carry
    lax.fori_loop(0, n_units, finish, 0)


def _attention(q, k, v, tables):
    bsz, seq, att_w = q.shape
    n_pairs = att_w // LANES
    blk = lambda b, p: (b, 0, p)
    qkv_spec = pl.BlockSpec((1, seq, LANES), blk)
    n_pat = tables.shape[0]
    tab_spec = pl.BlockSpec((n_pat, 2, 2, ATT_BLOCK, 2 * ATT_BLOCK), lambda b, p: (0, 0, p, 0, 0))
    return pl.pallas_call(
        functools.partial(_att_kernel, seq=seq),
        grid=(bsz, n_pairs),
        in_specs=[qkv_spec, qkv_spec, qkv_spec, tab_spec],
        out_specs=pl.BlockSpec((1, seq, LANES), blk),
        out_shape=jax.ShapeDtypeStruct((bsz, seq, att_w), F32),
        scratch_shapes=[pltpu.VMEM((seq, LANES), F32), pltpu.VMEM((seq, LANES), F32)],
        compiler_params=_cparams(("parallel", "parallel")),
        name="dilated_attention",
    )(q, k, v, tables)


def _ssm_params(a_re, a_im, b_re, b_im, c_re, c_im, d_skip, log_dt):
    n_g, n_p = a_re.shape
    n_c = b_re.shape[-1]
    n_state = n_g * n_p
    dt = jnp.exp(log_dt.astype(F32))[:, None]

    def a_pow(kk):
        mag = jnp.exp(kk * dt * a_re)
        ph = kk * dt * a_im
        return mag * jnp.cos(ph), mag * jnp.sin(ph)

    ab_re, ab_im = a_pow(1.0)
    nr, ni = ab_re - 1.0, ab_im
    den = a_re * a_re + a_im * a_im
    f_re = (nr * a_re + ni * a_im) / den
    f_im = (ni * a_re - nr * a_im) / den
    bb_re = f_re[:, :, None] * b_re - f_im[:, :, None] * b_im
    bb_im = f_re[:, :, None] * b_im + f_im[:, :, None] * b_re
    eye = jnp.eye(n_g, dtype=F32)
    w_re = jnp.einsum('gpc,gh->gchp', bb_re, eye).reshape(n_g * n_c, n_state)
    w_im = jnp.einsum('gpc,gh->gchp', bb_im, eye).reshape(n_g * n_c, n_state)
    o_re = jnp.einsum('gcp,gh->gphc', c_re, eye).reshape(n_state, n_g * n_c)
    o_im = jnp.einsum('gcp,gh->gphc', c_im, eye).reshape(n_state, n_g * n_c)
    n_kb = (n_g * n_c) // LANES
    sp = n_state // n_kb
    w_in = jnp.stack([jnp.concatenate([w_re[kb * LANES:(kb + 1) * LANES, kb * sp:(kb + 1) * sp],
                                       w_im[kb * LANES:(kb + 1) * LANES, kb * sp:(kb + 1) * sp]], axis=1)
                      for kb in range(n_kb)])
    w_out = jnp.stack([jnp.concatenate([o_re[kb * sp:(kb + 1) * sp, kb * LANES:(kb + 1) * LANES],
                                        -o_im[kb * sp:(kb + 1) * sp, kb * LANES:(kb + 1) * LANES]], axis=0)
                       for kb in range(n_kb)])
    j = jnp.arange(SUBLANES, dtype=F32)[:, None]
    flat = lambda t: jnp.broadcast_to(t.reshape(1, n_state), (SUBLANES, n_state))
    coef = []
    for sh in (1, 2, 4):
        pr, pim = a_pow(float(sh))
        keep = (j >= sh).astype(F32)
        coef += [flat(pr) * keep, flat(pim) * keep]
    a_re_f = jnp.broadcast_to(a_re.reshape(1, n_state), (SUBLANES, n_state))
    a_im_f = jnp.broadcast_to(a_im.reshape(1, n_state), (SUBLANES, n_state))
    dt_f = jnp.broadcast_to(jnp.repeat(dt[:, 0], n_p).reshape(1, n_state), (SUBLANES, n_state))
    mag = jnp.exp((j + 1.0) * dt_f * a_re_f)
    ph = (j + 1.0) * dt_f * a_im_f
    coef += [mag * jnp.cos(ph), mag * jnp.sin(ph)]
    coef = jnp.stack(coef)
    return w_in.astype(BF16), w_out.astype(BF16), coef, d_skip.reshape(1, n_g * n_c).astype(F32)


def _gelu_tanh(x):
    return 0.5 * x * (1.0 + jnp.tanh(math.sqrt(2.0 / math.pi) * (x + 0.044715 * (x * x * x))))


def _ssm_kernel(u_ref, win_ref, wout_ref, coef_ref, d_ref, wglu_ref, bglu_ref, o_ref,
                hre_ref, him_ref, cre_ref, cim_ref, *, rows, n_state, col_w):
    n_kb = win_ref.shape[0]
    sp = n_state // n_kb

    @pl.when(pl.program_id(1) == 0)
    def _():
        cre_ref[...] = jnp.zeros_like(cre_ref)
        cim_ref[...] = jnp.zeros_like(cim_ref)

    u = u_ref[0]
    ub = u.astype(BF16)
    for kb in range(n_kb):
        bu = _dot(ub[:, kb * LANES:(kb + 1) * LANES], win_ref[kb])
        hre_ref[:, kb * sp:(kb + 1) * sp] = bu[:, :sp]
        him_ref[:, kb * sp:(kb + 1) * sp] = bu[:, sp:]

    n_slab = rows // SUBLANES
    for c0 in range(0, n_state, col_w):
        cs = slice(c0, c0 + col_w)

        def body(i, carry, cs=cs):
            c_re, c_im = carry
            sl = pl.ds(pl.multiple_of(i * SUBLANES, SUBLANES), SUBLANES)
            x_re = hre_ref[sl, cs]
            x_im = him_ref[sl, cs]
            for si, sh in enumerate((1, 2, 4)):
                p_re = coef_ref[2 * si, :, cs]
                p_im = coef_ref[2 * si + 1, :, cs]
                s_re = pltpu.roll(x_re, sh, axis=0)
                s_im = pltpu.roll(x_im, sh, axis=0)
                x_re, x_im = (x_re + p_re * s_re - p_im * s_im,
                              x_im + p_re * s_im + p_im * s_re)
            p_re = coef_ref[6, :, cs]
            p_im = coef_ref[7, :, cs]
            h_re = x_re + p_re * c_re - p_im * c_im
            h_im = x_im + p_re * c_im + p_im * c_re
            hre_ref[sl, cs] = h_re
            him_ref[sl, cs] = h_im
            last = slice(SUBLANES - 1, SUBLANES)
            return (jnp.broadcast_to(h_re[last, :], h_re.shape),
                    jnp.broadcast_to(h_im[last, :], h_im.shape))

        c_re, c_im = lax.fori_loop(0, n_slab, body, (cre_ref[:, cs], cim_ref[:, cs]))
        cre_ref[:, cs] = c_re
        cim_ref[:, cs] = c_im

    ys = []
    for kb in range(n_kb):
        hcat = jnp.concatenate([hre_ref[:, kb * sp:(kb + 1) * sp].astype(BF16),
                                him_ref[:, kb * sp:(kb + 1) * sp].astype(BF16)], axis=1)
        ys.append(_dot(hcat, wout_ref[kb]))
    y = jnp.concatenate(ys, axis=1) + d_ref[...] * u
    y = _gelu_tanh(y)
    z = _dot(y.astype(BF16), wglu_ref[...]) + bglu_ref[...]
    o_ref[0] = y * jax.nn.sigmoid(z)


def _ssm(u, w_in, w_out, coef, d_flat, w_glu, b_glu):
    bsz, seq, ssm_w = u.shape
    n_state = coef.shape[-1]
    rows = min(SSM_ROWS, seq)
    full = lambda *shape: pl.BlockSpec(shape, lambda b, i: (0,) * len(shape))
    kern = functools.partial(_ssm_kernel, rows=rows, n_state=n_state, col_w=4 * LANES)
    return pl.pallas_call(
        kern,
        grid=(bsz, seq // rows),
        in_specs=[pl.BlockSpec((1, rows, ssm_w), lambda b, i: (b, i, 0)),
                  full(*w_in.shape), full(*w_out.shape), full(*coef.shape), full(1, ssm_w),
                  full(ssm_w, ssm_w), full(1, ssm_w)],
        out_specs=pl.BlockSpec((1, rows, ssm_w), lambda b, i: (b, i, 0)),
        out_shape=jax.ShapeDtypeStruct((bsz, seq, ssm_w), F32),
        scratch_shapes=[pltpu.VMEM((rows, n_state), F32), pltpu.VMEM((rows, n_state), F32),
                        pltpu.VMEM((SUBLANES, n_state), F32), pltpu.VMEM((SUBLANES, n_state), F32)],
        compiler_params=_cparams(("parallel", "arbitrary")),
        name="s5_glu",
    )(u, w_in, w_out, coef, d_flat, w_glu.astype(BF16), b_glu.reshape(1, ssm_w))


def _layer_norm(y, g, b):
    mu = jnp.mean(y, axis=-1, keepdims=True)
    yc = y - mu
    var = jnp.mean(yc * yc, axis=-1, keepdims=True)
    return yc * lax.rsqrt(var + EPS) * g + b


def _rms_norm(y, g):
    return y * lax.rsqrt(jnp.mean(y * y, axis=-1, keepdims=True) + EPS) * g


def _mix_kernel(x_ref, att_ref, ssm_ref, gatt_ref, gssm_ref, wout_ref, g1_ref, ln_g_ref, ln_b_ref,
                sc_ref, sh_ref, g2_ref, wsg_ref, wsu_ref, wsd_ref, h_ref, ht_ref, pre_ref, *, alpha, att_w):
    a_n = _rms_norm(att_ref[0], gatt_ref[...]).astype(BF16)
    s_n = _rms_norm(ssm_ref[0], gssm_ref[...]).astype(BF16)
    mix = _dot(a_n, wout_ref[0:att_w, :]) + _dot(s_n, wout_ref[att_w:, :])
    x1 = _layer_norm(alpha * x_ref[0] + g1_ref[0] * mix, ln_g_ref[...], ln_b_ref[...])
    h = x1 * (1.0 + sc_ref[0]) + sh_ref[0]
    h_ref[0] = h
    _store_token_tiles(ht_ref, h)
    hb = h.astype(BF16)
    hid = _silu(_dot(hb, wsg_ref[...])) * _dot(hb, wsu_ref[...])
    shared = _dot(hid.astype(BF16), wsd_ref[...])
    pre_ref[0] = alpha * x1 + g2_ref[0] * shared


def _mix(x, att, ssm, g_att, g_ssm, w_out, gate1, ln_g, ln_b, scale2, shift2, gate2,
         w_s_gate, w_s_up, w_s_down, alpha):
    bsz, seq, d = x.shape
    att_w = att.shape[-1]
    ssm_w = ssm.shape[-1]
    ff = w_s_gate.shape[1]
    tm = min(MIX_ROWS, seq)
    n_sub = d // LANES
    row = lambda b, i: (b, i, 0)
    per_b = lambda b, i: (b, 0, 0)
    full = lambda *shape: pl.BlockSpec(shape, lambda b, i: (0,) * len(shape))
    vec = lambda t: t.reshape(bsz, 1, d)
    return pl.pallas_call(
        functools.partial(_mix_kernel, alpha=alpha, att_w=att_w),
        grid=(bsz, seq // tm),
        in_specs=[pl.BlockSpec((1, tm, d), row), pl.BlockSpec((1, tm, att_w), row),
                  pl.BlockSpec((1, tm, ssm_w), row),
                  full(1, att_w), full(1, ssm_w), full(att_w + ssm_w, d),
                  pl.BlockSpec((1, 1, d), per_b), full(1, d), full(1, d),
                  pl.BlockSpec((1, 1, d), per_b), pl.BlockSpec((1, 1, d), per_b),
                  pl.BlockSpec((1, 1, d), per_b),
                  full(d, ff), full(d, ff), full(ff, d)],
        out_specs=[pl.BlockSpec((1, tm, d), row),
                   pl.BlockSpec((tm * n_sub, LANES), lambda b, i: (b * (seq // tm) + i, 0)),
                   pl.BlockSpec((1, tm, d), row)],
        out_shape=[jax.ShapeDtypeStruct((bsz, seq, d), F32),
                   jax.ShapeDtypeStruct((bsz * seq * n_sub, LANES), F32),
                   jax.ShapeDtypeStruct((bsz, seq, d), F32)],
        compiler_params=_cparams(("parallel", "parallel")),
        name="mix_ln1_shared",
    )(x, att, ssm, g_att.reshape(1, att_w), g_ssm.reshape(1, ssm_w), w_out.astype(BF16),
      vec(gate1), ln_g.reshape(1, d), ln_b.reshape(1, d), vec(scale2), vec(shift2), vec(gate2),
      w_s_gate.astype(BF16), w_s_up.astype(BF16), w_s_down.astype(BF16))


def _route_kernel(h_ref, wrt_ref, wrt_lo_ref, bias_ref, tri_ref, idx_ref, gate_ref, rank_ref, cnt_ref, carry_ref):
    n_e = wrt_ref.shape[0]
    tm = h_ref.shape[0]
    per_group = n_e // N_EXPERT_GROUPS
    neg = -jnp.inf

    @pl.when(pl.program_id(0) == 0)
    def _():
        carry_ref[...] = jnp.zeros_like(carry_ref)

    h_hi, h_lo = _split_bf16(h_ref[...])
    logits = (_dot_nt(wrt_ref[...], h_hi) + _dot_nt(wrt_ref[...], h_lo)
              + _dot_nt(wrt_lo_ref[...], h_hi))
    scores = jax.nn.sigmoid(logits)
    biased = scores + bias_ref[...]

    g3 = biased.reshape(N_EXPERT_GROUPS, per_group, tm)
    ridx = lax.broadcasted_iota(jnp.int32, g3.shape, 1).astype(F32)
    m1 = jnp.max(g3, axis=1, keepdims=True)
    first = jnp.min(jnp.where(g3 == m1, ridx, float(per_group)), axis=1, keepdims=True)
    m2 = jnp.max(jnp.where(ridx == first, neg, g3), axis=1, keepdims=True)
    gs = m1 + m2

    gidx = lax.broadcasted_iota(jnp.int32, gs.shape, 0).astype(F32)
    ok = jnp.zeros_like(gs)
    cur = gs
    for _ in range(TOPK_GROUPS):
        mx = jnp.max(cur, axis=0, keepdims=True)
        fi = jnp.min(jnp.where(cur == mx, gidx, float(N_EXPERT_GROUPS)), axis=0, keepdims=True)
        hit = gidx == fi
        ok = jnp.where(hit, 1.0, ok)
        cur = jnp.where(hit, neg, cur)
    masked = jnp.where(ok > 0.5, g3, neg).reshape(n_e, tm)

    eidx = lax.broadcasted_iota(jnp.int32, (n_e, tm), 0).astype(F32)
    onehot = jnp.zeros((n_e, tm), F32)
    cur = masked
    sel_idx = []
    sel_gate = []
    for _ in range(TOP_K):
        mx = jnp.max(cur, axis=0, keepdims=True)
        fi = jnp.min(jnp.where(cur == mx, eidx, float(n_e)), axis=0, keepdims=True)
        hit = eidx == fi
        sel_idx.append(fi)
        sel_gate.append(jnp.sum(jnp.where(hit, scores, 0.0), axis=0, keepdims=True))
        onehot = jnp.where(hit, 1.0, onehot)
        cur = jnp.where(hit, neg, cur)
    idx = jnp.concatenate(sel_idx, axis=0)
    gate = jnp.concatenate(sel_gate, axis=0)
    gate = gate / jnp.sum(gate, axis=0, keepdims=True) * ROUTED_SCALE

    prior = _dot(onehot.astype(BF16), tri_ref[...]) + carry_ref[:, 0:1]
    ranks = [jnp.sum(jnp.where(eidx == sel_idx[k], prior, 0.0), axis=0, keepdims=True)
             for k in range(TOP_K)]
    rank = jnp.concatenate(ranks, axis=0)
    carry = carry_ref[...] + jnp.sum(onehot, axis=1, keepdims=True)
    carry_ref[...] = carry
    cnt_ref[...] = carry

    idx_ref[...] = idx.astype(jnp.int32)
    gate_ref[...] = gate
    rank_ref[...] = rank.astype(jnp.int32)


def _route(h2d, w_router, router_bias):
    n_tok, d = h2d.shape
    n_e = w_router.shape[1]
    tm = min(ROUTE_COLS, n_tok)
    tri = (np.arange(tm)[:, None] < np.arange(tm)[None, :]).astype(np.float32)
    w_hi, w_lo = _split_bf16(w_router.T.astype(F32))
    col = lambda i: (0, i)
    full = lambda *shape: pl.BlockSpec(shape, lambda i: (0,) * len(shape))
    idx, gate, rank, cnt = pl.pallas_call(
        _route_kernel,
        grid=(n_tok // tm,),
        in_specs=[pl.BlockSpec((tm, d), lambda i: (i, 0)), full(n_e, d), full(n_e, d), full(n_e, 1),
                  full(tm, tm)],
        out_specs=[pl.BlockSpec((TOP_K, tm), col), pl.BlockSpec((TOP_K, tm), col),
                   pl.BlockSpec((TOP_K, tm), col), full(n_e, LANES)],
        out_shape=[jax.ShapeDtypeStruct((TOP_K, n_tok), jnp.int32),
                   jax.ShapeDtypeStruct((TOP_K, n_tok), F32),
                   jax.ShapeDtypeStruct((TOP_K, n_tok), jnp.int32),
                   jax.ShapeDtypeStruct((n_e, LANES), F32)],
        scratch_shapes=[pltpu.VMEM((n_e, LANES), F32)],
        compiler_params=_cparams(("arbitrary",)),
        name="router_topk",
    )(h2d, w_hi, w_lo, router_bias.reshape(n_e, 1).astype(F32), jnp.asarray(tri, BF16))
    return idx, gate, rank, cnt[:, 0].astype(jnp.int32)


def _dispatch_kernel(pend_ref, cnt_ref, dest_ref, ht_ref, xs_ref, zero_ref, sem_ref, *, n_e, blk, tm):

    def zero_copy(e):
        start = pl.multiple_of(pend_ref[e] - blk, blk)
        return pltpu.make_async_copy(zero_ref, xs_ref.at[pl.ds(start, blk)], sem_ref.at[0])

    @pl.when(pl.program_id(0) == 0)
    def _():
        zero_ref[...] = jnp.zeros_like(zero_ref)

        def start(e, c):
            @pl.when(cnt_ref[e] > 0)
            def _():
                zero_copy(e).start()
            return c
        lax.fori_loop(0, n_e, start, 0)

        def wait(e, c):
            @pl.when(cnt_ref[e] > 0)
            def _():
                zero_copy(e).wait()
            return c
        lax.fori_loop(0, n_e, wait, 0)

    n_sub = ht_ref.shape[0] // tm

    def row_copy(t, dst_row):
        src = ht_ref.at[pl.ds(pl.multiple_of(t * n_sub, n_sub), n_sub)]
        return pltpu.make_async_copy(src, xs_ref.at[dst_row], sem_ref.at[1])

    def issue(g, c):
        t0 = g * DMA_ISSUE_UNROLL
        rows = [[dest_ref[(t0 + j) * TOP_K + k] for k in range(TOP_K)] for j in range(DMA_ISSUE_UNROLL)]
        for j in range(DMA_ISSUE_UNROLL):
            for k in range(TOP_K):
                row_copy(t0 + j, rows[j][k]).start(priority=k % 2)
        return c
    lax.fori_loop(0, tm // DMA_ISSUE_UNROLL, issue, 0)

    def drain(t, c):
        for k in range(TOP_K):
            row_copy(t, dest_ref[t * TOP_K + k]).wait()
        return c
    lax.fori_loop(0, tm, drain, 0)


def _dispatch(h_tiles, n_sub, dest, pend, cnt, n_rows):
    n_tok = h_tiles.shape[0] // n_sub
    n_e = cnt.shape[0]
    tm = min(DISPATCH_ROWS, n_tok)
    grid_spec = pltpu.PrefetchScalarGridSpec(
        num_scalar_prefetch=2,
        grid=(n_tok // tm,),
        in_specs=[pl.BlockSpec((TOP_K * tm,), lambda i, *_: (i,), memory_space=pltpu.SMEM),
                  pl.BlockSpec((tm * n_sub, LANES), lambda i, *_: (i, 0))],
        out_specs=pl.BlockSpec(memory_space=pl.ANY),
        scratch_shapes=[pltpu.VMEM((MOE_ROWS, n_sub, LANES), F32), pltpu.SemaphoreType.DMA((2,))],
    )
    return pl.pallas_call(
        functools.partial(_dispatch_kernel, n_e=n_e, blk=MOE_ROWS, tm=tm),
        grid_spec=grid_spec,
        out_shape=jax.ShapeDtypeStruct((n_rows, n_sub, LANES), F32),
        compiler_params=_cparams(("arbitrary",)),
        name="moe_dispatch",
    )(pend, cnt, dest, h_tiles)


def _expert_kernel(be_ref, nused_ref, next_ref, slot_ref, x_ref, wg_ref, wu_ref, wd_ref, y_ref,
                   wgf_ref, wuf_ref, wdf_ref, wgb_ref, wub_ref, wdb_ref, sem_ref):
    i = pl.program_id(0)
    active = i < nused_ref[0]
    prev = be_ref[jnp.maximum(i - 1, 0)]
    fresh = jnp.logical_and(active, jnp.logical_or(i == 0, be_ref[i] != prev))

    def weight_copies(e, slot):
        return [pltpu.make_async_copy(src.at[e], dst.at[slot], sem_ref.at[slot])
                for src, dst in ((wg_ref, wgf_ref), (wu_ref, wuf_ref), (wd_ref, wdf_ref))]

    @pl.when(jnp.logical_and(active, i == 0))
    def _():
        for cp in weight_copies(be_ref[0], 0):
            cp.start()

    @pl.when(fresh)
    def _():
        slot = slot_ref[i]
        for cp in weight_copies(be_ref[i], slot):
            cp.wait()
        wgb_ref[...] = wgf_ref[slot].astype(BF16)
        wub_ref[...] = wuf_ref[slot].astype(BF16)
        wdb_ref[...] = wdf_ref[slot].astype(BF16)

        @pl.when(next_ref[i] >= 0)
        def _():
            for cp in weight_copies(next_ref[i], 1 - slot):
                cp.start()

    @pl.when(active)
    def _():
        d = wgb_ref.shape[0]
        xb = jnp.concatenate(_load_token_tiles(x_ref, MOE_ROWS, d), axis=1).astype(BF16)
        hid = _silu(_dot(xb, wgb_ref[...])) * _dot(xb, wub_ref[...])
        _store_token_tiles(y_ref, _dot(hid.astype(BF16), wdb_ref[...]))


def _experts(xs_tiles, plan, w_gate, w_up, w_down):
    n_rows, n_sub, _ = xs_tiles.shape
    d = n_sub * LANES
    n_blk = n_rows // MOE_ROWS
    ff = w_gate.shape[-1]
    rows_spec = pl.BlockSpec((MOE_ROWS * n_sub, LANES), lambda i, be, nu, *_: (jnp.minimum(i, nu[0] - 1), 0))
    hbm = pl.BlockSpec(memory_space=pl.ANY)
    grid_spec = pltpu.PrefetchScalarGridSpec(
        num_scalar_prefetch=4,
        grid=(n_blk,),
        in_specs=[rows_spec, hbm, hbm, hbm],
        out_specs=rows_spec,
        scratch_shapes=[pltpu.VMEM((2, d, ff), F32), pltpu.VMEM((2, d, ff), F32), pltpu.VMEM((2, ff, d), F32),
                        pltpu.VMEM((d, ff), BF16), pltpu.VMEM((d, ff), BF16), pltpu.VMEM((ff, d), BF16),
                        pltpu.SemaphoreType.DMA((2,))],
    )
    ys = pl.pallas_call(
        _expert_kernel,
        grid_spec=grid_spec,
        out_shape=jax.ShapeDtypeStruct((n_rows * n_sub, LANES), F32),
        compiler_params=_cparams(("arbitrary",)),
        name="moe_experts",
    )(*plan, xs_tiles.reshape(n_rows * n_sub, LANES), w_gate, w_up, w_down)
    return ys.reshape(n_rows, n_sub, LANES)


def _combine_kernel(dest_ref, ys_ref, gate_ref, pre_ref, g2_ref, ln_g_ref, ln_b_ref, o_ref, buf_ref, sem_ref):
    tm, d = pre_ref.shape[1], pre_ref.shape[2]
    n_sub = d // LANES

    def row_copy(t, k, src_row):
        slot = pl.multiple_of((k * tm + t) * n_sub, n_sub)
        return pltpu.make_async_copy(ys_ref.at[src_row], buf_ref.at[pl.ds(slot, n_sub)], sem_ref.at[0])

    def issue(g, c):
        t0 = g * DMA_ISSUE_UNROLL
        rows = [[dest_ref[(t0 + j) * TOP_K + k] for k in range(TOP_K)] for j in range(DMA_ISSUE_UNROLL)]
        for j in range(DMA_ISSUE_UNROLL):
            for k in range(TOP_K):
                row_copy(t0 + j, k, rows[j][k]).start(priority=k % 2)
        return c
    lax.fori_loop(0, tm // DMA_ISSUE_UNROLL, issue, 0)

    def drain(t, c):
        for k in range(TOP_K):
            row_copy(t, k, dest_ref[t * TOP_K + k]).wait()
        return c
    lax.fori_loop(0, tm, drain, 0)

    gates = gate_ref[...]
    gate_cols = [jnp.broadcast_to(gates[:, k:k + 1], (tm, LANES)) for k in range(TOP_K)]
    pieces = []
    for s in range(n_sub):
        acc = None
        for k in range(TOP_K):
            term = buf_ref[pl.ds(k * tm * n_sub + s, tm, stride=n_sub), :] * gate_cols[k]
            acc = term if acc is None else acc + term
        pieces.append(acc)
    routed = jnp.concatenate(pieces, axis=1)
    y = pre_ref[0] + g2_ref[0] * routed
    o_ref[0] = _layer_norm(y, ln_g_ref[...], ln_b_ref[...])


def _combine(ys, dest, gates_t, pre, gate2, ln_g, ln_b):
    bsz, seq, d = pre.shape
    tm = min(DISPATCH_ROWS, seq)
    per_seq = seq // tm
    row = lambda i, *_: (i // per_seq, i % per_seq, 0)
    per_b = lambda i, *_: (i // per_seq, 0, 0)
    full = lambda *shape: pl.BlockSpec(shape, lambda i, *_: (0,) * len(shape))
    grid_spec = pltpu.PrefetchScalarGridSpec(
        num_scalar_prefetch=0,
        grid=(bsz * per_seq,),
        in_specs=[pl.BlockSpec((TOP_K * tm,), lambda i: (i,), memory_space=pltpu.SMEM),
                  pl.BlockSpec(memory_space=pl.ANY),
                  pl.BlockSpec((tm, TOP_K), lambda i: (i, 0)),
                  pl.BlockSpec((1, tm, d), row), pl.BlockSpec((1, 1, d), per_b), full(1, d), full(1, d)],
        out_specs=pl.BlockSpec((1, tm, d), row),
        scratch_shapes=[pltpu.VMEM((TOP_K * tm * (d // LANES), LANES), F32), pltpu.SemaphoreType.DMA((1,))],
    )
    return pl.pallas_call(
        _combine_kernel,
        grid_spec=grid_spec,
        out_shape=jax.ShapeDtypeStruct((bsz, seq, d), F32),
        compiler_params=_cparams(("arbitrary",)),
        name="moe_combine_ln2",
    )(dest, ys, gates_t, pre, gate2.reshape(bsz, 1, d), ln_g.reshape(1, d), ln_b.reshape(1, d))


def _dest_kernel(pstart_ref, idx_ref, rank_ref, dest_ref):
    idx = idx_ref[...]

    def body(e, acc):
        return acc + jnp.where(idx == e, pstart_ref[e], 0)
    dest_ref[...] = lax.fori_loop(0, pstart_ref.shape[0], body, rank_ref[...], unroll=8)


def _dest_rows(pstart, idx, rank):
    n_k, n_tok = idx.shape
    tn = min(4096, n_tok)
    grid_spec = pltpu.PrefetchScalarGridSpec(
        num_scalar_prefetch=1,
        grid=(n_tok // tn,),
        in_specs=[pl.BlockSpec((n_k, tn), lambda i, *_: (0, i))] * 2,
        out_specs=pl.BlockSpec((n_k, tn), lambda i, *_: (0, i)),
    )
    return pl.pallas_call(
        _dest_kernel,
        grid_spec=grid_spec,
        out_shape=jax.ShapeDtypeStruct((n_k, n_tok), jnp.int32),
        compiler_params=_cparams(("arbitrary",)),
        name="moe_dest",
    )(pstart, idx, rank)


def _moe_plan(idx, rank, cnt, n_tok):
    n_e = cnt.shape[0]
    padded = (cnt + MOE_ROWS - 1) // MOE_ROWS * MOE_ROWS
    pend = jnp.cumsum(padded).astype(jnp.int32)
    pstart = pend - padded
    dest = _dest_rows(pstart, idx, rank).T.reshape(-1)
    n_blk = n_tok * TOP_K // MOE_ROWS + n_e
    starts = jnp.arange(n_blk, dtype=jnp.int32) * MOE_ROWS
    block_expert = jnp.minimum(jnp.sum((pend[None, :] <= starts[:, None]).astype(jnp.int32), axis=1),
                               n_e - 1).astype(jnp.int32)
    n_used = (pend[-1:] // MOE_ROWS).astype(jnp.int32)
    ids = jnp.arange(n_e, dtype=jnp.int32)
    later = jnp.where((cnt[None, :] > 0) & (ids[None, :] > ids[:, None]), ids[None, :], n_e)
    nxt = jnp.min(later, axis=1)
    nxt = jnp.where(nxt == n_e, -1, nxt).astype(jnp.int32)
    parity = ((jnp.cumsum((cnt > 0).astype(jnp.int32)) - 1) & 1).astype(jnp.int32)
    plan = (block_expert, n_used, jnp.take(nxt, block_expert), jnp.take(parity, block_expert))
    return dest, pend, plan, n_blk * MOE_ROWS


def kernel(x, c, rel_bias, w_ada, b_ada, w_in, ssm_a_re, ssm_a_im, ssm_b_re, ssm_b_im, ssm_c_re, ssm_c_im, ssm_d, ssm_log_dt, w_glu, b_glu, g_att, g_ssm, w_out, ln1_g, ln1_b, w_router, router_bias, w_e_gate, w_e_up, w_e_down, w_s_gate, w_s_up, w_s_down, ln2_g, ln2_b):
    bsz, seq, d = x.shape
    depth = w_ada.shape[0]
    alpha = (2 * depth) ** 0.25
    att_w = g_att.shape[-1]
    tables = _att_tables(rel_bias)
    for layer in range(depth):
        ada = _ada(c, w_ada[layer], b_ada[layer])
        shift1, scale1, gate1, shift2, scale2, gate2 = jnp.split(ada, 6, axis=-1)
        q, k, v, u = _inproj(x, scale1, shift1, w_in[layer], att_w)
        att = _attention(q, k, v, tables)
        ssm_prm = _ssm_params(ssm_a_re[layer], ssm_a_im[layer], ssm_b_re[layer], ssm_b_im[layer],
                              ssm_c_re[layer], ssm_c_im[layer], ssm_d[layer], ssm_log_dt[layer])
        ssm = _ssm(u, *ssm_prm, w_glu[layer], b_glu[layer])
        h2, h2_tiles, pre = _mix(x, att, ssm, g_att[layer], g_ssm[layer], w_out[layer], gate1,
                                 ln1_g[layer], ln1_b[layer], scale2, shift2, gate2, w_s_gate[layer],
                                 w_s_up[layer], w_s_down[layer], alpha)
        h2d = h2.reshape(bsz * seq, d)
        idx, gate, rank, cnt = _route(h2d, w_router[layer], router_bias[layer])
        dest, pend, plan, n_rows = _moe_plan(idx, rank, cnt, bsz * seq)
        xs = _dispatch(h2_tiles, d // LANES, dest, pend, cnt, n_rows)
        ys = _experts(xs, plan, w_e_gate[layer], w_e_up[layer], w_e_down[layer])
        x = _combine(ys, dest, gate.T, pre, gate2, ln2_g[layer], ln2_b[layer])
    return x
```

```python
import functools
import math

import jax
import jax.numpy as jnp
import numpy as np
from jax import lax
from jax.experimental import pallas as pl
from jax.experimental.pallas import tpu as pltpu

F32 = jnp.float32
BF16 = jnp.bfloat16

HEAD_DIM = 64
ATT_BLOCK = 128
PATTERNS = ((128, 1), (512, 4), (2048, 16))
N_BUCKETS = 32
MAX_DISTANCE = 2048
SSM_GROUP = 16
SSM_STATE = 64
N_EXPERTS = 256
TOP_K = 8
N_EXPERT_GROUPS = 8
TOPK_GROUPS = 4
ROUTED_SCALE = 2.5
EPS = 1e-5
NEG_INF = -1e30

LANES = 128
SUBLANES = 8
VMEM_LIMIT_BYTES = 56 * 1024 * 1024

PROJ_ROWS = 512
SSM_ROWS = 256
MIX_ROWS = 512
ROUTE_COLS = 512
MOE_ROWS = 256
DISPATCH_ROWS = 256
ATT_UNITS_PER_STEP = 4
DMA_ISSUE_UNROLL = 4


def _cparams(sem, vmem=VMEM_LIMIT_BYTES):
    return pltpu.CompilerParams(dimension_semantics=sem, vmem_limit_bytes=vmem)


def _dot(a, b):
    return jnp.dot(a, b, preferred_element_type=F32)


def _dot_nt(a, b):
    return lax.dot_general(a, b, (((1,), (1,)), ((), ())), preferred_element_type=F32)


def _silu(x):
    return x * jax.nn.sigmoid(x)


U32 = jnp.uint32
HIGH_HALF = 0xFFFF0000


def _slab_rows(d):
    return d // (2 * LANES)


def _pack_pair(a, b):
    lo = lax.bitcast_convert_type(a.astype(BF16).astype(F32), U32)
    hi = lax.bitcast_convert_type(b.astype(BF16).astype(F32), U32)
    return (lo >> 16) | (hi & U32(HIGH_HALF))


def _unpack_pair(w):
    return (lax.bitcast_convert_type(w << 16, F32), lax.bitcast_convert_type(w & U32(HIGH_HALF), F32))


def _store_token_slabs(ref, x):
    m, d = x.shape
    n_sub = _slab_rows(d)
    half = d // 2
    for s in range(n_sub):
        ref[pl.ds(s, m, stride=n_sub), :] = _pack_pair(x[:, s * LANES:(s + 1) * LANES],
                                                        x[:, half + s * LANES:half + (s + 1) * LANES])


def _load_token_slabs(ref, m, d, row0=0):
    n_sub = _slab_rows(d)
    pairs = [_unpack_pair(ref[pl.ds(row0 + s, m, stride=n_sub), :]) for s in range(n_sub)]
    return [p[0] for p in pairs], [p[1] for p in pairs]


def _split_bf16(x):
    hi = x.astype(BF16)
    lo = (x - hi.astype(F32)).astype(BF16)
    return hi, lo


def _ada_kernel(c_ref, w_ref, b_ref, o_ref):
    c = c_ref[...]
    a_hi, a_lo = _split_bf16(_silu(c))
    w_hi, w_lo = _split_bf16(w_ref[...])
    acc = _dot(a_hi, w_hi) + _dot(a_hi, w_lo) + _dot(a_lo, w_hi)
    o_ref[...] = acc + b_ref[...]


def _ada(c, w_ada, b_ada):
    bsz, d = c.shape
    n = w_ada.shape[1]
    rows = SUBLANES
    c_pad = jnp.zeros((rows, d), F32).at[:bsz].set(c)
    tn = 1024
    out = pl.pallas_call(
        _ada_kernel,
        grid=(n // tn,),
        in_specs=[pl.BlockSpec((rows, d), lambda j: (0, 0)),
                  pl.BlockSpec((d, tn), lambda j: (0, j)),
                  pl.BlockSpec((1, tn), lambda j: (0, j))],
        out_specs=pl.BlockSpec((rows, tn), lambda j: (0, j)),
        out_shape=jax.ShapeDtypeStruct((rows, n), F32),
        compiler_params=_cparams(("parallel",)),
        name="ada",
    )(c_pad, w_ada, b_ada.reshape(1, n))
    return out[:bsz]


def _inproj_kernel(x_ref, sc_ref, sh_ref, w_ref, q_ref, k_ref, v_ref, u_ref, *, att_w, q_scale):
    h = (x_ref[0] * (1.0 + sc_ref[0]) + sh_ref[0]).astype(BF16)
    q_ref[0] = _dot(h, w_ref[:, 0:att_w]) * q_scale
    k_ref[0] = _dot(h, w_ref[:, att_w:2 * att_w])
    v_ref[0] = _dot(h, w_ref[:, 2 * att_w:3 * att_w])
    u_ref[0] = _dot(h, w_ref[:, 3 * att_w:])


def _inproj(x, scale, shift, w_in, att_w):
    bsz, seq, d = x.shape
    n = w_in.shape[1]
    ssm_w = n - 3 * att_w
    ts = min(PROJ_ROWS, seq)
    row = lambda b, i: (b, i, 0)
    per_b = lambda b, i: (b, 0, 0)
    kern = functools.partial(_inproj_kernel, att_w=att_w, q_scale=HEAD_DIM ** -0.5)
    return pl.pallas_call(
        kern,
        grid=(bsz, seq // ts),
        in_specs=[pl.BlockSpec((1, ts, d), row),
                  pl.BlockSpec((1, 1, d), per_b),
                  pl.BlockSpec((1, 1, d), per_b),
                  pl.BlockSpec((d, n), lambda b, i: (0, 0))],
        out_specs=[pl.BlockSpec((1, ts, att_w), row)] * 3 + [pl.BlockSpec((1, ts, ssm_w), row)],
        out_shape=[jax.ShapeDtypeStruct((bsz, seq, att_w), F32)] * 3
                  + [jax.ShapeDtypeStruct((bsz, seq, ssm_w), F32)],
        compiler_params=_cparams(("parallel", "parallel")),
        name="inproj",
    )(x, scale.reshape(bsz, 1, d), shift.reshape(bsz, 1, d), w_in.astype(BF16))


def _t5_bucket_np(dist):
    exact = N_BUCKETS // 2
    large = exact + (np.log(np.maximum(dist, 1).astype(np.float64) / exact)
                     / math.log(MAX_DISTANCE / exact) * (N_BUCKETS - exact)).astype(np.int64)
    return np.where(dist < exact, dist, np.minimum(large, N_BUCKETS - 1))


def _att_tables(rel_bias):
    qi = np.arange(ATT_BLOCK)[:, None]
    ki = np.arange(2 * ATT_BLOCK)[None, :]
    rel = qi + ATT_BLOCK - ki
    tabs = []
    for window, dil in PATTERNS:
        band = (rel >= 0) & (rel <= window // dil)
        bucket = _t5_bucket_np(np.maximum(rel, 0) * dil)
        onehot = (bucket[:, :, None] == np.arange(N_BUCKETS)[None, None, :]).astype(np.float32)
        bias = jnp.einsum('qkb,bh->hqk', onehot, rel_bias.astype(F32),
                          precision=lax.Precision.HIGHEST)
        full = jnp.where(band[None], bias, NEG_INF)
        first = jnp.concatenate([full[:, :, ATT_BLOCK:], jnp.full_like(full[:, :, ATT_BLOCK:], NEG_INF)], axis=-1)
        tabs.append(jnp.stack([full, first]))
    return jnp.stack(tabs)


def _att_kernel(q_ref, k_ref, v_ref, tab_ref, o_ref, m_ref, l_ref, *, seq):
    lane = lax.broadcasted_iota(jnp.int32, (ATT_BLOCK, LANES), 1)
    head0 = lane < HEAD_DIM

    def rows(ref, start, n, dil):
        if dil == 1:
            return ref[0, pl.ds(pl.multiple_of(start, ATT_BLOCK), n), :]
        return ref[0, pl.ds(start, n, stride=dil), :]

    nb = ATT_UNITS_PER_STEP
    q_rows = nb * ATT_BLOCK

    def lane_mask(n_rows, h):
        m = lax.broadcasted_iota(jnp.int32, (n_rows, LANES), 1) < HEAD_DIM
        return m if h == 0 else ~m

    def batch(pi, dil, res, blk0, head, init):
        kv_blocks = nb if head else nb + 1
        q_start = res + dil * ATT_BLOCK * blk0
        kv_start = q_start if head else q_start - dil * ATT_BLOCK
        if dil == 1:
            sl = pl.ds(pl.multiple_of(q_start, ATT_BLOCK), q_rows)
        else:
            sl = pl.ds(q_start, q_rows, stride=dil)
        q = rows(q_ref, q_start, q_rows, dil)
        k = rows(k_ref, kv_start, kv_blocks * ATT_BLOCK, dil).astype(BF16)
        v = rows(v_ref, kv_start, kv_blocks * ATT_BLOCK, dil)
        if not init:
            m_old, l_old, o_old = m_ref[sl, :], l_ref[sl, :], o_ref[0, sl, :]
        qh = [jnp.where(lane_mask(q_rows, h), q, 0.0).astype(BF16) for h in range(2)]
        vh = [jnp.where(lane_mask(kv_blocks * ATT_BLOCK, h), v, 1.0).astype(BF16) for h in range(2)]
        o_parts, l_parts, m_parts = [], [], []
        for j in range(nb):
            first = 1 if (head and j == 0) else 0
            kb = j if (not head or j == 0) else j - 1
            ksl = slice(kb * ATT_BLOCK, (kb + 2) * ATT_BLOCK)
            qsl = slice(j * ATT_BLOCK, (j + 1) * ATT_BLOCK)
            outs, ms = [], []
            for h in range(2):
                s = _dot_nt(qh[h][qsl], k[ksl]) + tab_ref[pi, first, h]
                m = jnp.max(s, axis=1, keepdims=True)
                p = jnp.exp(s - m).astype(BF16)
                outs.append(_dot(p, vh[h][ksl]))
                ms.append(m)
            o_parts.append(jnp.where(head0, outs[0], outs[1]))
            l_parts.append(pltpu.roll(jnp.where(head0, outs[1], outs[0]), HEAD_DIM, axis=1))
            m_parts.append(jnp.where(head0, ms[0], ms[1]))
        o = jnp.concatenate(o_parts, axis=0)
        l = jnp.concatenate(l_parts, axis=0)
        m = jnp.concatenate(m_parts, axis=0)
        if not init:
            m_new = jnp.maximum(m_old, m)
            a_old = jnp.exp(m_old - m_new)
            a_cur = jnp.exp(m - m_new)
            o = o_old * a_old + o * a_cur
            l = l_old * a_old + l * a_cur
            m = m_new
        o_ref[0, sl, :] = o
        m_ref[sl, :] = m
        l_ref[sl, :] = l

    n_units = seq // ATT_BLOCK
    order = sorted(range(len(PATTERNS)), key=lambda p: -PATTERNS[p][1])
    for pos, pi in enumerate(order):
        dil = PATTERNS[pi][1]
        log_d = dil.bit_length() - 1
        per_res = seq // (dil * ATT_BLOCK) // nb
        init = pos == 0

        def head_body(res, carry, pi=pi, dil=dil, init=init):
            batch(pi, dil, res, 0, True, init)
            return carry
        lax.fori_loop(0, dil, head_body, 0)

        if per_res > 1:
            def tail_body(i, carry, pi=pi, dil=dil, log_d=log_d, init=init):
                res = i & (dil - 1)
                blk0 = nb * (1 + (i >> log_d))
                batch(pi, dil, res, blk0, False, init)
                return carry
            lax.fori_loop(0, dil * (per_res - 1), tail_body, 0)

    def finish(i, carry):
        sl = pl.ds(pl.multiple_of(i * ATT_BLOCK, ATT_BLOCK), ATT_BLOCK)
        o_ref[0, sl, :] = o_ref[0, sl, :] / l_ref[sl, :]
        return carry
    lax.fori_loop(0, n_units, finish, 0)


def _attention(q, k, v, tables):
    bsz, seq, att_w = q.shape
    n_pairs = att_w // LANES
    blk = lambda b, p: (b, 0, p)
    qkv_spec = pl.BlockSpec((1, seq, LANES), blk)
    n_pat = tables.shape[0]
    tab_spec = pl.BlockSpec((n_pat, 2, 2, ATT_BLOCK, 2 * ATT_BLOCK), lambda b, p: (0, 0, p, 0, 0))
    return pl.pallas_call(
        functools.partial(_att_kernel, seq=seq),
        grid=(bsz, n_pairs),
        in_specs=[qkv_spec, qkv_spec, qkv_spec, tab_spec],
        out_specs=pl.BlockSpec((1, seq, LANES), blk),
        out_shape=jax.ShapeDtypeStruct((bsz, seq, att_w), F32),
        scratch_shapes=[pltpu.VMEM((seq, LANES), F32), pltpu.VMEM((seq, LANES), F32)],
        compiler_params=_cparams(("parallel", "parallel")),
        name="dilated_attention",
    )(q, k, v, tables)


def _ssm_params(a_re, a_im, b_re, b_im, c_re, c_im, d_skip, log_dt):
    n_g, n_p = a_re.shape
    n_c = b_re.shape[-1]
    n_state = n_g * n_p
    dt = jnp.exp(log_dt.astype(F32))[:, None]

    def a_pow(kk):
        mag = jnp.exp(kk * dt * a_re)
        ph = kk * dt * a_im
        return mag * jnp.cos(ph), mag * jnp.sin(ph)

    ab_re, ab_im = a_pow(1.0)
    nr, ni = ab_re - 1.0, ab_im
    den = a_re * a_re + a_im * a_im
    f_re = (nr * a_re + ni * a_im) / den
    f_im = (ni * a_re - nr * a_im) / den
    bb_re = f_re[:, :, None] * b_re - f_im[:, :, None] * b_im
    bb_im = f_re[:, :, None] * b_im + f_im[:, :, None] * b_re
    eye = jnp.eye(n_g, dtype=F32)
    w_re = jnp.einsum('gpc,gh->gchp', bb_re, eye).reshape(n_g * n_c, n_state)
    w_im = jnp.einsum('gpc,gh->gchp', bb_im, eye).reshape(n_g * n_c, n_state)
    o_re = jnp.einsum('gcp,gh->gphc', c_re, eye).reshape(n_state, n_g * n_c)
    o_im = jnp.einsum('gcp,gh->gphc', c_im, eye).reshape(n_state, n_g * n_c)
    n_kb = (n_g * n_c) // LANES
    sp = n_state // n_kb
    w_in = jnp.stack([jnp.concatenate([w_re[kb * LANES:(kb + 1) * LANES, kb * sp:(kb + 1) * sp],
                                       w_im[kb * LANES:(kb + 1) * LANES, kb * sp:(kb + 1) * sp]], axis=1)
                      for kb in range(n_kb)])
    w_out = jnp.stack([jnp.concatenate([o_re[kb * sp:(kb + 1) * sp, kb * LANES:(kb + 1) * LANES],
                                        -o_im[kb * sp:(kb + 1) * sp, kb * LANES:(kb + 1) * LANES]], axis=0)
                       for kb in range(n_kb)])
    j = jnp.arange(SUBLANES, dtype=F32)[:, None]
    flat = lambda t: jnp.broadcast_to(t.reshape(1, n_state), (SUBLANES, n_state))
    coef = []
    for sh in (1, 2, 4):
        pr, pim = a_pow(float(sh))
        keep = (j >= sh).astype(F32)
        coef += [flat(pr) * keep, flat(pim) * keep]
    a_re_f = jnp.broadcast_to(a_re.reshape(1, n_state), (SUBLANES, n_state))
    a_im_f = jnp.broadcast_to(a_im.reshape(1, n_state), (SUBLANES, n_state))
    dt_f = jnp.broadcast_to(jnp.repeat(dt[:, 0], n_p).reshape(1, n_state), (SUBLANES, n_state))
    mag = jnp.exp((j + 1.0) * dt_f * a_re_f)
    ph = (j + 1.0) * dt_f * a_im_f
    coef += [mag * jnp.cos(ph), mag * jnp.sin(ph)]
    coef = jnp.stack(coef)
    return w_in.astype(BF16), w_out.astype(BF16), coef, d_skip.reshape(1, n_g * n_c).astype(F32)


def _gelu_tanh(x):
    return 0.5 * x * (1.0 + jnp.tanh(math.sqrt(2.0 / math.pi) * (x + 0.044715 * (x * x * x))))


def _ssm_kernel(u_ref, win_ref, wout_ref, coef_ref, d_ref, wglu_ref, bglu_ref, o_ref,
                hre_ref, him_ref, cre_ref, cim_ref, *, rows, n_state, col_w):
    n_kb = win_ref.shape[0]
    sp = n_state // n_kb

    @pl.when(pl.program_id(1) == 0)
    def _():
        cre_ref[...] = jnp.zeros_like(cre_ref)
        cim_ref[...] = jnp.zeros_like(cim_ref)

    u = u_ref[0]
    ub = u.astype(BF16)
    for kb in range(n_kb):
        bu = _dot(ub[:, kb * LANES:(kb + 1) * LANES], win_ref[kb])
        hre_ref[:, kb * sp:(kb + 1) * sp] = bu[:, :sp]
        him_ref[:, kb * sp:(kb + 1) * sp] = bu[:, sp:]

    n_slab = rows // SUBLANES
    for c0 in range(0, n_state, col_w):
        cs = slice(c0, c0 + col_w)

        def body(i, carry, cs=cs):
            c_re, c_im = carry
            sl = pl.ds(pl.multiple_of(i * SUBLANES, SUBLANES), SUBLANES)
            x_re = hre_ref[sl, cs]
            x_im = him_ref[sl, cs]
            for si, sh in enumerate((1, 2, 4)):
                p_re = coef_ref[2 * si, :, cs]
                p_im = coef_ref[2 * si + 1, :, cs]
                s_re = pltpu.roll(x_re, sh, axis=0)
                s_im = pltpu.roll(x_im, sh, axis=0)
                x_re, x_im = (x_re + p_re * s_re - p_im * s_im,
                              x_im + p_re * s_im + p_im * s_re)
            p_re = coef_ref[6, :, cs]
            p_im = coef_ref[7, :, cs]
            h_re = x_re + p_re * c_re - p_im * c_im
            h_im = x_im + p_re * c_im + p_im * c_re
            hre_ref[sl, cs] = h_re
            him_ref[sl, cs] = h_im
            last = slice(SUBLANES - 1, SUBLANES)
            return (jnp.broadcast_to(h_re[last, :], h_re.shape),
                    jnp.broadcast_to(h_im[last, :], h_im.shape))

        c_re, c_im = lax.fori_loop(0, n_slab, body, (cre_ref[:, cs], cim_ref[:, cs]))
        cre_ref[:, cs] = c_re
        cim_ref[:, cs] = c_im

    ys = []
    for kb in range(n_kb):
        hcat = jnp.concatenate([hre_ref[:, kb * sp:(kb + 1) * sp].astype(BF16),
                                him_ref[:, kb * sp:(kb + 1) * sp].astype(BF16)], axis=1)
        ys.append(_dot(hcat, wout_ref[kb]))
    y = jnp.concatenate(ys, axis=1) + d_ref[...] * u
    y = _gelu_tanh(y)
    z = _dot(y.astype(BF16), wglu_ref[...]) + bglu_ref[...]
    o_ref[0] = y * jax.nn.sigmoid(z)


def _ssm(u, w_in, w_out, coef, d_flat, w_glu, b_glu):
    bsz, seq, ssm_w = u.shape
    n_state = coef.shape[-1]
    rows = min(SSM_ROWS, seq)
    full = lambda *shape: pl.BlockSpec(shape, lambda b, i: (0,) * len(shape))
    kern = functools.partial(_ssm_kernel, rows=rows, n_state=n_state, col_w=4 * LANES)
    return pl.pallas_call(
        kern,
        grid=(bsz, seq // rows),
        in_specs=[pl.BlockSpec((1, rows, ssm_w), lambda b, i: (b, i, 0)),
                  full(*w_in.shape), full(*w_out.shape), full(*coef.shape), full(1, ssm_w),
                  full(ssm_w, ssm_w), full(1, ssm_w)],
        out_specs=pl.BlockSpec((1, rows, ssm_w), lambda b, i: (b, i, 0)),
        out_shape=jax.ShapeDtypeStruct((bsz, seq, ssm_w), F32),
        scratch_shapes=[pltpu.VMEM((rows, n_state), F32), pltpu.VMEM((rows, n_state), F32),
                        pltpu.VMEM((SUBLANES, n_state), F32), pltpu.VMEM((SUBLANES, n_state), F32)],
        compiler_params=_cparams(("parallel", "arbitrary")),
        name="s5_glu",
    )(u, w_in, w_out, coef, d_flat, w_glu.astype(BF16), b_glu.reshape(1, ssm_w))


def _layer_norm(y, g, b):
    mu = jnp.mean(y, axis=-1, keepdims=True)
    yc = y - mu
    var = jnp.mean(yc * yc, axis=-1, keepdims=True)
    return yc * lax.rsqrt(var + EPS) * g + b


def _rms_norm(y, g):
    return y * lax.rsqrt(jnp.mean(y * y, axis=-1, keepdims=True) + EPS) * g


def _mix_kernel(x_ref, att_ref, ssm_ref, gatt_ref, gssm_ref, wout_ref, g1_ref, ln_g_ref, ln_b_ref,
                sc_ref, sh_ref, g2_ref, wsg_ref, wsu_ref, wsd_ref, h_ref, ht_ref, pre_ref, *, alpha, att_w):
    a_n = _rms_norm(att_ref[0], gatt_ref[...]).astype(BF16)
    s_n = _rms_norm(ssm_ref[0], gssm_ref[...]).astype(BF16)
    mix = _dot(a_n, wout_ref[0:att_w, :]) + _dot(s_n, wout_ref[att_w:, :])
    x1 = _layer_norm(alpha * x_ref[0] + g1_ref[0] * mix, ln_g_ref[...], ln_b_ref[...])
    h = x1 * (1.0 + sc_ref[0]) + sh_ref[0]
    h_ref[0] = h
    _store_token_slabs(ht_ref, h)
    hb = h.astype(BF16)
    hid = _silu(_dot(hb, wsg_ref[...])) * _dot(hb, wsu_ref[...])
    shared = _dot(hid.astype(BF16), wsd_ref[...])
    pre_ref[0] = alpha * x1 + g2_ref[0] * shared


def _mix(x, att, ssm, g_att, g_ssm, w_out, gate1, ln_g, ln_b, scale2, shift2, gate2,
         w_s_gate, w_s_up, w_s_down, alpha):
    bsz, seq, d = x.shape
    att_w = att.shape[-1]
    ssm_w = ssm.shape[-1]
    ff = w_s_gate.shape[1]
    tm = min(MIX_ROWS, seq)
    n_sub = _slab_rows(d)
    row = lambda b, i: (b, i, 0)
    per_b = lambda b, i: (b, 0, 0)
    full = lambda *shape: pl.BlockSpec(shape, lambda b, i: (0,) * len(shape))
    vec = lambda t: t.reshape(bsz, 1, d)
    return pl.pallas_call(
        functools.partial(_mix_kernel, alpha=alpha, att_w=att_w),
        grid=(bsz, seq // tm),
        in_specs=[pl.BlockSpec((1, tm, d), row), pl.BlockSpec((1, tm, att_w), row),
                  pl.BlockSpec((1, tm, ssm_w), row),
                  full(1, att_w), full(1, ssm_w), full(att_w + ssm_w, d),
                  pl.BlockSpec((1, 1, d), per_b), full(1, d), full(1, d),
                  pl.BlockSpec((1, 1, d), per_b), pl.BlockSpec((1, 1, d), per_b),
                  pl.BlockSpec((1, 1, d), per_b),
                  full(d, ff), full(d, ff), full(ff, d)],
        out_specs=[pl.BlockSpec((1, tm, d), row),
                   pl.BlockSpec((tm * n_sub, LANES), lambda b, i: (b * (seq // tm) + i, 0)),
                   pl.BlockSpec((1, tm, d), row)],
        out_shape=[jax.ShapeDtypeStruct((bsz, seq, d), F32),
                   jax.ShapeDtypeStruct((bsz * seq * n_sub, LANES), U32),
                   jax.ShapeDtypeStruct((bsz, seq, d), F32)],
        compiler_params=_cparams(("parallel", "parallel")),
        name="mix_ln1_shared",
    )(x, att, ssm, g_att.reshape(1, att_w), g_ssm.reshape(1, ssm_w), w_out.astype(BF16),
      vec(gate1), ln_g.reshape(1, d), ln_b.reshape(1, d), vec(scale2), vec(shift2), vec(gate2),
      w_s_gate.astype(BF16), w_s_up.astype(BF16), w_s_down.astype(BF16))


def _route_kernel(h_ref, wrt_ref, wrt_lo_ref, bias_ref, tri_ref, idx_ref, gate_ref, rank_ref, cnt_ref, carry_ref):
    n_e = wrt_ref.shape[0]
    tm = h_ref.shape[0]
    per_group = n_e // N_EXPERT_GROUPS
    neg = -jnp.inf

    @pl.when(pl.program_id(0) == 0)
    def _():
        carry_ref[...] = jnp.zeros_like(carry_ref)

    h_hi, h_lo = _split_bf16(h_ref[...])
    logits = (_dot_nt(wrt_ref[...], h_hi) + _dot_nt(wrt_ref[...], h_lo)
              + _dot_nt(wrt_lo_ref[...], h_hi))
    scores = jax.nn.sigmoid(logits)
    biased = scores + bias_ref[...]

    g3 = biased.reshape(N_EXPERT_GROUPS, per_group, tm)
    ridx = lax.broadcasted_iota(jnp.int32, g3.shape, 1).astype(F32)
    m1 = jnp.max(g3, axis=1, keepdims=True)
    first = jnp.min(jnp.where(g3 == m1, ridx, float(per_group)), axis=1, keepdims=True)
    m2 = jnp.max(jnp.where(ridx == first, neg, g3), axis=1, keepdims=True)
    gs = m1 + m2

    gidx = lax.broadcasted_iota(jnp.int32, gs.shape, 0).astype(F32)
    ok = jnp.zeros_like(gs)
    cur = gs
    for _ in range(TOPK_GROUPS):
        mx = jnp.max(cur, axis=0, keepdims=True)
        fi = jnp.min(jnp.where(cur == mx, gidx, float(N_EXPERT_GROUPS)), axis=0, keepdims=True)
        hit = gidx == fi
        ok = jnp.where(hit, 1.0, ok)
        cur = jnp.where(hit, neg, cur)
    masked = jnp.where(ok > 0.5, g3, neg).reshape(n_e, tm)

    eidx = lax.broadcasted_iota(jnp.int32, (n_e, tm), 0).astype(F32)
    onehot = jnp.zeros((n_e, tm), F32)
    cur = masked
    sel_idx = []
    sel_gate = []
    for _ in range(TOP_K):
        mx = jnp.max(cur, axis=0, keepdims=True)
        fi = jnp.min(jnp.where(cur == mx, eidx, float(n_e)), axis=0, keepdims=True)
        hit = eidx == fi
        sel_idx.append(fi)
        sel_gate.append(jnp.sum(jnp.where(hit, scores, 0.0), axis=0, keepdims=True))
        onehot = jnp.where(hit, 1.0, onehot)
        cur = jnp.where(hit, neg, cur)
    idx = jnp.concatenate(sel_idx, axis=0)
    gate = jnp.concatenate(sel_gate, axis=0)
    gate = gate / jnp.sum(gate, axis=0, keepdims=True) * ROUTED_SCALE

    prior = _dot(onehot.astype(BF16), tri_ref[...]) + carry_ref[:, 0:1]
    ranks = [jnp.sum(jnp.where(eidx == sel_idx[k], prior, 0.0), axis=0, keepdims=True)
             for k in range(TOP_K)]
    rank = jnp.concatenate(ranks, axis=0)
    carry = carry_ref[...] + jnp.sum(onehot, axis=1, keepdims=True)
    carry_ref[...] = carry
    cnt_ref[...] = carry

    idx_ref[...] = idx.astype(jnp.int32)
    gate_ref[...] = gate
    rank_ref[...] = rank.astype(jnp.int32)


def _route(h2d, w_router, router_bias):
    n_tok, d = h2d.shape
    n_e = w_router.shape[1]
    tm = min(ROUTE_COLS, n_tok)
    tri = (np.arange(tm)[:, None] < np.arange(tm)[None, :]).astype(np.float32)
    w_hi, w_lo = _split_bf16(w_router.T.astype(F32))
    col = lambda i: (0, i)
    full = lambda *shape: pl.BlockSpec(shape, lambda i: (0,) * len(shape))
    idx, gate, rank, cnt = pl.pallas_call(
        _route_kernel,
        grid=(n_tok // tm,),
        in_specs=[pl.BlockSpec((tm, d), lambda i: (i, 0)), full(n_e, d), full(n_e, d), full(n_e, 1),
                  full(tm, tm)],
        out_specs=[pl.BlockSpec((TOP_K, tm), col), pl.BlockSpec((TOP_K, tm), col),
                   pl.BlockSpec((TOP_K, tm), col), full(n_e, LANES)],
        out_shape=[jax.ShapeDtypeStruct((TOP_K, n_tok), jnp.int32),
                   jax.ShapeDtypeStruct((TOP_K, n_tok), F32),
                   jax.ShapeDtypeStruct((TOP_K, n_tok), jnp.int32),
                   jax.ShapeDtypeStruct((n_e, LANES), F32)],
        scratch_shapes=[pltpu.VMEM((n_e, LANES), F32)],
        compiler_params=_cparams(("arbitrary",)),
        name="router_topk",
    )(h2d, w_hi, w_lo, router_bias.reshape(n_e, 1).astype(F32), jnp.asarray(tri, BF16))
    return idx, gate, rank, cnt[:, 0].astype(jnp.int32)


def _slab_at(ref, row, n_sub):
    per_tile = SUBLANES // n_sub
    if per_tile == 1:
        return ref.at[row]
    shift = per_tile.bit_length() - 1
    return ref.at[row >> shift, pl.ds((row & (per_tile - 1)) * n_sub, n_sub), :]


def _dispatch_kernel(pend_ref, cnt_ref, dest_ref, ht_ref, xs_ref, zero_ref, sem_ref, *, n_e, blk, tm):
    n_sub = ht_ref.shape[0] // tm
    blk_tiles = zero_ref.shape[0]

    def zero_copy(e):
        tile_shift = (SUBLANES // n_sub).bit_length() - 1
        start = pl.multiple_of((pend_ref[e] - blk) >> tile_shift, blk_tiles)
        return pltpu.make_async_copy(zero_ref, xs_ref.at[pl.ds(start, blk_tiles)], sem_ref.at[0])

    @pl.when(pl.program_id(0) == 0)
    def _():
        zero_ref[...] = jnp.zeros_like(zero_ref)

        def start(e, c):
            @pl.when(cnt_ref[e] > 0)
            def _():
                zero_copy(e).start()
            return c
        lax.fori_loop(0, n_e, start, 0)

        def wait(e, c):
            @pl.when(cnt_ref[e] > 0)
            def _():
                zero_copy(e).wait()
            return c
        lax.fori_loop(0, n_e, wait, 0)

    def row_copy(t, dst_row):
        src = ht_ref.at[pl.ds(pl.multiple_of(t * n_sub, n_sub), n_sub)]
        return pltpu.make_async_copy(src, _slab_at(xs_ref, dst_row, n_sub), sem_ref.at[1])

    def issue(g, c):
        t0 = g * DMA_ISSUE_UNROLL
        rows = [[dest_ref[(t0 + j) * TOP_K + k] for k in range(TOP_K)] for j in range(DMA_ISSUE_UNROLL)]
        for j in range(DMA_ISSUE_UNROLL):
            for k in range(TOP_K):
                row_copy(t0 + j, rows[j][k]).start(priority=k % 2)
        return c
    lax.fori_loop(0, tm // DMA_ISSUE_UNROLL, issue, 0)

    def drain(t, c):
        for k in range(TOP_K):
            row_copy(t, dest_ref[t * TOP_K + k]).wait()
        return c
    lax.fori_loop(0, tm, drain, 0)


def _dispatch(h_tiles, n_sub, dest, pend, cnt, n_rows):
    n_tok = h_tiles.shape[0] // n_sub
    n_e = cnt.shape[0]
    tm = min(DISPATCH_ROWS, n_tok)
    blk_tiles = MOE_ROWS * n_sub // SUBLANES
    grid_spec = pltpu.PrefetchScalarGridSpec(
        num_scalar_prefetch=2,
        grid=(n_tok // tm,),
        in_specs=[pl.BlockSpec((TOP_K * tm,), lambda i, *_: (i,), memory_space=pltpu.SMEM),
                  pl.BlockSpec((tm * n_sub, LANES), lambda i, *_: (i, 0))],
        out_specs=pl.BlockSpec(memory_space=pl.ANY),
        scratch_shapes=[pltpu.VMEM((blk_tiles, SUBLANES, LANES), U32), pltpu.SemaphoreType.DMA((2,))],
    )
    return pl.pallas_call(
        functools.partial(_dispatch_kernel, n_e=n_e, blk=MOE_ROWS, tm=tm),
        grid_spec=grid_spec,
        out_shape=jax.ShapeDtypeStruct((n_rows * n_sub // SUBLANES, SUBLANES, LANES), U32),
        compiler_params=_cparams(("arbitrary",)),
        name="moe_dispatch",
    )(pend, cnt, dest, h_tiles)


def _expert_kernel(be_ref, nused_ref, next_ref, slot_ref, x_ref, wg_ref, wu_ref, wd_ref, y_ref,
                   wgf_ref, wuf_ref, wdf_ref, wgb_ref, wub_ref, wdb_ref, sem_ref):
    i = pl.program_id(0)
    active = i < nused_ref[0]
    prev = be_ref[jnp.maximum(i - 1, 0)]
    fresh = jnp.logical_and(active, jnp.logical_or(i == 0, be_ref[i] != prev))

    def weight_copies(e, slot):
        return [pltpu.make_async_copy(src.at[e], dst.at[slot], sem_ref.at[slot])
                for src, dst in ((wg_ref, wgf_ref), (wu_ref, wuf_ref), (wd_ref, wdf_ref))]

    @pl.when(jnp.logical_and(active, i == 0))
    def _():
        for cp in weight_copies(be_ref[0], 0):
            cp.start()

    @pl.when(fresh)
    def _():
        slot = slot_ref[i]
        for cp in weight_copies(be_ref[i], slot):
            cp.wait()
        wgb_ref[...] = wgf_ref[slot].astype(BF16)
        wub_ref[...] = wuf_ref[slot].astype(BF16)
        wdb_ref[...] = wdf_ref[slot].astype(BF16)

        @pl.when(next_ref[i] >= 0)
        def _():
            for cp in weight_copies(next_ref[i], 1 - slot):
                cp.start()

    @pl.when(active)
    def _():
        d = wgb_ref.shape[0]
        lo, hi = _load_token_slabs(x_ref, MOE_ROWS, d)
        xb = jnp.concatenate(lo + hi, axis=1).astype(BF16)
        hid = _silu(_dot(xb, wgb_ref[...])) * _dot(xb, wub_ref[...])
        _store_token_slabs(y_ref, _dot(hid.astype(BF16), wdb_ref[...]))


def _experts(xs_tiles, plan, w_gate, w_up, w_down):
    d, ff = w_gate.shape[-2], w_gate.shape[-1]
    n_sub = _slab_rows(d)
    n_rows = xs_tiles.shape[0] * SUBLANES // n_sub
    n_blk = n_rows // MOE_ROWS
    rows_spec = pl.BlockSpec((MOE_ROWS * n_sub, LANES), lambda i, be, nu, *_: (jnp.minimum(i, nu[0] - 1), 0))
    hbm = pl.BlockSpec(memory_space=pl.ANY)
    grid_spec = pltpu.PrefetchScalarGridSpec(
        num_scalar_prefetch=4,
        grid=(n_blk,),
        in_specs=[rows_spec, hbm, hbm, hbm],
        out_specs=rows_spec,
        scratch_shapes=[pltpu.VMEM((2, d, ff), F32), pltpu.VMEM((2, d, ff), F32), pltpu.VMEM((2, ff, d), F32),
                        pltpu.VMEM((d, ff), BF16), pltpu.VMEM((d, ff), BF16), pltpu.VMEM((ff, d), BF16),
                        pltpu.SemaphoreType.DMA((2,))],
    )
    ys = pl.pallas_call(
        _expert_kernel,
        grid_spec=grid_spec,
        out_shape=jax.ShapeDtypeStruct((n_rows * n_sub, LANES), U32),
        compiler_params=_cparams(("arbitrary",)),
        name="moe_experts",
    )(*plan, xs_tiles.reshape(n_rows * n_sub, LANES), w_gate, w_up, w_down)
    return ys.reshape(xs_tiles.shape)


def _combine_kernel(dest_ref, ys_ref, gate_ref, pre_ref, g2_ref, ln_g_ref, ln_b_ref, o_ref, buf_ref, sem_ref):
    tm, d = pre_ref.shape[1], pre_ref.shape[2]
    n_sub = _slab_rows(d)

    def row_copy(t, k, src_row):
        slot = pl.multiple_of((k * tm + t) * n_sub, n_sub)
        return pltpu.make_async_copy(_slab_at(ys_ref, src_row, n_sub), buf_ref.at[pl.ds(slot, n_sub)],
                                     sem_ref.at[0])

    def issue(g, c):
        t0 = g * DMA_ISSUE_UNROLL
        rows = [[dest_ref[(t0 + j) * TOP_K + k] for k in range(TOP_K)] for j in range(DMA_ISSUE_UNROLL)]
        for j in range(DMA_ISSUE_UNROLL):
            for k in range(TOP_K):
                row_copy(t0 + j, k, rows[j][k]).start(priority=k % 2)
        return c
    lax.fori_loop(0, tm // DMA_ISSUE_UNROLL, issue, 0)

    def drain(t, c):
        for k in range(TOP_K):
            row_copy(t, k, dest_ref[t * TOP_K + k]).wait()
        return c
    lax.fori_loop(0, tm, drain, 0)

    gates = gate_ref[...]
    gate_cols = [jnp.broadcast_to(gates[:, k:k + 1], (tm, LANES)) for k in range(TOP_K)]
    lo_acc = [None] * n_sub
    hi_acc = [None] * n_sub
    for k in range(TOP_K):
        lo, hi = _load_token_slabs(buf_ref, tm, d, row0=k * tm * n_sub)
        for s in range(n_sub):
            lo_t, hi_t = lo[s] * gate_cols[k], hi[s] * gate_cols[k]
            lo_acc[s] = lo_t if k == 0 else lo_acc[s] + lo_t
            hi_acc[s] = hi_t if k == 0 else hi_acc[s] + hi_t
    routed = jnp.concatenate(lo_acc + hi_acc, axis=1)
    y = pre_ref[0] + g2_ref[0] * routed
    o_ref[0] = _layer_norm(y, ln_g_ref[...], ln_b_ref[...])


def _combine(ys, dest, gates_t, pre, gate2, ln_g, ln_b):
    bsz, seq, d = pre.shape
    tm = min(DISPATCH_ROWS, seq)
    per_seq = seq // tm
    row = lambda i, *_: (i // per_seq, i % per_seq, 0)
    per_b = lambda i, *_: (i // per_seq, 0, 0)
    full = lambda *shape: pl.BlockSpec(shape, lambda i, *_: (0,) * len(shape))
    grid_spec = pltpu.PrefetchScalarGridSpec(
        num_scalar_prefetch=0,
        grid=(bsz * per_seq,),
        in_specs=[pl.BlockSpec((TOP_K * tm,), lambda i: (i,), memory_space=pltpu.SMEM),
                  pl.BlockSpec(memory_space=pl.ANY),
                  pl.BlockSpec((tm, TOP_K), lambda i: (i, 0)),
                  pl.BlockSpec((1, tm, d), row), pl.BlockSpec((1, 1, d), per_b), full(1, d), full(1, d)],
        out_specs=pl.BlockSpec((1, tm, d), row),
        scratch_shapes=[pltpu.VMEM((TOP_K * tm * _slab_rows(d), LANES), U32), pltpu.SemaphoreType.DMA((1,))],
    )
    return pl.pallas_call(
        _combine_kernel,
        grid_spec=grid_spec,
        out_shape=jax.ShapeDtypeStruct((bsz, seq, d), F32),
        compiler_params=_cparams(("arbitrary",)),
        name="moe_combine_ln2",
    )(dest, ys, gates_t, pre, gate2.reshape(bsz, 1, d), ln_g.reshape(1, d), ln_b.reshape(1, d))


def _dest_kernel(pstart_ref, idx_ref, rank_ref, dest_ref):
    idx = idx_ref[...]

    def body(e, acc):
        return acc + jnp.where(idx == e, pstart_ref[e], 0)
    dest_ref[...] = lax.fori_loop(0, pstart_ref.shape[0], body, rank_ref[...], unroll=8)


def _dest_rows(pstart, idx, rank):
    n_k, n_tok = idx.shape
    tn = min(4096, n_tok)
    grid_spec = pltpu.PrefetchScalarGridSpec(
        num_scalar_prefetch=1,
        grid=(n_tok // tn,),
        in_specs=[pl.BlockSpec((n_k, tn), lambda i, *_: (0, i))] * 2,
        out_specs=pl.BlockSpec((n_k, tn), lambda i, *_: (0, i)),
    )
    return pl.pallas_call(
        _dest_kernel,
        grid_spec=grid_spec,
        out_shape=jax.ShapeDtypeStruct((n_k, n_tok), jnp.int32),
        compiler_params=_cparams(("arbitrary",)),
        name="moe_dest",
    )(pstart, idx, rank)


def _moe_plan(idx, rank, cnt, n_tok):
    n_e = cnt.shape[0]
    padded = (cnt + MOE_ROWS - 1) // MOE_ROWS * MOE_ROWS
    pend = jnp.cumsum(padded).astype(jnp.int32)
    pstart = pend - padded
    dest = _dest_rows(pstart, idx, rank).T.reshape(-1)
    n_blk = n_tok * TOP_K // MOE_ROWS + n_e
    starts = jnp.arange(n_blk, dtype=jnp.int32) * MOE_ROWS
    block_expert = jnp.minimum(jnp.sum((pend[None, :] <= starts[:, None]).astype(jnp.int32), axis=1),
                               n_e - 1).astype(jnp.int32)
    n_used = (pend[-1:] // MOE_ROWS).astype(jnp.int32)
    ids = jnp.arange(n_e, dtype=jnp.int32)
    later = jnp.where((cnt[None, :] > 0) & (ids[None, :] > ids[:, None]), ids[None, :], n_e)
    nxt = jnp.min(later, axis=1)
    nxt = jnp.where(nxt == n_e, -1, nxt).astype(jnp.int32)
    parity = ((jnp.cumsum((cnt > 0).astype(jnp.int32)) - 1) & 1).astype(jnp.int32)
    plan = (block_expert, n_used, jnp.take(nxt, block_expert), jnp.take(parity, block_expert))
    return dest, pend, plan, n_blk * MOE_ROWS


def kernel(x, c, rel_bias, w_ada, b_ada, w_in, ssm_a_re, ssm_a_im, ssm_b_re, ssm_b_im, ssm_c_re, ssm_c_im, ssm_d, ssm_log_dt, w_glu, b_glu, g_att, g_ssm, w_out, ln1_g, ln1_b, w_router, router_bias, w_e_gate, w_e_up, w_e_down, w_s_gate, w_s_up, w_s_down, ln2_g, ln2_b):
    bsz, seq, d = x.shape
    depth = w_ada.shape[0]
    alpha = (2 * depth) ** 0.25
    att_w = g_att.shape[-1]
    tables = _att_tables(rel_bias)
    for layer in range(depth):
        ada = _ada(c, w_ada[layer], b_ada[layer])
        shift1, scale1, gate1, shift2, scale2, gate2 = jnp.split(ada, 6, axis=-1)
        q, k, v, u = _inproj(x, scale1, shift1, w_in[layer], att_w)
        att = _attention(q, k, v, tables)
        ssm_prm = _ssm_params(ssm_a_re[layer], ssm_a_im[layer], ssm_b_re[layer], ssm_b_im[layer],
                              ssm_c_re[layer], ssm_c_im[layer], ssm_d[layer], ssm_log_dt[layer])
        ssm = _ssm(u, *ssm_prm, w_glu[layer], b_glu[layer])
        h2, h2_tiles, pre = _mix(x, att, ssm, g_att[layer], g_ssm[layer], w_out[layer], gate1,
                                 ln1_g[layer], ln1_b[layer], scale2, shift2, gate2, w_s_gate[layer],
                                 w_s_up[layer], w_s_down[layer], alpha)
        h2d = h2.reshape(bsz * seq, d)
        idx, gate, rank, cnt = _route(h2d, w_router[layer], router_bias[layer])
        dest, pend, plan, n_rows = _moe_plan(idx, rank, cnt, bsz * seq)
        xs = _dispatch(h2_tiles, _slab_rows(d), dest, pend, cnt, n_rows)
        ys = _experts(xs, plan, w_e_gate[layer], w_e_up[layer], w_e_down[layer])
        x = _combine(ys, dest, gate.T, pre, gate2, ln2_g[layer], ln2_b[layer])
    return x
```

```python
import functools
import math

import jax
import jax.numpy as jnp
import numpy as np
from jax import lax
from jax.experimental import pallas as pl
from jax.experimental.pallas import tpu as pltpu

F32 = jnp.float32
BF16 = jnp.bfloat16

HEAD_DIM = 64
ATT_BLOCK = 128
PATTERNS = ((128, 1), (512, 4), (2048, 16))
N_BUCKETS = 32
MAX_DISTANCE = 2048
SSM_GROUP = 16
SSM_STATE = 64
N_EXPERTS = 256
TOP_K = 8
N_EXPERT_GROUPS = 8
TOPK_GROUPS = 4
ROUTED_SCALE = 2.5
EPS = 1e-5
NEG_INF = -1e30

LANES = 128
SUBLANES = 8
VMEM_LIMIT_BYTES = 56 * 1024 * 1024

PROJ_ROWS = 512
SSM_ROWS = 256
MIX_ROWS = 512
ROUTE_COLS = 512
MOE_ROWS = 256
DISPATCH_ROWS = 256
ATT_UNITS_PER_STEP = 8
DMA_ISSUE_UNROLL = 4


def _cparams(sem, vmem=VMEM_LIMIT_BYTES):
    return pltpu.CompilerParams(dimension_semantics=sem, vmem_limit_bytes=vmem)


def _dot(a, b):
    return jnp.dot(a, b, preferred_element_type=F32)


def _dot_nt(a, b):
    return lax.dot_general(a, b, (((1,), (1,)), ((), ())), preferred_element_type=F32)


def _silu(x):
    return x * jax.nn.sigmoid(x)


def _store_token_tiles(ref, x):
    m, d = x.shape
    n_sub = d // LANES
    for s in range(n_sub):
        ref[pl.ds(s, m, stride=n_sub), :] = x[:, s * LANES:(s + 1) * LANES]


def _load_token_tiles(ref, m, d, row0=0):
    n_sub = d // LANES
    return [ref[pl.ds(row0 + s, m, stride=n_sub), :] for s in range(n_sub)]


def _split_bf16(x):
    hi = x.astype(BF16)
    lo = (x - hi.astype(F32)).astype(BF16)
    return hi, lo


def _ada_kernel(c_ref, w_ref, b_ref, o_ref):
    c = c_ref[...]
    a_hi, a_lo = _split_bf16(_silu(c))
    w_hi, w_lo = _split_bf16(w_ref[...])
    acc = _dot(a_hi, w_hi) + _dot(a_hi, w_lo) + _dot(a_lo, w_hi)
    o_ref[...] = acc + b_ref[...]


def _ada(c, w_ada, b_ada):
    bsz, d = c.shape
    n = w_ada.shape[1]
    rows = SUBLANES
    c_pad = jnp.zeros((rows, d), F32).at[:bsz].set(c)
    tn = 1024
    out = pl.pallas_call(
        _ada_kernel,
        grid=(n // tn,),
        in_specs=[pl.BlockSpec((rows, d), lambda j: (0, 0)),
                  pl.BlockSpec((d, tn), lambda j: (0, j)),
                  pl.BlockSpec((1, tn), lambda j: (0, j))],
        out_specs=pl.BlockSpec((rows, tn), lambda j: (0, j)),
        out_shape=jax.ShapeDtypeStruct((rows, n), F32),
        compiler_params=_cparams(("parallel",)),
        name="ada",
    )(c_pad, w_ada, b_ada.reshape(1, n))
    return out[:bsz]


def _inproj_kernel(x_ref, sc_ref, sh_ref, w_ref, q_ref, k_ref, v_ref, u_ref, *, att_w, q_scale):
    h = (x_ref[0] * (1.0 + sc_ref[0]) + sh_ref[0]).astype(BF16)
    q_ref[0] = _dot(h, w_ref[:, 0:att_w]) * q_scale
    k_ref[0] = _dot(h, w_ref[:, att_w:2 * att_w])
    v_ref[0] = _dot(h, w_ref[:, 2 * att_w:3 * att_w])
    u_ref[0] = _dot(h, w_ref[:, 3 * att_w:])


def _inproj(x, scale, shift, w_in, att_w):
    bsz, seq, d = x.shape
    n = w_in.shape[1]
    ssm_w = n - 3 * att_w
    ts = min(PROJ_ROWS, seq)
    row = lambda b, i: (b, i, 0)
    per_b = lambda b, i: (b, 0, 0)
    kern = functools.partial(_inproj_kernel, att_w=att_w, q_scale=HEAD_DIM ** -0.5)
    return pl.pallas_call(
        kern,
        grid=(bsz, seq // ts),
        in_specs=[pl.BlockSpec((1, ts, d), row),
                  pl.BlockSpec((1, 1, d), per_b),
                  pl.BlockSpec((1, 1, d), per_b),
                  pl.BlockSpec((d, n), lambda b, i: (0, 0))],
        out_specs=[pl.BlockSpec((1, ts, att_w), row)] * 3 + [pl.BlockSpec((1, ts, ssm_w), row)],
        out_shape=[jax.ShapeDtypeStruct((bsz, seq, att_w), F32)] * 3
                  + [jax.ShapeDtypeStruct((bsz, seq, ssm_w), F32)],
        compiler_params=_cparams(("parallel", "parallel")),
        name="inproj",
    )(x, scale.reshape(bsz, 1, d), shift.reshape(bsz, 1, d), w_in.astype(BF16))


def _t5_bucket_np(dist):
    exact = N_BUCKETS // 2
    large = exact + (np.log(np.maximum(dist, 1).astype(np.float64) / exact)
                     / math.log(MAX_DISTANCE / exact) * (N_BUCKETS - exact)).astype(np.int64)
    return np.where(dist < exact, dist, np.minimum(large, N_BUCKETS - 1))


def _att_tables(rel_bias):
    qi = np.arange(ATT_BLOCK)[:, None]
    ki = np.arange(2 * ATT_BLOCK)[None, :]
    rel = qi + ATT_BLOCK - ki
    tabs = []
    for window, dil in PATTERNS:
        band = (rel >= 0) & (rel <= window // dil)
        bucket = _t5_bucket_np(np.maximum(rel, 0) * dil)
        onehot = (bucket[:, :, None] == np.arange(N_BUCKETS)[None, None, :]).astype(np.float32)
        bias = jnp.einsum('qkb,bh->hqk', onehot, rel_bias.astype(F32),
                          precision=lax.Precision.HIGHEST)
        full = jnp.where(band[None], bias, NEG_INF)
        first = jnp.concatenate([full[:, :, ATT_BLOCK:], jnp.full_like(full[:, :, ATT_BLOCK:], NEG_INF)], axis=-1)
        tabs.append(jnp.stack([full, first]))
    return jnp.stack(tabs)


def _att_kernel(q_ref, k_ref, v_ref, tab_ref, o_ref, m_ref, l_ref, *, seq):
    lane = lax.broadcasted_iota(jnp.int32, (ATT_BLOCK, LANES), 1)
    head0 = lane < HEAD_DIM

    def rows(ref, start, n, dil):
        if dil == 1:
            return ref[0, pl.ds(pl.multiple_of(start, ATT_BLOCK), n), :]
        return ref[0, pl.ds(start, n, stride=dil), :]

    def lane_mask(n_rows, h):
        m = lax.broadcasted_iota(jnp.int32, (n_rows, LANES), 1) < HEAD_DIM
        return m if h == 0 else ~m

    def batch(pi, dil, nb, res, blk0, head, init):
        q_rows = nb * ATT_BLOCK
        kv_blocks = nb if head else nb + 1
        q_start = res + dil * ATT_BLOCK * blk0
        kv_start = q_start if head else q_start - dil * ATT_BLOCK
        if dil == 1:
            sl = pl.ds(pl.multiple_of(q_start, ATT_BLOCK), q_rows)
        else:
            sl = pl.ds(q_start, q_rows, stride=dil)
        q = rows(q_ref, q_start, q_rows, dil)
        k = rows(k_ref, kv_start, kv_blocks * ATT_BLOCK, dil).astype(BF16)
        v = rows(v_ref, kv_start, kv_blocks * ATT_BLOCK, dil)
        if not init:
            m_old, l_old, o_old = m_ref[sl, :], l_ref[sl, :], o_ref[0, sl, :]
        qh = [jnp.where(lane_mask(q_rows, h), q, 0.0).astype(BF16) for h in range(2)]
        vh = [jnp.where(lane_mask(kv_blocks * ATT_BLOCK, h), v, 1.0).astype(BF16) for h in range(2)]
        o_parts, l_parts, m_parts = [], [], []
        for j in range(nb):
            first = 1 if (head and j == 0) else 0
            kb = j if (not head or j == 0) else j - 1
            ksl = slice(kb * ATT_BLOCK, (kb + 2) * ATT_BLOCK)
            qsl = slice(j * ATT_BLOCK, (j + 1) * ATT_BLOCK)
            outs, ms = [], []
            for h in range(2):
                s = _dot_nt(qh[h][qsl], k[ksl]) + tab_ref[pi, first, h]
                m = jnp.max(s, axis=1, keepdims=True)
                p = jnp.exp(s - m).astype(BF16)
                outs.append(_dot(p, vh[h][ksl]))
                ms.append(m)
            o_parts.append(jnp.where(head0, outs[0], outs[1]))
            l_parts.append(pltpu.roll(jnp.where(head0, outs[1], outs[0]), HEAD_DIM, axis=1))
            m_parts.append(jnp.where(head0, ms[0], ms[1]))
        o = jnp.concatenate(o_parts, axis=0)
        l = jnp.concatenate(l_parts, axis=0)
        m = jnp.concatenate(m_parts, axis=0)
        if not init:
            m_new = jnp.maximum(m_old, m)
            a_old = jnp.exp(m_old - m_new)
            a_cur = jnp.exp(m - m_new)
            o = o_old * a_old + o * a_cur
            l = l_old * a_old + l * a_cur
            m = m_new
        o_ref[0, sl, :] = o
        m_ref[sl, :] = m
        l_ref[sl, :] = l

    n_units = seq // ATT_BLOCK
    order = sorted(range(len(PATTERNS)), key=lambda p: -PATTERNS[p][1])
    for pos, pi in enumerate(order):
        dil = PATTERNS[pi][1]
        log_d = dil.bit_length() - 1
        res_blocks = seq // (dil * ATT_BLOCK)
        nb = min(ATT_UNITS_PER_STEP, res_blocks)
        per_res = res_blocks // nb
        init = pos == 0

        def head_body(res, carry, pi=pi, dil=dil, nb=nb, init=init):
            batch(pi, dil, nb, res, 0, True, init)
            return carry
        lax.fori_loop(0, dil, head_body, 0)

        if per_res > 1:
            def tail_body(i, carry, pi=pi, dil=dil, nb=nb, log_d=log_d, init=init):
                res = i & (dil - 1)
                blk0 = nb * (1 + (i >> log_d))
                batch(pi, dil, nb, res, blk0, False, init)
                return carry
            lax.fori_loop(0, dil * (per_res - 1), tail_body, 0)

    def finish(i, carry):
        sl = pl.ds(pl.multiple_of(i * ATT_BLOCK, ATT_BLOCK), ATT_BLOCK)
        o_ref[0, sl, :] = o_ref[0, sl, :] / l_ref[sl, :]
        return carry
    lax.fori_loop(0, n_units, finish, 0)


def _attention(q, k, v, tables):
    bsz, seq, att_w = q.shape
    n_pairs = att_w // LANES
    blk = lambda b, p: (b, 0, p)
    qkv_spec = pl.BlockSpec((1, seq, LANES), blk)
    n_pat = tables.shape[0]
    tab_spec = pl.BlockSpec((n_pat, 2, 2, ATT_BLOCK, 2 * ATT_BLOCK), lambda b, p: (0, 0, p, 0, 0))
    return pl.pallas_call(
        functools.partial(_att_kernel, seq=seq),
        grid=(bsz, n_pairs),
        in_specs=[qkv_spec, qkv_spec, qkv_spec, tab_spec],
        out_specs=pl.BlockSpec((1, seq, LANES), blk),
        out_shape=jax.ShapeDtypeStruct((bsz, seq, att_w), F32),
        scratch_shapes=[pltpu.VMEM((seq, LANES), F32), pltpu.VMEM((seq, LANES), F32)],
        compiler_params=_cparams(("parallel", "parallel")),
        name="dilated_attention",
    )(q, k, v, tables)


def _ssm_params(a_re, a_im, b_re, b_im, c_re, c_im, d_skip, log_dt):
    n_g, n_p = a_re.shape
    n_c = b_re.shape[-1]
    n_state = n_g * n_p
    dt = jnp.exp(log_dt.astype(F32))[:, None]

    def a_pow(kk):
        mag = jnp.exp(kk * dt * a_re)
        ph = kk * dt * a_im
        return mag * jnp.cos(ph), mag * jnp.sin(ph)

    ab_re, ab_im = a_pow(1.0)
    nr, ni = ab_re - 1.0, ab_im
    den = a_re * a_re + a_im * a_im
    f_re = (nr * a_re + ni * a_im) / den
    f_im = (ni * a_re - nr * a_im) / den
    bb_re = f_re[:, :, None] * b_re - f_im[:, :, None] * b_im
    bb_im = f_re[:, :, None] * b_im + f_im[:, :, None] * b_re
    eye = jnp.eye(n_g, dtype=F32)
    w_re = jnp.einsum('gpc,gh->gchp', bb_re, eye).reshape(n_g * n_c, n_state)
    w_im = jnp.einsum('gpc,gh->gchp', bb_im, eye).reshape(n_g * n_c, n_state)
    o_re = jnp.einsum('gcp,gh->gphc', c_re, eye).reshape(n_state, n_g * n_c)
    o_im = jnp.einsum('gcp,gh->gphc', c_im, eye).reshape(n_state, n_g * n_c)
    n_kb = (n_g * n_c) // LANES
    sp = n_state // n_kb
    w_in = jnp.stack([jnp.concatenate([w_re[kb * LANES:(kb + 1) * LANES, kb * sp:(kb + 1) * sp],
                                       w_im[kb * LANES:(kb + 1) * LANES, kb * sp:(kb + 1) * sp]], axis=1)
                      for kb in range(n_kb)])
    w_out = jnp.stack([jnp.concatenate([o_re[kb * sp:(kb + 1) * sp, kb * LANES:(kb + 1) * LANES],
                                        -o_im[kb * sp:(kb + 1) * sp, kb * LANES:(kb + 1) * LANES]], axis=0)
                       for kb in range(n_kb)])
    j = jnp.arange(SUBLANES, dtype=F32)[:, None]
    flat = lambda t: jnp.broadcast_to(t.reshape(1, n_state), (SUBLANES, n_state))
    coef = []
    for sh in (1, 2, 4):
        pr, pim = a_pow(float(sh))
        keep = (j >= sh).astype(F32)
        coef += [flat(pr) * keep, flat(pim) * keep]
    a_re_f = jnp.broadcast_to(a_re.reshape(1, n_state), (SUBLANES, n_state))
    a_im_f = jnp.broadcast_to(a_im.reshape(1, n_state), (SUBLANES, n_state))
    dt_f = jnp.broadcast_to(jnp.repeat(dt[:, 0], n_p).reshape(1, n_state), (SUBLANES, n_state))
    mag = jnp.exp((j + 1.0) * dt_f * a_re_f)
    ph = (j + 1.0) * dt_f * a_im_f
    coef += [mag * jnp.cos(ph), mag * jnp.sin(ph)]
    coef = jnp.stack(coef)
    return w_in.astype(BF16), w_out.astype(BF16), coef, d_skip.reshape(1, n_g * n_c).astype(F32)


def _gelu_tanh(x):
    return 0.5 * x * (1.0 + jnp.tanh(math.sqrt(2.0 / math.pi) * (x + 0.044715 * (x * x * x))))


def _ssm_kernel(u_ref, win_ref, wout_ref, coef_ref, d_ref, wglu_ref, bglu_ref, o_ref,
                hre_ref, him_ref, cre_ref, cim_ref, *, rows, n_state, col_w):
    n_kb = win_ref.shape[0]
    sp = n_state // n_kb

    @pl.when(pl.program_id(1) == 0)
    def _():
        cre_ref[...] = jnp.zeros_like(cre_ref)
        cim_ref[...] = jnp.zeros_like(cim_ref)

    u = u_ref[0]
    ub = u.astype(BF16)
    for kb in range(n_kb):
        bu = _dot(ub[:, kb * LANES:(kb + 1) * LANES], win_ref[kb])
        hre_ref[:, kb * sp:(kb + 1) * sp] = bu[:, :sp]
        him_ref[:, kb * sp:(kb + 1) * sp] = bu[:, sp:]

    n_slab = rows // SUBLANES
    for c0 in range(0, n_state, col_w):
        cs = slice(c0, c0 + col_w)

        def body(i, carry, cs=cs):
            c_re, c_im = carry
            sl = pl.ds(pl.multiple_of(i * SUBLANES, SUBLANES), SUBLANES)
            x_re = hre_ref[sl, cs]
            x_im = him_ref[sl, cs]
            for si, sh in enumerate((1, 2, 4)):
                p_re = coef_ref[2 * si, :, cs]
                p_im = coef_ref[2 * si + 1, :, cs]
                s_re = pltpu.roll(x_re, sh, axis=0)
                s_im = pltpu.roll(x_im, sh, axis=0)
                x_re, x_im = (x_re + p_re * s_re - p_im * s_im,
                              x_im + p_re * s_im + p_im * s_re)
            p_re = coef_ref[6, :, cs]
            p_im = coef_ref[7, :, cs]
            h_re = x_re + p_re * c_re - p_im * c_im
            h_im = x_im + p_re * c_im + p_im * c_re
            hre_ref[sl, cs] = h_re
            him_ref[sl, cs] = h_im
            last = slice(SUBLANES - 1, SUBLANES)
            return (jnp.broadcast_to(h_re[last, :], h_re.shape),
                    jnp.broadcast_to(h_im[last, :], h_im.shape))

        c_re, c_im = lax.fori_loop(0, n_slab, body, (cre_ref[:, cs], cim_ref[:, cs]))
        cre_ref[:, cs] = c_re
        cim_ref[:, cs] = c_im

    ys = []
    for kb in range(n_kb):
        hcat = jnp.concatenate([hre_ref[:, kb * sp:(kb + 1) * sp].astype(BF16),
                                him_ref[:, kb * sp:(kb + 1) * sp].astype(BF16)], axis=1)
        ys.append(_dot(hcat, wout_ref[kb]))
    y = jnp.concatenate(ys, axis=1) + d_ref[...] * u
    y = _gelu_tanh(y)
    z = _dot(y.astype(BF16), wglu_ref[...]) + bglu_ref[...]
    o_ref[0] = y * jax.nn.sigmoid(z)


def _ssm(u, w_in, w_out, coef, d_flat, w_glu, b_glu):
    bsz, seq, ssm_w = u.shape
    n_state = coef.shape[-1]
    rows = min(SSM_ROWS, seq)
    full = lambda *shape: pl.BlockSpec(shape, lambda b, i: (0,) * len(shape))
    kern = functools.partial(_ssm_kernel, rows=rows, n_state=n_state, col_w=4 * LANES)
    return pl.pallas_call(
        kern,
        grid=(bsz, seq // rows),
        in_specs=[pl.BlockSpec((1, rows, ssm_w), lambda b, i: (b, i, 0)),
                  full(*w_in.shape), full(*w_out.shape), full(*coef.shape), full(1, ssm_w),
                  full(ssm_w, ssm_w), full(1, ssm_w)],
        out_specs=pl.BlockSpec((1, rows, ssm_w), lambda b, i: (b, i, 0)),
        out_shape=jax.ShapeDtypeStruct((bsz, seq, ssm_w), F32),
        scratch_shapes=[pltpu.VMEM((rows, n_state), F32), pltpu.VMEM((rows, n_state), F32),
                        pltpu.VMEM((SUBLANES, n_state), F32), pltpu.VMEM((SUBLANES, n_state), F32)],
        compiler_params=_cparams(("parallel", "arbitrary")),
        name="s5_glu",
    )(u, w_in, w_out, coef, d_flat, w_glu.astype(BF16), b_glu.reshape(1, ssm_w))


def _layer_norm(y, g, b):
    mu = jnp.mean(y, axis=-1, keepdims=True)
    yc = y - mu
    var = jnp.mean(yc * yc, axis=-1, keepdims=True)
    return yc * lax.rsqrt(var + EPS) * g + b


def _rms_norm(y, g):
    return y * lax.rsqrt(jnp.mean(y * y, axis=-1, keepdims=True) + EPS) * g


def _mix_kernel(x_ref, att_ref, ssm_ref, gatt_ref, gssm_ref, wout_ref, g1_ref, ln_g_ref, ln_b_ref,
                sc_ref, sh_ref, g2_ref, wsg_ref, wsu_ref, wsd_ref, h_ref, ht_ref, pre_ref, *, alpha, att_w):
    a_n = _rms_norm(att_ref[0], gatt_ref[...]).astype(BF16)
    s_n = _rms_norm(ssm_ref[0], gssm_ref[...]).astype(BF16)
    mix = _dot(a_n, wout_ref[0:att_w, :]) + _dot(s_n, wout_ref[att_w:, :])
    x1 = _layer_norm(alpha * x_ref[0] + g1_ref[0] * mix, ln_g_ref[...], ln_b_ref[...])
    h = x1 * (1.0 + sc_ref[0]) + sh_ref[0]
    h_ref[0] = h
    _store_token_tiles(ht_ref, h)
    hb = h.astype(BF16)
    hid = _silu(_dot(hb, wsg_ref[...])) * _dot(hb, wsu_ref[...])
    shared = _dot(hid.astype(BF16), wsd_ref[...])
    pre_ref[0] = alpha * x1 + g2_ref[0] * shared


def _mix(x, att, ssm, g_att, g_ssm, w_out, gate1, ln_g, ln_b, scale2, shift2, gate2,
         w_s_gate, w_s_up, w_s_down, alpha):
    bsz, seq, d = x.shape
    att_w = att.shape[-1]
    ssm_w = ssm.shape[-1]
    ff = w_s_gate.shape[1]
    tm = min(MIX_ROWS, seq)
    n_sub = d // LANES
    row = lambda b, i: (b, i, 0)
    per_b = lambda b, i: (b, 0, 0)
    full = lambda *shape: pl.BlockSpec(shape, lambda b, i: (0,) * len(shape))
    vec = lambda t: t.reshape(bsz, 1, d)
    return pl.pallas_call(
        functools.partial(_mix_kernel, alpha=alpha, att_w=att_w),
        grid=(bsz, seq // tm),
        in_specs=[pl.BlockSpec((1, tm, d), row), pl.BlockSpec((1, tm, att_w), row),
                  pl.BlockSpec((1, tm, ssm_w), row),
                  full(1, att_w), full(1, ssm_w), full(att_w + ssm_w, d),
                  pl.BlockSpec((1, 1, d), per_b), full(1, d), full(1, d),
                  pl.BlockSpec((1, 1, d), per_b), pl.BlockSpec((1, 1, d), per_b),
                  pl.BlockSpec((1, 1, d), per_b),
                  full(d, ff), full(d, ff), full(ff, d)],
        out_specs=[pl.BlockSpec((1, tm, d), row),
                   pl.BlockSpec((tm * n_sub, LANES), lambda b, i: (b * (seq // tm) + i, 0)),
                   pl.BlockSpec((1, tm, d), row)],
        out_shape=[jax.ShapeDtypeStruct((bsz, seq, d), F32),
                   jax.ShapeDtypeStruct((bsz * seq * n_sub, LANES), F32),
                   jax.ShapeDtypeStruct((bsz, seq, d), F32)],
        compiler_params=_cparams(("parallel", "parallel")),
        name="mix_ln1_shared",
    )(x, att, ssm, g_att.reshape(1, att_w), g_ssm.reshape(1, ssm_w), w_out.astype(BF16),
      vec(gate1), ln_g.reshape(1, d), ln_b.reshape(1, d), vec(scale2), vec(shift2), vec(gate2),
      w_s_gate.astype(BF16), w_s_up.astype(BF16), w_s_down.astype(BF16))


def _route_kernel(h_ref, wrt_ref, wrt_lo_ref, bias_ref, tri_ref, idx_ref, gate_ref, rank_ref, cnt_ref, carry_ref):
    n_e = wrt_ref.shape[0]
    tm = h_ref.shape[0]
    per_group = n_e // N_EXPERT_GROUPS
    neg = -jnp.inf

    @pl.when(pl.program_id(0) == 0)
    def _():
        carry_ref[...] = jnp.zeros_like(carry_ref)

    h_hi, h_lo = _split_bf16(h_ref[...])
    logits = (_dot_nt(wrt_ref[...], h_hi) + _dot_nt(wrt_ref[...], h_lo)
              + _dot_nt(wrt_lo_ref[...], h_hi))
    scores = jax.nn.sigmoid(logits)
    biased = scores + bias_ref[...]

    g3 = biased.reshape(N_EXPERT_GROUPS, per_group, tm)
    ridx = lax.broadcasted_iota(jnp.int32, g3.shape, 1).astype(F32)
    m1 = jnp.max(g3, axis=1, keepdims=True)
    first = jnp.min(jnp.where(g3 == m1, ridx, float(per_group)), axis=1, keepdims=True)
    m2 = jnp.max(jnp.where(ridx == first, neg, g3), axis=1, keepdims=True)
    gs = m1 + m2

    gidx = lax.broadcasted_iota(jnp.int32, gs.shape, 0).astype(F32)
    ok = jnp.zeros_like(gs)
    cur = gs
    for _ in range(TOPK_GROUPS):
        mx = jnp.max(cur, axis=0, keepdims=True)
        fi = jnp.min(jnp.where(cur == mx, gidx, float(N_EXPERT_GROUPS)), axis=0, keepdims=True)
        hit = gidx == fi
        ok = jnp.where(hit, 1.0, ok)
        cur = jnp.where(hit, neg, cur)
    masked = jnp.where(ok > 0.5, g3, neg).reshape(n_e, tm)

    eidx = lax.broadcasted_iota(jnp.int32, (n_e, tm), 0).astype(F32)
    onehot = jnp.zeros((n_e, tm), F32)
    cur = masked
    sel_idx = []
    sel_gate = []
    for _ in range(TOP_K):
        mx = jnp.max(cur, axis=0, keepdims=True)
        fi = jnp.min(jnp.where(cur == mx, eidx, float(n_e)), axis=0, keepdims=True)
        hit = eidx == fi
        sel_idx.append(fi)
        sel_gate.append(jnp.sum(jnp.where(hit, scores, 0.0), axis=0, keepdims=True))
        onehot = jnp.where(hit, 1.0, onehot)
        cur = jnp.where(hit, neg, cur)
    idx = jnp.concatenate(sel_idx, axis=0)
    gate = jnp.concatenate(sel_gate, axis=0)
    gate = gate / jnp.sum(gate, axis=0, keepdims=True) * ROUTED_SCALE

    prior = _dot(onehot.astype(BF16), tri_ref[...]) + carry_ref[:, 0:1]
    ranks = [jnp.sum(jnp.where(eidx == sel_idx[k], prior, 0.0), axis=0, keepdims=True)
             for k in range(TOP_K)]
    rank = jnp.concatenate(ranks, axis=0)
    carry = carry_ref[...] + jnp.sum(onehot, axis=1, keepdims=True)
    carry_ref[...] = carry
    cnt_ref[...] = carry

    idx_ref[...] = idx.astype(jnp.int32)
    gate_ref[...] = gate
    rank_ref[...] = rank.astype(jnp.int32)


def _route(h2d, w_router, router_bias):
    n_tok, d = h2d.shape
    n_e = w_router.shape[1]
    tm = min(ROUTE_COLS, n_tok)
    tri = (np.arange(tm)[:, None] < np.arange(tm)[None, :]).astype(np.float32)
    w_hi, w_lo = _split_bf16(w_router.T.astype(F32))
    col = lambda i: (0, i)
    full = lambda *shape: pl.BlockSpec(shape, lambda i: (0,) * len(shape))
    idx, gate, rank, cnt = pl.pallas_call(
        _route_kernel,
        grid=(n_tok // tm,),
        in_specs=[pl.BlockSpec((tm, d), lambda i: (i, 0)), full(n_e, d), full(n_e, d), full(n_e, 1),
                  full(tm, tm)],
        out_specs=[pl.BlockSpec((TOP_K, tm), col), pl.BlockSpec((TOP_K, tm), col),
                   pl.BlockSpec((TOP_K, tm), col), full(n_e, LANES)],
        out_shape=[jax.ShapeDtypeStruct((TOP_K, n_tok), jnp.int32),
                   jax.ShapeDtypeStruct((TOP_K, n_tok), F32),
                   jax.ShapeDtypeStruct((TOP_K, n_tok), jnp.int32),
                   jax.ShapeDtypeStruct((n_e, LANES), F32)],
        scratch_shapes=[pltpu.VMEM((n_e, LANES), F32)],
        compiler_params=_cparams(("arbitrary",)),
        name="router_topk",
    )(h2d, w_hi, w_lo, router_bias.reshape(n_e, 1).astype(F32), jnp.asarray(tri, BF16))
    return idx, gate, rank, cnt[:, 0].astype(jnp.int32)


def _dispatch_kernel(pend_ref, cnt_ref, dest_ref, ht_ref, xs_ref, zero_ref, sem_ref, *, n_e, blk, tm):

    def zero_copy(e):
        start = pl.multiple_of(pend_ref[e] - blk, blk)
        return pltpu.make_async_copy(zero_ref, xs_ref.at[pl.ds(start, blk)], sem_ref.at[0])

    @pl.when(pl.program_id(0) == 0)
    def _():
        zero_ref[...] = jnp.zeros_like(zero_ref)

        def start(e, c):
            @pl.when(cnt_ref[e] > 0)
            def _():
                zero_copy(e).start()
            return c
        lax.fori_loop(0, n_e, start, 0)

        def wait(e, c):
            @pl.when(cnt_ref[e] > 0)
            def _():
                zero_copy(e).wait()
            return c
        lax.fori_loop(0, n_e, wait, 0)

    n_sub = ht_ref.shape[0] // tm

    def row_copy(t, dst_row):
        src = ht_ref.at[pl.ds(pl.multiple_of(t * n_sub, n_sub), n_sub)]
        return pltpu.make_async_copy(src, xs_ref.at[dst_row], sem_ref.at[1])

    def issue(g, c):
        t0 = g * DMA_ISSUE_UNROLL
        rows = [[dest_ref[(t0 + j) * TOP_K + k] for k in range(TOP_K)] for j in range(DMA_ISSUE_UNROLL)]
        for j in range(DMA_ISSUE_UNROLL):
            for k in range(TOP_K):
                row_copy(t0 + j, rows[j][k]).start(priority=k % 2)
        return c
    lax.fori_loop(0, tm // DMA_ISSUE_UNROLL, issue, 0)

    def drain(t, c):
        for k in range(TOP_K):
            row_copy(t, dest_ref[t * TOP_K + k]).wait()
        return c
    lax.fori_loop(0, tm, drain, 0)


def _dispatch(h_tiles, n_sub, dest, pend, cnt, n_rows):
    n_tok = h_tiles.shape[0] // n_sub
    n_e = cnt.shape[0]
    tm = min(DISPATCH_ROWS, n_tok)
    grid_spec = pltpu.PrefetchScalarGridSpec(
        num_scalar_prefetch=2,
        grid=(n_tok // tm,),
        in_specs=[pl.BlockSpec((TOP_K * tm,), lambda i, *_: (i,), memory_space=pltpu.SMEM),
                  pl.BlockSpec((tm * n_sub, LANES), lambda i, *_: (i, 0))],
        out_specs=pl.BlockSpec(memory_space=pl.ANY),
        scratch_shapes=[pltpu.VMEM((MOE_ROWS, n_sub, LANES), F32), pltpu.SemaphoreType.DMA((2,))],
    )
    return pl.pallas_call(
        functools.partial(_dispatch_kernel, n_e=n_e, blk=MOE_ROWS, tm=tm),
        grid_spec=grid_spec,
        out_shape=jax.ShapeDtypeStruct((n_rows, n_sub, LANES), F32),
        compiler_params=_cparams(("arbitrary",)),
        name="moe_dispatch",
    )(pend, cnt, dest, h_tiles)


def _expert_kernel(be_ref, nused_ref, next_ref, slot_ref, x_ref, wg_ref, wu_ref, wd_ref, y_ref,
                   wgf_ref, wuf_ref, wdf_ref, wgb_ref, wub_ref, wdb_ref, sem_ref):
    i = pl.program_id(0)
    active = i < nused_ref[0]
    prev = be_ref[jnp.maximum(i - 1, 0)]
    fresh = jnp.logical_and(active, jnp.logical_or(i == 0, be_ref[i] != prev))

    def weight_copies(e, slot):
        return [pltpu.make_async_copy(src.at[e], dst.at[slot], sem_ref.at[slot])
                for src, dst in ((wg_ref, wgf_ref), (wu_ref, wuf_ref), (wd_ref, wdf_ref))]

    def start_weights(e, slot):
        for cp in weight_copies(e, slot):
            cp.start(priority=1)

    @pl.when(jnp.logical_and(active, i == 0))
    def _():
        start_weights(be_ref[0], 0)

    @pl.when(fresh)
    def _():
        slot = slot_ref[i]
        for cp in weight_copies(be_ref[i], slot):
            cp.wait()
        wgb_ref[...] = wgf_ref[slot].astype(BF16)
        wub_ref[...] = wuf_ref[slot].astype(BF16)
        wdb_ref[...] = wdf_ref[slot].astype(BF16)

        @pl.when(next_ref[i] >= 0)
        def _():
            start_weights(next_ref[i], 1 - slot)

    @pl.when(active)
    def _():
        d = wgb_ref.shape[0]
        xb = jnp.concatenate(_load_token_tiles(x_ref, MOE_ROWS, d), axis=1).astype(BF16)
        hid = _silu(_dot(xb, wgb_ref[...])) * _dot(xb, wub_ref[...])
        _store_token_tiles(y_ref, _dot(hid.astype(BF16), wdb_ref[...]))


def _experts(xs_tiles, plan, w_gate, w_up, w_down):
    n_rows, n_sub, _ = xs_tiles.shape
    d = n_sub * LANES
    n_blk = n_rows // MOE_ROWS
    ff = w_gate.shape[-1]
    rows_spec = pl.BlockSpec((MOE_ROWS * n_sub, LANES), lambda i, be, nu, *_: (jnp.minimum(i, nu[0] - 1), 0))
    hbm = pl.BlockSpec(memory_space=pl.ANY)
    grid_spec = pltpu.PrefetchScalarGridSpec(
        num_scalar_prefetch=4,
        grid=(n_blk,),
        in_specs=[rows_spec, hbm, hbm, hbm],
        out_specs=rows_spec,
        scratch_shapes=[pltpu.VMEM((2, d, ff), F32), pltpu.VMEM((2, d, ff), F32), pltpu.VMEM((2, ff, d), F32),
                        pltpu.VMEM((d, ff), BF16), pltpu.VMEM((d, ff), BF16), pltpu.VMEM((ff, d), BF16),
                        pltpu.SemaphoreType.DMA((2,))],
    )
    ys = pl.pallas_call(
        _expert_kernel,
        grid_spec=grid_spec,
        out_shape=jax.ShapeDtypeStruct((n_rows * n_sub, LANES), F32),
        compiler_params=_cparams(("arbitrary",)),
        name="moe_experts",
    )(*plan, xs_tiles.reshape(n_rows * n_sub, LANES), w_gate, w_up, w_down)
    return ys.reshape(n_rows, n_sub, LANES)


def _combine_kernel(dest_ref, ys_ref, gate_ref, pre_ref, g2_ref, ln_g_ref, ln_b_ref, o_ref, buf_ref, sem_ref):
    tm, d = pre_ref.shape[1], pre_ref.shape[2]
    n_sub = d // LANES

    def row_copy(t, k, src_row):
        slot = pl.multiple_of((k * tm + t) * n_sub, n_sub)
        return pltpu.make_async_copy(ys_ref.at[src_row], buf_ref.at[pl.ds(slot, n_sub)], sem_ref.at[0])

    def issue(g, c):
        t0 = g * DMA_ISSUE_UNROLL
        rows = [[dest_ref[(t0 + j) * TOP_K + k] for k in range(TOP_K)] for j in range(DMA_ISSUE_UNROLL)]
        for j in range(DMA_ISSUE_UNROLL):
            for k in range(TOP_K):
                row_copy(t0 + j, k, rows[j][k]).start(priority=k % 2)
        return c
    lax.fori_loop(0, tm // DMA_ISSUE_UNROLL, issue, 0)

    def drain(t, c):
        for k in range(TOP_K):
            row_copy(t, k, dest_ref[t * TOP_K + k]).wait()
        return c
    lax.fori_loop(0, tm, drain, 0)

    gates = gate_ref[...]
    gate_cols = [jnp.broadcast_to(gates[:, k:k + 1], (tm, LANES)) for k in range(TOP_K)]
    pieces = []
    for s in range(n_sub):
        acc = None
        for k in range(TOP_K):
            term = buf_ref[pl.ds(k * tm * n_sub + s, tm, stride=n_sub), :] * gate_cols[k]
            acc = term if acc is None else acc + term
        pieces.append(acc)
    routed = jnp.concatenate(pieces, axis=1)
    y = pre_ref[0] + g2_ref[0] * routed
    o_ref[0] = _layer_norm(y, ln_g_ref[...], ln_b_ref[...])


def _combine(ys, dest, gates_t, pre, gate2, ln_g, ln_b):
    bsz, seq, d = pre.shape
    tm = min(DISPATCH_ROWS, seq)
    per_seq = seq // tm
    row = lambda i, *_: (i // per_seq, i % per_seq, 0)
    per_b = lambda i, *_: (i // per_seq, 0, 0)
    full = lambda *shape: pl.BlockSpec(shape, lambda i, *_: (0,) * len(shape))
    grid_spec = pltpu.PrefetchScalarGridSpec(
        num_scalar_prefetch=0,
        grid=(bsz * per_seq,),
        in_specs=[pl.BlockSpec((TOP_K * tm,), lambda i: (i,), memory_space=pltpu.SMEM),
                  pl.BlockSpec(memory_space=pl.ANY),
                  pl.BlockSpec((tm, TOP_K), lambda i: (i, 0)),
                  pl.BlockSpec((1, tm, d), row), pl.BlockSpec((1, 1, d), per_b), full(1, d), full(1, d)],
        out_specs=pl.BlockSpec((1, tm, d), row),
        scratch_shapes=[pltpu.VMEM((TOP_K * tm * (d // LANES), LANES), F32), pltpu.SemaphoreType.DMA((1,))],
    )
    return pl.pallas_call(
        _combine_kernel,
        grid_spec=grid_spec,
        out_shape=jax.ShapeDtypeStruct((bsz, seq, d), F32),
        compiler_params=_cparams(("arbitrary",)),
        name="moe_combine_ln2",
    )(dest, ys, gates_t, pre, gate2.reshape(bsz, 1, d), ln_g.reshape(1, d), ln_b.reshape(1, d))


def _dest_kernel(pstart_ref, idx_ref, rank_ref, dest_ref):
    idx = idx_ref[...]

    def body(e, acc):
        return acc + jnp.where(idx == e, pstart_ref[e], 0)
    dest_ref[...] = lax.fori_loop(0, pstart_ref.shape[0], body, rank_ref[...], unroll=8)


def _dest_rows(pstart, idx, rank):
    n_k, n_tok = idx.shape
    tn = min(4096, n_tok)
    grid_spec = pltpu.PrefetchScalarGridSpec(
        num_scalar_prefetch=1,
        grid=(n_tok // tn,),
        in_specs=[pl.BlockSpec((n_k, tn), lambda i, *_: (0, i))] * 2,
        out_specs=pl.BlockSpec((n_k, tn), lambda i, *_: (0, i)),
    )
    return pl.pallas_call(
        _dest_kernel,
        grid_spec=grid_spec,
        out_shape=jax.ShapeDtypeStruct((n_k, n_tok), jnp.int32),
        compiler_params=_cparams(("arbitrary",)),
        name="moe_dest",
    )(pstart, idx, rank)


def _moe_plan(idx, rank, cnt, n_tok):
    n_e = cnt.shape[0]
    padded = (cnt + MOE_ROWS - 1) // MOE_ROWS * MOE_ROWS
    pend = jnp.cumsum(padded).astype(jnp.int32)
    pstart = pend - padded
    dest = _dest_rows(pstart, idx, rank).T.reshape(-1)
    n_blk = n_tok * TOP_K // MOE_ROWS + n_e
    starts = jnp.arange(n_blk, dtype=jnp.int32) * MOE_ROWS
    block_expert = jnp.minimum(jnp.sum((pend[None, :] <= starts[:, None]).astype(jnp.int32), axis=1),
                               n_e - 1).astype(jnp.int32)
    n_used = (pend[-1:] // MOE_ROWS).astype(jnp.int32)
    ids = jnp.arange(n_e, dtype=jnp.int32)
    later = jnp.where((cnt[None, :] > 0) & (ids[None, :] > ids[:, None]), ids[None, :], n_e)
    nxt = jnp.min(later, axis=1)
    nxt = jnp.where(nxt == n_e, -1, nxt).astype(jnp.int32)
    parity = ((jnp.cumsum((cnt > 0).astype(jnp.int32)) - 1) & 1).astype(jnp.int32)
    plan = (block_expert, n_used, jnp.take(nxt, block_expert), jnp.take(parity, block_expert))
    return dest, pend, plan, n_blk * MOE_ROWS


def kernel(x, c, rel_bias, w_ada, b_ada, w_in, ssm_a_re, ssm_a_im, ssm_b_re, ssm_b_im, ssm_c_re, ssm_c_im, ssm_d, ssm_log_dt, w_glu, b_glu, g_att, g_ssm, w_out, ln1_g, ln1_b, w_router, router_bias, w_e_gate, w_e_up, w_e_down, w_s_gate, w_s_up, w_s_down, ln2_g, ln2_b):
    bsz, seq, d = x.shape
    depth = w_ada.shape[0]
    alpha = (2 * depth) ** 0.25
    att_w = g_att.shape[-1]
    tables = _att_tables(rel_bias)
    for layer in range(depth):
        ada = _ada(c, w_ada[layer], b_ada[layer])
        shift1, scale1, gate1, shift2, scale2, gate2 = jnp.split(ada, 6, axis=-1)
        q, k, v, u = _inproj(x, scale1, shift1, w_in[layer], att_w)
        att = _attention(q, k, v, tables)
        ssm_prm = _ssm_params(ssm_a_re[layer], ssm_a_im[layer], ssm_b_re[layer], ssm_b_im[layer],
                              ssm_c_re[layer], ssm_c_im[layer], ssm_d[layer], ssm_log_dt[layer])
        ssm = _ssm(u, *ssm_prm, w_glu[layer], b_glu[layer])
        h2, h2_tiles, pre = _mix(x, att, ssm, g_att[layer], g_ssm[layer], w_out[layer], gate1,
                                 ln1_g[layer], ln1_b[layer], scale2, shift2, gate2, w_s_gate[layer],
                                 w_s_up[layer], w_s_down[layer], alpha)
        h2d = h2.reshape(bsz * seq, d)
        idx, gate, rank, cnt = _route(h2d, w_router[layer], router_bias[layer])
        dest, pend, plan, n_rows = _moe_plan(idx, rank, cnt, bsz * seq)
        xs = _dispatch(h2_tiles, d // LANES, dest, pend, cnt, n_rows)
        ys = _experts(xs, plan, w_e_gate[layer], w_e_up[layer], w_e_down[layer])
        x = _combine(ys, dest, gate.T, pre, gate2, ln2_g[layer], ln2_b[layer])
    return x
```

```python
import functools
import math

import jax
import jax.numpy as jnp
import numpy as np
from jax import lax
from jax.experimental import pallas as pl
from jax.experimental.pallas import tpu as pltpu

F32 = jnp.float32
BF16 = jnp.bfloat16

HEAD_DIM = 64
ATT_BLOCK = 128
PATTERNS = ((128, 1), (512, 4), (2048, 16))
N_BUCKETS = 32
MAX_DISTANCE = 2048
SSM_GROUP = 16
SSM_STATE = 64
N_EXPERTS = 256
TOP_K = 8
N_EXPERT_GROUPS = 8
TOPK_GROUPS = 4
ROUTED_SCALE = 2.5
EPS = 1e-5
NEG_INF = -1e30

LANES = 128
SUBLANES = 8
VMEM_LIMIT_BYTES = 56 * 1024 * 1024

PROJ_ROWS = 512
SSM_ROWS = 256
MIX_ROWS = 512
ROUTE_COLS = 512
MOE_ROWS = 256
DISPATCH_ROWS = 256
ATT_UNITS_PER_STEP = 8
DMA_ISSUE_UNROLL = 4


def _cparams(sem, vmem=VMEM_LIMIT_BYTES):
    return pltpu.CompilerParams(dimension_semantics=sem, vmem_limit_bytes=vmem)


def _dot(a, b):
    return jnp.dot(a, b, preferred_element_type=F32)


def _dot_nt(a, b):
    return lax.dot_general(a, b, (((1,), (1,)), ((), ())), preferred_element_type=F32)


def _silu(x):
    return x * jax.nn.sigmoid(x)


def _store_token_tiles(ref, x):
    m, d = x.shape
    n_sub = d // LANES
    for s in range(n_sub):
        ref[pl.ds(s, m, stride=n_sub), :] = x[:, s * LANES:(s + 1) * LANES]


def _load_token_tiles(ref, m, d, row0=0):
    n_sub = d // LANES
    return [ref[pl.ds(row0 + s, m, stride=n_sub), :] for s in range(n_sub)]


def _split_bf16(x):
    hi = x.astype(BF16)
    lo = (x - hi.astype(F32)).astype(BF16)
    return hi, lo


def _ada_kernel(c_ref, w_ref, b_ref, o_ref):
    c = c_ref[...]
    a_hi, a_lo = _split_bf16(_silu(c))
    w_hi, w_lo = _split_bf16(w_ref[...])
    acc = _dot(a_hi, w_hi) + _dot(a_hi, w_lo) + _dot(a_lo, w_hi)
    o_ref[...] = acc + b_ref[...]


def _ada(c, w_ada, b_ada):
    bsz, d = c.shape
    n = w_ada.shape[1]
    rows = SUBLANES
    c_pad = jnp.zeros((rows, d), F32).at[:bsz].set(c)
    tn = 1024
    out = pl.pallas_call(
        _ada_kernel,
        grid=(n // tn,),
        in_specs=[pl.BlockSpec((rows, d), lambda j: (0, 0)),
                  pl.BlockSpec((d, tn), lambda j: (0, j)),
                  pl.BlockSpec((1, tn), lambda j: (0, j))],
        out_specs=pl.BlockSpec((rows, tn), lambda j: (0, j)),
        out_shape=jax.ShapeDtypeStruct((rows, n), F32),
        compiler_params=_cparams(("parallel",)),
        name="ada",
    )(c_pad, w_ada, b_ada.reshape(1, n))
    return out[:bsz]


def _inproj_kernel(x_ref, sc_ref, sh_ref, w_ref, q_ref, k_ref, v_ref, u_ref, *, att_w, q_scale):
    h = (x_ref[0] * (1.0 + sc_ref[0]) + sh_ref[0]).astype(BF16)
    q_ref[0] = _dot(h, w_ref[:, 0:att_w]) * q_scale
    k_ref[0] = _dot(h, w_ref[:, att_w:2 * att_w])
    v_ref[0] = _dot(h, w_ref[:, 2 * att_w:3 * att_w])
    u_ref[0] = _dot(h, w_ref[:, 3 * att_w:])


def _inproj(x, scale, shift, w_in, att_w):
    bsz, seq, d = x.shape
    n = w_in.shape[1]
    ssm_w = n - 3 * att_w
    ts = min(PROJ_ROWS, seq)
    row = lambda b, i: (b, i, 0)
    per_b = lambda b, i: (b, 0, 0)
    kern = functools.partial(_inproj_kernel, att_w=att_w, q_scale=HEAD_DIM ** -0.5)
    return pl.pallas_call(
        kern,
        grid=(bsz, seq // ts),
        in_specs=[pl.BlockSpec((1, ts, d), row),
                  pl.BlockSpec((1, 1, d), per_b),
                  pl.BlockSpec((1, 1, d), per_b),
                  pl.BlockSpec((d, n), lambda b, i: (0, 0))],
        out_specs=[pl.BlockSpec((1, ts, att_w), row)] * 3 + [pl.BlockSpec((1, ts, ssm_w), row)],
        out_shape=[jax.ShapeDtypeStruct((bsz, seq, att_w), F32)] * 3
                  + [jax.ShapeDtypeStruct((bsz, seq, ssm_w), F32)],
        compiler_params=_cparams(("parallel", "parallel")),
        name="inproj",
    )(x, scale.reshape(bsz, 1, d), shift.reshape(bsz, 1, d), w_in.astype(BF16))


def _t5_bucket_np(dist):
    exact = N_BUCKETS // 2
    large = exact + (np.log(np.maximum(dist, 1).astype(np.float64) / exact)
                     / math.log(MAX_DISTANCE / exact) * (N_BUCKETS - exact)).astype(np.int64)
    return np.where(dist < exact, dist, np.minimum(large, N_BUCKETS - 1))


def _att_tables(rel_bias):
    qi = np.arange(ATT_BLOCK)[:, None]
    ki = np.arange(2 * ATT_BLOCK)[None, :]
    rel = qi + ATT_BLOCK - ki
    tabs = []
    for window, dil in PATTERNS:
        band = (rel >= 0) & (rel <= window // dil)
        bucket = _t5_bucket_np(np.maximum(rel, 0) * dil)
        onehot = (bucket[:, :, None] == np.arange(N_BUCKETS)[None, None, :]).astype(np.float32)
        bias = jnp.einsum('qkb,bh->hqk', onehot, rel_bias.astype(F32),
                          precision=lax.Precision.HIGHEST)
        full = jnp.where(band[None], bias, NEG_INF)
        first = jnp.concatenate([full[:, :, ATT_BLOCK:], jnp.full_like(full[:, :, ATT_BLOCK:], NEG_INF)], axis=-1)
        tabs.append(jnp.stack([full, first]))
    return jnp.stack(tabs)


def _att_kernel(q_ref, k_ref, v_ref, tab_ref, o_ref, m_ref, l_ref, *, seq):
    lane = lax.broadcasted_iota(jnp.int32, (ATT_BLOCK, LANES), 1)
    head0 = lane < HEAD_DIM

    def rows(ref, start, n, dil):
        if dil == 1:
            return ref[0, pl.ds(pl.multiple_of(start, ATT_BLOCK), n), :]
        return ref[0, pl.ds(start, n, stride=dil), :]

    def lane_mask(n_rows, h):
        m = lax.broadcasted_iota(jnp.int32, (n_rows, LANES), 1) < HEAD_DIM
        return m if h == 0 else ~m

    def batch(pi, dil, nb, res, blk0, head, init):
        q_rows = nb * ATT_BLOCK
        kv_blocks = nb if head else nb + 1
        q_start = res + dil * ATT_BLOCK * blk0
        kv_start = q_start if head else q_start - dil * ATT_BLOCK
        if dil == 1:
            sl = pl.ds(pl.multiple_of(q_start, ATT_BLOCK), q_rows)
        else:
            sl = pl.ds(q_start, q_rows, stride=dil)
        q = rows(q_ref, q_start, q_rows, dil)
        k = rows(k_ref, kv_start, kv_blocks * ATT_BLOCK, dil).astype(BF16)
        v = rows(v_ref, kv_start, kv_blocks * ATT_BLOCK, dil)
        if not init:
            m_old, l_old, o_old = m_ref[sl, :], l_ref[sl, :], o_ref[0, sl, :]
        qh = [jnp.where(lane_mask(q_rows, h), q, 0.0).astype(BF16) for h in range(2)]
        vh = [jnp.where(lane_mask(kv_blocks * ATT_BLOCK, h), v, 1.0).astype(BF16) for h in range(2)]
        o_parts, l_parts, m_parts = [], [], []
        for j in range(nb):
            first = 1 if (head and j == 0) else 0
            kb = j if (not head or j == 0) else j - 1
            ksl = slice(kb * ATT_BLOCK, (kb + 2) * ATT_BLOCK)
            qsl = slice(j * ATT_BLOCK, (j + 1) * ATT_BLOCK)
            outs, ms = [], []
            for h in range(2):
                s = _dot_nt(qh[h][qsl], k[ksl]) + tab_ref[pi, first, h]
                m = jnp.max(s, axis=1, keepdims=True)
                p = jnp.exp(s - m).astype(BF16)
                outs.append(_dot(p, vh[h][ksl]))
                ms.append(m)
            o_parts.append(jnp.where(head0, outs[0], outs[1]))
            l_parts.append(pltpu.roll(jnp.where(head0, outs[1], outs[0]), HEAD_DIM, axis=1))
            m_parts.append(jnp.where(head0, ms[0], ms[1]))
        o = jnp.concatenate(o_parts, axis=0)
        l = jnp.concatenate(l_parts, axis=0)
        m = jnp.concatenate(m_parts, axis=0)
        if not init:
            m_new = jnp.maximum(m_old, m)
            a_old = jnp.exp(m_old - m_new)
            a_cur = jnp.exp(m - m_new)
            o = o_old * a_old + o * a_cur
            l = l_old * a_old + l * a_cur
            m = m_new
        o_ref[0, sl, :] = o
        m_ref[sl, :] = m
        l_ref[sl, :] = l

    n_units = seq // ATT_BLOCK
    order = sorted(range(len(PATTERNS)), key=lambda p: -PATTERNS[p][1])
    for pos, pi in enumerate(order):
        dil = PATTERNS[pi][1]
        log_d = dil.bit_length() - 1
        res_blocks = seq // (dil * ATT_BLOCK)
        nb = min(ATT_UNITS_PER_STEP, res_blocks)
        per_res = res_blocks // nb
        init = pos == 0

        def head_body(res, carry, pi=pi, dil=dil, nb=nb, init=init):
            batch(pi, dil, nb, res, 0, True, init)
            return carry
        lax.fori_loop(0, dil, head_body, 0)

        if per_res > 1:
            def tail_body(i, carry, pi=pi, dil=dil, nb=nb, log_d=log_d, init=init):
                res = i & (dil - 1)
                blk0 = nb * (1 + (i >> log_d))
                batch(pi, dil, nb, res, blk0, False, init)
                return carry
            lax.fori_loop(0, dil * (per_res - 1), tail_body, 0)

    def finish(i, carry):
        sl = pl.ds(pl.multiple_of(i * ATT_BLOCK, ATT_BLOCK), ATT_BLOCK)
        o_ref[0, sl, :] = o_ref[0, sl, :] / l_ref[sl, :]
        return carry
    lax.fori_loop(0, n_units, finish, 0)


def _attention(q, k, v, tables):
    bsz, seq, att_w = q.shape
    n_pairs = att_w // LANES
    blk = lambda b, p: (b, 0, p)
    qkv_spec = pl.BlockSpec((1, seq, LANES), blk)
    n_pat = tables.shape[0]
    tab_spec = pl.BlockSpec((n_pat, 2, 2, ATT_BLOCK, 2 * ATT_BLOCK), lambda b, p: (0, 0, p, 0, 0))
    return pl.pallas_call(
        functools.partial(_att_kernel, seq=seq),
        grid=(bsz, n_pairs),
        in_specs=[qkv_spec, qkv_spec, qkv_spec, tab_spec],
        out_specs=pl.BlockSpec((1, seq, LANES), blk),
        out_shape=jax.ShapeDtypeStruct((bsz, seq, att_w), F32),
        scratch_shapes=[pltpu.VMEM((seq, LANES), F32), pltpu.VMEM((seq, LANES), F32)],
        compiler_params=_cparams(("parallel", "parallel")),
        name="dilated_attention",
    )(q, k, v, tables)


def _ssm_params(a_re, a_im, b_re, b_im, c_re, c_im, d_skip, log_dt):
    n_g, n_p = a_re.shape
    n_c = b_re.shape[-1]
    n_state = n_g * n_p
    dt = jnp.exp(log_dt.astype(F32))[:, None]

    def a_pow(kk):
        mag = jnp.exp(kk * dt * a_re)
        ph = kk * dt * a_im
        return mag * jnp.cos(ph), mag * jnp.sin(ph)

    ab_re, ab_im = a_pow(1.0)
    nr, ni = ab_re - 1.0, ab_im
    den = a_re * a_re + a_im * a_im
    f_re = (nr * a_re + ni * a_im) / den
    f_im = (ni * a_re - nr * a_im) / den
    bb_re = f_re[:, :, None] * b_re - f_im[:, :, None] * b_im
    bb_im = f_re[:, :, None] * b_im + f_im[:, :, None] * b_re
    eye = jnp.eye(n_g, dtype=F32)
    w_re = jnp.einsum('gpc,gh->gchp', bb_re, eye).reshape(n_g * n_c, n_state)
    w_im = jnp.einsum('gpc,gh->gchp', bb_im, eye).reshape(n_g * n_c, n_state)
    o_re = jnp.einsum('gcp,gh->gphc', c_re, eye).reshape(n_state, n_g * n_c)
    o_im = jnp.einsum('gcp,gh->gphc', c_im, eye).reshape(n_state, n_g * n_c)
    n_kb = (n_g * n_c) // LANES
    sp = n_state // n_kb
    w_in = jnp.stack([jnp.concatenate([w_re[kb * LANES:(kb + 1) * LANES, kb * sp:(kb + 1) * sp],
                                       w_im[kb * LANES:(kb + 1) * LANES, kb * sp:(kb + 1) * sp]], axis=1)
                      for kb in range(n_kb)])
    w_out = jnp.stack([jnp.concatenate([o_re[kb * sp:(kb + 1) * sp, kb * LANES:(kb + 1) * LANES],
                                        -o_im[kb * sp:(kb + 1) * sp, kb * LANES:(kb + 1) * LANES]], axis=0)
                       for kb in range(n_kb)])
    j = jnp.arange(SUBLANES, dtype=F32)[:, None]
    flat = lambda t: jnp.broadcast_to(t.reshape(1, n_state), (SUBLANES, n_state))
    coef = []
    for sh in (1, 2, 4):
        pr, pim = a_pow(float(sh))
        keep = (j >= sh).astype(F32)
        coef += [flat(pr) * keep, flat(pim) * keep]
    a_re_f = jnp.broadcast_to(a_re.reshape(1, n_state), (SUBLANES, n_state))
    a_im_f = jnp.broadcast_to(a_im.reshape(1, n_state), (SUBLANES, n_state))
    dt_f = jnp.broadcast_to(jnp.repeat(dt[:, 0], n_p).reshape(1, n_state), (SUBLANES, n_state))
    mag = jnp.exp((j + 1.0) * dt_f * a_re_f)
    ph = (j + 1.0) * dt_f * a_im_f
    coef += [mag * jnp.cos(ph), mag * jnp.sin(ph)]
    coef = jnp.stack(coef)
    return w_in.astype(BF16), w_out.astype(BF16), coef, d_skip.reshape(1, n_g * n_c).astype(F32)


def _gelu_tanh(x):
    return 0.5 * x * (1.0 + jnp.tanh(math.sqrt(2.0 / math.pi) * (x + 0.044715 * (x * x * x))))


def _ssm_kernel(u_ref, win_ref, wout_ref, coef_ref, d_ref, wglu_ref, bglu_ref, o_ref,
                hre_ref, him_ref, cre_ref, cim_ref, *, rows, n_state, col_w):
    n_kb = win_ref.shape[0]
    sp = n_state // n_kb

    @pl.when(pl.program_id(1) == 0)
    def _():
        cre_ref[...] = jnp.zeros_like(cre_ref)
        cim_ref[...] = jnp.zeros_like(cim_ref)

    u = u_ref[0]
    ub = u.astype(BF16)
    for kb in range(n_kb):
        bu = _dot(ub[:, kb * LANES:(kb + 1) * LANES], win_ref[kb])
        hre_ref[:, kb * sp:(kb + 1) * sp] = bu[:, :sp]
        him_ref[:, kb * sp:(kb + 1) * sp] = bu[:, sp:]

    n_slab = rows // SUBLANES
    for c0 in range(0, n_state, col_w):
        cs = slice(c0, c0 + col_w)

        def body(i, carry, cs=cs):
            c_re, c_im = carry
            sl = pl.ds(pl.multiple_of(i * SUBLANES, SUBLANES), SUBLANES)
            x_re = hre_ref[sl, cs]
            x_im = him_ref[sl, cs]
            for si, sh in enumerate((1, 2, 4)):
                p_re = coef_ref[2 * si, :, cs]
                p_im = coef_ref[2 * si + 1, :, cs]
                s_re = pltpu.roll(x_re, sh, axis=0)
                s_im = pltpu.roll(x_im, sh, axis=0)
                x_re, x_im = (x_re + p_re * s_re - p_im * s_im,
                              x_im + p_re * s_im + p_im * s_re)
            p_re = coef_ref[6, :, cs]
            p_im = coef_ref[7, :, cs]
            h_re = x_re + p_re * c_re - p_im * c_im
            h_im = x_im + p_re * c_im + p_im * c_re
            hre_ref[sl, cs] = h_re
            him_ref[sl, cs] = h_im
            last = slice(SUBLANES - 1, SUBLANES)
            return (jnp.broadcast_to(h_re[last, :], h_re.shape),
                    jnp.broadcast_to(h_im[last, :], h_im.shape))

        c_re, c_im = lax.fori_loop(0, n_slab, body, (cre_ref[:, cs], cim_ref[:, cs]))
        cre_ref[:, cs] = c_re
        cim_ref[:, cs] = c_im

    ys = []
    for kb in range(n_kb):
        hcat = jnp.concatenate([hre_ref[:, kb * sp:(kb + 1) * sp].astype(BF16),
                                him_ref[:, kb * sp:(kb + 1) * sp].astype(BF16)], axis=1)
        ys.append(_dot(hcat, wout_ref[kb]))
    y = jnp.concatenate(ys, axis=1) + d_ref[...] * u
    y = _gelu_tanh(y)
    z = _dot(y.astype(BF16), wglu_ref[...]) + bglu_ref[...]
    o_ref[0] = y * jax.nn.sigmoid(z)


def _ssm(u, w_in, w_out, coef, d_flat, w_glu, b_glu):
    bsz, seq, ssm_w = u.shape
    n_state = coef.shape[-1]
    rows = min(SSM_ROWS, seq)
    full = lambda *shape: pl.BlockSpec(shape, lambda b, i: (0,) * len(shape))
    kern = functools.partial(_ssm_kernel, rows=rows, n_state=n_state, col_w=4 * LANES)
    return pl.pallas_call(
        kern,
        grid=(bsz, seq // rows),
        in_specs=[pl.BlockSpec((1, rows, ssm_w), lambda b, i: (b, i, 0)),
                  full(*w_in.shape), full(*w_out.shape), full(*coef.shape), full(1, ssm_w),
                  full(ssm_w, ssm_w), full(1, ssm_w)],
        out_specs=pl.BlockSpec((1, rows, ssm_w), lambda b, i: (b, i, 0)),
        out_shape=jax.ShapeDtypeStruct((bsz, seq, ssm_w), F32),
        scratch_shapes=[pltpu.VMEM((rows, n_state), F32), pltpu.VMEM((rows, n_state), F32),
                        pltpu.VMEM((SUBLANES, n_state), F32), pltpu.VMEM((SUBLANES, n_state), F32)],
        compiler_params=_cparams(("parallel", "arbitrary")),
        name="s5_glu",
    )(u, w_in, w_out, coef, d_flat, w_glu.astype(BF16), b_glu.reshape(1, ssm_w))


def _layer_norm(y, g, b):
    mu = jnp.mean(y, axis=-1, keepdims=True)
    yc = y - mu
    var = jnp.mean(yc * yc, axis=-1, keepdims=True)
    return yc * lax.rsqrt(var + EPS) * g + b


def _rms_norm(y, g):
    return y * lax.rsqrt(jnp.mean(y * y, axis=-1, keepdims=True) + EPS) * g


def _mix_kernel(x_ref, att_ref, ssm_ref, gatt_ref, gssm_ref, wout_ref, g1_ref, ln_g_ref, ln_b_ref,
                sc_ref, sh_ref, g2_ref, wsg_ref, wsu_ref, wsd_ref, h_ref, ht_ref, pre_ref, *, alpha, att_w):
    a_n = _rms_norm(att_ref[0], gatt_ref[...]).astype(BF16)
    s_n = _rms_norm(ssm_ref[0], gssm_ref[...]).astype(BF16)
    mix = _dot(a_n, wout_ref[0:att_w, :]) + _dot(s_n, wout_ref[att_w:, :])
    x1 = _layer_norm(alpha * x_ref[0] + g1_ref[0] * mix, ln_g_ref[...], ln_b_ref[...])
    h = x1 * (1.0 + sc_ref[0]) + sh_ref[0]
    h_ref[0] = h
    _store_token_tiles(ht_ref, h)
    hb = h.astype(BF16)
    hid = _silu(_dot(hb, wsg_ref[...])) * _dot(hb, wsu_ref[...])
    shared = _dot(hid.astype(BF16), wsd_ref[...])
    pre_ref[0] = alpha * x1 + g2_ref[0] * shared


def _mix(x, att, ssm, g_att, g_ssm, w_out, gate1, ln_g, ln_b, scale2, shift2, gate2,
         w_s_gate, w_s_up, w_s_down, alpha):
    bsz, seq, d = x.shape
    att_w = att.shape[-1]
    ssm_w = ssm.shape[-1]
    ff = w_s_gate.shape[1]
    tm = min(MIX_ROWS, seq)
    n_sub = d // LANES
    row = lambda b, i: (b, i, 0)
    per_b = lambda b, i: (b, 0, 0)
    full = lambda *shape: pl.BlockSpec(shape, lambda b, i: (0,) * len(shape))
    vec = lambda t: t.reshape(bsz, 1, d)
    return pl.pallas_call(
        functools.partial(_mix_kernel, alpha=alpha, att_w=att_w),
        grid=(bsz, seq // tm),
        in_specs=[pl.BlockSpec((1, tm, d), row), pl.BlockSpec((1, tm, att_w), row),
                  pl.BlockSpec((1, tm, ssm_w), row),
                  full(1, att_w), full(1, ssm_w), full(att_w + ssm_w, d),
                  pl.BlockSpec((1, 1, d), per_b), full(1, d), full(1, d),
                  pl.BlockSpec((1, 1, d), per_b), pl.BlockSpec((1, 1, d), per_b),
                  pl.BlockSpec((1, 1, d), per_b),
                  full(d, ff), full(d, ff), full(ff, d)],
        out_specs=[pl.BlockSpec((1, tm, d), row),
                   pl.BlockSpec((tm * n_sub, LANES), lambda b, i: (b * (seq // tm) + i, 0)),
                   pl.BlockSpec((1, tm, d), row)],
        out_shape=[jax.ShapeDtypeStruct((bsz, seq, d), F32),
                   jax.ShapeDtypeStruct((bsz * seq * n_sub, LANES), F32),
                   jax.ShapeDtypeStruct((bsz, seq, d), F32)],
        compiler_params=_cparams(("parallel", "parallel")),
        name="mix_ln1_shared",
    )(x, att, ssm, g_att.reshape(1, att_w), g_ssm.reshape(1, ssm_w), w_out.astype(BF16),
      vec(gate1), ln_g.reshape(1, d), ln_b.reshape(1, d), vec(scale2), vec(shift2), vec(gate2),
      w_s_gate.astype(BF16), w_s_up.astype(BF16), w_s_down.astype(BF16))


def _route_kernel(h_ref, wrt_ref, wrt_lo_ref, bias_ref, tri_ref, idx_ref, gate_ref, rank_ref, cnt_ref, carry_ref):
    n_e = wrt_ref.shape[0]
    tm = h_ref.shape[0]
    per_group = n_e // N_EXPERT_GROUPS
    neg = -jnp.inf

    @pl.when(pl.program_id(0) == 0)
    def _():
        carry_ref[...] = jnp.zeros_like(carry_ref)

    h_hi, h_lo = _split_bf16(h_ref[...])
    logits = (_dot_nt(wrt_ref[...], h_hi) + _dot_nt(wrt_ref[...], h_lo)
              + _dot_nt(wrt_lo_ref[...], h_hi))
    scores = jax.nn.sigmoid(logits)
    biased = scores + bias_ref[...]

    g3 = biased.reshape(N_EXPERT_GROUPS, per_group, tm)
    ridx = lax.broadcasted_iota(jnp.int32, g3.shape, 1).astype(F32)
    m1 = jnp.max(g3, axis=1, keepdims=True)
    first = jnp.min(jnp.where(g3 == m1, ridx, float(per_group)), axis=1, keepdims=True)
    m2 = jnp.max(jnp.where(ridx == first, neg, g3), axis=1, keepdims=True)
    gs = m1 + m2

    gidx = lax.broadcasted_iota(jnp.int32, gs.shape, 0).astype(F32)
    ok = jnp.zeros_like(gs)
    cur = gs
    for _ in range(TOPK_GROUPS):
        mx = jnp.max(cur, axis=0, keepdims=True)
        fi = jnp.min(jnp.where(cur == mx, gidx, float(N_EXPERT_GROUPS)), axis=0, keepdims=True)
        hit = gidx == fi
        ok = jnp.where(hit, 1.0, ok)
        cur = jnp.where(hit, neg, cur)
    masked = jnp.where(ok > 0.5, g3, neg).reshape(n_e, tm)

    eidx = lax.broadcasted_iota(jnp.int32, (n_e, tm), 0).astype(F32)
    onehot = jnp.zeros((n_e, tm), F32)
    cur = masked
    sel_idx = []
    sel_gate = []
    for _ in range(TOP_K):
        mx = jnp.max(cur, axis=0, keepdims=True)
        fi = jnp.min(jnp.where(cur == mx, eidx, float(n_e)), axis=0, keepdims=True)
        hit = eidx == fi
        sel_idx.append(fi)
        sel_gate.append(jnp.sum(jnp.where(hit, scores, 0.0), axis=0, keepdims=True))
        onehot = jnp.where(hit, 1.0, onehot)
        cur = jnp.where(hit, neg, cur)
    idx = jnp.concatenate(sel_idx, axis=0)
    gate = jnp.concatenate(sel_gate, axis=0)
    gate = gate / jnp.sum(gate, axis=0, keepdims=True) * ROUTED_SCALE

    prior = _dot(onehot.astype(BF16), tri_ref[...]) + carry_ref[:, 0:1]
    ranks = [jnp.sum(jnp.where(eidx == sel_idx[k], prior, 0.0), axis=0, keepdims=True)
             for k in range(TOP_K)]
    rank = jnp.concatenate(ranks, axis=0)
    carry = carry_ref[...] + jnp.sum(onehot, axis=1, keepdims=True)
    carry_ref[...] = carry
    cnt_ref[...] = carry

    idx_ref[...] = idx.astype(jnp.int32)
    gate_ref[...] = gate
    rank_ref[...] = rank.astype(jnp.int32)


def _route(h2d, w_router, router_bias):
    n_tok, d = h2d.shape
    n_e = w_router.shape[1]
    tm = min(ROUTE_COLS, n_tok)
    tri = (np.arange(tm)[:, None] < np.arange(tm)[None, :]).astype(np.float32)
    w_hi, w_lo = _split_bf16(w_router.T.astype(F32))
    col = lambda i: (0, i)
    full = lambda *shape: pl.BlockSpec(shape, lambda i: (0,) * len(shape))
    idx, gate, rank, cnt = pl.pallas_call(
        _route_kernel,
        grid=(n_tok // tm,),
        in_specs=[pl.BlockSpec((tm, d), lambda i: (i, 0)), full(n_e, d), full(n_e, d), full(n_e, 1),
                  full(tm, tm)],
        out_specs=[pl.BlockSpec((TOP_K, tm), col), pl.BlockSpec((TOP_K, tm), col),
                   pl.BlockSpec((TOP_K, tm), col), full(n_e, LANES)],
        out_shape=[jax.ShapeDtypeStruct((TOP_K, n_tok), jnp.int32),
                   jax.ShapeDtypeStruct((TOP_K, n_tok), F32),
                   jax.ShapeDtypeStruct((TOP_K, n_tok), jnp.int32),
                   jax.ShapeDtypeStruct((n_e, LANES), F32)],
        scratch_shapes=[pltpu.VMEM((n_e, LANES), F32)],
        compiler_params=_cparams(("arbitrary",)),
        name="router_topk",
    )(h2d, w_hi, w_lo, router_bias.reshape(n_e, 1).astype(F32), jnp.asarray(tri, BF16))
    return idx, gate, rank, cnt[:, 0].astype(jnp.int32)


def _dispatch_kernel(pend_ref, cnt_ref, dest_ref, ht_ref, xs_ref, zero_ref, sem_ref, *, n_e, blk, tm):

    def zero_copy(e):
        start = pl.multiple_of(pend_ref[e] - blk, blk)
        return pltpu.make_async_copy(zero_ref, xs_ref.at[pl.ds(start, blk)], sem_ref.at[0])

    @pl.when(pl.program_id(0) == 0)
    def _():
        zero_ref[...] = jnp.zeros_like(zero_ref)

        def start(e, c):
            @pl.when(cnt_ref[e] > 0)
            def _():
                zero_copy(e).start()
            return c
        lax.fori_loop(0, n_e, start, 0)

        def wait(e, c):
            @pl.when(cnt_ref[e] > 0)
            def _():
                zero_copy(e).wait()
            return c
        lax.fori_loop(0, n_e, wait, 0)

    n_sub = ht_ref.shape[0] // tm

    def row_copy(t, dst_row):
        src = ht_ref.at[pl.ds(pl.multiple_of(t * n_sub, n_sub), n_sub)]
        return pltpu.make_async_copy(src, xs_ref.at[dst_row], sem_ref.at[1])

    def issue(g, c):
        t0 = g * DMA_ISSUE_UNROLL
        rows = [[dest_ref[(t0 + j) * TOP_K + k] for k in range(TOP_K)] for j in range(DMA_ISSUE_UNROLL)]
        for j in range(DMA_ISSUE_UNROLL):
            for k in range(TOP_K):
                row_copy(t0 + j, rows[j][k]).start(priority=k % 2)
        return c
    lax.fori_loop(0, tm // DMA_ISSUE_UNROLL, issue, 0)

    def drain(t, c):
        for k in range(TOP_K):
            row_copy(t, dest_ref[t * TOP_K + k]).wait()
        return c
    lax.fori_loop(0, tm, drain, 0)


def _dispatch(h_tiles, n_sub, dest, pend, cnt, n_rows):
    n_tok = h_tiles.shape[0] // n_sub
    n_e = cnt.shape[0]
    tm = min(DISPATCH_ROWS, n_tok)
    grid_spec = pltpu.PrefetchScalarGridSpec(
        num_scalar_prefetch=2,
        grid=(n_tok // tm,),
        in_specs=[pl.BlockSpec((TOP_K * tm,), lambda i, *_: (i,), memory_space=pltpu.SMEM),
                  pl.BlockSpec((tm * n_sub, LANES), lambda i, *_: (i, 0))],
        out_specs=pl.BlockSpec(memory_space=pl.ANY),
        scratch_shapes=[pltpu.VMEM((MOE_ROWS, n_sub, LANES), F32), pltpu.SemaphoreType.DMA((2,))],
    )
    return pl.pallas_call(
        functools.partial(_dispatch_kernel, n_e=n_e, blk=MOE_ROWS, tm=tm),
        grid_spec=grid_spec,
        out_shape=jax.ShapeDtypeStruct((n_rows, n_sub, LANES), F32),
        compiler_params=_cparams(("arbitrary",)),
        name="moe_dispatch",
    )(pend, cnt, dest, h_tiles)


def _expert_kernel(first_ref, nblk_ref, nused_ref, xs_ref, wg_ref, wu_ref, wd_ref, ys_ref,
                   xbuf_ref, ybuf_ref, wgb_ref, wub_ref, wdb_ref, xsem_ref, ysem_ref):
    e = pl.program_id(0)
    n_used = nused_ref[0]
    d = wgb_ref.shape[0]
    blk_rows = xbuf_ref.shape[1]

    def x_copy(g, slot):
        src = xs_ref.at[pl.ds(pl.multiple_of(g * blk_rows, blk_rows), blk_rows)]
        return pltpu.make_async_copy(src, xbuf_ref.at[slot], xsem_ref.at[slot])

    def y_copy(g, slot):
        dst = ys_ref.at[pl.ds(pl.multiple_of(g * blk_rows, blk_rows), blk_rows)]
        return pltpu.make_async_copy(ybuf_ref.at[slot], dst, ysem_ref.at[slot])

    @pl.when(jnp.logical_and(e == 0, n_used > 0))
    def _():
        x_copy(0, 0).start()

    @pl.when(nblk_ref[e] > 0)
    def _():
        wgb_ref[...] = wg_ref[0].astype(BF16)
        wub_ref[...] = wu_ref[0].astype(BF16)
        wdb_ref[...] = wd_ref[0].astype(BF16)

    def block(j, carry):
        g = first_ref[e] + j
        slot = g & 1
        x_copy(g, slot).wait()

        @pl.when(g + 1 < n_used)
        def _():
            x_copy(g + 1, 1 - slot).start()

        @pl.when(g >= 2)
        def _():
            y_copy(g - 2, slot).wait()

        xb = jnp.concatenate(_load_token_tiles(xbuf_ref.at[slot], MOE_ROWS, d), axis=1).astype(BF16)
        hid = _silu(_dot(xb, wgb_ref[...])) * _dot(xb, wub_ref[...])
        _store_token_tiles(ybuf_ref.at[slot], _dot(hid.astype(BF16), wdb_ref[...]))
        y_copy(g, slot).start()
        return carry
    lax.fori_loop(0, nblk_ref[e], block, 0)

    @pl.when(e == pl.num_programs(0) - 1)
    def _():
        @pl.when(n_used >= 1)
        def _():
            y_copy(n_used - 1, (n_used - 1) & 1).wait()

        @pl.when(n_used >= 2)
        def _():
            y_copy(n_used - 2, (n_used - 2) & 1).wait()


def _experts(xs_tiles, plan, w_gate, w_up, w_down):
    n_rows, n_sub, _ = xs_tiles.shape
    d = n_sub * LANES
    n_e, _, ff = w_gate.shape
    blk_rows = MOE_ROWS * n_sub
    hbm = pl.BlockSpec(memory_space=pl.ANY)
    grid_spec = pltpu.PrefetchScalarGridSpec(
        num_scalar_prefetch=3,
        grid=(n_e,),
        in_specs=[hbm,
                  pl.BlockSpec((1, d, ff), lambda e, *_: (e, 0, 0)),
                  pl.BlockSpec((1, d, ff), lambda e, *_: (e, 0, 0)),
                  pl.BlockSpec((1, ff, d), lambda e, *_: (e, 0, 0))],
        out_specs=hbm,
        scratch_shapes=[pltpu.VMEM((2, blk_rows, LANES), F32), pltpu.VMEM((2, blk_rows, LANES), F32),
                        pltpu.VMEM((d, ff), BF16), pltpu.VMEM((d, ff), BF16), pltpu.VMEM((ff, d), BF16),
                        pltpu.SemaphoreType.DMA((2,)), pltpu.SemaphoreType.DMA((2,))],
    )
    ys = pl.pallas_call(
        _expert_kernel,
        grid_spec=grid_spec,
        out_shape=jax.ShapeDtypeStruct((n_rows * n_sub, LANES), F32),
        compiler_params=_cparams(("arbitrary",)),
        name="moe_experts",
    )(*plan, xs_tiles.reshape(n_rows * n_sub, LANES), w_gate, w_up, w_down)
    return ys.reshape(n_rows, n_sub, LANES)


def _combine_kernel(dest_ref, ys_ref, gate_ref, pre_ref, g2_ref, ln_g_ref, ln_b_ref, o_ref, buf_ref, sem_ref):
    tm, d = pre_ref.shape[1], pre_ref.shape[2]
    n_sub = d // LANES

    def row_copy(t, k, src_row):
        slot = pl.multiple_of((k * tm + t) * n_sub, n_sub)
        return pltpu.make_async_copy(ys_ref.at[src_row], buf_ref.at[pl.ds(slot, n_sub)], sem_ref.at[0])

    def issue(g, c):
        t0 = g * DMA_ISSUE_UNROLL
        rows = [[dest_ref[(t0 + j) * TOP_K + k] for k in range(TOP_K)] for j in range(DMA_ISSUE_UNROLL)]
        for j in range(DMA_ISSUE_UNROLL):
            for k in range(TOP_K):
                row_copy(t0 + j, k, rows[j][k]).start(priority=k % 2)
        return c
    lax.fori_loop(0, tm // DMA_ISSUE_UNROLL, issue, 0)

    def drain(t, c):
        for k in range(TOP_K):
            row_copy(t, k, dest_ref[t * TOP_K + k]).wait()
        return c
    lax.fori_loop(0, tm, drain, 0)

    gates = gate_ref[...]
    gate_cols = [jnp.broadcast_to(gates[:, k:k + 1], (tm, LANES)) for k in range(TOP_K)]
    pieces = []
    for s in range(n_sub):
        acc = None
        for k in range(TOP_K):
            term = buf_ref[pl.ds(k * tm * n_sub + s, tm, stride=n_sub), :] * gate_cols[k]
            acc = term if acc is None else acc + term
        pieces.append(acc)
    routed = jnp.concatenate(pieces, axis=1)
    y = pre_ref[0] + g2_ref[0] * routed
    o_ref[0] = _layer_norm(y, ln_g_ref[...], ln_b_ref[...])


def _combine(ys, dest, gates_t, pre, gate2, ln_g, ln_b):
    bsz, seq, d = pre.shape
    tm = min(DISPATCH_ROWS, seq)
    per_seq = seq // tm
    row = lambda i, *_: (i // per_seq, i % per_seq, 0)
    per_b = lambda i, *_: (i // per_seq, 0, 0)
    full = lambda *shape: pl.BlockSpec(shape, lambda i, *_: (0,) * len(shape))
    grid_spec = pltpu.PrefetchScalarGridSpec(
        num_scalar_prefetch=0,
        grid=(bsz * per_seq,),
        in_specs=[pl.BlockSpec((TOP_K * tm,), lambda i: (i,), memory_space=pltpu.SMEM),
                  pl.BlockSpec(memory_space=pl.ANY),
                  pl.BlockSpec((tm, TOP_K), lambda i: (i, 0)),
                  pl.BlockSpec((1, tm, d), row), pl.BlockSpec((1, 1, d), per_b), full(1, d), full(1, d)],
        out_specs=pl.BlockSpec((1, tm, d), row),
        scratch_shapes=[pltpu.VMEM((TOP_K * tm * (d // LANES), LANES), F32), pltpu.SemaphoreType.DMA((1,))],
    )
    return pl.pallas_call(
        _combine_kernel,
        grid_spec=grid_spec,
        out_shape=jax.ShapeDtypeStruct((bsz, seq, d), F32),
        compiler_params=_cparams(("arbitrary",)),
        name="moe_combine_ln2",
    )(dest, ys, gates_t, pre, gate2.reshape(bsz, 1, d), ln_g.reshape(1, d), ln_b.reshape(1, d))


def _dest_kernel(pstart_ref, idx_ref, rank_ref, dest_ref):
    idx = idx_ref[...]

    def body(e, acc):
        return acc + jnp.where(idx == e, pstart_ref[e], 0)
    dest_ref[...] = lax.fori_loop(0, pstart_ref.shape[0], body, rank_ref[...], unroll=8)


def _dest_rows(pstart, idx, rank):
    n_k, n_tok = idx.shape
    tn = min(4096, n_tok)
    grid_spec = pltpu.PrefetchScalarGridSpec(
        num_scalar_prefetch=1,
        grid=(n_tok // tn,),
        in_specs=[pl.BlockSpec((n_k, tn), lambda i, *_: (0, i))] * 2,
        out_specs=pl.BlockSpec((n_k, tn), lambda i, *_: (0, i)),
    )
    return pl.pallas_call(
        _dest_kernel,
        grid_spec=grid_spec,
        out_shape=jax.ShapeDtypeStruct((n_k, n_tok), jnp.int32),
        compiler_params=_cparams(("arbitrary",)),
        name="moe_dest",
    )(pstart, idx, rank)


def _moe_plan(idx, rank, cnt, n_tok):
    n_e = cnt.shape[0]
    padded = (cnt + MOE_ROWS - 1) // MOE_ROWS * MOE_ROWS
    pend = jnp.cumsum(padded).astype(jnp.int32)
    pstart = pend - padded
    dest = _dest_rows(pstart, idx, rank).T.reshape(-1)
    n_blk = n_tok * TOP_K // MOE_ROWS + n_e
    n_used = (pend[-1:] // MOE_ROWS).astype(jnp.int32)
    plan = ((pstart // MOE_ROWS).astype(jnp.int32), (padded // MOE_ROWS).astype(jnp.int32), n_used)
    return dest, pend, plan, n_blk * MOE_ROWS


def kernel(x, c, rel_bias, w_ada, b_ada, w_in, ssm_a_re, ssm_a_im, ssm_b_re, ssm_b_im, ssm_c_re, ssm_c_im, ssm_d, ssm_log_dt, w_glu, b_glu, g_att, g_ssm, w_out, ln1_g, ln1_b, w_router, router_bias, w_e_gate, w_e_up, w_e_down, w_s_gate, w_s_up, w_s_down, ln2_g, ln2_b):
    bsz, seq, d = x.shape
    depth = w_ada.shape[0]
    alpha = (2 * depth) ** 0.25
    att_w = g_att.shape[-1]
    tables = _att_tables(rel_bias)
    for layer in range(depth):
        ada = _ada(c, w_ada[layer], b_ada[layer])
        shift1, scale1, gate1, shift2, scale2, gate2 = jnp.split(ada, 6, axis=-1)
        q, k, v, u = _inproj(x, scale1, shift1, w_in[layer], att_w)
        att = _attention(q, k, v, tables)
        ssm_prm = _ssm_params(ssm_a_re[layer], ssm_a_im[layer], ssm_b_re[layer], ssm_b_im[layer],
                              ssm_c_re[layer], ssm_c_im[layer], ssm_d[layer], ssm_log_dt[layer])
        ssm = _ssm(u, *ssm_prm, w_glu[layer], b_glu[layer])
        h2, h2_tiles, pre = _mix(x, att, ssm, g_att[layer], g_ssm[layer], w_out[layer], gate1,
                                 ln1_g[layer], ln1_b[layer], scale2, shift2, gate2, w_s_gate[layer],
                                 w_s_up[layer], w_s_down[layer], alpha)
        h2d = h2.reshape(bsz * seq, d)
        idx, gate, rank, cnt = _route(h2d, w_router[layer], router_bias[layer])
        dest, pend, plan, n_rows = _moe_plan(idx, rank, cnt, bsz * seq)
        xs = _dispatch(h2_tiles, d // LANES, dest, pend, cnt, n_rows)
        ys = _experts(xs, plan, w_e_gate[layer], w_e_up[layer], w_e_down[layer])
        x = _combine(ys, dest, gate.T, pre, gate2, ln2_g[layer], ln2_b[layer])
    return x
```

```python
import functools
import math

import jax
import jax.numpy as jnp
import numpy as np
from jax import lax
from jax.experimental import pallas as pl
from jax.experimental.pallas import tpu as pltpu

F32 = jnp.float32
BF16 = jnp.bfloat16

HEAD_DIM = 64
ATT_BLOCK = 128
PATTERNS = ((128, 1), (512, 4), (2048, 16))
N_BUCKETS = 32
MAX_DISTANCE = 2048
SSM_GROUP = 16
SSM_STATE = 64
N_EXPERTS = 256
TOP_K = 8
N_EXPERT_GROUPS = 8
TOPK_GROUPS = 4
ROUTED_SCALE = 2.5
EPS = 1e-5
NEG_INF = -1e30

LANES = 128
SUBLANES = 8
VMEM_LIMIT_BYTES = 56 * 1024 * 1024

PROJ_ROWS = 512
SSM_ROWS = 256
MIX_ROWS = 512
ROUTE_COLS = 512
MOE_ROWS = 256
DISPATCH_ROWS = 256
ATT_UNITS_PER_STEP = 8
DMA_ISSUE_UNROLL = 4
EXPERT_SLOTS = 4


def _cparams(sem, vmem=VMEM_LIMIT_BYTES):
    return pltpu.CompilerParams(dimension_semantics=sem, vmem_limit_bytes=vmem)


def _dot(a, b):
    return jnp.dot(a, b, preferred_element_type=F32)


def _dot_nt(a, b):
    return lax.dot_general(a, b, (((1,), (1,)), ((), ())), preferred_element_type=F32)


def _silu(x):
    return x * jax.nn.sigmoid(x)


def _store_token_tiles(ref, x):
    m, d = x.shape
    n_sub = d // LANES
    for s in range(n_sub):
        ref[pl.ds(s, m, stride=n_sub), :] = x[:, s * LANES:(s + 1) * LANES]


def _load_token_tiles(ref, m, d, row0=0):
    n_sub = d // LANES
    return [ref[pl.ds(row0 + s, m, stride=n_sub), :] for s in range(n_sub)]


def _split_bf16(x):
    hi = x.astype(BF16)
    lo = (x - hi.astype(F32)).astype(BF16)
    return hi, lo


def _ada_kernel(c_ref, w_ref, b_ref, o_ref):
    c = c_ref[...]
    a_hi, a_lo = _split_bf16(_silu(c))
    w_hi, w_lo = _split_bf16(w_ref[...])
    acc = _dot(a_hi, w_hi) + _dot(a_hi, w_lo) + _dot(a_lo, w_hi)
    o_ref[...] = acc + b_ref[...]


def _ada(c, w_ada, b_ada):
    bsz, d = c.shape
    n = w_ada.shape[1]
    rows = SUBLANES
    c_pad = jnp.zeros((rows, d), F32).at[:bsz].set(c)
    tn = 1024
    out = pl.pallas_call(
        _ada_kernel,
        grid=(n // tn,),
        in_specs=[pl.BlockSpec((rows, d), lambda j: (0, 0)),
                  pl.BlockSpec((d, tn), lambda j: (0, j)),
                  pl.BlockSpec((1, tn), lambda j: (0, j))],
        out_specs=pl.BlockSpec((rows, tn), lambda j: (0, j)),
        out_shape=jax.ShapeDtypeStruct((rows, n), F32),
        compiler_params=_cparams(("parallel",)),
        name="ada",
    )(c_pad, w_ada, b_ada.reshape(1, n))
    return out[:bsz]


def _inproj_kernel(x_ref, sc_ref, sh_ref, w_ref, q_ref, k_ref, v_ref, u_ref, *, att_w, q_scale):
    h = (x_ref[0] * (1.0 + sc_ref[0]) + sh_ref[0]).astype(BF16)
    q_ref[0] = _dot(h, w_ref[:, 0:att_w]) * q_scale
    k_ref[0] = _dot(h, w_ref[:, att_w:2 * att_w])
    v_ref[0] = _dot(h, w_ref[:, 2 * att_w:3 * att_w])
    u_ref[0] = _dot(h, w_ref[:, 3 * att_w:])


def _inproj(x, scale, shift, w_in, att_w):
    bsz, seq, d = x.shape
    n = w_in.shape[1]
    ssm_w = n - 3 * att_w
    ts = min(PROJ_ROWS, seq)
    row = lambda b, i: (b, i, 0)
    per_b = lambda b, i: (b, 0, 0)
    kern = functools.partial(_inproj_kernel, att_w=att_w, q_scale=HEAD_DIM ** -0.5)
    return pl.pallas_call(
        kern,
        grid=(bsz, seq // ts),
        in_specs=[pl.BlockSpec((1, ts, d), row),
                  pl.BlockSpec((1, 1, d), per_b),
                  pl.BlockSpec((1, 1, d), per_b),
                  pl.BlockSpec((d, n), lambda b, i: (0, 0))],
        out_specs=[pl.BlockSpec((1, ts, att_w), row)] * 3 + [pl.BlockSpec((1, ts, ssm_w), row)],
        out_shape=[jax.ShapeDtypeStruct((bsz, seq, att_w), F32)] * 3
                  + [jax.ShapeDtypeStruct((bsz, seq, ssm_w), F32)],
        compiler_params=_cparams(("parallel", "parallel")),
        name="inproj",
    )(x, scale.reshape(bsz, 1, d), shift.reshape(bsz, 1, d), w_in.astype(BF16))


def _t5_bucket_np(dist):
    exact = N_BUCKETS // 2
    large = exact + (np.log(np.maximum(dist, 1).astype(np.float64) / exact)
                     / math.log(MAX_DISTANCE / exact) * (N_BUCKETS - exact)).astype(np.int64)
    return np.where(dist < exact, dist, np.minimum(large, N_BUCKETS - 1))


def _att_tables(rel_bias):
    qi = np.arange(ATT_BLOCK)[:, None]
    ki = np.arange(2 * ATT_BLOCK)[None, :]
    rel = qi + ATT_BLOCK - ki
    tabs = []
    for window, dil in PATTERNS:
        band = (rel >= 0) & (rel <= window // dil)
        bucket = _t5_bucket_np(np.maximum(rel, 0) * dil)
        onehot = (bucket[:, :, None] == np.arange(N_BUCKETS)[None, None, :]).astype(np.float32)
        bias = jnp.einsum('qkb,bh->hqk', onehot, rel_bias.astype(F32),
                          precision=lax.Precision.HIGHEST)
        full = jnp.where(band[None], bias, NEG_INF)
        first = jnp.concatenate([full[:, :, ATT_BLOCK:], jnp.full_like(full[:, :, ATT_BLOCK:], NEG_INF)], axis=-1)
        tabs.append(jnp.stack([full, first]))
    return jnp.stack(tabs)


def _att_kernel(q_ref, k_ref, v_ref, tab_ref, o_ref, m_ref, l_ref, *, seq):
    lane = lax.broadcasted_iota(jnp.int32, (ATT_BLOCK, LANES), 1)
    head0 = lane < HEAD_DIM

    def rows(ref, start, n, dil):
        if dil == 1:
            return ref[0, pl.ds(pl.multiple_of(start, ATT_BLOCK), n), :]
        return ref[0, pl.ds(start, n, stride=dil), :]

    def lane_mask(n_rows, h):
        m = lax.broadcasted_iota(jnp.int32, (n_rows, LANES), 1) < HEAD_DIM
        return m if h == 0 else ~m

    def batch(pi, dil, nb, res, blk0, head, init):
        q_rows = nb * ATT_BLOCK
        kv_blocks = nb if head else nb + 1
        q_start = res + dil * ATT_BLOCK * blk0
        kv_start = q_start if head else q_start - dil * ATT_BLOCK
        if dil == 1:
            sl = pl.ds(pl.multiple_of(q_start, ATT_BLOCK), q_rows)
        else:
            sl = pl.ds(q_start, q_rows, stride=dil)
        q = rows(q_ref, q_start, q_rows, dil)
        k = rows(k_ref, kv_start, kv_blocks * ATT_BLOCK, dil).astype(BF16)
        v = rows(v_ref, kv_start, kv_blocks * ATT_BLOCK, dil)
        if not init:
            m_old, l_old, o_old = m_ref[sl, :], l_ref[sl, :], o_ref[0, sl, :]
        qh = [jnp.where(lane_mask(q_rows, h), q, 0.0).astype(BF16) for h in range(2)]
        vh = [jnp.where(lane_mask(kv_blocks * ATT_BLOCK, h), v, 1.0).astype(BF16) for h in range(2)]
        o_parts, l_parts, m_parts = [], [], []
        for j in range(nb):
            first = 1 if (head and j == 0) else 0
            kb = j if (not head or j == 0) else j - 1
            ksl = slice(kb * ATT_BLOCK, (kb + 2) * ATT_BLOCK)
            qsl = slice(j * ATT_BLOCK, (j + 1) * ATT_BLOCK)
            outs, ms = [], []
            for h in range(2):
                s = _dot_nt(qh[h][qsl], k[ksl]) + tab_ref[pi, first, h]
                m = jnp.max(s, axis=1, keepdims=True)
                p = jnp.exp(s - m).astype(BF16)
                outs.append(_dot(p, vh[h][ksl]))
                ms.append(m)
            o_parts.append(jnp.where(head0, outs[0], outs[1]))
            l_parts.append(pltpu.roll(jnp.where(head0, outs[1], outs[0]), HEAD_DIM, axis=1))
            m_parts.append(jnp.where(head0, ms[0], ms[1]))
        o = jnp.concatenate(o_parts, axis=0)
        l = jnp.concatenate(l_parts, axis=0)
        m = jnp.concatenate(m_parts, axis=0)
        if not init:
            m_new = jnp.maximum(m_old, m)
            a_old = jnp.exp(m_old - m_new)
            a_cur = jnp.exp(m - m_new)
            o = o_old * a_old + o * a_cur
            l = l_old * a_old + l * a_cur
            m = m_new
        o_ref[0, sl, :] = o
        m_ref[sl, :] = m
        l_ref[sl, :] = l

    n_units = seq // ATT_BLOCK
    order = sorted(range(len(PATTERNS)), key=lambda p: -PATTERNS[p][1])
    for pos, pi in enumerate(order):
        dil = PATTERNS[pi][1]
        log_d = dil.bit_length() - 1
        res_blocks = seq // (dil * ATT_BLOCK)
        nb = min(ATT_UNITS_PER_STEP, res_blocks)
        per_res = res_blocks // nb
        init = pos == 0

        def head_body(res, carry, pi=pi, dil=dil, nb=nb, init=init):
            batch(pi, dil, nb, res, 0, True, init)
            return carry
        lax.fori_loop(0, dil, head_body, 0)

        if per_res > 1:
            def tail_body(i, carry, pi=pi, dil=dil, nb=nb, log_d=log_d, init=init):
                res = i & (dil - 1)
                blk0 = nb * (1 + (i >> log_d))
                batch(pi, dil, nb, res, blk0, False, init)
                return carry
            lax.fori_loop(0, dil * (per_res - 1), tail_body, 0)

    def finish(i, carry):
        sl = pl.ds(pl.multiple_of(i * ATT_BLOCK, ATT_BLOCK), ATT_BLOCK)
        o_ref[0, sl, :] = o_ref[0, sl, :] / l_ref[sl, :]
        return carry
    lax.fori_loop(0, n_units, finish, 0)


def _attention(q, k, v, tables):
    bsz, seq, att_w = q.shape
    n_pairs = att_w // LANES
    blk = lambda b, p: (b, 0, p)
    qkv_spec = pl.BlockSpec((1, seq, LANES), blk)
    n_pat = tables.shape[0]
    tab_spec = pl.BlockSpec((n_pat, 2, 2, ATT_BLOCK, 2 * ATT_BLOCK), lambda b, p: (0, 0, p, 0, 0))
    return pl.pallas_call(
        functools.partial(_att_kernel, seq=seq),
        grid=(bsz, n_pairs),
        in_specs=[qkv_spec, qkv_spec, qkv_spec, tab_spec],
        out_specs=pl.BlockSpec((1, seq, LANES), blk),
        out_shape=jax.ShapeDtypeStruct((bsz, seq, att_w), F32),
        scratch_shapes=[pltpu.VMEM((seq, LANES), F32), pltpu.VMEM((seq, LANES), F32)],
        compiler_params=_cparams(("parallel", "parallel")),
        name="dilated_attention",
    )(q, k, v, tables)


def _ssm_params(a_re, a_im, b_re, b_im, c_re, c_im, d_skip, log_dt):
    n_g, n_p = a_re.shape
    n_c = b_re.shape[-1]
    n_state = n_g * n_p
    dt = jnp.exp(log_dt.astype(F32))[:, None]

    def a_pow(kk):
        mag = jnp.exp(kk * dt * a_re)
        ph = kk * dt * a_im
        return mag * jnp.cos(ph), mag * jnp.sin(ph)

    ab_re, ab_im = a_pow(1.0)
    nr, ni = ab_re - 1.0, ab_im
    den = a_re * a_re + a_im * a_im
    f_re = (nr * a_re + ni * a_im) / den
    f_im = (ni * a_re - nr * a_im) / den
    bb_re = f_re[:, :, None] * b_re - f_im[:, :, None] * b_im
    bb_im = f_re[:, :, None] * b_im + f_im[:, :, None] * b_re
    eye = jnp.eye(n_g, dtype=F32)
    w_re = jnp.einsum('gpc,gh->gchp', bb_re, eye).reshape(n_g * n_c, n_state)
    w_im = jnp.einsum('gpc,gh->gchp', bb_im, eye).reshape(n_g * n_c, n_state)
    o_re = jnp.einsum('gcp,gh->gphc', c_re, eye).reshape(n_state, n_g * n_c)
    o_im = jnp.einsum('gcp,gh->gphc', c_im, eye).reshape(n_state, n_g * n_c)
    n_kb = (n_g * n_c) // LANES
    sp = n_state // n_kb
    w_in = jnp.stack([jnp.concatenate([w_re[kb * LANES:(kb + 1) * LANES, kb * sp:(kb + 1) * sp],
                                       w_im[kb * LANES:(kb + 1) * LANES, kb * sp:(kb + 1) * sp]], axis=1)
                      for kb in range(n_kb)])
    w_out = jnp.stack([jnp.concatenate([o_re[kb * sp:(kb + 1) * sp, kb * LANES:(kb + 1) * LANES],
                                        -o_im[kb * sp:(kb + 1) * sp, kb * LANES:(kb + 1) * LANES]], axis=0)
                       for kb in range(n_kb)])
    j = jnp.arange(SUBLANES, dtype=F32)[:, None]
    flat = lambda t: jnp.broadcast_to(t.reshape(1, n_state), (SUBLANES, n_state))
    coef = []
    for sh in (1, 2, 4):
        pr, pim = a_pow(float(sh))
        keep = (j >= sh).astype(F32)
        coef += [flat(pr) * keep, flat(pim) * keep]
    a_re_f = jnp.broadcast_to(a_re.reshape(1, n_state), (SUBLANES, n_state))
    a_im_f = jnp.broadcast_to(a_im.reshape(1, n_state), (SUBLANES, n_state))
    dt_f = jnp.broadcast_to(jnp.repeat(dt[:, 0], n_p).reshape(1, n_state), (SUBLANES, n_state))
    mag = jnp.exp((j + 1.0) * dt_f * a_re_f)
    ph = (j + 1.0) * dt_f * a_im_f
    coef += [mag * jnp.cos(ph), mag * jnp.sin(ph)]
    coef = jnp.stack(coef)
    return w_in.astype(BF16), w_out.astype(BF16), coef, d_skip.reshape(1, n_g * n_c).astype(F32)


def _gelu_tanh(x):
    return 0.5 * x * (1.0 + jnp.tanh(math.sqrt(2.0 / math.pi) * (x + 0.044715 * (x * x * x))))


def _ssm_kernel(u_ref, win_ref, wout_ref, coef_ref, d_ref, wglu_ref, bglu_ref, o_ref,
                hre_ref, him_ref, cre_ref, cim_ref, *, rows, n_state, col_w):
    n_kb = win_ref.shape[0]
    sp = n_state // n_kb

    @pl.when(pl.program_id(1) == 0)
    def _():
        cre_ref[...] = jnp.zeros_like(cre_ref)
        cim_ref[...] = jnp.zeros_like(cim_ref)

    u = u_ref[0]
    ub = u.astype(BF16)
    for kb in range(n_kb):
        bu = _dot(ub[:, kb * LANES:(kb + 1) * LANES], win_ref[kb])
        hre_ref[:, kb * sp:(kb + 1) * sp] = bu[:, :sp]
        him_ref[:, kb * sp:(kb + 1) * sp] = bu[:, sp:]

    n_slab = rows // SUBLANES
    for c0 in range(0, n_state, col_w):
        cs = slice(c0, c0 + col_w)

        def body(i, carry, cs=cs):
            c_re, c_im = carry
            sl = pl.ds(pl.multiple_of(i * SUBLANES, SUBLANES), SUBLANES)
            x_re = hre_ref[sl, cs]
            x_im = him_ref[sl, cs]
            for si, sh in enumerate((1, 2, 4)):
                p_re = coef_ref[2 * si, :, cs]
                p_im = coef_ref[2 * si + 1, :, cs]
                s_re = pltpu.roll(x_re, sh, axis=0)
                s_im = pltpu.roll(x_im, sh, axis=0)
                x_re, x_im = (x_re + p_re * s_re - p_im * s_im,
                              x_im + p_re * s_im + p_im * s_re)
            p_re = coef_ref[6, :, cs]
            p_im = coef_ref[7, :, cs]
            h_re = x_re + p_re * c_re - p_im * c_im
            h_im = x_im + p_re * c_im + p_im * c_re
            hre_ref[sl, cs] = h_re
            him_ref[sl, cs] = h_im
            last = slice(SUBLANES - 1, SUBLANES)
            return (jnp.broadcast_to(h_re[last, :], h_re.shape),
                    jnp.broadcast_to(h_im[last, :], h_im.shape))

        c_re, c_im = lax.fori_loop(0, n_slab, body, (cre_ref[:, cs], cim_ref[:, cs]))
        cre_ref[:, cs] = c_re
        cim_ref[:, cs] = c_im

    ys = []
    for kb in range(n_kb):
        hcat = jnp.concatenate([hre_ref[:, kb * sp:(kb + 1) * sp].astype(BF16),
                                him_ref[:, kb * sp:(kb + 1) * sp].astype(BF16)], axis=1)
        ys.append(_dot(hcat, wout_ref[kb]))
    y = jnp.concatenate(ys, axis=1) + d_ref[...] * u
    y = _gelu_tanh(y)
    z = _dot(y.astype(BF16), wglu_ref[...]) + bglu_ref[...]
    o_ref[0] = y * jax.nn.sigmoid(z)


def _ssm(u, w_in, w_out, coef, d_flat, w_glu, b_glu):
    bsz, seq, ssm_w = u.shape
    n_state = coef.shape[-1]
    rows = min(SSM_ROWS, seq)
    full = lambda *shape: pl.BlockSpec(shape, lambda b, i: (0,) * len(shape))
    kern = functools.partial(_ssm_kernel, rows=rows, n_state=n_state, col_w=4 * LANES)
    return pl.pallas_call(
        kern,
        grid=(bsz, seq // rows),
        in_specs=[pl.BlockSpec((1, rows, ssm_w), lambda b, i: (b, i, 0)),
                  full(*w_in.shape), full(*w_out.shape), full(*coef.shape), full(1, ssm_w),
                  full(ssm_w, ssm_w), full(1, ssm_w)],
        out_specs=pl.BlockSpec((1, rows, ssm_w), lambda b, i: (b, i, 0)),
        out_shape=jax.ShapeDtypeStruct((bsz, seq, ssm_w), F32),
        scratch_shapes=[pltpu.VMEM((rows, n_state), F32), pltpu.VMEM((rows, n_state), F32),
                        pltpu.VMEM((SUBLANES, n_state), F32), pltpu.VMEM((SUBLANES, n_state), F32)],
        compiler_params=_cparams(("parallel", "arbitrary")),
        name="s5_glu",
    )(u, w_in, w_out, coef, d_flat, w_glu.astype(BF16), b_glu.reshape(1, ssm_w))


def _layer_norm(y, g, b):
    mu = jnp.mean(y, axis=-1, keepdims=True)
    yc = y - mu
    var = jnp.mean(yc * yc, axis=-1, keepdims=True)
    return yc * lax.rsqrt(var + EPS) * g + b


def _rms_norm(y, g):
    return y * lax.rsqrt(jnp.mean(y * y, axis=-1, keepdims=True) + EPS) * g


def _mix_kernel(x_ref, att_ref, ssm_ref, gatt_ref, gssm_ref, wout_ref, g1_ref, ln_g_ref, ln_b_ref,
                sc_ref, sh_ref, g2_ref, wsg_ref, wsu_ref, wsd_ref, h_ref, ht_ref, pre_ref, *, alpha, att_w):
    a_n = _rms_norm(att_ref[0], gatt_ref[...]).astype(BF16)
    s_n = _rms_norm(ssm_ref[0], gssm_ref[...]).astype(BF16)
    mix = _dot(a_n, wout_ref[0:att_w, :]) + _dot(s_n, wout_ref[att_w:, :])
    x1 = _layer_norm(alpha * x_ref[0] + g1_ref[0] * mix, ln_g_ref[...], ln_b_ref[...])
    h = x1 * (1.0 + sc_ref[0]) + sh_ref[0]
    h_ref[0] = h
    _store_token_tiles(ht_ref, h)
    hb = h.astype(BF16)
    hid = _silu(_dot(hb, wsg_ref[...])) * _dot(hb, wsu_ref[...])
    shared = _dot(hid.astype(BF16), wsd_ref[...])
    pre_ref[0] = alpha * x1 + g2_ref[0] * shared


def _mix(x, att, ssm, g_att, g_ssm, w_out, gate1, ln_g, ln_b, scale2, shift2, gate2,
         w_s_gate, w_s_up, w_s_down, alpha):
    bsz, seq, d = x.shape
    att_w = att.shape[-1]
    ssm_w = ssm.shape[-1]
    ff = w_s_gate.shape[1]
    tm = min(MIX_ROWS, seq)
    n_sub = d // LANES
    row = lambda b, i: (b, i, 0)
    per_b = lambda b, i: (b, 0, 0)
    full = lambda *shape: pl.BlockSpec(shape, lambda b, i: (0,) * len(shape))
    vec = lambda t: t.reshape(bsz, 1, d)
    return pl.pallas_call(
        functools.partial(_mix_kernel, alpha=alpha, att_w=att_w),
        grid=(bsz, seq // tm),
        in_specs=[pl.BlockSpec((1, tm, d), row), pl.BlockSpec((1, tm, att_w), row),
                  pl.BlockSpec((1, tm, ssm_w), row),
                  full(1, att_w), full(1, ssm_w), full(att_w + ssm_w, d),
                  pl.BlockSpec((1, 1, d), per_b), full(1, d), full(1, d),
                  pl.BlockSpec((1, 1, d), per_b), pl.BlockSpec((1, 1, d), per_b),
                  pl.BlockSpec((1, 1, d), per_b),
                  full(d, ff), full(d, ff), full(ff, d)],
        out_specs=[pl.BlockSpec((1, tm, d), row),
                   pl.BlockSpec((tm * n_sub, LANES), lambda b, i: (b * (seq // tm) + i, 0)),
                   pl.BlockSpec((1, tm, d), row)],
        out_shape=[jax.ShapeDtypeStruct((bsz, seq, d), F32),
                   jax.ShapeDtypeStruct((bsz * seq * n_sub, LANES), F32),
                   jax.ShapeDtypeStruct((bsz, seq, d), F32)],
        compiler_params=_cparams(("parallel", "parallel")),
        name="mix_ln1_shared",
    )(x, att, ssm, g_att.reshape(1, att_w), g_ssm.reshape(1, ssm_w), w_out.astype(BF16),
      vec(gate1), ln_g.reshape(1, d), ln_b.reshape(1, d), vec(scale2), vec(shift2), vec(gate2),
      w_s_gate.astype(BF16), w_s_up.astype(BF16), w_s_down.astype(BF16))


def _route_kernel(h_ref, wrt_ref, wrt_lo_ref, bias_ref, tri_ref, idx_ref, gate_ref, rank_ref, cnt_ref, carry_ref):
    n_e = wrt_ref.shape[0]
    tm = h_ref.shape[0]
    per_group = n_e // N_EXPERT_GROUPS
    neg = -jnp.inf

    @pl.when(pl.program_id(0) == 0)
    def _():
        carry_ref[...] = jnp.zeros_like(carry_ref)

    h_hi, h_lo = _split_bf16(h_ref[...])
    logits = (_dot_nt(wrt_ref[...], h_hi) + _dot_nt(wrt_ref[...], h_lo)
              + _dot_nt(wrt_lo_ref[...], h_hi))
    scores = jax.nn.sigmoid(logits)
    biased = scores + bias_ref[...]

    g3 = biased.reshape(N_EXPERT_GROUPS, per_group, tm)
    ridx = lax.broadcasted_iota(jnp.int32, g3.shape, 1).astype(F32)
    m1 = jnp.max(g3, axis=1, keepdims=True)
    first = jnp.min(jnp.where(g3 == m1, ridx, float(per_group)), axis=1, keepdims=True)
    m2 = jnp.max(jnp.where(ridx == first, neg, g3), axis=1, keepdims=True)
    gs = m1 + m2

    gidx = lax.broadcasted_iota(jnp.int32, gs.shape, 0).astype(F32)
    ok = jnp.zeros_like(gs)
    cur = gs
    for _ in range(TOPK_GROUPS):
        mx = jnp.max(cur, axis=0, keepdims=True)
        fi = jnp.min(jnp.where(cur == mx, gidx, float(N_EXPERT_GROUPS)), axis=0, keepdims=True)
        hit = gidx == fi
        ok = jnp.where(hit, 1.0, ok)
        cur = jnp.where(hit, neg, cur)
    masked = jnp.where(ok > 0.5, g3, neg).reshape(n_e, tm)

    eidx = lax.broadcasted_iota(jnp.int32, (n_e, tm), 0).astype(F32)
    onehot = jnp.zeros((n_e, tm), F32)
    cur = masked
    sel_idx = []
    sel_gate = []
    for _ in range(TOP_K):
        mx = jnp.max(cur, axis=0, keepdims=True)
        fi = jnp.min(jnp.where(cur == mx, eidx, float(n_e)), axis=0, keepdims=True)
        hit = eidx == fi
        sel_idx.append(fi)
        sel_gate.append(jnp.sum(jnp.where(hit, scores, 0.0), axis=0, keepdims=True))
        onehot = jnp.where(hit, 1.0, onehot)
        cur = jnp.where(hit, neg, cur)
    idx = jnp.concatenate(sel_idx, axis=0)
    gate = jnp.concatenate(sel_gate, axis=0)
    gate = gate / jnp.sum(gate, axis=0, keepdims=True) * ROUTED_SCALE

    prior = _dot(onehot.astype(BF16), tri_ref[...]) + carry_ref[:, 0:1]
    ranks = [jnp.sum(jnp.where(eidx == sel_idx[k], prior, 0.0), axis=0, keepdims=True)
             for k in range(TOP_K)]
    rank = jnp.concatenate(ranks, axis=0)
    carry = carry_ref[...] + jnp.sum(onehot, axis=1, keepdims=True)
    carry_ref[...] = carry
    cnt_ref[...] = carry

    idx_ref[...] = idx.astype(jnp.int32)
    gate_ref[...] = gate
    rank_ref[...] = rank.astype(jnp.int32)


def _route(h2d, w_router, router_bias):
    n_tok, d = h2d.shape
    n_e = w_router.shape[1]
    tm = min(ROUTE_COLS, n_tok)
    tri = (np.arange(tm)[:, None] < np.arange(tm)[None, :]).astype(np.float32)
    w_hi, w_lo = _split_bf16(w_router.T.astype(F32))
    col = lambda i: (0, i)
    full = lambda *shape: pl.BlockSpec(shape, lambda i: (0,) * len(shape))
    idx, gate, rank, cnt = pl.pallas_call(
        _route_kernel,
        grid=(n_tok // tm,),
        in_specs=[pl.BlockSpec((tm, d), lambda i: (i, 0)), full(n_e, d), full(n_e, d), full(n_e, 1),
                  full(tm, tm)],
        out_specs=[pl.BlockSpec((TOP_K, tm), col), pl.BlockSpec((TOP_K, tm), col),
                   pl.BlockSpec((TOP_K, tm), col), full(n_e, LANES)],
        out_shape=[jax.ShapeDtypeStruct((TOP_K, n_tok), jnp.int32),
                   jax.ShapeDtypeStruct((TOP_K, n_tok), F32),
                   jax.ShapeDtypeStruct((TOP_K, n_tok), jnp.int32),
                   jax.ShapeDtypeStruct((n_e, LANES), F32)],
        scratch_shapes=[pltpu.VMEM((n_e, LANES), F32)],
        compiler_params=_cparams(("arbitrary",)),
        name="router_topk",
    )(h2d, w_hi, w_lo, router_bias.reshape(n_e, 1).astype(F32), jnp.asarray(tri, BF16))
    return idx, gate, rank, cnt[:, 0].astype(jnp.int32)


def _dispatch_kernel(pend_ref, cnt_ref, dest_ref, ht_ref, xs_ref, zero_ref, sem_ref, *, n_e, blk, tm):

    def zero_copy(e):
        start = pl.multiple_of(pend_ref[e] - blk, blk)
        return pltpu.make_async_copy(zero_ref, xs_ref.at[pl.ds(start, blk)], sem_ref.at[0])

    @pl.when(pl.program_id(0) == 0)
    def _():
        zero_ref[...] = jnp.zeros_like(zero_ref)

        def start(e, c):
            @pl.when(cnt_ref[e] > 0)
            def _():
                zero_copy(e).start()
            return c
        lax.fori_loop(0, n_e, start, 0)

        def wait(e, c):
            @pl.when(cnt_ref[e] > 0)
            def _():
                zero_copy(e).wait()
            return c
        lax.fori_loop(0, n_e, wait, 0)

    n_sub = ht_ref.shape[0] // tm

    def row_copy(t, dst_row):
        src = ht_ref.at[pl.ds(pl.multiple_of(t * n_sub, n_sub), n_sub)]
        return pltpu.make_async_copy(src, xs_ref.at[dst_row], sem_ref.at[1])

    def issue(g, c):
        t0 = g * DMA_ISSUE_UNROLL
        rows = [[dest_ref[(t0 + j) * TOP_K + k] for k in range(TOP_K)] for j in range(DMA_ISSUE_UNROLL)]
        for j in range(DMA_ISSUE_UNROLL):
            for k in range(TOP_K):
                row_copy(t0 + j, rows[j][k]).start(priority=k % 2)
        return c
    lax.fori_loop(0, tm // DMA_ISSUE_UNROLL, issue, 0)

    def drain(t, c):
        for k in range(TOP_K):
            row_copy(t, dest_ref[t * TOP_K + k]).wait()
        return c
    lax.fori_loop(0, tm, drain, 0)


def _dispatch(h_tiles, n_sub, dest, pend, cnt, n_rows):
    n_tok = h_tiles.shape[0] // n_sub
    n_e = cnt.shape[0]
    tm = min(DISPATCH_ROWS, n_tok)
    grid_spec = pltpu.PrefetchScalarGridSpec(
        num_scalar_prefetch=2,
        grid=(n_tok // tm,),
        in_specs=[pl.BlockSpec((TOP_K * tm,), lambda i, *_: (i,), memory_space=pltpu.SMEM),
                  pl.BlockSpec((tm * n_sub, LANES), lambda i, *_: (i, 0))],
        out_specs=pl.BlockSpec(memory_space=pl.ANY),
        scratch_shapes=[pltpu.VMEM((MOE_ROWS, n_sub, LANES), F32), pltpu.SemaphoreType.DMA((2,))],
    )
    return pl.pallas_call(
        functools.partial(_dispatch_kernel, n_e=n_e, blk=MOE_ROWS, tm=tm),
        grid_spec=grid_spec,
        out_shape=jax.ShapeDtypeStruct((n_rows, n_sub, LANES), F32),
        compiler_params=_cparams(("arbitrary",)),
        name="moe_dispatch",
    )(pend, cnt, dest, h_tiles)


def _expert_kernel(first_ref, nblk_ref, nused_ref, xs_ref, wg_ref, wu_ref, wd_ref, ys_ref,
                   xbuf_ref, ybuf_ref, wgb_ref, wub_ref, wdb_ref, xsem_ref, ysem_ref):
    e = pl.program_id(0)
    n_used = nused_ref[0]
    d = wgb_ref.shape[0]
    blk_rows = xbuf_ref.shape[1]

    def x_copy(g, slot):
        src = xs_ref.at[pl.ds(pl.multiple_of(g * blk_rows, blk_rows), blk_rows)]
        return pltpu.make_async_copy(src, xbuf_ref.at[slot], xsem_ref.at[slot])

    def y_copy(g, slot):
        dst = ys_ref.at[pl.ds(pl.multiple_of(g * blk_rows, blk_rows), blk_rows)]
        return pltpu.make_async_copy(ybuf_ref.at[slot], dst, ysem_ref.at[slot])

    n_slot = xbuf_ref.shape[0]

    @pl.when(e == 0)
    def _():
        for g in range(n_slot - 1):
            @pl.when(g < n_used)
            def _(g=g):
                x_copy(g, g).start()

    @pl.when(nblk_ref[e] > 0)
    def _():
        wgb_ref[...] = wg_ref[0].astype(BF16)
        wub_ref[...] = wu_ref[0].astype(BF16)
        wdb_ref[...] = wd_ref[0].astype(BF16)

    def block(j, carry):
        g = first_ref[e] + j
        slot = g & (n_slot - 1)
        x_copy(g, slot).wait()

        @pl.when(g + n_slot - 1 < n_used)
        def _():
            x_copy(g + n_slot - 1, (g + n_slot - 1) & (n_slot - 1)).start()

        @pl.when(g >= n_slot)
        def _():
            y_copy(g - n_slot, slot).wait()

        xb = jnp.concatenate(_load_token_tiles(xbuf_ref.at[slot], MOE_ROWS, d), axis=1).astype(BF16)
        hid = _silu(_dot(xb, wgb_ref[...])) * _dot(xb, wub_ref[...])
        _store_token_tiles(ybuf_ref.at[slot], _dot(hid.astype(BF16), wdb_ref[...]))
        y_copy(g, slot).start()
        return carry
    lax.fori_loop(0, nblk_ref[e], block, 0)

    @pl.when(e == pl.num_programs(0) - 1)
    def _():
        for back in range(1, n_slot + 1):
            @pl.when(n_used >= back)
            def _(back=back):
                y_copy(n_used - back, (n_used - back) & (n_slot - 1)).wait()


def _experts(xs_tiles, plan, w_gate, w_up, w_down):
    n_rows, n_sub, _ = xs_tiles.shape
    d = n_sub * LANES
    n_e, _, ff = w_gate.shape
    blk_rows = MOE_ROWS * n_sub
    hbm = pl.BlockSpec(memory_space=pl.ANY)
    grid_spec = pltpu.PrefetchScalarGridSpec(
        num_scalar_prefetch=3,
        grid=(n_e,),
        in_specs=[hbm,
                  pl.BlockSpec((1, d, ff), lambda e, *_: (e, 0, 0)),
                  pl.BlockSpec((1, d, ff), lambda e, *_: (e, 0, 0)),
                  pl.BlockSpec((1, ff, d), lambda e, *_: (e, 0, 0))],
        out_specs=hbm,
        scratch_shapes=[pltpu.VMEM((EXPERT_SLOTS, blk_rows, LANES), F32),
                        pltpu.VMEM((EXPERT_SLOTS, blk_rows, LANES), F32),
                        pltpu.VMEM((d, ff), BF16), pltpu.VMEM((d, ff), BF16), pltpu.VMEM((ff, d), BF16),
                        pltpu.SemaphoreType.DMA((EXPERT_SLOTS,)), pltpu.SemaphoreType.DMA((EXPERT_SLOTS,))],
    )
    ys = pl.pallas_call(
        _expert_kernel,
        grid_spec=grid_spec,
        out_shape=jax.ShapeDtypeStruct((n_rows * n_sub, LANES), F32),
        compiler_params=_cparams(("arbitrary",)),
        name="moe_experts",
    )(*plan, xs_tiles.reshape(n_rows * n_sub, LANES), w_gate, w_up, w_down)
    return ys.reshape(n_rows, n_sub, LANES)


def _combine_kernel(dest_ref, ys_ref, gate_ref, pre_ref, g2_ref, ln_g_ref, ln_b_ref, o_ref, buf_ref, sem_ref):
    tm, d = pre_ref.shape[1], pre_ref.shape[2]
    n_sub = d // LANES

    def row_copy(t, k, src_row):
        slot = pl.multiple_of((k * tm + t) * n_sub, n_sub)
        return pltpu.make_async_copy(ys_ref.at[src_row], buf_ref.at[pl.ds(slot, n_sub)], sem_ref.at[0])

    def issue(g, c):
        t0 = g * DMA_ISSUE_UNROLL
        rows = [[dest_ref[(t0 + j) * TOP_K + k] for k in range(TOP_K)] for j in range(DMA_ISSUE_UNROLL)]
        for j in range(DMA_ISSUE_UNROLL):
            for k in range(TOP_K):
                row_copy(t0 + j, k, rows[j][k]).start(priority=k % 2)
        return c
    lax.fori_loop(0, tm // DMA_ISSUE_UNROLL, issue, 0)

    def drain(t, c):
        for k in range(TOP_K):
            row_copy(t, k, dest_ref[t * TOP_K + k]).wait()
        return c
    lax.fori_loop(0, tm, drain, 0)

    gates = gate_ref[...]
    gate_cols = [jnp.broadcast_to(gates[:, k:k + 1], (tm, LANES)) for k in range(TOP_K)]
    pieces = []
    for s in range(n_sub):
        acc = None
        for k in range(TOP_K):
            term = buf_ref[pl.ds(k * tm * n_sub + s, tm, stride=n_sub), :] * gate_cols[k]
            acc = term if acc is None else acc + term
        pieces.append(acc)
    routed = jnp.concatenate(pieces, axis=1)
    y = pre_ref[0] + g2_ref[0] * routed
    o_ref[0] = _layer_norm(y, ln_g_ref[...], ln_b_ref[...])


def _combine(ys, dest, gates_t, pre, gate2, ln_g, ln_b):
    bsz, seq, d = pre.shape
    tm = min(DISPATCH_ROWS, seq)
    per_seq = seq // tm
    row = lambda i, *_: (i // per_seq, i % per_seq, 0)
    per_b = lambda i, *_: (i // per_seq, 0, 0)
    full = lambda *shape: pl.BlockSpec(shape, lambda i, *_: (0,) * len(shape))
    grid_spec = pltpu.PrefetchScalarGridSpec(
        num_scalar_prefetch=0,
        grid=(bsz * per_seq,),
        in_specs=[pl.BlockSpec((TOP_K * tm,), lambda i: (i,), memory_space=pltpu.SMEM),
                  pl.BlockSpec(memory_space=pl.ANY),
                  pl.BlockSpec((tm, TOP_K), lambda i: (i, 0)),
                  pl.BlockSpec((1, tm, d), row), pl.BlockSpec((1, 1, d), per_b), full(1, d), full(1, d)],
        out_specs=pl.BlockSpec((1, tm, d), row),
        scratch_shapes=[pltpu.VMEM((TOP_K * tm * (d // LANES), LANES), F32), pltpu.SemaphoreType.DMA((1,))],
    )
    return pl.pallas_call(
        _combine_kernel,
        grid_spec=grid_spec,
        out_shape=jax.ShapeDtypeStruct((bsz, seq, d), F32),
        compiler_params=_cparams(("arbitrary",)),
        name="moe_combine_ln2",
    )(dest, ys, gates_t, pre, gate2.reshape(bsz, 1, d), ln_g.reshape(1, d), ln_b.reshape(1, d))


def _dest_kernel(pstart_ref, idx_ref, rank_ref, dest_ref):
    idx = idx_ref[...]

    def body(e, acc):
        return acc + jnp.where(idx == e, pstart_ref[e], 0)
    dest_ref[...] = lax.fori_loop(0, pstart_ref.shape[0], body, rank_ref[...], unroll=8)


def _dest_rows(pstart, idx, rank):
    n_k, n_tok = idx.shape
    tn = min(4096, n_tok)
    grid_spec = pltpu.PrefetchScalarGridSpec(
        num_scalar_prefetch=1,
        grid=(n_tok // tn,),
        in_specs=[pl.BlockSpec((n_k, tn), lambda i, *_: (0, i))] * 2,
        out_specs=pl.BlockSpec((n_k, tn), lambda i, *_: (0, i)),
    )
    return pl.pallas_call(
        _dest_kernel,
        grid_spec=grid_spec,
        out_shape=jax.ShapeDtypeStruct((n_k, n_tok), jnp.int32),
        compiler_params=_cparams(("arbitrary",)),
        name="moe_dest",
    )(pstart, idx, rank)


def _moe_plan(idx, rank, cnt, n_tok):
    n_e = cnt.shape[0]
    padded = (cnt + MOE_ROWS - 1) // MOE_ROWS * MOE_ROWS
    pend = jnp.cumsum(padded).astype(jnp.int32)
    pstart = pend - padded
    dest = _dest_rows(pstart, idx, rank).T.reshape(-1)
    n_blk = n_tok * TOP_K // MOE_ROWS + n_e
    n_used = (pend[-1:] // MOE_ROWS).astype(jnp.int32)
    plan = ((pstart // MOE_ROWS).astype(jnp.int32), (padded // MOE_ROWS).astype(jnp.int32), n_used)
    return dest, pend, plan, n_blk * MOE_ROWS


def kernel(x, c, rel_bias, w_ada, b_ada, w_in, ssm_a_re, ssm_a_im, ssm_b_re, ssm_b_im, ssm_c_re, ssm_c_im, ssm_d, ssm_log_dt, w_glu, b_glu, g_att, g_ssm, w_out, ln1_g, ln1_b, w_router, router_bias, w_e_gate, w_e_up, w_e_down, w_s_gate, w_s_up, w_s_down, ln2_g, ln2_b):
    bsz, seq, d = x.shape
    depth = w_ada.shape[0]
    alpha = (2 * depth) ** 0.25
    att_w = g_att.shape[-1]
    tables = _att_tables(rel_bias)
    for layer in range(depth):
        ada = _ada(c, w_ada[layer], b_ada[layer])
        shift1, scale1, gate1, shift2, scale2, gate2 = jnp.split(ada, 6, axis=-1)
        q, k, v, u = _inproj(x, scale1, shift1, w_in[layer], att_w)
        att = _attention(q, k, v, tables)
        ssm_prm = _ssm_params(ssm_a_re[layer], ssm_a_im[layer], ssm_b_re[layer], ssm_b_im[layer],
                              ssm_c_re[layer], ssm_c_im[layer], ssm_d[layer], ssm_log_dt[layer])
        ssm = _ssm(u, *ssm_prm, w_glu[layer], b_glu[layer])
        h2, h2_tiles, pre = _mix(x, att, ssm, g_att[layer], g_ssm[layer], w_out[layer], gate1,
                                 ln1_g[layer], ln1_b[layer], scale2, shift2, gate2, w_s_gate[layer],
                                 w_s_up[layer], w_s_down[layer], alpha)
        h2d = h2.reshape(bsz * seq, d)
        idx, gate, rank, cnt = _route(h2d, w_router[layer], router_bias[layer])
        dest, pend, plan, n_rows = _moe_plan(idx, rank, cnt, bsz * seq)
        xs = _dispatch(h2_tiles, d // LANES, dest, pend, cnt, n_rows)
        ys = _experts(xs, plan, w_e_gate[layer], w_e_up[layer], w_e_down[layer])
        x = _combine(ys, dest, gate.T, pre, gate2, ln2_g[layer], ln2_b[layer])
    return x
```

```python
import functools
import math

import jax
import jax.numpy as jnp
import numpy as np
from jax import lax
from jax.experimental import pallas as pl
from jax.experimental.pallas import tpu as pltpu
from jax.experimental.pallas import tpu_sc as plsc

F32 = jnp.float32
BF16 = jnp.bfloat16

HEAD_DIM = 64
ATT_BLOCK = 128
PATTERNS = ((128, 1), (512, 4), (2048, 16))
N_BUCKETS = 32
MAX_DISTANCE = 2048
SSM_GROUP = 16
SSM_STATE = 64
N_EXPERTS = 256
TOP_K = 8
N_EXPERT_GROUPS = 8
TOPK_GROUPS = 4
ROUTED_SCALE = 2.5
EPS = 1e-5
NEG_INF = -1e30

LANES = 128
SUBLANES = 8
VMEM_LIMIT_BYTES = 56 * 1024 * 1024

PROJ_ROWS = 512
SSM_ROWS = 256
MIX_ROWS = 512
ROUTE_COLS = 512
MOE_ROWS = 256
DISPATCH_ROWS = 256
DISPATCH_WINDOW = 32
ATT_UNITS_PER_STEP = 8
DMA_ISSUE_UNROLL = 4
EXPERT_SLOTS = 4


def _cparams(sem, vmem=VMEM_LIMIT_BYTES):
    return pltpu.CompilerParams(dimension_semantics=sem, vmem_limit_bytes=vmem)


def _dot(a, b):
    return jnp.dot(a, b, preferred_element_type=F32)


def _dot_nt(a, b):
    return lax.dot_general(a, b, (((1,), (1,)), ((), ())), preferred_element_type=F32)


def _silu(x):
    return x * jax.nn.sigmoid(x)


def _store_token_tiles(ref, x):
    m, d = x.shape
    n_sub = d // LANES
    for s in range(n_sub):
        ref[pl.ds(s, m, stride=n_sub), :] = x[:, s * LANES:(s + 1) * LANES]


def _load_token_tiles(ref, m, d, row0=0):
    n_sub = d // LANES
    return [ref[pl.ds(row0 + s, m, stride=n_sub), :] for s in range(n_sub)]


def _split_bf16(x):
    hi = x.astype(BF16)
    lo = (x - hi.astype(F32)).astype(BF16)
    return hi, lo


def _ada_kernel(c_ref, w_ref, b_ref, o_ref):
    c = c_ref[...]
    a_hi, a_lo = _split_bf16(_silu(c))
    w_hi, w_lo = _split_bf16(w_ref[...])
    acc = _dot(a_hi, w_hi) + _dot(a_hi, w_lo) + _dot(a_lo, w_hi)
    o_ref[...] = acc + b_ref[...]


def _ada(c, w_ada, b_ada):
    bsz, d = c.shape
    n = w_ada.shape[1]
    rows = SUBLANES
    c_pad = jnp.zeros((rows, d), F32).at[:bsz].set(c)
    tn = 1024
    out = pl.pallas_call(
        _ada_kernel,
        grid=(n // tn,),
        in_specs=[pl.BlockSpec((rows, d), lambda j: (0, 0)),
                  pl.BlockSpec((d, tn), lambda j: (0, j)),
                  pl.BlockSpec((1, tn), lambda j: (0, j))],
        out_specs=pl.BlockSpec((rows, tn), lambda j: (0, j)),
        out_shape=jax.ShapeDtypeStruct((rows, n), F32),
        compiler_params=_cparams(("parallel",)),
        name="ada",
    )(c_pad, w_ada, b_ada.reshape(1, n))
    return out[:bsz]


def _inproj_kernel(x_ref, sc_ref, sh_ref, w_ref, q_ref, k_ref, v_ref, u_ref, *, att_w, q_scale):
    h = (x_ref[0] * (1.0 + sc_ref[0]) + sh_ref[0]).astype(BF16)
    q_ref[0] = _dot(h, w_ref[:, 0:att_w]) * q_scale
    k_ref[0] = _dot(h, w_ref[:, att_w:2 * att_w])
    v_ref[0] = _dot(h, w_ref[:, 2 * att_w:3 * att_w])
    u_ref[0] = _dot(h, w_ref[:, 3 * att_w:])


def _inproj(x, scale, shift, w_in, att_w):
    bsz, seq, d = x.shape
    n = w_in.shape[1]
    ssm_w = n - 3 * att_w
    ts = min(PROJ_ROWS, seq)
    row = lambda b, i: (b, i, 0)
    per_b = lambda b, i: (b, 0, 0)
    kern = functools.partial(_inproj_kernel, att_w=att_w, q_scale=HEAD_DIM ** -0.5)
    return pl.pallas_call(
        kern,
        grid=(bsz, seq // ts),
        in_specs=[pl.BlockSpec((1, ts, d), row),
                  pl.BlockSpec((1, 1, d), per_b),
                  pl.BlockSpec((1, 1, d), per_b),
                  pl.BlockSpec((d, n), lambda b, i: (0, 0))],
        out_specs=[pl.BlockSpec((1, ts, att_w), row)] * 3 + [pl.BlockSpec((1, ts, ssm_w), row)],
        out_shape=[jax.ShapeDtypeStruct((bsz, seq, att_w), F32)] * 3
                  + [jax.ShapeDtypeStruct((bsz, seq, ssm_w), F32)],
        compiler_params=_cparams(("parallel", "parallel")),
        name="inproj",
    )(x, scale.reshape(bsz, 1, d), shift.reshape(bsz, 1, d), w_in.astype(BF16))


def _t5_bucket_np(dist):
    exact = N_BUCKETS // 2
    large = exact + (np.log(np.maximum(dist, 1).astype(np.float64) / exact)
                     / math.log(MAX_DISTANCE / exact) * (N_BUCKETS - exact)).astype(np.int64)
    return np.where(dist < exact, dist, np.minimum(large, N_BUCKETS - 1))


def _att_tables(rel_bias):
    qi = np.arange(ATT_BLOCK)[:, None]
    ki = np.arange(2 * ATT_BLOCK)[None, :]
    rel = qi + ATT_BLOCK - ki
    tabs = []
    for window, dil in PATTERNS:
        band = (rel >= 0) & (rel <= window // dil)
        bucket = _t5_bucket_np(np.maximum(rel, 0) * dil)
        onehot = (bucket[:, :, None] == np.arange(N_BUCKETS)[None, None, :]).astype(np.float32)
        bias = jnp.einsum('qkb,bh->hqk', onehot, rel_bias.astype(F32),
                          precision=lax.Precision.HIGHEST)
        full = jnp.where(band[None], bias, NEG_INF)
        first = jnp.concatenate([full[:, :, ATT_BLOCK:], jnp.full_like(full[:, :, ATT_BLOCK:], NEG_INF)], axis=-1)
        tabs.append(jnp.stack([full, first]))
    return jnp.stack(tabs)


def _att_kernel(q_ref, k_ref, v_ref, tab_ref, o_ref, m_ref, l_ref, *, seq):
    lane = lax.broadcasted_iota(jnp.int32, (ATT_BLOCK, LANES), 1)
    head0 = lane < HEAD_DIM

    def rows(ref, start, n, dil):
        if dil == 1:
            return ref[0, pl.ds(pl.multiple_of(start, ATT_BLOCK), n), :]
        return ref[0, pl.ds(start, n, stride=dil), :]

    def lane_mask(n_rows, h):
        m = lax.broadcasted_iota(jnp.int32, (n_rows, LANES), 1) < HEAD_DIM
        return m if h == 0 else ~m

    def batch(pi, dil, nb, res, blk0, head, init):
        q_rows = nb * ATT_BLOCK
        kv_blocks = nb if head else nb + 1
        q_start = res + dil * ATT_BLOCK * blk0
        kv_start = q_start if head else q_start - dil * ATT_BLOCK
        if dil == 1:
            sl = pl.ds(pl.multiple_of(q_start, ATT_BLOCK), q_rows)
        else:
            sl = pl.ds(q_start, q_rows, stride=dil)
        q = rows(q_ref, q_start, q_rows, dil)
        k = rows(k_ref, kv_start, kv_blocks * ATT_BLOCK, dil).astype(BF16)
        v = rows(v_ref, kv_start, kv_blocks * ATT_BLOCK, dil)
        if not init:
            m_old, l_old, o_old = m_ref[sl, :], l_ref[sl, :], o_ref[0, sl, :]
        qh = [jnp.where(lane_mask(q_rows, h), q, 0.0).astype(BF16) for h in range(2)]
        vh = [jnp.where(lane_mask(kv_blocks * ATT_BLOCK, h), v, 1.0).astype(BF16) for h in range(2)]
        o_parts, l_parts, m_parts = [], [], []
        for j in range(nb):
            first = 1 if (head and j == 0) else 0
            kb = j if (not head or j == 0) else j - 1
            ksl = slice(kb * ATT_BLOCK, (kb + 2) * ATT_BLOCK)
            qsl = slice(j * ATT_BLOCK, (j + 1) * ATT_BLOCK)
            outs, ms = [], []
            for h in range(2):
                s = _dot_nt(qh[h][qsl], k[ksl]) + tab_ref[pi, first, h]
                m = jnp.max(s, axis=1, keepdims=True)
                p = jnp.exp(s - m).astype(BF16)
                outs.append(_dot(p, vh[h][ksl]))
                ms.append(m)
            o_parts.append(jnp.where(head0, outs[0], outs[1]))
            l_parts.append(pltpu.roll(jnp.where(head0, outs[1], outs[0]), HEAD_DIM, axis=1))
            m_parts.append(jnp.where(head0, ms[0], ms[1]))
        o = jnp.concatenate(o_parts, axis=0)
        l = jnp.concatenate(l_parts, axis=0)
        m = jnp.concatenate(m_parts, axis=0)
        if not init:
            m_new = jnp.maximum(m_old, m)
            a_old = jnp.exp(m_old - m_new)
            a_cur = jnp.exp(m - m_new)
            o = o_old * a_old + o * a_cur
            l = l_old * a_old + l * a_cur
            m = m_new
        o_ref[0, sl, :] = o
        m_ref[sl, :] = m
        l_ref[sl, :] = l

    n_units = seq // ATT_BLOCK
    order = sorted(range(len(PATTERNS)), key=lambda p: -PATTERNS[p][1])
    for pos, pi in enumerate(order):
        dil = PATTERNS[pi][1]
        log_d = dil.bit_length() - 1
        res_blocks = seq // (dil * ATT_BLOCK)
        nb = min(ATT_UNITS_PER_STEP, res_blocks)
        per_res = res_blocks // nb
        init = pos == 0

        def head_body(res, carry, pi=pi, dil=dil, nb=nb, init=init):
            batch(pi, dil, nb, res, 0, True, init)
            return carry
        lax.fori_loop(0, dil, head_body, 0)

        if per_res > 1:
            def tail_body(i, carry, pi=pi, dil=dil, nb=nb, log_d=log_d, init=init):
                res = i & (dil - 1)
                blk0 = nb * (1 + (i >> log_d))
                batch(pi, dil, nb, res, blk0, False, init)
                return carry
            lax.fori_loop(0, dil * (per_res - 1), tail_body, 0)

    def finish(i, carry):
        sl = pl.ds(pl.multiple_of(i * ATT_BLOCK, ATT_BLOCK), ATT_BLOCK)
        o_ref[0, sl, :] = o_ref[0, sl, :] / l_ref[sl, :]
        return carry
    lax.fori_loop(0, n_units, finish, 0)


def _attention(q, k, v, tables):
    bsz, seq, att_w = q.shape
    n_pairs = att_w // LANES
    blk = lambda b, p: (b, 0, p)
    qkv_spec = pl.BlockSpec((1, seq, LANES), blk)
    n_pat = tables.shape[0]
    tab_spec = pl.BlockSpec((n_pat, 2, 2, ATT_BLOCK, 2 * ATT_BLOCK), lambda b, p: (0, 0, p, 0, 0))
    return pl.pallas_call(
        functools.partial(_att_kernel, seq=seq),
        grid=(bsz, n_pairs),
        in_specs=[qkv_spec, qkv_spec, qkv_spec, tab_spec],
        out_specs=pl.BlockSpec((1, seq, LANES), blk),
        out_shape=jax.ShapeDtypeStruct((bsz, seq, att_w), F32),
        scratch_shapes=[pltpu.VMEM((seq, LANES), F32), pltpu.VMEM((seq, LANES), F32)],
        compiler_params=_cparams(("parallel", "parallel")),
        name="dilated_attention",
    )(q, k, v, tables)


def _ssm_params(a_re, a_im, b_re, b_im, c_re, c_im, d_skip, log_dt):
    n_g, n_p = a_re.shape
    n_c = b_re.shape[-1]
    n_state = n_g * n_p
    dt = jnp.exp(log_dt.astype(F32))[:, None]

    def a_pow(kk):
        mag = jnp.exp(kk * dt * a_re)
        ph = kk * dt * a_im
        return mag * jnp.cos(ph), mag * jnp.sin(ph)

    ab_re, ab_im = a_pow(1.0)
    nr, ni = ab_re - 1.0, ab_im
    den = a_re * a_re + a_im * a_im
    f_re = (nr * a_re + ni * a_im) / den
    f_im = (ni * a_re - nr * a_im) / den
    bb_re = f_re[:, :, None] * b_re - f_im[:, :, None] * b_im
    bb_im = f_re[:, :, None] * b_im + f_im[:, :, None] * b_re
    eye = jnp.eye(n_g, dtype=F32)
    w_re = jnp.einsum('gpc,gh->gchp', bb_re, eye).reshape(n_g * n_c, n_state)
    w_im = jnp.einsum('gpc,gh->gchp', bb_im, eye).reshape(n_g * n_c, n_state)
    o_re = jnp.einsum('gcp,gh->gphc', c_re, eye).reshape(n_state, n_g * n_c)
    o_im = jnp.einsum('gcp,gh->gphc', c_im, eye).reshape(n_state, n_g * n_c)
    n_kb = (n_g * n_c) // LANES
    sp = n_state // n_kb
    w_in = jnp.stack([jnp.concatenate([w_re[kb * LANES:(kb + 1) * LANES, kb * sp:(kb + 1) * sp],
                                       w_im[kb * LANES:(kb + 1) * LANES, kb * sp:(kb + 1) * sp]], axis=1)
                      for kb in range(n_kb)])
    w_out = jnp.stack([jnp.concatenate([o_re[kb * sp:(kb + 1) * sp, kb * LANES:(kb + 1) * LANES],
                                        -o_im[kb * sp:(kb + 1) * sp, kb * LANES:(kb + 1) * LANES]], axis=0)
                       for kb in range(n_kb)])
    j = jnp.arange(SUBLANES, dtype=F32)[:, None]
    flat = lambda t: jnp.broadcast_to(t.reshape(1, n_state), (SUBLANES, n_state))
    coef = []
    for sh in (1, 2, 4):
        pr, pim = a_pow(float(sh))
        keep = (j >= sh).astype(F32)
        coef += [flat(pr) * keep, flat(pim) * keep]
    a_re_f = jnp.broadcast_to(a_re.reshape(1, n_state), (SUBLANES, n_state))
    a_im_f = jnp.broadcast_to(a_im.reshape(1, n_state), (SUBLANES, n_state))
    dt_f = jnp.broadcast_to(jnp.repeat(dt[:, 0], n_p).reshape(1, n_state), (SUBLANES, n_state))
    mag = jnp.exp((j + 1.0) * dt_f * a_re_f)
    ph = (j + 1.0) * dt_f * a_im_f
    coef += [mag * jnp.cos(ph), mag * jnp.sin(ph)]
    coef = jnp.stack(coef)
    return w_in.astype(BF16), w_out.astype(BF16), coef, d_skip.reshape(1, n_g * n_c).astype(F32)


def _gelu_tanh(x):
    return 0.5 * x * (1.0 + jnp.tanh(math.sqrt(2.0 / math.pi) * (x + 0.044715 * (x * x * x))))


def _ssm_kernel(u_ref, win_ref, wout_ref, coef_ref, d_ref, wglu_ref, bglu_ref, o_ref,
                hre_ref, him_ref, cre_ref, cim_ref, *, rows, n_state, col_w):
    n_kb = win_ref.shape[0]
    sp = n_state // n_kb

    @pl.when(pl.program_id(1) == 0)
    def _():
        cre_ref[...] = jnp.zeros_like(cre_ref)
        cim_ref[...] = jnp.zeros_like(cim_ref)

    u = u_ref[0]
    ub = u.astype(BF16)
    for kb in range(n_kb):
        bu = _dot(ub[:, kb * LANES:(kb + 1) * LANES], win_ref[kb])
        hre_ref[:, kb * sp:(kb + 1) * sp] = bu[:, :sp]
        him_ref[:, kb * sp:(kb + 1) * sp] = bu[:, sp:]

    n_slab = rows // SUBLANES
    for c0 in range(0, n_state, col_w):
        cs = slice(c0, c0 + col_w)

        def body(i, carry, cs=cs):
            c_re, c_im = carry
            sl = pl.ds(pl.multiple_of(i * SUBLANES, SUBLANES), SUBLANES)
            x_re = hre_ref[sl, cs]
            x_im = him_ref[sl, cs]
            for si, sh in enumerate((1, 2, 4)):
                p_re = coef_ref[2 * si, :, cs]
                p_im = coef_ref[2 * si + 1, :, cs]
                s_re = pltpu.roll(x_re, sh, axis=0)
                s_im = pltpu.roll(x_im, sh, axis=0)
                x_re, x_im = (x_re + p_re * s_re - p_im * s_im,
                              x_im + p_re * s_im + p_im * s_re)
            p_re = coef_ref[6, :, cs]
            p_im = coef_ref[7, :, cs]
            h_re = x_re + p_re * c_re - p_im * c_im
            h_im = x_im + p_re * c_im + p_im * c_re
            hre_ref[sl, cs] = h_re
            him_ref[sl, cs] = h_im
            last = slice(SUBLANES - 1, SUBLANES)
            return (jnp.broadcast_to(h_re[last, :], h_re.shape),
                    jnp.broadcast_to(h_im[last, :], h_im.shape))

        c_re, c_im = lax.fori_loop(0, n_slab, body, (cre_ref[:, cs], cim_ref[:, cs]))
        cre_ref[:, cs] = c_re
        cim_ref[:, cs] = c_im

    ys = []
    for kb in range(n_kb):
        hcat = jnp.concatenate([hre_ref[:, kb * sp:(kb + 1) * sp].astype(BF16),
                                him_ref[:, kb * sp:(kb + 1) * sp].astype(BF16)], axis=1)
        ys.append(_dot(hcat, wout_ref[kb]))
    y = jnp.concatenate(ys, axis=1) + d_ref[...] * u
    y = _gelu_tanh(y)
    z = _dot(y.astype(BF16), wglu_ref[...]) + bglu_ref[...]
    o_ref[0] = y * jax.nn.sigmoid(z)


def _ssm(u, w_in, w_out, coef, d_flat, w_glu, b_glu):
    bsz, seq, ssm_w = u.shape
    n_state = coef.shape[-1]
    rows = min(SSM_ROWS, seq)
    full = lambda *shape: pl.BlockSpec(shape, lambda b, i: (0,) * len(shape))
    kern = functools.partial(_ssm_kernel, rows=rows, n_state=n_state, col_w=4 * LANES)
    return pl.pallas_call(
        kern,
        grid=(bsz, seq // rows),
        in_specs=[pl.BlockSpec((1, rows, ssm_w), lambda b, i: (b, i, 0)),
                  full(*w_in.shape), full(*w_out.shape), full(*coef.shape), full(1, ssm_w),
                  full(ssm_w, ssm_w), full(1, ssm_w)],
        out_specs=pl.BlockSpec((1, rows, ssm_w), lambda b, i: (b, i, 0)),
        out_shape=jax.ShapeDtypeStruct((bsz, seq, ssm_w), F32),
        scratch_shapes=[pltpu.VMEM((rows, n_state), F32), pltpu.VMEM((rows, n_state), F32),
                        pltpu.VMEM((SUBLANES, n_state), F32), pltpu.VMEM((SUBLANES, n_state), F32)],
        compiler_params=_cparams(("parallel", "arbitrary")),
        name="s5_glu",
    )(u, w_in, w_out, coef, d_flat, w_glu.astype(BF16), b_glu.reshape(1, ssm_w))


def _layer_norm(y, g, b):
    mu = jnp.mean(y, axis=-1, keepdims=True)
    yc = y - mu
    var = jnp.mean(yc * yc, axis=-1, keepdims=True)
    return yc * lax.rsqrt(var + EPS) * g + b


def _rms_norm(y, g):
    return y * lax.rsqrt(jnp.mean(y * y, axis=-1, keepdims=True) + EPS) * g


def _mix_kernel(x_ref, att_ref, ssm_ref, gatt_ref, gssm_ref, wout_ref, g1_ref, ln_g_ref, ln_b_ref,
                sc_ref, sh_ref, g2_ref, wsg_ref, wsu_ref, wsd_ref, h_ref, ht_ref, pre_ref, *, alpha, att_w):
    a_n = _rms_norm(att_ref[0], gatt_ref[...]).astype(BF16)
    s_n = _rms_norm(ssm_ref[0], gssm_ref[...]).astype(BF16)
    mix = _dot(a_n, wout_ref[0:att_w, :]) + _dot(s_n, wout_ref[att_w:, :])
    x1 = _layer_norm(alpha * x_ref[0] + g1_ref[0] * mix, ln_g_ref[...], ln_b_ref[...])
    h = x1 * (1.0 + sc_ref[0]) + sh_ref[0]
    h_ref[0] = h
    _store_token_tiles(ht_ref, h)
    hb = h.astype(BF16)
    hid = _silu(_dot(hb, wsg_ref[...])) * _dot(hb, wsu_ref[...])
    shared = _dot(hid.astype(BF16), wsd_ref[...])
    pre_ref[0] = alpha * x1 + g2_ref[0] * shared


def _mix(x, att, ssm, g_att, g_ssm, w_out, gate1, ln_g, ln_b, scale2, shift2, gate2,
         w_s_gate, w_s_up, w_s_down, alpha):
    bsz, seq, d = x.shape
    att_w = att.shape[-1]
    ssm_w = ssm.shape[-1]
    ff = w_s_gate.shape[1]
    tm = min(MIX_ROWS, seq)
    n_sub = d // LANES
    row = lambda b, i: (b, i, 0)
    per_b = lambda b, i: (b, 0, 0)
    full = lambda *shape: pl.BlockSpec(shape, lambda b, i: (0,) * len(shape))
    vec = lambda t: t.reshape(bsz, 1, d)
    return pl.pallas_call(
        functools.partial(_mix_kernel, alpha=alpha, att_w=att_w),
        grid=(bsz, seq // tm),
        in_specs=[pl.BlockSpec((1, tm, d), row), pl.BlockSpec((1, tm, att_w), row),
                  pl.BlockSpec((1, tm, ssm_w), row),
                  full(1, att_w), full(1, ssm_w), full(att_w + ssm_w, d),
                  pl.BlockSpec((1, 1, d), per_b), full(1, d), full(1, d),
                  pl.BlockSpec((1, 1, d), per_b), pl.BlockSpec((1, 1, d), per_b),
                  pl.BlockSpec((1, 1, d), per_b),
                  full(d, ff), full(d, ff), full(ff, d)],
        out_specs=[pl.BlockSpec((1, tm, d), row),
                   pl.BlockSpec((tm * n_sub, LANES), lambda b, i: (b * (seq // tm) + i, 0)),
                   pl.BlockSpec((1, tm, d), row)],
        out_shape=[jax.ShapeDtypeStruct((bsz, seq, d), F32),
                   jax.ShapeDtypeStruct((bsz * seq * n_sub, LANES), F32),
                   jax.ShapeDtypeStruct((bsz, seq, d), F32)],
        compiler_params=_cparams(("parallel", "parallel")),
        name="mix_ln1_shared",
    )(x, att, ssm, g_att.reshape(1, att_w), g_ssm.reshape(1, ssm_w), w_out.astype(BF16),
      vec(gate1), ln_g.reshape(1, d), ln_b.reshape(1, d), vec(scale2), vec(shift2), vec(gate2),
      w_s_gate.astype(BF16), w_s_up.astype(BF16), w_s_down.astype(BF16))


def _route_kernel(h_ref, wrt_ref, wrt_lo_ref, bias_ref, tri_ref, idx_ref, gate_ref, rank_ref, cnt_ref, carry_ref):
    n_e = wrt_ref.shape[0]
    tm = h_ref.shape[0]
    per_group = n_e // N_EXPERT_GROUPS
    neg = -jnp.inf

    @pl.when(pl.program_id(0) == 0)
    def _():
        carry_ref[...] = jnp.zeros_like(carry_ref)

    h_hi, h_lo = _split_bf16(h_ref[...])
    logits = (_dot_nt(wrt_ref[...], h_hi) + _dot_nt(wrt_ref[...], h_lo)
              + _dot_nt(wrt_lo_ref[...], h_hi))
    scores = jax.nn.sigmoid(logits)
    biased = scores + bias_ref[...]

    g3 = biased.reshape(N_EXPERT_GROUPS, per_group, tm)
    ridx = lax.broadcasted_iota(jnp.int32, g3.shape, 1).astype(F32)
    m1 = jnp.max(g3, axis=1, keepdims=True)
    first = jnp.min(jnp.where(g3 == m1, ridx, float(per_group)), axis=1, keepdims=True)
    m2 = jnp.max(jnp.where(ridx == first, neg, g3), axis=1, keepdims=True)
    gs = m1 + m2

    gidx = lax.broadcasted_iota(jnp.int32, gs.shape, 0).astype(F32)
    ok = jnp.zeros_like(gs)
    cur = gs
    for _ in range(TOPK_GROUPS):
        mx = jnp.max(cur, axis=0, keepdims=True)
        fi = jnp.min(jnp.where(cur == mx, gidx, float(N_EXPERT_GROUPS)), axis=0, keepdims=True)
        hit = gidx == fi
        ok = jnp.where(hit, 1.0, ok)
        cur = jnp.where(hit, neg, cur)
    masked = jnp.where(ok > 0.5, g3, neg).reshape(n_e, tm)

    eidx = lax.broadcasted_iota(jnp.int32, (n_e, tm), 0).astype(F32)
    onehot = jnp.zeros((n_e, tm), F32)
    cur = masked
    sel_idx = []
    sel_gate = []
    for _ in range(TOP_K):
        mx = jnp.max(cur, axis=0, keepdims=True)
        fi = jnp.min(jnp.where(cur == mx, eidx, float(n_e)), axis=0, keepdims=True)
        hit = eidx == fi
        sel_idx.append(fi)
        sel_gate.append(jnp.sum(jnp.where(hit, scores, 0.0), axis=0, keepdims=True))
        onehot = jnp.where(hit, 1.0, onehot)
        cur = jnp.where(hit, neg, cur)
    idx = jnp.concatenate(sel_idx, axis=0)
    gate = jnp.concatenate(sel_gate, axis=0)
    gate = gate / jnp.sum(gate, axis=0, keepdims=True) * ROUTED_SCALE

    prior = _dot(onehot.astype(BF16), tri_ref[...]) + carry_ref[:, 0:1]
    ranks = [jnp.sum(jnp.where(eidx == sel_idx[k], prior, 0.0), axis=0, keepdims=True)
             for k in range(TOP_K)]
    rank = jnp.concatenate(ranks, axis=0)
    carry = carry_ref[...] + jnp.sum(onehot, axis=1, keepdims=True)
    carry_ref[...] = carry
    cnt_ref[...] = carry

    idx_ref[...] = idx.astype(jnp.int32)
    gate_ref[...] = gate
    rank_ref[...] = rank.astype(jnp.int32)


def _route(h2d, w_router, router_bias):
    n_tok, d = h2d.shape
    n_e = w_router.shape[1]
    tm = min(ROUTE_COLS, n_tok)
    tri = (np.arange(tm)[:, None] < np.arange(tm)[None, :]).astype(np.float32)
    w_hi, w_lo = _split_bf16(w_router.T.astype(F32))
    col = lambda i: (0, i)
    full = lambda *shape: pl.BlockSpec(shape, lambda i: (0,) * len(shape))
    idx, gate, rank, cnt = pl.pallas_call(
        _route_kernel,
        grid=(n_tok // tm,),
        in_specs=[pl.BlockSpec((tm, d), lambda i: (i, 0)), full(n_e, d), full(n_e, d), full(n_e, 1),
                  full(tm, tm)],
        out_specs=[pl.BlockSpec((TOP_K, tm), col), pl.BlockSpec((TOP_K, tm), col),
                   pl.BlockSpec((TOP_K, tm), col), full(n_e, LANES)],
        out_shape=[jax.ShapeDtypeStruct((TOP_K, n_tok), jnp.int32),
                   jax.ShapeDtypeStruct((TOP_K, n_tok), F32),
                   jax.ShapeDtypeStruct((TOP_K, n_tok), jnp.int32),
                   jax.ShapeDtypeStruct((n_e, LANES), F32)],
        scratch_shapes=[pltpu.VMEM((n_e, LANES), F32)],
        compiler_params=_cparams(("arbitrary",)),
        name="router_topk",
    )(h2d, w_hi, w_lo, router_bias.reshape(n_e, 1).astype(F32), jnp.asarray(tri, BF16))
    return idx, gate, rank, cnt[:, 0].astype(jnp.int32)


def _dispatch(h_tiles, dest_windows, n_rows):
    n_tok, n_sub, _ = h_tiles.shape
    win = DISPATCH_WINDOW
    mesh = plsc.VectorSubcoreMesh(core_axis_name="core", subcore_axis_name="subcore")

    @pl.kernel(out_type=jax.ShapeDtypeStruct((n_rows, n_sub, LANES), h_tiles.dtype), mesh=mesh,
               scratch_types=[])
    def scatter_tiles(x_hbm, d_hbm, o_hbm):
        def window(x_vmem, d_vmem):
            for k in range(TOP_K):
                pltpu.sync_copy(x_vmem, o_hbm.at[d_vmem.at[0, pl.ds(k * win, win)]])
        pltpu.emit_pipeline(
            window,
            grid=(n_tok // win,),
            in_specs=[pl.BlockSpec((win, n_sub, LANES), lambda i: (i, 0, 0)),
                      pl.BlockSpec((1, TOP_K * win), lambda i: (i, 0))],
            out_specs=[],
            core_axis_name=("core", "subcore"),
            dimension_semantics=(pltpu.PARALLEL,),
        )(x_hbm, d_hbm)

    return scatter_tiles(h_tiles, dest_windows)


def _expert_kernel(first_ref, nblk_ref, cnt_ref, nused_ref, xs_ref, wg_ref, wu_ref, wd_ref, ys_ref,
                   xbuf_ref, ybuf_ref, wgb_ref, wub_ref, wdb_ref, xsem_ref, ysem_ref):
    e = pl.program_id(0)
    n_used = nused_ref[0]
    d = wgb_ref.shape[0]
    blk_rows = xbuf_ref.shape[1]

    def x_copy(g, slot):
        src = xs_ref.at[pl.ds(pl.multiple_of(g * blk_rows, blk_rows), blk_rows)]
        return pltpu.make_async_copy(src, xbuf_ref.at[slot], xsem_ref.at[slot])

    def y_copy(g, slot):
        dst = ys_ref.at[pl.ds(pl.multiple_of(g * blk_rows, blk_rows), blk_rows)]
        return pltpu.make_async_copy(ybuf_ref.at[slot], dst, ysem_ref.at[slot])

    n_slot = xbuf_ref.shape[0]

    @pl.when(e == 0)
    def _():
        for g in range(n_slot - 1):
            @pl.when(g < n_used)
            def _(g=g):
                x_copy(g, g).start()

    @pl.when(nblk_ref[e] > 0)
    def _():
        wgb_ref[...] = wg_ref[0].astype(BF16)
        wub_ref[...] = wu_ref[0].astype(BF16)
        wdb_ref[...] = wd_ref[0].astype(BF16)

    def block(j, carry):
        g = first_ref[e] + j
        slot = g & (n_slot - 1)
        x_copy(g, slot).wait()

        @pl.when(g + n_slot - 1 < n_used)
        def _():
            x_copy(g + n_slot - 1, (g + n_slot - 1) & (n_slot - 1)).start()

        @pl.when(g >= n_slot)
        def _():
            y_copy(g - n_slot, slot).wait()

        x = jnp.concatenate(_load_token_tiles(xbuf_ref.at[slot], MOE_ROWS, d), axis=1)
        row = lax.broadcasted_iota(jnp.int32, (MOE_ROWS, 1), 0)
        xb = jnp.where(row < cnt_ref[e] - j * MOE_ROWS, x, 0.0).astype(BF16)
        hid = _silu(_dot(xb, wgb_ref[...])) * _dot(xb, wub_ref[...])
        _store_token_tiles(ybuf_ref.at[slot], _dot(hid.astype(BF16), wdb_ref[...]))
        y_copy(g, slot).start()
        return carry
    lax.fori_loop(0, nblk_ref[e], block, 0)

    @pl.when(e == pl.num_programs(0) - 1)
    def _():
        for back in range(1, n_slot + 1):
            @pl.when(n_used >= back)
            def _(back=back):
                y_copy(n_used - back, (n_used - back) & (n_slot - 1)).wait()


def _experts(xs_tiles, plan, w_gate, w_up, w_down):
    n_rows, n_sub, _ = xs_tiles.shape
    d = n_sub * LANES
    n_e, _, ff = w_gate.shape
    blk_rows = MOE_ROWS * n_sub
    hbm = pl.BlockSpec(memory_space=pl.ANY)
    grid_spec = pltpu.PrefetchScalarGridSpec(
        num_scalar_prefetch=4,
        grid=(n_e,),
        in_specs=[hbm,
                  pl.BlockSpec((1, d, ff), lambda e, *_: (e, 0, 0)),
                  pl.BlockSpec((1, d, ff), lambda e, *_: (e, 0, 0)),
                  pl.BlockSpec((1, ff, d), lambda e, *_: (e, 0, 0))],
        out_specs=hbm,
        scratch_shapes=[pltpu.VMEM((EXPERT_SLOTS, blk_rows, LANES), F32),
                        pltpu.VMEM((EXPERT_SLOTS, blk_rows, LANES), F32),
                        pltpu.VMEM((d, ff), BF16), pltpu.VMEM((d, ff), BF16), pltpu.VMEM((ff, d), BF16),
                        pltpu.SemaphoreType.DMA((EXPERT_SLOTS,)), pltpu.SemaphoreType.DMA((EXPERT_SLOTS,))],
    )
    ys = pl.pallas_call(
        _expert_kernel,
        grid_spec=grid_spec,
        out_shape=jax.ShapeDtypeStruct((n_rows * n_sub, LANES), F32),
        compiler_params=_cparams(("arbitrary",)),
        name="moe_experts",
    )(*plan, xs_tiles.reshape(n_rows * n_sub, LANES), w_gate, w_up, w_down)
    return ys.reshape(n_rows, n_sub, LANES)


def _combine_kernel(dest_ref, ys_ref, gate_ref, pre_ref, g2_ref, ln_g_ref, ln_b_ref, o_ref, buf_ref, sem_ref):
    tm, d = pre_ref.shape[1], pre_ref.shape[2]
    n_sub = d // LANES

    def row_copy(t, k, src_row):
        slot = pl.multiple_of((k * tm + t) * n_sub, n_sub)
        return pltpu.make_async_copy(ys_ref.at[src_row], buf_ref.at[pl.ds(slot, n_sub)], sem_ref.at[0])

    def issue(g, c):
        t0 = g * DMA_ISSUE_UNROLL
        rows = [[dest_ref[(t0 + j) * TOP_K + k] for k in range(TOP_K)] for j in range(DMA_ISSUE_UNROLL)]
        for j in range(DMA_ISSUE_UNROLL):
            for k in range(TOP_K):
                row_copy(t0 + j, k, rows[j][k]).start(priority=k % 2)
        return c
    lax.fori_loop(0, tm // DMA_ISSUE_UNROLL, issue, 0)

    def drain(t, c):
        for k in range(TOP_K):
            row_copy(t, k, dest_ref[t * TOP_K + k]).wait()
        return c
    lax.fori_loop(0, tm, drain, 0)

    gates = gate_ref[...]
    gate_cols = [jnp.broadcast_to(gates[:, k:k + 1], (tm, LANES)) for k in range(TOP_K)]
    pieces = []
    for s in range(n_sub):
        acc = None
        for k in range(TOP_K):
            term = buf_ref[pl.ds(k * tm * n_sub + s, tm, stride=n_sub), :] * gate_cols[k]
            acc = term if acc is None else acc + term
        pieces.append(acc)
    routed = jnp.concatenate(pieces, axis=1)
    y = pre_ref[0] + g2_ref[0] * routed
    o_ref[0] = _layer_norm(y, ln_g_ref[...], ln_b_ref[...])


def _combine(ys, dest, gates_t, pre, gate2, ln_g, ln_b):
    bsz, seq, d = pre.shape
    tm = min(DISPATCH_ROWS, seq)
    per_seq = seq // tm
    row = lambda i, *_: (i // per_seq, i % per_seq, 0)
    per_b = lambda i, *_: (i // per_seq, 0, 0)
    full = lambda *shape: pl.BlockSpec(shape, lambda i, *_: (0,) * len(shape))
    grid_spec = pltpu.PrefetchScalarGridSpec(
        num_scalar_prefetch=0,
        grid=(bsz * per_seq,),
        in_specs=[pl.BlockSpec((TOP_K * tm,), lambda i: (i,), memory_space=pltpu.SMEM),
                  pl.BlockSpec(memory_space=pl.ANY),
                  pl.BlockSpec((tm, TOP_K), lambda i: (i, 0)),
                  pl.BlockSpec((1, tm, d), row), pl.BlockSpec((1, 1, d), per_b), full(1, d), full(1, d)],
        out_specs=pl.BlockSpec((1, tm, d), row),
        scratch_shapes=[pltpu.VMEM((TOP_K * tm * (d // LANES), LANES), F32), pltpu.SemaphoreType.DMA((1,))],
    )
    return pl.pallas_call(
        _combine_kernel,
        grid_spec=grid_spec,
        out_shape=jax.ShapeDtypeStruct((bsz, seq, d), F32),
        compiler_params=_cparams(("arbitrary",)),
        name="moe_combine_ln2",
    )(dest, ys, gates_t, pre, gate2.reshape(bsz, 1, d), ln_g.reshape(1, d), ln_b.reshape(1, d))


def _dest_kernel(pstart_ref, idx_ref, rank_ref, dest_ref):
    idx = idx_ref[...]

    def body(e, acc):
        return acc + jnp.where(idx == e, pstart_ref[e], 0)
    dest_ref[...] = lax.fori_loop(0, pstart_ref.shape[0], body, rank_ref[...], unroll=8)


def _dest_rows(pstart, idx, rank):
    n_k, n_tok = idx.shape
    tn = min(4096, n_tok)
    grid_spec = pltpu.PrefetchScalarGridSpec(
        num_scalar_prefetch=1,
        grid=(n_tok // tn,),
        in_specs=[pl.BlockSpec((n_k, tn), lambda i, *_: (0, i))] * 2,
        out_specs=pl.BlockSpec((n_k, tn), lambda i, *_: (0, i)),
    )
    return pl.pallas_call(
        _dest_kernel,
        grid_spec=grid_spec,
        out_shape=jax.ShapeDtypeStruct((n_k, n_tok), jnp.int32),
        compiler_params=_cparams(("arbitrary",)),
        name="moe_dest",
    )(pstart, idx, rank)


def _moe_plan(idx, rank, cnt, n_tok):
    n_e = cnt.shape[0]
    padded = (cnt + MOE_ROWS - 1) // MOE_ROWS * MOE_ROWS
    pend = jnp.cumsum(padded).astype(jnp.int32)
    pstart = pend - padded
    dest_kt = _dest_rows(pstart, idx, rank)
    dest = dest_kt.T.reshape(-1)
    win = min(DISPATCH_WINDOW, n_tok)
    dest_windows = dest_kt.reshape(TOP_K, n_tok // win, win).transpose(1, 0, 2).reshape(n_tok // win, TOP_K * win)
    n_blk = n_tok * TOP_K // MOE_ROWS + n_e
    n_used = (pend[-1:] // MOE_ROWS).astype(jnp.int32)
    plan = ((pstart // MOE_ROWS).astype(jnp.int32), (padded // MOE_ROWS).astype(jnp.int32),
            cnt.astype(jnp.int32), n_used)
    return dest, dest_windows, plan, n_blk * MOE_ROWS


def kernel(x, c, rel_bias, w_ada, b_ada, w_in, ssm_a_re, ssm_a_im, ssm_b_re, ssm_b_im, ssm_c_re, ssm_c_im, ssm_d, ssm_log_dt, w_glu, b_glu, g_att, g_ssm, w_out, ln1_g, ln1_b, w_router, router_bias, w_e_gate, w_e_up, w_e_down, w_s_gate, w_s_up, w_s_down, ln2_g, ln2_b):
    bsz, seq, d = x.shape
    depth = w_ada.shape[0]
    alpha = (2 * depth) ** 0.25
    att_w = g_att.shape[-1]
    tables = _att_tables(rel_bias)
    for layer in range(depth):
        ada = _ada(c, w_ada[layer], b_ada[layer])
        shift1, scale1, gate1, shift2, scale2, gate2 = jnp.split(ada, 6, axis=-1)
        q, k, v, u = _inproj(x, scale1, shift1, w_in[layer], att_w)
        att = _attention(q, k, v, tables)
        ssm_prm = _ssm_params(ssm_a_re[layer], ssm_a_im[layer], ssm_b_re[layer], ssm_b_im[layer],
                              ssm_c_re[layer], ssm_c_im[layer], ssm_d[layer], ssm_log_dt[layer])
        ssm = _ssm(u, *ssm_prm, w_glu[layer], b_glu[layer])
        h2, h2_tiles, pre = _mix(x, att, ssm, g_att[layer], g_ssm[layer], w_out[layer], gate1,
                                 ln1_g[layer], ln1_b[layer], scale2, shift2, gate2, w_s_gate[layer],
                                 w_s_up[layer], w_s_down[layer], alpha)
        h2d = h2.reshape(bsz * seq, d)
        idx, gate, rank, cnt = _route(h2d, w_router[layer], router_bias[layer])
        dest, dest_windows, plan, n_rows = _moe_plan(idx, rank, cnt, bsz * seq)
        xs = _dispatch(h2_tiles.reshape(bsz * seq, d // LANES, LANES), dest_windows, n_rows)
        ys = _experts(xs, plan, w_e_gate[layer], w_e_up[layer], w_e_down[layer])
        x = _combine(ys, dest, gate.T, pre, gate2, ln2_g[layer], ln2_b[layer])
    return x
```

```python
import functools
import math

import jax
import jax.numpy as jnp
import numpy as np
from jax import lax
from jax.experimental import pallas as pl
from jax.experimental.pallas import tpu as pltpu
from jax.experimental.pallas import tpu_sc as plsc

F32 = jnp.float32
BF16 = jnp.bfloat16

HEAD_DIM = 64
ATT_BLOCK = 128
PATTERNS = ((128, 1), (512, 4), (2048, 16))
N_BUCKETS = 32
MAX_DISTANCE = 2048
SSM_GROUP = 16
SSM_STATE = 64
N_EXPERTS = 256
TOP_K = 8
N_EXPERT_GROUPS = 8
TOPK_GROUPS = 4
ROUTED_SCALE = 2.5
EPS = 1e-5
NEG_INF = -1e30

LANES = 128
SUBLANES = 8
VMEM_LIMIT_BYTES = 56 * 1024 * 1024

PROJ_ROWS = 512
SSM_ROWS = 256
MIX_ROWS = 512
ROUTE_COLS = 512
MOE_ROWS = 256
DISPATCH_ROWS = 256
DISPATCH_WINDOW = 32
ATT_UNITS_PER_STEP = 8
DMA_ISSUE_UNROLL = 4
EXPERT_SLOTS = 4


def _cparams(sem, vmem=VMEM_LIMIT_BYTES):
    return pltpu.CompilerParams(dimension_semantics=sem, vmem_limit_bytes=vmem)


def _dot(a, b):
    return jnp.dot(a, b, preferred_element_type=F32)


def _dot_nt(a, b):
    return lax.dot_general(a, b, (((1,), (1,)), ((), ())), preferred_element_type=F32)


def _silu(x):
    return x * jax.nn.sigmoid(x)


def _store_token_tiles(ref, x):
    m, d = x.shape
    n_sub = d // LANES
    for s in range(n_sub):
        ref[pl.ds(s, m, stride=n_sub), :] = x[:, s * LANES:(s + 1) * LANES]


def _load_token_tiles(ref, m, d, row0=0):
    n_sub = d // LANES
    return [ref[pl.ds(row0 + s, m, stride=n_sub), :] for s in range(n_sub)]


def _split_bf16(x):
    hi = x.astype(BF16)
    lo = (x - hi.astype(F32)).astype(BF16)
    return hi, lo


def _ada_kernel(c_ref, w_ref, b_ref, o_ref):
    c = c_ref[...]
    a_hi, a_lo = _split_bf16(_silu(c))
    w_hi, w_lo = _split_bf16(w_ref[...])
    acc = _dot(a_hi, w_hi) + _dot(a_hi, w_lo) + _dot(a_lo, w_hi)
    o_ref[...] = acc + b_ref[...]


def _ada(c, w_ada, b_ada):
    bsz, d = c.shape
    n = w_ada.shape[1]
    rows = SUBLANES
    c_pad = jnp.zeros((rows, d), F32).at[:bsz].set(c)
    tn = 1024
    out = pl.pallas_call(
        _ada_kernel,
        grid=(n // tn,),
        in_specs=[pl.BlockSpec((rows, d), lambda j: (0, 0)),
                  pl.BlockSpec((d, tn), lambda j: (0, j)),
                  pl.BlockSpec((1, tn), lambda j: (0, j))],
        out_specs=pl.BlockSpec((rows, tn), lambda j: (0, j)),
        out_shape=jax.ShapeDtypeStruct((rows, n), F32),
        compiler_params=_cparams(("parallel",)),
        name="ada",
    )(c_pad, w_ada, b_ada.reshape(1, n))
    return out[:bsz]


def _inproj_kernel(x_ref, sc_ref, sh_ref, w_ref, q_ref, k_ref, v_ref, u_ref, *, att_w, q_scale):
    h = (x_ref[0] * (1.0 + sc_ref[0]) + sh_ref[0]).astype(BF16)
    q_ref[0] = _dot(h, w_ref[:, 0:att_w]) * q_scale
    k_ref[0] = _dot(h, w_ref[:, att_w:2 * att_w])
    v_ref[0] = _dot(h, w_ref[:, 2 * att_w:3 * att_w])
    u_ref[0] = _dot(h, w_ref[:, 3 * att_w:])


def _inproj(x, scale, shift, w_in, att_w):
    bsz, seq, d = x.shape
    n = w_in.shape[1]
    ssm_w = n - 3 * att_w
    ts = min(PROJ_ROWS, seq)
    row = lambda b, i: (b, i, 0)
    per_b = lambda b, i: (b, 0, 0)
    kern = functools.partial(_inproj_kernel, att_w=att_w, q_scale=HEAD_DIM ** -0.5)
    return pl.pallas_call(
        kern,
        grid=(bsz, seq // ts),
        in_specs=[pl.BlockSpec((1, ts, d), row),
                  pl.BlockSpec((1, 1, d), per_b),
                  pl.BlockSpec((1, 1, d), per_b),
                  pl.BlockSpec((d, n), lambda b, i: (0, 0))],
        out_specs=[pl.BlockSpec((1, ts, att_w), row)] * 3 + [pl.BlockSpec((1, ts, ssm_w), row)],
        out_shape=[jax.ShapeDtypeStruct((bsz, seq, att_w), F32)] * 3
                  + [jax.ShapeDtypeStruct((bsz, seq, ssm_w), F32)],
        compiler_params=_cparams(("parallel", "parallel")),
        name="inproj",
    )(x, scale.reshape(bsz, 1, d), shift.reshape(bsz, 1, d), w_in.astype(BF16))


def _t5_bucket_np(dist):
    exact = N_BUCKETS // 2
    large = exact + (np.log(np.maximum(dist, 1).astype(np.float64) / exact)
                     / math.log(MAX_DISTANCE / exact) * (N_BUCKETS - exact)).astype(np.int64)
    return np.where(dist < exact, dist, np.minimum(large, N_BUCKETS - 1))


def _att_tables(rel_bias):
    qi = np.arange(ATT_BLOCK)[:, None]
    ki = np.arange(2 * ATT_BLOCK)[None, :]
    rel = qi + ATT_BLOCK - ki
    tabs = []
    for window, dil in PATTERNS:
        band = (rel >= 0) & (rel <= window // dil)
        bucket = _t5_bucket_np(np.maximum(rel, 0) * dil)
        onehot = (bucket[:, :, None] == np.arange(N_BUCKETS)[None, None, :]).astype(np.float32)
        bias = jnp.einsum('qkb,bh->hqk', onehot, rel_bias.astype(F32),
                          precision=lax.Precision.HIGHEST)
        full = jnp.where(band[None], bias, NEG_INF)
        first = jnp.concatenate([full[:, :, ATT_BLOCK:], jnp.full_like(full[:, :, ATT_BLOCK:], NEG_INF)], axis=-1)
        tabs.append(jnp.stack([full, first]))
    return jnp.stack(tabs)


def _att_kernel(q_ref, k_ref, v_ref, tab_ref, o_ref, m_ref, l_ref, *, seq):
    lane = lax.broadcasted_iota(jnp.int32, (ATT_BLOCK, LANES), 1)
    head0 = lane < HEAD_DIM

    def rows(ref, start, n, dil):
        if dil == 1:
            return ref[0, pl.ds(pl.multiple_of(start, ATT_BLOCK), n), :]
        return ref[0, pl.ds(start, n, stride=dil), :]

    def lane_mask(n_rows, h):
        m = lax.broadcasted_iota(jnp.int32, (n_rows, LANES), 1) < HEAD_DIM
        return m if h == 0 else ~m

    def batch(pi, dil, nb, res, blk0, head, init):
        q_rows = nb * ATT_BLOCK
        kv_blocks = nb if head else nb + 1
        q_start = res + dil * ATT_BLOCK * blk0
        kv_start = q_start if head else q_start - dil * ATT_BLOCK
        if dil == 1:
            sl = pl.ds(pl.multiple_of(q_start, ATT_BLOCK), q_rows)
        else:
            sl = pl.ds(q_start, q_rows, stride=dil)
        q = rows(q_ref, q_start, q_rows, dil)
        k = rows(k_ref, kv_start, kv_blocks * ATT_BLOCK, dil).astype(BF16)
        v = rows(v_ref, kv_start, kv_blocks * ATT_BLOCK, dil)
        if not init:
            m_old, l_old, o_old = m_ref[sl, :], l_ref[sl, :], o_ref[0, sl, :]
        qh = [jnp.where(lane_mask(q_rows, h), q, 0.0).astype(BF16) for h in range(2)]
        vh = [jnp.where(lane_mask(kv_blocks * ATT_BLOCK, h), v, 1.0).astype(BF16) for h in range(2)]
        o_parts, l_parts, m_parts = [], [], []
        for j in range(nb):
            first = 1 if (head and j == 0) else 0
            kb = j if (not head or j == 0) else j - 1
            ksl = slice(kb * ATT_BLOCK, (kb + 2) * ATT_BLOCK)
            qsl = slice(j * ATT_BLOCK, (j + 1) * ATT_BLOCK)
            outs, ms = [], []
            for h in range(2):
                s = _dot_nt(qh[h][qsl], k[ksl]) + tab_ref[pi, first, h]
                m = jnp.max(s, axis=1, keepdims=True)
                p = jnp.exp(s - m).astype(BF16)
                outs.append(_dot(p, vh[h][ksl]))
                ms.append(m)
            o_parts.append(jnp.where(head0, outs[0], outs[1]))
            l_parts.append(pltpu.roll(jnp.where(head0, outs[1], outs[0]), HEAD_DIM, axis=1))
            m_parts.append(jnp.where(head0, ms[0], ms[1]))
        o = jnp.concatenate(o_parts, axis=0)
        l = jnp.concatenate(l_parts, axis=0)
        m = jnp.concatenate(m_parts, axis=0)
        if not init:
            m_new = jnp.maximum(m_old, m)
            a_old = jnp.exp(m_old - m_new)
            a_cur = jnp.exp(m - m_new)
            o = o_old * a_old + o * a_cur
            l = l_old * a_old + l * a_cur
            m = m_new
        o_ref[0, sl, :] = o
        m_ref[sl, :] = m
        l_ref[sl, :] = l

    n_units = seq // ATT_BLOCK
    order = sorted(range(len(PATTERNS)), key=lambda p: -PATTERNS[p][1])
    for pos, pi in enumerate(order):
        dil = PATTERNS[pi][1]
        log_d = dil.bit_length() - 1
        res_blocks = seq // (dil * ATT_BLOCK)
        nb = min(ATT_UNITS_PER_STEP, res_blocks)
        per_res = res_blocks // nb
        init = pos == 0

        def head_body(res, carry, pi=pi, dil=dil, nb=nb, init=init):
            batch(pi, dil, nb, res, 0, True, init)
            return carry
        lax.fori_loop(0, dil, head_body, 0)

        if per_res > 1:
            def tail_body(i, carry, pi=pi, dil=dil, nb=nb, log_d=log_d, init=init):
                res = i & (dil - 1)
                blk0 = nb * (1 + (i >> log_d))
                batch(pi, dil, nb, res, blk0, False, init)
                return carry
            lax.fori_loop(0, dil * (per_res - 1), tail_body, 0)

    def finish(i, carry):
        sl = pl.ds(pl.multiple_of(i * ATT_BLOCK, ATT_BLOCK), ATT_BLOCK)
        o_ref[0, sl, :] = o_ref[0, sl, :] / l_ref[sl, :]
        return carry
    lax.fori_loop(0, n_units, finish, 0)


def _attention(q, k, v, tables):
    bsz, seq, att_w = q.shape
    n_pairs = att_w // LANES
    blk = lambda b, p: (b, 0, p)
    qkv_spec = pl.BlockSpec((1, seq, LANES), blk)
    n_pat = tables.shape[0]
    tab_spec = pl.BlockSpec((n_pat, 2, 2, ATT_BLOCK, 2 * ATT_BLOCK), lambda b, p: (0, 0, p, 0, 0))
    return pl.pallas_call(
        functools.partial(_att_kernel, seq=seq),
        grid=(bsz, n_pairs),
        in_specs=[qkv_spec, qkv_spec, qkv_spec, tab_spec],
        out_specs=pl.BlockSpec((1, seq, LANES), blk),
        out_shape=jax.ShapeDtypeStruct((bsz, seq, att_w), F32),
        scratch_shapes=[pltpu.VMEM((seq, LANES), F32), pltpu.VMEM((seq, LANES), F32)],
        compiler_params=_cparams(("parallel", "parallel")),
        name="dilated_attention",
    )(q, k, v, tables)


def _ssm_params(a_re, a_im, b_re, b_im, c_re, c_im, d_skip, log_dt):
    n_g, n_p = a_re.shape
    n_c = b_re.shape[-1]
    n_state = n_g * n_p
    dt = jnp.exp(log_dt.astype(F32))[:, None]

    def a_pow(kk):
        mag = jnp.exp(kk * dt * a_re)
        ph = kk * dt * a_im
        return mag * jnp.cos(ph), mag * jnp.sin(ph)

    ab_re, ab_im = a_pow(1.0)
    nr, ni = ab_re - 1.0, ab_im
    den = a_re * a_re + a_im * a_im
    f_re = (nr * a_re + ni * a_im) / den
    f_im = (ni * a_re - nr * a_im) / den
    bb_re = f_re[:, :, None] * b_re - f_im[:, :, None] * b_im
    bb_im = f_re[:, :, None] * b_im + f_im[:, :, None] * b_re
    eye = jnp.eye(n_g, dtype=F32)
    w_re = jnp.einsum('gpc,gh->gchp', bb_re, eye).reshape(n_g * n_c, n_state)
    w_im = jnp.einsum('gpc,gh->gchp', bb_im, eye).reshape(n_g * n_c, n_state)
    o_re = jnp.einsum('gcp,gh->gphc', c_re, eye).reshape(n_state, n_g * n_c)
    o_im = jnp.einsum('gcp,gh->gphc', c_im, eye).reshape(n_state, n_g * n_c)
    n_kb = (n_g * n_c) // LANES
    sp = n_state // n_kb
    w_in = jnp.stack([jnp.concatenate([w_re[kb * LANES:(kb + 1) * LANES, kb * sp:(kb + 1) * sp],
                                       w_im[kb * LANES:(kb + 1) * LANES, kb * sp:(kb + 1) * sp]], axis=1)
                      for kb in range(n_kb)])
    w_out = jnp.stack([jnp.concatenate([o_re[kb * sp:(kb + 1) * sp, kb * LANES:(kb + 1) * LANES],
                                        -o_im[kb * sp:(kb + 1) * sp, kb * LANES:(kb + 1) * LANES]], axis=0)
                       for kb in range(n_kb)])
    j = jnp.arange(SUBLANES, dtype=F32)[:, None]
    flat = lambda t: jnp.broadcast_to(t.reshape(1, n_state), (SUBLANES, n_state))
    coef = []
    for sh in (1, 2, 4):
        pr, pim = a_pow(float(sh))
        keep = (j >= sh).astype(F32)
        coef += [flat(pr) * keep, flat(pim) * keep]
    a_re_f = jnp.broadcast_to(a_re.reshape(1, n_state), (SUBLANES, n_state))
    a_im_f = jnp.broadcast_to(a_im.reshape(1, n_state), (SUBLANES, n_state))
    dt_f = jnp.broadcast_to(jnp.repeat(dt[:, 0], n_p).reshape(1, n_state), (SUBLANES, n_state))
    mag = jnp.exp((j + 1.0) * dt_f * a_re_f)
    ph = (j + 1.0) * dt_f * a_im_f
    coef += [mag * jnp.cos(ph), mag * jnp.sin(ph)]
    coef = jnp.stack(coef)
    return w_in.astype(BF16), w_out.astype(BF16), coef, d_skip.reshape(1, n_g * n_c).astype(F32)


def _gelu_tanh(x):
    return 0.5 * x * (1.0 + jnp.tanh(math.sqrt(2.0 / math.pi) * (x + 0.044715 * (x * x * x))))


def _ssm_kernel(u_ref, win_ref, wout_ref, coef_ref, d_ref, wglu_ref, bglu_ref, o_ref,
                hre_ref, him_ref, cre_ref, cim_ref, *, rows, n_state, col_w):
    n_kb = win_ref.shape[0]
    sp = n_state // n_kb

    @pl.when(pl.program_id(1) == 0)
    def _():
        cre_ref[...] = jnp.zeros_like(cre_ref)
        cim_ref[...] = jnp.zeros_like(cim_ref)

    u = u_ref[0]
    ub = u.astype(BF16)
    for kb in range(n_kb):
        bu = _dot(ub[:, kb * LANES:(kb + 1) * LANES], win_ref[kb])
        hre_ref[:, kb * sp:(kb + 1) * sp] = bu[:, :sp]
        him_ref[:, kb * sp:(kb + 1) * sp] = bu[:, sp:]

    n_slab = rows // SUBLANES
    for c0 in range(0, n_state, col_w):
        cs = slice(c0, c0 + col_w)

        def body(i, carry, cs=cs):
            c_re, c_im = carry
            sl = pl.ds(pl.multiple_of(i * SUBLANES, SUBLANES), SUBLANES)
            x_re = hre_ref[sl, cs]
            x_im = him_ref[sl, cs]
            for si, sh in enumerate((1, 2, 4)):
                p_re = coef_ref[2 * si, :, cs]
                p_im = coef_ref[2 * si + 1, :, cs]
                s_re = pltpu.roll(x_re, sh, axis=0)
                s_im = pltpu.roll(x_im, sh, axis=0)
                x_re, x_im = (x_re + p_re * s_re - p_im * s_im,
                              x_im + p_re * s_im + p_im * s_re)
            p_re = coef_ref[6, :, cs]
            p_im = coef_ref[7, :, cs]
            h_re = x_re + p_re * c_re - p_im * c_im
            h_im = x_im + p_re * c_im + p_im * c_re
            hre_ref[sl, cs] = h_re
            him_ref[sl, cs] = h_im
            last = slice(SUBLANES - 1, SUBLANES)
            return (jnp.broadcast_to(h_re[last, :], h_re.shape),
                    jnp.broadcast_to(h_im[last, :], h_im.shape))

        c_re, c_im = lax.fori_loop(0, n_slab, body, (cre_ref[:, cs], cim_ref[:, cs]))
        cre_ref[:, cs] = c_re
        cim_ref[:, cs] = c_im

    ys = []
    for kb in range(n_kb):
        hcat = jnp.concatenate([hre_ref[:, kb * sp:(kb + 1) * sp].astype(BF16),
                                him_ref[:, kb * sp:(kb + 1) * sp].astype(BF16)], axis=1)
        ys.append(_dot(hcat, wout_ref[kb]))
    y = jnp.concatenate(ys, axis=1) + d_ref[...] * u
    y = _gelu_tanh(y)
    z = _dot(y.astype(BF16), wglu_ref[...]) + bglu_ref[...]
    o_ref[0] = y * jax.nn.sigmoid(z)


def _ssm(u, w_in, w_out, coef, d_flat, w_glu, b_glu):
    bsz, seq, ssm_w = u.shape
    n_state = coef.shape[-1]
    rows = min(SSM_ROWS, seq)
    full = lambda *shape: pl.BlockSpec(shape, lambda b, i: (0,) * len(shape))
    kern = functools.partial(_ssm_kernel, rows=rows, n_state=n_state, col_w=4 * LANES)
    return pl.pallas_call(
        kern,
        grid=(bsz, seq // rows),
        in_specs=[pl.BlockSpec((1, rows, ssm_w), lambda b, i: (b, i, 0)),
                  full(*w_in.shape), full(*w_out.shape), full(*coef.shape), full(1, ssm_w),
                  full(ssm_w, ssm_w), full(1, ssm_w)],
        out_specs=pl.BlockSpec((1, rows, ssm_w), lambda b, i: (b, i, 0)),
        out_shape=jax.ShapeDtypeStruct((bsz, seq, ssm_w), F32),
        scratch_shapes=[pltpu.VMEM((rows, n_state), F32), pltpu.VMEM((rows, n_state), F32),
                        pltpu.VMEM((SUBLANES, n_state), F32), pltpu.VMEM((SUBLANES, n_state), F32)],
        compiler_params=_cparams(("parallel", "arbitrary")),
        name="s5_glu",
    )(u, w_in, w_out, coef, d_flat, w_glu.astype(BF16), b_glu.reshape(1, ssm_w))


def _layer_norm(y, g, b):
    mu = jnp.mean(y, axis=-1, keepdims=True)
    yc = y - mu
    var = jnp.mean(yc * yc, axis=-1, keepdims=True)
    return yc * lax.rsqrt(var + EPS) * g + b


def _rms_norm(y, g):
    return y * lax.rsqrt(jnp.mean(y * y, axis=-1, keepdims=True) + EPS) * g


def _mix_kernel(x_ref, att_ref, ssm_ref, gatt_ref, gssm_ref, wout_ref, g1_ref, ln_g_ref, ln_b_ref,
                sc_ref, sh_ref, g2_ref, wsg_ref, wsu_ref, wsd_ref, h_ref, ht_ref, pre_ref, *, alpha, att_w):
    a_n = _rms_norm(att_ref[0], gatt_ref[...]).astype(BF16)
    s_n = _rms_norm(ssm_ref[0], gssm_ref[...]).astype(BF16)
    mix = _dot(a_n, wout_ref[0:att_w, :]) + _dot(s_n, wout_ref[att_w:, :])
    x1 = _layer_norm(alpha * x_ref[0] + g1_ref[0] * mix, ln_g_ref[...], ln_b_ref[...])
    h = x1 * (1.0 + sc_ref[0]) + sh_ref[0]
    h_ref[0] = h
    _store_token_tiles(ht_ref, h)
    hb = h.astype(BF16)
    hid = _silu(_dot(hb, wsg_ref[...])) * _dot(hb, wsu_ref[...])
    shared = _dot(hid.astype(BF16), wsd_ref[...])
    pre_ref[0] = alpha * x1 + g2_ref[0] * shared


def _mix(x, att, ssm, g_att, g_ssm, w_out, gate1, ln_g, ln_b, scale2, shift2, gate2,
         w_s_gate, w_s_up, w_s_down, alpha):
    bsz, seq, d = x.shape
    att_w = att.shape[-1]
    ssm_w = ssm.shape[-1]
    ff = w_s_gate.shape[1]
    tm = min(MIX_ROWS, seq)
    n_sub = d // LANES
    row = lambda b, i: (b, i, 0)
    per_b = lambda b, i: (b, 0, 0)
    full = lambda *shape: pl.BlockSpec(shape, lambda b, i: (0,) * len(shape))
    vec = lambda t: t.reshape(bsz, 1, d)
    return pl.pallas_call(
        functools.partial(_mix_kernel, alpha=alpha, att_w=att_w),
        grid=(bsz, seq // tm),
        in_specs=[pl.BlockSpec((1, tm, d), row), pl.BlockSpec((1, tm, att_w), row),
                  pl.BlockSpec((1, tm, ssm_w), row),
                  full(1, att_w), full(1, ssm_w), full(att_w + ssm_w, d),
                  pl.BlockSpec((1, 1, d), per_b), full(1, d), full(1, d),
                  pl.BlockSpec((1, 1, d), per_b), pl.BlockSpec((1, 1, d), per_b),
                  pl.BlockSpec((1, 1, d), per_b),
                  full(d, ff), full(d, ff), full(ff, d)],
        out_specs=[pl.BlockSpec((1, tm, d), row),
                   pl.BlockSpec((tm * n_sub, LANES), lambda b, i: (b * (seq // tm) + i, 0)),
                   pl.BlockSpec((1, tm, d), row)],
        out_shape=[jax.ShapeDtypeStruct((bsz, seq, d), F32),
                   jax.ShapeDtypeStruct((bsz * seq * n_sub, LANES), F32),
                   jax.ShapeDtypeStruct((bsz, seq, d), F32)],
        compiler_params=_cparams(("parallel", "parallel")),
        name="mix_ln1_shared",
    )(x, att, ssm, g_att.reshape(1, att_w), g_ssm.reshape(1, ssm_w), w_out.astype(BF16),
      vec(gate1), ln_g.reshape(1, d), ln_b.reshape(1, d), vec(scale2), vec(shift2), vec(gate2),
      w_s_gate.astype(BF16), w_s_up.astype(BF16), w_s_down.astype(BF16))


def _route_kernel(h_ref, wrt_ref, wrt_lo_ref, bias_ref, tri_ref, idx_ref, gate_ref, rank_ref, cnt_ref, carry_ref):
    n_e = wrt_ref.shape[0]
    tm = h_ref.shape[0]
    per_group = n_e // N_EXPERT_GROUPS
    neg = -jnp.inf

    @pl.when(pl.program_id(0) == 0)
    def _():
        carry_ref[...] = jnp.zeros_like(carry_ref)

    h_hi, h_lo = _split_bf16(h_ref[...])
    logits = (_dot_nt(wrt_ref[...], h_hi) + _dot_nt(wrt_ref[...], h_lo)
              + _dot_nt(wrt_lo_ref[...], h_hi))
    scores = jax.nn.sigmoid(logits)
    biased = scores + bias_ref[...]

    g3 = biased.reshape(N_EXPERT_GROUPS, per_group, tm)
    ridx = lax.broadcasted_iota(jnp.int32, g3.shape, 1).astype(F32)
    m1 = jnp.max(g3, axis=1, keepdims=True)
    first = jnp.min(jnp.where(g3 == m1, ridx, float(per_group)), axis=1, keepdims=True)
    m2 = jnp.max(jnp.where(ridx == first, neg, g3), axis=1, keepdims=True)
    gs = m1 + m2

    gidx = lax.broadcasted_iota(jnp.int32, gs.shape, 0).astype(F32)
    ok = jnp.zeros_like(gs)
    cur = gs
    for _ in range(TOPK_GROUPS):
        mx = jnp.max(cur, axis=0, keepdims=True)
        fi = jnp.min(jnp.where(cur == mx, gidx, float(N_EXPERT_GROUPS)), axis=0, keepdims=True)
        hit = gidx == fi
        ok = jnp.where(hit, 1.0, ok)
        cur = jnp.where(hit, neg, cur)
    masked = jnp.where(ok > 0.5, g3, neg).reshape(n_e, tm)

    eidx = lax.broadcasted_iota(jnp.int32, (n_e, tm), 0).astype(F32)
    onehot = jnp.zeros((n_e, tm), F32)
    cur = masked
    sel_idx = []
    sel_gate = []
    for _ in range(TOP_K):
        mx = jnp.max(cur, axis=0, keepdims=True)
        fi = jnp.min(jnp.where(cur == mx, eidx, float(n_e)), axis=0, keepdims=True)
        hit = eidx == fi
        sel_idx.append(fi)
        sel_gate.append(jnp.sum(jnp.where(hit, scores, 0.0), axis=0, keepdims=True))
        onehot = jnp.where(hit, 1.0, onehot)
        cur = jnp.where(hit, neg, cur)
    idx = jnp.concatenate(sel_idx, axis=0)
    gate = jnp.concatenate(sel_gate, axis=0)
    gate = gate / jnp.sum(gate, axis=0, keepdims=True) * ROUTED_SCALE

    prior = _dot(onehot.astype(BF16), tri_ref[...]) + carry_ref[:, 0:1]
    ranks = [jnp.sum(jnp.where(eidx == sel_idx[k], prior, 0.0), axis=0, keepdims=True)
             for k in range(TOP_K)]
    rank = jnp.concatenate(ranks, axis=0)
    carry = carry_ref[...] + jnp.sum(onehot, axis=1, keepdims=True)
    carry_ref[...] = carry
    cnt_ref[...] = carry

    idx_ref[...] = idx.astype(jnp.int32)
    gate_ref[...] = gate
    rank_ref[...] = rank.astype(jnp.int32)


def _route(h2d, w_router, router_bias):
    n_tok, d = h2d.shape
    n_e = w_router.shape[1]
    tm = min(ROUTE_COLS, n_tok)
    tri = (np.arange(tm)[:, None] < np.arange(tm)[None, :]).astype(np.float32)
    w_hi, w_lo = _split_bf16(w_router.T.astype(F32))
    col = lambda i: (0, i)
    full = lambda *shape: pl.BlockSpec(shape, lambda i: (0,) * len(shape))
    idx, gate, rank, cnt = pl.pallas_call(
        _route_kernel,
        grid=(n_tok // tm,),
        in_specs=[pl.BlockSpec((tm, d), lambda i: (i, 0)), full(n_e, d), full(n_e, d), full(n_e, 1),
                  full(tm, tm)],
        out_specs=[pl.BlockSpec((TOP_K, tm), col), pl.BlockSpec((TOP_K, tm), col),
                   pl.BlockSpec((TOP_K, tm), col), full(n_e, LANES)],
        out_shape=[jax.ShapeDtypeStruct((TOP_K, n_tok), jnp.int32),
                   jax.ShapeDtypeStruct((TOP_K, n_tok), F32),
                   jax.ShapeDtypeStruct((TOP_K, n_tok), jnp.int32),
                   jax.ShapeDtypeStruct((n_e, LANES), F32)],
        scratch_shapes=[pltpu.VMEM((n_e, LANES), F32)],
        compiler_params=_cparams(("arbitrary",)),
        name="router_topk",
    )(h2d, w_hi, w_lo, router_bias.reshape(n_e, 1).astype(F32), jnp.asarray(tri, BF16))
    return idx, gate, rank, cnt[:, 0].astype(jnp.int32)


def _dispatch(h_tiles, dest_windows, n_rows):
    n_tok, n_sub, _ = h_tiles.shape
    win = DISPATCH_WINDOW
    mesh = plsc.VectorSubcoreMesh(core_axis_name="core", subcore_axis_name="subcore")

    @pl.kernel(out_type=jax.ShapeDtypeStruct((n_rows, n_sub, LANES), h_tiles.dtype), mesh=mesh,
               scratch_types=[])
    def scatter_tiles(x_hbm, d_hbm, o_hbm):
        def window(x_vmem, d_vmem):
            for k in range(TOP_K):
                pltpu.sync_copy(x_vmem, o_hbm.at[d_vmem.at[0, pl.ds(k * win, win)]])
        pltpu.emit_pipeline(
            window,
            grid=(n_tok // win,),
            in_specs=[pl.BlockSpec((win, n_sub, LANES), lambda i: (i, 0, 0)),
                      pl.BlockSpec((1, TOP_K * win), lambda i: (i, 0))],
            out_specs=[],
            core_axis_name=("core", "subcore"),
            dimension_semantics=(pltpu.PARALLEL,),
        )(x_hbm, d_hbm)

    return scatter_tiles(h_tiles, dest_windows)


def _expert_kernel(first_ref, nblk_ref, cnt_ref, nused_ref, xs_ref, wg_ref, wu_ref, wd_ref, ys_ref,
                   xbuf_ref, ybuf_ref, wgb_ref, wub_ref, wdb_ref, xsem_ref, ysem_ref):
    e = pl.program_id(0)
    n_used = nused_ref[0]
    d = wgb_ref.shape[0]
    blk_rows = xbuf_ref.shape[1]

    def x_copy(g, slot):
        src = xs_ref.at[pl.ds(pl.multiple_of(g * blk_rows, blk_rows), blk_rows)]
        return pltpu.make_async_copy(src, xbuf_ref.at[slot], xsem_ref.at[slot])

    def y_copy(g, slot):
        dst = ys_ref.at[pl.ds(pl.multiple_of(g * blk_rows, blk_rows), blk_rows)]
        return pltpu.make_async_copy(ybuf_ref.at[slot], dst, ysem_ref.at[slot])

    n_slot = xbuf_ref.shape[0]

    @pl.when(e == 0)
    def _():
        for g in range(n_slot - 1):
            @pl.when(g < n_used)
            def _(g=g):
                x_copy(g, g).start()

    @pl.when(nblk_ref[e] > 0)
    def _():
        wgb_ref[...] = wg_ref[0].astype(BF16)
        wub_ref[...] = wu_ref[0].astype(BF16)
        wdb_ref[...] = wd_ref[0].astype(BF16)

    def block(j, carry):
        g = first_ref[e] + j
        slot = g & (n_slot - 1)
        x_copy(g, slot).wait()

        @pl.when(g + n_slot - 1 < n_used)
        def _():
            x_copy(g + n_slot - 1, (g + n_slot - 1) & (n_slot - 1)).start()

        @pl.when(g >= n_slot)
        def _():
            y_copy(g - n_slot, slot).wait()

        x = jnp.concatenate(_load_token_tiles(xbuf_ref.at[slot], MOE_ROWS, d), axis=1)
        row = lax.broadcasted_iota(jnp.int32, (MOE_ROWS, 1), 0)
        xb = jnp.where(row < cnt_ref[e] - j * MOE_ROWS, x, 0.0).astype(BF16)
        hid = _silu(_dot(xb, wgb_ref[...])) * _dot(xb, wub_ref[...])
        _store_token_tiles(ybuf_ref.at[slot], _dot(hid.astype(BF16), wdb_ref[...]))
        y_copy(g, slot).start()
        return carry
    lax.fori_loop(0, nblk_ref[e], block, 0)

    @pl.when(e == pl.num_programs(0) - 1)
    def _():
        for back in range(1, n_slot + 1):
            @pl.when(n_used >= back)
            def _(back=back):
                y_copy(n_used - back, (n_used - back) & (n_slot - 1)).wait()


def _experts(xs_tiles, plan, w_gate, w_up, w_down):
    n_rows, n_sub, _ = xs_tiles.shape
    d = n_sub * LANES
    n_e, _, ff = w_gate.shape
    blk_rows = MOE_ROWS * n_sub
    hbm = pl.BlockSpec(memory_space=pl.ANY)
    grid_spec = pltpu.PrefetchScalarGridSpec(
        num_scalar_prefetch=4,
        grid=(n_e,),
        in_specs=[hbm,
                  pl.BlockSpec((1, d, ff), lambda e, *_: (e, 0, 0)),
                  pl.BlockSpec((1, d, ff), lambda e, *_: (e, 0, 0)),
                  pl.BlockSpec((1, ff, d), lambda e, *_: (e, 0, 0))],
        out_specs=hbm,
        scratch_shapes=[pltpu.VMEM((EXPERT_SLOTS, blk_rows, LANES), F32),
                        pltpu.VMEM((EXPERT_SLOTS, blk_rows, LANES), F32),
                        pltpu.VMEM((d, ff), BF16), pltpu.VMEM((d, ff), BF16), pltpu.VMEM((ff, d), BF16),
                        pltpu.SemaphoreType.DMA((EXPERT_SLOTS,)), pltpu.SemaphoreType.DMA((EXPERT_SLOTS,))],
    )
    ys = pl.pallas_call(
        _expert_kernel,
        grid_spec=grid_spec,
        out_shape=jax.ShapeDtypeStruct((n_rows * n_sub, LANES), F32),
        compiler_params=_cparams(("arbitrary",)),
        name="moe_experts",
    )(*plan, xs_tiles.reshape(n_rows * n_sub, LANES), w_gate, w_up, w_down)
    return ys.reshape(n_rows, n_sub, LANES)


def _combine_kernel(dest_ref, dest_next_ref, ys_ref, gate_ref, pre_ref, g2_ref, ln_g_ref, ln_b_ref, o_ref,
                    buf_ref, sem_ref):
    tm, d = pre_ref.shape[1], pre_ref.shape[2]
    n_sub = d // LANES
    half_rows = TOP_K * tm * n_sub
    i = pl.program_id(0)
    cur = i & 1

    def row_copy(half, t, k, src_row):
        off = pl.multiple_of(half * half_rows + (k * tm + t) * n_sub, n_sub)
        return pltpu.make_async_copy(ys_ref.at[src_row], buf_ref.at[pl.ds(off, n_sub)], sem_ref.at[half])

    def issue_tile(rows_ref, half):
        def issue(g, c):
            t0 = g * DMA_ISSUE_UNROLL
            rows = [[rows_ref[(t0 + j) * TOP_K + k] for k in range(TOP_K)] for j in range(DMA_ISSUE_UNROLL)]
            for j in range(DMA_ISSUE_UNROLL):
                for k in range(TOP_K):
                    row_copy(half, t0 + j, k, rows[j][k]).start(priority=k % 2)
            return c
        lax.fori_loop(0, tm // DMA_ISSUE_UNROLL, issue, 0)

    @pl.when(i == 0)
    def _():
        issue_tile(dest_ref, 0)

    @pl.when(i + 1 < pl.num_programs(0))
    def _():
        issue_tile(dest_next_ref, 1 - cur)

    def drain(t, c):
        for k in range(TOP_K):
            row_copy(cur, t, k, dest_ref[t * TOP_K + k]).wait()
        return c
    lax.fori_loop(0, tm, drain, 0)

    base = cur * half_rows
    gates = gate_ref[...]
    gate_cols = [jnp.broadcast_to(gates[:, k:k + 1], (tm, LANES)) for k in range(TOP_K)]
    pieces = []
    for s in range(n_sub):
        acc = None
        for k in range(TOP_K):
            term = buf_ref[pl.ds(base + k * tm * n_sub + s, tm, stride=n_sub), :] * gate_cols[k]
            acc = term if acc is None else acc + term
        pieces.append(acc)
    routed = jnp.concatenate(pieces, axis=1)
    y = pre_ref[0] + g2_ref[0] * routed
    o_ref[0] = _layer_norm(y, ln_g_ref[...], ln_b_ref[...])


def _combine(ys, dest, gates_t, pre, gate2, ln_g, ln_b):
    bsz, seq, d = pre.shape
    tm = min(DISPATCH_ROWS, seq)
    per_seq = seq // tm
    n_steps = bsz * per_seq
    row = lambda i, *_: (i // per_seq, i % per_seq, 0)
    per_b = lambda i, *_: (i // per_seq, 0, 0)
    full = lambda *shape: pl.BlockSpec(shape, lambda i, *_: (0,) * len(shape))
    grid_spec = pltpu.PrefetchScalarGridSpec(
        num_scalar_prefetch=0,
        grid=(n_steps,),
        in_specs=[pl.BlockSpec((TOP_K * tm,), lambda i: (i,), memory_space=pltpu.SMEM),
                  pl.BlockSpec((TOP_K * tm,), lambda i: (jnp.minimum(i + 1, n_steps - 1),),
                               memory_space=pltpu.SMEM),
                  pl.BlockSpec(memory_space=pl.ANY),
                  pl.BlockSpec((tm, TOP_K), lambda i: (i, 0)),
                  pl.BlockSpec((1, tm, d), row), pl.BlockSpec((1, 1, d), per_b), full(1, d), full(1, d)],
        out_specs=pl.BlockSpec((1, tm, d), row),
        scratch_shapes=[pltpu.VMEM((2 * TOP_K * tm * (d // LANES), LANES), F32),
                        pltpu.SemaphoreType.DMA((2,))],
    )
    return pl.pallas_call(
        _combine_kernel,
        grid_spec=grid_spec,
        out_shape=jax.ShapeDtypeStruct((bsz, seq, d), F32),
        compiler_params=_cparams(("arbitrary",)),
        name="moe_combine_ln2",
    )(dest, dest, ys, gates_t, pre, gate2.reshape(bsz, 1, d), ln_g.reshape(1, d), ln_b.reshape(1, d))


def _dest_kernel(pstart_ref, idx_ref, rank_ref, dest_ref):
    idx = idx_ref[...]

    def body(e, acc):
        return acc + jnp.where(idx == e, pstart_ref[e], 0)
    dest_ref[...] = lax.fori_loop(0, pstart_ref.shape[0], body, rank_ref[...], unroll=8)


def _dest_rows(pstart, idx, rank):
    n_k, n_tok = idx.shape
    tn = min(4096, n_tok)
    grid_spec = pltpu.PrefetchScalarGridSpec(
        num_scalar_prefetch=1,
        grid=(n_tok // tn,),
        in_specs=[pl.BlockSpec((n_k, tn), lambda i, *_: (0, i))] * 2,
        out_specs=pl.BlockSpec((n_k, tn), lambda i, *_: (0, i)),
    )
    return pl.pallas_call(
        _dest_kernel,
        grid_spec=grid_spec,
        out_shape=jax.ShapeDtypeStruct((n_k, n_tok), jnp.int32),
        compiler_params=_cparams(("arbitrary",)),
        name="moe_dest",
    )(pstart, idx, rank)


def _moe_plan(idx, rank, cnt, n_tok):
    n_e = cnt.shape[0]
    padded = (cnt + MOE_ROWS - 1) // MOE_ROWS * MOE_ROWS
    pend = jnp.cumsum(padded).astype(jnp.int32)
    pstart = pend - padded
    dest_kt = _dest_rows(pstart, idx, rank)
    dest = dest_kt.T.reshape(-1)
    win = min(DISPATCH_WINDOW, n_tok)
    dest_windows = dest_kt.reshape(TOP_K, n_tok // win, win).transpose(1, 0, 2).reshape(n_tok // win, TOP_K * win)
    n_blk = n_tok * TOP_K // MOE_ROWS + n_e
    n_used = (pend[-1:] // MOE_ROWS).astype(jnp.int32)
    plan = ((pstart // MOE_ROWS).astype(jnp.int32), (padded // MOE_ROWS).astype(jnp.int32),
            cnt.astype(jnp.int32), n_used)
    return dest, dest_windows, plan, n_blk * MOE_ROWS


def kernel(x, c, rel_bias, w_ada, b_ada, w_in, ssm_a_re, ssm_a_im, ssm_b_re, ssm_b_im, ssm_c_re, ssm_c_im, ssm_d, ssm_log_dt, w_glu, b_glu, g_att, g_ssm, w_out, ln1_g, ln1_b, w_router, router_bias, w_e_gate, w_e_up, w_e_down, w_s_gate, w_s_up, w_s_down, ln2_g, ln2_b):
    bsz, seq, d = x.shape
    depth = w_ada.shape[0]
    alpha = (2 * depth) ** 0.25
    att_w = g_att.shape[-1]
    tables = _att_tables(rel_bias)
    for layer in range(depth):
        ada = _ada(c, w_ada[layer], b_ada[layer])
        shift1, scale1, gate1, shift2, scale2, gate2 = jnp.split(ada, 6, axis=-1)
        q, k, v, u = _inproj(x, scale1, shift1, w_in[layer], att_w)
        att = _attention(q, k, v, tables)
        ssm_prm = _ssm_params(ssm_a_re[layer], ssm_a_im[layer], ssm_b_re[layer], ssm_b_im[layer],
                              ssm_c_re[layer], ssm_c_im[layer], ssm_d[layer], ssm_log_dt[layer])
        ssm = _ssm(u, *ssm_prm, w_glu[layer], b_glu[layer])
        h2, h2_tiles, pre = _mix(x, att, ssm, g_att[layer], g_ssm[layer], w_out[layer], gate1,
                                 ln1_g[layer], ln1_b[layer], scale2, shift2, gate2, w_s_gate[layer],
                                 w_s_up[layer], w_s_down[layer], alpha)
        h2d = h2.reshape(bsz * seq, d)
        idx, gate, rank, cnt = _route(h2d, w_router[layer], router_bias[layer])
        dest, dest_windows, plan, n_rows = _moe_plan(idx, rank, cnt, bsz * seq)
        xs = _dispatch(h2_tiles.reshape(bsz * seq, d // LANES, LANES), dest_windows, n_rows)
        ys = _experts(xs, plan, w_e_gate[layer], w_e_up[layer], w_e_down[layer])
        x = _combine(ys, dest, gate.T, pre, gate2, ln2_g[layer], ln2_b[layer])
    return x
```

```python
import functools
import math

import jax
import jax.numpy as jnp
import numpy as np
from jax import lax
from jax.experimental import pallas as pl
from jax.experimental.pallas import tpu as pltpu
from jax.experimental.pallas import tpu_sc as plsc

F32 = jnp.float32
BF16 = jnp.bfloat16

HEAD_DIM = 64
ATT_BLOCK = 128
PATTERNS = ((128, 1), (512, 4), (2048, 16))
N_BUCKETS = 32
MAX_DISTANCE = 2048
SSM_GROUP = 16
SSM_STATE = 64
N_EXPERTS = 256
TOP_K = 8
N_EXPERT_GROUPS = 8
TOPK_GROUPS = 4
ROUTED_SCALE = 2.5
EPS = 1e-5
NEG_INF = -1e30
LOG2_E = math.log2(math.e)

LANES = 128
SUBLANES = 8
VMEM_LIMIT_BYTES = 56 * 1024 * 1024

PROJ_ROWS = 512
SSM_ROWS = 256
MIX_ROWS = 512
ROUTE_COLS = 512
MOE_ROWS = 256
DISPATCH_ROWS = 256
DISPATCH_WINDOW = 32
ATT_UNITS_PER_STEP = 8
DMA_ISSUE_UNROLL = 4
EXPERT_SLOTS = 4


def _cparams(sem, vmem=VMEM_LIMIT_BYTES):
    return pltpu.CompilerParams(dimension_semantics=sem, vmem_limit_bytes=vmem)


def _dot(a, b):
    return jnp.dot(a, b, preferred_element_type=F32)


def _dot_nt(a, b):
    return lax.dot_general(a, b, (((1,), (1,)), ((), ())), preferred_element_type=F32)


def _silu(x):
    return x * jax.nn.sigmoid(x)


def _store_token_tiles(ref, x):
    m, d = x.shape
    n_sub = d // LANES
    for s in range(n_sub):
        ref[pl.ds(s, m, stride=n_sub), :] = x[:, s * LANES:(s + 1) * LANES]


def _load_token_tiles(ref, m, d, row0=0):
    n_sub = d // LANES
    return [ref[pl.ds(row0 + s, m, stride=n_sub), :] for s in range(n_sub)]


def _split_bf16(x):
    hi = x.astype(BF16)
    lo = (x - hi.astype(F32)).astype(BF16)
    return hi, lo


def _ada_kernel(c_ref, w_ref, b_ref, o_ref):
    c = c_ref[...]
    a_hi, a_lo = _split_bf16(_silu(c))
    w_hi, w_lo = _split_bf16(w_ref[...])
    acc = _dot(a_hi, w_hi) + _dot(a_hi, w_lo) + _dot(a_lo, w_hi)
    o_ref[...] = acc + b_ref[...]


def _ada(c, w_ada, b_ada):
    bsz, d = c.shape
    n = w_ada.shape[1]
    rows = SUBLANES
    c_pad = jnp.zeros((rows, d), F32).at[:bsz].set(c)
    tn = 1024
    out = pl.pallas_call(
        _ada_kernel,
        grid=(n // tn,),
        in_specs=[pl.BlockSpec((rows, d), lambda j: (0, 0)),
                  pl.BlockSpec((d, tn), lambda j: (0, j)),
                  pl.BlockSpec((1, tn), lambda j: (0, j))],
        out_specs=pl.BlockSpec((rows, tn), lambda j: (0, j)),
        out_shape=jax.ShapeDtypeStruct((rows, n), F32),
        compiler_params=_cparams(("parallel",)),
        name="ada",
    )(c_pad, w_ada, b_ada.reshape(1, n))
    return out[:bsz]


def _inproj_kernel(x_ref, sc_ref, sh_ref, w_ref, q_ref, k_ref, v_ref, u_ref, *, att_w, q_scale):
    h = (x_ref[0] * (1.0 + sc_ref[0]) + sh_ref[0]).astype(BF16)
    q_ref[0] = _dot(h, w_ref[:, 0:att_w]) * q_scale
    k_ref[0] = _dot(h, w_ref[:, att_w:2 * att_w])
    v_ref[0] = _dot(h, w_ref[:, 2 * att_w:3 * att_w])
    u_ref[0] = _dot(h, w_ref[:, 3 * att_w:])


def _inproj(x, scale, shift, w_in, att_w):
    bsz, seq, d = x.shape
    n = w_in.shape[1]
    ssm_w = n - 3 * att_w
    ts = min(PROJ_ROWS, seq)
    row = lambda b, i: (b, i, 0)
    per_b = lambda b, i: (b, 0, 0)
    kern = functools.partial(_inproj_kernel, att_w=att_w, q_scale=HEAD_DIM ** -0.5 * LOG2_E)
    return pl.pallas_call(
        kern,
        grid=(bsz, seq // ts),
        in_specs=[pl.BlockSpec((1, ts, d), row),
                  pl.BlockSpec((1, 1, d), per_b),
                  pl.BlockSpec((1, 1, d), per_b),
                  pl.BlockSpec((d, n), lambda b, i: (0, 0))],
        out_specs=[pl.BlockSpec((1, ts, att_w), row)] * 3 + [pl.BlockSpec((1, ts, ssm_w), row)],
        out_shape=[jax.ShapeDtypeStruct((bsz, seq, att_w), F32)] * 3
                  + [jax.ShapeDtypeStruct((bsz, seq, ssm_w), F32)],
        compiler_params=_cparams(("parallel", "parallel")),
        name="inproj",
    )(x, scale.reshape(bsz, 1, d), shift.reshape(bsz, 1, d), w_in.astype(BF16))


def _t5_bucket_np(dist):
    exact = N_BUCKETS // 2
    large = exact + (np.log(np.maximum(dist, 1).astype(np.float64) / exact)
                     / math.log(MAX_DISTANCE / exact) * (N_BUCKETS - exact)).astype(np.int64)
    return np.where(dist < exact, dist, np.minimum(large, N_BUCKETS - 1))


def _att_tables(rel_bias):
    qi = np.arange(ATT_BLOCK)[:, None]
    ki = np.arange(2 * ATT_BLOCK)[None, :]
    rel = qi + ATT_BLOCK - ki
    tabs = []
    for window, dil in PATTERNS:
        band = (rel >= 0) & (rel <= window // dil)
        bucket = _t5_bucket_np(np.maximum(rel, 0) * dil)
        onehot = (bucket[:, :, None] == np.arange(N_BUCKETS)[None, None, :]).astype(np.float32)
        bias = jnp.einsum('qkb,bh->hqk', onehot, rel_bias.astype(F32),
                          precision=lax.Precision.HIGHEST)
        full = jnp.where(band[None], bias * LOG2_E, NEG_INF)
        first = jnp.concatenate([full[:, :, ATT_BLOCK:], jnp.full_like(full[:, :, ATT_BLOCK:], NEG_INF)], axis=-1)
        tabs.append(jnp.stack([full, first]))
    return jnp.stack(tabs)


def _att_kernel(q_ref, k_ref, v_ref, tab_ref, o_ref, m_ref, l_ref, *, seq):
    lane = lax.broadcasted_iota(jnp.int32, (ATT_BLOCK, LANES), 1)
    head0 = lane < HEAD_DIM

    def rows(ref, start, n, dil):
        if dil == 1:
            return ref[0, pl.ds(pl.multiple_of(start, ATT_BLOCK), n), :]
        return ref[0, pl.ds(start, n, stride=dil), :]

    def lane_mask(n_rows, h):
        m = lax.broadcasted_iota(jnp.int32, (n_rows, LANES), 1) < HEAD_DIM
        return m if h == 0 else ~m

    def batch(pi, dil, nb, res, blk0, head, init):
        q_rows = nb * ATT_BLOCK
        kv_blocks = nb if head else nb + 1
        q_start = res + dil * ATT_BLOCK * blk0
        kv_start = q_start if head else q_start - dil * ATT_BLOCK
        if dil == 1:
            sl = pl.ds(pl.multiple_of(q_start, ATT_BLOCK), q_rows)
        else:
            sl = pl.ds(q_start, q_rows, stride=dil)
        q = rows(q_ref, q_start, q_rows, dil)
        k = rows(k_ref, kv_start, kv_blocks * ATT_BLOCK, dil).astype(BF16)
        v = rows(v_ref, kv_start, kv_blocks * ATT_BLOCK, dil)
        if not init:
            m_old, l_old, o_old = m_ref[sl, :], l_ref[sl, :], o_ref[0, sl, :]
        qh = [jnp.where(lane_mask(q_rows, h), q, 0.0).astype(BF16) for h in range(2)]
        vh = [jnp.where(lane_mask(kv_blocks * ATT_BLOCK, h), v, 1.0).astype(BF16) for h in range(2)]
        o_parts, l_parts, m_parts = [], [], []
        for j in range(nb):
            first = 1 if (head and j == 0) else 0
            kb = j if (not head or j == 0) else j - 1
            ksl = slice(kb * ATT_BLOCK, (kb + 2) * ATT_BLOCK)
            qsl = slice(j * ATT_BLOCK, (j + 1) * ATT_BLOCK)
            outs, ms = [], []
            for h in range(2):
                s = _dot_nt(qh[h][qsl], k[ksl]) + tab_ref[pi, first, h]
                m = jnp.max(s, axis=1, keepdims=True)
                p = jnp.exp2(s - m).astype(BF16)
                outs.append(_dot(p, vh[h][ksl]))
                ms.append(m)
            o_parts.append(jnp.where(head0, outs[0], outs[1]))
            l_parts.append(pltpu.roll(jnp.where(head0, outs[1], outs[0]), HEAD_DIM, axis=1))
            m_parts.append(jnp.where(head0, ms[0], ms[1]))
        o = jnp.concatenate(o_parts, axis=0)
        l = jnp.concatenate(l_parts, axis=0)
        m = jnp.concatenate(m_parts, axis=0)
        if not init:
            m_new = jnp.maximum(m_old, m)
            a_old = jnp.exp2(m_old - m_new)
            a_cur = jnp.exp2(m - m_new)
            o = o_old * a_old + o * a_cur
            l = l_old * a_old + l * a_cur
            m = m_new
        o_ref[0, sl, :] = o
        m_ref[sl, :] = m
        l_ref[sl, :] = l

    n_units = seq // ATT_BLOCK
    order = sorted(range(len(PATTERNS)), key=lambda p: -PATTERNS[p][1])
    for pos, pi in enumerate(order):
        dil = PATTERNS[pi][1]
        log_d = dil.bit_length() - 1
        res_blocks = seq // (dil * ATT_BLOCK)
        nb = min(ATT_UNITS_PER_STEP, res_blocks)
        per_res = res_blocks // nb
        init = pos == 0

        def head_body(res, carry, pi=pi, dil=dil, nb=nb, init=init):
            batch(pi, dil, nb, res, 0, True, init)
            return carry
        lax.fori_loop(0, dil, head_body, 0)

        if per_res > 1:
            def tail_body(i, carry, pi=pi, dil=dil, nb=nb, log_d=log_d, init=init):
                res = i & (dil - 1)
                blk0 = nb * (1 + (i >> log_d))
                batch(pi, dil, nb, res, blk0, False, init)
                return carry
            lax.fori_loop(0, dil * (per_res - 1), tail_body, 0)

    def finish(i, carry):
        sl = pl.ds(pl.multiple_of(i * ATT_BLOCK, ATT_BLOCK), ATT_BLOCK)
        o_ref[0, sl, :] = o_ref[0, sl, :] / l_ref[sl, :]
        return carry
    lax.fori_loop(0, n_units, finish, 0)


def _attention(q, k, v, tables):
    bsz, seq, att_w = q.shape
    n_pairs = att_w // LANES
    blk = lambda b, p: (b, 0, p)
    qkv_spec = pl.BlockSpec((1, seq, LANES), blk)
    n_pat = tables.shape[0]
    tab_spec = pl.BlockSpec((n_pat, 2, 2, ATT_BLOCK, 2 * ATT_BLOCK), lambda b, p: (0, 0, p, 0, 0))
    return pl.pallas_call(
        functools.partial(_att_kernel, seq=seq),
        grid=(bsz, n_pairs),
        in_specs=[qkv_spec, qkv_spec, qkv_spec, tab_spec],
        out_specs=pl.BlockSpec((1, seq, LANES), blk),
        out_shape=jax.ShapeDtypeStruct((bsz, seq, att_w), F32),
        scratch_shapes=[pltpu.VMEM((seq, LANES), F32), pltpu.VMEM((seq, LANES), F32)],
        compiler_params=_cparams(("parallel", "parallel")),
        name="dilated_attention",
    )(q, k, v, tables)


def _ssm_params(a_re, a_im, b_re, b_im, c_re, c_im, d_skip, log_dt):
    n_g, n_p = a_re.shape
    n_c = b_re.shape[-1]
    n_state = n_g * n_p
    dt = jnp.exp(log_dt.astype(F32))[:, None]

    def a_pow(kk):
        mag = jnp.exp(kk * dt * a_re)
        ph = kk * dt * a_im
        return mag * jnp.cos(ph), mag * jnp.sin(ph)

    ab_re, ab_im = a_pow(1.0)
    nr, ni = ab_re - 1.0, ab_im
    den = a_re * a_re + a_im * a_im
    f_re = (nr * a_re + ni * a_im) / den
    f_im = (ni * a_re - nr * a_im) / den
    bb_re = f_re[:, :, None] * b_re - f_im[:, :, None] * b_im
    bb_im = f_re[:, :, None] * b_im + f_im[:, :, None] * b_re
    eye = jnp.eye(n_g, dtype=F32)
    w_re = jnp.einsum('gpc,gh->gchp', bb_re, eye).reshape(n_g * n_c, n_state)
    w_im = jnp.einsum('gpc,gh->gchp', bb_im, eye).reshape(n_g * n_c, n_state)
    o_re = jnp.einsum('gcp,gh->gphc', c_re, eye).reshape(n_state, n_g * n_c)
    o_im = jnp.einsum('gcp,gh->gphc', c_im, eye).reshape(n_state, n_g * n_c)
    n_kb = (n_g * n_c) // LANES
    sp = n_state // n_kb
    w_in = jnp.stack([jnp.concatenate([w_re[kb * LANES:(kb + 1) * LANES, kb * sp:(kb + 1) * sp],
                                       w_im[kb * LANES:(kb + 1) * LANES, kb * sp:(kb + 1) * sp]], axis=1)
                      for kb in range(n_kb)])
    w_out = jnp.stack([jnp.concatenate([o_re[kb * sp:(kb + 1) * sp, kb * LANES:(kb + 1) * LANES],
                                        -o_im[kb * sp:(kb + 1) * sp, kb * LANES:(kb + 1) * LANES]], axis=0)
                       for kb in range(n_kb)])
    j = jnp.arange(SUBLANES, dtype=F32)[:, None]
    flat = lambda t: jnp.broadcast_to(t.reshape(1, n_state), (SUBLANES, n_state))
    coef = []
    for sh in (1, 2, 4):
        pr, pim = a_pow(float(sh))
        keep = (j >= sh).astype(F32)
        coef += [flat(pr) * keep, flat(pim) * keep]
    a_re_f = jnp.broadcast_to(a_re.reshape(1, n_state), (SUBLANES, n_state))
    a_im_f = jnp.broadcast_to(a_im.reshape(1, n_state), (SUBLANES, n_state))
    dt_f = jnp.broadcast_to(jnp.repeat(dt[:, 0], n_p).reshape(1, n_state), (SUBLANES, n_state))
    mag = jnp.exp((j + 1.0) * dt_f * a_re_f)
    ph = (j + 1.0) * dt_f * a_im_f
    coef += [mag * jnp.cos(ph), mag * jnp.sin(ph)]
    coef = jnp.stack(coef)
    return w_in.astype(BF16), w_out.astype(BF16), coef, d_skip.reshape(1, n_g * n_c).astype(F32)


def _gelu_tanh(x):
    return 0.5 * x * (1.0 + jnp.tanh(math.sqrt(2.0 / math.pi) * (x + 0.044715 * (x * x * x))))


def _ssm_kernel(u_ref, win_ref, wout_ref, coef_ref, d_ref, wglu_ref, bglu_ref, o_ref,
                hre_ref, him_ref, cre_ref, cim_ref, *, rows, n_state, col_w):
    n_kb = win_ref.shape[0]
    sp = n_state // n_kb

    @pl.when(pl.program_id(1) == 0)
    def _():
        cre_ref[...] = jnp.zeros_like(cre_ref)
        cim_ref[...] = jnp.zeros_like(cim_ref)

    u = u_ref[0]
    ub = u.astype(BF16)
    for kb in range(n_kb):
        bu = _dot(ub[:, kb * LANES:(kb + 1) * LANES], win_ref[kb])
        hre_ref[:, kb * sp:(kb + 1) * sp] = bu[:, :sp]
        him_ref[:, kb * sp:(kb + 1) * sp] = bu[:, sp:]

    n_slab = rows // SUBLANES
    for c0 in range(0, n_state, col_w):
        cs = slice(c0, c0 + col_w)

        def body(i, carry, cs=cs):
            c_re, c_im = carry
            sl = pl.ds(pl.multiple_of(i * SUBLANES, SUBLANES), SUBLANES)
            x_re = hre_ref[sl, cs]
            x_im = him_ref[sl, cs]
            for si, sh in enumerate((1, 2, 4)):
                p_re = coef_ref[2 * si, :, cs]
                p_im = coef_ref[2 * si + 1, :, cs]
                s_re = pltpu.roll(x_re, sh, axis=0)
                s_im = pltpu.roll(x_im, sh, axis=0)
                x_re, x_im = (x_re + p_re * s_re - p_im * s_im,
                              x_im + p_re * s_im + p_im * s_re)
            p_re = coef_ref[6, :, cs]
            p_im = coef_ref[7, :, cs]
            h_re = x_re + p_re * c_re - p_im * c_im
            h_im = x_im + p_re * c_im + p_im * c_re
            hre_ref[sl, cs] = h_re
            him_ref[sl, cs] = h_im
            last = slice(SUBLANES - 1, SUBLANES)
            return (jnp.broadcast_to(h_re[last, :], h_re.shape),
                    jnp.broadcast_to(h_im[last, :], h_im.shape))

        c_re, c_im = lax.fori_loop(0, n_slab, body, (cre_ref[:, cs], cim_ref[:, cs]), unroll=True)
        cre_ref[:, cs] = c_re
        cim_ref[:, cs] = c_im

    ys = []
    for kb in range(n_kb):
        hcat = jnp.concatenate([hre_ref[:, kb * sp:(kb + 1) * sp].astype(BF16),
                                him_ref[:, kb * sp:(kb + 1) * sp].astype(BF16)], axis=1)
        ys.append(_dot(hcat, wout_ref[kb]))
    y = jnp.concatenate(ys, axis=1) + d_ref[...] * u
    y = _gelu_tanh(y)
    z = _dot(y.astype(BF16), wglu_ref[...]) + bglu_ref[...]
    o_ref[0] = y * jax.nn.sigmoid(z)


def _ssm(u, w_in, w_out, coef, d_flat, w_glu, b_glu):
    bsz, seq, ssm_w = u.shape
    n_state = coef.shape[-1]
    rows = min(SSM_ROWS, seq)
    full = lambda *shape: pl.BlockSpec(shape, lambda b, i: (0,) * len(shape))
    kern = functools.partial(_ssm_kernel, rows=rows, n_state=n_state, col_w=4 * LANES)
    return pl.pallas_call(
        kern,
        grid=(bsz, seq // rows),
        in_specs=[pl.BlockSpec((1, rows, ssm_w), lambda b, i: (b, i, 0)),
                  full(*w_in.shape), full(*w_out.shape), full(*coef.shape), full(1, ssm_w),
                  full(ssm_w, ssm_w), full(1, ssm_w)],
        out_specs=pl.BlockSpec((1, rows, ssm_w), lambda b, i: (b, i, 0)),
        out_shape=jax.ShapeDtypeStruct((bsz, seq, ssm_w), F32),
        scratch_shapes=[pltpu.VMEM((rows, n_state), F32), pltpu.VMEM((rows, n_state), F32),
                        pltpu.VMEM((SUBLANES, n_state), F32), pltpu.VMEM((SUBLANES, n_state), F32)],
        compiler_params=_cparams(("parallel", "arbitrary")),
        name="s5_glu",
    )(u, w_in, w_out, coef, d_flat, w_glu.astype(BF16), b_glu.reshape(1, ssm_w))


def _layer_norm(y, g, b):
    mu = jnp.mean(y, axis=-1, keepdims=True)
    yc = y - mu
    var = jnp.mean(yc * yc, axis=-1, keepdims=True)
    return yc * lax.rsqrt(var + EPS) * g + b


def _rms_norm(y, g):
    return y * lax.rsqrt(jnp.mean(y * y, axis=-1, keepdims=True) + EPS) * g


def _mix_kernel(x_ref, att_ref, ssm_ref, gatt_ref, gssm_ref, wout_ref, g1_ref, ln_g_ref, ln_b_ref,
                sc_ref, sh_ref, g2_ref, wsg_ref, wsu_ref, wsd_ref, h_ref, ht_ref, pre_ref, *, alpha, att_w):
    a_n = _rms_norm(att_ref[0], gatt_ref[...]).astype(BF16)
    s_n = _rms_norm(ssm_ref[0], gssm_ref[...]).astype(BF16)
    mix = _dot(a_n, wout_ref[0:att_w, :]) + _dot(s_n, wout_ref[att_w:, :])
    x1 = _layer_norm(alpha * x_ref[0] + g1_ref[0] * mix, ln_g_ref[...], ln_b_ref[...])
    h = x1 * (1.0 + sc_ref[0]) + sh_ref[0]
    h_ref[0] = h
    _store_token_tiles(ht_ref, h)
    hb = h.astype(BF16)
    hid = _silu(_dot(hb, wsg_ref[...])) * _dot(hb, wsu_ref[...])
    shared = _dot(hid.astype(BF16), wsd_ref[...])
    pre_ref[0] = alpha * x1 + g2_ref[0] * shared


def _mix(x, att, ssm, g_att, g_ssm, w_out, gate1, ln_g, ln_b, scale2, shift2, gate2,
         w_s_gate, w_s_up, w_s_down, alpha):
    bsz, seq, d = x.shape
    att_w = att.shape[-1]
    ssm_w = ssm.shape[-1]
    ff = w_s_gate.shape[1]
    tm = min(MIX_ROWS, seq)
    n_sub = d // LANES
    row = lambda b, i: (b, i, 0)
    per_b = lambda b, i: (b, 0, 0)
    full = lambda *shape: pl.BlockSpec(shape, lambda b, i: (0,) * len(shape))
    vec = lambda t: t.reshape(bsz, 1, d)
    return pl.pallas_call(
        functools.partial(_mix_kernel, alpha=alpha, att_w=att_w),
        grid=(bsz, seq // tm),
        in_specs=[pl.BlockSpec((1, tm, d), row), pl.BlockSpec((1, tm, att_w), row),
                  pl.BlockSpec((1, tm, ssm_w), row),
                  full(1, att_w), full(1, ssm_w), full(att_w + ssm_w, d),
                  pl.BlockSpec((1, 1, d), per_b), full(1, d), full(1, d),
                  pl.BlockSpec((1, 1, d), per_b), pl.BlockSpec((1, 1, d), per_b),
                  pl.BlockSpec((1, 1, d), per_b),
                  full(d, ff), full(d, ff), full(ff, d)],
        out_specs=[pl.BlockSpec((1, tm, d), row),
                   pl.BlockSpec((tm * n_sub, LANES), lambda b, i: (b * (seq // tm) + i, 0)),
                   pl.BlockSpec((1, tm, d), row)],
        out_shape=[jax.ShapeDtypeStruct((bsz, seq, d), F32),
                   jax.ShapeDtypeStruct((bsz * seq * n_sub, LANES), F32),
                   jax.ShapeDtypeStruct((bsz, seq, d), F32)],
        compiler_params=_cparams(("parallel", "parallel")),
        name="mix_ln1_shared",
    )(x, att, ssm, g_att.reshape(1, att_w), g_ssm.reshape(1, ssm_w), w_out.astype(BF16),
      vec(gate1), ln_g.reshape(1, d), ln_b.reshape(1, d), vec(scale2), vec(shift2), vec(gate2),
      w_s_gate.astype(BF16), w_s_up.astype(BF16), w_s_down.astype(BF16))


def _route_kernel(h_ref, wrt_ref, wrt_lo_ref, bias_ref, tri_ref, idx_ref, gate_ref, rank_ref, cnt_ref, carry_ref):
    n_e = wrt_ref.shape[0]
    tm = h_ref.shape[0]
    per_group = n_e // N_EXPERT_GROUPS
    neg = -jnp.inf

    @pl.when(pl.program_id(0) == 0)
    def _():
        carry_ref[...] = jnp.zeros_like(carry_ref)

    h_hi, h_lo = _split_bf16(h_ref[...])
    logits = (_dot_nt(wrt_ref[...], h_hi) + _dot_nt(wrt_ref[...], h_lo)
              + _dot_nt(wrt_lo_ref[...], h_hi))
    scores = jax.nn.sigmoid(logits)
    biased = scores + bias_ref[...]

    g3 = biased.reshape(N_EXPERT_GROUPS, per_group, tm)
    ridx = lax.broadcasted_iota(jnp.int32, g3.shape, 1).astype(F32)
    m1 = jnp.max(g3, axis=1, keepdims=True)
    first = jnp.min(jnp.where(g3 == m1, ridx, float(per_group)), axis=1, keepdims=True)
    m2 = jnp.max(jnp.where(ridx == first, neg, g3), axis=1, keepdims=True)
    gs = m1 + m2

    gidx = lax.broadcasted_iota(jnp.int32, gs.shape, 0).astype(F32)
    ok = jnp.zeros_like(gs)
    cur = gs
    for _ in range(TOPK_GROUPS):
        mx = jnp.max(cur, axis=0, keepdims=True)
        fi = jnp.min(jnp.where(cur == mx, gidx, float(N_EXPERT_GROUPS)), axis=0, keepdims=True)
        hit = gidx == fi
        ok = jnp.where(hit, 1.0, ok)
        cur = jnp.where(hit, neg, cur)
    masked = jnp.where(ok > 0.5, g3, neg).reshape(n_e, tm)

    eidx = lax.broadcasted_iota(jnp.int32, (n_e, tm), 0).astype(F32)
    onehot = jnp.zeros((n_e, tm), F32)
    cur = masked
    sel_idx = []
    sel_gate = []
    for _ in range(TOP_K):
        mx = jnp.max(cur, axis=0, keepdims=True)
        fi = jnp.min(jnp.where(cur == mx, eidx, float(n_e)), axis=0, keepdims=True)
        hit = eidx == fi
        sel_idx.append(fi)
        sel_gate.append(jnp.sum(jnp.where(hit, scores, 0.0), axis=0, keepdims=True))
        onehot = jnp.where(hit, 1.0, onehot)
        cur = jnp.where(hit, neg, cur)
    idx = jnp.concatenate(sel_idx, axis=0)
    gate = jnp.concatenate(sel_gate, axis=0)
    gate = gate / jnp.sum(gate, axis=0, keepdims=True) * ROUTED_SCALE

    prior = _dot(onehot.astype(BF16), tri_ref[...]) + carry_ref[:, 0:1]
    ranks = [jnp.sum(jnp.where(eidx == sel_idx[k], prior, 0.0), axis=0, keepdims=True)
             for k in range(TOP_K)]
    rank = jnp.concatenate(ranks, axis=0)
    carry = carry_ref[...] + jnp.sum(onehot, axis=1, keepdims=True)
    carry_ref[...] = carry
    cnt_ref[...] = carry

    idx_ref[...] = idx.astype(jnp.int32)
    gate_ref[...] = gate
    rank_ref[...] = rank.astype(jnp.int32)


def _route(h2d, w_router, router_bias):
    n_tok, d = h2d.shape
    n_e = w_router.shape[1]
    tm = min(ROUTE_COLS, n_tok)
    tri = (np.arange(tm)[:, None] < np.arange(tm)[None, :]).astype(np.float32)
    w_hi, w_lo = _split_bf16(w_router.T.astype(F32))
    col = lambda i: (0, i)
    full = lambda *shape: pl.BlockSpec(shape, lambda i: (0,) * len(shape))
    idx, gate, rank, cnt = pl.pallas_call(
        _route_kernel,
        grid=(n_tok // tm,),
        in_specs=[pl.BlockSpec((tm, d), lambda i: (i, 0)), full(n_e, d), full(n_e, d), full(n_e, 1),
                  full(tm, tm)],
        out_specs=[pl.BlockSpec((TOP_K, tm), col), pl.BlockSpec((TOP_K, tm), col),
                   pl.BlockSpec((TOP_K, tm), col), full(n_e, LANES)],
        out_shape=[jax.ShapeDtypeStruct((TOP_K, n_tok), jnp.int32),
                   jax.ShapeDtypeStruct((TOP_K, n_tok), F32),
                   jax.ShapeDtypeStruct((TOP_K, n_tok), jnp.int32),
                   jax.ShapeDtypeStruct((n_e, LANES), F32)],
        scratch_shapes=[pltpu.VMEM((n_e, LANES), F32)],
        compiler_params=_cparams(("arbitrary",)),
        name="router_topk",
    )(h2d, w_hi, w_lo, router_bias.reshape(n_e, 1).astype(F32), jnp.asarray(tri, BF16))
    return idx, gate, rank, cnt[:, 0].astype(jnp.int32)


def _dispatch(h_tiles, dest_windows, n_rows):
    n_tok, n_sub, _ = h_tiles.shape
    win = DISPATCH_WINDOW
    mesh = plsc.VectorSubcoreMesh(core_axis_name="core", subcore_axis_name="subcore")

    @pl.kernel(out_type=jax.ShapeDtypeStruct((n_rows, n_sub, LANES), h_tiles.dtype), mesh=mesh,
               scratch_types=[])
    def scatter_tiles(x_hbm, d_hbm, o_hbm):
        def window(x_vmem, d_vmem):
            for k in range(TOP_K):
                pltpu.sync_copy(x_vmem, o_hbm.at[d_vmem.at[0, pl.ds(k * win, win)]])
        pltpu.emit_pipeline(
            window,
            grid=(n_tok // win,),
            in_specs=[pl.BlockSpec((win, n_sub, LANES), lambda i: (i, 0, 0)),
                      pl.BlockSpec((1, TOP_K * win), lambda i: (i, 0))],
            out_specs=[],
            core_axis_name=("core", "subcore"),
            dimension_semantics=(pltpu.PARALLEL,),
        )(x_hbm, d_hbm)

    return scatter_tiles(h_tiles, dest_windows)


def _expert_kernel(first_ref, nblk_ref, cnt_ref, nused_ref, xs_ref, wg_ref, wu_ref, wd_ref, ys_ref,
                   xbuf_ref, ybuf_ref, wgb_ref, wub_ref, wdb_ref, xsem_ref, ysem_ref):
    e = pl.program_id(0)
    n_used = nused_ref[0]
    d = wgb_ref.shape[0]
    blk_rows = xbuf_ref.shape[1]

    def x_copy(g, slot):
        src = xs_ref.at[pl.ds(pl.multiple_of(g * blk_rows, blk_rows), blk_rows)]
        return pltpu.make_async_copy(src, xbuf_ref.at[slot], xsem_ref.at[slot])

    def y_copy(g, slot):
        dst = ys_ref.at[pl.ds(pl.multiple_of(g * blk_rows, blk_rows), blk_rows)]
        return pltpu.make_async_copy(ybuf_ref.at[slot], dst, ysem_ref.at[slot])

    n_slot = xbuf_ref.shape[0]

    @pl.when(e == 0)
    def _():
        for g in range(n_slot - 1):
            @pl.when(g < n_used)
            def _(g=g):
                x_copy(g, g).start()

    @pl.when(nblk_ref[e] > 0)
    def _():
        wgb_ref[...] = wg_ref[0].astype(BF16)
        wub_ref[...] = wu_ref[0].astype(BF16)
        wdb_ref[...] = wd_ref[0].astype(BF16)

    def block(j, carry):
        g = first_ref[e] + j
        slot = g & (n_slot - 1)
        x_copy(g, slot).wait()

        @pl.when(g + n_slot - 1 < n_used)
        def _():
            x_copy(g + n_slot - 1, (g + n_slot - 1) & (n_slot - 1)).start()

        @pl.when(g >= n_slot)
        def _():
            y_copy(g - n_slot, slot).wait()

        x = jnp.concatenate(_load_token_tiles(xbuf_ref.at[slot], MOE_ROWS, d), axis=1)
        row = lax.broadcasted_iota(jnp.int32, (MOE_ROWS, 1), 0)
        xb = jnp.where(row < cnt_ref[e] - j * MOE_ROWS, x, 0.0).astype(BF16)
        hid = _silu(_dot(xb, wgb_ref[...])) * _dot(xb, wub_ref[...])
        _store_token_tiles(ybuf_ref.at[slot], _dot(hid.astype(BF16), wdb_ref[...]))
        y_copy(g, slot).start()
        return carry
    lax.fori_loop(0, nblk_ref[e], block, 0)

    @pl.when(e == pl.num_programs(0) - 1)
    def _():
        for back in range(1, n_slot + 1):
            @pl.when(n_used >= back)
            def _(back=back):
                y_copy(n_used - back, (n_used - back) & (n_slot - 1)).wait()


def _experts(xs_tiles, plan, w_gate, w_up, w_down):
    n_rows, n_sub, _ = xs_tiles.shape
    d = n_sub * LANES
    n_e, _, ff = w_gate.shape
    blk_rows = MOE_ROWS * n_sub
    hbm = pl.BlockSpec(memory_space=pl.ANY)
    grid_spec = pltpu.PrefetchScalarGridSpec(
        num_scalar_prefetch=4,
        grid=(n_e,),
        in_specs=[hbm,
                  pl.BlockSpec((1, d, ff), lambda e, *_: (e, 0, 0)),
                  pl.BlockSpec((1, d, ff), lambda e, *_: (e, 0, 0)),
                  pl.BlockSpec((1, ff, d), lambda e, *_: (e, 0, 0))],
        out_specs=hbm,
        scratch_shapes=[pltpu.VMEM((EXPERT_SLOTS, blk_rows, LANES), F32),
                        pltpu.VMEM((EXPERT_SLOTS, blk_rows, LANES), F32),
                        pltpu.VMEM((d, ff), BF16), pltpu.VMEM((d, ff), BF16), pltpu.VMEM((ff, d), BF16),
                        pltpu.SemaphoreType.DMA((EXPERT_SLOTS,)), pltpu.SemaphoreType.DMA((EXPERT_SLOTS,))],
    )
    ys = pl.pallas_call(
        _expert_kernel,
        grid_spec=grid_spec,
        out_shape=jax.ShapeDtypeStruct((n_rows * n_sub, LANES), F32),
        compiler_params=_cparams(("arbitrary",)),
        name="moe_experts",
    )(*plan, xs_tiles.reshape(n_rows * n_sub, LANES), w_gate, w_up, w_down)
    return ys.reshape(n_rows, n_sub, LANES)


def _combine_kernel(dest_ref, ys_ref, gate_ref, pre_ref, g2_ref, ln_g_ref, ln_b_ref, o_ref, buf_ref, sem_ref):
    tm, d = pre_ref.shape[1], pre_ref.shape[2]
    n_sub = d // LANES

    def row_copy(t, k, src_row):
        slot = pl.multiple_of((k * tm + t) * n_sub, n_sub)
        return pltpu.make_async_copy(ys_ref.at[src_row], buf_ref.at[pl.ds(slot, n_sub)], sem_ref.at[0])

    def issue(g, c):
        t0 = g * DMA_ISSUE_UNROLL
        rows = [[dest_ref[(t0 + j) * TOP_K + k] for k in range(TOP_K)] for j in range(DMA_ISSUE_UNROLL)]
        for j in range(DMA_ISSUE_UNROLL):
            for k in range(TOP_K):
                row_copy(t0 + j, k, rows[j][k]).start(priority=k % 2)
        return c
    lax.fori_loop(0, tm // DMA_ISSUE_UNROLL, issue, 0)

    def drain(t, c):
        for k in range(TOP_K):
            row_copy(t, k, dest_ref[t * TOP_K + k]).wait()
        return c
    lax.fori_loop(0, tm, drain, 0)

    gates = gate_ref[...]
    gate_cols = [jnp.broadcast_to(gates[:, k:k + 1], (tm, LANES)) for k in range(TOP_K)]
    pieces = []
    for s in range(n_sub):
        acc = None
        for k in range(TOP_K):
            term = buf_ref[pl.ds(k * tm * n_sub + s, tm, stride=n_sub), :] * gate_cols[k]
            acc = term if acc is None else acc + term
        pieces.append(acc)
    routed = jnp.concatenate(pieces, axis=1)
    y = pre_ref[0] + g2_ref[0] * routed
    o_ref[0] = _layer_norm(y, ln_g_ref[...], ln_b_ref[...])


def _combine(ys, dest, gates_t, pre, gate2, ln_g, ln_b):
    bsz, seq, d = pre.shape
    tm = min(DISPATCH_ROWS, seq)
    per_seq = seq // tm
    row = lambda i, *_: (i // per_seq, i % per_seq, 0)
    per_b = lambda i, *_: (i // per_seq, 0, 0)
    full = lambda *shape: pl.BlockSpec(shape, lambda i, *_: (0,) * len(shape))
    grid_spec = pltpu.PrefetchScalarGridSpec(
        num_scalar_prefetch=0,
        grid=(bsz * per_seq,),
        in_specs=[pl.BlockSpec((TOP_K * tm,), lambda i: (i,), memory_space=pltpu.SMEM),
                  pl.BlockSpec(memory_space=pl.ANY),
                  pl.BlockSpec((tm, TOP_K), lambda i: (i, 0)),
                  pl.BlockSpec((1, tm, d), row), pl.BlockSpec((1, 1, d), per_b), full(1, d), full(1, d)],
        out_specs=pl.BlockSpec((1, tm, d), row),
        scratch_shapes=[pltpu.VMEM((TOP_K * tm * (d // LANES), LANES), F32), pltpu.SemaphoreType.DMA((1,))],
    )
    return pl.pallas_call(
        _combine_kernel,
        grid_spec=grid_spec,
        out_shape=jax.ShapeDtypeStruct((bsz, seq, d), F32),
        compiler_params=_cparams(("arbitrary",)),
        name="moe_combine_ln2",
    )(dest, ys, gates_t, pre, gate2.reshape(bsz, 1, d), ln_g.reshape(1, d), ln_b.reshape(1, d))


def _dest_kernel(pstart_ref, idx_ref, rank_ref, dest_ref):
    idx = idx_ref[...]

    def body(e, acc):
        return acc + jnp.where(idx == e, pstart_ref[e], 0)
    dest_ref[...] = lax.fori_loop(0, pstart_ref.shape[0], body, rank_ref[...], unroll=8)


def _dest_rows(pstart, idx, rank):
    n_k, n_tok = idx.shape
    tn = min(4096, n_tok)
    grid_spec = pltpu.PrefetchScalarGridSpec(
        num_scalar_prefetch=1,
        grid=(n_tok // tn,),
        in_specs=[pl.BlockSpec((n_k, tn), lambda i, *_: (0, i))] * 2,
        out_specs=pl.BlockSpec((n_k, tn), lambda i, *_: (0, i)),
    )
    return pl.pallas_call(
        _dest_kernel,
        grid_spec=grid_spec,
        out_shape=jax.ShapeDtypeStruct((n_k, n_tok), jnp.int32),
        compiler_params=_cparams(("arbitrary",)),
        name="moe_dest",
    )(pstart, idx, rank)


def _moe_plan(idx, rank, cnt, n_tok):
    n_e = cnt.shape[0]
    padded = (cnt + MOE_ROWS - 1) // MOE_ROWS * MOE_ROWS
    pend = jnp.cumsum(padded).astype(jnp.int32)
    pstart = pend - padded
    dest_kt = _dest_rows(pstart, idx, rank)
    dest = dest_kt.T.reshape(-1)
    win = min(DISPATCH_WINDOW, n_tok)
    dest_windows = dest_kt.reshape(TOP_K, n_tok // win, win).transpose(1, 0, 2).reshape(n_tok // win, TOP_K * win)
    n_blk = n_tok * TOP_K // MOE_ROWS + n_e
    n_used = (pend[-1:] // MOE_ROWS).astype(jnp.int32)
    plan = ((pstart // MOE_ROWS).astype(jnp.int32), (padded // MOE_ROWS).astype(jnp.int32),
            cnt.astype(jnp.int32), n_used)
    return dest, dest_windows, plan, n_blk * MOE_ROWS


def kernel(x, c, rel_bias, w_ada, b_ada, w_in, ssm_a_re, ssm_a_im, ssm_b_re, ssm_b_im, ssm_c_re, ssm_c_im, ssm_d, ssm_log_dt, w_glu, b_glu, g_att, g_ssm, w_out, ln1_g, ln1_b, w_router, router_bias, w_e_gate, w_e_up, w_e_down, w_s_gate, w_s_up, w_s_down, ln2_g, ln2_b):
    bsz, seq, d = x.shape
    depth = w_ada.shape[0]
    alpha = (2 * depth) ** 0.25
    att_w = g_att.shape[-1]
    tables = _att_tables(rel_bias)
    for layer in range(depth):
        ada = _ada(c, w_ada[layer], b_ada[layer])
        shift1, scale1, gate1, shift2, scale2, gate2 = jnp.split(ada, 6, axis=-1)
        q, k, v, u = _inproj(x, scale1, shift1, w_in[layer], att_w)
        att = _attention(q, k, v, tables)
        ssm_prm = _ssm_params(ssm_a_re[layer], ssm_a_im[layer], ssm_b_re[layer], ssm_b_im[layer],
                              ssm_c_re[layer], ssm_c_im[layer], ssm_d[layer], ssm_log_dt[layer])
        ssm = _ssm(u, *ssm_prm, w_glu[layer], b_glu[layer])
        h2, h2_tiles, pre = _mix(x, att, ssm, g_att[layer], g_ssm[layer], w_out[layer], gate1,
                                 ln1_g[layer], ln1_b[layer], scale2, shift2, gate2, w_s_gate[layer],
                                 w_s_up[layer], w_s_down[layer], alpha)
        h2d = h2.reshape(bsz * seq, d)
        idx, gate, rank, cnt = _route(h2d, w_router[layer], router_bias[layer])
        dest, dest_windows, plan, n_rows = _moe_plan(idx, rank, cnt, bsz * seq)
        xs = _dispatch(h2_tiles.reshape(bsz * seq, d // LANES, LANES), dest_windows, n_rows)
        ys = _experts(xs, plan, w_e_gate[layer], w_e_up[layer], w_e_down[layer])
        x = _combine(ys, dest, gate.T, pre, gate2, ln2_g[layer], ln2_b[layer])
    return x
```

```python
import functools
import math

import jax
import jax.numpy as jnp
import numpy as np
from jax import lax
from jax.experimental import pallas as pl
from jax.experimental.pallas import tpu as pltpu
from jax.experimental.pallas import tpu_sc as plsc

F32 = jnp.float32
BF16 = jnp.bfloat16

HEAD_DIM = 64
ATT_BLOCK = 128
PATTERNS = ((128, 1), (512, 4), (2048, 16))
N_BUCKETS = 32
MAX_DISTANCE = 2048
SSM_GROUP = 16
SSM_STATE = 64
N_EXPERTS = 256
TOP_K = 8
N_EXPERT_GROUPS = 8
TOPK_GROUPS = 4
ROUTED_SCALE = 2.5
EPS = 1e-5
NEG_INF = -1e30
LOG2_E = math.log2(math.e)

LANES = 128
SUBLANES = 8
VMEM_LIMIT_BYTES = 56 * 1024 * 1024

PROJ_ROWS = 512
SSM_ROWS = 256
MIX_ROWS = 512
ROUTE_COLS = 512
MOE_ROWS = 256
DISPATCH_ROWS = 256
COMBINE_TC_TILES = 0
DISPATCH_WINDOW = 16
ATT_UNITS_PER_STEP = 8
DMA_ISSUE_UNROLL = 4
EXPERT_SLOTS = 4


def _cparams(sem, vmem=VMEM_LIMIT_BYTES):
    return pltpu.CompilerParams(dimension_semantics=sem, vmem_limit_bytes=vmem)


def _dot(a, b):
    return jnp.dot(a, b, preferred_element_type=F32)


def _dot_nt(a, b):
    return lax.dot_general(a, b, (((1,), (1,)), ((), ())), preferred_element_type=F32)


def _silu(x):
    return x * jax.nn.sigmoid(x)


def _store_token_tiles(ref, x):
    m, d = x.shape
    n_sub = d // LANES
    for s in range(n_sub):
        ref[pl.ds(s, m, stride=n_sub), :] = x[:, s * LANES:(s + 1) * LANES]


def _load_token_tiles(ref, m, d, row0=0):
    n_sub = d // LANES
    return [ref[pl.ds(row0 + s, m, stride=n_sub), :] for s in range(n_sub)]


def _split_bf16(x):
    hi = x.astype(BF16)
    lo = (x - hi.astype(F32)).astype(BF16)
    return hi, lo


def _ada_kernel(c_ref, w_ref, b_ref, o_ref):
    c = c_ref[...]
    a_hi, a_lo = _split_bf16(_silu(c))
    w_hi, w_lo = _split_bf16(w_ref[...])
    acc = _dot(a_hi, w_hi) + _dot(a_hi, w_lo) + _dot(a_lo, w_hi)
    o_ref[...] = acc + b_ref[...]


def _ada(c, w_ada, b_ada):
    bsz, d = c.shape
    n = w_ada.shape[1]
    rows = SUBLANES
    c_pad = jnp.zeros((rows, d), F32).at[:bsz].set(c)
    tn = 1024
    out = pl.pallas_call(
        _ada_kernel,
        grid=(n // tn,),
        in_specs=[pl.BlockSpec((rows, d), lambda j: (0, 0)),
                  pl.BlockSpec((d, tn), lambda j: (0, j)),
                  pl.BlockSpec((1, tn), lambda j: (0, j))],
        out_specs=pl.BlockSpec((rows, tn), lambda j: (0, j)),
        out_shape=jax.ShapeDtypeStruct((rows, n), F32),
        compiler_params=_cparams(("parallel",)),
        name="ada",
    )(c_pad, w_ada, b_ada.reshape(1, n))
    return out[:bsz]


def _inproj_kernel(x_ref, sc_ref, sh_ref, w_ref, q_ref, k_ref, v_ref, u_ref, *, att_w, q_scale):
    h = (x_ref[0] * (1.0 + sc_ref[0]) + sh_ref[0]).astype(BF16)
    q_ref[0] = _dot(h, w_ref[:, 0:att_w]) * q_scale
    k_ref[0] = _dot(h, w_ref[:, att_w:2 * att_w])
    v_ref[0] = _dot(h, w_ref[:, 2 * att_w:3 * att_w])
    u_ref[0] = _dot(h, w_ref[:, 3 * att_w:])


def _inproj(x, scale, shift, w_in, att_w):
    bsz, seq, d = x.shape
    n = w_in.shape[1]
    ssm_w = n - 3 * att_w
    ts = min(PROJ_ROWS, seq)
    row = lambda b, i: (b, i, 0)
    per_b = lambda b, i: (b, 0, 0)
    kern = functools.partial(_inproj_kernel, att_w=att_w, q_scale=HEAD_DIM ** -0.5 * LOG2_E)
    return pl.pallas_call(
        kern,
        grid=(bsz, seq // ts),
        in_specs=[pl.BlockSpec((1, ts, d), row),
                  pl.BlockSpec((1, 1, d), per_b),
                  pl.BlockSpec((1, 1, d), per_b),
                  pl.BlockSpec((d, n), lambda b, i: (0, 0))],
        out_specs=[pl.BlockSpec((1, ts, att_w), row)] * 3 + [pl.BlockSpec((1, ts, ssm_w), row)],
        out_shape=[jax.ShapeDtypeStruct((bsz, seq, att_w), F32)] * 3
                  + [jax.ShapeDtypeStruct((bsz, seq, ssm_w), F32)],
        compiler_params=_cparams(("parallel", "parallel")),
        name="inproj",
    )(x, scale.reshape(bsz, 1, d), shift.reshape(bsz, 1, d), w_in.astype(BF16))


def _t5_bucket_np(dist):
    exact = N_BUCKETS // 2
    large = exact + (np.log(np.maximum(dist, 1).astype(np.float64) / exact)
                     / math.log(MAX_DISTANCE / exact) * (N_BUCKETS - exact)).astype(np.int64)
    return np.where(dist < exact, dist, np.minimum(large, N_BUCKETS - 1))


def _att_tables(rel_bias):
    qi = np.arange(ATT_BLOCK)[:, None]
    ki = np.arange(2 * ATT_BLOCK)[None, :]
    rel = qi + ATT_BLOCK - ki
    tabs = []
    for window, dil in PATTERNS:
        band = (rel >= 0) & (rel <= window // dil)
        bucket = _t5_bucket_np(np.maximum(rel, 0) * dil)
        onehot = (bucket[:, :, None] == np.arange(N_BUCKETS)[None, None, :]).astype(np.float32)
        bias = jnp.einsum('qkb,bh->hqk', onehot, rel_bias.astype(F32),
                          precision=lax.Precision.HIGHEST)
        full = jnp.where(band[None], bias * LOG2_E, NEG_INF)
        first = jnp.concatenate([full[:, :, ATT_BLOCK:], jnp.full_like(full[:, :, ATT_BLOCK:], NEG_INF)], axis=-1)
        tabs.append(jnp.stack([full, first]))
    return jnp.stack(tabs)


def _att_kernel(q_ref, k_ref, v_ref, tab_ref, o_ref, m_ref, l_ref, *, seq):
    lane = lax.broadcasted_iota(jnp.int32, (ATT_BLOCK, LANES), 1)
    head0 = lane < HEAD_DIM

    def rows(ref, start, n, dil):
        if dil == 1:
            return ref[0, pl.ds(pl.multiple_of(start, ATT_BLOCK), n), :]
        return ref[0, pl.ds(start, n, stride=dil), :]

    def lane_mask(n_rows, h):
        m = lax.broadcasted_iota(jnp.int32, (n_rows, LANES), 1) < HEAD_DIM
        return m if h == 0 else ~m

    def batch(pi, dil, nb, res, blk0, head, init):
        q_rows = nb * ATT_BLOCK
        kv_blocks = nb if head else nb + 1
        q_start = res + dil * ATT_BLOCK * blk0
        kv_start = q_start if head else q_start - dil * ATT_BLOCK
        if dil == 1:
            sl = pl.ds(pl.multiple_of(q_start, ATT_BLOCK), q_rows)
        else:
            sl = pl.ds(q_start, q_rows, stride=dil)
        q = rows(q_ref, q_start, q_rows, dil)
        k = rows(k_ref, kv_start, kv_blocks * ATT_BLOCK, dil).astype(BF16)
        v = rows(v_ref, kv_start, kv_blocks * ATT_BLOCK, dil)
        if not init:
            m_old, l_old, o_old = m_ref[sl, :], l_ref[sl, :], o_ref[0, sl, :]
        qh = [jnp.where(lane_mask(q_rows, h), q, 0.0).astype(BF16) for h in range(2)]
        vh = [jnp.where(lane_mask(kv_blocks * ATT_BLOCK, h), v, 1.0).astype(BF16) for h in range(2)]
        o_parts, l_parts, m_parts = [], [], []
        for j in range(nb):
            first = 1 if (head and j == 0) else 0
            kb = j if (not head or j == 0) else j - 1
            ksl = slice(kb * ATT_BLOCK, (kb + 2) * ATT_BLOCK)
            qsl = slice(j * ATT_BLOCK, (j + 1) * ATT_BLOCK)
            outs, ms = [], []
            for h in range(2):
                s = _dot_nt(qh[h][qsl], k[ksl]) + tab_ref[pi, first, h]
                m = jnp.max(s, axis=1, keepdims=True)
                p = jnp.exp2(s - m).astype(BF16)
                outs.append(_dot(p, vh[h][ksl]))
                ms.append(m)
            o_parts.append(jnp.where(head0, outs[0], outs[1]))
            l_parts.append(pltpu.roll(jnp.where(head0, outs[1], outs[0]), HEAD_DIM, axis=1))
            m_parts.append(jnp.where(head0, ms[0], ms[1]))
        o = jnp.concatenate(o_parts, axis=0)
        l = jnp.concatenate(l_parts, axis=0)
        m = jnp.concatenate(m_parts, axis=0)
        if not init:
            m_new = jnp.maximum(m_old, m)
            a_old = jnp.exp2(m_old - m_new)
            a_cur = jnp.exp2(m - m_new)
            o = o_old * a_old + o * a_cur
            l = l_old * a_old + l * a_cur
            m = m_new
        o_ref[0, sl, :] = o
        m_ref[sl, :] = m
        l_ref[sl, :] = l

    n_units = seq // ATT_BLOCK
    order = sorted(range(len(PATTERNS)), key=lambda p: -PATTERNS[p][1])
    for pos, pi in enumerate(order):
        dil = PATTERNS[pi][1]
        log_d = dil.bit_length() - 1
        res_blocks = seq // (dil * ATT_BLOCK)
        nb = min(ATT_UNITS_PER_STEP, res_blocks)
        per_res = res_blocks // nb
        init = pos == 0

        def head_body(res, carry, pi=pi, dil=dil, nb=nb, init=init):
            batch(pi, dil, nb, res, 0, True, init)
            return carry
        lax.fori_loop(0, dil, head_body, 0)

        if per_res > 1:
            def tail_body(i, carry, pi=pi, dil=dil, nb=nb, log_d=log_d, init=init):
                res = i & (dil - 1)
                blk0 = nb * (1 + (i >> log_d))
                batch(pi, dil, nb, res, blk0, False, init)
                return carry
            lax.fori_loop(0, dil * (per_res - 1), tail_body, 0)

    def finish(i, carry):
        sl = pl.ds(pl.multiple_of(i * ATT_BLOCK, ATT_BLOCK), ATT_BLOCK)
        o_ref[0, sl, :] = o_ref[0, sl, :] / l_ref[sl, :]
        return carry
    lax.fori_loop(0, n_units, finish, 0)


def _attention(q, k, v, tables):
    bsz, seq, att_w = q.shape
    n_pairs = att_w // LANES
    blk = lambda b, p: (b, 0, p)
    qkv_spec = pl.BlockSpec((1, seq, LANES), blk)
    n_pat = tables.shape[0]
    tab_spec = pl.BlockSpec((n_pat, 2, 2, ATT_BLOCK, 2 * ATT_BLOCK), lambda b, p: (0, 0, p, 0, 0))
    return pl.pallas_call(
        functools.partial(_att_kernel, seq=seq),
        grid=(bsz, n_pairs),
        in_specs=[qkv_spec, qkv_spec, qkv_spec, tab_spec],
        out_specs=pl.BlockSpec((1, seq, LANES), blk),
        out_shape=jax.ShapeDtypeStruct((bsz, seq, att_w), F32),
        scratch_shapes=[pltpu.VMEM((seq, LANES), F32), pltpu.VMEM((seq, LANES), F32)],
        compiler_params=_cparams(("parallel", "parallel")),
        name="dilated_attention",
    )(q, k, v, tables)


def _ssm_params(a_re, a_im, b_re, b_im, c_re, c_im, d_skip, log_dt):
    n_g, n_p = a_re.shape
    n_c = b_re.shape[-1]
    n_state = n_g * n_p
    dt = jnp.exp(log_dt.astype(F32))[:, None]

    def a_pow(kk):
        mag = jnp.exp(kk * dt * a_re)
        ph = kk * dt * a_im
        return mag * jnp.cos(ph), mag * jnp.sin(ph)

    ab_re, ab_im = a_pow(1.0)
    nr, ni = ab_re - 1.0, ab_im
    den = a_re * a_re + a_im * a_im
    f_re = (nr * a_re + ni * a_im) / den
    f_im = (ni * a_re - nr * a_im) / den
    bb_re = f_re[:, :, None] * b_re - f_im[:, :, None] * b_im
    bb_im = f_re[:, :, None] * b_im + f_im[:, :, None] * b_re
    eye = jnp.eye(n_g, dtype=F32)
    w_re = jnp.einsum('gpc,gh->gchp', bb_re, eye).reshape(n_g * n_c, n_state)
    w_im = jnp.einsum('gpc,gh->gchp', bb_im, eye).reshape(n_g * n_c, n_state)
    o_re = jnp.einsum('gcp,gh->gphc', c_re, eye).reshape(n_state, n_g * n_c)
    o_im = jnp.einsum('gcp,gh->gphc', c_im, eye).reshape(n_state, n_g * n_c)
    n_kb = (n_g * n_c) // LANES
    sp = n_state // n_kb
    w_in = jnp.stack([jnp.concatenate([w_re[kb * LANES:(kb + 1) * LANES, kb * sp:(kb + 1) * sp],
                                       w_im[kb * LANES:(kb + 1) * LANES, kb * sp:(kb + 1) * sp]], axis=1)
                      for kb in range(n_kb)])
    w_out = jnp.stack([jnp.concatenate([o_re[kb * sp:(kb + 1) * sp, kb * LANES:(kb + 1) * LANES],
                                        -o_im[kb * sp:(kb + 1) * sp, kb * LANES:(kb + 1) * LANES]], axis=0)
                       for kb in range(n_kb)])
    j = jnp.arange(SUBLANES, dtype=F32)[:, None]
    flat = lambda t: jnp.broadcast_to(t.reshape(1, n_state), (SUBLANES, n_state))
    coef = []
    for sh in (1, 2, 4):
        pr, pim = a_pow(float(sh))
        keep = (j >= sh).astype(F32)
        coef += [flat(pr) * keep, flat(pim) * keep]
    a_re_f = jnp.broadcast_to(a_re.reshape(1, n_state), (SUBLANES, n_state))
    a_im_f = jnp.broadcast_to(a_im.reshape(1, n_state), (SUBLANES, n_state))
    dt_f = jnp.broadcast_to(jnp.repeat(dt[:, 0], n_p).reshape(1, n_state), (SUBLANES, n_state))
    mag = jnp.exp((j + 1.0) * dt_f * a_re_f)
    ph = (j + 1.0) * dt_f * a_im_f
    coef += [mag * jnp.cos(ph), mag * jnp.sin(ph)]
    coef = jnp.stack(coef)
    return w_in.astype(BF16), w_out.astype(BF16), coef, d_skip.reshape(1, n_g * n_c).astype(F32)


def _gelu_tanh(x):
    return 0.5 * x * (1.0 + jnp.tanh(math.sqrt(2.0 / math.pi) * (x + 0.044715 * (x * x * x))))


def _ssm_kernel(u_ref, win_ref, wout_ref, coef_ref, d_ref, wglu_ref, bglu_ref, o_ref,
                hre_ref, him_ref, cre_ref, cim_ref, *, rows, n_state, col_w):
    n_kb = win_ref.shape[0]
    sp = n_state // n_kb

    @pl.when(pl.program_id(1) == 0)
    def _():
        cre_ref[...] = jnp.zeros_like(cre_ref)
        cim_ref[...] = jnp.zeros_like(cim_ref)

    u = u_ref[0]
    ub = u.astype(BF16)
    for kb in range(n_kb):
        bu = _dot(ub[:, kb * LANES:(kb + 1) * LANES], win_ref[kb])
        hre_ref[:, kb * sp:(kb + 1) * sp] = bu[:, :sp]
        him_ref[:, kb * sp:(kb + 1) * sp] = bu[:, sp:]

    n_slab = rows // SUBLANES
    for c0 in range(0, n_state, col_w):
        cs = slice(c0, c0 + col_w)

        def body(i, carry, cs=cs):
            c_re, c_im = carry
            sl = pl.ds(pl.multiple_of(i * SUBLANES, SUBLANES), SUBLANES)
            x_re = hre_ref[sl, cs]
            x_im = him_ref[sl, cs]
            for si, sh in enumerate((1, 2, 4)):
                p_re = coef_ref[2 * si, :, cs]
                p_im = coef_ref[2 * si + 1, :, cs]
                s_re = pltpu.roll(x_re, sh, axis=0)
                s_im = pltpu.roll(x_im, sh, axis=0)
                x_re, x_im = (x_re + p_re * s_re - p_im * s_im,
                              x_im + p_re * s_im + p_im * s_re)
            p_re = coef_ref[6, :, cs]
            p_im = coef_ref[7, :, cs]
            h_re = x_re + p_re * c_re - p_im * c_im
            h_im = x_im + p_re * c_im + p_im * c_re
            hre_ref[sl, cs] = h_re
            him_ref[sl, cs] = h_im
            last = slice(SUBLANES - 1, SUBLANES)
            return (jnp.broadcast_to(h_re[last, :], h_re.shape),
                    jnp.broadcast_to(h_im[last, :], h_im.shape))

        c_re, c_im = lax.fori_loop(0, n_slab, body, (cre_ref[:, cs], cim_ref[:, cs]), unroll=True)
        cre_ref[:, cs] = c_re
        cim_ref[:, cs] = c_im

    ys = []
    for kb in range(n_kb):
        hcat = jnp.concatenate([hre_ref[:, kb * sp:(kb + 1) * sp].astype(BF16),
                                him_ref[:, kb * sp:(kb + 1) * sp].astype(BF16)], axis=1)
        ys.append(_dot(hcat, wout_ref[kb]))
    y = jnp.concatenate(ys, axis=1) + d_ref[...] * u
    y = _gelu_tanh(y)
    z = _dot(y.astype(BF16), wglu_ref[...]) + bglu_ref[...]
    o_ref[0] = y * jax.nn.sigmoid(z)


def _ssm(u, w_in, w_out, coef, d_flat, w_glu, b_glu):
    bsz, seq, ssm_w = u.shape
    n_state = coef.shape[-1]
    rows = min(SSM_ROWS, seq)
    full = lambda *shape: pl.BlockSpec(shape, lambda b, i: (0,) * len(shape))
    kern = functools.partial(_ssm_kernel, rows=rows, n_state=n_state, col_w=4 * LANES)
    return pl.pallas_call(
        kern,
        grid=(bsz, seq // rows),
        in_specs=[pl.BlockSpec((1, rows, ssm_w), lambda b, i: (b, i, 0)),
                  full(*w_in.shape), full(*w_out.shape), full(*coef.shape), full(1, ssm_w),
                  full(ssm_w, ssm_w), full(1, ssm_w)],
        out_specs=pl.BlockSpec((1, rows, ssm_w), lambda b, i: (b, i, 0)),
        out_shape=jax.ShapeDtypeStruct((bsz, seq, ssm_w), F32),
        scratch_shapes=[pltpu.VMEM((rows, n_state), F32), pltpu.VMEM((rows, n_state), F32),
                        pltpu.VMEM((SUBLANES, n_state), F32), pltpu.VMEM((SUBLANES, n_state), F32)],
        compiler_params=_cparams(("parallel", "arbitrary")),
        name="s5_glu",
    )(u, w_in, w_out, coef, d_flat, w_glu.astype(BF16), b_glu.reshape(1, ssm_w))


def _layer_norm(y, g, b):
    mu = jnp.mean(y, axis=-1, keepdims=True)
    yc = y - mu
    var = jnp.mean(yc * yc, axis=-1, keepdims=True)
    return yc * lax.rsqrt(var + EPS) * g + b


def _rms_norm(y, g):
    return y * lax.rsqrt(jnp.mean(y * y, axis=-1, keepdims=True) + EPS) * g


def _mix_kernel(x_ref, att_ref, ssm_ref, gatt_ref, gssm_ref, wout_ref, g1_ref, ln_g_ref, ln_b_ref,
                sc_ref, sh_ref, g2_ref, wsg_ref, wsu_ref, wsd_ref, h_ref, ht_ref, pre_ref, *, alpha, att_w):
    a_n = _rms_norm(att_ref[0], gatt_ref[...]).astype(BF16)
    s_n = _rms_norm(ssm_ref[0], gssm_ref[...]).astype(BF16)
    mix = _dot(a_n, wout_ref[0:att_w, :]) + _dot(s_n, wout_ref[att_w:, :])
    x1 = _layer_norm(alpha * x_ref[0] + g1_ref[0] * mix, ln_g_ref[...], ln_b_ref[...])
    h = x1 * (1.0 + sc_ref[0]) + sh_ref[0]
    h_ref[0] = h
    _store_token_tiles(ht_ref, h)
    hb = h.astype(BF16)
    hid = _silu(_dot(hb, wsg_ref[...])) * _dot(hb, wsu_ref[...])
    shared = _dot(hid.astype(BF16), wsd_ref[...])
    pre_ref[0] = alpha * x1 + g2_ref[0] * shared


def _mix(x, att, ssm, g_att, g_ssm, w_out, gate1, ln_g, ln_b, scale2, shift2, gate2,
         w_s_gate, w_s_up, w_s_down, alpha):
    bsz, seq, d = x.shape
    att_w = att.shape[-1]
    ssm_w = ssm.shape[-1]
    ff = w_s_gate.shape[1]
    tm = min(MIX_ROWS, seq)
    n_sub = d // LANES
    row = lambda b, i: (b, i, 0)
    per_b = lambda b, i: (b, 0, 0)
    full = lambda *shape: pl.BlockSpec(shape, lambda b, i: (0,) * len(shape))
    vec = lambda t: t.reshape(bsz, 1, d)
    return pl.pallas_call(
        functools.partial(_mix_kernel, alpha=alpha, att_w=att_w),
        grid=(bsz, seq // tm),
        in_specs=[pl.BlockSpec((1, tm, d), row), pl.BlockSpec((1, tm, att_w), row),
                  pl.BlockSpec((1, tm, ssm_w), row),
                  full(1, att_w), full(1, ssm_w), full(att_w + ssm_w, d),
                  pl.BlockSpec((1, 1, d), per_b), full(1, d), full(1, d),
                  pl.BlockSpec((1, 1, d), per_b), pl.BlockSpec((1, 1, d), per_b),
                  pl.BlockSpec((1, 1, d), per_b),
                  full(d, ff), full(d, ff), full(ff, d)],
        out_specs=[pl.BlockSpec((1, tm, d), row),
                   pl.BlockSpec((tm * n_sub, LANES), lambda b, i: (b * (seq // tm) + i, 0)),
                   pl.BlockSpec((1, tm, d), row)],
        out_shape=[jax.ShapeDtypeStruct((bsz, seq, d), F32),
                   jax.ShapeDtypeStruct((bsz * seq * n_sub, LANES), F32),
                   jax.ShapeDtypeStruct((bsz, seq, d), F32)],
        compiler_params=_cparams(("parallel", "parallel")),
        name="mix_ln1_shared",
    )(x, att, ssm, g_att.reshape(1, att_w), g_ssm.reshape(1, ssm_w), w_out.astype(BF16),
      vec(gate1), ln_g.reshape(1, d), ln_b.reshape(1, d), vec(scale2), vec(shift2), vec(gate2),
      w_s_gate.astype(BF16), w_s_up.astype(BF16), w_s_down.astype(BF16))


def _route_kernel(h_ref, wrt_ref, wrt_lo_ref, bias_ref, tri_ref, idx_ref, gate_ref, rank_ref, cnt_ref, carry_ref):
    n_e = wrt_ref.shape[0]
    tm = h_ref.shape[0]
    per_group = n_e // N_EXPERT_GROUPS
    neg = -jnp.inf

    @pl.when(pl.program_id(0) == 0)
    def _():
        carry_ref[...] = jnp.zeros_like(carry_ref)

    h_hi, h_lo = _split_bf16(h_ref[...])
    logits = (_dot_nt(wrt_ref[...], h_hi) + _dot_nt(wrt_ref[...], h_lo)
              + _dot_nt(wrt_lo_ref[...], h_hi))
    scores = jax.nn.sigmoid(logits)
    biased = scores + bias_ref[...]

    g3 = biased.reshape(N_EXPERT_GROUPS, per_group, tm)
    ridx = lax.broadcasted_iota(jnp.int32, g3.shape, 1).astype(F32)
    m1 = jnp.max(g3, axis=1, keepdims=True)
    first = jnp.min(jnp.where(g3 == m1, ridx, float(per_group)), axis=1, keepdims=True)
    m2 = jnp.max(jnp.where(ridx == first, neg, g3), axis=1, keepdims=True)
    gs = m1 + m2

    gidx = lax.broadcasted_iota(jnp.int32, gs.shape, 0).astype(F32)
    ok = jnp.zeros_like(gs)
    cur = gs
    for _ in range(TOPK_GROUPS):
        mx = jnp.max(cur, axis=0, keepdims=True)
        fi = jnp.min(jnp.where(cur == mx, gidx, float(N_EXPERT_GROUPS)), axis=0, keepdims=True)
        hit = gidx == fi
        ok = jnp.where(hit, 1.0, ok)
        cur = jnp.where(hit, neg, cur)
    masked = jnp.where(ok > 0.5, g3, neg).reshape(n_e, tm)

    eidx = lax.broadcasted_iota(jnp.int32, (n_e, tm), 0).astype(F32)
    onehot = jnp.zeros((n_e, tm), F32)
    cur = masked
    sel_idx = []
    sel_gate = []
    for _ in range(TOP_K):
        mx = jnp.max(cur, axis=0, keepdims=True)
        fi = jnp.min(jnp.where(cur == mx, eidx, float(n_e)), axis=0, keepdims=True)
        hit = eidx == fi
        sel_idx.append(fi)
        sel_gate.append(jnp.sum(jnp.where(hit, scores, 0.0), axis=0, keepdims=True))
        onehot = jnp.where(hit, 1.0, onehot)
        cur = jnp.where(hit, neg, cur)
    idx = jnp.concatenate(sel_idx, axis=0)
    gate = jnp.concatenate(sel_gate, axis=0)
    gate = gate / jnp.sum(gate, axis=0, keepdims=True) * ROUTED_SCALE

    prior = _dot(onehot.astype(BF16), tri_ref[...]) + carry_ref[:, 0:1]
    ranks = [jnp.sum(jnp.where(eidx == sel_idx[k], prior, 0.0), axis=0, keepdims=True)
             for k in range(TOP_K)]
    rank = jnp.concatenate(ranks, axis=0)
    carry = carry_ref[...] + jnp.sum(onehot, axis=1, keepdims=True)
    carry_ref[...] = carry
    cnt_ref[...] = carry

    idx_ref[...] = idx.astype(jnp.int32)
    gate_ref[...] = gate
    rank_ref[...] = rank.astype(jnp.int32)


def _route(h2d, w_router, router_bias):
    n_tok, d = h2d.shape
    n_e = w_router.shape[1]
    tm = min(ROUTE_COLS, n_tok)
    tri = (np.arange(tm)[:, None] < np.arange(tm)[None, :]).astype(np.float32)
    w_hi, w_lo = _split_bf16(w_router.T.astype(F32))
    col = lambda i: (0, i)
    full = lambda *shape: pl.BlockSpec(shape, lambda i: (0,) * len(shape))
    idx, gate, rank, cnt = pl.pallas_call(
        _route_kernel,
        grid=(n_tok // tm,),
        in_specs=[pl.BlockSpec((tm, d), lambda i: (i, 0)), full(n_e, d), full(n_e, d), full(n_e, 1),
                  full(tm, tm)],
        out_specs=[pl.BlockSpec((TOP_K, tm), col), pl.BlockSpec((TOP_K, tm), col),
                   pl.BlockSpec((TOP_K, tm), col), full(n_e, LANES)],
        out_shape=[jax.ShapeDtypeStruct((TOP_K, n_tok), jnp.int32),
                   jax.ShapeDtypeStruct((TOP_K, n_tok), F32),
                   jax.ShapeDtypeStruct((TOP_K, n_tok), jnp.int32),
                   jax.ShapeDtypeStruct((n_e, LANES), F32)],
        scratch_shapes=[pltpu.VMEM((n_e, LANES), F32)],
        compiler_params=_cparams(("arbitrary",)),
        name="router_topk",
    )(h2d, w_hi, w_lo, router_bias.reshape(n_e, 1).astype(F32), jnp.asarray(tri, BF16))
    return idx, gate, rank, cnt[:, 0].astype(jnp.int32)


def _sc_mesh():
    return plsc.VectorSubcoreMesh(core_axis_name="core", subcore_axis_name="subcore")


def _dispatch(h_tiles, gate_rows, dest_windows, n_rows):
    n_tok, n_sub, _ = h_tiles.shape
    win = DISPATCH_WINDOW

    @pl.kernel(out_type=[jax.ShapeDtypeStruct((n_rows, n_sub, LANES), h_tiles.dtype),
                         jax.ShapeDtypeStruct((n_rows, LANES), F32)],
               mesh=_sc_mesh(), scratch_types=[])
    def scatter_tiles(x_hbm, g_hbm, d_hbm, xs_hbm, rg_hbm):
        def window(x_vmem, g_vmem, d_vmem):
            for k in range(TOP_K):
                rows = d_vmem.at[0, pl.ds(k * win, win)]
                pltpu.sync_copy(x_vmem, xs_hbm.at[rows])
                pltpu.sync_copy(g_vmem.at[k], rg_hbm.at[rows])
        pltpu.emit_pipeline(
            window,
            grid=(n_tok // win,),
            in_specs=[pl.BlockSpec((win, n_sub, LANES), lambda i: (i, 0, 0)),
                      pl.BlockSpec((TOP_K, win, LANES), lambda i: (0, i, 0)),
                      pl.BlockSpec((1, TOP_K * win), lambda i: (i, 0))],
            out_specs=[],
            core_axis_name=("core", "subcore"),
            dimension_semantics=(pltpu.PARALLEL,),
        )(x_hbm, g_hbm, d_hbm)

    return scatter_tiles(h_tiles, gate_rows, dest_windows)


def _gather_sum(ys_tiles, dest_windows, n_tok):
    _, n_sub, _ = ys_tiles.shape
    win = DISPATCH_WINDOW

    @pl.kernel(out_type=jax.ShapeDtypeStruct((n_tok, n_sub, LANES), ys_tiles.dtype), mesh=_sc_mesh(),
               scratch_types=[])
    def gather_tiles(y_hbm, d_hbm, o_hbm):
        def window(d_vmem, o_vmem):
            pltpu.sync_copy(y_hbm.at[d_vmem.at[0, pl.ds(0, win)]], o_vmem)
            for k in range(1, TOP_K):
                pltpu.sync_copy(y_hbm.at[d_vmem.at[0, pl.ds(k * win, win)]], o_vmem, add=True)
        pltpu.emit_pipeline(
            window,
            grid=(n_tok // win,),
            in_specs=[pl.BlockSpec((1, TOP_K * win), lambda i: (i, 0))],
            out_specs=[pl.BlockSpec((win, n_sub, LANES), lambda i: (i, 0, 0))],
            core_axis_name=("core", "subcore"),
            dimension_semantics=(pltpu.PARALLEL,),
        )(d_hbm, o_hbm)

    return gather_tiles(ys_tiles, dest_windows)


def _expert_kernel(first_ref, nblk_ref, cnt_ref, nused_ref, xs_ref, rg_ref, wg_ref, wu_ref, wd_ref, ys_ref,
                   xbuf_ref, gbuf_ref, ybuf_ref, wgb_ref, wub_ref, wdb_ref, xsem_ref, ysem_ref):
    e = pl.program_id(0)
    n_used = nused_ref[0]
    d = wgb_ref.shape[0]
    blk_rows = xbuf_ref.shape[1]

    class _Pair:
        def __init__(self, g, slot):
            src = xs_ref.at[pl.ds(pl.multiple_of(g * blk_rows, blk_rows), blk_rows)]
            gsrc = rg_ref.at[pl.ds(pl.multiple_of(g * MOE_ROWS, MOE_ROWS), MOE_ROWS)]
            self.copies = (pltpu.make_async_copy(src, xbuf_ref.at[slot], xsem_ref.at[slot]),
                           pltpu.make_async_copy(gsrc, gbuf_ref.at[slot], xsem_ref.at[slot]))

        def start(self):
            for cp in self.copies:
                cp.start()

        def wait(self):
            for cp in self.copies:
                cp.wait()

    def x_copy(g, slot):
        return _Pair(g, slot)

    def y_copy(g, slot):
        dst = ys_ref.at[pl.ds(pl.multiple_of(g * blk_rows, blk_rows), blk_rows)]
        return pltpu.make_async_copy(ybuf_ref.at[slot], dst, ysem_ref.at[slot])

    n_slot = xbuf_ref.shape[0]

    @pl.when(e == 0)
    def _():
        for g in range(n_slot - 1):
            @pl.when(g < n_used)
            def _(g=g):
                x_copy(g, g).start()

    @pl.when(nblk_ref[e] > 0)
    def _():
        wgb_ref[...] = wg_ref[0].astype(BF16)
        wub_ref[...] = wu_ref[0].astype(BF16)
        wdb_ref[...] = wd_ref[0].astype(BF16)

    def block(j, carry):
        g = first_ref[e] + j
        slot = g & (n_slot - 1)
        x_copy(g, slot).wait()

        @pl.when(g + n_slot - 1 < n_used)
        def _():
            x_copy(g + n_slot - 1, (g + n_slot - 1) & (n_slot - 1)).start()

        @pl.when(g >= n_slot)
        def _():
            y_copy(g - n_slot, slot).wait()

        x = jnp.concatenate(_load_token_tiles(xbuf_ref.at[slot], MOE_ROWS, d), axis=1)
        row = lax.broadcasted_iota(jnp.int32, (MOE_ROWS, 1), 0)
        xb = jnp.where(row < cnt_ref[e] - j * MOE_ROWS, x, 0.0).astype(BF16)
        hid = _silu(_dot(xb, wgb_ref[...])) * _dot(xb, wub_ref[...])
        y = _dot(hid.astype(BF16), wdb_ref[...]) * gbuf_ref[slot][:, 0:1]
        _store_token_tiles(ybuf_ref.at[slot], jnp.where(row < cnt_ref[e] - j * MOE_ROWS, y, 0.0))
        y_copy(g, slot).start()
        return carry
    lax.fori_loop(0, nblk_ref[e], block, 0)

    @pl.when(e == pl.num_programs(0) - 1)
    def _():
        for back in range(1, n_slot + 1):
            @pl.when(n_used >= back)
            def _(back=back):
                y_copy(n_used - back, (n_used - back) & (n_slot - 1)).wait()


def _experts(xs_tiles, row_gates, plan, w_gate, w_up, w_down):
    n_rows, n_sub, _ = xs_tiles.shape
    d = n_sub * LANES
    n_e, _, ff = w_gate.shape
    blk_rows = MOE_ROWS * n_sub
    hbm = pl.BlockSpec(memory_space=pl.ANY)
    grid_spec = pltpu.PrefetchScalarGridSpec(
        num_scalar_prefetch=4,
        grid=(n_e,),
        in_specs=[hbm, hbm,
                  pl.BlockSpec((1, d, ff), lambda e, *_: (e, 0, 0)),
                  pl.BlockSpec((1, d, ff), lambda e, *_: (e, 0, 0)),
                  pl.BlockSpec((1, ff, d), lambda e, *_: (e, 0, 0))],
        out_specs=hbm,
        scratch_shapes=[pltpu.VMEM((EXPERT_SLOTS, blk_rows, LANES), F32),
                        pltpu.VMEM((EXPERT_SLOTS, MOE_ROWS, LANES), F32),
                        pltpu.VMEM((EXPERT_SLOTS, blk_rows, LANES), F32),
                        pltpu.VMEM((d, ff), BF16), pltpu.VMEM((d, ff), BF16), pltpu.VMEM((ff, d), BF16),
                        pltpu.SemaphoreType.DMA((EXPERT_SLOTS,)), pltpu.SemaphoreType.DMA((EXPERT_SLOTS,))],
    )
    ys = pl.pallas_call(
        _expert_kernel,
        grid_spec=grid_spec,
        out_shape=jax.ShapeDtypeStruct((n_rows * n_sub, LANES), F32),
        compiler_params=_cparams(("arbitrary",)),
        name="moe_experts",
    )(*plan, xs_tiles.reshape(n_rows * n_sub, LANES), row_gates, w_gate, w_up, w_down)
    return ys.reshape(n_rows, n_sub, LANES)


def _combine_kernel(dest_ref, ys_ref, pre_ref, g2_ref, ln_g_ref, ln_b_ref, o_ref, buf_ref, sem_ref):
    tm, d = pre_ref.shape[1], pre_ref.shape[2]
    n_sub = d // LANES

    def row_copy(t, k, src_row):
        slot = pl.multiple_of((k * tm + t) * n_sub, n_sub)
        return pltpu.make_async_copy(ys_ref.at[src_row], buf_ref.at[pl.ds(slot, n_sub)], sem_ref.at[0])

    def issue(g, c):
        t0 = g * DMA_ISSUE_UNROLL
        rows = [[dest_ref[(t0 + j) * TOP_K + k] for k in range(TOP_K)] for j in range(DMA_ISSUE_UNROLL)]
        for j in range(DMA_ISSUE_UNROLL):
            for k in range(TOP_K):
                row_copy(t0 + j, k, rows[j][k]).start(priority=k % 2)
        return c
    lax.fori_loop(0, tm // DMA_ISSUE_UNROLL, issue, 0)

    def drain(t, c):
        for k in range(TOP_K):
            row_copy(t, k, dest_ref[t * TOP_K + k]).wait()
        return c
    lax.fori_loop(0, tm, drain, 0)

    pieces = []
    for s in range(n_sub):
        acc = None
        for k in range(TOP_K):
            term = buf_ref[pl.ds(k * tm * n_sub + s, tm, stride=n_sub), :]
            acc = term if acc is None else acc + term
        pieces.append(acc)
    routed = jnp.concatenate(pieces, axis=1)
    y = pre_ref[0] + g2_ref[0] * routed
    o_ref[0] = _layer_norm(y, ln_g_ref[...], ln_b_ref[...])


def _combine(ys, dest, pre, gate2, ln_g, ln_b, n_tiles):
    bsz, seq, d = pre.shape
    tm = min(DISPATCH_ROWS, seq)
    per_seq = seq // tm
    row = lambda i, *_: (i // per_seq, i % per_seq, 0)
    per_b = lambda i, *_: (i // per_seq, 0, 0)
    full = lambda *shape: pl.BlockSpec(shape, lambda i, *_: (0,) * len(shape))
    grid_spec = pltpu.PrefetchScalarGridSpec(
        num_scalar_prefetch=0,
        grid=(n_tiles,),
        in_specs=[pl.BlockSpec((TOP_K * tm,), lambda i: (i,), memory_space=pltpu.SMEM),
                  pl.BlockSpec(memory_space=pl.ANY),
                  pl.BlockSpec((1, tm, d), row), pl.BlockSpec((1, 1, d), per_b), full(1, d), full(1, d)],
        out_specs=pl.BlockSpec((1, tm, d), row),
        scratch_shapes=[pltpu.VMEM((TOP_K * tm * (d // LANES), LANES), F32), pltpu.SemaphoreType.DMA((1,))],
    )
    return pl.pallas_call(
        _combine_kernel,
        grid_spec=grid_spec,
        out_shape=jax.ShapeDtypeStruct((bsz, seq, d), F32),
        compiler_params=_cparams(("arbitrary",)),
        name="moe_combine_ln2",
    )(dest, ys, pre, gate2.reshape(bsz, 1, d), ln_g.reshape(1, d), ln_b.reshape(1, d))


def _final_ln_kernel(r_ref, pre_ref, g2_ref, ln_g_ref, ln_b_ref, *rest):
    o_ref = rest[-1]
    tm, d = pre_ref.shape[1], pre_ref.shape[2]
    routed = jnp.concatenate(_load_token_tiles(r_ref, tm, d), axis=1)
    y = pre_ref[0] + g2_ref[0] * routed
    o_ref[0] = _layer_norm(y, ln_g_ref[...], ln_b_ref[...])


def _final_ln(routed_tiles, pre, gate2, ln_g, ln_b, first_tile, partial=None):
    bsz, seq, d = pre.shape
    tm = min(DISPATCH_ROWS, seq)
    per_seq = seq // tm
    n_sub = d // LANES
    n_tiles = bsz * per_seq - first_tile
    row = lambda i: ((i + first_tile) // per_seq, (i + first_tile) % per_seq, 0)
    per_b = lambda i: ((i + first_tile) // per_seq, 0, 0)
    full = lambda *shape: pl.BlockSpec(shape, lambda i: (0,) * len(shape))
    in_specs = [pl.BlockSpec((tm * n_sub, LANES), lambda i: (i, 0)),
                pl.BlockSpec((1, tm, d), row), pl.BlockSpec((1, 1, d), per_b), full(1, d), full(1, d)]
    args = [routed_tiles.reshape(-1, LANES), pre, gate2.reshape(bsz, 1, d), ln_g.reshape(1, d),
            ln_b.reshape(1, d)]
    aliases = {}
    if partial is not None:
        in_specs.append(pl.BlockSpec(memory_space=pl.ANY))
        args.append(partial)
        aliases = {len(args) - 1: 0}
    return pl.pallas_call(
        _final_ln_kernel,
        grid=(n_tiles,),
        in_specs=in_specs,
        out_specs=pl.BlockSpec((1, tm, d), row),
        out_shape=jax.ShapeDtypeStruct((bsz, seq, d), F32),
        input_output_aliases=aliases,
        compiler_params=_cparams(("arbitrary",)),
        name="moe_final_ln2",
    )(*args)


def _dest_kernel(pstart_ref, idx_ref, rank_ref, dest_ref):
    idx = idx_ref[...]

    def body(e, acc):
        return acc + jnp.where(idx == e, pstart_ref[e], 0)
    dest_ref[...] = lax.fori_loop(0, pstart_ref.shape[0], body, rank_ref[...], unroll=8)


def _dest_rows(pstart, idx, rank):
    n_k, n_tok = idx.shape
    tn = min(4096, n_tok)
    grid_spec = pltpu.PrefetchScalarGridSpec(
        num_scalar_prefetch=1,
        grid=(n_tok // tn,),
        in_specs=[pl.BlockSpec((n_k, tn), lambda i, *_: (0, i))] * 2,
        out_specs=pl.BlockSpec((n_k, tn), lambda i, *_: (0, i)),
    )
    return pl.pallas_call(
        _dest_kernel,
        grid_spec=grid_spec,
        out_shape=jax.ShapeDtypeStruct((n_k, n_tok), jnp.int32),
        compiler_params=_cparams(("arbitrary",)),
        name="moe_dest",
    )(pstart, idx, rank)


def _moe_plan(idx, rank, cnt, n_tok):
    n_e = cnt.shape[0]
    padded = (cnt + MOE_ROWS - 1) // MOE_ROWS * MOE_ROWS
    pend = jnp.cumsum(padded).astype(jnp.int32)
    pstart = pend - padded
    dest_kt = _dest_rows(pstart, idx, rank)
    dest = dest_kt.T.reshape(-1)
    win = min(DISPATCH_WINDOW, n_tok)
    dest_windows = dest_kt.reshape(TOP_K, n_tok // win, win).transpose(1, 0, 2).reshape(n_tok // win, TOP_K * win)
    n_blk = n_tok * TOP_K // MOE_ROWS + n_e
    n_used = (pend[-1:] // MOE_ROWS).astype(jnp.int32)
    plan = ((pstart // MOE_ROWS).astype(jnp.int32), (padded // MOE_ROWS).astype(jnp.int32),
            cnt.astype(jnp.int32), n_used)
    return dest, dest_windows, plan, n_blk * MOE_ROWS


def kernel(x, c, rel_bias, w_ada, b_ada, w_in, ssm_a_re, ssm_a_im, ssm_b_re, ssm_b_im, ssm_c_re, ssm_c_im, ssm_d, ssm_log_dt, w_glu, b_glu, g_att, g_ssm, w_out, ln1_g, ln1_b, w_router, router_bias, w_e_gate, w_e_up, w_e_down, w_s_gate, w_s_up, w_s_down, ln2_g, ln2_b):
    bsz, seq, d = x.shape
    depth = w_ada.shape[0]
    alpha = (2 * depth) ** 0.25
    att_w = g_att.shape[-1]
    tables = _att_tables(rel_bias)
    for layer in range(depth):
        ada = _ada(c, w_ada[layer], b_ada[layer])
        shift1, scale1, gate1, shift2, scale2, gate2 = jnp.split(ada, 6, axis=-1)
        q, k, v, u = _inproj(x, scale1, shift1, w_in[layer], att_w)
        att = _attention(q, k, v, tables)
        ssm_prm = _ssm_params(ssm_a_re[layer], ssm_a_im[layer], ssm_b_re[layer], ssm_b_im[layer],
                              ssm_c_re[layer], ssm_c_im[layer], ssm_d[layer], ssm_log_dt[layer])
        ssm = _ssm(u, *ssm_prm, w_glu[layer], b_glu[layer])
        h2, h2_tiles, pre = _mix(x, att, ssm, g_att[layer], g_ssm[layer], w_out[layer], gate1,
                                 ln1_g[layer], ln1_b[layer], scale2, shift2, gate2, w_s_gate[layer],
                                 w_s_up[layer], w_s_down[layer], alpha)
        h2d = h2.reshape(bsz * seq, d)
        idx, gate, rank, cnt = _route(h2d, w_router[layer], router_bias[layer])
        dest, dest_windows, plan, n_rows = _moe_plan(idx, rank, cnt, bsz * seq)
        n_tok = bsz * seq
        gate_rows = jnp.broadcast_to(gate[:, :, None], (TOP_K, n_tok, LANES))
        xs, row_gates = _dispatch(h2_tiles.reshape(n_tok, d // LANES, LANES), gate_rows, dest_windows, n_rows)
        ys = _experts(xs, row_gates, plan, w_e_gate[layer], w_e_up[layer], w_e_down[layer])
        tm = min(DISPATCH_ROWS, seq)
        tc_tiles = min(COMBINE_TC_TILES, n_tok // tm)
        tc_tokens = tc_tiles * tm
        win = min(DISPATCH_WINDOW, n_tok)
        partial = (_combine(ys, dest, pre, gate2, ln2_g[layer], ln2_b[layer], tc_tiles)
                   if tc_tiles > 0 else None)
        routed = _gather_sum(ys, dest_windows[tc_tokens // win:], n_tok - tc_tokens)
        x = _final_ln(routed, pre, gate2, ln2_g[layer], ln2_b[layer], tc_tiles, partial)
    return x
```

```python
import functools
import math

import jax
import jax.numpy as jnp
import numpy as np
from jax import lax
from jax.experimental import pallas as pl
from jax.experimental.pallas import tpu as pltpu
from jax.experimental.pallas import tpu_sc as plsc

F32 = jnp.float32
BF16 = jnp.bfloat16

HEAD_DIM = 64
ATT_BLOCK = 128
PATTERNS = ((128, 1), (512, 4), (2048, 16))
N_BUCKETS = 32
MAX_DISTANCE = 2048
SSM_GROUP = 16
SSM_STATE = 64
N_EXPERTS = 256
TOP_K = 8
N_EXPERT_GROUPS = 8
TOPK_GROUPS = 4
ROUTED_SCALE = 2.5
EPS = 1e-5
NEG_INF = -1e30
LOG2_E = math.log2(math.e)

LANES = 128
SUBLANES = 8
VMEM_LIMIT_BYTES = 56 * 1024 * 1024

PROJ_ROWS = 512
SSM_ROWS = 256
MIX_ROWS = 512
ROUTE_COLS = 512
MOE_ROWS = 256
DISPATCH_ROWS = 256
COMBINE_TC_TILES = 48
DISPATCH_WINDOW = 16
ATT_UNITS_PER_STEP = 8
DMA_ISSUE_UNROLL = 4
EXPERT_SLOTS = 4


def _cparams(sem, vmem=VMEM_LIMIT_BYTES):
    return pltpu.CompilerParams(dimension_semantics=sem, vmem_limit_bytes=vmem)


def _dot(a, b):
    return jnp.dot(a, b, preferred_element_type=F32)


def _dot_nt(a, b):
    return lax.dot_general(a, b, (((1,), (1,)), ((), ())), preferred_element_type=F32)


def _silu(x):
    return x * jax.nn.sigmoid(x)


def _store_token_tiles(ref, x):
    m, d = x.shape
    n_sub = d // LANES
    for s in range(n_sub):
        ref[pl.ds(s, m, stride=n_sub), :] = x[:, s * LANES:(s + 1) * LANES]


def _load_token_tiles(ref, m, d, row0=0):
    n_sub = d // LANES
    return [ref[pl.ds(row0 + s, m, stride=n_sub), :] for s in range(n_sub)]


def _split_bf16(x):
    hi = x.astype(BF16)
    lo = (x - hi.astype(F32)).astype(BF16)
    return hi, lo


def _ada_kernel(c_ref, w_ref, b_ref, o_ref):
    c = c_ref[...]
    a_hi, a_lo = _split_bf16(_silu(c))
    w_hi, w_lo = _split_bf16(w_ref[...])
    acc = _dot(a_hi, w_hi) + _dot(a_hi, w_lo) + _dot(a_lo, w_hi)
    o_ref[...] = acc + b_ref[...]


def _ada(c, w_ada, b_ada):
    bsz, d = c.shape
    n = w_ada.shape[1]
    rows = SUBLANES
    c_pad = jnp.zeros((rows, d), F32).at[:bsz].set(c)
    tn = 1024
    out = pl.pallas_call(
        _ada_kernel,
        grid=(n // tn,),
        in_specs=[pl.BlockSpec((rows, d), lambda j: (0, 0)),
                  pl.BlockSpec((d, tn), lambda j: (0, j)),
                  pl.BlockSpec((1, tn), lambda j: (0, j))],
        out_specs=pl.BlockSpec((rows, tn), lambda j: (0, j)),
        out_shape=jax.ShapeDtypeStruct((rows, n), F32),
        compiler_params=_cparams(("parallel",)),
        name="ada",
    )(c_pad, w_ada, b_ada.reshape(1, n))
    return out[:bsz]


def _inproj_kernel(x_ref, sc_ref, sh_ref, w_ref, q_ref, k_ref, v_ref, u_ref, *, att_w, q_scale):
    h = (x_ref[0] * (1.0 + sc_ref[0]) + sh_ref[0]).astype(BF16)
    q_ref[0] = _dot(h, w_ref[:, 0:att_w]) * q_scale
    k_ref[0] = _dot(h, w_ref[:, att_w:2 * att_w])
    v_ref[0] = _dot(h, w_ref[:, 2 * att_w:3 * att_w])
    u_ref[0] = _dot(h, w_ref[:, 3 * att_w:])


def _inproj(x, scale, shift, w_in, att_w):
    bsz, seq, d = x.shape
    n = w_in.shape[1]
    ssm_w = n - 3 * att_w
    ts = min(PROJ_ROWS, seq)
    row = lambda b, i: (b, i, 0)
    per_b = lambda b, i: (b, 0, 0)
    kern = functools.partial(_inproj_kernel, att_w=att_w, q_scale=HEAD_DIM ** -0.5 * LOG2_E)
    return pl.pallas_call(
        kern,
        grid=(bsz, seq // ts),
        in_specs=[pl.BlockSpec((1, ts, d), row),
                  pl.BlockSpec((1, 1, d), per_b),
                  pl.BlockSpec((1, 1, d), per_b),
                  pl.BlockSpec((d, n), lambda b, i: (0, 0))],
        out_specs=[pl.BlockSpec((1, ts, att_w), row)] * 3 + [pl.BlockSpec((1, ts, ssm_w), row)],
        out_shape=[jax.ShapeDtypeStruct((bsz, seq, att_w), F32)] * 3
                  + [jax.ShapeDtypeStruct((bsz, seq, ssm_w), F32)],
        compiler_params=_cparams(("parallel", "parallel")),
        name="inproj",
    )(x, scale.reshape(bsz, 1, d), shift.reshape(bsz, 1, d), w_in.astype(BF16))


def _t5_bucket_np(dist):
    exact = N_BUCKETS // 2
    large = exact + (np.log(np.maximum(dist, 1).astype(np.float64) / exact)
                     / math.log(MAX_DISTANCE / exact) * (N_BUCKETS - exact)).astype(np.int64)
    return np.where(dist < exact, dist, np.minimum(large, N_BUCKETS - 1))


def _att_tables(rel_bias):
    qi = np.arange(ATT_BLOCK)[:, None]
    ki = np.arange(2 * ATT_BLOCK)[None, :]
    rel = qi + ATT_BLOCK - ki
    tabs = []
    for window, dil in PATTERNS:
        band = (rel >= 0) & (rel <= window // dil)
        bucket = _t5_bucket_np(np.maximum(rel, 0) * dil)
        onehot = (bucket[:, :, None] == np.arange(N_BUCKETS)[None, None, :]).astype(np.float32)
        bias = jnp.einsum('qkb,bh->hqk', onehot, rel_bias.astype(F32),
                          precision=lax.Precision.HIGHEST)
        full = jnp.where(band[None], bias * LOG2_E, NEG_INF)
        first = jnp.concatenate([full[:, :, ATT_BLOCK:], jnp.full_like(full[:, :, ATT_BLOCK:], NEG_INF)], axis=-1)
        tabs.append(jnp.stack([full, first]))
    return jnp.stack(tabs)


def _att_kernel(q_ref, k_ref, v_ref, tab_ref, o_ref, m_ref, l_ref, *, seq):
    lane = lax.broadcasted_iota(jnp.int32, (ATT_BLOCK, LANES), 1)
    head0 = lane < HEAD_DIM

    def rows(ref, start, n, dil):
        if dil == 1:
            return ref[0, pl.ds(pl.multiple_of(start, ATT_BLOCK), n), :]
        return ref[0, pl.ds(start, n, stride=dil), :]

    def lane_mask(n_rows, h):
        m = lax.broadcasted_iota(jnp.int32, (n_rows, LANES), 1) < HEAD_DIM
        return m if h == 0 else ~m

    def batch(pi, dil, nb, res, blk0, head, init):
        q_rows = nb * ATT_BLOCK
        kv_blocks = nb if head else nb + 1
        q_start = res + dil * ATT_BLOCK * blk0
        kv_start = q_start if head else q_start - dil * ATT_BLOCK
        if dil == 1:
            sl = pl.ds(pl.multiple_of(q_start, ATT_BLOCK), q_rows)
        else:
            sl = pl.ds(q_start, q_rows, stride=dil)
        q = rows(q_ref, q_start, q_rows, dil)
        k = rows(k_ref, kv_start, kv_blocks * ATT_BLOCK, dil).astype(BF16)
        v = rows(v_ref, kv_start, kv_blocks * ATT_BLOCK, dil)
        if not init:
            m_old, l_old, o_old = m_ref[sl, :], l_ref[sl, :], o_ref[0, sl, :]
        qh = [jnp.where(lane_mask(q_rows, h), q, 0.0).astype(BF16) for h in range(2)]
        vh = [jnp.where(lane_mask(kv_blocks * ATT_BLOCK, h), v, 1.0).astype(BF16) for h in range(2)]
        o_parts, l_parts, m_parts = [], [], []
        for j in range(nb):
            first = 1 if (head and j == 0) else 0
            kb = j if (not head or j == 0) else j - 1
            ksl = slice(kb * ATT_BLOCK, (kb + 2) * ATT_BLOCK)
            qsl = slice(j * ATT_BLOCK, (j + 1) * ATT_BLOCK)
            outs, ms = [], []
            for h in range(2):
                s = _dot_nt(qh[h][qsl], k[ksl]) + tab_ref[pi, first, h]
                m = jnp.max(s, axis=1, keepdims=True)
                p = jnp.exp2(s - m).astype(BF16)
                outs.append(_dot(p, vh[h][ksl]))
                ms.append(m)
            o_parts.append(jnp.where(head0, outs[0], outs[1]))
            l_parts.append(pltpu.roll(jnp.where(head0, outs[1], outs[0]), HEAD_DIM, axis=1))
            m_parts.append(jnp.where(head0, ms[0], ms[1]))
        o = jnp.concatenate(o_parts, axis=0)
        l = jnp.concatenate(l_parts, axis=0)
        m = jnp.concatenate(m_parts, axis=0)
        if not init:
            m_new = jnp.maximum(m_old, m)
            a_old = jnp.exp2(m_old - m_new)
            a_cur = jnp.exp2(m - m_new)
            o = o_old * a_old + o * a_cur
            l = l_old * a_old + l * a_cur
            m = m_new
        o_ref[0, sl, :] = o
        m_ref[sl, :] = m
        l_ref[sl, :] = l

    n_units = seq // ATT_BLOCK
    order = sorted(range(len(PATTERNS)), key=lambda p: -PATTERNS[p][1])
    for pos, pi in enumerate(order):
        dil = PATTERNS[pi][1]
        log_d = dil.bit_length() - 1
        res_blocks = seq // (dil * ATT_BLOCK)
        nb = min(ATT_UNITS_PER_STEP, res_blocks)
        per_res = res_blocks // nb
        init = pos == 0

        def head_body(res, carry, pi=pi, dil=dil, nb=nb, init=init):
            batch(pi, dil, nb, res, 0, True, init)
            return carry
        lax.fori_loop(0, dil, head_body, 0)

        if per_res > 1:
            def tail_body(i, carry, pi=pi, dil=dil, nb=nb, log_d=log_d, init=init):
                res = i & (dil - 1)
                blk0 = nb * (1 + (i >> log_d))
                batch(pi, dil, nb, res, blk0, False, init)
                return carry
            lax.fori_loop(0, dil * (per_res - 1), tail_body, 0)

    def finish(i, carry):
        sl = pl.ds(pl.multiple_of(i * ATT_BLOCK, ATT_BLOCK), ATT_BLOCK)
        o_ref[0, sl, :] = o_ref[0, sl, :] / l_ref[sl, :]
        return carry
    lax.fori_loop(0, n_units, finish, 0)


def _attention(q, k, v, tables):
    bsz, seq, att_w = q.shape
    n_pairs = att_w // LANES
    blk = lambda b, p: (b, 0, p)
    qkv_spec = pl.BlockSpec((1, seq, LANES), blk)
    n_pat = tables.shape[0]
    tab_spec = pl.BlockSpec((n_pat, 2, 2, ATT_BLOCK, 2 * ATT_BLOCK), lambda b, p: (0, 0, p, 0, 0))
    return pl.pallas_call(
        functools.partial(_att_kernel, seq=seq),
        grid=(bsz, n_pairs),
        in_specs=[qkv_spec, qkv_spec, qkv_spec, tab_spec],
        out_specs=pl.BlockSpec((1, seq, LANES), blk),
        out_shape=jax.ShapeDtypeStruct((bsz, seq, att_w), F32),
        scratch_shapes=[pltpu.VMEM((seq, LANES), F32), pltpu.VMEM((seq, LANES), F32)],
        compiler_params=_cparams(("parallel", "parallel")),
        name="dilated_attention",
    )(q, k, v, tables)


def _ssm_params(a_re, a_im, b_re, b_im, c_re, c_im, d_skip, log_dt):
    n_g, n_p = a_re.shape
    n_c = b_re.shape[-1]
    n_state = n_g * n_p
    dt = jnp.exp(log_dt.astype(F32))[:, None]

    def a_pow(kk):
        mag = jnp.exp(kk * dt * a_re)
        ph = kk * dt * a_im
        return mag * jnp.cos(ph), mag * jnp.sin(ph)

    ab_re, ab_im = a_pow(1.0)
    nr, ni = ab_re - 1.0, ab_im
    den = a_re * a_re + a_im * a_im
    f_re = (nr * a_re + ni * a_im) / den
    f_im = (ni * a_re - nr * a_im) / den
    bb_re = f_re[:, :, None] * b_re - f_im[:, :, None] * b_im
    bb_im = f_re[:, :, None] * b_im + f_im[:, :, None] * b_re
    eye = jnp.eye(n_g, dtype=F32)
    w_re = jnp.einsum('gpc,gh->gchp', bb_re, eye).reshape(n_g * n_c, n_state)
    w_im = jnp.einsum('gpc,gh->gchp', bb_im, eye).reshape(n_g * n_c, n_state)
    o_re = jnp.einsum('gcp,gh->gphc', c_re, eye).reshape(n_state, n_g * n_c)
    o_im = jnp.einsum('gcp,gh->gphc', c_im, eye).reshape(n_state, n_g * n_c)
    n_kb = (n_g * n_c) // LANES
    sp = n_state // n_kb
    w_in = jnp.stack([jnp.concatenate([w_re[kb * LANES:(kb + 1) * LANES, kb * sp:(kb + 1) * sp],
                                       w_im[kb * LANES:(kb + 1) * LANES, kb * sp:(kb + 1) * sp]], axis=1)
                      for kb in range(n_kb)])
    w_out = jnp.stack([jnp.concatenate([o_re[kb * sp:(kb + 1) * sp, kb * LANES:(kb + 1) * LANES],
                                        -o_im[kb * sp:(kb + 1) * sp, kb * LANES:(kb + 1) * LANES]], axis=0)
                       for kb in range(n_kb)])
    j = jnp.arange(SUBLANES, dtype=F32)[:, None]
    flat = lambda t: jnp.broadcast_to(t.reshape(1, n_state), (SUBLANES, n_state))
    coef = []
    for sh in (1, 2, 4):
        pr, pim = a_pow(float(sh))
        keep = (j >= sh).astype(F32)
        coef += [flat(pr) * keep, flat(pim) * keep]
    a_re_f = jnp.broadcast_to(a_re.reshape(1, n_state), (SUBLANES, n_state))
    a_im_f = jnp.broadcast_to(a_im.reshape(1, n_state), (SUBLANES, n_state))
    dt_f = jnp.broadcast_to(jnp.repeat(dt[:, 0], n_p).reshape(1, n_state), (SUBLANES, n_state))
    mag = jnp.exp((j + 1.0) * dt_f * a_re_f)
    ph = (j + 1.0) * dt_f * a_im_f
    coef += [mag * jnp.cos(ph), mag * jnp.sin(ph)]
    coef = jnp.stack(coef)
    return w_in.astype(BF16), w_out.astype(BF16), coef, d_skip.reshape(1, n_g * n_c).astype(F32)


def _gelu_tanh(x):
    return 0.5 * x * (1.0 + jnp.tanh(math.sqrt(2.0 / math.pi) * (x + 0.044715 * (x * x * x))))


def _ssm_kernel(u_ref, win_ref, wout_ref, coef_ref, d_ref, wglu_ref, bglu_ref, o_ref,
                hre_ref, him_ref, cre_ref, cim_ref, *, rows, n_state, col_w):
    n_kb = win_ref.shape[0]
    sp = n_state // n_kb

    @pl.when(pl.program_id(1) == 0)
    def _():
        cre_ref[...] = jnp.zeros_like(cre_ref)
        cim_ref[...] = jnp.zeros_like(cim_ref)

    u = u_ref[0]
    ub = u.astype(BF16)
    for kb in range(n_kb):
        bu = _dot(ub[:, kb * LANES:(kb + 1) * LANES], win_ref[kb])
        hre_ref[:, kb * sp:(kb + 1) * sp] = bu[:, :sp]
        him_ref[:, kb * sp:(kb + 1) * sp] = bu[:, sp:]

    n_slab = rows // SUBLANES
    for c0 in range(0, n_state, col_w):
        cs = slice(c0, c0 + col_w)

        def body(i, carry, cs=cs):
            c_re, c_im = carry
            sl = pl.ds(pl.multiple_of(i * SUBLANES, SUBLANES), SUBLANES)
            x_re = hre_ref[sl, cs]
            x_im = him_ref[sl, cs]
            for si, sh in enumerate((1, 2, 4)):
                p_re = coef_ref[2 * si, :, cs]
                p_im = coef_ref[2 * si + 1, :, cs]
                s_re = pltpu.roll(x_re, sh, axis=0)
                s_im = pltpu.roll(x_im, sh, axis=0)
                x_re, x_im = (x_re + p_re * s_re - p_im * s_im,
                              x_im + p_re * s_im + p_im * s_re)
            p_re = coef_ref[6, :, cs]
            p_im = coef_ref[7, :, cs]
            h_re = x_re + p_re * c_re - p_im * c_im
            h_im = x_im + p_re * c_im + p_im * c_re
            hre_ref[sl, cs] = h_re
            him_ref[sl, cs] = h_im
            last = slice(SUBLANES - 1, SUBLANES)
            return (jnp.broadcast_to(h_re[last, :], h_re.shape),
                    jnp.broadcast_to(h_im[last, :], h_im.shape))

        c_re, c_im = lax.fori_loop(0, n_slab, body, (cre_ref[:, cs], cim_ref[:, cs]), unroll=True)
        cre_ref[:, cs] = c_re
        cim_ref[:, cs] = c_im

    ys = []
    for kb in range(n_kb):
        hcat = jnp.concatenate([hre_ref[:, kb * sp:(kb + 1) * sp].astype(BF16),
                                him_ref[:, kb * sp:(kb + 1) * sp].astype(BF16)], axis=1)
        ys.append(_dot(hcat, wout_ref[kb]))
    y = jnp.concatenate(ys, axis=1) + d_ref[...] * u
    y = _gelu_tanh(y)
    z = _dot(y.astype(BF16), wglu_ref[...]) + bglu_ref[...]
    o_ref[0] = y * jax.nn.sigmoid(z)


def _ssm(u, w_in, w_out, coef, d_flat, w_glu, b_glu):
    bsz, seq, ssm_w = u.shape
    n_state = coef.shape[-1]
    rows = min(SSM_ROWS, seq)
    full = lambda *shape: pl.BlockSpec(shape, lambda b, i: (0,) * len(shape))
    kern = functools.partial(_ssm_kernel, rows=rows, n_state=n_state, col_w=4 * LANES)
    return pl.pallas_call(
        kern,
        grid=(bsz, seq // rows),
        in_specs=[pl.BlockSpec((1, rows, ssm_w), lambda b, i: (b, i, 0)),
                  full(*w_in.shape), full(*w_out.shape), full(*coef.shape), full(1, ssm_w),
                  full(ssm_w, ssm_w), full(1, ssm_w)],
        out_specs=pl.BlockSpec((1, rows, ssm_w), lambda b, i: (b, i, 0)),
        out_shape=jax.ShapeDtypeStruct((bsz, seq, ssm_w), F32),
        scratch_shapes=[pltpu.VMEM((rows, n_state), F32), pltpu.VMEM((rows, n_state), F32),
                        pltpu.VMEM((SUBLANES, n_state), F32), pltpu.VMEM((SUBLANES, n_state), F32)],
        compiler_params=_cparams(("parallel", "arbitrary")),
        name="s5_glu",
    )(u, w_in, w_out, coef, d_flat, w_glu.astype(BF16), b_glu.reshape(1, ssm_w))


def _layer_norm(y, g, b):
    mu = jnp.mean(y, axis=-1, keepdims=True)
    yc = y - mu
    var = jnp.mean(yc * yc, axis=-1, keepdims=True)
    return yc * lax.rsqrt(var + EPS) * g + b


def _rms_norm(y, g):
    return y * lax.rsqrt(jnp.mean(y * y, axis=-1, keepdims=True) + EPS) * g


def _mix_kernel(x_ref, att_ref, ssm_ref, gatt_ref, gssm_ref, wout_ref, g1_ref, ln_g_ref, ln_b_ref,
                sc_ref, sh_ref, g2_ref, wsg_ref, wsu_ref, wsd_ref, h_ref, ht_ref, pre_ref, *, alpha, att_w):
    a_n = _rms_norm(att_ref[0], gatt_ref[...]).astype(BF16)
    s_n = _rms_norm(ssm_ref[0], gssm_ref[...]).astype(BF16)
    mix = _dot(a_n, wout_ref[0:att_w, :]) + _dot(s_n, wout_ref[att_w:, :])
    x1 = _layer_norm(alpha * x_ref[0] + g1_ref[0] * mix, ln_g_ref[...], ln_b_ref[...])
    h = x1 * (1.0 + sc_ref[0]) + sh_ref[0]
    h_ref[0] = h
    _store_token_tiles(ht_ref, h)
    hb = h.astype(BF16)
    hid = _silu(_dot(hb, wsg_ref[...])) * _dot(hb, wsu_ref[...])
    shared = _dot(hid.astype(BF16), wsd_ref[...])
    pre_ref[0] = alpha * x1 + g2_ref[0] * shared


def _mix(x, att, ssm, g_att, g_ssm, w_out, gate1, ln_g, ln_b, scale2, shift2, gate2,
         w_s_gate, w_s_up, w_s_down, alpha):
    bsz, seq, d = x.shape
    att_w = att.shape[-1]
    ssm_w = ssm.shape[-1]
    ff = w_s_gate.shape[1]
    tm = min(MIX_ROWS, seq)
    n_sub = d // LANES
    row = lambda b, i: (b, i, 0)
    per_b = lambda b, i: (b, 0, 0)
    full = lambda *shape: pl.BlockSpec(shape, lambda b, i: (0,) * len(shape))
    vec = lambda t: t.reshape(bsz, 1, d)
    return pl.pallas_call(
        functools.partial(_mix_kernel, alpha=alpha, att_w=att_w),
        grid=(bsz, seq // tm),
        in_specs=[pl.BlockSpec((1, tm, d), row), pl.BlockSpec((1, tm, att_w), row),
                  pl.BlockSpec((1, tm, ssm_w), row),
                  full(1, att_w), full(1, ssm_w), full(att_w + ssm_w, d),
                  pl.BlockSpec((1, 1, d), per_b), full(1, d), full(1, d),
                  pl.BlockSpec((1, 1, d), per_b), pl.BlockSpec((1, 1, d), per_b),
                  pl.BlockSpec((1, 1, d), per_b),
                  full(d, ff), full(d, ff), full(ff, d)],
        out_specs=[pl.BlockSpec((1, tm, d), row),
                   pl.BlockSpec((tm * n_sub, LANES), lambda b, i: (b * (seq // tm) + i, 0)),
                   pl.BlockSpec((1, tm, d), row)],
        out_shape=[jax.ShapeDtypeStruct((bsz, seq, d), F32),
                   jax.ShapeDtypeStruct((bsz * seq * n_sub, LANES), F32),
                   jax.ShapeDtypeStruct((bsz, seq, d), F32)],
        compiler_params=_cparams(("parallel", "parallel")),
        name="mix_ln1_shared",
    )(x, att, ssm, g_att.reshape(1, att_w), g_ssm.reshape(1, ssm_w), w_out.astype(BF16),
      vec(gate1), ln_g.reshape(1, d), ln_b.reshape(1, d), vec(scale2), vec(shift2), vec(gate2),
      w_s_gate.astype(BF16), w_s_up.astype(BF16), w_s_down.astype(BF16))


def _route_kernel(h_ref, wrt_ref, wrt_lo_ref, bias_ref, tri_ref, idx_ref, gate_ref, rank_ref, cnt_ref, carry_ref):
    n_e = wrt_ref.shape[0]
    tm = h_ref.shape[0]
    per_group = n_e // N_EXPERT_GROUPS
    neg = -jnp.inf

    @pl.when(pl.program_id(0) == 0)
    def _():
        carry_ref[...] = jnp.zeros_like(carry_ref)

    h_hi, h_lo = _split_bf16(h_ref[...])
    logits = (_dot_nt(wrt_ref[...], h_hi) + _dot_nt(wrt_ref[...], h_lo)
              + _dot_nt(wrt_lo_ref[...], h_hi))
    scores = jax.nn.sigmoid(logits)
    biased = scores + bias_ref[...]

    g3 = biased.reshape(N_EXPERT_GROUPS, per_group, tm)
    ridx = lax.broadcasted_iota(jnp.int32, g3.shape, 1).astype(F32)
    m1 = jnp.max(g3, axis=1, keepdims=True)
    first = jnp.min(jnp.where(g3 == m1, ridx, float(per_group)), axis=1, keepdims=True)
    m2 = jnp.max(jnp.where(ridx == first, neg, g3), axis=1, keepdims=True)
    gs = m1 + m2

    gidx = lax.broadcasted_iota(jnp.int32, gs.shape, 0).astype(F32)
    ok = jnp.zeros_like(gs)
    cur = gs
    for _ in range(TOPK_GROUPS):
        mx = jnp.max(cur, axis=0, keepdims=True)
        fi = jnp.min(jnp.where(cur == mx, gidx, float(N_EXPERT_GROUPS)), axis=0, keepdims=True)
        hit = gidx == fi
        ok = jnp.where(hit, 1.0, ok)
        cur = jnp.where(hit, neg, cur)
    masked = jnp.where(ok > 0.5, g3, neg).reshape(n_e, tm)

    eidx = lax.broadcasted_iota(jnp.int32, (n_e, tm), 0).astype(F32)
    onehot = jnp.zeros((n_e, tm), F32)
    cur = masked
    sel_idx = []
    sel_gate = []
    for _ in range(TOP_K):
        mx = jnp.max(cur, axis=0, keepdims=True)
        fi = jnp.min(jnp.where(cur == mx, eidx, float(n_e)), axis=0, keepdims=True)
        hit = eidx == fi
        sel_idx.append(fi)
        sel_gate.append(jnp.sum(jnp.where(hit, scores, 0.0), axis=0, keepdims=True))
        onehot = jnp.where(hit, 1.0, onehot)
        cur = jnp.where(hit, neg, cur)
    idx = jnp.concatenate(sel_idx, axis=0)
    gate = jnp.concatenate(sel_gate, axis=0)
    gate = gate / jnp.sum(gate, axis=0, keepdims=True) * ROUTED_SCALE

    prior = _dot(onehot.astype(BF16), tri_ref[...]) + carry_ref[:, 0:1]
    ranks = [jnp.sum(jnp.where(eidx == sel_idx[k], prior, 0.0), axis=0, keepdims=True)
             for k in range(TOP_K)]
    rank = jnp.concatenate(ranks, axis=0)
    carry = carry_ref[...] + jnp.sum(onehot, axis=1, keepdims=True)
    carry_ref[...] = carry
    cnt_ref[...] = carry

    idx_ref[...] = idx.astype(jnp.int32)
    gate_ref[...] = gate
    rank_ref[...] = rank.astype(jnp.int32)


def _route(h2d, w_router, router_bias):
    n_tok, d = h2d.shape
    n_e = w_router.shape[1]
    tm = min(ROUTE_COLS, n_tok)
    tri = (np.arange(tm)[:, None] < np.arange(tm)[None, :]).astype(np.float32)
    w_hi, w_lo = _split_bf16(w_router.T.astype(F32))
    col = lambda i: (0, i)
    full = lambda *shape: pl.BlockSpec(shape, lambda i: (0,) * len(shape))
    idx, gate, rank, cnt = pl.pallas_call(
        _route_kernel,
        grid=(n_tok // tm,),
        in_specs=[pl.BlockSpec((tm, d), lambda i: (i, 0)), full(n_e, d), full(n_e, d), full(n_e, 1),
                  full(tm, tm)],
        out_specs=[pl.BlockSpec((TOP_K, tm), col), pl.BlockSpec((TOP_K, tm), col),
                   pl.BlockSpec((TOP_K, tm), col), full(n_e, LANES)],
        out_shape=[jax.ShapeDtypeStruct((TOP_K, n_tok), jnp.int32),
                   jax.ShapeDtypeStruct((TOP_K, n_tok), F32),
                   jax.ShapeDtypeStruct((TOP_K, n_tok), jnp.int32),
                   jax.ShapeDtypeStruct((n_e, LANES), F32)],
        scratch_shapes=[pltpu.VMEM((n_e, LANES), F32)],
        compiler_params=_cparams(("arbitrary",)),
        name="router_topk",
    )(h2d, w_hi, w_lo, router_bias.reshape(n_e, 1).astype(F32), jnp.asarray(tri, BF16))
    return idx, gate, rank, cnt[:, 0].astype(jnp.int32)


def _sc_mesh():
    return plsc.VectorSubcoreMesh(core_axis_name="core", subcore_axis_name="subcore")


def _dispatch(h_tiles, gate_rows, dest_windows, n_rows):
    n_tok, n_sub, _ = h_tiles.shape
    win = DISPATCH_WINDOW

    @pl.kernel(out_type=[jax.ShapeDtypeStruct((n_rows, n_sub, LANES), h_tiles.dtype),
                         jax.ShapeDtypeStruct((n_rows, LANES), F32)],
               mesh=_sc_mesh(), scratch_types=[])
    def scatter_tiles(x_hbm, g_hbm, d_hbm, xs_hbm, rg_hbm):
        def window(x_vmem, g_vmem, d_vmem):
            for k in range(TOP_K):
                rows = d_vmem.at[0, pl.ds(k * win, win)]
                pltpu.sync_copy(x_vmem, xs_hbm.at[rows])
                pltpu.sync_copy(g_vmem.at[k], rg_hbm.at[rows])
        pltpu.emit_pipeline(
            window,
            grid=(n_tok // win,),
            in_specs=[pl.BlockSpec((win, n_sub, LANES), lambda i: (i, 0, 0)),
                      pl.BlockSpec((TOP_K, win, LANES), lambda i: (0, i, 0)),
                      pl.BlockSpec((1, TOP_K * win), lambda i: (i, 0))],
            out_specs=[],
            core_axis_name=("core", "subcore"),
            dimension_semantics=(pltpu.PARALLEL,),
        )(x_hbm, g_hbm, d_hbm)

    return scatter_tiles(h_tiles, gate_rows, dest_windows)


def _gather_sum(ys_tiles, dest_windows, n_tok):
    _, n_sub, _ = ys_tiles.shape
    win = DISPATCH_WINDOW

    @pl.kernel(out_type=jax.ShapeDtypeStruct((n_tok, n_sub, LANES), ys_tiles.dtype), mesh=_sc_mesh(),
               scratch_types=[])
    def gather_tiles(y_hbm, d_hbm, o_hbm):
        def window(d_vmem, o_vmem):
            pltpu.sync_copy(y_hbm.at[d_vmem.at[0, pl.ds(0, win)]], o_vmem)
            for k in range(1, TOP_K):
                pltpu.sync_copy(y_hbm.at[d_vmem.at[0, pl.ds(k * win, win)]], o_vmem, add=True)
        pltpu.emit_pipeline(
            window,
            grid=(n_tok // win,),
            in_specs=[pl.BlockSpec((1, TOP_K * win), lambda i: (i, 0))],
            out_specs=[pl.BlockSpec((win, n_sub, LANES), lambda i: (i, 0, 0))],
            core_axis_name=("core", "subcore"),
            dimension_semantics=(pltpu.PARALLEL,),
        )(d_hbm, o_hbm)

    return gather_tiles(ys_tiles, dest_windows)


def _expert_kernel(first_ref, nblk_ref, cnt_ref, nused_ref, xs_ref, rg_ref, wg_ref, wu_ref, wd_ref, ys_ref,
                   xbuf_ref, gbuf_ref, ybuf_ref, wgb_ref, wub_ref, wdb_ref, xsem_ref, ysem_ref):
    e = pl.program_id(0)
    n_used = nused_ref[0]
    d = wgb_ref.shape[0]
    blk_rows = xbuf_ref.shape[1]

    class _Pair:
        def __init__(self, g, slot):
            src = xs_ref.at[pl.ds(pl.multiple_of(g * blk_rows, blk_rows), blk_rows)]
            gsrc = rg_ref.at[pl.ds(pl.multiple_of(g * MOE_ROWS, MOE_ROWS), MOE_ROWS)]
            self.copies = (pltpu.make_async_copy(src, xbuf_ref.at[slot], xsem_ref.at[slot]),
                           pltpu.make_async_copy(gsrc, gbuf_ref.at[slot], xsem_ref.at[slot]))

        def start(self):
            for cp in self.copies:
                cp.start()

        def wait(self):
            for cp in self.copies:
                cp.wait()

    def x_copy(g, slot):
        return _Pair(g, slot)

    def y_copy(g, slot):
        dst = ys_ref.at[pl.ds(pl.multiple_of(g * blk_rows, blk_rows), blk_rows)]
        return pltpu.make_async_copy(ybuf_ref.at[slot], dst, ysem_ref.at[slot])

    n_slot = xbuf_ref.shape[0]

    @pl.when(e == 0)
    def _():
        for g in range(n_slot - 1):
            @pl.when(g < n_used)
            def _(g=g):
                x_copy(g, g).start()

    @pl.when(nblk_ref[e] > 0)
    def _():
        wgb_ref[...] = wg_ref[0].astype(BF16)
        wub_ref[...] = wu_ref[0].astype(BF16)
        wdb_ref[...] = wd_ref[0].astype(BF16)

    def block(j, carry):
        g = first_ref[e] + j
        slot = g & (n_slot - 1)
        x_copy(g, slot).wait()

        @pl.when(g + n_slot - 1 < n_used)
        def _():
            x_copy(g + n_slot - 1, (g + n_slot - 1) & (n_slot - 1)).start()

        @pl.when(g >= n_slot)
        def _():
            y_copy(g - n_slot, slot).wait()

        x = jnp.concatenate(_load_token_tiles(xbuf_ref.at[slot], MOE_ROWS, d), axis=1)
        row = lax.broadcasted_iota(jnp.int32, (MOE_ROWS, 1), 0)
        xb = jnp.where(row < cnt_ref[e] - j * MOE_ROWS, x, 0.0).astype(BF16)
        hid = _silu(_dot(xb, wgb_ref[...])) * _dot(xb, wub_ref[...])
        y = _dot(hid.astype(BF16), wdb_ref[...]) * gbuf_ref[slot][:, 0:1]
        _store_token_tiles(ybuf_ref.at[slot], jnp.where(row < cnt_ref[e] - j * MOE_ROWS, y, 0.0))
        y_copy(g, slot).start()
        return carry
    lax.fori_loop(0, nblk_ref[e], block, 0)

    @pl.when(e == pl.num_programs(0) - 1)
    def _():
        for back in range(1, n_slot + 1):
            @pl.when(n_used >= back)
            def _(back=back):
                y_copy(n_used - back, (n_used - back) & (n_slot - 1)).wait()


def _experts(xs_tiles, row_gates, plan, w_gate, w_up, w_down):
    n_rows, n_sub, _ = xs_tiles.shape
    d = n_sub * LANES
    n_e, _, ff = w_gate.shape
    blk_rows = MOE_ROWS * n_sub
    hbm = pl.BlockSpec(memory_space=pl.ANY)
    grid_spec = pltpu.PrefetchScalarGridSpec(
        num_scalar_prefetch=4,
        grid=(n_e,),
        in_specs=[hbm, hbm,
                  pl.BlockSpec((1, d, ff), lambda e, *_: (e, 0, 0)),
                  pl.BlockSpec((1, d, ff), lambda e, *_: (e, 0, 0)),
                  pl.BlockSpec((1, ff, d), lambda e, *_: (e, 0, 0))],
        out_specs=hbm,
        scratch_shapes=[pltpu.VMEM((EXPERT_SLOTS, blk_rows, LANES), F32),
                        pltpu.VMEM((EXPERT_SLOTS, MOE_ROWS, LANES), F32),
                        pltpu.VMEM((EXPERT_SLOTS, blk_rows, LANES), F32),
                        pltpu.VMEM((d, ff), BF16), pltpu.VMEM((d, ff), BF16), pltpu.VMEM((ff, d), BF16),
                        pltpu.SemaphoreType.DMA((EXPERT_SLOTS,)), pltpu.SemaphoreType.DMA((EXPERT_SLOTS,))],
    )
    ys = pl.pallas_call(
        _expert_kernel,
        grid_spec=grid_spec,
        out_shape=jax.ShapeDtypeStruct((n_rows * n_sub, LANES), F32),
        compiler_params=_cparams(("arbitrary",)),
        name="moe_experts",
    )(*plan, xs_tiles.reshape(n_rows * n_sub, LANES), row_gates, w_gate, w_up, w_down)
    return ys.reshape(n_rows, n_sub, LANES)


def _combine_kernel(dest_ref, ys_ref, pre_ref, g2_ref, ln_g_ref, ln_b_ref, o_ref, buf_ref, sem_ref):
    tm, d = pre_ref.shape[1], pre_ref.shape[2]
    n_sub = d // LANES

    def row_copy(t, k, src_row):
        slot = pl.multiple_of((k * tm + t) * n_sub, n_sub)
        return pltpu.make_async_copy(ys_ref.at[src_row], buf_ref.at[pl.ds(slot, n_sub)], sem_ref.at[0])

    def issue(g, c):
        t0 = g * DMA_ISSUE_UNROLL
        rows = [[dest_ref[(t0 + j) * TOP_K + k] for k in range(TOP_K)] for j in range(DMA_ISSUE_UNROLL)]
        for j in range(DMA_ISSUE_UNROLL):
            for k in range(TOP_K):
                row_copy(t0 + j, k, rows[j][k]).start(priority=k % 2)
        return c
    lax.fori_loop(0, tm // DMA_ISSUE_UNROLL, issue, 0)

    def drain(t, c):
        for k in range(TOP_K):
            row_copy(t, k, dest_ref[t * TOP_K + k]).wait()
        return c
    lax.fori_loop(0, tm, drain, 0)

    pieces = []
    for s in range(n_sub):
        acc = None
        for k in range(TOP_K):
            term = buf_ref[pl.ds(k * tm * n_sub + s, tm, stride=n_sub), :]
            acc = term if acc is None else acc + term
        pieces.append(acc)
    routed = jnp.concatenate(pieces, axis=1)
    y = pre_ref[0] + g2_ref[0] * routed
    o_ref[0] = _layer_norm(y, ln_g_ref[...], ln_b_ref[...])


def _combine(ys, dest, pre, gate2, ln_g, ln_b, n_tiles):
    bsz, seq, d = pre.shape
    tm = min(DISPATCH_ROWS, seq)
    per_seq = seq // tm
    row = lambda i, *_: (i // per_seq, i % per_seq, 0)
    per_b = lambda i, *_: (i // per_seq, 0, 0)
    full = lambda *shape: pl.BlockSpec(shape, lambda i, *_: (0,) * len(shape))
    grid_spec = pltpu.PrefetchScalarGridSpec(
        num_scalar_prefetch=0,
        grid=(n_tiles,),
        in_specs=[pl.BlockSpec((TOP_K * tm,), lambda i: (i,), memory_space=pltpu.SMEM),
                  pl.BlockSpec(memory_space=pl.ANY),
                  pl.BlockSpec((1, tm, d), row), pl.BlockSpec((1, 1, d), per_b), full(1, d), full(1, d)],
        out_specs=pl.BlockSpec((1, tm, d), row),
        scratch_shapes=[pltpu.VMEM((TOP_K * tm * (d // LANES), LANES), F32), pltpu.SemaphoreType.DMA((1,))],
    )
    return pl.pallas_call(
        _combine_kernel,
        grid_spec=grid_spec,
        out_shape=jax.ShapeDtypeStruct((bsz, seq, d), F32),
        compiler_params=_cparams(("arbitrary",)),
        name="moe_combine_ln2",
    )(dest, ys, pre, gate2.reshape(bsz, 1, d), ln_g.reshape(1, d), ln_b.reshape(1, d))


def _final_ln_kernel(r_ref, pre_ref, g2_ref, ln_g_ref, ln_b_ref, *rest):
    o_ref = rest[-1]
    tm, d = pre_ref.shape[1], pre_ref.shape[2]
    routed = jnp.concatenate(_load_token_tiles(r_ref, tm, d), axis=1)
    y = pre_ref[0] + g2_ref[0] * routed
    o_ref[0] = _layer_norm(y, ln_g_ref[...], ln_b_ref[...])


def _final_ln(routed_tiles, pre, gate2, ln_g, ln_b, first_tile, partial=None):
    bsz, seq, d = pre.shape
    tm = min(DISPATCH_ROWS, seq)
    per_seq = seq // tm
    n_sub = d // LANES
    n_tiles = bsz * per_seq - first_tile
    row = lambda i: ((i + first_tile) // per_seq, (i + first_tile) % per_seq, 0)
    per_b = lambda i: ((i + first_tile) // per_seq, 0, 0)
    full = lambda *shape: pl.BlockSpec(shape, lambda i: (0,) * len(shape))
    in_specs = [pl.BlockSpec((tm * n_sub, LANES), lambda i: (i, 0)),
                pl.BlockSpec((1, tm, d), row), pl.BlockSpec((1, 1, d), per_b), full(1, d), full(1, d)]
    args = [routed_tiles.reshape(-1, LANES), pre, gate2.reshape(bsz, 1, d), ln_g.reshape(1, d),
            ln_b.reshape(1, d)]
    aliases = {}
    if partial is not None:
        in_specs.append(pl.BlockSpec(memory_space=pl.ANY))
        args.append(partial)
        aliases = {len(args) - 1: 0}
    return pl.pallas_call(
        _final_ln_kernel,
        grid=(n_tiles,),
        in_specs=in_specs,
        out_specs=pl.BlockSpec((1, tm, d), row),
        out_shape=jax.ShapeDtypeStruct((bsz, seq, d), F32),
        input_output_aliases=aliases,
        compiler_params=_cparams(("arbitrary",)),
        name="moe_final_ln2",
    )(*args)


def _dest_kernel(pstart_ref, idx_ref, rank_ref, dest_ref):
    idx = idx_ref[...]

    def body(e, acc):
        return acc + jnp.where(idx == e, pstart_ref[e], 0)
    dest_ref[...] = lax.fori_loop(0, pstart_ref.shape[0], body, rank_ref[...], unroll=8)


def _dest_rows(pstart, idx, rank):
    n_k, n_tok = idx.shape
    tn = min(4096, n_tok)
    grid_spec = pltpu.PrefetchScalarGridSpec(
        num_scalar_prefetch=1,
        grid=(n_tok // tn,),
        in_specs=[pl.BlockSpec((n_k, tn), lambda i, *_: (0, i))] * 2,
        out_specs=pl.BlockSpec((n_k, tn), lambda i, *_: (0, i)),
    )
    return pl.pallas_call(
        _dest_kernel,
        grid_spec=grid_spec,
        out_shape=jax.ShapeDtypeStruct((n_k, n_tok), jnp.int32),
        compiler_params=_cparams(("arbitrary",)),
        name="moe_dest",
    )(pstart, idx, rank)


def _moe_plan(idx, rank, cnt, n_tok):
    n_e = cnt.shape[0]
    padded = (cnt + MOE_ROWS - 1) // MOE_ROWS * MOE_ROWS
    pend = jnp.cumsum(padded).astype(jnp.int32)
    pstart = pend - padded
    dest_kt = _dest_rows(pstart, idx, rank)
    dest = dest_kt.T.reshape(-1)
    win = min(DISPATCH_WINDOW, n_tok)
    dest_windows = dest_kt.reshape(TOP_K, n_tok // win, win).transpose(1, 0, 2).reshape(n_tok // win, TOP_K * win)
    n_blk = n_tok * TOP_K // MOE_ROWS + n_e
    n_used = (pend[-1:] // MOE_ROWS).astype(jnp.int32)
    plan = ((pstart // MOE_ROWS).astype(jnp.int32), (padded // MOE_ROWS).astype(jnp.int32),
            cnt.astype(jnp.int32), n_used)
    return dest, dest_windows, plan, n_blk * MOE_ROWS


def kernel(x, c, rel_bias, w_ada, b_ada, w_in, ssm_a_re, ssm_a_im, ssm_b_re, ssm_b_im, ssm_c_re, ssm_c_im, ssm_d, ssm_log_dt, w_glu, b_glu, g_att, g_ssm, w_out, ln1_g, ln1_b, w_router, router_bias, w_e_gate, w_e_up, w_e_down, w_s_gate, w_s_up, w_s_down, ln2_g, ln2_b):
    bsz, seq, d = x.shape
    depth = w_ada.shape[0]
    alpha = (2 * depth) ** 0.25
    att_w = g_att.shape[-1]
    tables = _att_tables(rel_bias)
    for layer in range(depth):
        ada = _ada(c, w_ada[layer], b_ada[layer])
        shift1, scale1, gate1, shift2, scale2, gate2 = jnp.split(ada, 6, axis=-1)
        q, k, v, u = _inproj(x, scale1, shift1, w_in[layer], att_w)
        att = _attention(q, k, v, tables)
        ssm_prm = _ssm_params(ssm_a_re[layer], ssm_a_im[layer], ssm_b_re[layer], ssm_b_im[layer],
                              ssm_c_re[layer], ssm_c_im[layer], ssm_d[layer], ssm_log_dt[layer])
        ssm = _ssm(u, *ssm_prm, w_glu[layer], b_glu[layer])
        h2, h2_tiles, pre = _mix(x, att, ssm, g_att[layer], g_ssm[layer], w_out[layer], gate1,
                                 ln1_g[layer], ln1_b[layer], scale2, shift2, gate2, w_s_gate[layer],
                                 w_s_up[layer], w_s_down[layer], alpha)
        h2d = h2.reshape(bsz * seq, d)
        idx, gate, rank, cnt = _route(h2d, w_router[layer], router_bias[layer])
        dest, dest_windows, plan, n_rows = _moe_plan(idx, rank, cnt, bsz * seq)
        n_tok = bsz * seq
        gate_rows = jnp.broadcast_to(gate[:, :, None], (TOP_K, n_tok, LANES))
        xs, row_gates = _dispatch(h2_tiles.reshape(n_tok, d // LANES, LANES), gate_rows, dest_windows, n_rows)
        ys = _experts(xs, row_gates, plan, w_e_gate[layer], w_e_up[layer], w_e_down[layer])
        tm = min(DISPATCH_ROWS, seq)
        tc_tiles = min(COMBINE_TC_TILES, n_tok // tm)
        tc_tokens = tc_tiles * tm
        win = min(DISPATCH_WINDOW, n_tok)
        partial = (_combine(ys, dest, pre, gate2, ln2_g[layer], ln2_b[layer], tc_tiles)
                   if tc_tiles > 0 else None)
        routed = _gather_sum(ys, dest_windows[tc_tokens // win:], n_tok - tc_tokens)
        x = _final_ln(routed, pre, gate2, ln2_g[layer], ln2_b[layer], tc_tiles, partial)
    return x
```

```python
import functools
import math

import jax
import jax.numpy as jnp
import numpy as np
from jax import lax
from jax.experimental import pallas as pl
from jax.experimental.pallas import tpu as pltpu
from jax.experimental.pallas import tpu_sc as plsc

F32 = jnp.float32
BF16 = jnp.bfloat16

HEAD_DIM = 64
ATT_BLOCK = 128
PATTERNS = ((128, 1), (512, 4), (2048, 16))
N_BUCKETS = 32
MAX_DISTANCE = 2048
SSM_GROUP = 16
SSM_STATE = 64
N_EXPERTS = 256
TOP_K = 8
N_EXPERT_GROUPS = 8
TOPK_GROUPS = 4
ROUTED_SCALE = 2.5
EPS = 1e-5
NEG_INF = -1e30
LOG2_E = math.log2(math.e)

LANES = 128
SUBLANES = 8
VMEM_LIMIT_BYTES = 56 * 1024 * 1024

PROJ_ROWS = 512
SSM_ROWS = 256
MIX_ROWS = 512
ROUTE_COLS = 512
MOE_ROWS = 256
DISPATCH_ROWS = 256
COMBINE_TC_TILES = 52
GATE_LANES = LANES
DISPATCH_WINDOW = 16
ATT_UNITS_PER_STEP = 8
DMA_ISSUE_UNROLL = 4
EXPERT_SLOTS = 4


def _cparams(sem, vmem=VMEM_LIMIT_BYTES):
    return pltpu.CompilerParams(dimension_semantics=sem, vmem_limit_bytes=vmem)


def _dot(a, b):
    return jnp.dot(a, b, preferred_element_type=F32)


def _dot_nt(a, b):
    return lax.dot_general(a, b, (((1,), (1,)), ((), ())), preferred_element_type=F32)


def _silu(x):
    return x * jax.nn.sigmoid(x)


def _store_token_tiles(ref, x):
    m, d = x.shape
    n_sub = d // LANES
    for s in range(n_sub):
        ref[pl.ds(s, m, stride=n_sub), :] = x[:, s * LANES:(s + 1) * LANES]


def _load_token_tiles(ref, m, d, row0=0):
    n_sub = d // LANES
    return [ref[pl.ds(row0 + s, m, stride=n_sub), :] for s in range(n_sub)]


def _split_bf16(x):
    hi = x.astype(BF16)
    lo = (x - hi.astype(F32)).astype(BF16)
    return hi, lo


def _ada_kernel(c_ref, w_ref, b_ref, o_ref):
    c = c_ref[...]
    a_hi, a_lo = _split_bf16(_silu(c))
    w_hi, w_lo = _split_bf16(w_ref[...])
    acc = _dot(a_hi, w_hi) + _dot(a_hi, w_lo) + _dot(a_lo, w_hi)
    o_ref[...] = acc + b_ref[...]


def _ada(c, w_ada, b_ada):
    bsz, d = c.shape
    n = w_ada.shape[1]
    rows = SUBLANES
    c_pad = jnp.zeros((rows, d), F32).at[:bsz].set(c)
    tn = 1024
    out = pl.pallas_call(
        _ada_kernel,
        grid=(n // tn,),
        in_specs=[pl.BlockSpec((rows, d), lambda j: (0, 0)),
                  pl.BlockSpec((d, tn), lambda j: (0, j)),
                  pl.BlockSpec((1, tn), lambda j: (0, j))],
        out_specs=pl.BlockSpec((rows, tn), lambda j: (0, j)),
        out_shape=jax.ShapeDtypeStruct((rows, n), F32),
        compiler_params=_cparams(("parallel",)),
        name="ada",
    )(c_pad, w_ada, b_ada.reshape(1, n))
    return out[:bsz]


def _inproj_kernel(x_ref, sc_ref, sh_ref, w_ref, q_ref, k_ref, v_ref, u_ref, *, att_w, q_scale):
    h = (x_ref[0] * (1.0 + sc_ref[0]) + sh_ref[0]).astype(BF16)
    q_ref[0] = _dot(h, w_ref[:, 0:att_w]) * q_scale
    k_ref[0] = _dot(h, w_ref[:, att_w:2 * att_w])
    v_ref[0] = _dot(h, w_ref[:, 2 * att_w:3 * att_w])
    u_ref[0] = _dot(h, w_ref[:, 3 * att_w:])


def _inproj(x, scale, shift, w_in, att_w):
    bsz, seq, d = x.shape
    n = w_in.shape[1]
    ssm_w = n - 3 * att_w
    ts = min(PROJ_ROWS, seq)
    row = lambda b, i: (b, i, 0)
    per_b = lambda b, i: (b, 0, 0)
    kern = functools.partial(_inproj_kernel, att_w=att_w, q_scale=HEAD_DIM ** -0.5 * LOG2_E)
    return pl.pallas_call(
        kern,
        grid=(bsz, seq // ts),
        in_specs=[pl.BlockSpec((1, ts, d), row),
                  pl.BlockSpec((1, 1, d), per_b),
                  pl.BlockSpec((1, 1, d), per_b),
                  pl.BlockSpec((d, n), lambda b, i: (0, 0))],
        out_specs=[pl.BlockSpec((1, ts, att_w), row)] * 3 + [pl.BlockSpec((1, ts, ssm_w), row)],
        out_shape=[jax.ShapeDtypeStruct((bsz, seq, att_w), F32)] * 3
                  + [jax.ShapeDtypeStruct((bsz, seq, ssm_w), F32)],
        compiler_params=_cparams(("parallel", "parallel")),
        name="inproj",
    )(x, scale.reshape(bsz, 1, d), shift.reshape(bsz, 1, d), w_in.astype(BF16))


def _t5_bucket_np(dist):
    exact = N_BUCKETS // 2
    large = exact + (np.log(np.maximum(dist, 1).astype(np.float64) / exact)
                     / math.log(MAX_DISTANCE / exact) * (N_BUCKETS - exact)).astype(np.int64)
    return np.where(dist < exact, dist, np.minimum(large, N_BUCKETS - 1))


def _att_tables(rel_bias):
    qi = np.arange(ATT_BLOCK)[:, None]
    ki = np.arange(2 * ATT_BLOCK)[None, :]
    rel = qi + ATT_BLOCK - ki
    tabs = []
    for window, dil in PATTERNS:
        band = (rel >= 0) & (rel <= window // dil)
        bucket = _t5_bucket_np(np.maximum(rel, 0) * dil)
        onehot = (bucket[:, :, None] == np.arange(N_BUCKETS)[None, None, :]).astype(np.float32)
        bias = jnp.einsum('qkb,bh->hqk', onehot, rel_bias.astype(F32),
                          precision=lax.Precision.HIGHEST)
        full = jnp.where(band[None], bias * LOG2_E, NEG_INF)
        first = jnp.concatenate([full[:, :, ATT_BLOCK:], jnp.full_like(full[:, :, ATT_BLOCK:], NEG_INF)], axis=-1)
        tabs.append(jnp.stack([full, first]))
    return jnp.stack(tabs)


def _att_kernel(q_ref, k_ref, v_ref, tab_ref, o_ref, m_ref, l_ref, *, seq):
    lane = lax.broadcasted_iota(jnp.int32, (ATT_BLOCK, LANES), 1)
    head0 = lane < HEAD_DIM

    def rows(ref, start, n, dil):
        if dil == 1:
            return ref[0, pl.ds(pl.multiple_of(start, ATT_BLOCK), n), :]
        return ref[0, pl.ds(start, n, stride=dil), :]

    def lane_mask(n_rows, h):
        m = lax.broadcasted_iota(jnp.int32, (n_rows, LANES), 1) < HEAD_DIM
        return m if h == 0 else ~m

    def batch(pi, dil, nb, res, blk0, head, init):
        q_rows = nb * ATT_BLOCK
        kv_blocks = nb if head else nb + 1
        q_start = res + dil * ATT_BLOCK * blk0
        kv_start = q_start if head else q_start - dil * ATT_BLOCK
        if dil == 1:
            sl = pl.ds(pl.multiple_of(q_start, ATT_BLOCK), q_rows)
        else:
            sl = pl.ds(q_start, q_rows, stride=dil)
        q = rows(q_ref, q_start, q_rows, dil)
        k = rows(k_ref, kv_start, kv_blocks * ATT_BLOCK, dil).astype(BF16)
        v = rows(v_ref, kv_start, kv_blocks * ATT_BLOCK, dil)
        if not init:
            m_old, l_old, o_old = m_ref[sl, :], l_ref[sl, :], o_ref[0, sl, :]
        qh = [jnp.where(lane_mask(q_rows, h), q, 0.0).astype(BF16) for h in range(2)]
        vh = [jnp.where(lane_mask(kv_blocks * ATT_BLOCK, h), v, 1.0).astype(BF16) for h in range(2)]
        o_parts, l_parts, m_parts = [], [], []
        for j in range(nb):
            first = 1 if (head and j == 0) else 0
            kb = j if (not head or j == 0) else j - 1
            ksl = slice(kb * ATT_BLOCK, (kb + 2) * ATT_BLOCK)
            qsl = slice(j * ATT_BLOCK, (j + 1) * ATT_BLOCK)
            outs, ms = [], []
            for h in range(2):
                s = _dot_nt(qh[h][qsl], k[ksl]) + tab_ref[pi, first, h]
                m = jnp.max(s, axis=1, keepdims=True)
                p = jnp.exp2(s - m).astype(BF16)
                outs.append(_dot(p, vh[h][ksl]))
                ms.append(m)
            o_parts.append(jnp.where(head0, outs[0], outs[1]))
            l_parts.append(pltpu.roll(jnp.where(head0, outs[1], outs[0]), HEAD_DIM, axis=1))
            m_parts.append(jnp.where(head0, ms[0], ms[1]))
        o = jnp.concatenate(o_parts, axis=0)
        l = jnp.concatenate(l_parts, axis=0)
        m = jnp.concatenate(m_parts, axis=0)
        if not init:
            m_new = jnp.maximum(m_old, m)
            a_old = jnp.exp2(m_old - m_new)
            a_cur = jnp.exp2(m - m_new)
            o = o_old * a_old + o * a_cur
            l = l_old * a_old + l * a_cur
            m = m_new
        o_ref[0, sl, :] = o
        m_ref[sl, :] = m
        l_ref[sl, :] = l

    n_units = seq // ATT_BLOCK
    order = sorted(range(len(PATTERNS)), key=lambda p: -PATTERNS[p][1])
    for pos, pi in enumerate(order):
        dil = PATTERNS[pi][1]
        log_d = dil.bit_length() - 1
        res_blocks = seq // (dil * ATT_BLOCK)
        nb = min(ATT_UNITS_PER_STEP, res_blocks)
        per_res = res_blocks // nb
        init = pos == 0

        def head_body(res, carry, pi=pi, dil=dil, nb=nb, init=init):
            batch(pi, dil, nb, res, 0, True, init)
            return carry
        lax.fori_loop(0, dil, head_body, 0)

        if per_res > 1:
            def tail_body(i, carry, pi=pi, dil=dil, nb=nb, log_d=log_d, init=init):
                res = i & (dil - 1)
                blk0 = nb * (1 + (i >> log_d))
                batch(pi, dil, nb, res, blk0, False, init)
                return carry
            lax.fori_loop(0, dil * (per_res - 1), tail_body, 0)

    def finish(i, carry):
        sl = pl.ds(pl.multiple_of(i * ATT_BLOCK, ATT_BLOCK), ATT_BLOCK)
        o_ref[0, sl, :] = o_ref[0, sl, :] / l_ref[sl, :]
        return carry
    lax.fori_loop(0, n_units, finish, 0)


def _attention(q, k, v, tables):
    bsz, seq, att_w = q.shape
    n_pairs = att_w // LANES
    blk = lambda b, p: (b, 0, p)
    qkv_spec = pl.BlockSpec((1, seq, LANES), blk)
    n_pat = tables.shape[0]
    tab_spec = pl.BlockSpec((n_pat, 2, 2, ATT_BLOCK, 2 * ATT_BLOCK), lambda b, p: (0, 0, p, 0, 0))
    return pl.pallas_call(
        functools.partial(_att_kernel, seq=seq),
        grid=(bsz, n_pairs),
        in_specs=[qkv_spec, qkv_spec, qkv_spec, tab_spec],
        out_specs=pl.BlockSpec((1, seq, LANES), blk),
        out_shape=jax.ShapeDtypeStruct((bsz, seq, att_w), F32),
        scratch_shapes=[pltpu.VMEM((seq, LANES), F32), pltpu.VMEM((seq, LANES), F32)],
        compiler_params=_cparams(("parallel", "parallel")),
        name="dilated_attention",
    )(q, k, v, tables)


def _ssm_params(a_re, a_im, b_re, b_im, c_re, c_im, d_skip, log_dt):
    n_g, n_p = a_re.shape
    n_c = b_re.shape[-1]
    n_state = n_g * n_p
    dt = jnp.exp(log_dt.astype(F32))[:, None]

    def a_pow(kk):
        mag = jnp.exp(kk * dt * a_re)
        ph = kk * dt * a_im
        return mag * jnp.cos(ph), mag * jnp.sin(ph)

    ab_re, ab_im = a_pow(1.0)
    nr, ni = ab_re - 1.0, ab_im
    den = a_re * a_re + a_im * a_im
    f_re = (nr * a_re + ni * a_im) / den
    f_im = (ni * a_re - nr * a_im) / den
    bb_re = f_re[:, :, None] * b_re - f_im[:, :, None] * b_im
    bb_im = f_re[:, :, None] * b_im + f_im[:, :, None] * b_re
    eye = jnp.eye(n_g, dtype=F32)
    w_re = jnp.einsum('gpc,gh->gchp', bb_re, eye).reshape(n_g * n_c, n_state)
    w_im = jnp.einsum('gpc,gh->gchp', bb_im, eye).reshape(n_g * n_c, n_state)
    o_re = jnp.einsum('gcp,gh->gphc', c_re, eye).reshape(n_state, n_g * n_c)
    o_im = jnp.einsum('gcp,gh->gphc', c_im, eye).reshape(n_state, n_g * n_c)
    n_kb = (n_g * n_c) // LANES
    sp = n_state // n_kb
    w_in = jnp.stack([jnp.concatenate([w_re[kb * LANES:(kb + 1) * LANES, kb * sp:(kb + 1) * sp],
                                       w_im[kb * LANES:(kb + 1) * LANES, kb * sp:(kb + 1) * sp]], axis=1)
                      for kb in range(n_kb)])
    w_out = jnp.stack([jnp.concatenate([o_re[kb * sp:(kb + 1) * sp, kb * LANES:(kb + 1) * LANES],
                                        -o_im[kb * sp:(kb + 1) * sp, kb * LANES:(kb + 1) * LANES]], axis=0)
                       for kb in range(n_kb)])
    j = jnp.arange(SUBLANES, dtype=F32)[:, None]
    flat = lambda t: jnp.broadcast_to(t.reshape(1, n_state), (SUBLANES, n_state))
    coef = []
    for sh in (1, 2, 4):
        pr, pim = a_pow(float(sh))
        keep = (j >= sh).astype(F32)
        coef += [flat(pr) * keep, flat(pim) * keep]
    a_re_f = jnp.broadcast_to(a_re.reshape(1, n_state), (SUBLANES, n_state))
    a_im_f = jnp.broadcast_to(a_im.reshape(1, n_state), (SUBLANES, n_state))
    dt_f = jnp.broadcast_to(jnp.repeat(dt[:, 0], n_p).reshape(1, n_state), (SUBLANES, n_state))
    mag = jnp.exp((j + 1.0) * dt_f * a_re_f)
    ph = (j + 1.0) * dt_f * a_im_f
    coef += [mag * jnp.cos(ph), mag * jnp.sin(ph)]
    coef = jnp.stack(coef)
    return w_in.astype(BF16), w_out.astype(BF16), coef, d_skip.reshape(1, n_g * n_c).astype(F32)


def _gelu_tanh(x):
    return 0.5 * x * (1.0 + jnp.tanh(math.sqrt(2.0 / math.pi) * (x + 0.044715 * (x * x * x))))


def _ssm_kernel(u_ref, win_ref, wout_ref, coef_ref, d_ref, wglu_ref, bglu_ref, o_ref,
                hre_ref, him_ref, cre_ref, cim_ref, *, rows, n_state, col_w):
    n_kb = win_ref.shape[0]
    sp = n_state // n_kb

    @pl.when(pl.program_id(1) == 0)
    def _():
        cre_ref[...] = jnp.zeros_like(cre_ref)
        cim_ref[...] = jnp.zeros_like(cim_ref)

    u = u_ref[0]
    ub = u.astype(BF16)
    for kb in range(n_kb):
        bu = _dot(ub[:, kb * LANES:(kb + 1) * LANES], win_ref[kb])
        hre_ref[:, kb * sp:(kb + 1) * sp] = bu[:, :sp]
        him_ref[:, kb * sp:(kb + 1) * sp] = bu[:, sp:]

    n_slab = rows // SUBLANES
    for c0 in range(0, n_state, col_w):
        cs = slice(c0, c0 + col_w)

        def body(i, carry, cs=cs):
            c_re, c_im = carry
            sl = pl.ds(pl.multiple_of(i * SUBLANES, SUBLANES), SUBLANES)
            x_re = hre_ref[sl, cs]
            x_im = him_ref[sl, cs]
            for si, sh in enumerate((1, 2, 4)):
                p_re = coef_ref[2 * si, :, cs]
                p_im = coef_ref[2 * si + 1, :, cs]
                s_re = pltpu.roll(x_re, sh, axis=0)
                s_im = pltpu.roll(x_im, sh, axis=0)
                x_re, x_im = (x_re + p_re * s_re - p_im * s_im,
                              x_im + p_re * s_im + p_im * s_re)
            p_re = coef_ref[6, :, cs]
            p_im = coef_ref[7, :, cs]
            h_re = x_re + p_re * c_re - p_im * c_im
            h_im = x_im + p_re * c_im + p_im * c_re
            hre_ref[sl, cs] = h_re
            him_ref[sl, cs] = h_im
            last = slice(SUBLANES - 1, SUBLANES)
            return (jnp.broadcast_to(h_re[last, :], h_re.shape),
                    jnp.broadcast_to(h_im[last, :], h_im.shape))

        c_re, c_im = lax.fori_loop(0, n_slab, body, (cre_ref[:, cs], cim_ref[:, cs]), unroll=True)
        cre_ref[:, cs] = c_re
        cim_ref[:, cs] = c_im

    ys = []
    for kb in range(n_kb):
        hcat = jnp.concatenate([hre_ref[:, kb * sp:(kb + 1) * sp].astype(BF16),
                                him_ref[:, kb * sp:(kb + 1) * sp].astype(BF16)], axis=1)
        ys.append(_dot(hcat, wout_ref[kb]))
    y = jnp.concatenate(ys, axis=1) + d_ref[...] * u
    y = _gelu_tanh(y)
    z = _dot(y.astype(BF16), wglu_ref[...]) + bglu_ref[...]
    o_ref[0] = y * jax.nn.sigmoid(z)


def _ssm(u, w_in, w_out, coef, d_flat, w_glu, b_glu):
    bsz, seq, ssm_w = u.shape
    n_state = coef.shape[-1]
    rows = min(SSM_ROWS, seq)
    full = lambda *shape: pl.BlockSpec(shape, lambda b, i: (0,) * len(shape))
    kern = functools.partial(_ssm_kernel, rows=rows, n_state=n_state, col_w=4 * LANES)
    return pl.pallas_call(
        kern,
        grid=(bsz, seq // rows),
        in_specs=[pl.BlockSpec((1, rows, ssm_w), lambda b, i: (b, i, 0)),
                  full(*w_in.shape), full(*w_out.shape), full(*coef.shape), full(1, ssm_w),
                  full(ssm_w, ssm_w), full(1, ssm_w)],
        out_specs=pl.BlockSpec((1, rows, ssm_w), lambda b, i: (b, i, 0)),
        out_shape=jax.ShapeDtypeStruct((bsz, seq, ssm_w), F32),
        scratch_shapes=[pltpu.VMEM((rows, n_state), F32), pltpu.VMEM((rows, n_state), F32),
                        pltpu.VMEM((SUBLANES, n_state), F32), pltpu.VMEM((SUBLANES, n_state), F32)],
        compiler_params=_cparams(("parallel", "arbitrary")),
        name="s5_glu",
    )(u, w_in, w_out, coef, d_flat, w_glu.astype(BF16), b_glu.reshape(1, ssm_w))


def _layer_norm(y, g, b):
    mu = jnp.mean(y, axis=-1, keepdims=True)
    yc = y - mu
    var = jnp.mean(yc * yc, axis=-1, keepdims=True)
    return yc * lax.rsqrt(var + EPS) * g + b


def _rms_norm(y, g):
    return y * lax.rsqrt(jnp.mean(y * y, axis=-1, keepdims=True) + EPS) * g


def _mix_kernel(x_ref, att_ref, ssm_ref, gatt_ref, gssm_ref, wout_ref, g1_ref, ln_g_ref, ln_b_ref,
                sc_ref, sh_ref, g2_ref, wsg_ref, wsu_ref, wsd_ref, h_ref, ht_ref, pre_ref, *, alpha, att_w):
    a_n = _rms_norm(att_ref[0], gatt_ref[...]).astype(BF16)
    s_n = _rms_norm(ssm_ref[0], gssm_ref[...]).astype(BF16)
    mix = _dot(a_n, wout_ref[0:att_w, :]) + _dot(s_n, wout_ref[att_w:, :])
    x1 = _layer_norm(alpha * x_ref[0] + g1_ref[0] * mix, ln_g_ref[...], ln_b_ref[...])
    h = x1 * (1.0 + sc_ref[0]) + sh_ref[0]
    h_ref[0] = h
    _store_token_tiles(ht_ref, h)
    hb = h.astype(BF16)
    hid = _silu(_dot(hb, wsg_ref[...])) * _dot(hb, wsu_ref[...])
    shared = _dot(hid.astype(BF16), wsd_ref[...])
    pre_ref[0] = alpha * x1 + g2_ref[0] * shared


def _mix(x, att, ssm, g_att, g_ssm, w_out, gate1, ln_g, ln_b, scale2, shift2, gate2,
         w_s_gate, w_s_up, w_s_down, alpha):
    bsz, seq, d = x.shape
    att_w = att.shape[-1]
    ssm_w = ssm.shape[-1]
    ff = w_s_gate.shape[1]
    tm = min(MIX_ROWS, seq)
    n_sub = d // LANES
    row = lambda b, i: (b, i, 0)
    per_b = lambda b, i: (b, 0, 0)
    full = lambda *shape: pl.BlockSpec(shape, lambda b, i: (0,) * len(shape))
    vec = lambda t: t.reshape(bsz, 1, d)
    return pl.pallas_call(
        functools.partial(_mix_kernel, alpha=alpha, att_w=att_w),
        grid=(bsz, seq // tm),
        in_specs=[pl.BlockSpec((1, tm, d), row), pl.BlockSpec((1, tm, att_w), row),
                  pl.BlockSpec((1, tm, ssm_w), row),
                  full(1, att_w), full(1, ssm_w), full(att_w + ssm_w, d),
                  pl.BlockSpec((1, 1, d), per_b), full(1, d), full(1, d),
                  pl.BlockSpec((1, 1, d), per_b), pl.BlockSpec((1, 1, d), per_b),
                  pl.BlockSpec((1, 1, d), per_b),
                  full(d, ff), full(d, ff), full(ff, d)],
        out_specs=[pl.BlockSpec((1, tm, d), row),
                   pl.BlockSpec((tm * n_sub, LANES), lambda b, i: (b * (seq // tm) + i, 0)),
                   pl.BlockSpec((1, tm, d), row)],
        out_shape=[jax.ShapeDtypeStruct((bsz, seq, d), F32),
                   jax.ShapeDtypeStruct((bsz * seq * n_sub, LANES), F32),
                   jax.ShapeDtypeStruct((bsz, seq, d), F32)],
        compiler_params=_cparams(("parallel", "parallel")),
        name="mix_ln1_shared",
    )(x, att, ssm, g_att.reshape(1, att_w), g_ssm.reshape(1, ssm_w), w_out.astype(BF16),
      vec(gate1), ln_g.reshape(1, d), ln_b.reshape(1, d), vec(scale2), vec(shift2), vec(gate2),
      w_s_gate.astype(BF16), w_s_up.astype(BF16), w_s_down.astype(BF16))


def _route_kernel(h_ref, wrt_ref, wrt_lo_ref, bias_ref, tri_ref, idx_ref, gate_ref, rank_ref, cnt_ref, carry_ref):
    n_e = wrt_ref.shape[0]
    tm = h_ref.shape[0]
    per_group = n_e // N_EXPERT_GROUPS
    neg = -jnp.inf

    @pl.when(pl.program_id(0) == 0)
    def _():
        carry_ref[...] = jnp.zeros_like(carry_ref)

    h_hi, h_lo = _split_bf16(h_ref[...])
    logits = (_dot_nt(wrt_ref[...], h_hi) + _dot_nt(wrt_ref[...], h_lo)
              + _dot_nt(wrt_lo_ref[...], h_hi))
    scores = jax.nn.sigmoid(logits)
    biased = scores + bias_ref[...]

    g3 = biased.reshape(N_EXPERT_GROUPS, per_group, tm)
    ridx = lax.broadcasted_iota(jnp.int32, g3.shape, 1).astype(F32)
    m1 = jnp.max(g3, axis=1, keepdims=True)
    first = jnp.min(jnp.where(g3 == m1, ridx, float(per_group)), axis=1, keepdims=True)
    m2 = jnp.max(jnp.where(ridx == first, neg, g3), axis=1, keepdims=True)
    gs = m1 + m2

    gidx = lax.broadcasted_iota(jnp.int32, gs.shape, 0).astype(F32)
    ok = jnp.zeros_like(gs)
    cur = gs
    for _ in range(TOPK_GROUPS):
        mx = jnp.max(cur, axis=0, keepdims=True)
        fi = jnp.min(jnp.where(cur == mx, gidx, float(N_EXPERT_GROUPS)), axis=0, keepdims=True)
        hit = gidx == fi
        ok = jnp.where(hit, 1.0, ok)
        cur = jnp.where(hit, neg, cur)
    masked = jnp.where(ok > 0.5, g3, neg).reshape(n_e, tm)

    eidx = lax.broadcasted_iota(jnp.int32, (n_e, tm), 0).astype(F32)
    onehot = jnp.zeros((n_e, tm), F32)
    cur = masked
    sel_idx = []
    sel_gate = []
    for _ in range(TOP_K):
        mx = jnp.max(cur, axis=0, keepdims=True)
        fi = jnp.min(jnp.where(cur == mx, eidx, float(n_e)), axis=0, keepdims=True)
        hit = eidx == fi
        sel_idx.append(fi)
        sel_gate.append(jnp.sum(jnp.where(hit, scores, 0.0), axis=0, keepdims=True))
        onehot = jnp.where(hit, 1.0, onehot)
        cur = jnp.where(hit, neg, cur)
    idx = jnp.concatenate(sel_idx, axis=0)
    gate = jnp.concatenate(sel_gate, axis=0)
    gate = gate / jnp.sum(gate, axis=0, keepdims=True) * ROUTED_SCALE

    prior = _dot(onehot.astype(BF16), tri_ref[...]) + carry_ref[:, 0:1]
    ranks = [jnp.sum(jnp.where(eidx == sel_idx[k], prior, 0.0), axis=0, keepdims=True)
             for k in range(TOP_K)]
    rank = jnp.concatenate(ranks, axis=0)
    carry = carry_ref[...] + jnp.sum(onehot, axis=1, keepdims=True)
    carry_ref[...] = carry
    cnt_ref[...] = carry

    idx_ref[...] = idx.astype(jnp.int32)
    gate_ref[...] = gate
    rank_ref[...] = rank.astype(jnp.int32)


def _route(h2d, w_router, router_bias):
    n_tok, d = h2d.shape
    n_e = w_router.shape[1]
    tm = min(ROUTE_COLS, n_tok)
    tri = (np.arange(tm)[:, None] < np.arange(tm)[None, :]).astype(np.float32)
    w_hi, w_lo = _split_bf16(w_router.T.astype(F32))
    col = lambda i: (0, i)
    full = lambda *shape: pl.BlockSpec(shape, lambda i: (0,) * len(shape))
    idx, gate, rank, cnt = pl.pallas_call(
        _route_kernel,
        grid=(n_tok // tm,),
        in_specs=[pl.BlockSpec((tm, d), lambda i: (i, 0)), full(n_e, d), full(n_e, d), full(n_e, 1),
                  full(tm, tm)],
        out_specs=[pl.BlockSpec((TOP_K, tm), col), pl.BlockSpec((TOP_K, tm), col),
                   pl.BlockSpec((TOP_K, tm), col), full(n_e, LANES)],
        out_shape=[jax.ShapeDtypeStruct((TOP_K, n_tok), jnp.int32),
                   jax.ShapeDtypeStruct((TOP_K, n_tok), F32),
                   jax.ShapeDtypeStruct((TOP_K, n_tok), jnp.int32),
                   jax.ShapeDtypeStruct((n_e, LANES), F32)],
        scratch_shapes=[pltpu.VMEM((n_e, LANES), F32)],
        compiler_params=_cparams(("arbitrary",)),
        name="router_topk",
    )(h2d, w_hi, w_lo, router_bias.reshape(n_e, 1).astype(F32), jnp.asarray(tri, BF16))
    return idx, gate, rank, cnt[:, 0].astype(jnp.int32)


def _sc_mesh():
    return plsc.VectorSubcoreMesh(core_axis_name="core", subcore_axis_name="subcore")


def _dispatch(h_tiles, gate_rows, dest_windows, n_rows):
    n_tok, n_sub, _ = h_tiles.shape
    win = DISPATCH_WINDOW

    @pl.kernel(out_type=[jax.ShapeDtypeStruct((n_rows, n_sub, LANES), h_tiles.dtype),
                         jax.ShapeDtypeStruct((n_rows, GATE_LANES), F32)],
               mesh=_sc_mesh(), scratch_types=[])
    def scatter_tiles(x_hbm, g_hbm, d_hbm, xs_hbm, rg_hbm):
        def window(x_vmem, g_vmem, d_vmem):
            for k in range(TOP_K):
                rows = d_vmem.at[0, pl.ds(k * win, win)]
                pltpu.sync_copy(x_vmem, xs_hbm.at[rows])
                pltpu.sync_copy(g_vmem.at[k], rg_hbm.at[rows])
        pltpu.emit_pipeline(
            window,
            grid=(n_tok // win,),
            in_specs=[pl.BlockSpec((win, n_sub, LANES), lambda i: (i, 0, 0)),
                      pl.BlockSpec((TOP_K, win, GATE_LANES), lambda i: (0, i, 0)),
                      pl.BlockSpec((1, TOP_K * win), lambda i: (i, 0))],
            out_specs=[],
            core_axis_name=("core", "subcore"),
            dimension_semantics=(pltpu.PARALLEL,),
        )(x_hbm, g_hbm, d_hbm)

    return scatter_tiles(h_tiles, gate_rows, dest_windows)


def _gather_sum(ys_tiles, dest_windows, n_tok):
    _, n_sub, _ = ys_tiles.shape
    win = DISPATCH_WINDOW

    @pl.kernel(out_type=jax.ShapeDtypeStruct((n_tok, n_sub, LANES), ys_tiles.dtype), mesh=_sc_mesh(),
               scratch_types=[])
    def gather_tiles(y_hbm, d_hbm, o_hbm):
        def window(d_vmem, o_vmem):
            pltpu.sync_copy(y_hbm.at[d_vmem.at[0, pl.ds(0, win)]], o_vmem)
            for k in range(1, TOP_K):
                pltpu.sync_copy(y_hbm.at[d_vmem.at[0, pl.ds(k * win, win)]], o_vmem, add=True)
        pltpu.emit_pipeline(
            window,
            grid=(n_tok // win,),
            in_specs=[pl.BlockSpec((1, TOP_K * win), lambda i: (i, 0))],
            out_specs=[pl.BlockSpec((win, n_sub, LANES), lambda i: (i, 0, 0))],
            core_axis_name=("core", "subcore"),
            dimension_semantics=(pltpu.PARALLEL,),
        )(d_hbm, o_hbm)

    return gather_tiles(ys_tiles, dest_windows)


def _expert_kernel(first_ref, nblk_ref, cnt_ref, nused_ref, xs_ref, rg_ref, wg_ref, wu_ref, wd_ref, ys_ref,
                   xbuf_ref, gbuf_ref, ybuf_ref, wgb_ref, wub_ref, wdb_ref, xsem_ref, ysem_ref):
    e = pl.program_id(0)
    n_used = nused_ref[0]
    d = wgb_ref.shape[0]
    blk_rows = xbuf_ref.shape[1]

    class _Pair:
        def __init__(self, g, slot):
            src = xs_ref.at[pl.ds(pl.multiple_of(g * blk_rows, blk_rows), blk_rows)]
            gsrc = rg_ref.at[pl.ds(pl.multiple_of(g * MOE_ROWS, MOE_ROWS), MOE_ROWS)]
            self.copies = (pltpu.make_async_copy(src, xbuf_ref.at[slot], xsem_ref.at[slot]),
                           pltpu.make_async_copy(gsrc, gbuf_ref.at[slot], xsem_ref.at[slot]))

        def start(self):
            for cp in self.copies:
                cp.start()

        def wait(self):
            for cp in self.copies:
                cp.wait()

    def x_copy(g, slot):
        return _Pair(g, slot)

    def y_copy(g, slot):
        dst = ys_ref.at[pl.ds(pl.multiple_of(g * blk_rows, blk_rows), blk_rows)]
        return pltpu.make_async_copy(ybuf_ref.at[slot], dst, ysem_ref.at[slot])

    n_slot = xbuf_ref.shape[0]

    @pl.when(e == 0)
    def _():
        for g in range(n_slot - 1):
            @pl.when(g < n_used)
            def _(g=g):
                x_copy(g, g).start()

    @pl.when(nblk_ref[e] > 0)
    def _():
        wgb_ref[...] = wg_ref[0].astype(BF16)
        wub_ref[...] = wu_ref[0].astype(BF16)
        wdb_ref[...] = wd_ref[0].astype(BF16)

    def block(j, carry):
        g = first_ref[e] + j
        slot = g & (n_slot - 1)
        x_copy(g, slot).wait()

        @pl.when(g + n_slot - 1 < n_used)
        def _():
            x_copy(g + n_slot - 1, (g + n_slot - 1) & (n_slot - 1)).start()

        @pl.when(g >= n_slot)
        def _():
            y_copy(g - n_slot, slot).wait()

        x = jnp.concatenate(_load_token_tiles(xbuf_ref.at[slot], MOE_ROWS, d), axis=1)
        row = lax.broadcasted_iota(jnp.int32, (MOE_ROWS, 1), 0)
        xb = jnp.where(row < cnt_ref[e] - j * MOE_ROWS, x, 0.0).astype(BF16)
        hid = _silu(_dot(xb, wgb_ref[...])) * _dot(xb, wub_ref[...])
        y = _dot(hid.astype(BF16), wdb_ref[...]) * gbuf_ref[slot][:, 0:1]
        _store_token_tiles(ybuf_ref.at[slot], jnp.where(row < cnt_ref[e] - j * MOE_ROWS, y, 0.0))
        y_copy(g, slot).start()
        return carry
    lax.fori_loop(0, nblk_ref[e], block, 0)

    @pl.when(e == pl.num_programs(0) - 1)
    def _():
        for back in range(1, n_slot + 1):
            @pl.when(n_used >= back)
            def _(back=back):
                y_copy(n_used - back, (n_used - back) & (n_slot - 1)).wait()


def _experts(xs_tiles, row_gates, plan, w_gate, w_up, w_down):
    n_rows, n_sub, _ = xs_tiles.shape
    d = n_sub * LANES
    n_e, _, ff = w_gate.shape
    blk_rows = MOE_ROWS * n_sub
    hbm = pl.BlockSpec(memory_space=pl.ANY)
    grid_spec = pltpu.PrefetchScalarGridSpec(
        num_scalar_prefetch=4,
        grid=(n_e,),
        in_specs=[hbm, hbm,
                  pl.BlockSpec((1, d, ff), lambda e, *_: (e, 0, 0)),
                  pl.BlockSpec((1, d, ff), lambda e, *_: (e, 0, 0)),
                  pl.BlockSpec((1, ff, d), lambda e, *_: (e, 0, 0))],
        out_specs=hbm,
        scratch_shapes=[pltpu.VMEM((EXPERT_SLOTS, blk_rows, LANES), F32),
                        pltpu.VMEM((EXPERT_SLOTS, MOE_ROWS, GATE_LANES), F32),
                        pltpu.VMEM((EXPERT_SLOTS, blk_rows, LANES), F32),
                        pltpu.VMEM((d, ff), BF16), pltpu.VMEM((d, ff), BF16), pltpu.VMEM((ff, d), BF16),
                        pltpu.SemaphoreType.DMA((EXPERT_SLOTS,)), pltpu.SemaphoreType.DMA((EXPERT_SLOTS,))],
    )
    ys = pl.pallas_call(
        _expert_kernel,
        grid_spec=grid_spec,
        out_shape=jax.ShapeDtypeStruct((n_rows * n_sub, LANES), F32),
        compiler_params=_cparams(("arbitrary",)),
        name="moe_experts",
    )(*plan, xs_tiles.reshape(n_rows * n_sub, LANES), row_gates, w_gate, w_up, w_down)
    return ys.reshape(n_rows, n_sub, LANES)


def _combine_kernel(dest_ref, ys_ref, pre_ref, g2_ref, ln_g_ref, ln_b_ref, o_ref, buf_ref, sem_ref):
    tm, d = pre_ref.shape[1], pre_ref.shape[2]
    n_sub = d // LANES

    def row_copy(t, k, src_row):
        slot = pl.multiple_of((k * tm + t) * n_sub, n_sub)
        return pltpu.make_async_copy(ys_ref.at[src_row], buf_ref.at[pl.ds(slot, n_sub)], sem_ref.at[0])

    def issue(g, c):
        t0 = g * DMA_ISSUE_UNROLL
        rows = [[dest_ref[(t0 + j) * TOP_K + k] for k in range(TOP_K)] for j in range(DMA_ISSUE_UNROLL)]
        for j in range(DMA_ISSUE_UNROLL):
            for k in range(TOP_K):
                row_copy(t0 + j, k, rows[j][k]).start(priority=k % 2)
        return c
    lax.fori_loop(0, tm // DMA_ISSUE_UNROLL, issue, 0)

    def drain(t, c):
        for k in range(TOP_K):
            row_copy(t, k, dest_ref[t * TOP_K + k]).wait()
        return c
    lax.fori_loop(0, tm, drain, 0)

    pieces = []
    for s in range(n_sub):
        acc = None
        for k in range(TOP_K):
            term = buf_ref[pl.ds(k * tm * n_sub + s, tm, stride=n_sub), :]
            acc = term if acc is None else acc + term
        pieces.append(acc)
    routed = jnp.concatenate(pieces, axis=1)
    y = pre_ref[0] + g2_ref[0] * routed
    o_ref[0] = _layer_norm(y, ln_g_ref[...], ln_b_ref[...])


def _combine(ys, dest, pre, gate2, ln_g, ln_b, n_tiles):
    bsz, seq, d = pre.shape
    tm = min(DISPATCH_ROWS, seq)
    per_seq = seq // tm
    row = lambda i, *_: (i // per_seq, i % per_seq, 0)
    per_b = lambda i, *_: (i // per_seq, 0, 0)
    full = lambda *shape: pl.BlockSpec(shape, lambda i, *_: (0,) * len(shape))
    grid_spec = pltpu.PrefetchScalarGridSpec(
        num_scalar_prefetch=0,
        grid=(n_tiles,),
        in_specs=[pl.BlockSpec((TOP_K * tm,), lambda i: (i,), memory_space=pltpu.SMEM),
                  pl.BlockSpec(memory_space=pl.ANY),
                  pl.BlockSpec((1, tm, d), row), pl.BlockSpec((1, 1, d), per_b), full(1, d), full(1, d)],
        out_specs=pl.BlockSpec((1, tm, d), row),
        scratch_shapes=[pltpu.VMEM((TOP_K * tm * (d // LANES), LANES), F32), pltpu.SemaphoreType.DMA((1,))],
    )
    return pl.pallas_call(
        _combine_kernel,
        grid_spec=grid_spec,
        out_shape=jax.ShapeDtypeStruct((bsz, seq, d), F32),
        compiler_params=_cparams(("arbitrary",)),
        name="moe_combine_ln2",
    )(dest, ys, pre, gate2.reshape(bsz, 1, d), ln_g.reshape(1, d), ln_b.reshape(1, d))


def _final_ln_kernel(r_ref, pre_ref, g2_ref, ln_g_ref, ln_b_ref, *rest):
    o_ref = rest[-1]
    tm, d = pre_ref.shape[1], pre_ref.shape[2]
    routed = jnp.concatenate(_load_token_tiles(r_ref, tm, d), axis=1)
    y = pre_ref[0] + g2_ref[0] * routed
    o_ref[0] = _layer_norm(y, ln_g_ref[...], ln_b_ref[...])


def _final_ln(routed_tiles, pre, gate2, ln_g, ln_b, first_tile, partial=None):
    bsz, seq, d = pre.shape
    tm = min(DISPATCH_ROWS, seq)
    per_seq = seq // tm
    n_sub = d // LANES
    n_tiles = bsz * per_seq - first_tile
    row = lambda i: ((i + first_tile) // per_seq, (i + first_tile) % per_seq, 0)
    per_b = lambda i: ((i + first_tile) // per_seq, 0, 0)
    full = lambda *shape: pl.BlockSpec(shape, lambda i: (0,) * len(shape))
    in_specs = [pl.BlockSpec((tm * n_sub, LANES), lambda i: (i, 0)),
                pl.BlockSpec((1, tm, d), row), pl.BlockSpec((1, 1, d), per_b), full(1, d), full(1, d)]
    args = [routed_tiles.reshape(-1, LANES), pre, gate2.reshape(bsz, 1, d), ln_g.reshape(1, d),
            ln_b.reshape(1, d)]
    aliases = {}
    if partial is not None:
        in_specs.append(pl.BlockSpec(memory_space=pl.ANY))
        args.append(partial)
        aliases = {len(args) - 1: 0}
    return pl.pallas_call(
        _final_ln_kernel,
        grid=(n_tiles,),
        in_specs=in_specs,
        out_specs=pl.BlockSpec((1, tm, d), row),
        out_shape=jax.ShapeDtypeStruct((bsz, seq, d), F32),
        input_output_aliases=aliases,
        compiler_params=_cparams(("arbitrary",)),
        name="moe_final_ln2",
    )(*args)


def _dest_kernel(pstart_ref, idx_ref, rank_ref, dest_ref):
    idx = idx_ref[...]

    def body(e, acc):
        return acc + jnp.where(idx == e, pstart_ref[e], 0)
    dest_ref[...] = lax.fori_loop(0, pstart_ref.shape[0], body, rank_ref[...], unroll=8)


def _dest_rows(pstart, idx, rank):
    n_k, n_tok = idx.shape
    tn = min(4096, n_tok)
    grid_spec = pltpu.PrefetchScalarGridSpec(
        num_scalar_prefetch=1,
        grid=(n_tok // tn,),
        in_specs=[pl.BlockSpec((n_k, tn), lambda i, *_: (0, i))] * 2,
        out_specs=pl.BlockSpec((n_k, tn), lambda i, *_: (0, i)),
    )
    return pl.pallas_call(
        _dest_kernel,
        grid_spec=grid_spec,
        out_shape=jax.ShapeDtypeStruct((n_k, n_tok), jnp.int32),
        compiler_params=_cparams(("arbitrary",)),
        name="moe_dest",
    )(pstart, idx, rank)


def _moe_plan(idx, rank, cnt, n_tok):
    n_e = cnt.shape[0]
    padded = (cnt + MOE_ROWS - 1) // MOE_ROWS * MOE_ROWS
    pend = jnp.cumsum(padded).astype(jnp.int32)
    pstart = pend - padded
    dest_kt = _dest_rows(pstart, idx, rank)
    dest = dest_kt.T.reshape(-1)
    win = min(DISPATCH_WINDOW, n_tok)
    dest_windows = dest_kt.reshape(TOP_K, n_tok // win, win).transpose(1, 0, 2).reshape(n_tok // win, TOP_K * win)
    n_blk = n_tok * TOP_K // MOE_ROWS + n_e
    n_used = (pend[-1:] // MOE_ROWS).astype(jnp.int32)
    plan = ((pstart // MOE_ROWS).astype(jnp.int32), (padded // MOE_ROWS).astype(jnp.int32),
            cnt.astype(jnp.int32), n_used)
    return dest, dest_windows, plan, n_blk * MOE_ROWS


def kernel(x, c, rel_bias, w_ada, b_ada, w_in, ssm_a_re, ssm_a_im, ssm_b_re, ssm_b_im, ssm_c_re, ssm_c_im, ssm_d, ssm_log_dt, w_glu, b_glu, g_att, g_ssm, w_out, ln1_g, ln1_b, w_router, router_bias, w_e_gate, w_e_up, w_e_down, w_s_gate, w_s_up, w_s_down, ln2_g, ln2_b):
    bsz, seq, d = x.shape
    depth = w_ada.shape[0]
    alpha = (2 * depth) ** 0.25
    att_w = g_att.shape[-1]
    tables = _att_tables(rel_bias)
    for layer in range(depth):
        ada = _ada(c, w_ada[layer], b_ada[layer])
        shift1, scale1, gate1, shift2, scale2, gate2 = jnp.split(ada, 6, axis=-1)
        q, k, v, u = _inproj(x, scale1, shift1, w_in[layer], att_w)
        att = _attention(q, k, v, tables)
        ssm_prm = _ssm_params(ssm_a_re[layer], ssm_a_im[layer], ssm_b_re[layer], ssm_b_im[layer],
                              ssm_c_re[layer], ssm_c_im[layer], ssm_d[layer], ssm_log_dt[layer])
        ssm = _ssm(u, *ssm_prm, w_glu[layer], b_glu[layer])
        h2, h2_tiles, pre = _mix(x, att, ssm, g_att[layer], g_ssm[layer], w_out[layer], gate1,
                                 ln1_g[layer], ln1_b[layer], scale2, shift2, gate2, w_s_gate[layer],
                                 w_s_up[layer], w_s_down[layer], alpha)
        h2d = h2.reshape(bsz * seq, d)
        idx, gate, rank, cnt = _route(h2d, w_router[layer], router_bias[layer])
        dest, dest_windows, plan, n_rows = _moe_plan(idx, rank, cnt, bsz * seq)
        n_tok = bsz * seq
        gate_rows = jnp.broadcast_to(gate[:, :, None], (TOP_K, n_tok, GATE_LANES))
        xs, row_gates = _dispatch(h2_tiles.reshape(n_tok, d // LANES, LANES), gate_rows, dest_windows, n_rows)
        ys = _experts(xs, row_gates, plan, w_e_gate[layer], w_e_up[layer], w_e_down[layer])
        tm = min(DISPATCH_ROWS, seq)
        tc_tiles = min(COMBINE_TC_TILES, n_tok // tm)
        tc_tokens = tc_tiles * tm
        win = min(DISPATCH_WINDOW, n_tok)
        partial = (_combine(ys, dest, pre, gate2, ln2_g[layer], ln2_b[layer], tc_tiles)
                   if tc_tiles > 0 else None)
        routed = _gather_sum(ys, dest_windows[tc_tokens // win:], n_tok - tc_tokens)
        x = _final_ln(routed, pre, gate2, ln2_g[layer], ln2_b[layer], tc_tiles, partial)
    return x
```

```python
import functools
import math

import jax
import jax.numpy as jnp
import numpy as np
from jax import lax
from jax.experimental import pallas as pl
from jax.experimental.pallas import tpu as pltpu
from jax.experimental.pallas import tpu_sc as plsc

F32 = jnp.float32
BF16 = jnp.bfloat16

HEAD_DIM = 64
ATT_BLOCK = 128
PATTERNS = ((128, 1), (512, 4), (2048, 16))
N_BUCKETS = 32
MAX_DISTANCE = 2048
SSM_GROUP = 16
SSM_STATE = 64
N_EXPERTS = 256
TOP_K = 8
N_EXPERT_GROUPS = 8
TOPK_GROUPS = 4
ROUTED_SCALE = 2.5
EPS = 1e-5
NEG_INF = -1e30
LOG2_E = math.log2(math.e)

LANES = 128
SUBLANES = 8
VMEM_LIMIT_BYTES = 56 * 1024 * 1024

PROJ_ROWS = 512
SSM_ROWS = 256
MIX_ROWS = 512
ROUTE_COLS = 512
MOE_ROWS = 256
DISPATCH_ROWS = 256
COMBINE_TC_TILES = 48
GATE_LANES = LANES
DISPATCH_WINDOW = 32
ATT_UNITS_PER_STEP = 8
DMA_ISSUE_UNROLL = 4
EXPERT_SLOTS = 4


def _cparams(sem, vmem=VMEM_LIMIT_BYTES):
    return pltpu.CompilerParams(dimension_semantics=sem, vmem_limit_bytes=vmem)


def _dot(a, b):
    return jnp.dot(a, b, preferred_element_type=F32)


def _dot_nt(a, b):
    return lax.dot_general(a, b, (((1,), (1,)), ((), ())), preferred_element_type=F32)


def _silu(x):
    return x * jax.nn.sigmoid(x)


def _store_token_tiles(ref, x):
    m, d = x.shape
    n_sub = d // LANES
    for s in range(n_sub):
        ref[pl.ds(s, m, stride=n_sub), :] = x[:, s * LANES:(s + 1) * LANES]


def _load_token_tiles(ref, m, d, row0=0):
    n_sub = d // LANES
    return [ref[pl.ds(row0 + s, m, stride=n_sub), :] for s in range(n_sub)]


def _split_bf16(x):
    hi = x.astype(BF16)
    lo = (x - hi.astype(F32)).astype(BF16)
    return hi, lo


def _ada_kernel(c_ref, w_ref, b_ref, o_ref):
    c = c_ref[...]
    a_hi, a_lo = _split_bf16(_silu(c))
    w_hi, w_lo = _split_bf16(w_ref[...])
    acc = _dot(a_hi, w_hi) + _dot(a_hi, w_lo) + _dot(a_lo, w_hi)
    o_ref[...] = acc + b_ref[...]


def _ada(c, w_ada, b_ada):
    bsz, d = c.shape
    n = w_ada.shape[1]
    rows = SUBLANES
    c_pad = jnp.zeros((rows, d), F32).at[:bsz].set(c)
    tn = 1024
    out = pl.pallas_call(
        _ada_kernel,
        grid=(n // tn,),
        in_specs=[pl.BlockSpec((rows, d), lambda j: (0, 0)),
                  pl.BlockSpec((d, tn), lambda j: (0, j)),
                  pl.BlockSpec((1, tn), lambda j: (0, j))],
        out_specs=pl.BlockSpec((rows, tn), lambda j: (0, j)),
        out_shape=jax.ShapeDtypeStruct((rows, n), F32),
        compiler_params=_cparams(("parallel",)),
        name="ada",
    )(c_pad, w_ada, b_ada.reshape(1, n))
    return out[:bsz]


def _inproj_kernel(x_ref, sc_ref, sh_ref, w_ref, q_ref, k_ref, v_ref, u_ref, *, att_w, q_scale):
    h = (x_ref[0] * (1.0 + sc_ref[0]) + sh_ref[0]).astype(BF16)
    q_ref[0] = _dot(h, w_ref[:, 0:att_w]) * q_scale
    k_ref[0] = _dot(h, w_ref[:, att_w:2 * att_w])
    v_ref[0] = _dot(h, w_ref[:, 2 * att_w:3 * att_w])
    u_ref[0] = _dot(h, w_ref[:, 3 * att_w:])


def _inproj(x, scale, shift, w_in, att_w):
    bsz, seq, d = x.shape
    n = w_in.shape[1]
    ssm_w = n - 3 * att_w
    ts = min(PROJ_ROWS, seq)
    row = lambda b, i: (b, i, 0)
    per_b = lambda b, i: (b, 0, 0)
    kern = functools.partial(_inproj_kernel, att_w=att_w, q_scale=HEAD_DIM ** -0.5 * LOG2_E)
    return pl.pallas_call(
        kern,
        grid=(bsz, seq // ts),
        in_specs=[pl.BlockSpec((1, ts, d), row),
                  pl.BlockSpec((1, 1, d), per_b),
                  pl.BlockSpec((1, 1, d), per_b),
                  pl.BlockSpec((d, n), lambda b, i: (0, 0))],
        out_specs=[pl.BlockSpec((1, ts, att_w), row)] * 3 + [pl.BlockSpec((1, ts, ssm_w), row)],
        out_shape=[jax.ShapeDtypeStruct((bsz, seq, att_w), F32)] * 3
                  + [jax.ShapeDtypeStruct((bsz, seq, ssm_w), F32)],
        compiler_params=_cparams(("parallel", "parallel")),
        name="inproj",
    )(x, scale.reshape(bsz, 1, d), shift.reshape(bsz, 1, d), w_in.astype(BF16))


def _t5_bucket_np(dist):
    exact = N_BUCKETS // 2
    large = exact + (np.log(np.maximum(dist, 1).astype(np.float64) / exact)
                     / math.log(MAX_DISTANCE / exact) * (N_BUCKETS - exact)).astype(np.int64)
    return np.where(dist < exact, dist, np.minimum(large, N_BUCKETS - 1))


def _att_tables(rel_bias):
    qi = np.arange(ATT_BLOCK)[:, None]
    ki = np.arange(2 * ATT_BLOCK)[None, :]
    rel = qi + ATT_BLOCK - ki
    tabs = []
    for window, dil in PATTERNS:
        band = (rel >= 0) & (rel <= window // dil)
        bucket = _t5_bucket_np(np.maximum(rel, 0) * dil)
        onehot = (bucket[:, :, None] == np.arange(N_BUCKETS)[None, None, :]).astype(np.float32)
        bias = jnp.einsum('qkb,bh->hqk', onehot, rel_bias.astype(F32),
                          precision=lax.Precision.HIGHEST)
        full = jnp.where(band[None], bias * LOG2_E, NEG_INF)
        first = jnp.concatenate([full[:, :, ATT_BLOCK:], jnp.full_like(full[:, :, ATT_BLOCK:], NEG_INF)], axis=-1)
        tabs.append(jnp.stack([full, first]))
    return jnp.stack(tabs)


def _att_kernel(q_ref, k_ref, v_ref, tab_ref, o_ref, m_ref, l_ref, *, seq):
    lane = lax.broadcasted_iota(jnp.int32, (ATT_BLOCK, LANES), 1)
    head0 = lane < HEAD_DIM

    def rows(ref, start, n, dil):
        if dil == 1:
            return ref[0, pl.ds(pl.multiple_of(start, ATT_BLOCK), n), :]
        return ref[0, pl.ds(start, n, stride=dil), :]

    def lane_mask(n_rows, h):
        m = lax.broadcasted_iota(jnp.int32, (n_rows, LANES), 1) < HEAD_DIM
        return m if h == 0 else ~m

    def batch(pi, dil, nb, res, blk0, head, init):
        q_rows = nb * ATT_BLOCK
        kv_blocks = nb if head else nb + 1
        q_start = res + dil * ATT_BLOCK * blk0
        kv_start = q_start if head else q_start - dil * ATT_BLOCK
        if dil == 1:
            sl = pl.ds(pl.multiple_of(q_start, ATT_BLOCK), q_rows)
        else:
            sl = pl.ds(q_start, q_rows, stride=dil)
        q = rows(q_ref, q_start, q_rows, dil)
        k = rows(k_ref, kv_start, kv_blocks * ATT_BLOCK, dil).astype(BF16)
        v = rows(v_ref, kv_start, kv_blocks * ATT_BLOCK, dil)
        if not init:
            m_old, l_old, o_old = m_ref[sl, :], l_ref[sl, :], o_ref[0, sl, :]
        qh = [jnp.where(lane_mask(q_rows, h), q, 0.0).astype(BF16) for h in range(2)]
        vh = [jnp.where(lane_mask(kv_blocks * ATT_BLOCK, h), v, 1.0).astype(BF16) for h in range(2)]
        o_parts, l_parts, m_parts = [], [], []
        for j in range(nb):
            first = 1 if (head and j == 0) else 0
            kb = j if (not head or j == 0) else j - 1
            ksl = slice(kb * ATT_BLOCK, (kb + 2) * ATT_BLOCK)
            qsl = slice(j * ATT_BLOCK, (j + 1) * ATT_BLOCK)
            outs, ms = [], []
            for h in range(2):
                s = _dot_nt(qh[h][qsl], k[ksl]) + tab_ref[pi, first, h]
                m = jnp.max(s, axis=1, keepdims=True)
                p = jnp.exp2(s - m).astype(BF16)
                outs.append(_dot(p, vh[h][ksl]))
                ms.append(m)
            o_parts.append(jnp.where(head0, outs[0], outs[1]))
            l_parts.append(pltpu.roll(jnp.where(head0, outs[1], outs[0]), HEAD_DIM, axis=1))
            m_parts.append(jnp.where(head0, ms[0], ms[1]))
        o = jnp.concatenate(o_parts, axis=0)
        l = jnp.concatenate(l_parts, axis=0)
        m = jnp.concatenate(m_parts, axis=0)
        if not init:
            m_new = jnp.maximum(m_old, m)
            a_old = jnp.exp2(m_old - m_new)
            a_cur = jnp.exp2(m - m_new)
            o = o_old * a_old + o * a_cur
            l = l_old * a_old + l * a_cur
            m = m_new
        o_ref[0, sl, :] = o
        m_ref[sl, :] = m
        l_ref[sl, :] = l

    n_units = seq // ATT_BLOCK
    order = sorted(range(len(PATTERNS)), key=lambda p: -PATTERNS[p][1])
    for pos, pi in enumerate(order):
        dil = PATTERNS[pi][1]
        log_d = dil.bit_length() - 1
        res_blocks = seq // (dil * ATT_BLOCK)
        nb = min(ATT_UNITS_PER_STEP, res_blocks)
        per_res = res_blocks // nb
        init = pos == 0

        def head_body(res, carry, pi=pi, dil=dil, nb=nb, init=init):
            batch(pi, dil, nb, res, 0, True, init)
            return carry
        lax.fori_loop(0, dil, head_body, 0)

        if per_res > 1:
            def tail_body(i, carry, pi=pi, dil=dil, nb=nb, log_d=log_d, init=init):
                res = i & (dil - 1)
                blk0 = nb * (1 + (i >> log_d))
                batch(pi, dil, nb, res, blk0, False, init)
                return carry
            lax.fori_loop(0, dil * (per_res - 1), tail_body, 0)

    def finish(i, carry):
        sl = pl.ds(pl.multiple_of(i * ATT_BLOCK, ATT_BLOCK), ATT_BLOCK)
        o_ref[0, sl, :] = o_ref[0, sl, :] / l_ref[sl, :]
        return carry
    lax.fori_loop(0, n_units, finish, 0)


def _attention(q, k, v, tables):
    bsz, seq, att_w = q.shape
    n_pairs = att_w // LANES
    blk = lambda b, p: (b, 0, p)
    qkv_spec = pl.BlockSpec((1, seq, LANES), blk)
    n_pat = tables.shape[0]
    tab_spec = pl.BlockSpec((n_pat, 2, 2, ATT_BLOCK, 2 * ATT_BLOCK), lambda b, p: (0, 0, p, 0, 0))
    return pl.pallas_call(
        functools.partial(_att_kernel, seq=seq),
        grid=(bsz, n_pairs),
        in_specs=[qkv_spec, qkv_spec, qkv_spec, tab_spec],
        out_specs=pl.BlockSpec((1, seq, LANES), blk),
        out_shape=jax.ShapeDtypeStruct((bsz, seq, att_w), F32),
        scratch_shapes=[pltpu.VMEM((seq, LANES), F32), pltpu.VMEM((seq, LANES), F32)],
        compiler_params=_cparams(("parallel", "parallel")),
        name="dilated_attention",
    )(q, k, v, tables)


def _ssm_params(a_re, a_im, b_re, b_im, c_re, c_im, d_skip, log_dt):
    n_g, n_p = a_re.shape
    n_c = b_re.shape[-1]
    n_state = n_g * n_p
    dt = jnp.exp(log_dt.astype(F32))[:, None]

    def a_pow(kk):
        mag = jnp.exp(kk * dt * a_re)
        ph = kk * dt * a_im
        return mag * jnp.cos(ph), mag * jnp.sin(ph)

    ab_re, ab_im = a_pow(1.0)
    nr, ni = ab_re - 1.0, ab_im
    den = a_re * a_re + a_im * a_im
    f_re = (nr * a_re + ni * a_im) / den
    f_im = (ni * a_re - nr * a_im) / den
    bb_re = f_re[:, :, None] * b_re - f_im[:, :, None] * b_im
    bb_im = f_re[:, :, None] * b_im + f_im[:, :, None] * b_re
    eye = jnp.eye(n_g, dtype=F32)
    w_re = jnp.einsum('gpc,gh->gchp', bb_re, eye).reshape(n_g * n_c, n_state)
    w_im = jnp.einsum('gpc,gh->gchp', bb_im, eye).reshape(n_g * n_c, n_state)
    o_re = jnp.einsum('gcp,gh->gphc', c_re, eye).reshape(n_state, n_g * n_c)
    o_im = jnp.einsum('gcp,gh->gphc', c_im, eye).reshape(n_state, n_g * n_c)
    n_kb = (n_g * n_c) // LANES
    sp = n_state // n_kb
    w_in = jnp.stack([jnp.concatenate([w_re[kb * LANES:(kb + 1) * LANES, kb * sp:(kb + 1) * sp],
                                       w_im[kb * LANES:(kb + 1) * LANES, kb * sp:(kb + 1) * sp]], axis=1)
                      for kb in range(n_kb)])
    w_out = jnp.stack([jnp.concatenate([o_re[kb * sp:(kb + 1) * sp, kb * LANES:(kb + 1) * LANES],
                                        -o_im[kb * sp:(kb + 1) * sp, kb * LANES:(kb + 1) * LANES]], axis=0)
                       for kb in range(n_kb)])
    j = jnp.arange(SUBLANES, dtype=F32)[:, None]
    flat = lambda t: jnp.broadcast_to(t.reshape(1, n_state), (SUBLANES, n_state))
    coef = []
    for sh in (1, 2, 4):
        pr, pim = a_pow(float(sh))
        keep = (j >= sh).astype(F32)
        coef += [flat(pr) * keep, flat(pim) * keep]
    a_re_f = jnp.broadcast_to(a_re.reshape(1, n_state), (SUBLANES, n_state))
    a_im_f = jnp.broadcast_to(a_im.reshape(1, n_state), (SUBLANES, n_state))
    dt_f = jnp.broadcast_to(jnp.repeat(dt[:, 0], n_p).reshape(1, n_state), (SUBLANES, n_state))
    mag = jnp.exp((j + 1.0) * dt_f * a_re_f)
    ph = (j + 1.0) * dt_f * a_im_f
    coef += [mag * jnp.cos(ph), mag * jnp.sin(ph)]
    coef = jnp.stack(coef)
    return w_in.astype(BF16), w_out.astype(BF16), coef, d_skip.reshape(1, n_g * n_c).astype(F32)


def _gelu_tanh(x):
    return 0.5 * x * (1.0 + jnp.tanh(math.sqrt(2.0 / math.pi) * (x + 0.044715 * (x * x * x))))


def _ssm_kernel(u_ref, win_ref, wout_ref, coef_ref, d_ref, wglu_ref, bglu_ref, o_ref,
                hre_ref, him_ref, cre_ref, cim_ref, *, rows, n_state, col_w):
    n_kb = win_ref.shape[0]
    sp = n_state // n_kb

    @pl.when(pl.program_id(1) == 0)
    def _():
        cre_ref[...] = jnp.zeros_like(cre_ref)
        cim_ref[...] = jnp.zeros_like(cim_ref)

    u = u_ref[0]
    ub = u.astype(BF16)
    for kb in range(n_kb):
        bu = _dot(ub[:, kb * LANES:(kb + 1) * LANES], win_ref[kb])
        hre_ref[:, kb * sp:(kb + 1) * sp] = bu[:, :sp]
        him_ref[:, kb * sp:(kb + 1) * sp] = bu[:, sp:]

    n_slab = rows // SUBLANES
    for c0 in range(0, n_state, col_w):
        cs = slice(c0, c0 + col_w)

        def body(i, carry, cs=cs):
            c_re, c_im = carry
            sl = pl.ds(pl.multiple_of(i * SUBLANES, SUBLANES), SUBLANES)
            x_re = hre_ref[sl, cs]
            x_im = him_ref[sl, cs]
            for si, sh in enumerate((1, 2, 4)):
                p_re = coef_ref[2 * si, :, cs]
                p_im = coef_ref[2 * si + 1, :, cs]
                s_re = pltpu.roll(x_re, sh, axis=0)
                s_im = pltpu.roll(x_im, sh, axis=0)
                x_re, x_im = (x_re + p_re * s_re - p_im * s_im,
                              x_im + p_re * s_im + p_im * s_re)
            p_re = coef_ref[6, :, cs]
            p_im = coef_ref[7, :, cs]
            h_re = x_re + p_re * c_re - p_im * c_im
            h_im = x_im + p_re * c_im + p_im * c_re
            hre_ref[sl, cs] = h_re
            him_ref[sl, cs] = h_im
            last = slice(SUBLANES - 1, SUBLANES)
            return (jnp.broadcast_to(h_re[last, :], h_re.shape),
                    jnp.broadcast_to(h_im[last, :], h_im.shape))

        c_re, c_im = lax.fori_loop(0, n_slab, body, (cre_ref[:, cs], cim_ref[:, cs]), unroll=True)
        cre_ref[:, cs] = c_re
        cim_ref[:, cs] = c_im

    ys = []
    for kb in range(n_kb):
        hcat = jnp.concatenate([hre_ref[:, kb * sp:(kb + 1) * sp].astype(BF16),
                                him_ref[:, kb * sp:(kb + 1) * sp].astype(BF16)], axis=1)
        ys.append(_dot(hcat, wout_ref[kb]))
    y = jnp.concatenate(ys, axis=1) + d_ref[...] * u
    y = _gelu_tanh(y)
    z = _dot(y.astype(BF16), wglu_ref[...]) + bglu_ref[...]
    o_ref[0] = y * jax.nn.sigmoid(z)


def _ssm(u, w_in, w_out, coef, d_flat, w_glu, b_glu):
    bsz, seq, ssm_w = u.shape
    n_state = coef.shape[-1]
    rows = min(SSM_ROWS, seq)
    full = lambda *shape: pl.BlockSpec(shape, lambda b, i: (0,) * len(shape))
    kern = functools.partial(_ssm_kernel, rows=rows, n_state=n_state, col_w=4 * LANES)
    return pl.pallas_call(
        kern,
        grid=(bsz, seq // rows),
        in_specs=[pl.BlockSpec((1, rows, ssm_w), lambda b, i: (b, i, 0)),
                  full(*w_in.shape), full(*w_out.shape), full(*coef.shape), full(1, ssm_w),
                  full(ssm_w, ssm_w), full(1, ssm_w)],
        out_specs=pl.BlockSpec((1, rows, ssm_w), lambda b, i: (b, i, 0)),
        out_shape=jax.ShapeDtypeStruct((bsz, seq, ssm_w), F32),
        scratch_shapes=[pltpu.VMEM((rows, n_state), F32), pltpu.VMEM((rows, n_state), F32),
                        pltpu.VMEM((SUBLANES, n_state), F32), pltpu.VMEM((SUBLANES, n_state), F32)],
        compiler_params=_cparams(("parallel", "arbitrary")),
        name="s5_glu",
    )(u, w_in, w_out, coef, d_flat, w_glu.astype(BF16), b_glu.reshape(1, ssm_w))


def _layer_norm(y, g, b):
    mu = jnp.mean(y, axis=-1, keepdims=True)
    yc = y - mu
    var = jnp.mean(yc * yc, axis=-1, keepdims=True)
    return yc * lax.rsqrt(var + EPS) * g + b


def _rms_norm(y, g):
    return y * lax.rsqrt(jnp.mean(y * y, axis=-1, keepdims=True) + EPS) * g


def _mix_kernel(x_ref, att_ref, ssm_ref, gatt_ref, gssm_ref, wout_ref, g1_ref, ln_g_ref, ln_b_ref,
                sc_ref, sh_ref, g2_ref, wsg_ref, wsu_ref, wsd_ref, h_ref, ht_ref, pre_ref, *, alpha, att_w):
    a_n = _rms_norm(att_ref[0], gatt_ref[...]).astype(BF16)
    s_n = _rms_norm(ssm_ref[0], gssm_ref[...]).astype(BF16)
    mix = _dot(a_n, wout_ref[0:att_w, :]) + _dot(s_n, wout_ref[att_w:, :])
    x1 = _layer_norm(alpha * x_ref[0] + g1_ref[0] * mix, ln_g_ref[...], ln_b_ref[...])
    h = x1 * (1.0 + sc_ref[0]) + sh_ref[0]
    h_ref[0] = h
    _store_token_tiles(ht_ref, h)
    hb = h.astype(BF16)
    hid = _silu(_dot(hb, wsg_ref[...])) * _dot(hb, wsu_ref[...])
    shared = _dot(hid.astype(BF16), wsd_ref[...])
    pre_ref[0] = alpha * x1 + g2_ref[0] * shared


def _mix(x, att, ssm, g_att, g_ssm, w_out, gate1, ln_g, ln_b, scale2, shift2, gate2,
         w_s_gate, w_s_up, w_s_down, alpha):
    bsz, seq, d = x.shape
    att_w = att.shape[-1]
    ssm_w = ssm.shape[-1]
    ff = w_s_gate.shape[1]
    tm = min(MIX_ROWS, seq)
    n_sub = d // LANES
    row = lambda b, i: (b, i, 0)
    per_b = lambda b, i: (b, 0, 0)
    full = lambda *shape: pl.BlockSpec(shape, lambda b, i: (0,) * len(shape))
    vec = lambda t: t.reshape(bsz, 1, d)
    return pl.pallas_call(
        functools.partial(_mix_kernel, alpha=alpha, att_w=att_w),
        grid=(bsz, seq // tm),
        in_specs=[pl.BlockSpec((1, tm, d), row), pl.BlockSpec((1, tm, att_w), row),
                  pl.BlockSpec((1, tm, ssm_w), row),
                  full(1, att_w), full(1, ssm_w), full(att_w + ssm_w, d),
                  pl.BlockSpec((1, 1, d), per_b), full(1, d), full(1, d),
                  pl.BlockSpec((1, 1, d), per_b), pl.BlockSpec((1, 1, d), per_b),
                  pl.BlockSpec((1, 1, d), per_b),
                  full(d, ff), full(d, ff), full(ff, d)],
        out_specs=[pl.BlockSpec((1, tm, d), row),
                   pl.BlockSpec((tm * n_sub, LANES), lambda b, i: (b * (seq // tm) + i, 0)),
                   pl.BlockSpec((1, tm, d), row)],
        out_shape=[jax.ShapeDtypeStruct((bsz, seq, d), F32),
                   jax.ShapeDtypeStruct((bsz * seq * n_sub, LANES), F32),
                   jax.ShapeDtypeStruct((bsz, seq, d), F32)],
        compiler_params=_cparams(("parallel", "parallel")),
        name="mix_ln1_shared",
    )(x, att, ssm, g_att.reshape(1, att_w), g_ssm.reshape(1, ssm_w), w_out.astype(BF16),
      vec(gate1), ln_g.reshape(1, d), ln_b.reshape(1, d), vec(scale2), vec(shift2), vec(gate2),
      w_s_gate.astype(BF16), w_s_up.astype(BF16), w_s_down.astype(BF16))


def _route_kernel(h_ref, wrt_ref, wrt_lo_ref, bias_ref, tri_ref, idx_ref, gate_ref, rank_ref, cnt_ref, carry_ref):
    n_e = wrt_ref.shape[0]
    tm = h_ref.shape[0]
    per_group = n_e // N_EXPERT_GROUPS
    neg = -jnp.inf

    @pl.when(pl.program_id(0) == 0)
    def _():
        carry_ref[...] = jnp.zeros_like(carry_ref)

    h_hi, h_lo = _split_bf16(h_ref[...])
    logits = (_dot_nt(wrt_ref[...], h_hi) + _dot_nt(wrt_ref[...], h_lo)
              + _dot_nt(wrt_lo_ref[...], h_hi))
    scores = jax.nn.sigmoid(logits)
    biased = scores + bias_ref[...]

    g3 = biased.reshape(N_EXPERT_GROUPS, per_group, tm)
    ridx = lax.broadcasted_iota(jnp.int32, g3.shape, 1).astype(F32)
    m1 = jnp.max(g3, axis=1, keepdims=True)
    first = jnp.min(jnp.where(g3 == m1, ridx, float(per_group)), axis=1, keepdims=True)
    m2 = jnp.max(jnp.where(ridx == first, neg, g3), axis=1, keepdims=True)
    gs = m1 + m2

    gidx = lax.broadcasted_iota(jnp.int32, gs.shape, 0).astype(F32)
    ok = jnp.zeros_like(gs)
    cur = gs
    for _ in range(TOPK_GROUPS):
        mx = jnp.max(cur, axis=0, keepdims=True)
        fi = jnp.min(jnp.where(cur == mx, gidx, float(N_EXPERT_GROUPS)), axis=0, keepdims=True)
        hit = gidx == fi
        ok = jnp.where(hit, 1.0, ok)
        cur = jnp.where(hit, neg, cur)
    masked = jnp.where(ok > 0.5, g3, neg).reshape(n_e, tm)

    eidx = lax.broadcasted_iota(jnp.int32, (n_e, tm), 0).astype(F32)
    onehot = jnp.zeros((n_e, tm), F32)
    cur = masked
    sel_idx = []
    sel_gate = []
    for _ in range(TOP_K):
        mx = jnp.max(cur, axis=0, keepdims=True)
        fi = jnp.min(jnp.where(cur == mx, eidx, float(n_e)), axis=0, keepdims=True)
        hit = eidx == fi
        sel_idx.append(fi)
        sel_gate.append(jnp.sum(jnp.where(hit, scores, 0.0), axis=0, keepdims=True))
        onehot = jnp.where(hit, 1.0, onehot)
        cur = jnp.where(hit, neg, cur)
    idx = jnp.concatenate(sel_idx, axis=0)
    gate = jnp.concatenate(sel_gate, axis=0)
    gate = gate / jnp.sum(gate, axis=0, keepdims=True) * ROUTED_SCALE

    prior = _dot(onehot.astype(BF16), tri_ref[...]) + carry_ref[:, 0:1]
    ranks = [jnp.sum(jnp.where(eidx == sel_idx[k], prior, 0.0), axis=0, keepdims=True)
             for k in range(TOP_K)]
    rank = jnp.concatenate(ranks, axis=0)
    carry = carry_ref[...] + jnp.sum(onehot, axis=1, keepdims=True)
    carry_ref[...] = carry
    cnt_ref[...] = carry

    idx_ref[...] = idx.astype(jnp.int32)
    gate_ref[...] = gate
    rank_ref[...] = rank.astype(jnp.int32)


def _route(h2d, w_router, router_bias):
    n_tok, d = h2d.shape
    n_e = w_router.shape[1]
    tm = min(ROUTE_COLS, n_tok)
    tri = (np.arange(tm)[:, None] < np.arange(tm)[None, :]).astype(np.float32)
    w_hi, w_lo = _split_bf16(w_router.T.astype(F32))
    col = lambda i: (0, i)
    full = lambda *shape: pl.BlockSpec(shape, lambda i: (0,) * len(shape))
    idx, gate, rank, cnt = pl.pallas_call(
        _route_kernel,
        grid=(n_tok // tm,),
        in_specs=[pl.BlockSpec((tm, d), lambda i: (i, 0)), full(n_e, d), full(n_e, d), full(n_e, 1),
                  full(tm, tm)],
        out_specs=[pl.BlockSpec((TOP_K, tm), col), pl.BlockSpec((TOP_K, tm), col),
                   pl.BlockSpec((TOP_K, tm), col), full(n_e, LANES)],
        out_shape=[jax.ShapeDtypeStruct((TOP_K, n_tok), jnp.int32),
                   jax.ShapeDtypeStruct((TOP_K, n_tok), F32),
                   jax.ShapeDtypeStruct((TOP_K, n_tok), jnp.int32),
                   jax.ShapeDtypeStruct((n_e, LANES), F32)],
        scratch_shapes=[pltpu.VMEM((n_e, LANES), F32)],
        compiler_params=_cparams(("arbitrary",)),
        name="router_topk",
    )(h2d, w_hi, w_lo, router_bias.reshape(n_e, 1).astype(F32), jnp.asarray(tri, BF16))
    return idx, gate, rank, cnt[:, 0].astype(jnp.int32)


def _sc_mesh():
    return plsc.VectorSubcoreMesh(core_axis_name="core", subcore_axis_name="subcore")


def _dispatch(h_tiles, gate_rows, dest_windows, n_rows):
    n_tok, n_sub, _ = h_tiles.shape
    win = DISPATCH_WINDOW
    index_spec = pl.BlockSpec((1, TOP_K * win), lambda i: (i, 0))

    def scatter(body, src, src_spec, out_type):
        @pl.kernel(out_type=out_type, mesh=_sc_mesh(), scratch_types=[])
        def run(s_hbm, d_hbm, o_hbm):
            pltpu.emit_pipeline(
                functools.partial(body, o_hbm),
                grid=(n_tok // win,),
                in_specs=[src_spec, index_spec],
                out_specs=[],
                core_axis_name=("core", "subcore"),
                dimension_semantics=(pltpu.PARALLEL,),
            )(s_hbm, d_hbm)
        return run(src, dest_windows)

    def tile_window(o_hbm, x_vmem, d_vmem):
        for k in range(TOP_K):
            pltpu.sync_copy(x_vmem, o_hbm.at[d_vmem.at[0, pl.ds(k * win, win)]])

    def gate_window(o_hbm, g_vmem, d_vmem):
        for k in range(TOP_K):
            pltpu.sync_copy(g_vmem.at[k], o_hbm.at[d_vmem.at[0, pl.ds(k * win, win)]])

    xs = scatter(tile_window, h_tiles, pl.BlockSpec((win, n_sub, LANES), lambda i: (i, 0, 0)),
                 jax.ShapeDtypeStruct((n_rows, n_sub, LANES), h_tiles.dtype))
    row_gates = scatter(gate_window, gate_rows, pl.BlockSpec((TOP_K, win, GATE_LANES), lambda i: (0, i, 0)),
                        jax.ShapeDtypeStruct((n_rows, GATE_LANES), F32))
    return xs, row_gates


def _gather_sum(ys_tiles, dest_windows, n_tok):
    _, n_sub, _ = ys_tiles.shape
    win = DISPATCH_WINDOW

    @pl.kernel(out_type=jax.ShapeDtypeStruct((n_tok, n_sub, LANES), ys_tiles.dtype), mesh=_sc_mesh(),
               scratch_types=[])
    def gather_tiles(y_hbm, d_hbm, o_hbm):
        def window(d_vmem, o_vmem):
            pltpu.sync_copy(y_hbm.at[d_vmem.at[0, pl.ds(0, win)]], o_vmem)
            for k in range(1, TOP_K):
                pltpu.sync_copy(y_hbm.at[d_vmem.at[0, pl.ds(k * win, win)]], o_vmem, add=True)
        pltpu.emit_pipeline(
            window,
            grid=(n_tok // win,),
            in_specs=[pl.BlockSpec((1, TOP_K * win), lambda i: (i, 0))],
            out_specs=[pl.BlockSpec((win, n_sub, LANES), lambda i: (i, 0, 0))],
            core_axis_name=("core", "subcore"),
            dimension_semantics=(pltpu.PARALLEL,),
        )(d_hbm, o_hbm)

    return gather_tiles(ys_tiles, dest_windows)


def _expert_kernel(first_ref, nblk_ref, cnt_ref, nused_ref, xs_ref, rg_ref, wg_ref, wu_ref, wd_ref, ys_ref,
                   xbuf_ref, gbuf_ref, ybuf_ref, wgb_ref, wub_ref, wdb_ref, xsem_ref, ysem_ref):
    e = pl.program_id(0)
    n_used = nused_ref[0]
    d = wgb_ref.shape[0]
    blk_rows = xbuf_ref.shape[1]

    class _Pair:
        def __init__(self, g, slot):
            src = xs_ref.at[pl.ds(pl.multiple_of(g * blk_rows, blk_rows), blk_rows)]
            gsrc = rg_ref.at[pl.ds(pl.multiple_of(g * MOE_ROWS, MOE_ROWS), MOE_ROWS)]
            self.copies = (pltpu.make_async_copy(src, xbuf_ref.at[slot], xsem_ref.at[slot]),
                           pltpu.make_async_copy(gsrc, gbuf_ref.at[slot], xsem_ref.at[slot]))

        def start(self):
            for cp in self.copies:
                cp.start()

        def wait(self):
            for cp in self.copies:
                cp.wait()

    def x_copy(g, slot):
        return _Pair(g, slot)

    def y_copy(g, slot):
        dst = ys_ref.at[pl.ds(pl.multiple_of(g * blk_rows, blk_rows), blk_rows)]
        return pltpu.make_async_copy(ybuf_ref.at[slot], dst, ysem_ref.at[slot])

    n_slot = xbuf_ref.shape[0]

    @pl.when(e == 0)
    def _():
        for g in range(n_slot - 1):
            @pl.when(g < n_used)
            def _(g=g):
                x_copy(g, g).start()

    @pl.when(nblk_ref[e] > 0)
    def _():
        wgb_ref[...] = wg_ref[0].astype(BF16)
        wub_ref[...] = wu_ref[0].astype(BF16)
        wdb_ref[...] = wd_ref[0].astype(BF16)

    def block(j, carry):
        g = first_ref[e] + j
        slot = g & (n_slot - 1)
        x_copy(g, slot).wait()

        @pl.when(g + n_slot - 1 < n_used)
        def _():
            x_copy(g + n_slot - 1, (g + n_slot - 1) & (n_slot - 1)).start()

        @pl.when(g >= n_slot)
        def _():
            y_copy(g - n_slot, slot).wait()

        x = jnp.concatenate(_load_token_tiles(xbuf_ref.at[slot], MOE_ROWS, d), axis=1)
        row = lax.broadcasted_iota(jnp.int32, (MOE_ROWS, 1), 0)
        xb = jnp.where(row < cnt_ref[e] - j * MOE_ROWS, x, 0.0).astype(BF16)
        hid = _silu(_dot(xb, wgb_ref[...])) * _dot(xb, wub_ref[...])
        y = _dot(hid.astype(BF16), wdb_ref[...]) * gbuf_ref[slot][:, 0:1]
        _store_token_tiles(ybuf_ref.at[slot], jnp.where(row < cnt_ref[e] - j * MOE_ROWS, y, 0.0))
        y_copy(g, slot).start()
        return carry
    lax.fori_loop(0, nblk_ref[e], block, 0)

    @pl.when(e == pl.num_programs(0) - 1)
    def _():
        for back in range(1, n_slot + 1):
            @pl.when(n_used >= back)
            def _(back=back):
                y_copy(n_used - back, (n_used - back) & (n_slot - 1)).wait()


def _experts(xs_tiles, row_gates, plan, w_gate, w_up, w_down):
    n_rows, n_sub, _ = xs_tiles.shape
    d = n_sub * LANES
    n_e, _, ff = w_gate.shape
    blk_rows = MOE_ROWS * n_sub
    hbm = pl.BlockSpec(memory_space=pl.ANY)
    grid_spec = pltpu.PrefetchScalarGridSpec(
        num_scalar_prefetch=4,
        grid=(n_e,),
        in_specs=[hbm, hbm,
                  pl.BlockSpec((1, d, ff), lambda e, *_: (e, 0, 0)),
                  pl.BlockSpec((1, d, ff), lambda e, *_: (e, 0, 0)),
                  pl.BlockSpec((1, ff, d), lambda e, *_: (e, 0, 0))],
        out_specs=hbm,
        scratch_shapes=[pltpu.VMEM((EXPERT_SLOTS, blk_rows, LANES), F32),
                        pltpu.VMEM((EXPERT_SLOTS, MOE_ROWS, GATE_LANES), F32),
                        pltpu.VMEM((EXPERT_SLOTS, blk_rows, LANES), F32),
                        pltpu.VMEM((d, ff), BF16), pltpu.VMEM((d, ff), BF16), pltpu.VMEM((ff, d), BF16),
                        pltpu.SemaphoreType.DMA((EXPERT_SLOTS,)), pltpu.SemaphoreType.DMA((EXPERT_SLOTS,))],
    )
    ys = pl.pallas_call(
        _expert_kernel,
        grid_spec=grid_spec,
        out_shape=jax.ShapeDtypeStruct((n_rows * n_sub, LANES), F32),
        compiler_params=_cparams(("arbitrary",)),
        name="moe_experts",
    )(*plan, xs_tiles.reshape(n_rows * n_sub, LANES), row_gates, w_gate, w_up, w_down)
    return ys.reshape(n_rows, n_sub, LANES)


def _combine_kernel(dest_ref, ys_ref, pre_ref, g2_ref, ln_g_ref, ln_b_ref, o_ref, buf_ref, sem_ref):
    tm, d = pre_ref.shape[1], pre_ref.shape[2]
    n_sub = d // LANES

    def row_copy(t, k, src_row):
        slot = pl.multiple_of((k * tm + t) * n_sub, n_sub)
        return pltpu.make_async_copy(ys_ref.at[src_row], buf_ref.at[pl.ds(slot, n_sub)], sem_ref.at[0])

    def issue(g, c):
        t0 = g * DMA_ISSUE_UNROLL
        rows = [[dest_ref[(t0 + j) * TOP_K + k] for k in range(TOP_K)] for j in range(DMA_ISSUE_UNROLL)]
        for j in range(DMA_ISSUE_UNROLL):
            for k in range(TOP_K):
                row_copy(t0 + j, k, rows[j][k]).start(priority=k % 2)
        return c
    lax.fori_loop(0, tm // DMA_ISSUE_UNROLL, issue, 0)

    def drain(t, c):
        for k in range(TOP_K):
            row_copy(t, k, dest_ref[t * TOP_K + k]).wait()
        return c
    lax.fori_loop(0, tm, drain, 0)

    pieces = []
    for s in range(n_sub):
        acc = None
        for k in range(TOP_K):
            term = buf_ref[pl.ds(k * tm * n_sub + s, tm, stride=n_sub), :]
            acc = term if acc is None else acc + term
        pieces.append(acc)
    routed = jnp.concatenate(pieces, axis=1)
    y = pre_ref[0] + g2_ref[0] * routed
    o_ref[0] = _layer_norm(y, ln_g_ref[...], ln_b_ref[...])


def _combine(ys, dest, pre, gate2, ln_g, ln_b, n_tiles):
    bsz, seq, d = pre.shape
    tm = min(DISPATCH_ROWS, seq)
    per_seq = seq // tm
    row = lambda i, *_: (i // per_seq, i % per_seq, 0)
    per_b = lambda i, *_: (i // per_seq, 0, 0)
    full = lambda *shape: pl.BlockSpec(shape, lambda i, *_: (0,) * len(shape))
    grid_spec = pltpu.PrefetchScalarGridSpec(
        num_scalar_prefetch=0,
        grid=(n_tiles,),
        in_specs=[pl.BlockSpec((TOP_K * tm,), lambda i: (i,), memory_space=pltpu.SMEM),
                  pl.BlockSpec(memory_space=pl.ANY),
                  pl.BlockSpec((1, tm, d), row), pl.BlockSpec((1, 1, d), per_b), full(1, d), full(1, d)],
        out_specs=pl.BlockSpec((1, tm, d), row),
        scratch_shapes=[pltpu.VMEM((TOP_K * tm * (d // LANES), LANES), F32), pltpu.SemaphoreType.DMA((1,))],
    )
    return pl.pallas_call(
        _combine_kernel,
        grid_spec=grid_spec,
        out_shape=jax.ShapeDtypeStruct((bsz, seq, d), F32),
        compiler_params=_cparams(("arbitrary",)),
        name="moe_combine_ln2",
    )(dest, ys, pre, gate2.reshape(bsz, 1, d), ln_g.reshape(1, d), ln_b.reshape(1, d))


def _final_ln_kernel(r_ref, pre_ref, g2_ref, ln_g_ref, ln_b_ref, *rest):
    o_ref = rest[-1]
    tm, d = pre_ref.shape[1], pre_ref.shape[2]
    routed = jnp.concatenate(_load_token_tiles(r_ref, tm, d), axis=1)
    y = pre_ref[0] + g2_ref[0] * routed
    o_ref[0] = _layer_norm(y, ln_g_ref[...], ln_b_ref[...])


def _final_ln(routed_tiles, pre, gate2, ln_g, ln_b, first_tile, partial=None):
    bsz, seq, d = pre.shape
    tm = min(DISPATCH_ROWS, seq)
    per_seq = seq // tm
    n_sub = d // LANES
    n_tiles = bsz * per_seq - first_tile
    row = lambda i: ((i + first_tile) // per_seq, (i + first_tile) % per_seq, 0)
    per_b = lambda i: ((i + first_tile) // per_seq, 0, 0)
    full = lambda *shape: pl.BlockSpec(shape, lambda i: (0,) * len(shape))
    in_specs = [pl.BlockSpec((tm * n_sub, LANES), lambda i: (i, 0)),
                pl.BlockSpec((1, tm, d), row), pl.BlockSpec((1, 1, d), per_b), full(1, d), full(1, d)]
    args = [routed_tiles.reshape(-1, LANES), pre, gate2.reshape(bsz, 1, d), ln_g.reshape(1, d),
            ln_b.reshape(1, d)]
    aliases = {}
    if partial is not None:
        in_specs.append(pl.BlockSpec(memory_space=pl.ANY))
        args.append(partial)
        aliases = {len(args) - 1: 0}
    return pl.pallas_call(
        _final_ln_kernel,
        grid=(n_tiles,),
        in_specs=in_specs,
        out_specs=pl.BlockSpec((1, tm, d), row),
        out_shape=jax.ShapeDtypeStruct((bsz, seq, d), F32),
        input_output_aliases=aliases,
        compiler_params=_cparams(("arbitrary",)),
        name="moe_final_ln2",
    )(*args)


def _dest_kernel(pstart_ref, idx_ref, rank_ref, dest_ref):
    idx = idx_ref[...]

    def body(e, acc):
        return acc + jnp.where(idx == e, pstart_ref[e], 0)
    dest_ref[...] = lax.fori_loop(0, pstart_ref.shape[0], body, rank_ref[...], unroll=8)


def _dest_rows(pstart, idx, rank):
    n_k, n_tok = idx.shape
    tn = min(4096, n_tok)
    grid_spec = pltpu.PrefetchScalarGridSpec(
        num_scalar_prefetch=1,
        grid=(n_tok // tn,),
        in_specs=[pl.BlockSpec((n_k, tn), lambda i, *_: (0, i))] * 2,
        out_specs=pl.BlockSpec((n_k, tn), lambda i, *_: (0, i)),
    )
    return pl.pallas_call(
        _dest_kernel,
        grid_spec=grid_spec,
        out_shape=jax.ShapeDtypeStruct((n_k, n_tok), jnp.int32),
        compiler_params=_cparams(("arbitrary",)),
        name="moe_dest",
    )(pstart, idx, rank)


def _moe_plan(idx, rank, cnt, n_tok):
    n_e = cnt.shape[0]
    padded = (cnt + MOE_ROWS - 1) // MOE_ROWS * MOE_ROWS
    pend = jnp.cumsum(padded).astype(jnp.int32)
    pstart = pend - padded
    dest_kt = _dest_rows(pstart, idx, rank)
    dest = dest_kt.T.reshape(-1)
    win = min(DISPATCH_WINDOW, n_tok)
    dest_windows = dest_kt.reshape(TOP_K, n_tok // win, win).transpose(1, 0, 2).reshape(n_tok // win, TOP_K * win)
    n_blk = n_tok * TOP_K // MOE_ROWS + n_e
    n_used = (pend[-1:] // MOE_ROWS).astype(jnp.int32)
    plan = ((pstart // MOE_ROWS).astype(jnp.int32), (padded // MOE_ROWS).astype(jnp.int32),
            cnt.astype(jnp.int32), n_used)
    return dest, dest_windows, plan, n_blk * MOE_ROWS


def kernel(x, c, rel_bias, w_ada, b_ada, w_in, ssm_a_re, ssm_a_im, ssm_b_re, ssm_b_im, ssm_c_re, ssm_c_im, ssm_d, ssm_log_dt, w_glu, b_glu, g_att, g_ssm, w_out, ln1_g, ln1_b, w_router, router_bias, w_e_gate, w_e_up, w_e_down, w_s_gate, w_s_up, w_s_down, ln2_g, ln2_b):
    bsz, seq, d = x.shape
    depth = w_ada.shape[0]
    alpha = (2 * depth) ** 0.25
    att_w = g_att.shape[-1]
    tables = _att_tables(rel_bias)
    for layer in range(depth):
        ada = _ada(c, w_ada[layer], b_ada[layer])
        shift1, scale1, gate1, shift2, scale2, gate2 = jnp.split(ada, 6, axis=-1)
        q, k, v, u = _inproj(x, scale1, shift1, w_in[layer], att_w)
        att = _attention(q, k, v, tables)
        ssm_prm = _ssm_params(ssm_a_re[layer], ssm_a_im[layer], ssm_b_re[layer], ssm_b_im[layer],
                              ssm_c_re[layer], ssm_c_im[layer], ssm_d[layer], ssm_log_dt[layer])
        ssm = _ssm(u, *ssm_prm, w_glu[layer], b_glu[layer])
        h2, h2_tiles, pre = _mix(x, att, ssm, g_att[layer], g_ssm[layer], w_out[layer], gate1,
                                 ln1_g[layer], ln1_b[layer], scale2, shift2, gate2, w_s_gate[layer],
                                 w_s_up[layer], w_s_down[layer], alpha)
        h2d = h2.reshape(bsz * seq, d)
        idx, gate, rank, cnt = _route(h2d, w_router[layer], router_bias[layer])
        dest, dest_windows, plan, n_rows = _moe_plan(idx, rank, cnt, bsz * seq)
        n_tok = bsz * seq
        gate_rows = jnp.broadcast_to(gate[:, :, None], (TOP_K, n_tok, GATE_LANES))
        xs, row_gates = _dispatch(h2_tiles.reshape(n_tok, d // LANES, LANES), gate_rows, dest_windows, n_rows)
        ys = _experts(xs, row_gates, plan, w_e_gate[layer], w_e_up[layer], w_e_down[layer])
        tm = min(DISPATCH_ROWS, seq)
        tc_tiles = min(COMBINE_TC_TILES, n_tok // tm)
        tc_tokens = tc_tiles * tm
        win = min(DISPATCH_WINDOW, n_tok)
        partial = (_combine(ys, dest, pre, gate2, ln2_g[layer], ln2_b[layer], tc_tiles)
                   if tc_tiles > 0 else None)
        routed = _gather_sum(ys, dest_windows[tc_tokens // win:], n_tok - tc_tokens)
        x = _final_ln(routed, pre, gate2, ln2_g[layer], ln2_b[layer], tc_tiles, partial)
    return x
```

```python
import functools
import math

import jax
import jax.numpy as jnp
import numpy as np
from jax import lax
from jax.experimental import pallas as pl
from jax.experimental.pallas import tpu as pltpu
from jax.experimental.pallas import tpu_sc as plsc

F32 = jnp.float32
BF16 = jnp.bfloat16

HEAD_DIM = 64
ATT_BLOCK = 128
PATTERNS = ((128, 1), (512, 4), (2048, 16))
N_BUCKETS = 32
MAX_DISTANCE = 2048
SSM_GROUP = 16
SSM_STATE = 64
N_EXPERTS = 256
TOP_K = 8
N_EXPERT_GROUPS = 8
TOPK_GROUPS = 4
ROUTED_SCALE = 2.5
EPS = 1e-5
NEG_INF = -1e30
LOG2_E = math.log2(math.e)

LANES = 128
SUBLANES = 8
VMEM_LIMIT_BYTES = 56 * 1024 * 1024

PROJ_ROWS = 512
SSM_ROWS = 256
MIX_ROWS = 512
ROUTE_COLS = 512
MOE_ROWS = 256
DISPATCH_ROWS = 256
COMBINE_TC_TILES = 48
GATE_LANES = LANES
DISPATCH_WINDOW = 32
ATT_UNITS_PER_STEP = 8
DMA_ISSUE_UNROLL = 4
EXPERT_SLOTS = 4


def _cparams(sem, vmem=VMEM_LIMIT_BYTES):
    return pltpu.CompilerParams(dimension_semantics=sem, vmem_limit_bytes=vmem)


def _dot(a, b):
    return jnp.dot(a, b, preferred_element_type=F32)


def _dot_nt(a, b):
    return lax.dot_general(a, b, (((1,), (1,)), ((), ())), preferred_element_type=F32)


def _silu(x):
    return x * jax.nn.sigmoid(x)


def _store_token_tiles(ref, x):
    m, d = x.shape
    n_sub = d // LANES
    for s in range(n_sub):
        ref[pl.ds(s, m, stride=n_sub), :] = x[:, s * LANES:(s + 1) * LANES]


def _load_token_tiles(ref, m, d, row0=0):
    n_sub = d // LANES
    return [ref[pl.ds(row0 + s, m, stride=n_sub), :] for s in range(n_sub)]


def _split_bf16(x):
    hi = x.astype(BF16)
    lo = (x - hi.astype(F32)).astype(BF16)
    return hi, lo


def _ada_kernel(c_ref, w_ref, b_ref, o_ref):
    c = c_ref[...]
    a_hi, a_lo = _split_bf16(_silu(c))
    w_hi, w_lo = _split_bf16(w_ref[...])
    acc = _dot(a_hi, w_hi) + _dot(a_hi, w_lo) + _dot(a_lo, w_hi)
    o_ref[...] = acc + b_ref[...]


def _ada(c, w_ada, b_ada):
    bsz, d = c.shape
    n = w_ada.shape[1]
    rows = SUBLANES
    c_pad = jnp.zeros((rows, d), F32).at[:bsz].set(c)
    tn = 1024
    out = pl.pallas_call(
        _ada_kernel,
        grid=(n // tn,),
        in_specs=[pl.BlockSpec((rows, d), lambda j: (0, 0)),
                  pl.BlockSpec((d, tn), lambda j: (0, j)),
                  pl.BlockSpec((1, tn), lambda j: (0, j))],
        out_specs=pl.BlockSpec((rows, tn), lambda j: (0, j)),
        out_shape=jax.ShapeDtypeStruct((rows, n), F32),
        compiler_params=_cparams(("parallel",)),
        name="ada",
    )(c_pad, w_ada, b_ada.reshape(1, n))
    return out[:bsz]


def _inproj_kernel(x_ref, sc_ref, sh_ref, w_ref, q_ref, k_ref, v_ref, u_ref, *, att_w, q_scale):
    h = (x_ref[0] * (1.0 + sc_ref[0]) + sh_ref[0]).astype(BF16)
    q_ref[0] = _dot(h, w_ref[:, 0:att_w]) * q_scale
    k_ref[0] = _dot(h, w_ref[:, att_w:2 * att_w])
    v_ref[0] = _dot(h, w_ref[:, 2 * att_w:3 * att_w])
    u_ref[0] = _dot(h, w_ref[:, 3 * att_w:])


def _inproj(x, scale, shift, w_in, att_w):
    bsz, seq, d = x.shape
    n = w_in.shape[1]
    ssm_w = n - 3 * att_w
    ts = min(PROJ_ROWS, seq)
    row = lambda b, i: (b, i, 0)
    per_b = lambda b, i: (b, 0, 0)
    kern = functools.partial(_inproj_kernel, att_w=att_w, q_scale=HEAD_DIM ** -0.5 * LOG2_E)
    return pl.pallas_call(
        kern,
        grid=(bsz, seq // ts),
        in_specs=[pl.BlockSpec((1, ts, d), row),
                  pl.BlockSpec((1, 1, d), per_b),
                  pl.BlockSpec((1, 1, d), per_b),
                  pl.BlockSpec((d, n), lambda b, i: (0, 0))],
        out_specs=[pl.BlockSpec((1, ts, att_w), row)] * 3 + [pl.BlockSpec((1, ts, ssm_w), row)],
        out_shape=[jax.ShapeDtypeStruct((bsz, seq, att_w), F32)] * 3
                  + [jax.ShapeDtypeStruct((bsz, seq, ssm_w), F32)],
        compiler_params=_cparams(("parallel", "parallel")),
        name="inproj",
    )(x, scale.reshape(bsz, 1, d), shift.reshape(bsz, 1, d), w_in.astype(BF16))


def _t5_bucket_np(dist):
    exact = N_BUCKETS // 2
    large = exact + (np.log(np.maximum(dist, 1).astype(np.float64) / exact)
                     / math.log(MAX_DISTANCE / exact) * (N_BUCKETS - exact)).astype(np.int64)
    return np.where(dist < exact, dist, np.minimum(large, N_BUCKETS - 1))


def _att_tables(rel_bias):
    qi = np.arange(ATT_BLOCK)[:, None]
    ki = np.arange(2 * ATT_BLOCK)[None, :]
    rel = qi + ATT_BLOCK - ki
    tabs = []
    for window, dil in PATTERNS:
        band = (rel >= 0) & (rel <= window // dil)
        bucket = _t5_bucket_np(np.maximum(rel, 0) * dil)
        onehot = (bucket[:, :, None] == np.arange(N_BUCKETS)[None, None, :]).astype(np.float32)
        bias = jnp.einsum('qkb,bh->hqk', onehot, rel_bias.astype(F32),
                          precision=lax.Precision.HIGHEST)
        full = jnp.where(band[None], bias * LOG2_E, NEG_INF)
        first = jnp.concatenate([full[:, :, ATT_BLOCK:], jnp.full_like(full[:, :, ATT_BLOCK:], NEG_INF)], axis=-1)
        tabs.append(jnp.stack([full, first]))
    return jnp.stack(tabs)


def _att_kernel(q_ref, k_ref, v_ref, tab_ref, o_ref, m_ref, l_ref, *, seq):
    lane = lax.broadcasted_iota(jnp.int32, (ATT_BLOCK, LANES), 1)
    head0 = lane < HEAD_DIM

    def rows(ref, start, n, dil):
        if dil == 1:
            return ref[0, pl.ds(pl.multiple_of(start, ATT_BLOCK), n), :]
        return ref[0, pl.ds(start, n, stride=dil), :]

    def lane_mask(n_rows, h):
        m = lax.broadcasted_iota(jnp.int32, (n_rows, LANES), 1) < HEAD_DIM
        return m if h == 0 else ~m

    def batch(pi, dil, nb, res, blk0, head, init):
        q_rows = nb * ATT_BLOCK
        kv_blocks = nb if head else nb + 1
        q_start = res + dil * ATT_BLOCK * blk0
        kv_start = q_start if head else q_start - dil * ATT_BLOCK
        if dil == 1:
            sl = pl.ds(pl.multiple_of(q_start, ATT_BLOCK), q_rows)
        else:
            sl = pl.ds(q_start, q_rows, stride=dil)
        q = rows(q_ref, q_start, q_rows, dil)
        k = rows(k_ref, kv_start, kv_blocks * ATT_BLOCK, dil).astype(BF16)
        v = rows(v_ref, kv_start, kv_blocks * ATT_BLOCK, dil)
        if not init:
            m_old, l_old, o_old = m_ref[sl, :], l_ref[sl, :], o_ref[0, sl, :]
        qh = [jnp.where(lane_mask(q_rows, h), q, 0.0).astype(BF16) for h in range(2)]
        vh = [jnp.where(lane_mask(kv_blocks * ATT_BLOCK, h), v, 1.0).astype(BF16) for h in range(2)]
        o_parts, l_parts, m_parts = [], [], []
        for j in range(nb):
            first = 1 if (head and j == 0) else 0
            kb = j if (not head or j == 0) else j - 1
            ksl = slice(kb * ATT_BLOCK, (kb + 2) * ATT_BLOCK)
            qsl = slice(j * ATT_BLOCK, (j + 1) * ATT_BLOCK)
            outs, ms = [], []
            for h in range(2):
                s = _dot_nt(qh[h][qsl], k[ksl]) + tab_ref[pi, first, h]
                m = jnp.max(s, axis=1, keepdims=True)
                p = jnp.exp2(s - m).astype(BF16)
                outs.append(_dot(p, vh[h][ksl]))
                ms.append(m)
            o_parts.append(jnp.where(head0, outs[0], outs[1]))
            l_parts.append(pltpu.roll(jnp.where(head0, outs[1], outs[0]), HEAD_DIM, axis=1))
            m_parts.append(jnp.where(head0, ms[0], ms[1]))
        o = jnp.concatenate(o_parts, axis=0)
        l = jnp.concatenate(l_parts, axis=0)
        m = jnp.concatenate(m_parts, axis=0)
        if not init:
            m_new = jnp.maximum(m_old, m)
            a_old = jnp.exp2(m_old - m_new)
            a_cur = jnp.exp2(m - m_new)
            o = o_old * a_old + o * a_cur
            l = l_old * a_old + l * a_cur
            m = m_new
        o_ref[0, sl, :] = o
        m_ref[sl, :] = m
        l_ref[sl, :] = l

    n_units = seq // ATT_BLOCK
    order = sorted(range(len(PATTERNS)), key=lambda p: -PATTERNS[p][1])
    for pos, pi in enumerate(order):
        dil = PATTERNS[pi][1]
        log_d = dil.bit_length() - 1
        res_blocks = seq // (dil * ATT_BLOCK)
        nb = min(ATT_UNITS_PER_STEP, res_blocks)
        per_res = res_blocks // nb
        init = pos == 0

        res_step = 2 if (nb < ATT_UNITS_PER_STEP and dil % 2 == 0) else 1

        def head_body(r, carry, pi=pi, dil=dil, nb=nb, init=init, res_step=res_step):
            for dr in range(res_step):
                batch(pi, dil, nb, r * res_step + dr, 0, True, init)
            return carry
        lax.fori_loop(0, dil // res_step, head_body, 0)

        if per_res > 1:
            def tail_body(i, carry, pi=pi, dil=dil, nb=nb, log_d=log_d, init=init):
                res = i & (dil - 1)
                blk0 = nb * (1 + (i >> log_d))
                batch(pi, dil, nb, res, blk0, False, init)
                return carry
            lax.fori_loop(0, dil * (per_res - 1), tail_body, 0)

    def finish(i, carry):
        sl = pl.ds(pl.multiple_of(i * ATT_BLOCK, ATT_BLOCK), ATT_BLOCK)
        o_ref[0, sl, :] = o_ref[0, sl, :] / l_ref[sl, :]
        return carry
    lax.fori_loop(0, n_units, finish, 0)


def _attention(q, k, v, tables):
    bsz, seq, att_w = q.shape
    n_pairs = att_w // LANES
    blk = lambda b, p: (b, 0, p)
    qkv_spec = pl.BlockSpec((1, seq, LANES), blk)
    n_pat = tables.shape[0]
    tab_spec = pl.BlockSpec((n_pat, 2, 2, ATT_BLOCK, 2 * ATT_BLOCK), lambda b, p: (0, 0, p, 0, 0))
    return pl.pallas_call(
        functools.partial(_att_kernel, seq=seq),
        grid=(bsz, n_pairs),
        in_specs=[qkv_spec, qkv_spec, qkv_spec, tab_spec],
        out_specs=pl.BlockSpec((1, seq, LANES), blk),
        out_shape=jax.ShapeDtypeStruct((bsz, seq, att_w), F32),
        scratch_shapes=[pltpu.VMEM((seq, LANES), F32), pltpu.VMEM((seq, LANES), F32)],
        compiler_params=_cparams(("parallel", "parallel")),
        name="dilated_attention",
    )(q, k, v, tables)


def _ssm_params(a_re, a_im, b_re, b_im, c_re, c_im, d_skip, log_dt):
    n_g, n_p = a_re.shape
    n_c = b_re.shape[-1]
    n_state = n_g * n_p
    dt = jnp.exp(log_dt.astype(F32))[:, None]

    def a_pow(kk):
        mag = jnp.exp(kk * dt * a_re)
        ph = kk * dt * a_im
        return mag * jnp.cos(ph), mag * jnp.sin(ph)

    ab_re, ab_im = a_pow(1.0)
    nr, ni = ab_re - 1.0, ab_im
    den = a_re * a_re + a_im * a_im
    f_re = (nr * a_re + ni * a_im) / den
    f_im = (ni * a_re - nr * a_im) / den
    bb_re = f_re[:, :, None] * b_re - f_im[:, :, None] * b_im
    bb_im = f_re[:, :, None] * b_im + f_im[:, :, None] * b_re
    eye = jnp.eye(n_g, dtype=F32)
    w_re = jnp.einsum('gpc,gh->gchp', bb_re, eye).reshape(n_g * n_c, n_state)
    w_im = jnp.einsum('gpc,gh->gchp', bb_im, eye).reshape(n_g * n_c, n_state)
    o_re = jnp.einsum('gcp,gh->gphc', c_re, eye).reshape(n_state, n_g * n_c)
    o_im = jnp.einsum('gcp,gh->gphc', c_im, eye).reshape(n_state, n_g * n_c)
    n_kb = (n_g * n_c) // LANES
    sp = n_state // n_kb
    w_in = jnp.stack([jnp.concatenate([w_re[kb * LANES:(kb + 1) * LANES, kb * sp:(kb + 1) * sp],
                                       w_im[kb * LANES:(kb + 1) * LANES, kb * sp:(kb + 1) * sp]], axis=1)
                      for kb in range(n_kb)])
    w_out = jnp.stack([jnp.concatenate([o_re[kb * sp:(kb + 1) * sp, kb * LANES:(kb + 1) * LANES],
                                        -o_im[kb * sp:(kb + 1) * sp, kb * LANES:(kb + 1) * LANES]], axis=0)
                       for kb in range(n_kb)])
    j = jnp.arange(SUBLANES, dtype=F32)[:, None]
    flat = lambda t: jnp.broadcast_to(t.reshape(1, n_state), (SUBLANES, n_state))
    coef = []
    for sh in (1, 2, 4):
        pr, pim = a_pow(float(sh))
        keep = (j >= sh).astype(F32)
        coef += [flat(pr) * keep, flat(pim) * keep]
    a_re_f = jnp.broadcast_to(a_re.reshape(1, n_state), (SUBLANES, n_state))
    a_im_f = jnp.broadcast_to(a_im.reshape(1, n_state), (SUBLANES, n_state))
    dt_f = jnp.broadcast_to(jnp.repeat(dt[:, 0], n_p).reshape(1, n_state), (SUBLANES, n_state))
    mag = jnp.exp((j + 1.0) * dt_f * a_re_f)
    ph = (j + 1.0) * dt_f * a_im_f
    coef += [mag * jnp.cos(ph), mag * jnp.sin(ph)]
    coef = jnp.stack(coef)
    return w_in.astype(BF16), w_out.astype(BF16), coef, d_skip.reshape(1, n_g * n_c).astype(F32)


def _gelu_tanh(x):
    return 0.5 * x * (1.0 + jnp.tanh(math.sqrt(2.0 / math.pi) * (x + 0.044715 * (x * x * x))))


def _ssm_kernel(u_ref, win_ref, wout_ref, coef_ref, d_ref, wglu_ref, bglu_ref, o_ref,
                hre_ref, him_ref, cre_ref, cim_ref, *, rows, n_state, col_w):
    n_kb = win_ref.shape[0]
    sp = n_state // n_kb

    @pl.when(pl.program_id(1) == 0)
    def _():
        cre_ref[...] = jnp.zeros_like(cre_ref)
        cim_ref[...] = jnp.zeros_like(cim_ref)

    u = u_ref[0]
    ub = u.astype(BF16)
    for kb in range(n_kb):
        bu = _dot(ub[:, kb * LANES:(kb + 1) * LANES], win_ref[kb])
        hre_ref[:, kb * sp:(kb + 1) * sp] = bu[:, :sp]
        him_ref[:, kb * sp:(kb + 1) * sp] = bu[:, sp:]

    n_slab = rows // SUBLANES
    for c0 in range(0, n_state, col_w):
        cs = slice(c0, c0 + col_w)

        def body(i, carry, cs=cs):
            c_re, c_im = carry
            sl = pl.ds(pl.multiple_of(i * SUBLANES, SUBLANES), SUBLANES)
            x_re = hre_ref[sl, cs]
            x_im = him_ref[sl, cs]
            for si, sh in enumerate((1, 2, 4)):
                p_re = coef_ref[2 * si, :, cs]
                p_im = coef_ref[2 * si + 1, :, cs]
                s_re = pltpu.roll(x_re, sh, axis=0)
                s_im = pltpu.roll(x_im, sh, axis=0)
                x_re, x_im = (x_re + p_re * s_re - p_im * s_im,
                              x_im + p_re * s_im + p_im * s_re)
            p_re = coef_ref[6, :, cs]
            p_im = coef_ref[7, :, cs]
            h_re = x_re + p_re * c_re - p_im * c_im
            h_im = x_im + p_re * c_im + p_im * c_re
            hre_ref[sl, cs] = h_re
            him_ref[sl, cs] = h_im
            last = slice(SUBLANES - 1, SUBLANES)
            return (jnp.broadcast_to(h_re[last, :], h_re.shape),
                    jnp.broadcast_to(h_im[last, :], h_im.shape))

        c_re, c_im = lax.fori_loop(0, n_slab, body, (cre_ref[:, cs], cim_ref[:, cs]), unroll=True)
        cre_ref[:, cs] = c_re
        cim_ref[:, cs] = c_im

    ys = []
    for kb in range(n_kb):
        hcat = jnp.concatenate([hre_ref[:, kb * sp:(kb + 1) * sp].astype(BF16),
                                him_ref[:, kb * sp:(kb + 1) * sp].astype(BF16)], axis=1)
        ys.append(_dot(hcat, wout_ref[kb]))
    y = jnp.concatenate(ys, axis=1) + d_ref[...] * u
    y = _gelu_tanh(y)
    z = _dot(y.astype(BF16), wglu_ref[...]) + bglu_ref[...]
    o_ref[0] = y * jax.nn.sigmoid(z)


def _ssm(u, w_in, w_out, coef, d_flat, w_glu, b_glu):
    bsz, seq, ssm_w = u.shape
    n_state = coef.shape[-1]
    rows = min(SSM_ROWS, seq)
    full = lambda *shape: pl.BlockSpec(shape, lambda b, i: (0,) * len(shape))
    kern = functools.partial(_ssm_kernel, rows=rows, n_state=n_state, col_w=4 * LANES)
    return pl.pallas_call(
        kern,
        grid=(bsz, seq // rows),
        in_specs=[pl.BlockSpec((1, rows, ssm_w), lambda b, i: (b, i, 0)),
                  full(*w_in.shape), full(*w_out.shape), full(*coef.shape), full(1, ssm_w),
                  full(ssm_w, ssm_w), full(1, ssm_w)],
        out_specs=pl.BlockSpec((1, rows, ssm_w), lambda b, i: (b, i, 0)),
        out_shape=jax.ShapeDtypeStruct((bsz, seq, ssm_w), F32),
        scratch_shapes=[pltpu.VMEM((rows, n_state), F32), pltpu.VMEM((rows, n_state), F32),
                        pltpu.VMEM((SUBLANES, n_state), F32), pltpu.VMEM((SUBLANES, n_state), F32)],
        compiler_params=_cparams(("parallel", "arbitrary")),
        name="s5_glu",
    )(u, w_in, w_out, coef, d_flat, w_glu.astype(BF16), b_glu.reshape(1, ssm_w))


def _layer_norm(y, g, b):
    mu = jnp.mean(y, axis=-1, keepdims=True)
    yc = y - mu
    var = jnp.mean(yc * yc, axis=-1, keepdims=True)
    return yc * lax.rsqrt(var + EPS) * g + b


def _rms_norm(y, g):
    return y * lax.rsqrt(jnp.mean(y * y, axis=-1, keepdims=True) + EPS) * g


def _mix_kernel(x_ref, att_ref, ssm_ref, gatt_ref, gssm_ref, wout_ref, g1_ref, ln_g_ref, ln_b_ref,
                sc_ref, sh_ref, g2_ref, wsg_ref, wsu_ref, wsd_ref, h_ref, ht_ref, pre_ref, *, alpha, att_w):
    a_n = _rms_norm(att_ref[0], gatt_ref[...]).astype(BF16)
    s_n = _rms_norm(ssm_ref[0], gssm_ref[...]).astype(BF16)
    mix = _dot(a_n, wout_ref[0:att_w, :]) + _dot(s_n, wout_ref[att_w:, :])
    x1 = _layer_norm(alpha * x_ref[0] + g1_ref[0] * mix, ln_g_ref[...], ln_b_ref[...])
    h = x1 * (1.0 + sc_ref[0]) + sh_ref[0]
    h_ref[0] = h
    _store_token_tiles(ht_ref, h)
    hb = h.astype(BF16)
    hid = _silu(_dot(hb, wsg_ref[...])) * _dot(hb, wsu_ref[...])
    shared = _dot(hid.astype(BF16), wsd_ref[...])
    pre_ref[0] = alpha * x1 + g2_ref[0] * shared


def _mix(x, att, ssm, g_att, g_ssm, w_out, gate1, ln_g, ln_b, scale2, shift2, gate2,
         w_s_gate, w_s_up, w_s_down, alpha):
    bsz, seq, d = x.shape
    att_w = att.shape[-1]
    ssm_w = ssm.shape[-1]
    ff = w_s_gate.shape[1]
    tm = min(MIX_ROWS, seq)
    n_sub = d // LANES
    row = lambda b, i: (b, i, 0)
    per_b = lambda b, i: (b, 0, 0)
    full = lambda *shape: pl.BlockSpec(shape, lambda b, i: (0,) * len(shape))
    vec = lambda t: t.reshape(bsz, 1, d)
    return pl.pallas_call(
        functools.partial(_mix_kernel, alpha=alpha, att_w=att_w),
        grid=(bsz, seq // tm),
        in_specs=[pl.BlockSpec((1, tm, d), row), pl.BlockSpec((1, tm, att_w), row),
                  pl.BlockSpec((1, tm, ssm_w), row),
                  full(1, att_w), full(1, ssm_w), full(att_w + ssm_w, d),
                  pl.BlockSpec((1, 1, d), per_b), full(1, d), full(1, d),
                  pl.BlockSpec((1, 1, d), per_b), pl.BlockSpec((1, 1, d), per_b),
                  pl.BlockSpec((1, 1, d), per_b),
                  full(d, ff), full(d, ff), full(ff, d)],
        out_specs=[pl.BlockSpec((1, tm, d), row),
                   pl.BlockSpec((tm * n_sub, LANES), lambda b, i: (b * (seq // tm) + i, 0)),
                   pl.BlockSpec((1, tm, d), row)],
        out_shape=[jax.ShapeDtypeStruct((bsz, seq, d), F32),
                   jax.ShapeDtypeStruct((bsz * seq * n_sub, LANES), F32),
                   jax.ShapeDtypeStruct((bsz, seq, d), F32)],
        compiler_params=_cparams(("parallel", "parallel")),
        name="mix_ln1_shared",
    )(x, att, ssm, g_att.reshape(1, att_w), g_ssm.reshape(1, ssm_w), w_out.astype(BF16),
      vec(gate1), ln_g.reshape(1, d), ln_b.reshape(1, d), vec(scale2), vec(shift2), vec(gate2),
      w_s_gate.astype(BF16), w_s_up.astype(BF16), w_s_down.astype(BF16))


def _route_kernel(h_ref, wrt_ref, wrt_lo_ref, bias_ref, tri_ref, idx_ref, gate_ref, rank_ref, cnt_ref, carry_ref):
    n_e = wrt_ref.shape[0]
    tm = h_ref.shape[0]
    per_group = n_e // N_EXPERT_GROUPS
    neg = -jnp.inf

    @pl.when(pl.program_id(0) == 0)
    def _():
        carry_ref[...] = jnp.zeros_like(carry_ref)

    h_hi, h_lo = _split_bf16(h_ref[...])
    logits = (_dot_nt(wrt_ref[...], h_hi) + _dot_nt(wrt_ref[...], h_lo)
              + _dot_nt(wrt_lo_ref[...], h_hi))
    scores = jax.nn.sigmoid(logits)
    biased = scores + bias_ref[...]

    g3 = biased.reshape(N_EXPERT_GROUPS, per_group, tm)
    ridx = lax.broadcasted_iota(jnp.int32, g3.shape, 1).astype(F32)
    m1 = jnp.max(g3, axis=1, keepdims=True)
    first = jnp.min(jnp.where(g3 == m1, ridx, float(per_group)), axis=1, keepdims=True)
    m2 = jnp.max(jnp.where(ridx == first, neg, g3), axis=1, keepdims=True)
    gs = m1 + m2

    gidx = lax.broadcasted_iota(jnp.int32, gs.shape, 0).astype(F32)
    ok = jnp.zeros_like(gs)
    cur = gs
    for _ in range(TOPK_GROUPS):
        mx = jnp.max(cur, axis=0, keepdims=True)
        fi = jnp.min(jnp.where(cur == mx, gidx, float(N_EXPERT_GROUPS)), axis=0, keepdims=True)
        hit = gidx == fi
        ok = jnp.where(hit, 1.0, ok)
        cur = jnp.where(hit, neg, cur)
    masked = jnp.where(ok > 0.5, g3, neg).reshape(n_e, tm)

    eidx = lax.broadcasted_iota(jnp.int32, (n_e, tm), 0).astype(F32)
    onehot = jnp.zeros((n_e, tm), F32)
    cur = masked
    sel_idx = []
    sel_gate = []
    for _ in range(TOP_K):
        mx = jnp.max(cur, axis=0, keepdims=True)
        fi = jnp.min(jnp.where(cur == mx, eidx, float(n_e)), axis=0, keepdims=True)
        hit = eidx == fi
        sel_idx.append(fi)
        sel_gate.append(jnp.sum(jnp.where(hit, scores, 0.0), axis=0, keepdims=True))
        onehot = jnp.where(hit, 1.0, onehot)
        cur = jnp.where(hit, neg, cur)
    idx = jnp.concatenate(sel_idx, axis=0)
    gate = jnp.concatenate(sel_gate, axis=0)
    gate = gate / jnp.sum(gate, axis=0, keepdims=True) * ROUTED_SCALE

    prior = _dot(onehot.astype(BF16), tri_ref[...]) + carry_ref[:, 0:1]
    ranks = [jnp.sum(jnp.where(eidx == sel_idx[k], prior, 0.0), axis=0, keepdims=True)
             for k in range(TOP_K)]
    rank = jnp.concatenate(ranks, axis=0)
    carry = carry_ref[...] + jnp.sum(onehot, axis=1, keepdims=True)
    carry_ref[...] = carry
    cnt_ref[...] = carry

    idx_ref[...] = idx.astype(jnp.int32)
    gate_ref[...] = gate
    rank_ref[...] = rank.astype(jnp.int32)


def _route(h2d, w_router, router_bias):
    n_tok, d = h2d.shape
    n_e = w_router.shape[1]
    tm = min(ROUTE_COLS, n_tok)
    tri = (np.arange(tm)[:, None] < np.arange(tm)[None, :]).astype(np.float32)
    w_hi, w_lo = _split_bf16(w_router.T.astype(F32))
    col = lambda i: (0, i)
    full = lambda *shape: pl.BlockSpec(shape, lambda i: (0,) * len(shape))
    idx, gate, rank, cnt = pl.pallas_call(
        _route_kernel,
        grid=(n_tok // tm,),
        in_specs=[pl.BlockSpec((tm, d), lambda i: (i, 0)), full(n_e, d), full(n_e, d), full(n_e, 1),
                  full(tm, tm)],
        out_specs=[pl.BlockSpec((TOP_K, tm), col), pl.BlockSpec((TOP_K, tm), col),
                   pl.BlockSpec((TOP_K, tm), col), full(n_e, LANES)],
        out_shape=[jax.ShapeDtypeStruct((TOP_K, n_tok), jnp.int32),
                   jax.ShapeDtypeStruct((TOP_K, n_tok), F32),
                   jax.ShapeDtypeStruct((TOP_K, n_tok), jnp.int32),
                   jax.ShapeDtypeStruct((n_e, LANES), F32)],
        scratch_shapes=[pltpu.VMEM((n_e, LANES), F32)],
        compiler_params=_cparams(("arbitrary",)),
        name="router_topk",
    )(h2d, w_hi, w_lo, router_bias.reshape(n_e, 1).astype(F32), jnp.asarray(tri, BF16))
    return idx, gate, rank, cnt[:, 0].astype(jnp.int32)


def _sc_mesh():
    return plsc.VectorSubcoreMesh(core_axis_name="core", subcore_axis_name="subcore")


def _dispatch(h_tiles, gate_rows, dest_windows, n_rows):
    n_tok, n_sub, _ = h_tiles.shape
    win = DISPATCH_WINDOW
    index_spec = pl.BlockSpec((1, TOP_K * win), lambda i: (i, 0))

    def scatter(body, src, src_spec, out_type):
        @pl.kernel(out_type=out_type, mesh=_sc_mesh(), scratch_types=[])
        def run(s_hbm, d_hbm, o_hbm):
            pltpu.emit_pipeline(
                functools.partial(body, o_hbm),
                grid=(n_tok // win,),
                in_specs=[src_spec, index_spec],
                out_specs=[],
                core_axis_name=("core", "subcore"),
                dimension_semantics=(pltpu.PARALLEL,),
            )(s_hbm, d_hbm)
        return run(src, dest_windows)

    def tile_window(o_hbm, x_vmem, d_vmem):
        for k in range(TOP_K):
            pltpu.sync_copy(x_vmem, o_hbm.at[d_vmem.at[0, pl.ds(k * win, win)]])

    def gate_window(o_hbm, g_vmem, d_vmem):
        for k in range(TOP_K):
            pltpu.sync_copy(g_vmem.at[k], o_hbm.at[d_vmem.at[0, pl.ds(k * win, win)]])

    xs = scatter(tile_window, h_tiles, pl.BlockSpec((win, n_sub, LANES), lambda i: (i, 0, 0)),
                 jax.ShapeDtypeStruct((n_rows, n_sub, LANES), h_tiles.dtype))
    row_gates = scatter(gate_window, gate_rows, pl.BlockSpec((TOP_K, win, GATE_LANES), lambda i: (0, i, 0)),
                        jax.ShapeDtypeStruct((n_rows, GATE_LANES), F32))
    return xs, row_gates


def _gather_sum(ys_tiles, dest_windows, n_tok):
    _, n_sub, _ = ys_tiles.shape
    win = DISPATCH_WINDOW

    @pl.kernel(out_type=jax.ShapeDtypeStruct((n_tok, n_sub, LANES), ys_tiles.dtype), mesh=_sc_mesh(),
               scratch_types=[])
    def gather_tiles(y_hbm, d_hbm, o_hbm):
        def window(d_vmem, o_vmem):
            pltpu.sync_copy(y_hbm.at[d_vmem.at[0, pl.ds(0, win)]], o_vmem)
            for k in range(1, TOP_K):
                pltpu.sync_copy(y_hbm.at[d_vmem.at[0, pl.ds(k * win, win)]], o_vmem, add=True)
        pltpu.emit_pipeline(
            window,
            grid=(n_tok // win,),
            in_specs=[pl.BlockSpec((1, TOP_K * win), lambda i: (i, 0))],
            out_specs=[pl.BlockSpec((win, n_sub, LANES), lambda i: (i, 0, 0))],
            core_axis_name=("core", "subcore"),
            dimension_semantics=(pltpu.PARALLEL,),
        )(d_hbm, o_hbm)

    return gather_tiles(ys_tiles, dest_windows)


def _expert_kernel(first_ref, nblk_ref, cnt_ref, nused_ref, xs_ref, rg_ref, wg_ref, wu_ref, wd_ref, ys_ref,
                   xbuf_ref, gbuf_ref, ybuf_ref, wgb_ref, wub_ref, wdb_ref, xsem_ref, ysem_ref):
    e = pl.program_id(0)
    n_used = nused_ref[0]
    d = wgb_ref.shape[0]
    blk_rows = xbuf_ref.shape[1]

    class _Pair:
        def __init__(self, g, slot):
            src = xs_ref.at[pl.ds(pl.multiple_of(g * blk_rows, blk_rows), blk_rows)]
            gsrc = rg_ref.at[pl.ds(pl.multiple_of(g * MOE_ROWS, MOE_ROWS), MOE_ROWS)]
            self.copies = (pltpu.make_async_copy(src, xbuf_ref.at[slot], xsem_ref.at[slot]),
                           pltpu.make_async_copy(gsrc, gbuf_ref.at[slot], xsem_ref.at[slot]))

        def start(self):
            for cp in self.copies:
                cp.start()

        def wait(self):
            for cp in self.copies:
                cp.wait()

    def x_copy(g, slot):
        return _Pair(g, slot)

    def y_copy(g, slot):
        dst = ys_ref.at[pl.ds(pl.multiple_of(g * blk_rows, blk_rows), blk_rows)]
        return pltpu.make_async_copy(ybuf_ref.at[slot], dst, ysem_ref.at[slot])

    n_slot = xbuf_ref.shape[0]

    @pl.when(e == 0)
    def _():
        for g in range(n_slot - 1):
            @pl.when(g < n_used)
            def _(g=g):
                x_copy(g, g).start()

    @pl.when(nblk_ref[e] > 0)
    def _():
        wgb_ref[...] = wg_ref[0].astype(BF16)
        wub_ref[...] = wu_ref[0].astype(BF16)
        wdb_ref[...] = wd_ref[0].astype(BF16)

    def block(j, carry):
        g = first_ref[e] + j
        slot = g & (n_slot - 1)
        x_copy(g, slot).wait()

        @pl.when(g + n_slot - 1 < n_used)
        def _():
            x_copy(g + n_slot - 1, (g + n_slot - 1) & (n_slot - 1)).start()

        @pl.when(g >= n_slot)
        def _():
            y_copy(g - n_slot, slot).wait()

        x = jnp.concatenate(_load_token_tiles(xbuf_ref.at[slot], MOE_ROWS, d), axis=1)
        row = lax.broadcasted_iota(jnp.int32, (MOE_ROWS, 1), 0)
        xb = jnp.where(row < cnt_ref[e] - j * MOE_ROWS, x, 0.0).astype(BF16)
        hid = _silu(_dot(xb, wgb_ref[...])) * _dot(xb, wub_ref[...])
        y = _dot(hid.astype(BF16), wdb_ref[...]) * gbuf_ref[slot][:, 0:1]
        _store_token_tiles(ybuf_ref.at[slot], jnp.where(row < cnt_ref[e] - j * MOE_ROWS, y, 0.0))
        y_copy(g, slot).start()
        return carry
    lax.fori_loop(0, nblk_ref[e], block, 0)

    @pl.when(e == pl.num_programs(0) - 1)
    def _():
        for back in range(1, n_slot + 1):
            @pl.when(n_used >= back)
            def _(back=back):
                y_copy(n_used - back, (n_used - back) & (n_slot - 1)).wait()


def _experts(xs_tiles, row_gates, plan, w_gate, w_up, w_down):
    n_rows, n_sub, _ = xs_tiles.shape
    d = n_sub * LANES
    n_e, _, ff = w_gate.shape
    blk_rows = MOE_ROWS * n_sub
    hbm = pl.BlockSpec(memory_space=pl.ANY)
    grid_spec = pltpu.PrefetchScalarGridSpec(
        num_scalar_prefetch=4,
        grid=(n_e,),
        in_specs=[hbm, hbm,
                  pl.BlockSpec((1, d, ff), lambda e, *_: (e, 0, 0)),
                  pl.BlockSpec((1, d, ff), lambda e, *_: (e, 0, 0)),
                  pl.BlockSpec((1, ff, d), lambda e, *_: (e, 0, 0))],
        out_specs=hbm,
        scratch_shapes=[pltpu.VMEM((EXPERT_SLOTS, blk_rows, LANES), F32),
                        pltpu.VMEM((EXPERT_SLOTS, MOE_ROWS, GATE_LANES), F32),
                        pltpu.VMEM((EXPERT_SLOTS, blk_rows, LANES), F32),
                        pltpu.VMEM((d, ff), BF16), pltpu.VMEM((d, ff), BF16), pltpu.VMEM((ff, d), BF16),
                        pltpu.SemaphoreType.DMA((EXPERT_SLOTS,)), pltpu.SemaphoreType.DMA((EXPERT_SLOTS,))],
    )
    ys = pl.pallas_call(
        _expert_kernel,
        grid_spec=grid_spec,
        out_shape=jax.ShapeDtypeStruct((n_rows * n_sub, LANES), F32),
        compiler_params=_cparams(("arbitrary",)),
        name="moe_experts",
    )(*plan, xs_tiles.reshape(n_rows * n_sub, LANES), row_gates, w_gate, w_up, w_down)
    return ys.reshape(n_rows, n_sub, LANES)


def _combine_kernel(dest_ref, ys_ref, pre_ref, g2_ref, ln_g_ref, ln_b_ref, o_ref, buf_ref, sem_ref):
    tm, d = pre_ref.shape[1], pre_ref.shape[2]
    n_sub = d // LANES

    def row_copy(t, k, src_row):
        slot = pl.multiple_of((k * tm + t) * n_sub, n_sub)
        return pltpu.make_async_copy(ys_ref.at[src_row], buf_ref.at[pl.ds(slot, n_sub)], sem_ref.at[0])

    def issue(g, c):
        t0 = g * DMA_ISSUE_UNROLL
        rows = [[dest_ref[(t0 + j) * TOP_K + k] for k in range(TOP_K)] for j in range(DMA_ISSUE_UNROLL)]
        for j in range(DMA_ISSUE_UNROLL):
            for k in range(TOP_K):
                row_copy(t0 + j, k, rows[j][k]).start(priority=k % 2)
        return c
    lax.fori_loop(0, tm // DMA_ISSUE_UNROLL, issue, 0)

    def drain(t, c):
        for k in range(TOP_K):
            row_copy(t, k, dest_ref[t * TOP_K + k]).wait()
        return c
    lax.fori_loop(0, tm, drain, 0)

    pieces = []
    for s in range(n_sub):
        acc = None
        for k in range(TOP_K):
            term = buf_ref[pl.ds(k * tm * n_sub + s, tm, stride=n_sub), :]
            acc = term if acc is None else acc + term
        pieces.append(acc)
    routed = jnp.concatenate(pieces, axis=1)
    y = pre_ref[0] + g2_ref[0] * routed
    o_ref[0] = _layer_norm(y, ln_g_ref[...], ln_b_ref[...])


def _combine(ys, dest, pre, gate2, ln_g, ln_b, n_tiles):
    bsz, seq, d = pre.shape
    tm = min(DISPATCH_ROWS, seq)
    per_seq = seq // tm
    row = lambda i, *_: (i // per_seq, i % per_seq, 0)
    per_b = lambda i, *_: (i // per_seq, 0, 0)
    full = lambda *shape: pl.BlockSpec(shape, lambda i, *_: (0,) * len(shape))
    grid_spec = pltpu.PrefetchScalarGridSpec(
        num_scalar_prefetch=0,
        grid=(n_tiles,),
        in_specs=[pl.BlockSpec((TOP_K * tm,), lambda i: (i,), memory_space=pltpu.SMEM),
                  pl.BlockSpec(memory_space=pl.ANY),
                  pl.BlockSpec((1, tm, d), row), pl.BlockSpec((1, 1, d), per_b), full(1, d), full(1, d)],
        out_specs=pl.BlockSpec((1, tm, d), row),
        scratch_shapes=[pltpu.VMEM((TOP_K * tm * (d // LANES), LANES), F32), pltpu.SemaphoreType.DMA((1,))],
    )
    return pl.pallas_call(
        _combine_kernel,
        grid_spec=grid_spec,
        out_shape=jax.ShapeDtypeStruct((bsz, seq, d), F32),
        compiler_params=_cparams(("arbitrary",)),
        name="moe_combine_ln2",
    )(dest, ys, pre, gate2.reshape(bsz, 1, d), ln_g.reshape(1, d), ln_b.reshape(1, d))


def _final_ln_kernel(r_ref, pre_ref, g2_ref, ln_g_ref, ln_b_ref, *rest):
    o_ref = rest[-1]
    tm, d = pre_ref.shape[1], pre_ref.shape[2]
    routed = jnp.concatenate(_load_token_tiles(r_ref, tm, d), axis=1)
    y = pre_ref[0] + g2_ref[0] * routed
    o_ref[0] = _layer_norm(y, ln_g_ref[...], ln_b_ref[...])


def _final_ln(routed_tiles, pre, gate2, ln_g, ln_b, first_tile, partial=None):
    bsz, seq, d = pre.shape
    tm = min(DISPATCH_ROWS, seq)
    per_seq = seq // tm
    n_sub = d // LANES
    n_tiles = bsz * per_seq - first_tile
    row = lambda i: ((i + first_tile) // per_seq, (i + first_tile) % per_seq, 0)
    per_b = lambda i: ((i + first_tile) // per_seq, 0, 0)
    full = lambda *shape: pl.BlockSpec(shape, lambda i: (0,) * len(shape))
    in_specs = [pl.BlockSpec((tm * n_sub, LANES), lambda i: (i, 0)),
                pl.BlockSpec((1, tm, d), row), pl.BlockSpec((1, 1, d), per_b), full(1, d), full(1, d)]
    args = [routed_tiles.reshape(-1, LANES), pre, gate2.reshape(bsz, 1, d), ln_g.reshape(1, d),
            ln_b.reshape(1, d)]
    aliases = {}
    if partial is not None:
        in_specs.append(pl.BlockSpec(memory_space=pl.ANY))
        args.append(partial)
        aliases = {len(args) - 1: 0}
    return pl.pallas_call(
        _final_ln_kernel,
        grid=(n_tiles,),
        in_specs=in_specs,
        out_specs=pl.BlockSpec((1, tm, d), row),
        out_shape=jax.ShapeDtypeStruct((bsz, seq, d), F32),
        input_output_aliases=aliases,
        compiler_params=_cparams(("arbitrary",)),
        name="moe_final_ln2",
    )(*args)


def _dest_kernel(pstart_ref, idx_ref, rank_ref, dest_ref):
    idx = idx_ref[...]

    def body(e, acc):
        return acc + jnp.where(idx == e, pstart_ref[e], 0)
    dest_ref[...] = lax.fori_loop(0, pstart_ref.shape[0], body, rank_ref[...], unroll=8)


def _dest_rows(pstart, idx, rank):
    n_k, n_tok = idx.shape
    tn = min(4096, n_tok)
    grid_spec = pltpu.PrefetchScalarGridSpec(
        num_scalar_prefetch=1,
        grid=(n_tok // tn,),
        in_specs=[pl.BlockSpec((n_k, tn), lambda i, *_: (0, i))] * 2,
        out_specs=pl.BlockSpec((n_k, tn), lambda i, *_: (0, i)),
    )
    return pl.pallas_call(
        _dest_kernel,
        grid_spec=grid_spec,
        out_shape=jax.ShapeDtypeStruct((n_k, n_tok), jnp.int32),
        compiler_params=_cparams(("arbitrary",)),
        name="moe_dest",
    )(pstart, idx, rank)


def _moe_plan(idx, rank, cnt, n_tok):
    n_e = cnt.shape[0]
    padded = (cnt + MOE_ROWS - 1) // MOE_ROWS * MOE_ROWS
    pend = jnp.cumsum(padded).astype(jnp.int32)
    pstart = pend - padded
    dest_kt = _dest_rows(pstart, idx, rank)
    dest = dest_kt.T.reshape(-1)
    win = min(DISPATCH_WINDOW, n_tok)
    dest_windows = dest_kt.reshape(TOP_K, n_tok // win, win).transpose(1, 0, 2).reshape(n_tok // win, TOP_K * win)
    n_blk = n_tok * TOP_K // MOE_ROWS + n_e
    n_used = (pend[-1:] // MOE_ROWS).astype(jnp.int32)
    plan = ((pstart // MOE_ROWS).astype(jnp.int32), (padded // MOE_ROWS).astype(jnp.int32),
            cnt.astype(jnp.int32), n_used)
    return dest, dest_windows, plan, n_blk * MOE_ROWS


def kernel(x, c, rel_bias, w_ada, b_ada, w_in, ssm_a_re, ssm_a_im, ssm_b_re, ssm_b_im, ssm_c_re, ssm_c_im, ssm_d, ssm_log_dt, w_glu, b_glu, g_att, g_ssm, w_out, ln1_g, ln1_b, w_router, router_bias, w_e_gate, w_e_up, w_e_down, w_s_gate, w_s_up, w_s_down, ln2_g, ln2_b):
    bsz, seq, d = x.shape
    depth = w_ada.shape[0]
    alpha = (2 * depth) ** 0.25
    att_w = g_att.shape[-1]
    tables = _att_tables(rel_bias)
    for layer in range(depth):
        ada = _ada(c, w_ada[layer], b_ada[layer])
        shift1, scale1, gate1, shift2, scale2, gate2 = jnp.split(ada, 6, axis=-1)
        q, k, v, u = _inproj(x, scale1, shift1, w_in[layer], att_w)
        att = _attention(q, k, v, tables)
        ssm_prm = _ssm_params(ssm_a_re[layer], ssm_a_im[layer], ssm_b_re[layer], ssm_b_im[layer],
                              ssm_c_re[layer], ssm_c_im[layer], ssm_d[layer], ssm_log_dt[layer])
        ssm = _ssm(u, *ssm_prm, w_glu[layer], b_glu[layer])
        h2, h2_tiles, pre = _mix(x, att, ssm, g_att[layer], g_ssm[layer], w_out[layer], gate1,
                                 ln1_g[layer], ln1_b[layer], scale2, shift2, gate2, w_s_gate[layer],
                                 w_s_up[layer], w_s_down[layer], alpha)
        h2d = h2.reshape(bsz * seq, d)
        idx, gate, rank, cnt = _route(h2d, w_router[layer], router_bias[layer])
        dest, dest_windows, plan, n_rows = _moe_plan(idx, rank, cnt, bsz * seq)
        n_tok = bsz * seq
        gate_rows = jnp.broadcast_to(gate[:, :, None], (TOP_K, n_tok, GATE_LANES))
        xs, row_gates = _dispatch(h2_tiles.reshape(n_tok, d // LANES, LANES), gate_rows, dest_windows, n_rows)
        ys = _experts(xs, row_gates, plan, w_e_gate[layer], w_e_up[layer], w_e_down[layer])
        tm = min(DISPATCH_ROWS, seq)
        tc_tiles = min(COMBINE_TC_TILES, n_tok // tm)
        tc_tokens = tc_tiles * tm
        win = min(DISPATCH_WINDOW, n_tok)
        partial = (_combine(ys, dest, pre, gate2, ln2_g[layer], ln2_b[layer], tc_tiles)
                   if tc_tiles > 0 else None)
        routed = _gather_sum(ys, dest_windows[tc_tokens // win:], n_tok - tc_tokens)
        x = _final_ln(routed, pre, gate2, ln2_g[layer], ln2_b[layer], tc_tiles, partial)
    return x
```

```python
import functools
import math

import jax
import jax.numpy as jnp
import numpy as np
from jax import lax
from jax.experimental import pallas as pl
from jax.experimental.pallas import tpu as pltpu
from jax.experimental.pallas import tpu_sc as plsc

F32 = jnp.float32
BF16 = jnp.bfloat16

HEAD_DIM = 64
ATT_BLOCK = 128
PATTERNS = ((128, 1), (512, 4), (2048, 16))
N_BUCKETS = 32
MAX_DISTANCE = 2048
SSM_GROUP = 16
SSM_STATE = 64
N_EXPERTS = 256
TOP_K = 8
N_EXPERT_GROUPS = 8
TOPK_GROUPS = 4
ROUTED_SCALE = 2.5
EPS = 1e-5
NEG_INF = -1e30
LOG2_E = math.log2(math.e)

LANES = 128
SUBLANES = 8
VMEM_LIMIT_BYTES = 56 * 1024 * 1024

PROJ_ROWS = 512
SSM_ROWS = 256
MIX_ROWS = 512
ROUTE_COLS = 512
MOE_ROWS = 256
DISPATCH_ROWS = 256
COMBINE_TC_TILES = 54
GATE_LANES = LANES
DISPATCH_WINDOW = 32
ATT_UNITS_PER_STEP = 8
DMA_ISSUE_UNROLL = 4
EXPERT_SLOTS = 4


def _cparams(sem, vmem=VMEM_LIMIT_BYTES):
    return pltpu.CompilerParams(dimension_semantics=sem, vmem_limit_bytes=vmem)


def _dot(a, b):
    return jnp.dot(a, b, preferred_element_type=F32)


def _dot_nt(a, b):
    return lax.dot_general(a, b, (((1,), (1,)), ((), ())), preferred_element_type=F32)


def _silu(x):
    return x * jax.nn.sigmoid(x)


def _store_token_tiles(ref, x):
    m, d = x.shape
    n_sub = d // LANES
    for s in range(n_sub):
        ref[pl.ds(s, m, stride=n_sub), :] = x[:, s * LANES:(s + 1) * LANES]


def _load_token_tiles(ref, m, d, row0=0):
    n_sub = d // LANES
    return [ref[pl.ds(row0 + s, m, stride=n_sub), :] for s in range(n_sub)]


def _split_bf16(x):
    hi = x.astype(BF16)
    lo = (x - hi.astype(F32)).astype(BF16)
    return hi, lo


def _ada_kernel(c_ref, w_ref, b_ref, o_ref):
    c = c_ref[...]
    a_hi, a_lo = _split_bf16(_silu(c))
    w_hi, w_lo = _split_bf16(w_ref[...])
    acc = _dot(a_hi, w_hi) + _dot(a_hi, w_lo) + _dot(a_lo, w_hi)
    o_ref[...] = acc + b_ref[...]


def _ada(c, w_ada, b_ada):
    bsz, d = c.shape
    n = w_ada.shape[1]
    rows = SUBLANES
    c_pad = jnp.zeros((rows, d), F32).at[:bsz].set(c)
    tn = 1024
    out = pl.pallas_call(
        _ada_kernel,
        grid=(n // tn,),
        in_specs=[pl.BlockSpec((rows, d), lambda j: (0, 0)),
                  pl.BlockSpec((d, tn), lambda j: (0, j)),
                  pl.BlockSpec((1, tn), lambda j: (0, j))],
        out_specs=pl.BlockSpec((rows, tn), lambda j: (0, j)),
        out_shape=jax.ShapeDtypeStruct((rows, n), F32),
        compiler_params=_cparams(("parallel",)),
        name="ada",
    )(c_pad, w_ada, b_ada.reshape(1, n))
    return out[:bsz]


def _inproj_kernel(x_ref, sc_ref, sh_ref, w_ref, q_ref, k_ref, v_ref, u_ref, *, att_w, q_scale):
    h = (x_ref[0] * (1.0 + sc_ref[0]) + sh_ref[0]).astype(BF16)
    q_ref[0] = _dot(h, w_ref[:, 0:att_w]) * q_scale
    k_ref[0] = _dot(h, w_ref[:, att_w:2 * att_w])
    v_ref[0] = _dot(h, w_ref[:, 2 * att_w:3 * att_w])
    u_ref[0] = _dot(h, w_ref[:, 3 * att_w:])


def _inproj(x, scale, shift, w_in, att_w):
    bsz, seq, d = x.shape
    n = w_in.shape[1]
    ssm_w = n - 3 * att_w
    ts = min(PROJ_ROWS, seq)
    row = lambda b, i: (b, i, 0)
    per_b = lambda b, i: (b, 0, 0)
    kern = functools.partial(_inproj_kernel, att_w=att_w, q_scale=HEAD_DIM ** -0.5 * LOG2_E)
    return pl.pallas_call(
        kern,
        grid=(bsz, seq // ts),
        in_specs=[pl.BlockSpec((1, ts, d), row),
                  pl.BlockSpec((1, 1, d), per_b),
                  pl.BlockSpec((1, 1, d), per_b),
                  pl.BlockSpec((d, n), lambda b, i: (0, 0))],
        out_specs=[pl.BlockSpec((1, ts, att_w), row)] * 3 + [pl.BlockSpec((1, ts, ssm_w), row)],
        out_shape=[jax.ShapeDtypeStruct((bsz, seq, att_w), F32)] * 3
                  + [jax.ShapeDtypeStruct((bsz, seq, ssm_w), F32)],
        compiler_params=_cparams(("parallel", "parallel")),
        name="inproj",
    )(x, scale.reshape(bsz, 1, d), shift.reshape(bsz, 1, d), w_in.astype(BF16))


def _t5_bucket_np(dist):
    exact = N_BUCKETS // 2
    large = exact + (np.log(np.maximum(dist, 1).astype(np.float64) / exact)
                     / math.log(MAX_DISTANCE / exact) * (N_BUCKETS - exact)).astype(np.int64)
    return np.where(dist < exact, dist, np.minimum(large, N_BUCKETS - 1))


def _att_tables(rel_bias):
    qi = np.arange(ATT_BLOCK)[:, None]
    ki = np.arange(2 * ATT_BLOCK)[None, :]
    rel = qi + ATT_BLOCK - ki
    tabs = []
    for window, dil in PATTERNS:
        band = (rel >= 0) & (rel <= window // dil)
        bucket = _t5_bucket_np(np.maximum(rel, 0) * dil)
        onehot = (bucket[:, :, None] == np.arange(N_BUCKETS)[None, None, :]).astype(np.float32)
        bias = jnp.einsum('qkb,bh->hqk', onehot, rel_bias.astype(F32),
                          precision=lax.Precision.HIGHEST)
        full = jnp.where(band[None], bias * LOG2_E, NEG_INF)
        first = jnp.concatenate([full[:, :, ATT_BLOCK:], jnp.full_like(full[:, :, ATT_BLOCK:], NEG_INF)], axis=-1)
        tabs.append(jnp.stack([full, first]))
    return jnp.stack(tabs)


def _att_kernel(q_ref, k_ref, v_ref, tab_ref, o_ref, m_ref, l_ref, *, seq):
    lane = lax.broadcasted_iota(jnp.int32, (ATT_BLOCK, LANES), 1)
    head0 = lane < HEAD_DIM

    def rows(ref, start, n, dil):
        if dil == 1:
            return ref[0, pl.ds(pl.multiple_of(start, ATT_BLOCK), n), :]
        return ref[0, pl.ds(start, n, stride=dil), :]

    def lane_mask(n_rows, h):
        m = lax.broadcasted_iota(jnp.int32, (n_rows, LANES), 1) < HEAD_DIM
        return m if h == 0 else ~m

    def batch(pi, dil, nb, res, blk0, head, init):
        q_rows = nb * ATT_BLOCK
        kv_blocks = nb if head else nb + 1
        q_start = res + dil * ATT_BLOCK * blk0
        kv_start = q_start if head else q_start - dil * ATT_BLOCK
        if dil == 1:
            sl = pl.ds(pl.multiple_of(q_start, ATT_BLOCK), q_rows)
        else:
            sl = pl.ds(q_start, q_rows, stride=dil)
        q = rows(q_ref, q_start, q_rows, dil)
        k = rows(k_ref, kv_start, kv_blocks * ATT_BLOCK, dil).astype(BF16)
        v = rows(v_ref, kv_start, kv_blocks * ATT_BLOCK, dil)
        if not init:
            m_old, l_old, o_old = m_ref[sl, :], l_ref[sl, :], o_ref[0, sl, :]
        qh = [jnp.where(lane_mask(q_rows, h), q, 0.0).astype(BF16) for h in range(2)]
        vh = [jnp.where(lane_mask(kv_blocks * ATT_BLOCK, h), v, 1.0).astype(BF16) for h in range(2)]
        o_parts, l_parts, m_parts = [], [], []
        for j in range(nb):
            first = 1 if (head and j == 0) else 0
            kb = j if (not head or j == 0) else j - 1
            ksl = slice(kb * ATT_BLOCK, (kb + 2) * ATT_BLOCK)
            qsl = slice(j * ATT_BLOCK, (j + 1) * ATT_BLOCK)
            outs, ms = [], []
            for h in range(2):
                s = _dot_nt(qh[h][qsl], k[ksl]) + tab_ref[pi, first, h]
                m = jnp.max(s, axis=1, keepdims=True)
                p = jnp.exp2(s - m).astype(BF16)
                outs.append(_dot(p, vh[h][ksl]))
                ms.append(m)
            o_parts.append(jnp.where(head0, outs[0], outs[1]))
            l_parts.append(pltpu.roll(jnp.where(head0, outs[1], outs[0]), HEAD_DIM, axis=1))
            m_parts.append(jnp.where(head0, ms[0], ms[1]))
        o = jnp.concatenate(o_parts, axis=0)
        l = jnp.concatenate(l_parts, axis=0)
        m = jnp.concatenate(m_parts, axis=0)
        if not init:
            m_new = jnp.maximum(m_old, m)
            a_old = jnp.exp2(m_old - m_new)
            a_cur = jnp.exp2(m - m_new)
            o = o_old * a_old + o * a_cur
            l = l_old * a_old + l * a_cur
            m = m_new
        o_ref[0, sl, :] = o
        m_ref[sl, :] = m
        l_ref[sl, :] = l

    n_units = seq // ATT_BLOCK
    order = sorted(range(len(PATTERNS)), key=lambda p: -PATTERNS[p][1])
    for pos, pi in enumerate(order):
        dil = PATTERNS[pi][1]
        log_d = dil.bit_length() - 1
        res_blocks = seq // (dil * ATT_BLOCK)
        nb = min(ATT_UNITS_PER_STEP, res_blocks)
        per_res = res_blocks // nb
        init = pos == 0

        res_step = 2 if (nb < ATT_UNITS_PER_STEP and dil % 2 == 0) else 1

        def head_body(r, carry, pi=pi, dil=dil, nb=nb, init=init, res_step=res_step):
            for dr in range(res_step):
                batch(pi, dil, nb, r * res_step + dr, 0, True, init)
            return carry
        lax.fori_loop(0, dil // res_step, head_body, 0)

        if per_res > 1:
            def tail_body(i, carry, pi=pi, dil=dil, nb=nb, log_d=log_d, init=init):
                res = i & (dil - 1)
                blk0 = nb * (1 + (i >> log_d))
                batch(pi, dil, nb, res, blk0, False, init)
                return carry
            lax.fori_loop(0, dil * (per_res - 1), tail_body, 0)

    def finish(i, carry):
        sl = pl.ds(pl.multiple_of(i * ATT_BLOCK, ATT_BLOCK), ATT_BLOCK)
        o_ref[0, sl, :] = o_ref[0, sl, :] / l_ref[sl, :]
        return carry
    lax.fori_loop(0, n_units, finish, 0)


def _attention(q, k, v, tables):
    bsz, seq, att_w = q.shape
    n_pairs = att_w // LANES
    blk = lambda b, p: (b, 0, p)
    qkv_spec = pl.BlockSpec((1, seq, LANES), blk)
    n_pat = tables.shape[0]
    tab_spec = pl.BlockSpec((n_pat, 2, 2, ATT_BLOCK, 2 * ATT_BLOCK), lambda b, p: (0, 0, p, 0, 0))
    return pl.pallas_call(
        functools.partial(_att_kernel, seq=seq),
        grid=(bsz, n_pairs),
        in_specs=[qkv_spec, qkv_spec, qkv_spec, tab_spec],
        out_specs=pl.BlockSpec((1, seq, LANES), blk),
        out_shape=jax.ShapeDtypeStruct((bsz, seq, att_w), F32),
        scratch_shapes=[pltpu.VMEM((seq, LANES), F32), pltpu.VMEM((seq, LANES), F32)],
        compiler_params=_cparams(("parallel", "parallel")),
        name="dilated_attention",
    )(q, k, v, tables)


def _ssm_params(a_re, a_im, b_re, b_im, c_re, c_im, d_skip, log_dt):
    n_g, n_p = a_re.shape
    n_c = b_re.shape[-1]
    n_state = n_g * n_p
    dt = jnp.exp(log_dt.astype(F32))[:, None]

    def a_pow(kk):
        mag = jnp.exp(kk * dt * a_re)
        ph = kk * dt * a_im
        return mag * jnp.cos(ph), mag * jnp.sin(ph)

    ab_re, ab_im = a_pow(1.0)
    nr, ni = ab_re - 1.0, ab_im
    den = a_re * a_re + a_im * a_im
    f_re = (nr * a_re + ni * a_im) / den
    f_im = (ni * a_re - nr * a_im) / den
    bb_re = f_re[:, :, None] * b_re - f_im[:, :, None] * b_im
    bb_im = f_re[:, :, None] * b_im + f_im[:, :, None] * b_re
    eye = jnp.eye(n_g, dtype=F32)
    w_re = jnp.einsum('gpc,gh->gchp', bb_re, eye).reshape(n_g * n_c, n_state)
    w_im = jnp.einsum('gpc,gh->gchp', bb_im, eye).reshape(n_g * n_c, n_state)
    o_re = jnp.einsum('gcp,gh->gphc', c_re, eye).reshape(n_state, n_g * n_c)
    o_im = jnp.einsum('gcp,gh->gphc', c_im, eye).reshape(n_state, n_g * n_c)
    n_kb = (n_g * n_c) // LANES
    sp = n_state // n_kb
    w_in = jnp.stack([jnp.concatenate([w_re[kb * LANES:(kb + 1) * LANES, kb * sp:(kb + 1) * sp],
                                       w_im[kb * LANES:(kb + 1) * LANES, kb * sp:(kb + 1) * sp]], axis=1)
                      for kb in range(n_kb)])
    w_out = jnp.stack([jnp.concatenate([o_re[kb * sp:(kb + 1) * sp, kb * LANES:(kb + 1) * LANES],
                                        -o_im[kb * sp:(kb + 1) * sp, kb * LANES:(kb + 1) * LANES]], axis=0)
                       for kb in range(n_kb)])
    j = jnp.arange(SUBLANES, dtype=F32)[:, None]
    flat = lambda t: jnp.broadcast_to(t.reshape(1, n_state), (SUBLANES, n_state))
    coef = []
    for sh in (1, 2, 4):
        pr, pim = a_pow(float(sh))
        keep = (j >= sh).astype(F32)
        coef += [flat(pr) * keep, flat(pim) * keep]
    a_re_f = jnp.broadcast_to(a_re.reshape(1, n_state), (SUBLANES, n_state))
    a_im_f = jnp.broadcast_to(a_im.reshape(1, n_state), (SUBLANES, n_state))
    dt_f = jnp.broadcast_to(jnp.repeat(dt[:, 0], n_p).reshape(1, n_state), (SUBLANES, n_state))
    mag = jnp.exp((j + 1.0) * dt_f * a_re_f)
    ph = (j + 1.0) * dt_f * a_im_f
    coef += [mag * jnp.cos(ph), mag * jnp.sin(ph)]
    coef = jnp.stack(coef)
    return w_in.astype(BF16), w_out.astype(BF16), coef, d_skip.reshape(1, n_g * n_c).astype(F32)


def _gelu_tanh(x):
    return 0.5 * x * (1.0 + jnp.tanh(math.sqrt(2.0 / math.pi) * (x + 0.044715 * (x * x * x))))


def _ssm_kernel(u_ref, win_ref, wout_ref, coef_ref, d_ref, wglu_ref, bglu_ref, o_ref,
                hre_ref, him_ref, cre_ref, cim_ref, *, rows, n_state, col_w):
    n_kb = win_ref.shape[0]
    sp = n_state // n_kb

    @pl.when(pl.program_id(1) == 0)
    def _():
        cre_ref[...] = jnp.zeros_like(cre_ref)
        cim_ref[...] = jnp.zeros_like(cim_ref)

    u = u_ref[0]
    ub = u.astype(BF16)
    for kb in range(n_kb):
        bu = _dot(ub[:, kb * LANES:(kb + 1) * LANES], win_ref[kb])
        hre_ref[:, kb * sp:(kb + 1) * sp] = bu[:, :sp]
        him_ref[:, kb * sp:(kb + 1) * sp] = bu[:, sp:]

    n_slab = rows // SUBLANES
    for c0 in range(0, n_state, col_w):
        cs = slice(c0, c0 + col_w)

        def body(i, carry, cs=cs):
            c_re, c_im = carry
            sl = pl.ds(pl.multiple_of(i * SUBLANES, SUBLANES), SUBLANES)
            x_re = hre_ref[sl, cs]
            x_im = him_ref[sl, cs]
            for si, sh in enumerate((1, 2, 4)):
                p_re = coef_ref[2 * si, :, cs]
                p_im = coef_ref[2 * si + 1, :, cs]
                s_re = pltpu.roll(x_re, sh, axis=0)
                s_im = pltpu.roll(x_im, sh, axis=0)
                x_re, x_im = (x_re + p_re * s_re - p_im * s_im,
                              x_im + p_re * s_im + p_im * s_re)
            p_re = coef_ref[6, :, cs]
            p_im = coef_ref[7, :, cs]
            h_re = x_re + p_re * c_re - p_im * c_im
            h_im = x_im + p_re * c_im + p_im * c_re
            hre_ref[sl, cs] = h_re
            him_ref[sl, cs] = h_im
            last = slice(SUBLANES - 1, SUBLANES)
            return (jnp.broadcast_to(h_re[last, :], h_re.shape),
                    jnp.broadcast_to(h_im[last, :], h_im.shape))

        c_re, c_im = lax.fori_loop(0, n_slab, body, (cre_ref[:, cs], cim_ref[:, cs]), unroll=True)
        cre_ref[:, cs] = c_re
        cim_ref[:, cs] = c_im

    ys = []
    for kb in range(n_kb):
        hcat = jnp.concatenate([hre_ref[:, kb * sp:(kb + 1) * sp].astype(BF16),
                                him_ref[:, kb * sp:(kb + 1) * sp].astype(BF16)], axis=1)
        ys.append(_dot(hcat, wout_ref[kb]))
    y = jnp.concatenate(ys, axis=1) + d_ref[...] * u
    y = _gelu_tanh(y)
    z = _dot(y.astype(BF16), wglu_ref[...]) + bglu_ref[...]
    o_ref[0] = y * jax.nn.sigmoid(z)


def _ssm(u, w_in, w_out, coef, d_flat, w_glu, b_glu):
    bsz, seq, ssm_w = u.shape
    n_state = coef.shape[-1]
    rows = min(SSM_ROWS, seq)
    full = lambda *shape: pl.BlockSpec(shape, lambda b, i: (0,) * len(shape))
    kern = functools.partial(_ssm_kernel, rows=rows, n_state=n_state, col_w=4 * LANES)
    return pl.pallas_call(
        kern,
        grid=(bsz, seq // rows),
        in_specs=[pl.BlockSpec((1, rows, ssm_w), lambda b, i: (b, i, 0)),
                  full(*w_in.shape), full(*w_out.shape), full(*coef.shape), full(1, ssm_w),
                  full(ssm_w, ssm_w), full(1, ssm_w)],
        out_specs=pl.BlockSpec((1, rows, ssm_w), lambda b, i: (b, i, 0)),
        out_shape=jax.ShapeDtypeStruct((bsz, seq, ssm_w), F32),
        scratch_shapes=[pltpu.VMEM((rows, n_state), F32), pltpu.VMEM((rows, n_state), F32),
                        pltpu.VMEM((SUBLANES, n_state), F32), pltpu.VMEM((SUBLANES, n_state), F32)],
        compiler_params=_cparams(("parallel", "arbitrary")),
        name="s5_glu",
    )(u, w_in, w_out, coef, d_flat, w_glu.astype(BF16), b_glu.reshape(1, ssm_w))


def _layer_norm(y, g, b):
    mu = jnp.mean(y, axis=-1, keepdims=True)
    yc = y - mu
    var = jnp.mean(yc * yc, axis=-1, keepdims=True)
    return yc * lax.rsqrt(var + EPS) * g + b


def _rms_norm(y, g):
    return y * lax.rsqrt(jnp.mean(y * y, axis=-1, keepdims=True) + EPS) * g


def _mix_kernel(x_ref, att_ref, ssm_ref, gatt_ref, gssm_ref, wout_ref, g1_ref, ln_g_ref, ln_b_ref,
                sc_ref, sh_ref, g2_ref, wsg_ref, wsu_ref, wsd_ref, h_ref, ht_ref, pre_ref, *, alpha, att_w):
    a_n = _rms_norm(att_ref[0], gatt_ref[...]).astype(BF16)
    s_n = _rms_norm(ssm_ref[0], gssm_ref[...]).astype(BF16)
    mix = _dot(a_n, wout_ref[0:att_w, :]) + _dot(s_n, wout_ref[att_w:, :])
    x1 = _layer_norm(alpha * x_ref[0] + g1_ref[0] * mix, ln_g_ref[...], ln_b_ref[...])
    h = x1 * (1.0 + sc_ref[0]) + sh_ref[0]
    h_ref[0] = h
    _store_token_tiles(ht_ref, h)
    hb = h.astype(BF16)
    hid = _silu(_dot(hb, wsg_ref[...])) * _dot(hb, wsu_ref[...])
    shared = _dot(hid.astype(BF16), wsd_ref[...])
    pre_ref[0] = alpha * x1 + g2_ref[0] * shared


def _mix(x, att, ssm, g_att, g_ssm, w_out, gate1, ln_g, ln_b, scale2, shift2, gate2,
         w_s_gate, w_s_up, w_s_down, alpha):
    bsz, seq, d = x.shape
    att_w = att.shape[-1]
    ssm_w = ssm.shape[-1]
    ff = w_s_gate.shape[1]
    tm = min(MIX_ROWS, seq)
    n_sub = d // LANES
    row = lambda b, i: (b, i, 0)
    per_b = lambda b, i: (b, 0, 0)
    full = lambda *shape: pl.BlockSpec(shape, lambda b, i: (0,) * len(shape))
    vec = lambda t: t.reshape(bsz, 1, d)
    return pl.pallas_call(
        functools.partial(_mix_kernel, alpha=alpha, att_w=att_w),
        grid=(bsz, seq // tm),
        in_specs=[pl.BlockSpec((1, tm, d), row), pl.BlockSpec((1, tm, att_w), row),
                  pl.BlockSpec((1, tm, ssm_w), row),
                  full(1, att_w), full(1, ssm_w), full(att_w + ssm_w, d),
                  pl.BlockSpec((1, 1, d), per_b), full(1, d), full(1, d),
                  pl.BlockSpec((1, 1, d), per_b), pl.BlockSpec((1, 1, d), per_b),
                  pl.BlockSpec((1, 1, d), per_b),
                  full(d, ff), full(d, ff), full(ff, d)],
        out_specs=[pl.BlockSpec((1, tm, d), row),
                   pl.BlockSpec((tm * n_sub, LANES), lambda b, i: (b * (seq // tm) + i, 0)),
                   pl.BlockSpec((1, tm, d), row)],
        out_shape=[jax.ShapeDtypeStruct((bsz, seq, d), F32),
                   jax.ShapeDtypeStruct((bsz * seq * n_sub, LANES), F32),
                   jax.ShapeDtypeStruct((bsz, seq, d), F32)],
        compiler_params=_cparams(("parallel", "parallel")),
        name="mix_ln1_shared",
    )(x, att, ssm, g_att.reshape(1, att_w), g_ssm.reshape(1, ssm_w), w_out.astype(BF16),
      vec(gate1), ln_g.reshape(1, d), ln_b.reshape(1, d), vec(scale2), vec(shift2), vec(gate2),
      w_s_gate.astype(BF16), w_s_up.astype(BF16), w_s_down.astype(BF16))


def _route_kernel(h_ref, wrt_ref, wrt_lo_ref, bias_ref, tri_ref, idx_ref, gate_ref, rank_ref, cnt_ref, carry_ref):
    n_e = wrt_ref.shape[0]
    tm = h_ref.shape[0]
    per_group = n_e // N_EXPERT_GROUPS
    neg = -jnp.inf

    @pl.when(pl.program_id(0) == 0)
    def _():
        carry_ref[...] = jnp.zeros_like(carry_ref)

    h_hi, h_lo = _split_bf16(h_ref[...])
    logits = (_dot_nt(wrt_ref[...], h_hi) + _dot_nt(wrt_ref[...], h_lo)
              + _dot_nt(wrt_lo_ref[...], h_hi))
    scores = jax.nn.sigmoid(logits)
    biased = scores + bias_ref[...]

    g3 = biased.reshape(N_EXPERT_GROUPS, per_group, tm)
    ridx = lax.broadcasted_iota(jnp.int32, g3.shape, 1).astype(F32)
    m1 = jnp.max(g3, axis=1, keepdims=True)
    first = jnp.min(jnp.where(g3 == m1, ridx, float(per_group)), axis=1, keepdims=True)
    m2 = jnp.max(jnp.where(ridx == first, neg, g3), axis=1, keepdims=True)
    gs = m1 + m2

    gidx = lax.broadcasted_iota(jnp.int32, gs.shape, 0).astype(F32)
    ok = jnp.zeros_like(gs)
    cur = gs
    for _ in range(TOPK_GROUPS):
        mx = jnp.max(cur, axis=0, keepdims=True)
        fi = jnp.min(jnp.where(cur == mx, gidx, float(N_EXPERT_GROUPS)), axis=0, keepdims=True)
        hit = gidx == fi
        ok = jnp.where(hit, 1.0, ok)
        cur = jnp.where(hit, neg, cur)
    masked = jnp.where(ok > 0.5, g3, neg).reshape(n_e, tm)

    eidx = lax.broadcasted_iota(jnp.int32, (n_e, tm), 0).astype(F32)
    onehot = jnp.zeros((n_e, tm), F32)
    cur = masked
    sel_idx = []
    sel_gate = []
    for _ in range(TOP_K):
        mx = jnp.max(cur, axis=0, keepdims=True)
        fi = jnp.min(jnp.where(cur == mx, eidx, float(n_e)), axis=0, keepdims=True)
        hit = eidx == fi
        sel_idx.append(fi)
        sel_gate.append(jnp.sum(jnp.where(hit, scores, 0.0), axis=0, keepdims=True))
        onehot = jnp.where(hit, 1.0, onehot)
        cur = jnp.where(hit, neg, cur)
    idx = jnp.concatenate(sel_idx, axis=0)
    gate = jnp.concatenate(sel_gate, axis=0)
    gate = gate / jnp.sum(gate, axis=0, keepdims=True) * ROUTED_SCALE

    prior = _dot(onehot.astype(BF16), tri_ref[...]) + carry_ref[:, 0:1]
    ranks = [jnp.sum(jnp.where(eidx == sel_idx[k], prior, 0.0), axis=0, keepdims=True)
             for k in range(TOP_K)]
    rank = jnp.concatenate(ranks, axis=0)
    carry = carry_ref[...] + jnp.sum(onehot, axis=1, keepdims=True)
    carry_ref[...] = carry
    cnt_ref[...] = carry

    idx_ref[...] = idx.astype(jnp.int32)
    gate_ref[...] = gate
    rank_ref[...] = rank.astype(jnp.int32)


def _route(h2d, w_router, router_bias):
    n_tok, d = h2d.shape
    n_e = w_router.shape[1]
    tm = min(ROUTE_COLS, n_tok)
    tri = (np.arange(tm)[:, None] < np.arange(tm)[None, :]).astype(np.float32)
    w_hi, w_lo = _split_bf16(w_router.T.astype(F32))
    col = lambda i: (0, i)
    full = lambda *shape: pl.BlockSpec(shape, lambda i: (0,) * len(shape))
    idx, gate, rank, cnt = pl.pallas_call(
        _route_kernel,
        grid=(n_tok // tm,),
        in_specs=[pl.BlockSpec((tm, d), lambda i: (i, 0)), full(n_e, d), full(n_e, d), full(n_e, 1),
                  full(tm, tm)],
        out_specs=[pl.BlockSpec((TOP_K, tm), col), pl.BlockSpec((TOP_K, tm), col),
                   pl.BlockSpec((TOP_K, tm), col), full(n_e, LANES)],
        out_shape=[jax.ShapeDtypeStruct((TOP_K, n_tok), jnp.int32),
                   jax.ShapeDtypeStruct((TOP_K, n_tok), F32),
                   jax.ShapeDtypeStruct((TOP_K, n_tok), jnp.int32),
                   jax.ShapeDtypeStruct((n_e, LANES), F32)],
        scratch_shapes=[pltpu.VMEM((n_e, LANES), F32)],
        compiler_params=_cparams(("arbitrary",)),
        name="router_topk",
    )(h2d, w_hi, w_lo, router_bias.reshape(n_e, 1).astype(F32), jnp.asarray(tri, BF16))
    return idx, gate, rank, cnt[:, 0].astype(jnp.int32)


def _sc_mesh():
    return plsc.VectorSubcoreMesh(core_axis_name="core", subcore_axis_name="subcore")


def _dispatch(h_tiles, gate_rows, dest_windows, n_rows):
    n_tok, n_sub, _ = h_tiles.shape
    win = DISPATCH_WINDOW
    index_spec = pl.BlockSpec((1, TOP_K * win), lambda i: (i, 0))

    def scatter(body, src, src_spec, out_type):
        @pl.kernel(out_type=out_type, mesh=_sc_mesh(), scratch_types=[])
        def run(s_hbm, d_hbm, o_hbm):
            pltpu.emit_pipeline(
                functools.partial(body, o_hbm),
                grid=(n_tok // win,),
                in_specs=[src_spec, index_spec],
                out_specs=[],
                core_axis_name=("core", "subcore"),
                dimension_semantics=(pltpu.PARALLEL,),
            )(s_hbm, d_hbm)
        return run(src, dest_windows)

    def tile_window(o_hbm, x_vmem, d_vmem):
        for k in range(TOP_K):
            pltpu.sync_copy(x_vmem, o_hbm.at[d_vmem.at[0, pl.ds(k * win, win)]])

    def gate_window(o_hbm, g_vmem, d_vmem):
        for k in range(TOP_K):
            pltpu.sync_copy(g_vmem.at[k], o_hbm.at[d_vmem.at[0, pl.ds(k * win, win)]])

    xs = scatter(tile_window, h_tiles, pl.BlockSpec((win, n_sub, LANES), lambda i: (i, 0, 0)),
                 jax.ShapeDtypeStruct((n_rows, n_sub, LANES), h_tiles.dtype))
    row_gates = scatter(gate_window, gate_rows, pl.BlockSpec((TOP_K, win, GATE_LANES), lambda i: (0, i, 0)),
                        jax.ShapeDtypeStruct((n_rows, GATE_LANES), F32))
    return xs, row_gates


def _gather_sum(ys_tiles, dest_windows, n_tok):
    _, n_sub, _ = ys_tiles.shape
    win = DISPATCH_WINDOW

    @pl.kernel(out_type=jax.ShapeDtypeStruct((n_tok, n_sub, LANES), ys_tiles.dtype), mesh=_sc_mesh(),
               scratch_types=[])
    def gather_tiles(y_hbm, d_hbm, o_hbm):
        def window(d_vmem, o_vmem):
            pltpu.sync_copy(y_hbm.at[d_vmem.at[0, pl.ds(0, win)]], o_vmem)
            for k in range(1, TOP_K):
                pltpu.sync_copy(y_hbm.at[d_vmem.at[0, pl.ds(k * win, win)]], o_vmem, add=True)
        pltpu.emit_pipeline(
            window,
            grid=(n_tok // win,),
            in_specs=[pl.BlockSpec((1, TOP_K * win), lambda i: (i, 0))],
            out_specs=[pl.BlockSpec((win, n_sub, LANES), lambda i: (i, 0, 0))],
            core_axis_name=("core", "subcore"),
            dimension_semantics=(pltpu.PARALLEL,),
        )(d_hbm, o_hbm)

    return gather_tiles(ys_tiles, dest_windows)


def _expert_kernel(first_ref, nblk_ref, cnt_ref, nused_ref, xs_ref, rg_ref, wg_ref, wu_ref, wd_ref, ys_ref,
                   xbuf_ref, gbuf_ref, ybuf_ref, wgb_ref, wub_ref, wdb_ref, xsem_ref, ysem_ref):
    e = pl.program_id(0)
    n_used = nused_ref[0]
    d = wgb_ref.shape[0]
    blk_rows = xbuf_ref.shape[1]

    class _Pair:
        def __init__(self, g, slot):
            src = xs_ref.at[pl.ds(pl.multiple_of(g * blk_rows, blk_rows), blk_rows)]
            gsrc = rg_ref.at[pl.ds(pl.multiple_of(g * MOE_ROWS, MOE_ROWS), MOE_ROWS)]
            self.copies = (pltpu.make_async_copy(src, xbuf_ref.at[slot], xsem_ref.at[slot]),
                           pltpu.make_async_copy(gsrc, gbuf_ref.at[slot], xsem_ref.at[slot]))

        def start(self):
            for cp in self.copies:
                cp.start()

        def wait(self):
            for cp in self.copies:
                cp.wait()

    def x_copy(g, slot):
        return _Pair(g, slot)

    def y_copy(g, slot):
        dst = ys_ref.at[pl.ds(pl.multiple_of(g * blk_rows, blk_rows), blk_rows)]
        return pltpu.make_async_copy(ybuf_ref.at[slot], dst, ysem_ref.at[slot])

    n_slot = xbuf_ref.shape[0]

    @pl.when(e == 0)
    def _():
        for g in range(n_slot - 1):
            @pl.when(g < n_used)
            def _(g=g):
                x_copy(g, g).start()

    @pl.when(nblk_ref[e] > 0)
    def _():
        wgb_ref[...] = wg_ref[0].astype(BF16)
        wub_ref[...] = wu_ref[0].astype(BF16)
        wdb_ref[...] = wd_ref[0].astype(BF16)

    def block(j, carry):
        g = first_ref[e] + j
        slot = g & (n_slot - 1)
        x_copy(g, slot).wait()

        @pl.when(g + n_slot - 1 < n_used)
        def _():
            x_copy(g + n_slot - 1, (g + n_slot - 1) & (n_slot - 1)).start()

        @pl.when(g >= n_slot)
        def _():
            y_copy(g - n_slot, slot).wait()

        x = jnp.concatenate(_load_token_tiles(xbuf_ref.at[slot], MOE_ROWS, d), axis=1)
        row = lax.broadcasted_iota(jnp.int32, (MOE_ROWS, 1), 0)
        xb = jnp.where(row < cnt_ref[e] - j * MOE_ROWS, x, 0.0).astype(BF16)
        hid = _silu(_dot(xb, wgb_ref[...])) * _dot(xb, wub_ref[...])
        y = _dot(hid.astype(BF16), wdb_ref[...]) * gbuf_ref[slot][:, 0:1]
        _store_token_tiles(ybuf_ref.at[slot], jnp.where(row < cnt_ref[e] - j * MOE_ROWS, y, 0.0))
        y_copy(g, slot).start()
        return carry
    lax.fori_loop(0, nblk_ref[e], block, 0)

    @pl.when(e == pl.num_programs(0) - 1)
    def _():
        for back in range(1, n_slot + 1):
            @pl.when(n_used >= back)
            def _(back=back):
                y_copy(n_used - back, (n_used - back) & (n_slot - 1)).wait()


def _experts(xs_tiles, row_gates, plan, w_gate, w_up, w_down):
    n_rows, n_sub, _ = xs_tiles.shape
    d = n_sub * LANES
    n_e, _, ff = w_gate.shape
    blk_rows = MOE_ROWS * n_sub
    hbm = pl.BlockSpec(memory_space=pl.ANY)
    grid_spec = pltpu.PrefetchScalarGridSpec(
        num_scalar_prefetch=4,
        grid=(n_e,),
        in_specs=[hbm, hbm,
                  pl.BlockSpec((1, d, ff), lambda e, *_: (e, 0, 0)),
                  pl.BlockSpec((1, d, ff), lambda e, *_: (e, 0, 0)),
                  pl.BlockSpec((1, ff, d), lambda e, *_: (e, 0, 0))],
        out_specs=hbm,
        scratch_shapes=[pltpu.VMEM((EXPERT_SLOTS, blk_rows, LANES), F32),
                        pltpu.VMEM((EXPERT_SLOTS, MOE_ROWS, GATE_LANES), F32),
                        pltpu.VMEM((EXPERT_SLOTS, blk_rows, LANES), F32),
                        pltpu.VMEM((d, ff), BF16), pltpu.VMEM((d, ff), BF16), pltpu.VMEM((ff, d), BF16),
                        pltpu.SemaphoreType.DMA((EXPERT_SLOTS,)), pltpu.SemaphoreType.DMA((EXPERT_SLOTS,))],
    )
    ys = pl.pallas_call(
        _expert_kernel,
        grid_spec=grid_spec,
        out_shape=jax.ShapeDtypeStruct((n_rows * n_sub, LANES), F32),
        compiler_params=_cparams(("arbitrary",)),
        name="moe_experts",
    )(*plan, xs_tiles.reshape(n_rows * n_sub, LANES), row_gates, w_gate, w_up, w_down)
    return ys.reshape(n_rows, n_sub, LANES)


def _combine_kernel(dest_ref, ys_ref, pre_ref, g2_ref, ln_g_ref, ln_b_ref, o_ref, buf_ref, sem_ref):
    tm, d = pre_ref.shape[1], pre_ref.shape[2]
    n_sub = d // LANES

    def row_copy(t, k, src_row):
        slot = pl.multiple_of((k * tm + t) * n_sub, n_sub)
        return pltpu.make_async_copy(ys_ref.at[src_row], buf_ref.at[pl.ds(slot, n_sub)], sem_ref.at[0])

    def issue(g, c):
        t0 = g * DMA_ISSUE_UNROLL
        rows = [[dest_ref[(t0 + j) * TOP_K + k] for k in range(TOP_K)] for j in range(DMA_ISSUE_UNROLL)]
        for j in range(DMA_ISSUE_UNROLL):
            for k in range(TOP_K):
                row_copy(t0 + j, k, rows[j][k]).start(priority=k % 2)
        return c
    lax.fori_loop(0, tm // DMA_ISSUE_UNROLL, issue, 0)

    def drain(t, c):
        for k in range(TOP_K):
            row_copy(t, k, dest_ref[t * TOP_K + k]).wait()
        return c
    lax.fori_loop(0, tm, drain, 0)

    pieces = []
    for s in range(n_sub):
        acc = None
        for k in range(TOP_K):
            term = buf_ref[pl.ds(k * tm * n_sub + s, tm, stride=n_sub), :]
            acc = term if acc is None else acc + term
        pieces.append(acc)
    routed = jnp.concatenate(pieces, axis=1)
    y = pre_ref[0] + g2_ref[0] * routed
    o_ref[0] = _layer_norm(y, ln_g_ref[...], ln_b_ref[...])


def _combine(ys, dest, pre, gate2, ln_g, ln_b, n_tiles):
    bsz, seq, d = pre.shape
    tm = min(DISPATCH_ROWS, seq)
    per_seq = seq // tm
    row = lambda i, *_: (i // per_seq, i % per_seq, 0)
    per_b = lambda i, *_: (i // per_seq, 0, 0)
    full = lambda *shape: pl.BlockSpec(shape, lambda i, *_: (0,) * len(shape))
    grid_spec = pltpu.PrefetchScalarGridSpec(
        num_scalar_prefetch=0,
        grid=(n_tiles,),
        in_specs=[pl.BlockSpec((TOP_K * tm,), lambda i: (i,), memory_space=pltpu.SMEM),
                  pl.BlockSpec(memory_space=pl.ANY),
                  pl.BlockSpec((1, tm, d), row), pl.BlockSpec((1, 1, d), per_b), full(1, d), full(1, d)],
        out_specs=pl.BlockSpec((1, tm, d), row),
        scratch_shapes=[pltpu.VMEM((TOP_K * tm * (d // LANES), LANES), F32), pltpu.SemaphoreType.DMA((1,))],
    )
    return pl.pallas_call(
        _combine_kernel,
        grid_spec=grid_spec,
        out_shape=jax.ShapeDtypeStruct((bsz, seq, d), F32),
        compiler_params=_cparams(("arbitrary",)),
        name="moe_combine_ln2",
    )(dest, ys, pre, gate2.reshape(bsz, 1, d), ln_g.reshape(1, d), ln_b.reshape(1, d))


def _final_ln_kernel(r_ref, pre_ref, g2_ref, ln_g_ref, ln_b_ref, *rest):
    o_ref = rest[-1]
    tm, d = pre_ref.shape[1], pre_ref.shape[2]
    routed = jnp.concatenate(_load_token_tiles(r_ref, tm, d), axis=1)
    y = pre_ref[0] + g2_ref[0] * routed
    o_ref[0] = _layer_norm(y, ln_g_ref[...], ln_b_ref[...])


def _final_ln(routed_tiles, pre, gate2, ln_g, ln_b, first_tile, partial=None):
    bsz, seq, d = pre.shape
    tm = min(DISPATCH_ROWS, seq)
    per_seq = seq // tm
    n_sub = d // LANES
    n_tiles = bsz * per_seq - first_tile
    row = lambda i: ((i + first_tile) // per_seq, (i + first_tile) % per_seq, 0)
    per_b = lambda i: ((i + first_tile) // per_seq, 0, 0)
    full = lambda *shape: pl.BlockSpec(shape, lambda i: (0,) * len(shape))
    in_specs = [pl.BlockSpec((tm * n_sub, LANES), lambda i: (i, 0)),
                pl.BlockSpec((1, tm, d), row), pl.BlockSpec((1, 1, d), per_b), full(1, d), full(1, d)]
    args = [routed_tiles.reshape(-1, LANES), pre, gate2.reshape(bsz, 1, d), ln_g.reshape(1, d),
            ln_b.reshape(1, d)]
    aliases = {}
    if partial is not None:
        in_specs.append(pl.BlockSpec(memory_space=pl.ANY))
        args.append(partial)
        aliases = {len(args) - 1: 0}
    return pl.pallas_call(
        _final_ln_kernel,
        grid=(n_tiles,),
        in_specs=in_specs,
        out_specs=pl.BlockSpec((1, tm, d), row),
        out_shape=jax.ShapeDtypeStruct((bsz, seq, d), F32),
        input_output_aliases=aliases,
        compiler_params=_cparams(("arbitrary",)),
        name="moe_final_ln2",
    )(*args)


def _dest_kernel(pstart_ref, idx_ref, rank_ref, dest_ref):
    idx = idx_ref[...]

    def body(e, acc):
        return acc + jnp.where(idx == e, pstart_ref[e], 0)
    dest_ref[...] = lax.fori_loop(0, pstart_ref.shape[0], body, rank_ref[...], unroll=8)


def _dest_rows(pstart, idx, rank):
    n_k, n_tok = idx.shape
    tn = min(4096, n_tok)
    grid_spec = pltpu.PrefetchScalarGridSpec(
        num_scalar_prefetch=1,
        grid=(n_tok // tn,),
        in_specs=[pl.BlockSpec((n_k, tn), lambda i, *_: (0, i))] * 2,
        out_specs=pl.BlockSpec((n_k, tn), lambda i, *_: (0, i)),
    )
    return pl.pallas_call(
        _dest_kernel,
        grid_spec=grid_spec,
        out_shape=jax.ShapeDtypeStruct((n_k, n_tok), jnp.int32),
        compiler_params=_cparams(("arbitrary",)),
        name="moe_dest",
    )(pstart, idx, rank)


def _moe_plan(idx, rank, cnt, n_tok):
    n_e = cnt.shape[0]
    padded = (cnt + MOE_ROWS - 1) // MOE_ROWS * MOE_ROWS
    pend = jnp.cumsum(padded).astype(jnp.int32)
    pstart = pend - padded
    dest_kt = _dest_rows(pstart, idx, rank)
    dest = dest_kt.T.reshape(-1)
    win = min(DISPATCH_WINDOW, n_tok)
    dest_windows = dest_kt.reshape(TOP_K, n_tok // win, win).transpose(1, 0, 2).reshape(n_tok // win, TOP_K * win)
    n_blk = n_tok * TOP_K // MOE_ROWS + n_e
    n_used = (pend[-1:] // MOE_ROWS).astype(jnp.int32)
    plan = ((pstart // MOE_ROWS).astype(jnp.int32), (padded // MOE_ROWS).astype(jnp.int32),
            cnt.astype(jnp.int32), n_used)
    return dest, dest_windows, plan, n_blk * MOE_ROWS


def kernel(x, c, rel_bias, w_ada, b_ada, w_in, ssm_a_re, ssm_a_im, ssm_b_re, ssm_b_im, ssm_c_re, ssm_c_im, ssm_d, ssm_log_dt, w_glu, b_glu, g_att, g_ssm, w_out, ln1_g, ln1_b, w_router, router_bias, w_e_gate, w_e_up, w_e_down, w_s_gate, w_s_up, w_s_down, ln2_g, ln2_b):
    bsz, seq, d = x.shape
    depth = w_ada.shape[0]
    alpha = (2 * depth) ** 0.25
    att_w = g_att.shape[-1]
    tables = _att_tables(rel_bias)
    for layer in range(depth):
        ada = _ada(c, w_ada[layer], b_ada[layer])
        shift1, scale1, gate1, shift2, scale2, gate2 = jnp.split(ada, 6, axis=-1)
        q, k, v, u = _inproj(x, scale1, shift1, w_in[layer], att_w)
        att = _attention(q, k, v, tables)
        ssm_prm = _ssm_params(ssm_a_re[layer], ssm_a_im[layer], ssm_b_re[layer], ssm_b_im[layer],
                              ssm_c_re[layer], ssm_c_im[layer], ssm_d[layer], ssm_log_dt[layer])
        ssm = _ssm(u, *ssm_prm, w_glu[layer], b_glu[layer])
        h2, h2_tiles, pre = _mix(x, att, ssm, g_att[layer], g_ssm[layer], w_out[layer], gate1,
                                 ln1_g[layer], ln1_b[layer], scale2, shift2, gate2, w_s_gate[layer],
                                 w_s_up[layer], w_s_down[layer], alpha)
        h2d = h2.reshape(bsz * seq, d)
        idx, gate, rank, cnt = _route(h2d, w_router[layer], router_bias[layer])
        dest, dest_windows, plan, n_rows = _moe_plan(idx, rank, cnt, bsz * seq)
        n_tok = bsz * seq
        gate_rows = jnp.broadcast_to(gate[:, :, None], (TOP_K, n_tok, GATE_LANES))
        xs, row_gates = _dispatch(h2_tiles.reshape(n_tok, d // LANES, LANES), gate_rows, dest_windows, n_rows)
        ys = _experts(xs, row_gates, plan, w_e_gate[layer], w_e_up[layer], w_e_down[layer])
        tm = min(DISPATCH_ROWS, seq)
        tc_tiles = min(COMBINE_TC_TILES, n_tok // tm)
        tc_tokens = tc_tiles * tm
        win = min(DISPATCH_WINDOW, n_tok)
        partial = (_combine(ys, dest, pre, gate2, ln2_g[layer], ln2_b[layer], tc_tiles)
                   if tc_tiles > 0 else None)
        routed = _gather_sum(ys, dest_windows[tc_tokens // win:], n_tok - tc_tokens)
        x = _final_ln(routed, pre, gate2, ln2_g[layer], ln2_b[layer], tc_tiles, partial)
    return x
```

```python
import functools
import math

import jax
import jax.numpy as jnp
import numpy as np
from jax import lax
from jax.experimental import pallas as pl
from jax.experimental.pallas import tpu as pltpu
from jax.experimental.pallas import tpu_sc as plsc

F32 = jnp.float32
BF16 = jnp.bfloat16

HEAD_DIM = 64
ATT_BLOCK = 128
PATTERNS = ((128, 1), (512, 4), (2048, 16))
N_BUCKETS = 32
MAX_DISTANCE = 2048
TOP_K = 8
N_EXPERT_GROUPS = 8
TOPK_GROUPS = 4
ROUTED_SCALE = 2.5
EPS = 1e-5
NEG_INF = -1e30
LOG2_E = math.log2(math.e)

LANES = 128
SUBLANES = 8
VMEM_LIMIT_BYTES = 56 * 1024 * 1024

PROJ_ROWS = 512
SSM_ROWS = 256
MIX_ROWS = 512
ROUTE_COLS = 512
MOE_ROWS = 256
DISPATCH_ROWS = 256
COMBINE_TC_TILES = 48
GATE_LANES = LANES
DISPATCH_WINDOW = 32
ATT_UNITS_PER_STEP = 8
DMA_ISSUE_UNROLL = 4
EXPERT_SLOTS = 4


def _cparams(sem, vmem=VMEM_LIMIT_BYTES):
    return pltpu.CompilerParams(dimension_semantics=sem, vmem_limit_bytes=vmem)


def _dot(a, b):
    return jnp.dot(a, b, preferred_element_type=F32)


def _dot_nt(a, b):
    return lax.dot_general(a, b, (((1,), (1,)), ((), ())), preferred_element_type=F32)


def _silu(x):
    return x * jax.nn.sigmoid(x)


def _store_token_tiles(ref, x):
    m, d = x.shape
    n_sub = d // LANES
    for s in range(n_sub):
        ref[pl.ds(s, m, stride=n_sub), :] = x[:, s * LANES:(s + 1) * LANES]


def _load_token_tiles(ref, m, d, row0=0):
    n_sub = d // LANES
    return [ref[pl.ds(row0 + s, m, stride=n_sub), :] for s in range(n_sub)]


def _split_bf16(x):
    hi = x.astype(BF16)
    lo = (x - hi.astype(F32)).astype(BF16)
    return hi, lo


def _ada_kernel(c_ref, w_ref, b_ref, o_ref):
    c = c_ref[...]
    a_hi, a_lo = _split_bf16(_silu(c))
    w_hi, w_lo = _split_bf16(w_ref[...])
    acc = _dot(a_hi, w_hi) + _dot(a_hi, w_lo) + _dot(a_lo, w_hi)
    o_ref[...] = acc + b_ref[...]


def _ada(c, w_ada, b_ada):
    bsz, d = c.shape
    n = w_ada.shape[1]
    rows = SUBLANES
    c_pad = jnp.zeros((rows, d), F32).at[:bsz].set(c)
    tn = 1024
    out = pl.pallas_call(
        _ada_kernel,
        grid=(n // tn,),
        in_specs=[pl.BlockSpec((rows, d), lambda j: (0, 0)),
                  pl.BlockSpec((d, tn), lambda j: (0, j)),
                  pl.BlockSpec((1, tn), lambda j: (0, j))],
        out_specs=pl.BlockSpec((rows, tn), lambda j: (0, j)),
        out_shape=jax.ShapeDtypeStruct((rows, n), F32),
        compiler_params=_cparams(("parallel",)),
        name="ada",
    )(c_pad, w_ada, b_ada.reshape(1, n))
    return out[:bsz]


def _inproj_kernel(x_ref, sc_ref, sh_ref, w_ref, q_ref, k_ref, v_ref, u_ref, *, att_w, q_scale):
    h = (x_ref[0] * (1.0 + sc_ref[0]) + sh_ref[0]).astype(BF16)
    q_ref[0] = _dot(h, w_ref[:, 0:att_w]) * q_scale
    k_ref[0] = _dot(h, w_ref[:, att_w:2 * att_w])
    v_ref[0] = _dot(h, w_ref[:, 2 * att_w:3 * att_w])
    u_ref[0] = _dot(h, w_ref[:, 3 * att_w:])


def _inproj(x, scale, shift, w_in, att_w):
    bsz, seq, d = x.shape
    n = w_in.shape[1]
    ssm_w = n - 3 * att_w
    ts = min(PROJ_ROWS, seq)
    row = lambda b, i: (b, i, 0)
    per_b = lambda b, i: (b, 0, 0)
    kern = functools.partial(_inproj_kernel, att_w=att_w, q_scale=HEAD_DIM ** -0.5 * LOG2_E)
    return pl.pallas_call(
        kern,
        grid=(bsz, seq // ts),
        in_specs=[pl.BlockSpec((1, ts, d), row),
                  pl.BlockSpec((1, 1, d), per_b),
                  pl.BlockSpec((1, 1, d), per_b),
                  pl.BlockSpec((d, n), lambda b, i: (0, 0))],
        out_specs=[pl.BlockSpec((1, ts, att_w), row)] * 3 + [pl.BlockSpec((1, ts, ssm_w), row)],
        out_shape=[jax.ShapeDtypeStruct((bsz, seq, att_w), F32)] * 3
                  + [jax.ShapeDtypeStruct((bsz, seq, ssm_w), F32)],
        compiler_params=_cparams(("parallel", "parallel")),
        name="inproj",
    )(x, scale.reshape(bsz, 1, d), shift.reshape(bsz, 1, d), w_in.astype(BF16))


def _t5_bucket_np(dist):
    exact = N_BUCKETS // 2
    large = exact + (np.log(np.maximum(dist, 1).astype(np.float64) / exact)
                     / math.log(MAX_DISTANCE / exact) * (N_BUCKETS - exact)).astype(np.int64)
    return np.where(dist < exact, dist, np.minimum(large, N_BUCKETS - 1))


def _att_tables(rel_bias):
    qi = np.arange(ATT_BLOCK)[:, None]
    ki = np.arange(2 * ATT_BLOCK)[None, :]
    rel = qi + ATT_BLOCK - ki
    tabs = []
    for window, dil in PATTERNS:
        band = (rel >= 0) & (rel <= window // dil)
        bucket = _t5_bucket_np(np.maximum(rel, 0) * dil)
        onehot = (bucket[:, :, None] == np.arange(N_BUCKETS)[None, None, :]).astype(np.float32)
        bias = jnp.einsum('qkb,bh->hqk', onehot, rel_bias.astype(F32),
                          precision=lax.Precision.HIGHEST)
        full = jnp.where(band[None], bias * LOG2_E, NEG_INF)
        first = jnp.concatenate([full[:, :, ATT_BLOCK:], jnp.full_like(full[:, :, ATT_BLOCK:], NEG_INF)], axis=-1)
        tabs.append(jnp.stack([full, first]))
    return jnp.stack(tabs)


def _att_kernel(q_ref, k_ref, v_ref, tab_ref, o_ref, m_ref, l_ref, *, seq):
    lane = lax.broadcasted_iota(jnp.int32, (ATT_BLOCK, LANES), 1)
    head0 = lane < HEAD_DIM

    def rows(ref, start, n, dil):
        if dil == 1:
            return ref[0, pl.ds(pl.multiple_of(start, ATT_BLOCK), n), :]
        return ref[0, pl.ds(start, n, stride=dil), :]

    def lane_mask(n_rows, h):
        m = lax.broadcasted_iota(jnp.int32, (n_rows, LANES), 1) < HEAD_DIM
        return m if h == 0 else ~m

    def batch(pi, dil, nb, res, blk0, head, init):
        q_rows = nb * ATT_BLOCK
        kv_blocks = nb if head else nb + 1
        q_start = res + dil * ATT_BLOCK * blk0
        kv_start = q_start if head else q_start - dil * ATT_BLOCK
        if dil == 1:
            sl = pl.ds(pl.multiple_of(q_start, ATT_BLOCK), q_rows)
        else:
            sl = pl.ds(q_start, q_rows, stride=dil)
        q = rows(q_ref, q_start, q_rows, dil)
        k = rows(k_ref, kv_start, kv_blocks * ATT_BLOCK, dil).astype(BF16)
        v = rows(v_ref, kv_start, kv_blocks * ATT_BLOCK, dil)
        if not init:
            m_old, l_old, o_old = m_ref[sl, :], l_ref[sl, :], o_ref[0, sl, :]
        qh = [jnp.where(lane_mask(q_rows, h), q, 0.0).astype(BF16) for h in range(2)]
        vh = [jnp.where(lane_mask(kv_blocks * ATT_BLOCK, h), v, 1.0).astype(BF16) for h in range(2)]
        o_parts, l_parts, m_parts = [], [], []
        for j in range(nb):
            first = 1 if (head and j == 0) else 0
            kb = j if (not head or j == 0) else j - 1
            ksl = slice(kb * ATT_BLOCK, (kb + 2) * ATT_BLOCK)
            qsl = slice(j * ATT_BLOCK, (j + 1) * ATT_BLOCK)
            outs, ms = [], []
            for h in range(2):
                s = _dot_nt(qh[h][qsl], k[ksl]) + tab_ref[pi, first, h]
                m = jnp.max(s, axis=1, keepdims=True)
                p = jnp.exp2(s - m).astype(BF16)
                outs.append(_dot(p, vh[h][ksl]))
                ms.append(m)
            o_parts.append(jnp.where(head0, outs[0], outs[1]))
            l_parts.append(pltpu.roll(jnp.where(head0, outs[1], outs[0]), HEAD_DIM, axis=1))
            m_parts.append(jnp.where(head0, ms[0], ms[1]))
        o = jnp.concatenate(o_parts, axis=0)
        l = jnp.concatenate(l_parts, axis=0)
        m = jnp.concatenate(m_parts, axis=0)
        if not init:
            m_new = jnp.maximum(m_old, m)
            a_old = jnp.exp2(m_old - m_new)
            a_cur = jnp.exp2(m - m_new)
            o = o_old * a_old + o * a_cur
            l = l_old * a_old + l * a_cur
            m = m_new
        o_ref[0, sl, :] = o
        m_ref[sl, :] = m
        l_ref[sl, :] = l

    n_units = seq // ATT_BLOCK
    order = sorted(range(len(PATTERNS)), key=lambda p: -PATTERNS[p][1])
    for pos, pi in enumerate(order):
        dil = PATTERNS[pi][1]
        log_d = dil.bit_length() - 1
        res_blocks = seq // (dil * ATT_BLOCK)
        nb = min(ATT_UNITS_PER_STEP, res_blocks)
        per_res = res_blocks // nb
        init = pos == 0

        res_step = 2 if (nb < ATT_UNITS_PER_STEP and dil % 2 == 0) else 1

        def head_body(r, carry, pi=pi, dil=dil, nb=nb, init=init, res_step=res_step):
            for dr in range(res_step):
                batch(pi, dil, nb, r * res_step + dr, 0, True, init)
            return carry
        lax.fori_loop(0, dil // res_step, head_body, 0)

        if per_res > 1:
            def tail_body(i, carry, pi=pi, dil=dil, nb=nb, log_d=log_d, init=init):
                res = i & (dil - 1)
                blk0 = nb * (1 + (i >> log_d))
                batch(pi, dil, nb, res, blk0, False, init)
                return carry
            lax.fori_loop(0, dil * (per_res - 1), tail_body, 0)

    def finish(i, carry):
        sl = pl.ds(pl.multiple_of(i * ATT_BLOCK, ATT_BLOCK), ATT_BLOCK)
        o_ref[0, sl, :] = o_ref[0, sl, :] / l_ref[sl, :]
        return carry
    lax.fori_loop(0, n_units, finish, 0)


def _attention(q, k, v, tables):
    bsz, seq, att_w = q.shape
    n_pairs = att_w // LANES
    blk = lambda b, p: (b, 0, p)
    qkv_spec = pl.BlockSpec((1, seq, LANES), blk)
    n_pat = tables.shape[0]
    tab_spec = pl.BlockSpec((n_pat, 2, 2, ATT_BLOCK, 2 * ATT_BLOCK), lambda b, p: (0, 0, p, 0, 0))
    return pl.pallas_call(
        functools.partial(_att_kernel, seq=seq),
        grid=(bsz, n_pairs),
        in_specs=[qkv_spec, qkv_spec, qkv_spec, tab_spec],
        out_specs=pl.BlockSpec((1, seq, LANES), blk),
        out_shape=jax.ShapeDtypeStruct((bsz, seq, att_w), F32),
        scratch_shapes=[pltpu.VMEM((seq, LANES), F32), pltpu.VMEM((seq, LANES), F32)],
        compiler_params=_cparams(("parallel", "parallel")),
        name="dilated_attention",
    )(q, k, v, tables)


def _ssm_params(a_re, a_im, b_re, b_im, c_re, c_im, d_skip, log_dt):
    n_g, n_p = a_re.shape
    n_c = b_re.shape[-1]
    n_state = n_g * n_p
    dt = jnp.exp(log_dt.astype(F32))[:, None]

    def a_pow(kk):
        mag = jnp.exp(kk * dt * a_re)
        ph = kk * dt * a_im
        return mag * jnp.cos(ph), mag * jnp.sin(ph)

    ab_re, ab_im = a_pow(1.0)
    nr, ni = ab_re - 1.0, ab_im
    den = a_re * a_re + a_im * a_im
    f_re = (nr * a_re + ni * a_im) / den
    f_im = (ni * a_re - nr * a_im) / den
    bb_re = f_re[:, :, None] * b_re - f_im[:, :, None] * b_im
    bb_im = f_re[:, :, None] * b_im + f_im[:, :, None] * b_re
    eye = jnp.eye(n_g, dtype=F32)
    w_re = jnp.einsum('gpc,gh->gchp', bb_re, eye).reshape(n_g * n_c, n_state)
    w_im = jnp.einsum('gpc,gh->gchp', bb_im, eye).reshape(n_g * n_c, n_state)
    o_re = jnp.einsum('gcp,gh->gphc', c_re, eye).reshape(n_state, n_g * n_c)
    o_im = jnp.einsum('gcp,gh->gphc', c_im, eye).reshape(n_state, n_g * n_c)
    n_kb = (n_g * n_c) // LANES
    sp = n_state // n_kb
    w_in = jnp.stack([jnp.concatenate([w_re[kb * LANES:(kb + 1) * LANES, kb * sp:(kb + 1) * sp],
                                       w_im[kb * LANES:(kb + 1) * LANES, kb * sp:(kb + 1) * sp]], axis=1)
                      for kb in range(n_kb)])
    w_out = jnp.stack([jnp.concatenate([o_re[kb * sp:(kb + 1) * sp, kb * LANES:(kb + 1) * LANES],
                                        -o_im[kb * sp:(kb + 1) * sp, kb * LANES:(kb + 1) * LANES]], axis=0)
                       for kb in range(n_kb)])
    j = jnp.arange(SUBLANES, dtype=F32)[:, None]
    flat = lambda t: jnp.broadcast_to(t.reshape(1, n_state), (SUBLANES, n_state))
    coef = []
    for sh in (1, 2, 4):
        pr, pim = a_pow(float(sh))
        keep = (j >= sh).astype(F32)
        coef += [flat(pr) * keep, flat(pim) * keep]
    a_re_f = jnp.broadcast_to(a_re.reshape(1, n_state), (SUBLANES, n_state))
    a_im_f = jnp.broadcast_to(a_im.reshape(1, n_state), (SUBLANES, n_state))
    dt_f = jnp.broadcast_to(jnp.repeat(dt[:, 0], n_p).reshape(1, n_state), (SUBLANES, n_state))
    mag = jnp.exp((j + 1.0) * dt_f * a_re_f)
    ph = (j + 1.0) * dt_f * a_im_f
    coef += [mag * jnp.cos(ph), mag * jnp.sin(ph)]
    coef = jnp.stack(coef)
    return w_in.astype(BF16), w_out.astype(BF16), coef, d_skip.reshape(1, n_g * n_c).astype(F32)


def _gelu_tanh(x):
    return 0.5 * x * (1.0 + jnp.tanh(math.sqrt(2.0 / math.pi) * (x + 0.044715 * (x * x * x))))


def _ssm_kernel(u_ref, win_ref, wout_ref, coef_ref, d_ref, wglu_ref, bglu_ref, o_ref,
                hre_ref, him_ref, cre_ref, cim_ref, *, rows, n_state, col_w):
    n_kb = win_ref.shape[0]
    sp = n_state // n_kb

    @pl.when(pl.program_id(1) == 0)
    def _():
        cre_ref[...] = jnp.zeros_like(cre_ref)
        cim_ref[...] = jnp.zeros_like(cim_ref)

    u = u_ref[0]
    ub = u.astype(BF16)
    for kb in range(n_kb):
        bu = _dot(ub[:, kb * LANES:(kb + 1) * LANES], win_ref[kb])
        hre_ref[:, kb * sp:(kb + 1) * sp] = bu[:, :sp]
        him_ref[:, kb * sp:(kb + 1) * sp] = bu[:, sp:]

    n_slab = rows // SUBLANES
    for c0 in range(0, n_state, col_w):
        cs = slice(c0, c0 + col_w)

        def body(i, carry, cs=cs):
            c_re, c_im = carry
            sl = pl.ds(pl.multiple_of(i * SUBLANES, SUBLANES), SUBLANES)
            x_re = hre_ref[sl, cs]
            x_im = him_ref[sl, cs]
            for si, sh in enumerate((1, 2, 4)):
                p_re = coef_ref[2 * si, :, cs]
                p_im = coef_ref[2 * si + 1, :, cs]
                s_re = pltpu.roll(x_re, sh, axis=0)
                s_im = pltpu.roll(x_im, sh, axis=0)
                x_re, x_im = (x_re + p_re * s_re - p_im * s_im,
                              x_im + p_re * s_im + p_im * s_re)
            p_re = coef_ref[6, :, cs]
            p_im = coef_ref[7, :, cs]
            h_re = x_re + p_re * c_re - p_im * c_im
            h_im = x_im + p_re * c_im + p_im * c_re
            hre_ref[sl, cs] = h_re
            him_ref[sl, cs] = h_im
            last = slice(SUBLANES - 1, SUBLANES)
            return (jnp.broadcast_to(h_re[last, :], h_re.shape),
                    jnp.broadcast_to(h_im[last, :], h_im.shape))

        c_re, c_im = lax.fori_loop(0, n_slab, body, (cre_ref[:, cs], cim_ref[:, cs]), unroll=True)
        cre_ref[:, cs] = c_re
        cim_ref[:, cs] = c_im

    ys = []
    for kb in range(n_kb):
        hcat = jnp.concatenate([hre_ref[:, kb * sp:(kb + 1) * sp].astype(BF16),
                                him_ref[:, kb * sp:(kb + 1) * sp].astype(BF16)], axis=1)
        ys.append(_dot(hcat, wout_ref[kb]))
    y = jnp.concatenate(ys, axis=1) + d_ref[...] * u
    y = _gelu_tanh(y)
    z = _dot(y.astype(BF16), wglu_ref[...]) + bglu_ref[...]
    o_ref[0] = y * jax.nn.sigmoid(z)


def _ssm(u, w_in, w_out, coef, d_flat, w_glu, b_glu):
    bsz, seq, ssm_w = u.shape
    n_state = coef.shape[-1]
    rows = min(SSM_ROWS, seq)
    full = lambda *shape: pl.BlockSpec(shape, lambda b, i: (0,) * len(shape))
    kern = functools.partial(_ssm_kernel, rows=rows, n_state=n_state, col_w=4 * LANES)
    return pl.pallas_call(
        kern,
        grid=(bsz, seq // rows),
        in_specs=[pl.BlockSpec((1, rows, ssm_w), lambda b, i: (b, i, 0)),
                  full(*w_in.shape), full(*w_out.shape), full(*coef.shape), full(1, ssm_w),
                  full(ssm_w, ssm_w), full(1, ssm_w)],
        out_specs=pl.BlockSpec((1, rows, ssm_w), lambda b, i: (b, i, 0)),
        out_shape=jax.ShapeDtypeStruct((bsz, seq, ssm_w), F32),
        scratch_shapes=[pltpu.VMEM((rows, n_state), F32), pltpu.VMEM((rows, n_state), F32),
                        pltpu.VMEM((SUBLANES, n_state), F32), pltpu.VMEM((SUBLANES, n_state), F32)],
        compiler_params=_cparams(("parallel", "arbitrary")),
        name="s5_glu",
    )(u, w_in, w_out, coef, d_flat, w_glu.astype(BF16), b_glu.reshape(1, ssm_w))


def _layer_norm(y, g, b):
    mu = jnp.mean(y, axis=-1, keepdims=True)
    yc = y - mu
    var = jnp.mean(yc * yc, axis=-1, keepdims=True)
    return yc * lax.rsqrt(var + EPS) * g + b


def _rms_norm(y, g):
    return y * lax.rsqrt(jnp.mean(y * y, axis=-1, keepdims=True) + EPS) * g


def _mix_kernel(x_ref, att_ref, ssm_ref, gatt_ref, gssm_ref, wout_ref, g1_ref, ln_g_ref, ln_b_ref,
                sc_ref, sh_ref, g2_ref, wsg_ref, wsu_ref, wsd_ref, h_ref, ht_ref, pre_ref, *, alpha, att_w):
    a_n = _rms_norm(att_ref[0], gatt_ref[...]).astype(BF16)
    s_n = _rms_norm(ssm_ref[0], gssm_ref[...]).astype(BF16)
    mix = _dot(a_n, wout_ref[0:att_w, :]) + _dot(s_n, wout_ref[att_w:, :])
    x1 = _layer_norm(alpha * x_ref[0] + g1_ref[0] * mix, ln_g_ref[...], ln_b_ref[...])
    h = x1 * (1.0 + sc_ref[0]) + sh_ref[0]
    h_ref[0] = h
    _store_token_tiles(ht_ref, h)
    hb = h.astype(BF16)
    hid = _silu(_dot(hb, wsg_ref[...])) * _dot(hb, wsu_ref[...])
    shared = _dot(hid.astype(BF16), wsd_ref[...])
    pre_ref[0] = alpha * x1 + g2_ref[0] * shared


def _mix(x, att, ssm, g_att, g_ssm, w_out, gate1, ln_g, ln_b, scale2, shift2, gate2,
         w_s_gate, w_s_up, w_s_down, alpha):
    bsz, seq, d = x.shape
    att_w = att.shape[-1]
    ssm_w = ssm.shape[-1]
    ff = w_s_gate.shape[1]
    tm = min(MIX_ROWS, seq)
    n_sub = d // LANES
    row = lambda b, i: (b, i, 0)
    per_b = lambda b, i: (b, 0, 0)
    full = lambda *shape: pl.BlockSpec(shape, lambda b, i: (0,) * len(shape))
    vec = lambda t: t.reshape(bsz, 1, d)
    return pl.pallas_call(
        functools.partial(_mix_kernel, alpha=alpha, att_w=att_w),
        grid=(bsz, seq // tm),
        in_specs=[pl.BlockSpec((1, tm, d), row), pl.BlockSpec((1, tm, att_w), row),
                  pl.BlockSpec((1, tm, ssm_w), row),
                  full(1, att_w), full(1, ssm_w), full(att_w + ssm_w, d),
                  pl.BlockSpec((1, 1, d), per_b), full(1, d), full(1, d),
                  pl.BlockSpec((1, 1, d), per_b), pl.BlockSpec((1, 1, d), per_b),
                  pl.BlockSpec((1, 1, d), per_b),
                  full(d, ff), full(d, ff), full(ff, d)],
        out_specs=[pl.BlockSpec((1, tm, d), row),
                   pl.BlockSpec((tm * n_sub, LANES), lambda b, i: (b * (seq // tm) + i, 0)),
                   pl.BlockSpec((1, tm, d), row)],
        out_shape=[jax.ShapeDtypeStruct((bsz, seq, d), F32),
                   jax.ShapeDtypeStruct((bsz * seq * n_sub, LANES), F32),
                   jax.ShapeDtypeStruct((bsz, seq, d), F32)],
        compiler_params=_cparams(("parallel", "parallel")),
        name="mix_ln1_shared",
    )(x, att, ssm, g_att.reshape(1, att_w), g_ssm.reshape(1, ssm_w), w_out.astype(BF16),
      vec(gate1), ln_g.reshape(1, d), ln_b.reshape(1, d), vec(scale2), vec(shift2), vec(gate2),
      w_s_gate.astype(BF16), w_s_up.astype(BF16), w_s_down.astype(BF16))


def _route_kernel(h_ref, wrt_ref, wrt_lo_ref, bias_ref, tri_ref, idx_ref, gate_ref, rank_ref, cnt_ref, carry_ref):
    n_e = wrt_ref.shape[0]
    tm = h_ref.shape[0]
    per_group = n_e // N_EXPERT_GROUPS
    neg = -jnp.inf

    @pl.when(pl.program_id(0) == 0)
    def _():
        carry_ref[...] = jnp.zeros_like(carry_ref)

    h_hi, h_lo = _split_bf16(h_ref[...])
    logits = (_dot_nt(wrt_ref[...], h_hi) + _dot_nt(wrt_ref[...], h_lo)
              + _dot_nt(wrt_lo_ref[...], h_hi))
    scores = jax.nn.sigmoid(logits)
    biased = scores + bias_ref[...]

    g3 = biased.reshape(N_EXPERT_GROUPS, per_group, tm)
    ridx = lax.broadcasted_iota(jnp.int32, g3.shape, 1).astype(F32)
    m1 = jnp.max(g3, axis=1, keepdims=True)
    first = jnp.min(jnp.where(g3 == m1, ridx, float(per_group)), axis=1, keepdims=True)
    m2 = jnp.max(jnp.where(ridx == first, neg, g3), axis=1, keepdims=True)
    gs = m1 + m2

    gidx = lax.broadcasted_iota(jnp.int32, gs.shape, 0).astype(F32)
    ok = jnp.zeros_like(gs)
    cur = gs
    for _ in range(TOPK_GROUPS):
        mx = jnp.max(cur, axis=0, keepdims=True)
        fi = jnp.min(jnp.where(cur == mx, gidx, float(N_EXPERT_GROUPS)), axis=0, keepdims=True)
        hit = gidx == fi
        ok = jnp.where(hit, 1.0, ok)
        cur = jnp.where(hit, neg, cur)
    masked = jnp.where(ok > 0.5, g3, neg).reshape(n_e, tm)

    eidx = lax.broadcasted_iota(jnp.int32, (n_e, tm), 0).astype(F32)
    onehot = jnp.zeros((n_e, tm), F32)
    cur = masked
    sel_idx = []
    sel_gate = []
    for _ in range(TOP_K):
        mx = jnp.max(cur, axis=0, keepdims=True)
        fi = jnp.min(jnp.where(cur == mx, eidx, float(n_e)), axis=0, keepdims=True)
        hit = eidx == fi
        sel_idx.append(fi)
        sel_gate.append(jnp.sum(jnp.where(hit, scores, 0.0), axis=0, keepdims=True))
        onehot = jnp.where(hit, 1.0, onehot)
        cur = jnp.where(hit, neg, cur)
    idx = jnp.concatenate(sel_idx, axis=0)
    gate = jnp.concatenate(sel_gate, axis=0)
    gate = gate / jnp.sum(gate, axis=0, keepdims=True) * ROUTED_SCALE

    prior = _dot(onehot.astype(BF16), tri_ref[...]) + carry_ref[:, 0:1]
    ranks = [jnp.sum(jnp.where(eidx == sel_idx[k], prior, 0.0), axis=0, keepdims=True)
             for k in range(TOP_K)]
    rank = jnp.concatenate(ranks, axis=0)
    carry = carry_ref[...] + jnp.sum(onehot, axis=1, keepdims=True)
    carry_ref[...] = carry
    cnt_ref[...] = carry

    idx_ref[...] = idx.astype(jnp.int32)
    gate_ref[...] = gate
    rank_ref[...] = rank.astype(jnp.int32)


def _route(h2d, w_router, router_bias):
    n_tok, d = h2d.shape
    n_e = w_router.shape[1]
    tm = min(ROUTE_COLS, n_tok)
    tri = (np.arange(tm)[:, None] < np.arange(tm)[None, :]).astype(np.float32)
    w_hi, w_lo = _split_bf16(w_router.T.astype(F32))
    col = lambda i: (0, i)
    full = lambda *shape: pl.BlockSpec(shape, lambda i: (0,) * len(shape))
    idx, gate, rank, cnt = pl.pallas_call(
        _route_kernel,
        grid=(n_tok // tm,),
        in_specs=[pl.BlockSpec((tm, d), lambda i: (i, 0)), full(n_e, d), full(n_e, d), full(n_e, 1),
                  full(tm, tm)],
        out_specs=[pl.BlockSpec((TOP_K, tm), col), pl.BlockSpec((TOP_K, tm), col),
                   pl.BlockSpec((TOP_K, tm), col), full(n_e, LANES)],
        out_shape=[jax.ShapeDtypeStruct((TOP_K, n_tok), jnp.int32),
                   jax.ShapeDtypeStruct((TOP_K, n_tok), F32),
                   jax.ShapeDtypeStruct((TOP_K, n_tok), jnp.int32),
                   jax.ShapeDtypeStruct((n_e, LANES), F32)],
        scratch_shapes=[pltpu.VMEM((n_e, LANES), F32)],
        compiler_params=_cparams(("arbitrary",)),
        name="router_topk",
    )(h2d, w_hi, w_lo, router_bias.reshape(n_e, 1).astype(F32), jnp.asarray(tri, BF16))
    return idx, gate, rank, cnt[:, 0].astype(jnp.int32)


def _sc_mesh():
    return plsc.VectorSubcoreMesh(core_axis_name="core", subcore_axis_name="subcore")


def _dispatch(h_tiles, gate_rows, dest_windows, n_rows):
    n_tok, n_sub, _ = h_tiles.shape
    win = DISPATCH_WINDOW
    index_spec = pl.BlockSpec((1, TOP_K * win), lambda i: (i, 0))

    def scatter(body, src, src_spec, out_type):
        @pl.kernel(out_type=out_type, mesh=_sc_mesh(), scratch_types=[])
        def run(s_hbm, d_hbm, o_hbm):
            pltpu.emit_pipeline(
                functools.partial(body, o_hbm),
                grid=(n_tok // win,),
                in_specs=[src_spec, index_spec],
                out_specs=[],
                core_axis_name=("core", "subcore"),
                dimension_semantics=(pltpu.PARALLEL,),
            )(s_hbm, d_hbm)
        return run(src, dest_windows)

    def tile_window(o_hbm, x_vmem, d_vmem):
        for k in range(TOP_K):
            pltpu.sync_copy(x_vmem, o_hbm.at[d_vmem.at[0, pl.ds(k * win, win)]])

    def gate_window(o_hbm, g_vmem, d_vmem):
        for k in range(TOP_K):
            pltpu.sync_copy(g_vmem.at[k], o_hbm.at[d_vmem.at[0, pl.ds(k * win, win)]])

    xs = scatter(tile_window, h_tiles, pl.BlockSpec((win, n_sub, LANES), lambda i: (i, 0, 0)),
                 jax.ShapeDtypeStruct((n_rows, n_sub, LANES), h_tiles.dtype))
    row_gates = scatter(gate_window, gate_rows, pl.BlockSpec((TOP_K, win, GATE_LANES), lambda i: (0, i, 0)),
                        jax.ShapeDtypeStruct((n_rows, GATE_LANES), F32))
    return xs, row_gates


def _gather_sum(ys_tiles, dest_windows, n_tok):
    _, n_sub, _ = ys_tiles.shape
    win = DISPATCH_WINDOW

    @pl.kernel(out_type=jax.ShapeDtypeStruct((n_tok, n_sub, LANES), ys_tiles.dtype), mesh=_sc_mesh(),
               scratch_types=[])
    def gather_tiles(y_hbm, d_hbm, o_hbm):
        def window(d_vmem, o_vmem):
            pltpu.sync_copy(y_hbm.at[d_vmem.at[0, pl.ds(0, win)]], o_vmem)
            for k in range(1, TOP_K):
                pltpu.sync_copy(y_hbm.at[d_vmem.at[0, pl.ds(k * win, win)]], o_vmem, add=True)
        pltpu.emit_pipeline(
            window,
            grid=(n_tok // win,),
            in_specs=[pl.BlockSpec((1, TOP_K * win), lambda i: (i, 0))],
            out_specs=[pl.BlockSpec((win, n_sub, LANES), lambda i: (i, 0, 0))],
            core_axis_name=("core", "subcore"),
            dimension_semantics=(pltpu.PARALLEL,),
        )(d_hbm, o_hbm)

    return gather_tiles(ys_tiles, dest_windows)


def _expert_kernel(first_ref, nblk_ref, cnt_ref, nused_ref, xs_ref, rg_ref, wg_ref, wu_ref, wd_ref, ys_ref,
                   xbuf_ref, gbuf_ref, ybuf_ref, wgb_ref, wub_ref, wdb_ref, xsem_ref, ysem_ref):
    e = pl.program_id(0)
    n_used = nused_ref[0]
    d = wgb_ref.shape[0]
    blk_rows = xbuf_ref.shape[1]

    class _InCopy:
        def __init__(self, g, slot):
            src = xs_ref.at[pl.ds(pl.multiple_of(g * blk_rows, blk_rows), blk_rows)]
            gsrc = rg_ref.at[pl.ds(pl.multiple_of(g * MOE_ROWS, MOE_ROWS), MOE_ROWS)]
            self.copies = (pltpu.make_async_copy(src, xbuf_ref.at[slot], xsem_ref.at[slot]),
                           pltpu.make_async_copy(gsrc, gbuf_ref.at[slot], xsem_ref.at[slot]))

        def start(self):
            for cp in self.copies:
                cp.start()

        def wait(self):
            for cp in self.copies:
                cp.wait()

    x_copy = _InCopy

    def y_copy(g, slot):
        dst = ys_ref.at[pl.ds(pl.multiple_of(g * blk_rows, blk_rows), blk_rows)]
        return pltpu.make_async_copy(ybuf_ref.at[slot], dst, ysem_ref.at[slot])

    n_slot = xbuf_ref.shape[0]

    @pl.when(e == 0)
    def _():
        for g in range(n_slot - 1):
            @pl.when(g < n_used)
            def _(g=g):
                x_copy(g, g).start()

    @pl.when(nblk_ref[e] > 0)
    def _():
        wgb_ref[...] = wg_ref[0].astype(BF16)
        wub_ref[...] = wu_ref[0].astype(BF16)
        wdb_ref[...] = wd_ref[0].astype(BF16)

    def block(j, carry):
        g = first_ref[e] + j
        slot = g & (n_slot - 1)
        x_copy(g, slot).wait()

        @pl.when(g + n_slot - 1 < n_used)
        def _():
            x_copy(g + n_slot - 1, (g + n_slot - 1) & (n_slot - 1)).start()

        @pl.when(g >= n_slot)
        def _():
            y_copy(g - n_slot, slot).wait()

        x = jnp.concatenate(_load_token_tiles(xbuf_ref.at[slot], MOE_ROWS, d), axis=1)
        row = lax.broadcasted_iota(jnp.int32, (MOE_ROWS, 1), 0)
        xb = jnp.where(row < cnt_ref[e] - j * MOE_ROWS, x, 0.0).astype(BF16)
        hid = _silu(_dot(xb, wgb_ref[...])) * _dot(xb, wub_ref[...])
        y = _dot(hid.astype(BF16), wdb_ref[...]) * gbuf_ref[slot][:, 0:1]
        _store_token_tiles(ybuf_ref.at[slot], jnp.where(row < cnt_ref[e] - j * MOE_ROWS, y, 0.0))
        y_copy(g, slot).start()
        return carry
    lax.fori_loop(0, nblk_ref[e], block, 0)

    @pl.when(e == pl.num_programs(0) - 1)
    def _():
        for back in range(1, n_slot + 1):
            @pl.when(n_used >= back)
            def _(back=back):
                y_copy(n_used - back, (n_used - back) & (n_slot - 1)).wait()


def _experts(xs_tiles, row_gates, plan, w_gate, w_up, w_down):
    n_rows, n_sub, _ = xs_tiles.shape
    d = n_sub * LANES
    n_e, _, ff = w_gate.shape
    blk_rows = MOE_ROWS * n_sub
    hbm = pl.BlockSpec(memory_space=pl.ANY)
    grid_spec = pltpu.PrefetchScalarGridSpec(
        num_scalar_prefetch=4,
        grid=(n_e,),
        in_specs=[hbm, hbm,
                  pl.BlockSpec((1, d, ff), lambda e, *_: (e, 0, 0)),
                  pl.BlockSpec((1, d, ff), lambda e, *_: (e, 0, 0)),
                  pl.BlockSpec((1, ff, d), lambda e, *_: (e, 0, 0))],
        out_specs=hbm,
        scratch_shapes=[pltpu.VMEM((EXPERT_SLOTS, blk_rows, LANES), F32),
                        pltpu.VMEM((EXPERT_SLOTS, MOE_ROWS, GATE_LANES), F32),
                        pltpu.VMEM((EXPERT_SLOTS, blk_rows, LANES), F32),
                        pltpu.VMEM((d, ff), BF16), pltpu.VMEM((d, ff), BF16), pltpu.VMEM((ff, d), BF16),
                        pltpu.SemaphoreType.DMA((EXPERT_SLOTS,)), pltpu.SemaphoreType.DMA((EXPERT_SLOTS,))],
    )
    ys = pl.pallas_call(
        _expert_kernel,
        grid_spec=grid_spec,
        out_shape=jax.ShapeDtypeStruct((n_rows * n_sub, LANES), F32),
        compiler_params=_cparams(("arbitrary",)),
        name="moe_experts",
    )(*plan, xs_tiles.reshape(n_rows * n_sub, LANES), row_gates, w_gate, w_up, w_down)
    return ys.reshape(n_rows, n_sub, LANES)


def _combine_kernel(dest_ref, ys_ref, pre_ref, g2_ref, ln_g_ref, ln_b_ref, o_ref, buf_ref, sem_ref):
    tm, d = pre_ref.shape[1], pre_ref.shape[2]
    n_sub = d // LANES

    def row_copy(t, k, src_row):
        slot = pl.multiple_of((k * tm + t) * n_sub, n_sub)
        return pltpu.make_async_copy(ys_ref.at[src_row], buf_ref.at[pl.ds(slot, n_sub)], sem_ref.at[0])

    def issue(g, c):
        t0 = g * DMA_ISSUE_UNROLL
        rows = [[dest_ref[(t0 + j) * TOP_K + k] for k in range(TOP_K)] for j in range(DMA_ISSUE_UNROLL)]
        for j in range(DMA_ISSUE_UNROLL):
            for k in range(TOP_K):
                row_copy(t0 + j, k, rows[j][k]).start(priority=k % 2)
        return c
    lax.fori_loop(0, tm // DMA_ISSUE_UNROLL, issue, 0)

    def drain(t, c):
        for k in range(TOP_K):
            row_copy(t, k, dest_ref[t * TOP_K + k]).wait()
        return c
    lax.fori_loop(0, tm, drain, 0)

    pieces = []
    for s in range(n_sub):
        acc = None
        for k in range(TOP_K):
            term = buf_ref[pl.ds(k * tm * n_sub + s, tm, stride=n_sub), :]
            acc = term if acc is None else acc + term
        pieces.append(acc)
    routed = jnp.concatenate(pieces, axis=1)
    y = pre_ref[0] + g2_ref[0] * routed
    o_ref[0] = _layer_norm(y, ln_g_ref[...], ln_b_ref[...])


def _combine(ys, dest, pre, gate2, ln_g, ln_b, n_tiles):
    bsz, seq, d = pre.shape
    tm = min(DISPATCH_ROWS, seq)
    per_seq = seq // tm
    row = lambda i, *_: (i // per_seq, i % per_seq, 0)
    per_b = lambda i, *_: (i // per_seq, 0, 0)
    full = lambda *shape: pl.BlockSpec(shape, lambda i, *_: (0,) * len(shape))
    grid_spec = pltpu.PrefetchScalarGridSpec(
        num_scalar_prefetch=0,
        grid=(n_tiles,),
        in_specs=[pl.BlockSpec((TOP_K * tm,), lambda i: (i,), memory_space=pltpu.SMEM),
                  pl.BlockSpec(memory_space=pl.ANY),
                  pl.BlockSpec((1, tm, d), row), pl.BlockSpec((1, 1, d), per_b), full(1, d), full(1, d)],
        out_specs=pl.BlockSpec((1, tm, d), row),
        scratch_shapes=[pltpu.VMEM((TOP_K * tm * (d // LANES), LANES), F32), pltpu.SemaphoreType.DMA((1,))],
    )
    return pl.pallas_call(
        _combine_kernel,
        grid_spec=grid_spec,
        out_shape=jax.ShapeDtypeStruct((bsz, seq, d), F32),
        compiler_params=_cparams(("arbitrary",)),
        name="moe_combine_ln2",
    )(dest, ys, pre, gate2.reshape(bsz, 1, d), ln_g.reshape(1, d), ln_b.reshape(1, d))


def _final_ln_kernel(r_ref, pre_ref, g2_ref, ln_g_ref, ln_b_ref, *rest):
    o_ref = rest[-1]
    tm, d = pre_ref.shape[1], pre_ref.shape[2]
    routed = jnp.concatenate(_load_token_tiles(r_ref, tm, d), axis=1)
    y = pre_ref[0] + g2_ref[0] * routed
    o_ref[0] = _layer_norm(y, ln_g_ref[...], ln_b_ref[...])


def _final_ln(routed_tiles, pre, gate2, ln_g, ln_b, first_tile, partial=None):
    bsz, seq, d = pre.shape
    tm = min(DISPATCH_ROWS, seq)
    per_seq = seq // tm
    n_sub = d // LANES
    n_tiles = bsz * per_seq - first_tile
    row = lambda i: ((i + first_tile) // per_seq, (i + first_tile) % per_seq, 0)
    per_b = lambda i: ((i + first_tile) // per_seq, 0, 0)
    full = lambda *shape: pl.BlockSpec(shape, lambda i: (0,) * len(shape))
    in_specs = [pl.BlockSpec((tm * n_sub, LANES), lambda i: (i, 0)),
                pl.BlockSpec((1, tm, d), row), pl.BlockSpec((1, 1, d), per_b), full(1, d), full(1, d)]
    args = [routed_tiles.reshape(-1, LANES), pre, gate2.reshape(bsz, 1, d), ln_g.reshape(1, d),
            ln_b.reshape(1, d)]
    aliases = {}
    if partial is not None:
        in_specs.append(pl.BlockSpec(memory_space=pl.ANY))
        args.append(partial)
        aliases = {len(args) - 1: 0}
    return pl.pallas_call(
        _final_ln_kernel,
        grid=(n_tiles,),
        in_specs=in_specs,
        out_specs=pl.BlockSpec((1, tm, d), row),
        out_shape=jax.ShapeDtypeStruct((bsz, seq, d), F32),
        input_output_aliases=aliases,
        compiler_params=_cparams(("arbitrary",)),
        name="moe_final_ln2",
    )(*args)


def _dest_kernel(pstart_ref, idx_ref, rank_ref, dest_ref):
    idx = idx_ref[...]

    def body(e, acc):
        return acc + jnp.where(idx == e, pstart_ref[e], 0)
    dest_ref[...] = lax.fori_loop(0, pstart_ref.shape[0], body, rank_ref[...], unroll=8)


def _dest_rows(pstart, idx, rank):
    n_k, n_tok = idx.shape
    tn = min(4096, n_tok)
    grid_spec = pltpu.PrefetchScalarGridSpec(
        num_scalar_prefetch=1,
        grid=(n_tok // tn,),
        in_specs=[pl.BlockSpec((n_k, tn), lambda i, *_: (0, i))] * 2,
        out_specs=pl.BlockSpec((n_k, tn), lambda i, *_: (0, i)),
    )
    return pl.pallas_call(
        _dest_kernel,
        grid_spec=grid_spec,
        out_shape=jax.ShapeDtypeStruct((n_k, n_tok), jnp.int32),
        compiler_params=_cparams(("arbitrary",)),
        name="moe_dest",
    )(pstart, idx, rank)


def _moe_plan(idx, rank, cnt, n_tok):
    n_e = cnt.shape[0]
    padded = (cnt + MOE_ROWS - 1) // MOE_ROWS * MOE_ROWS
    pend = jnp.cumsum(padded).astype(jnp.int32)
    pstart = pend - padded
    dest_kt = _dest_rows(pstart, idx, rank)
    dest = dest_kt.T.reshape(-1)
    win = min(DISPATCH_WINDOW, n_tok)
    dest_windows = dest_kt.reshape(TOP_K, n_tok // win, win).transpose(1, 0, 2).reshape(n_tok // win, TOP_K * win)
    n_blk = n_tok * TOP_K // MOE_ROWS + n_e
    n_used = (pend[-1:] // MOE_ROWS).astype(jnp.int32)
    plan = ((pstart // MOE_ROWS).astype(jnp.int32), (padded // MOE_ROWS).astype(jnp.int32),
            cnt.astype(jnp.int32), n_used)
    return dest, dest_windows, plan, n_blk * MOE_ROWS


def kernel(x, c, rel_bias, w_ada, b_ada, w_in, ssm_a_re, ssm_a_im, ssm_b_re, ssm_b_im, ssm_c_re, ssm_c_im, ssm_d, ssm_log_dt, w_glu, b_glu, g_att, g_ssm, w_out, ln1_g, ln1_b, w_router, router_bias, w_e_gate, w_e_up, w_e_down, w_s_gate, w_s_up, w_s_down, ln2_g, ln2_b):
    bsz, seq, d = x.shape
    depth = w_ada.shape[0]
    alpha = (2 * depth) ** 0.25
    att_w = g_att.shape[-1]
    tables = _att_tables(rel_bias)
    for layer in range(depth):
        ada = _ada(c, w_ada[layer], b_ada[layer])
        shift1, scale1, gate1, shift2, scale2, gate2 = jnp.split(ada, 6, axis=-1)
        q, k, v, u = _inproj(x, scale1, shift1, w_in[layer], att_w)
        att = _attention(q, k, v, tables)
        ssm_prm = _ssm_params(ssm_a_re[layer], ssm_a_im[layer], ssm_b_re[layer], ssm_b_im[layer],
                              ssm_c_re[layer], ssm_c_im[layer], ssm_d[layer], ssm_log_dt[layer])
        ssm = _ssm(u, *ssm_prm, w_glu[layer], b_glu[layer])
        h2, h2_tiles, pre = _mix(x, att, ssm, g_att[layer], g_ssm[layer], w_out[layer], gate1,
                                 ln1_g[layer], ln1_b[layer], scale2, shift2, gate2, w_s_gate[layer],
                                 w_s_up[layer], w_s_down[layer], alpha)
        h2d = h2.reshape(bsz * seq, d)
        idx, gate, rank, cnt = _route(h2d, w_router[layer], router_bias[layer])
        dest, dest_windows, plan, n_rows = _moe_plan(idx, rank, cnt, bsz * seq)
        n_tok = bsz * seq
        gate_rows = jnp.broadcast_to(gate[:, :, None], (TOP_K, n_tok, GATE_LANES))
        xs, row_gates = _dispatch(h2_tiles.reshape(n_tok, d // LANES, LANES), gate_rows, dest_windows, n_rows)
        ys = _experts(xs, row_gates, plan, w_e_gate[layer], w_e_up[layer], w_e_down[layer])
        tm = min(DISPATCH_ROWS, seq)
        tc_tiles = min(COMBINE_TC_TILES, n_tok // tm)
        tc_tokens = tc_tiles * tm
        win = min(DISPATCH_WINDOW, n_tok)
        partial = (_combine(ys, dest, pre, gate2, ln2_g[layer], ln2_b[layer], tc_tiles)
                   if tc_tiles > 0 else None)
        routed = _gather_sum(ys, dest_windows[tc_tokens // win:], n_tok - tc_tokens)
        x = _final_ln(routed, pre, gate2, ln2_g[layer], ln2_b[layer], tc_tiles, partial)
    return x
```

```python
import functools
import math

import jax
import jax.numpy as jnp
import numpy as np
from jax import lax
from jax.experimental import pallas as pl
from jax.experimental.pallas import tpu as pltpu
from jax.experimental.pallas import tpu_sc as plsc

F32 = jnp.float32
BF16 = jnp.bfloat16

HEAD_DIM = 64
ATT_BLOCK = 128
PATTERNS = ((128, 1), (512, 4), (2048, 16))
N_BUCKETS = 32
MAX_DISTANCE = 2048
TOP_K = 8
N_EXPERT_GROUPS = 8
TOPK_GROUPS = 4
ROUTED_SCALE = 2.5
EPS = 1e-5
NEG_INF = -1e30
LOG2_E = math.log2(math.e)

LANES = 128
SUBLANES = 8
VMEM_LIMIT_BYTES = 56 * 1024 * 1024

PROJ_ROWS = 512
SSM_ROWS = 256
MIX_ROWS = 512
ROUTE_COLS = 512
MOE_ROWS = 256
DISPATCH_ROWS = 256
COMBINE_TC_TILES = 48
GATE_LANES = LANES
DISPATCH_WINDOW = 32
ATT_UNITS_PER_STEP = 8
DMA_ISSUE_UNROLL = 4
EXPERT_SLOTS = 4


def _cparams(sem, vmem=VMEM_LIMIT_BYTES):
    return pltpu.CompilerParams(dimension_semantics=sem, vmem_limit_bytes=vmem)


def _dot(a, b):
    return jnp.dot(a, b, preferred_element_type=F32)


def _dot_nt(a, b):
    return lax.dot_general(a, b, (((1,), (1,)), ((), ())), preferred_element_type=F32)


def _silu(x):
    return x * jax.nn.sigmoid(x)


def _store_token_tiles(ref, x):
    m, d = x.shape
    n_sub = d // LANES
    for s in range(n_sub):
        ref[pl.ds(s, m, stride=n_sub), :] = x[:, s * LANES:(s + 1) * LANES]


def _load_token_tiles(ref, m, d, row0=0):
    n_sub = d // LANES
    return [ref[pl.ds(row0 + s, m, stride=n_sub), :] for s in range(n_sub)]


def _split_bf16(x):
    hi = x.astype(BF16)
    lo = (x - hi.astype(F32)).astype(BF16)
    return hi, lo


def _ada_kernel(c_ref, w_ref, b_ref, o_ref):
    c = c_ref[...]
    a_hi, a_lo = _split_bf16(_silu(c))
    w_hi, w_lo = _split_bf16(w_ref[...])
    acc = _dot(a_hi, w_hi) + _dot(a_hi, w_lo) + _dot(a_lo, w_hi)
    o_ref[...] = acc + b_ref[...]


def _ada(c, w_ada, b_ada):
    bsz, d = c.shape
    n = w_ada.shape[1]
    rows = SUBLANES
    c_pad = jnp.zeros((rows, d), F32).at[:bsz].set(c)
    tn = 1024
    out = pl.pallas_call(
        _ada_kernel,
        grid=(n // tn,),
        in_specs=[pl.BlockSpec((rows, d), lambda j: (0, 0)),
                  pl.BlockSpec((d, tn), lambda j: (0, j)),
                  pl.BlockSpec((1, tn), lambda j: (0, j))],
        out_specs=pl.BlockSpec((rows, tn), lambda j: (0, j)),
        out_shape=jax.ShapeDtypeStruct((rows, n), F32),
        compiler_params=_cparams(("parallel",)),
        name="ada",
    )(c_pad, w_ada, b_ada.reshape(1, n))
    return out[:bsz]


def _inproj_kernel(x_ref, sc_ref, sh_ref, w_ref, q_ref, k_ref, v_ref, u_ref, *, att_w, q_scale):
    h = (x_ref[0] * (1.0 + sc_ref[0]) + sh_ref[0]).astype(BF16)
    q_ref[0] = _dot(h, w_ref[:, 0:att_w]) * q_scale
    k_ref[0] = _dot(h, w_ref[:, att_w:2 * att_w])
    v_ref[0] = _dot(h, w_ref[:, 2 * att_w:3 * att_w])
    u_ref[0] = _dot(h, w_ref[:, 3 * att_w:])


def _inproj(x, scale, shift, w_in, att_w):
    bsz, seq, d = x.shape
    n = w_in.shape[1]
    ssm_w = n - 3 * att_w
    ts = min(PROJ_ROWS, seq)
    row = lambda b, i: (b, i, 0)
    per_b = lambda b, i: (b, 0, 0)
    kern = functools.partial(_inproj_kernel, att_w=att_w, q_scale=HEAD_DIM ** -0.5 * LOG2_E)
    return pl.pallas_call(
        kern,
        grid=(bsz, seq // ts),
        in_specs=[pl.BlockSpec((1, ts, d), row),
                  pl.BlockSpec((1, 1, d), per_b),
                  pl.BlockSpec((1, 1, d), per_b),
                  pl.BlockSpec((d, n), lambda b, i: (0, 0))],
        out_specs=[pl.BlockSpec((1, ts, att_w), row)] * 3 + [pl.BlockSpec((1, ts, ssm_w), row)],
        out_shape=[jax.ShapeDtypeStruct((bsz, seq, att_w), F32)] * 3
                  + [jax.ShapeDtypeStruct((bsz, seq, ssm_w), F32)],
        compiler_params=_cparams(("parallel", "parallel")),
        name="inproj",
    )(x, scale.reshape(bsz, 1, d), shift.reshape(bsz, 1, d), w_in.astype(BF16))


def _t5_bucket_np(dist):
    exact = N_BUCKETS // 2
    large = exact + (np.log(np.maximum(dist, 1).astype(np.float64) / exact)
                     / math.log(MAX_DISTANCE / exact) * (N_BUCKETS - exact)).astype(np.int64)
    return np.where(dist < exact, dist, np.minimum(large, N_BUCKETS - 1))


def _att_tables(rel_bias):
    qi = np.arange(ATT_BLOCK)[:, None]
    ki = np.arange(2 * ATT_BLOCK)[None, :]
    rel = qi + ATT_BLOCK - ki
    tabs = []
    for window, dil in PATTERNS:
        band = (rel >= 0) & (rel <= window // dil)
        bucket = _t5_bucket_np(np.maximum(rel, 0) * dil)
        onehot = (bucket[:, :, None] == np.arange(N_BUCKETS)[None, None, :]).astype(np.float32)
        bias = jnp.einsum('qkb,bh->hqk', onehot, rel_bias.astype(F32),
                          precision=lax.Precision.HIGHEST)
        full = jnp.where(band[None], bias * LOG2_E, NEG_INF)
        first = jnp.concatenate([full[:, :, ATT_BLOCK:], jnp.full_like(full[:, :, ATT_BLOCK:], NEG_INF)], axis=-1)
        tabs.append(jnp.stack([full, first]))
    return jnp.stack(tabs)


def _att_kernel(q_ref, k_ref, v_ref, tab_ref, o_ref, m_ref, l_ref, *, seq):
    lane = lax.broadcasted_iota(jnp.int32, (ATT_BLOCK, LANES), 1)
    head0 = lane < HEAD_DIM

    def rows(ref, start, n, dil):
        if dil == 1:
            return ref[0, pl.ds(pl.multiple_of(start, ATT_BLOCK), n), :]
        return ref[0, pl.ds(start, n, stride=dil), :]

    def lane_mask(n_rows, h):
        m = lax.broadcasted_iota(jnp.int32, (n_rows, LANES), 1) < HEAD_DIM
        return m if h == 0 else ~m

    def batch(pi, dil, nb, res, blk0, head, init):
        q_rows = nb * ATT_BLOCK
        kv_blocks = nb if head else nb + 1
        q_start = res + dil * ATT_BLOCK * blk0
        kv_start = q_start if head else q_start - dil * ATT_BLOCK
        if dil == 1:
            sl = pl.ds(pl.multiple_of(q_start, ATT_BLOCK), q_rows)
        else:
            sl = pl.ds(q_start, q_rows, stride=dil)
        q = rows(q_ref, q_start, q_rows, dil)
        k = rows(k_ref, kv_start, kv_blocks * ATT_BLOCK, dil).astype(BF16)
        v = rows(v_ref, kv_start, kv_blocks * ATT_BLOCK, dil)
        if not init:
            m_old, l_old, o_old = m_ref[sl, :], l_ref[sl, :], o_ref[0, sl, :]
        qh = [jnp.where(lane_mask(q_rows, h), q, 0.0).astype(BF16) for h in range(2)]
        vh = [jnp.where(lane_mask(kv_blocks * ATT_BLOCK, h), v, 1.0).astype(BF16) for h in range(2)]
        o_parts, l_parts, m_parts = [], [], []
        for j in range(nb):
            first = 1 if (head and j == 0) else 0
            kb = j if (not head or j == 0) else j - 1
            ksl = slice(kb * ATT_BLOCK, (kb + 2) * ATT_BLOCK)
            qsl = slice(j * ATT_BLOCK, (j + 1) * ATT_BLOCK)
            outs, ms = [], []
            for h in range(2):
                s = _dot_nt(qh[h][qsl], k[ksl]) + tab_ref[pi, first, h]
                m = jnp.max(s, axis=1, keepdims=True)
                p = jnp.exp2(s - m).astype(BF16)
                outs.append(_dot(p, vh[h][ksl]))
                ms.append(m)
            o_parts.append(jnp.where(head0, outs[0], outs[1]))
            l_parts.append(pltpu.roll(jnp.where(head0, outs[1], outs[0]), HEAD_DIM, axis=1))
            m_parts.append(jnp.where(head0, ms[0], ms[1]))
        o = jnp.concatenate(o_parts, axis=0)
        l = jnp.concatenate(l_parts, axis=0)
        m = jnp.concatenate(m_parts, axis=0)
        if not init:
            m_new = jnp.maximum(m_old, m)
            a_old = jnp.exp2(m_old - m_new)
            a_cur = jnp.exp2(m - m_new)
            o = o_old * a_old + o * a_cur
            l = l_old * a_old + l * a_cur
            m = m_new
        o_ref[0, sl, :] = o
        m_ref[sl, :] = m
        l_ref[sl, :] = l

    n_units = seq // ATT_BLOCK
    order = sorted(range(len(PATTERNS)), key=lambda p: -PATTERNS[p][1])
    for pos, pi in enumerate(order):
        dil = PATTERNS[pi][1]
        log_d = dil.bit_length() - 1
        res_blocks = seq // (dil * ATT_BLOCK)
        nb = min(ATT_UNITS_PER_STEP, res_blocks)
        per_res = res_blocks // nb
        init = pos == 0

        res_step = 2 if (nb < ATT_UNITS_PER_STEP and dil % 2 == 0) else 1

        def head_body(r, carry, pi=pi, dil=dil, nb=nb, init=init, res_step=res_step):
            for dr in range(res_step):
                batch(pi, dil, nb, r * res_step + dr, 0, True, init)
            return carry
        lax.fori_loop(0, dil // res_step, head_body, 0)

        if per_res > 1:
            def tail_body(i, carry, pi=pi, dil=dil, nb=nb, log_d=log_d, init=init):
                res = i & (dil - 1)
                blk0 = nb * (1 + (i >> log_d))
                batch(pi, dil, nb, res, blk0, False, init)
                return carry
            lax.fori_loop(0, dil * (per_res - 1), tail_body, 0)

    def finish(i, carry):
        sl = pl.ds(pl.multiple_of(i * ATT_BLOCK, ATT_BLOCK), ATT_BLOCK)
        o_ref[0, sl, :] = o_ref[0, sl, :] / l_ref[sl, :]
        return carry
    lax.fori_loop(0, n_units, finish, 0)


def _attention(q, k, v, tables):
    bsz, seq, att_w = q.shape
    n_pairs = att_w // LANES
    blk = lambda b, p: (b, 0, p)
    qkv_spec = pl.BlockSpec((1, seq, LANES), blk)
    n_pat = tables.shape[0]
    tab_spec = pl.BlockSpec((n_pat, 2, 2, ATT_BLOCK, 2 * ATT_BLOCK), lambda b, p: (0, 0, p, 0, 0))
    return pl.pallas_call(
        functools.partial(_att_kernel, seq=seq),
        grid=(bsz, n_pairs),
        in_specs=[qkv_spec, qkv_spec, qkv_spec, tab_spec],
        out_specs=pl.BlockSpec((1, seq, LANES), blk),
        out_shape=jax.ShapeDtypeStruct((bsz, seq, att_w), F32),
        scratch_shapes=[pltpu.VMEM((seq, LANES), F32), pltpu.VMEM((seq, LANES), F32)],
        compiler_params=_cparams(("parallel", "parallel")),
        name="dilated_attention",
    )(q, k, v, tables)


def _ssm_params(a_re, a_im, b_re, b_im, c_re, c_im, d_skip, log_dt):
    n_g, n_p = a_re.shape
    n_c = b_re.shape[-1]
    n_state = n_g * n_p
    dt = jnp.exp(log_dt.astype(F32))[:, None]

    def a_pow(kk):
        mag = jnp.exp(kk * dt * a_re)
        ph = kk * dt * a_im
        return mag * jnp.cos(ph), mag * jnp.sin(ph)

    ab_re, ab_im = a_pow(1.0)
    nr, ni = ab_re - 1.0, ab_im
    den = a_re * a_re + a_im * a_im
    f_re = (nr * a_re + ni * a_im) / den
    f_im = (ni * a_re - nr * a_im) / den
    bb_re = f_re[:, :, None] * b_re - f_im[:, :, None] * b_im
    bb_im = f_re[:, :, None] * b_im + f_im[:, :, None] * b_re
    eye = jnp.eye(n_g, dtype=F32)
    w_re = jnp.einsum('gpc,gh->gchp', bb_re, eye).reshape(n_g * n_c, n_state)
    w_im = jnp.einsum('gpc,gh->gchp', bb_im, eye).reshape(n_g * n_c, n_state)
    o_re = jnp.einsum('gcp,gh->gphc', c_re, eye).reshape(n_state, n_g * n_c)
    o_im = jnp.einsum('gcp,gh->gphc', c_im, eye).reshape(n_state, n_g * n_c)
    n_kb = (n_g * n_c) // LANES
    sp = n_state // n_kb
    w_in = jnp.stack([jnp.concatenate([w_re[kb * LANES:(kb + 1) * LANES, kb * sp:(kb + 1) * sp],
                                       w_im[kb * LANES:(kb + 1) * LANES, kb * sp:(kb + 1) * sp]], axis=1)
                      for kb in range(n_kb)])
    w_out = jnp.stack([jnp.concatenate([o_re[kb * sp:(kb + 1) * sp, kb * LANES:(kb + 1) * LANES],
                                        -o_im[kb * sp:(kb + 1) * sp, kb * LANES:(kb + 1) * LANES]], axis=0)
                       for kb in range(n_kb)])
    j = jnp.arange(SUBLANES, dtype=F32)[:, None]
    flat = lambda t: jnp.broadcast_to(t.reshape(1, n_state), (SUBLANES, n_state))
    coef = []
    for sh in (1, 2, 4):
        pr, pim = a_pow(float(sh))
        keep = (j >= sh).astype(F32)
        coef += [flat(pr) * keep, flat(pim) * keep]
    a_re_f = jnp.broadcast_to(a_re.reshape(1, n_state), (SUBLANES, n_state))
    a_im_f = jnp.broadcast_to(a_im.reshape(1, n_state), (SUBLANES, n_state))
    dt_f = jnp.broadcast_to(jnp.repeat(dt[:, 0], n_p).reshape(1, n_state), (SUBLANES, n_state))
    mag = jnp.exp((j + 1.0) * dt_f * a_re_f)
    ph = (j + 1.0) * dt_f * a_im_f
    coef += [mag * jnp.cos(ph), mag * jnp.sin(ph)]
    coef = jnp.stack(coef)
    return w_in.astype(BF16), w_out.astype(BF16), coef, d_skip.reshape(1, n_g * n_c).astype(F32)


def _gelu_tanh(x):
    return 0.5 * x * (1.0 + jnp.tanh(math.sqrt(2.0 / math.pi) * (x + 0.044715 * (x * x * x))))


def _ssm_kernel(u_ref, win_ref, wout_ref, coef_ref, d_ref, wglu_ref, bglu_ref, o_ref,
                hre_ref, him_ref, cre_ref, cim_ref, *, rows, n_state, col_w):
    n_kb = win_ref.shape[0]
    sp = n_state // n_kb

    @pl.when(pl.program_id(1) == 0)
    def _():
        cre_ref[...] = jnp.zeros_like(cre_ref)
        cim_ref[...] = jnp.zeros_like(cim_ref)

    u = u_ref[0]
    ub = u.astype(BF16)
    for kb in range(n_kb):
        bu = _dot(ub[:, kb * LANES:(kb + 1) * LANES], win_ref[kb])
        hre_ref[:, kb * sp:(kb + 1) * sp] = bu[:, :sp]
        him_ref[:, kb * sp:(kb + 1) * sp] = bu[:, sp:]

    n_slab = rows // SUBLANES
    for c0 in range(0, n_state, col_w):
        cs = slice(c0, c0 + col_w)

        def body(i, carry, cs=cs):
            c_re, c_im = carry
            sl = pl.ds(pl.multiple_of(i * SUBLANES, SUBLANES), SUBLANES)
            x_re = hre_ref[sl, cs]
            x_im = him_ref[sl, cs]
            for si, sh in enumerate((1, 2, 4)):
                p_re = coef_ref[2 * si, :, cs]
                p_im = coef_ref[2 * si + 1, :, cs]
                s_re = pltpu.roll(x_re, sh, axis=0)
                s_im = pltpu.roll(x_im, sh, axis=0)
                x_re, x_im = (x_re + p_re * s_re - p_im * s_im,
                              x_im + p_re * s_im + p_im * s_re)
            p_re = coef_ref[6, :, cs]
            p_im = coef_ref[7, :, cs]
            h_re = x_re + p_re * c_re - p_im * c_im
            h_im = x_im + p_re * c_im + p_im * c_re
            hre_ref[sl, cs] = h_re
            him_ref[sl, cs] = h_im
            last = slice(SUBLANES - 1, SUBLANES)
            return (jnp.broadcast_to(h_re[last, :], h_re.shape),
                    jnp.broadcast_to(h_im[last, :], h_im.shape))

        c_re, c_im = lax.fori_loop(0, n_slab, body, (cre_ref[:, cs], cim_ref[:, cs]), unroll=True)
        cre_ref[:, cs] = c_re
        cim_ref[:, cs] = c_im

    ys = []
    for kb in range(n_kb):
        hcat = jnp.concatenate([hre_ref[:, kb * sp:(kb + 1) * sp].astype(BF16),
                                him_ref[:, kb * sp:(kb + 1) * sp].astype(BF16)], axis=1)
        ys.append(_dot(hcat, wout_ref[kb]))
    y = jnp.concatenate(ys, axis=1) + d_ref[...] * u
    y = _gelu_tanh(y)
    z = _dot(y.astype(BF16), wglu_ref[...]) + bglu_ref[...]
    o_ref[0] = y * jax.nn.sigmoid(z)


def _ssm(u, w_in, w_out, coef, d_flat, w_glu, b_glu):
    bsz, seq, ssm_w = u.shape
    n_state = coef.shape[-1]
    rows = min(SSM_ROWS, seq)
    full = lambda *shape: pl.BlockSpec(shape, lambda b, i: (0,) * len(shape))
    kern = functools.partial(_ssm_kernel, rows=rows, n_state=n_state, col_w=4 * LANES)
    return pl.pallas_call(
        kern,
        grid=(bsz, seq // rows),
        in_specs=[pl.BlockSpec((1, rows, ssm_w), lambda b, i: (b, i, 0)),
                  full(*w_in.shape), full(*w_out.shape), full(*coef.shape), full(1, ssm_w),
                  full(ssm_w, ssm_w), full(1, ssm_w)],
        out_specs=pl.BlockSpec((1, rows, ssm_w), lambda b, i: (b, i, 0)),
        out_shape=jax.ShapeDtypeStruct((bsz, seq, ssm_w), F32),
        scratch_shapes=[pltpu.VMEM((rows, n_state), F32), pltpu.VMEM((rows, n_state), F32),
                        pltpu.VMEM((SUBLANES, n_state), F32), pltpu.VMEM((SUBLANES, n_state), F32)],
        compiler_params=_cparams(("parallel", "arbitrary")),
        name="s5_glu",
    )(u, w_in, w_out, coef, d_flat, w_glu.astype(BF16), b_glu.reshape(1, ssm_w))


def _layer_norm(y, g, b):
    mu = jnp.mean(y, axis=-1, keepdims=True)
    yc = y - mu
    var = jnp.mean(yc * yc, axis=-1, keepdims=True)
    return yc * lax.rsqrt(var + EPS) * g + b


def _rms_norm(y, g):
    return y * lax.rsqrt(jnp.mean(y * y, axis=-1, keepdims=True) + EPS) * g


def _mix_kernel(x_ref, att_ref, ssm_ref, gatt_ref, gssm_ref, wout_ref, g1_ref, ln_g_ref, ln_b_ref,
                sc_ref, sh_ref, g2_ref, wsg_ref, wsu_ref, wsd_ref, h_ref, ht_ref, pre_ref, *, alpha, att_w):
    a_n = _rms_norm(att_ref[0], gatt_ref[...]).astype(BF16)
    s_n = _rms_norm(ssm_ref[0], gssm_ref[...]).astype(BF16)
    mix = _dot(a_n, wout_ref[0:att_w, :]) + _dot(s_n, wout_ref[att_w:, :])
    x1 = _layer_norm(alpha * x_ref[0] + g1_ref[0] * mix, ln_g_ref[...], ln_b_ref[...])
    h = x1 * (1.0 + sc_ref[0]) + sh_ref[0]
    h_ref[0] = h
    _store_token_tiles(ht_ref, h)
    hb = h.astype(BF16)
    hid = _silu(_dot(hb, wsg_ref[...])) * _dot(hb, wsu_ref[...])
    shared = _dot(hid.astype(BF16), wsd_ref[...])
    pre_ref[0] = alpha * x1 + g2_ref[0] * shared


def _mix(x, att, ssm, g_att, g_ssm, w_out, gate1, ln_g, ln_b, scale2, shift2, gate2,
         w_s_gate, w_s_up, w_s_down, alpha):
    bsz, seq, d = x.shape
    att_w = att.shape[-1]
    ssm_w = ssm.shape[-1]
    ff = w_s_gate.shape[1]
    tm = min(MIX_ROWS, seq)
    n_sub = d // LANES
    row = lambda b, i: (b, i, 0)
    per_b = lambda b, i: (b, 0, 0)
    full = lambda *shape: pl.BlockSpec(shape, lambda b, i: (0,) * len(shape))
    vec = lambda t: t.reshape(bsz, 1, d)
    return pl.pallas_call(
        functools.partial(_mix_kernel, alpha=alpha, att_w=att_w),
        grid=(bsz, seq // tm),
        in_specs=[pl.BlockSpec((1, tm, d), row), pl.BlockSpec((1, tm, att_w), row),
                  pl.BlockSpec((1, tm, ssm_w), row),
                  full(1, att_w), full(1, ssm_w), full(att_w + ssm_w, d),
                  pl.BlockSpec((1, 1, d), per_b), full(1, d), full(1, d),
                  pl.BlockSpec((1, 1, d), per_b), pl.BlockSpec((1, 1, d), per_b),
                  pl.BlockSpec((1, 1, d), per_b),
                  full(d, ff), full(d, ff), full(ff, d)],
        out_specs=[pl.BlockSpec((1, tm, d), row),
                   pl.BlockSpec((tm * n_sub, LANES), lambda b, i: (b * (seq // tm) + i, 0)),
                   pl.BlockSpec((1, tm, d), row)],
        out_shape=[jax.ShapeDtypeStruct((bsz, seq, d), F32),
                   jax.ShapeDtypeStruct((bsz * seq * n_sub, LANES), F32),
                   jax.ShapeDtypeStruct((bsz, seq, d), F32)],
        compiler_params=_cparams(("parallel", "parallel")),
        name="mix_ln1_shared",
    )(x, att, ssm, g_att.reshape(1, att_w), g_ssm.reshape(1, ssm_w), w_out.astype(BF16),
      vec(gate1), ln_g.reshape(1, d), ln_b.reshape(1, d), vec(scale2), vec(shift2), vec(gate2),
      w_s_gate.astype(BF16), w_s_up.astype(BF16), w_s_down.astype(BF16))


def _route_kernel(h_ref, wrt_ref, wrt_lo_ref, bias_ref, tri_ref, idx_ref, gate_ref, rank_ref, cnt_ref, carry_ref):
    n_e = wrt_ref.shape[0]
    tm = h_ref.shape[0]
    per_group = n_e // N_EXPERT_GROUPS
    neg = -jnp.inf

    @pl.when(pl.program_id(0) == 0)
    def _():
        carry_ref[...] = jnp.zeros_like(carry_ref)

    h_hi, h_lo = _split_bf16(h_ref[...])
    logits = (_dot_nt(wrt_ref[...], h_hi) + _dot_nt(wrt_ref[...], h_lo)
              + _dot_nt(wrt_lo_ref[...], h_hi))
    scores = jax.nn.sigmoid(logits)
    biased = scores + bias_ref[...]

    g3 = biased.reshape(N_EXPERT_GROUPS, per_group, tm)
    ridx = lax.broadcasted_iota(jnp.int32, g3.shape, 1).astype(F32)
    m1 = jnp.max(g3, axis=1, keepdims=True)
    first = jnp.min(jnp.where(g3 == m1, ridx, float(per_group)), axis=1, keepdims=True)
    m2 = jnp.max(jnp.where(ridx == first, neg, g3), axis=1, keepdims=True)
    gs = m1 + m2

    gidx = lax.broadcasted_iota(jnp.int32, gs.shape, 0).astype(F32)
    ok = jnp.zeros_like(gs)
    cur = gs
    for _ in range(TOPK_GROUPS):
        mx = jnp.max(cur, axis=0, keepdims=True)
        fi = jnp.min(jnp.where(cur == mx, gidx, float(N_EXPERT_GROUPS)), axis=0, keepdims=True)
        hit = gidx == fi
        ok = jnp.where(hit, 1.0, ok)
        cur = jnp.where(hit, neg, cur)
    masked = jnp.where(ok > 0.5, g3, neg).reshape(n_e, tm)

    eidx = lax.broadcasted_iota(jnp.int32, (n_e, tm), 0).astype(F32)
    onehot = jnp.zeros((n_e, tm), F32)
    cur = masked
    sel_idx = []
    sel_gate = []
    for _ in range(TOP_K):
        mx = jnp.max(cur, axis=0, keepdims=True)
        fi = jnp.min(jnp.where(cur == mx, eidx, float(n_e)), axis=0, keepdims=True)
        hit = eidx == fi
        sel_idx.append(fi)
        sel_gate.append(jnp.sum(jnp.where(hit, scores, 0.0), axis=0, keepdims=True))
        onehot = jnp.where(hit, 1.0, onehot)
        cur = jnp.where(hit, neg, cur)
    idx = jnp.concatenate(sel_idx, axis=0)
    gate = jnp.concatenate(sel_gate, axis=0)
    gate = gate / jnp.sum(gate, axis=0, keepdims=True) * ROUTED_SCALE

    prior = _dot(onehot.astype(BF16), tri_ref[...]) + carry_ref[:, 0:1]
    ranks = [jnp.sum(jnp.where(eidx == sel_idx[k], prior, 0.0), axis=0, keepdims=True)
             for k in range(TOP_K)]
    rank = jnp.concatenate(ranks, axis=0)
    carry = carry_ref[...] + jnp.sum(onehot, axis=1, keepdims=True)
    carry_ref[...] = carry
    cnt_ref[...] = carry

    idx_ref[...] = idx.astype(jnp.int32)
    gate_ref[...] = gate
    rank_ref[...] = rank.astype(jnp.int32)


def _route(h2d, w_router, router_bias):
    n_tok, d = h2d.shape
    n_e = w_router.shape[1]
    tm = min(ROUTE_COLS, n_tok)
    tri = (np.arange(tm)[:, None] < np.arange(tm)[None, :]).astype(np.float32)
    w_hi, w_lo = _split_bf16(w_router.T.astype(F32))
    col = lambda i: (0, i)
    full = lambda *shape: pl.BlockSpec(shape, lambda i: (0,) * len(shape))
    idx, gate, rank, cnt = pl.pallas_call(
        _route_kernel,
        grid=(n_tok // tm,),
        in_specs=[pl.BlockSpec((tm, d), lambda i: (i, 0)), full(n_e, d), full(n_e, d), full(n_e, 1),
                  full(tm, tm)],
        out_specs=[pl.BlockSpec((TOP_K, tm), col), pl.BlockSpec((TOP_K, tm), col),
                   pl.BlockSpec((TOP_K, tm), col), full(n_e, LANES)],
        out_shape=[jax.ShapeDtypeStruct((TOP_K, n_tok), jnp.int32),
                   jax.ShapeDtypeStruct((TOP_K, n_tok), F32),
                   jax.ShapeDtypeStruct((TOP_K, n_tok), jnp.int32),
                   jax.ShapeDtypeStruct((n_e, LANES), F32)],
        scratch_shapes=[pltpu.VMEM((n_e, LANES), F32)],
        compiler_params=_cparams(("arbitrary",)),
        name="router_topk",
    )(h2d, w_hi, w_lo, router_bias.reshape(n_e, 1).astype(F32), jnp.asarray(tri, BF16))
    return idx, gate, rank, cnt[:, 0].astype(jnp.int32)


def _sc_mesh():
    return plsc.VectorSubcoreMesh(core_axis_name="core", subcore_axis_name="subcore")


def _dispatch(h_tiles, gate_rows, dest_windows, n_rows):
    n_tok, n_sub, _ = h_tiles.shape
    win = DISPATCH_WINDOW
    index_spec = pl.BlockSpec((1, TOP_K * win), lambda i: (i, 0))

    def scatter(body, src, src_spec, out_type):
        @pl.kernel(out_type=out_type, mesh=_sc_mesh(), scratch_types=[])
        def run(s_hbm, d_hbm, o_hbm):
            pltpu.emit_pipeline(
                functools.partial(body, o_hbm),
                grid=(n_tok // win,),
                in_specs=[src_spec, index_spec],
                out_specs=[],
                core_axis_name=("core", "subcore"),
                dimension_semantics=(pltpu.PARALLEL,),
            )(s_hbm, d_hbm)
        return run(src, dest_windows)

    def tile_window(o_hbm, x_vmem, d_vmem):
        for k in range(TOP_K):
            pltpu.sync_copy(x_vmem, o_hbm.at[d_vmem.at[0, pl.ds(k * win, win)]])

    def gate_window(o_hbm, g_vmem, d_vmem):
        for k in range(TOP_K):
            pltpu.sync_copy(g_vmem.at[k], o_hbm.at[d_vmem.at[0, pl.ds(k * win, win)]])

    xs = scatter(tile_window, h_tiles, pl.BlockSpec((win, n_sub, LANES), lambda i: (i, 0, 0)),
                 jax.ShapeDtypeStruct((n_rows, n_sub, LANES), h_tiles.dtype))
    row_gates = scatter(gate_window, gate_rows, pl.BlockSpec((TOP_K, win, GATE_LANES), lambda i: (0, i, 0)),
                        jax.ShapeDtypeStruct((n_rows, GATE_LANES), F32))
    return xs, row_gates


def _gather_sum(ys_tiles, dest_windows, n_tok):
    _, n_sub, _ = ys_tiles.shape
    win = DISPATCH_WINDOW

    @pl.kernel(out_type=jax.ShapeDtypeStruct((n_tok, n_sub, LANES), ys_tiles.dtype), mesh=_sc_mesh(),
               scratch_types=[])
    def gather_tiles(y_hbm, d_hbm, o_hbm):
        def window(d_vmem, o_vmem):
            pltpu.sync_copy(y_hbm.at[d_vmem.at[0, pl.ds(0, win)]], o_vmem)
            for k in range(1, TOP_K):
                pltpu.sync_copy(y_hbm.at[d_vmem.at[0, pl.ds(k * win, win)]], o_vmem, add=True)
        pltpu.emit_pipeline(
            window,
            grid=(n_tok // win,),
            in_specs=[pl.BlockSpec((1, TOP_K * win), lambda i: (i, 0))],
            out_specs=[pl.BlockSpec((win, n_sub, LANES), lambda i: (i, 0, 0))],
            core_axis_name=("core", "subcore"),
            dimension_semantics=(pltpu.PARALLEL,),
        )(d_hbm, o_hbm)

    return gather_tiles(ys_tiles, dest_windows)


def _expert_kernel(first_ref, nblk_ref, cnt_ref, nused_ref, xs_ref, rg_ref, wg_ref, wu_ref, wd_ref, ys_ref,
                   xbuf_ref, gbuf_ref, ybuf_ref, wgb_ref, wub_ref, wdb_ref, xsem_ref, ysem_ref):
    e = pl.program_id(0)
    n_used = nused_ref[0]
    d = wgb_ref.shape[0]
    blk_rows = xbuf_ref.shape[1]

    class _InCopy:
        def __init__(self, g, slot):
            src = xs_ref.at[pl.ds(pl.multiple_of(g * blk_rows, blk_rows), blk_rows)]
            gsrc = rg_ref.at[pl.ds(pl.multiple_of(g * MOE_ROWS, MOE_ROWS), MOE_ROWS)]
            self.copies = (pltpu.make_async_copy(src, xbuf_ref.at[slot], xsem_ref.at[slot]),
                           pltpu.make_async_copy(gsrc, gbuf_ref.at[slot], xsem_ref.at[slot]))

        def start(self):
            for cp in self.copies:
                cp.start()

        def wait(self):
            for cp in self.copies:
                cp.wait()

    x_copy = _InCopy

    def y_copy(g, slot):
        dst = ys_ref.at[pl.ds(pl.multiple_of(g * blk_rows, blk_rows), blk_rows)]
        return pltpu.make_async_copy(ybuf_ref.at[slot], dst, ysem_ref.at[slot])

    n_slot = xbuf_ref.shape[0]

    @pl.when(e == 0)
    def _():
        for g in range(n_slot - 1):
            @pl.when(g < n_used)
            def _(g=g):
                x_copy(g, g).start()

    @pl.when(nblk_ref[e] > 0)
    def _():
        wgb_ref[...] = wg_ref[0].astype(BF16)
        wub_ref[...] = wu_ref[0].astype(BF16)
        wdb_ref[...] = wd_ref[0].astype(BF16)

    def block(j, carry):
        g = first_ref[e] + j
        slot = g & (n_slot - 1)
        x_copy(g, slot).wait()

        @pl.when(g + n_slot - 1 < n_used)
        def _():
            x_copy(g + n_slot - 1, (g + n_slot - 1) & (n_slot - 1)).start()

        @pl.when(g >= n_slot)
        def _():
            y_copy(g - n_slot, slot).wait()

        x = jnp.concatenate(_load_token_tiles(xbuf_ref.at[slot], MOE_ROWS, d), axis=1)
        row = lax.broadcasted_iota(jnp.int32, (MOE_ROWS, 1), 0)
        xb = jnp.where(row < cnt_ref[e] - j * MOE_ROWS, x, 0.0).astype(BF16)
        hid = _silu(_dot(xb, wgb_ref[...])) * _dot(xb, wub_ref[...])
        y = _dot(hid.astype(BF16), wdb_ref[...]) * gbuf_ref[slot][:, 0:1]
        _store_token_tiles(ybuf_ref.at[slot], jnp.where(row < cnt_ref[e] - j * MOE_ROWS, y, 0.0))
        y_copy(g, slot).start()
        return carry
    lax.fori_loop(0, nblk_ref[e], block, 0)

    @pl.when(e == pl.num_programs(0) - 1)
    def _():
        for back in range(1, n_slot + 1):
            @pl.when(n_used >= back)
            def _(back=back):
                y_copy(n_used - back, (n_used - back) & (n_slot - 1)).wait()


def _experts(xs_tiles, row_gates, plan, w_gate, w_up, w_down):
    n_rows, n_sub, _ = xs_tiles.shape
    d = n_sub * LANES
    n_e, _, ff = w_gate.shape
    blk_rows = MOE_ROWS * n_sub
    hbm = pl.BlockSpec(memory_space=pl.ANY)
    grid_spec = pltpu.PrefetchScalarGridSpec(
        num_scalar_prefetch=4,
        grid=(n_e,),
        in_specs=[hbm, hbm,
                  pl.BlockSpec((1, d, ff), lambda e, *_: (e, 0, 0)),
                  pl.BlockSpec((1, d, ff), lambda e, *_: (e, 0, 0)),
                  pl.BlockSpec((1, ff, d), lambda e, *_: (e, 0, 0))],
        out_specs=hbm,
        scratch_shapes=[pltpu.VMEM((EXPERT_SLOTS, blk_rows, LANES), F32),
                        pltpu.VMEM((EXPERT_SLOTS, MOE_ROWS, GATE_LANES), F32),
                        pltpu.VMEM((EXPERT_SLOTS, blk_rows, LANES), F32),
                        pltpu.VMEM((d, ff), BF16), pltpu.VMEM((d, ff), BF16), pltpu.VMEM((ff, d), BF16),
                        pltpu.SemaphoreType.DMA((EXPERT_SLOTS,)), pltpu.SemaphoreType.DMA((EXPERT_SLOTS,))],
    )
    ys = pl.pallas_call(
        _expert_kernel,
        grid_spec=grid_spec,
        out_shape=jax.ShapeDtypeStruct((n_rows * n_sub, LANES), F32),
        compiler_params=_cparams(("arbitrary",)),
        name="moe_experts",
    )(*plan, xs_tiles.reshape(n_rows * n_sub, LANES), row_gates, w_gate, w_up, w_down)
    return ys.reshape(n_rows, n_sub, LANES)


def _combine_kernel(dest_ref, ys_ref, pre_ref, g2_ref, ln_g_ref, ln_b_ref, o_ref, buf_ref, sem_ref):
    tm, d = pre_ref.shape[1], pre_ref.shape[2]
    n_sub = d // LANES

    def row_copy(t, k, src_row):
        slot = pl.multiple_of((k * tm + t) * n_sub, n_sub)
        return pltpu.make_async_copy(ys_ref.at[src_row], buf_ref.at[pl.ds(slot, n_sub)], sem_ref.at[0])

    def issue(g, c):
        t0 = g * DMA_ISSUE_UNROLL
        rows = [[dest_ref[(t0 + j) * TOP_K + k] for k in range(TOP_K)] for j in range(DMA_ISSUE_UNROLL)]
        for j in range(DMA_ISSUE_UNROLL):
            for k in range(TOP_K):
                row_copy(t0 + j, k, rows[j][k]).start(priority=k % 2)
        return c
    lax.fori_loop(0, tm // DMA_ISSUE_UNROLL, issue, 0)

    def drain(t, c):
        for k in range(TOP_K):
            row_copy(t, k, dest_ref[t * TOP_K + k]).wait()
        return c
    lax.fori_loop(0, tm, drain, 0)

    pieces = []
    for s in range(n_sub):
        acc = None
        for k in range(TOP_K):
            term = buf_ref[pl.ds(k * tm * n_sub + s, tm, stride=n_sub), :]
            acc = term if acc is None else acc + term
        pieces.append(acc)
    routed = jnp.concatenate(pieces, axis=1)
    y = pre_ref[0] + g2_ref[0] * routed
    o_ref[0] = _layer_norm(y, ln_g_ref[...], ln_b_ref[...])


def _combine(ys, dest, pre, gate2, ln_g, ln_b, n_tiles):
    bsz, seq, d = pre.shape
    tm = min(DISPATCH_ROWS, seq)
    per_seq = seq // tm
    row = lambda i, *_: (i // per_seq, i % per_seq, 0)
    per_b = lambda i, *_: (i // per_seq, 0, 0)
    full = lambda *shape: pl.BlockSpec(shape, lambda i, *_: (0,) * len(shape))
    grid_spec = pltpu.PrefetchScalarGridSpec(
        num_scalar_prefetch=0,
        grid=(n_tiles,),
        in_specs=[pl.BlockSpec((TOP_K * tm,), lambda i: (i,), memory_space=pltpu.SMEM),
                  pl.BlockSpec(memory_space=pl.ANY),
                  pl.BlockSpec((1, tm, d), row), pl.BlockSpec((1, 1, d), per_b), full(1, d), full(1, d)],
        out_specs=pl.BlockSpec((1, tm, d), row),
        scratch_shapes=[pltpu.VMEM((TOP_K * tm * (d // LANES), LANES), F32), pltpu.SemaphoreType.DMA((1,))],
    )
    return pl.pallas_call(
        _combine_kernel,
        grid_spec=grid_spec,
        out_shape=jax.ShapeDtypeStruct((bsz, seq, d), F32),
        compiler_params=_cparams(("arbitrary",)),
        name="moe_combine_ln2",
    )(dest, ys, pre, gate2.reshape(bsz, 1, d), ln_g.reshape(1, d), ln_b.reshape(1, d))


def _final_ln_kernel(r_ref, pre_ref, g2_ref, ln_g_ref, ln_b_ref, *rest):
    o_ref = rest[-1]
    tm, d = pre_ref.shape[1], pre_ref.shape[2]
    routed = jnp.concatenate(_load_token_tiles(r_ref, tm, d), axis=1)
    y = pre_ref[0] + g2_ref[0] * routed
    o_ref[0] = _layer_norm(y, ln_g_ref[...], ln_b_ref[...])


def _final_ln(routed_tiles, pre, gate2, ln_g, ln_b, first_tile, n_tiles, partial=None):
    bsz, seq, d = pre.shape
    tm = min(DISPATCH_ROWS, seq)
    per_seq = seq // tm
    n_sub = d // LANES
    row = lambda i: ((i + first_tile) // per_seq, (i + first_tile) % per_seq, 0)
    per_b = lambda i: ((i + first_tile) // per_seq, 0, 0)
    full = lambda *shape: pl.BlockSpec(shape, lambda i: (0,) * len(shape))
    in_specs = [pl.BlockSpec((tm * n_sub, LANES), lambda i: (i, 0)),
                pl.BlockSpec((1, tm, d), row), pl.BlockSpec((1, 1, d), per_b), full(1, d), full(1, d)]
    args = [routed_tiles.reshape(-1, LANES), pre, gate2.reshape(bsz, 1, d), ln_g.reshape(1, d),
            ln_b.reshape(1, d)]
    aliases = {}
    if partial is not None:
        in_specs.append(pl.BlockSpec(memory_space=pl.ANY))
        args.append(partial)
        aliases = {len(args) - 1: 0}
    return pl.pallas_call(
        _final_ln_kernel,
        grid=(n_tiles,),
        in_specs=in_specs,
        out_specs=pl.BlockSpec((1, tm, d), row),
        out_shape=jax.ShapeDtypeStruct((bsz, seq, d), F32),
        input_output_aliases=aliases,
        compiler_params=_cparams(("arbitrary",)),
        name="moe_final_ln2",
    )(*args)


def _dest_kernel(pstart_ref, idx_ref, rank_ref, dest_ref):
    idx = idx_ref[...]

    def body(e, acc):
        return acc + jnp.where(idx == e, pstart_ref[e], 0)
    dest_ref[...] = lax.fori_loop(0, pstart_ref.shape[0], body, rank_ref[...], unroll=8)


def _dest_rows(pstart, idx, rank):
    n_k, n_tok = idx.shape
    tn = min(4096, n_tok)
    grid_spec = pltpu.PrefetchScalarGridSpec(
        num_scalar_prefetch=1,
        grid=(n_tok // tn,),
        in_specs=[pl.BlockSpec((n_k, tn), lambda i, *_: (0, i))] * 2,
        out_specs=pl.BlockSpec((n_k, tn), lambda i, *_: (0, i)),
    )
    return pl.pallas_call(
        _dest_kernel,
        grid_spec=grid_spec,
        out_shape=jax.ShapeDtypeStruct((n_k, n_tok), jnp.int32),
        compiler_params=_cparams(("arbitrary",)),
        name="moe_dest",
    )(pstart, idx, rank)


def _moe_plan(idx, rank, cnt, n_tok):
    n_e = cnt.shape[0]
    padded = (cnt + MOE_ROWS - 1) // MOE_ROWS * MOE_ROWS
    pend = jnp.cumsum(padded).astype(jnp.int32)
    pstart = pend - padded
    dest_kt = _dest_rows(pstart, idx, rank)
    dest = dest_kt.T.reshape(-1)
    win = min(DISPATCH_WINDOW, n_tok)
    dest_windows = dest_kt.reshape(TOP_K, n_tok // win, win).transpose(1, 0, 2).reshape(n_tok // win, TOP_K * win)
    n_blk = n_tok * TOP_K // MOE_ROWS + n_e
    n_used = (pend[-1:] // MOE_ROWS).astype(jnp.int32)
    plan = ((pstart // MOE_ROWS).astype(jnp.int32), (padded // MOE_ROWS).astype(jnp.int32),
            cnt.astype(jnp.int32), n_used)
    return dest, dest_windows, plan, n_blk * MOE_ROWS


def kernel(x, c, rel_bias, w_ada, b_ada, w_in, ssm_a_re, ssm_a_im, ssm_b_re, ssm_b_im, ssm_c_re, ssm_c_im, ssm_d, ssm_log_dt, w_glu, b_glu, g_att, g_ssm, w_out, ln1_g, ln1_b, w_router, router_bias, w_e_gate, w_e_up, w_e_down, w_s_gate, w_s_up, w_s_down, ln2_g, ln2_b):
    bsz, seq, d = x.shape
    depth = w_ada.shape[0]
    alpha = (2 * depth) ** 0.25
    att_w = g_att.shape[-1]
    tables = _att_tables(rel_bias)
    for layer in range(depth):
        ada = _ada(c, w_ada[layer], b_ada[layer])
        shift1, scale1, gate1, shift2, scale2, gate2 = jnp.split(ada, 6, axis=-1)
        q, k, v, u = _inproj(x, scale1, shift1, w_in[layer], att_w)
        att = _attention(q, k, v, tables)
        ssm_prm = _ssm_params(ssm_a_re[layer], ssm_a_im[layer], ssm_b_re[layer], ssm_b_im[layer],
                              ssm_c_re[layer], ssm_c_im[layer], ssm_d[layer], ssm_log_dt[layer])
        ssm = _ssm(u, *ssm_prm, w_glu[layer], b_glu[layer])
        h2, h2_tiles, pre = _mix(x, att, ssm, g_att[layer], g_ssm[layer], w_out[layer], gate1,
                                 ln1_g[layer], ln1_b[layer], scale2, shift2, gate2, w_s_gate[layer],
                                 w_s_up[layer], w_s_down[layer], alpha)
        h2d = h2.reshape(bsz * seq, d)
        idx, gate, rank, cnt = _route(h2d, w_router[layer], router_bias[layer])
        dest, dest_windows, plan, n_rows = _moe_plan(idx, rank, cnt, bsz * seq)
        n_tok = bsz * seq
        gate_rows = jnp.broadcast_to(gate[:, :, None], (TOP_K, n_tok, GATE_LANES))
        xs, row_gates = _dispatch(h2_tiles.reshape(n_tok, d // LANES, LANES), gate_rows, dest_windows, n_rows)
        ys = _experts(xs, row_gates, plan, w_e_gate[layer], w_e_up[layer], w_e_down[layer])
        tm = min(DISPATCH_ROWS, seq)
        tc_tiles = min(COMBINE_TC_TILES, n_tok // tm)
        tc_tokens = tc_tiles * tm
        win = min(DISPATCH_WINDOW, n_tok)
        partial = (_combine(ys, dest, pre, gate2, ln2_g[layer], ln2_b[layer], tc_tiles)
                   if tc_tiles > 0 else None)
        sc_tiles = n_tok // tm - tc_tiles
        first = tc_tiles
        for part_tiles in (sc_tiles // 2, sc_tiles - sc_tiles // 2):
            if part_tiles == 0:
                continue
            lo, hi = first * tm // win, (first + part_tiles) * tm // win
            routed = _gather_sum(ys, dest_windows[lo:hi], part_tiles * tm)
            partial = _final_ln(routed, pre, gate2, ln2_g[layer], ln2_b[layer], first, part_tiles, partial)
            first += part_tiles
        x = partial
    return x
```

```python
import functools
import math

import jax
import jax.numpy as jnp
import numpy as np
from jax import lax
from jax.experimental import pallas as pl
from jax.experimental.pallas import tpu as pltpu
from jax.experimental.pallas import tpu_sc as plsc

F32 = jnp.float32
BF16 = jnp.bfloat16

HEAD_DIM = 64
ATT_BLOCK = 128
PATTERNS = ((128, 1), (512, 4), (2048, 16))
N_BUCKETS = 32
MAX_DISTANCE = 2048
TOP_K = 8
N_EXPERT_GROUPS = 8
TOPK_GROUPS = 4
ROUTED_SCALE = 2.5
EPS = 1e-5
NEG_INF = -1e30
LOG2_E = math.log2(math.e)

LANES = 128
SUBLANES = 8
VMEM_LIMIT_BYTES = 56 * 1024 * 1024

PROJ_ROWS = 512
SSM_ROWS = 512
MIX_ROWS = 512
ROUTE_COLS = 512
MOE_ROWS = 256
DISPATCH_ROWS = 256
COMBINE_TC_TILES = 48
GATE_LANES = LANES
DISPATCH_WINDOW = 32
ATT_UNITS_PER_STEP = 8
DMA_ISSUE_UNROLL = 4
EXPERT_SLOTS = 4


def _cparams(sem, vmem=VMEM_LIMIT_BYTES):
    return pltpu.CompilerParams(dimension_semantics=sem, vmem_limit_bytes=vmem)


def _dot(a, b):
    return jnp.dot(a, b, preferred_element_type=F32)


def _dot_nt(a, b):
    return lax.dot_general(a, b, (((1,), (1,)), ((), ())), preferred_element_type=F32)


def _silu(x):
    return x * jax.nn.sigmoid(x)


def _store_token_tiles(ref, x):
    m, d = x.shape
    n_sub = d // LANES
    for s in range(n_sub):
        ref[pl.ds(s, m, stride=n_sub), :] = x[:, s * LANES:(s + 1) * LANES]


def _load_token_tiles(ref, m, d, row0=0):
    n_sub = d // LANES
    return [ref[pl.ds(row0 + s, m, stride=n_sub), :] for s in range(n_sub)]


def _split_bf16(x):
    hi = x.astype(BF16)
    lo = (x - hi.astype(F32)).astype(BF16)
    return hi, lo


def _ada_kernel(c_ref, w_ref, b_ref, o_ref):
    c = c_ref[...]
    a_hi, a_lo = _split_bf16(_silu(c))
    w_hi, w_lo = _split_bf16(w_ref[...])
    acc = _dot(a_hi, w_hi) + _dot(a_hi, w_lo) + _dot(a_lo, w_hi)
    o_ref[...] = acc + b_ref[...]


def _ada(c, w_ada, b_ada):
    bsz, d = c.shape
    n = w_ada.shape[1]
    rows = SUBLANES
    c_pad = jnp.zeros((rows, d), F32).at[:bsz].set(c)
    tn = 1024
    out = pl.pallas_call(
        _ada_kernel,
        grid=(n // tn,),
        in_specs=[pl.BlockSpec((rows, d), lambda j: (0, 0)),
                  pl.BlockSpec((d, tn), lambda j: (0, j)),
                  pl.BlockSpec((1, tn), lambda j: (0, j))],
        out_specs=pl.BlockSpec((rows, tn), lambda j: (0, j)),
        out_shape=jax.ShapeDtypeStruct((rows, n), F32),
        compiler_params=_cparams(("parallel",)),
        name="ada",
    )(c_pad, w_ada, b_ada.reshape(1, n))
    return out[:bsz]


def _inproj_kernel(x_ref, sc_ref, sh_ref, w_ref, q_ref, k_ref, v_ref, u_ref, *, att_w, q_scale):
    h = (x_ref[0] * (1.0 + sc_ref[0]) + sh_ref[0]).astype(BF16)
    q_ref[0] = _dot(h, w_ref[:, 0:att_w]) * q_scale
    k_ref[0] = _dot(h, w_ref[:, att_w:2 * att_w])
    v_ref[0] = _dot(h, w_ref[:, 2 * att_w:3 * att_w])
    u_ref[0] = _dot(h, w_ref[:, 3 * att_w:])


def _inproj(x, scale, shift, w_in, att_w):
    bsz, seq, d = x.shape
    n = w_in.shape[1]
    ssm_w = n - 3 * att_w
    ts = min(PROJ_ROWS, seq)
    row = lambda b, i: (b, i, 0)
    per_b = lambda b, i: (b, 0, 0)
    kern = functools.partial(_inproj_kernel, att_w=att_w, q_scale=HEAD_DIM ** -0.5 * LOG2_E)
    return pl.pallas_call(
        kern,
        grid=(bsz, seq // ts),
        in_specs=[pl.BlockSpec((1, ts, d), row),
                  pl.BlockSpec((1, 1, d), per_b),
                  pl.BlockSpec((1, 1, d), per_b),
                  pl.BlockSpec((d, n), lambda b, i: (0, 0))],
        out_specs=[pl.BlockSpec((1, ts, att_w), row)] * 3 + [pl.BlockSpec((1, ts, ssm_w), row)],
        out_shape=[jax.ShapeDtypeStruct((bsz, seq, att_w), F32)] * 3
                  + [jax.ShapeDtypeStruct((bsz, seq, ssm_w), F32)],
        compiler_params=_cparams(("parallel", "parallel")),
        name="inproj",
    )(x, scale.reshape(bsz, 1, d), shift.reshape(bsz, 1, d), w_in.astype(BF16))


def _t5_bucket_np(dist):
    exact = N_BUCKETS // 2
    large = exact + (np.log(np.maximum(dist, 1).astype(np.float64) / exact)
                     / math.log(MAX_DISTANCE / exact) * (N_BUCKETS - exact)).astype(np.int64)
    return np.where(dist < exact, dist, np.minimum(large, N_BUCKETS - 1))


def _att_tables(rel_bias):
    qi = np.arange(ATT_BLOCK)[:, None]
    ki = np.arange(2 * ATT_BLOCK)[None, :]
    rel = qi + ATT_BLOCK - ki
    tabs = []
    for window, dil in PATTERNS:
        band = (rel >= 0) & (rel <= window // dil)
        bucket = _t5_bucket_np(np.maximum(rel, 0) * dil)
        onehot = (bucket[:, :, None] == np.arange(N_BUCKETS)[None, None, :]).astype(np.float32)
        bias = jnp.einsum('qkb,bh->hqk', onehot, rel_bias.astype(F32),
                          precision=lax.Precision.HIGHEST)
        full = jnp.where(band[None], bias * LOG2_E, NEG_INF)
        first = jnp.concatenate([full[:, :, ATT_BLOCK:], jnp.full_like(full[:, :, ATT_BLOCK:], NEG_INF)], axis=-1)
        tabs.append(jnp.stack([full, first]))
    return jnp.stack(tabs)


def _att_kernel(q_ref, k_ref, v_ref, tab_ref, o_ref, m_ref, l_ref, *, seq):
    lane = lax.broadcasted_iota(jnp.int32, (ATT_BLOCK, LANES), 1)
    head0 = lane < HEAD_DIM

    def rows(ref, start, n, dil):
        if dil == 1:
            return ref[0, pl.ds(pl.multiple_of(start, ATT_BLOCK), n), :]
        return ref[0, pl.ds(start, n, stride=dil), :]

    def lane_mask(n_rows, h):
        m = lax.broadcasted_iota(jnp.int32, (n_rows, LANES), 1) < HEAD_DIM
        return m if h == 0 else ~m

    def batch(pi, dil, nb, res, blk0, head, init):
        q_rows = nb * ATT_BLOCK
        kv_blocks = nb if head else nb + 1
        q_start = res + dil * ATT_BLOCK * blk0
        kv_start = q_start if head else q_start - dil * ATT_BLOCK
        if dil == 1:
            sl = pl.ds(pl.multiple_of(q_start, ATT_BLOCK), q_rows)
        else:
            sl = pl.ds(q_start, q_rows, stride=dil)
        q = rows(q_ref, q_start, q_rows, dil)
        k = rows(k_ref, kv_start, kv_blocks * ATT_BLOCK, dil).astype(BF16)
        v = rows(v_ref, kv_start, kv_blocks * ATT_BLOCK, dil)
        if not init:
            m_old, l_old, o_old = m_ref[sl, :], l_ref[sl, :], o_ref[0, sl, :]
        qh = [jnp.where(lane_mask(q_rows, h), q, 0.0).astype(BF16) for h in range(2)]
        vh = [jnp.where(lane_mask(kv_blocks * ATT_BLOCK, h), v, 1.0).astype(BF16) for h in range(2)]
        o_parts, l_parts, m_parts = [], [], []
        for j in range(nb):
            first = 1 if (head and j == 0) else 0
            kb = j if (not head or j == 0) else j - 1
            ksl = slice(kb * ATT_BLOCK, (kb + 2) * ATT_BLOCK)
            qsl = slice(j * ATT_BLOCK, (j + 1) * ATT_BLOCK)
            outs, ms = [], []
            for h in range(2):
                s = _dot_nt(qh[h][qsl], k[ksl]) + tab_ref[pi, first, h]
                m = jnp.max(s, axis=1, keepdims=True)
                p = jnp.exp2(s - m).astype(BF16)
                outs.append(_dot(p, vh[h][ksl]))
                ms.append(m)
            o_parts.append(jnp.where(head0, outs[0], outs[1]))
            l_parts.append(pltpu.roll(jnp.where(head0, outs[1], outs[0]), HEAD_DIM, axis=1))
            m_parts.append(jnp.where(head0, ms[0], ms[1]))
        o = jnp.concatenate(o_parts, axis=0)
        l = jnp.concatenate(l_parts, axis=0)
        m = jnp.concatenate(m_parts, axis=0)
        if not init:
            m_new = jnp.maximum(m_old, m)
            a_old = jnp.exp2(m_old - m_new)
            a_cur = jnp.exp2(m - m_new)
            o = o_old * a_old + o * a_cur
            l = l_old * a_old + l * a_cur
            m = m_new
        o_ref[0, sl, :] = o
        m_ref[sl, :] = m
        l_ref[sl, :] = l

    n_units = seq // ATT_BLOCK
    order = sorted(range(len(PATTERNS)), key=lambda p: -PATTERNS[p][1])
    for pos, pi in enumerate(order):
        dil = PATTERNS[pi][1]
        log_d = dil.bit_length() - 1
        res_blocks = seq // (dil * ATT_BLOCK)
        nb = min(ATT_UNITS_PER_STEP, res_blocks)
        per_res = res_blocks // nb
        init = pos == 0

        res_step = 2 if (nb < ATT_UNITS_PER_STEP and dil % 2 == 0) else 1

        def head_body(r, carry, pi=pi, dil=dil, nb=nb, init=init, res_step=res_step):
            for dr in range(res_step):
                batch(pi, dil, nb, r * res_step + dr, 0, True, init)
            return carry
        lax.fori_loop(0, dil // res_step, head_body, 0)

        if per_res > 1:
            def tail_body(i, carry, pi=pi, dil=dil, nb=nb, log_d=log_d, init=init):
                res = i & (dil - 1)
                blk0 = nb * (1 + (i >> log_d))
                batch(pi, dil, nb, res, blk0, False, init)
                return carry
            lax.fori_loop(0, dil * (per_res - 1), tail_body, 0)

    def finish(i, carry):
        sl = pl.ds(pl.multiple_of(i * ATT_BLOCK, ATT_BLOCK), ATT_BLOCK)
        o_ref[0, sl, :] = o_ref[0, sl, :] / l_ref[sl, :]
        return carry
    lax.fori_loop(0, n_units, finish, 0)


def _attention(q, k, v, tables):
    bsz, seq, att_w = q.shape
    n_pairs = att_w // LANES
    blk = lambda b, p: (b, 0, p)
    qkv_spec = pl.BlockSpec((1, seq, LANES), blk)
    n_pat = tables.shape[0]
    tab_spec = pl.BlockSpec((n_pat, 2, 2, ATT_BLOCK, 2 * ATT_BLOCK), lambda b, p: (0, 0, p, 0, 0))
    return pl.pallas_call(
        functools.partial(_att_kernel, seq=seq),
        grid=(bsz, n_pairs),
        in_specs=[qkv_spec, qkv_spec, qkv_spec, tab_spec],
        out_specs=pl.BlockSpec((1, seq, LANES), blk),
        out_shape=jax.ShapeDtypeStruct((bsz, seq, att_w), F32),
        scratch_shapes=[pltpu.VMEM((seq, LANES), F32), pltpu.VMEM((seq, LANES), F32)],
        compiler_params=_cparams(("parallel", "parallel")),
        name="dilated_attention",
    )(q, k, v, tables)


def _ssm_params(a_re, a_im, b_re, b_im, c_re, c_im, d_skip, log_dt):
    n_g, n_p = a_re.shape
    n_c = b_re.shape[-1]
    n_state = n_g * n_p
    dt = jnp.exp(log_dt.astype(F32))[:, None]

    def a_pow(kk):
        mag = jnp.exp(kk * dt * a_re)
        ph = kk * dt * a_im
        return mag * jnp.cos(ph), mag * jnp.sin(ph)

    ab_re, ab_im = a_pow(1.0)
    nr, ni = ab_re - 1.0, ab_im
    den = a_re * a_re + a_im * a_im
    f_re = (nr * a_re + ni * a_im) / den
    f_im = (ni * a_re - nr * a_im) / den
    bb_re = f_re[:, :, None] * b_re - f_im[:, :, None] * b_im
    bb_im = f_re[:, :, None] * b_im + f_im[:, :, None] * b_re
    eye = jnp.eye(n_g, dtype=F32)
    w_re = jnp.einsum('gpc,gh->gchp', bb_re, eye).reshape(n_g * n_c, n_state)
    w_im = jnp.einsum('gpc,gh->gchp', bb_im, eye).reshape(n_g * n_c, n_state)
    o_re = jnp.einsum('gcp,gh->gphc', c_re, eye).reshape(n_state, n_g * n_c)
    o_im = jnp.einsum('gcp,gh->gphc', c_im, eye).reshape(n_state, n_g * n_c)
    n_kb = (n_g * n_c) // LANES
    sp = n_state // n_kb
    w_in = jnp.stack([jnp.concatenate([w_re[kb * LANES:(kb + 1) * LANES, kb * sp:(kb + 1) * sp],
                                       w_im[kb * LANES:(kb + 1) * LANES, kb * sp:(kb + 1) * sp]], axis=1)
                      for kb in range(n_kb)])
    w_out = jnp.stack([jnp.concatenate([o_re[kb * sp:(kb + 1) * sp, kb * LANES:(kb + 1) * LANES],
                                        -o_im[kb * sp:(kb + 1) * sp, kb * LANES:(kb + 1) * LANES]], axis=0)
                       for kb in range(n_kb)])
    j = jnp.arange(SUBLANES, dtype=F32)[:, None]
    flat = lambda t: jnp.broadcast_to(t.reshape(1, n_state), (SUBLANES, n_state))
    coef = []
    for sh in (1, 2, 4):
        pr, pim = a_pow(float(sh))
        keep = (j >= sh).astype(F32)
        coef += [flat(pr) * keep, flat(pim) * keep]
    a_re_f = jnp.broadcast_to(a_re.reshape(1, n_state), (SUBLANES, n_state))
    a_im_f = jnp.broadcast_to(a_im.reshape(1, n_state), (SUBLANES, n_state))
    dt_f = jnp.broadcast_to(jnp.repeat(dt[:, 0], n_p).reshape(1, n_state), (SUBLANES, n_state))
    mag = jnp.exp((j + 1.0) * dt_f * a_re_f)
    ph = (j + 1.0) * dt_f * a_im_f
    coef += [mag * jnp.cos(ph), mag * jnp.sin(ph)]
    coef = jnp.stack(coef)
    return w_in.astype(BF16), w_out.astype(BF16), coef, d_skip.reshape(1, n_g * n_c).astype(F32)


def _gelu_tanh(x):
    return 0.5 * x * (1.0 + jnp.tanh(math.sqrt(2.0 / math.pi) * (x + 0.044715 * (x * x * x))))


def _ssm_kernel(u_ref, win_ref, wout_ref, coef_ref, d_ref, wglu_ref, bglu_ref, o_ref,
                hre_ref, him_ref, cre_ref, cim_ref, *, rows, n_state, col_w):
    n_kb = win_ref.shape[0]
    sp = n_state // n_kb

    @pl.when(pl.program_id(1) == 0)
    def _():
        cre_ref[...] = jnp.zeros_like(cre_ref)
        cim_ref[...] = jnp.zeros_like(cim_ref)

    u = u_ref[0]
    ub = u.astype(BF16)
    for kb in range(n_kb):
        bu = _dot(ub[:, kb * LANES:(kb + 1) * LANES], win_ref[kb])
        hre_ref[:, kb * sp:(kb + 1) * sp] = bu[:, :sp]
        him_ref[:, kb * sp:(kb + 1) * sp] = bu[:, sp:]

    n_slab = rows // SUBLANES
    for c0 in range(0, n_state, col_w):
        cs = slice(c0, c0 + col_w)

        def body(i, carry, cs=cs):
            c_re, c_im = carry
            sl = pl.ds(pl.multiple_of(i * SUBLANES, SUBLANES), SUBLANES)
            x_re = hre_ref[sl, cs]
            x_im = him_ref[sl, cs]
            for si, sh in enumerate((1, 2, 4)):
                p_re = coef_ref[2 * si, :, cs]
                p_im = coef_ref[2 * si + 1, :, cs]
                s_re = pltpu.roll(x_re, sh, axis=0)
                s_im = pltpu.roll(x_im, sh, axis=0)
                x_re, x_im = (x_re + p_re * s_re - p_im * s_im,
                              x_im + p_re * s_im + p_im * s_re)
            p_re = coef_ref[6, :, cs]
            p_im = coef_ref[7, :, cs]
            h_re = x_re + p_re * c_re - p_im * c_im
            h_im = x_im + p_re * c_im + p_im * c_re
            hre_ref[sl, cs] = h_re
            him_ref[sl, cs] = h_im
            last = slice(SUBLANES - 1, SUBLANES)
            return (jnp.broadcast_to(h_re[last, :], h_re.shape),
                    jnp.broadcast_to(h_im[last, :], h_im.shape))

        c_re, c_im = lax.fori_loop(0, n_slab, body, (cre_ref[:, cs], cim_ref[:, cs]), unroll=True)
        cre_ref[:, cs] = c_re
        cim_ref[:, cs] = c_im

    ys = []
    for kb in range(n_kb):
        hcat = jnp.concatenate([hre_ref[:, kb * sp:(kb + 1) * sp].astype(BF16),
                                him_ref[:, kb * sp:(kb + 1) * sp].astype(BF16)], axis=1)
        ys.append(_dot(hcat, wout_ref[kb]))
    y = jnp.concatenate(ys, axis=1) + d_ref[...] * u
    y = _gelu_tanh(y)
    z = _dot(y.astype(BF16), wglu_ref[...]) + bglu_ref[...]
    o_ref[0] = y * jax.nn.sigmoid(z)


def _ssm(u, w_in, w_out, coef, d_flat, w_glu, b_glu):
    bsz, seq, ssm_w = u.shape
    n_state = coef.shape[-1]
    rows = min(SSM_ROWS, seq)
    full = lambda *shape: pl.BlockSpec(shape, lambda b, i: (0,) * len(shape))
    kern = functools.partial(_ssm_kernel, rows=rows, n_state=n_state, col_w=4 * LANES)
    return pl.pallas_call(
        kern,
        grid=(bsz, seq // rows),
        in_specs=[pl.BlockSpec((1, rows, ssm_w), lambda b, i: (b, i, 0)),
                  full(*w_in.shape), full(*w_out.shape), full(*coef.shape), full(1, ssm_w),
                  full(ssm_w, ssm_w), full(1, ssm_w)],
        out_specs=pl.BlockSpec((1, rows, ssm_w), lambda b, i: (b, i, 0)),
        out_shape=jax.ShapeDtypeStruct((bsz, seq, ssm_w), F32),
        scratch_shapes=[pltpu.VMEM((rows, n_state), F32), pltpu.VMEM((rows, n_state), F32),
                        pltpu.VMEM((SUBLANES, n_state), F32), pltpu.VMEM((SUBLANES, n_state), F32)],
        compiler_params=_cparams(("parallel", "arbitrary")),
        name="s5_glu",
    )(u, w_in, w_out, coef, d_flat, w_glu.astype(BF16), b_glu.reshape(1, ssm_w))


def _layer_norm(y, g, b):
    mu = jnp.mean(y, axis=-1, keepdims=True)
    yc = y - mu
    var = jnp.mean(yc * yc, axis=-1, keepdims=True)
    return yc * lax.rsqrt(var + EPS) * g + b


def _rms_norm(y, g):
    return y * lax.rsqrt(jnp.mean(y * y, axis=-1, keepdims=True) + EPS) * g


def _mix_kernel(x_ref, att_ref, ssm_ref, gatt_ref, gssm_ref, wout_ref, g1_ref, ln_g_ref, ln_b_ref,
                sc_ref, sh_ref, g2_ref, wsg_ref, wsu_ref, wsd_ref, h_ref, ht_ref, pre_ref, *, alpha, att_w):
    a_n = _rms_norm(att_ref[0], gatt_ref[...]).astype(BF16)
    s_n = _rms_norm(ssm_ref[0], gssm_ref[...]).astype(BF16)
    mix = _dot(a_n, wout_ref[0:att_w, :]) + _dot(s_n, wout_ref[att_w:, :])
    x1 = _layer_norm(alpha * x_ref[0] + g1_ref[0] * mix, ln_g_ref[...], ln_b_ref[...])
    h = x1 * (1.0 + sc_ref[0]) + sh_ref[0]
    h_ref[0] = h
    _store_token_tiles(ht_ref, h)
    hb = h.astype(BF16)
    hid = _silu(_dot(hb, wsg_ref[...])) * _dot(hb, wsu_ref[...])
    shared = _dot(hid.astype(BF16), wsd_ref[...])
    pre_ref[0] = alpha * x1 + g2_ref[0] * shared


def _mix(x, att, ssm, g_att, g_ssm, w_out, gate1, ln_g, ln_b, scale2, shift2, gate2,
         w_s_gate, w_s_up, w_s_down, alpha):
    bsz, seq, d = x.shape
    att_w = att.shape[-1]
    ssm_w = ssm.shape[-1]
    ff = w_s_gate.shape[1]
    tm = min(MIX_ROWS, seq)
    n_sub = d // LANES
    row = lambda b, i: (b, i, 0)
    per_b = lambda b, i: (b, 0, 0)
    full = lambda *shape: pl.BlockSpec(shape, lambda b, i: (0,) * len(shape))
    vec = lambda t: t.reshape(bsz, 1, d)
    return pl.pallas_call(
        functools.partial(_mix_kernel, alpha=alpha, att_w=att_w),
        grid=(bsz, seq // tm),
        in_specs=[pl.BlockSpec((1, tm, d), row), pl.BlockSpec((1, tm, att_w), row),
                  pl.BlockSpec((1, tm, ssm_w), row),
                  full(1, att_w), full(1, ssm_w), full(att_w + ssm_w, d),
                  pl.BlockSpec((1, 1, d), per_b), full(1, d), full(1, d),
                  pl.BlockSpec((1, 1, d), per_b), pl.BlockSpec((1, 1, d), per_b),
                  pl.BlockSpec((1, 1, d), per_b),
                  full(d, ff), full(d, ff), full(ff, d)],
        out_specs=[pl.BlockSpec((1, tm, d), row),
                   pl.BlockSpec((tm * n_sub, LANES), lambda b, i: (b * (seq // tm) + i, 0)),
                   pl.BlockSpec((1, tm, d), row)],
        out_shape=[jax.ShapeDtypeStruct((bsz, seq, d), F32),
                   jax.ShapeDtypeStruct((bsz * seq * n_sub, LANES), F32),
                   jax.ShapeDtypeStruct((bsz, seq, d), F32)],
        compiler_params=_cparams(("parallel", "parallel")),
        name="mix_ln1_shared",
    )(x, att, ssm, g_att.reshape(1, att_w), g_ssm.reshape(1, ssm_w), w_out.astype(BF16),
      vec(gate1), ln_g.reshape(1, d), ln_b.reshape(1, d), vec(scale2), vec(shift2), vec(gate2),
      w_s_gate.astype(BF16), w_s_up.astype(BF16), w_s_down.astype(BF16))


def _route_kernel(h_ref, wrt_ref, wrt_lo_ref, bias_ref, tri_ref, idx_ref, gate_ref, rank_ref, cnt_ref, carry_ref):
    n_e = wrt_ref.shape[0]
    tm = h_ref.shape[0]
    per_group = n_e // N_EXPERT_GROUPS
    neg = -jnp.inf

    @pl.when(pl.program_id(0) == 0)
    def _():
        carry_ref[...] = jnp.zeros_like(carry_ref)

    h_hi, h_lo = _split_bf16(h_ref[...])
    logits = (_dot_nt(wrt_ref[...], h_hi) + _dot_nt(wrt_ref[...], h_lo)
              + _dot_nt(wrt_lo_ref[...], h_hi))
    scores = jax.nn.sigmoid(logits)
    biased = scores + bias_ref[...]

    g3 = biased.reshape(N_EXPERT_GROUPS, per_group, tm)
    ridx = lax.broadcasted_iota(jnp.int32, g3.shape, 1).astype(F32)
    m1 = jnp.max(g3, axis=1, keepdims=True)
    first = jnp.min(jnp.where(g3 == m1, ridx, float(per_group)), axis=1, keepdims=True)
    m2 = jnp.max(jnp.where(ridx == first, neg, g3), axis=1, keepdims=True)
    gs = m1 + m2

    gidx = lax.broadcasted_iota(jnp.int32, gs.shape, 0).astype(F32)
    ok = jnp.zeros_like(gs)
    cur = gs
    for _ in range(TOPK_GROUPS):
        mx = jnp.max(cur, axis=0, keepdims=True)
        fi = jnp.min(jnp.where(cur == mx, gidx, float(N_EXPERT_GROUPS)), axis=0, keepdims=True)
        hit = gidx == fi
        ok = jnp.where(hit, 1.0, ok)
        cur = jnp.where(hit, neg, cur)
    masked = jnp.where(ok > 0.5, g3, neg).reshape(n_e, tm)

    eidx = lax.broadcasted_iota(jnp.int32, (n_e, tm), 0).astype(F32)
    onehot = jnp.zeros((n_e, tm), F32)
    cur = masked
    sel_idx = []
    sel_gate = []
    for _ in range(TOP_K):
        mx = jnp.max(cur, axis=0, keepdims=True)
        fi = jnp.min(jnp.where(cur == mx, eidx, float(n_e)), axis=0, keepdims=True)
        hit = eidx == fi
        sel_idx.append(fi)
        sel_gate.append(jnp.sum(jnp.where(hit, scores, 0.0), axis=0, keepdims=True))
        onehot = jnp.where(hit, 1.0, onehot)
        cur = jnp.where(hit, neg, cur)
    idx = jnp.concatenate(sel_idx, axis=0)
    gate = jnp.concatenate(sel_gate, axis=0)
    gate = gate / jnp.sum(gate, axis=0, keepdims=True) * ROUTED_SCALE

    prior = _dot(onehot.astype(BF16), tri_ref[...]) + carry_ref[:, 0:1]
    ranks = [jnp.sum(jnp.where(eidx == sel_idx[k], prior, 0.0), axis=0, keepdims=True)
             for k in range(TOP_K)]
    rank = jnp.concatenate(ranks, axis=0)
    carry = carry_ref[...] + jnp.sum(onehot, axis=1, keepdims=True)
    carry_ref[...] = carry
    cnt_ref[...] = carry

    idx_ref[...] = idx.astype(jnp.int32)
    gate_ref[...] = gate
    rank_ref[...] = rank.astype(jnp.int32)


def _route(h2d, w_router, router_bias):
    n_tok, d = h2d.shape
    n_e = w_router.shape[1]
    tm = min(ROUTE_COLS, n_tok)
    tri = (np.arange(tm)[:, None] < np.arange(tm)[None, :]).astype(np.float32)
    w_hi, w_lo = _split_bf16(w_router.T.astype(F32))
    col = lambda i: (0, i)
    full = lambda *shape: pl.BlockSpec(shape, lambda i: (0,) * len(shape))
    idx, gate, rank, cnt = pl.pallas_call(
        _route_kernel,
        grid=(n_tok // tm,),
        in_specs=[pl.BlockSpec((tm, d), lambda i: (i, 0)), full(n_e, d), full(n_e, d), full(n_e, 1),
                  full(tm, tm)],
        out_specs=[pl.BlockSpec((TOP_K, tm), col), pl.BlockSpec((TOP_K, tm), col),
                   pl.BlockSpec((TOP_K, tm), col), full(n_e, LANES)],
        out_shape=[jax.ShapeDtypeStruct((TOP_K, n_tok), jnp.int32),
                   jax.ShapeDtypeStruct((TOP_K, n_tok), F32),
                   jax.ShapeDtypeStruct((TOP_K, n_tok), jnp.int32),
                   jax.ShapeDtypeStruct((n_e, LANES), F32)],
        scratch_shapes=[pltpu.VMEM((n_e, LANES), F32)],
        compiler_params=_cparams(("arbitrary",)),
        name="router_topk",
    )(h2d, w_hi, w_lo, router_bias.reshape(n_e, 1).astype(F32), jnp.asarray(tri, BF16))
    return idx, gate, rank, cnt[:, 0].astype(jnp.int32)


def _sc_mesh():
    return plsc.VectorSubcoreMesh(core_axis_name="core", subcore_axis_name="subcore")


def _dispatch(h_tiles, gate_rows, dest_windows, n_rows):
    n_tok, n_sub, _ = h_tiles.shape
    win = DISPATCH_WINDOW
    index_spec = pl.BlockSpec((1, TOP_K * win), lambda i: (i, 0))

    def scatter(body, src, src_spec, out_type):
        @pl.kernel(out_type=out_type, mesh=_sc_mesh(), scratch_types=[])
        def run(s_hbm, d_hbm, o_hbm):
            pltpu.emit_pipeline(
                functools.partial(body, o_hbm),
                grid=(n_tok // win,),
                in_specs=[src_spec, index_spec],
                out_specs=[],
                core_axis_name=("core", "subcore"),
                dimension_semantics=(pltpu.PARALLEL,),
            )(s_hbm, d_hbm)
        return run(src, dest_windows)

    def tile_window(o_hbm, x_vmem, d_vmem):
        for k in range(TOP_K):
            pltpu.sync_copy(x_vmem, o_hbm.at[d_vmem.at[0, pl.ds(k * win, win)]])

    def gate_window(o_hbm, g_vmem, d_vmem):
        for k in range(TOP_K):
            pltpu.sync_copy(g_vmem.at[k], o_hbm.at[d_vmem.at[0, pl.ds(k * win, win)]])

    xs = scatter(tile_window, h_tiles, pl.BlockSpec((win, n_sub, LANES), lambda i: (i, 0, 0)),
                 jax.ShapeDtypeStruct((n_rows, n_sub, LANES), h_tiles.dtype))
    row_gates = scatter(gate_window, gate_rows, pl.BlockSpec((TOP_K, win, GATE_LANES), lambda i: (0, i, 0)),
                        jax.ShapeDtypeStruct((n_rows, GATE_LANES), F32))
    return xs, row_gates


def _gather_sum(ys_tiles, dest_windows, n_tok):
    _, n_sub, _ = ys_tiles.shape
    win = DISPATCH_WINDOW

    @pl.kernel(out_type=jax.ShapeDtypeStruct((n_tok, n_sub, LANES), ys_tiles.dtype), mesh=_sc_mesh(),
               scratch_types=[])
    def gather_tiles(y_hbm, d_hbm, o_hbm):
        def window(d_vmem, o_vmem):
            pltpu.sync_copy(y_hbm.at[d_vmem.at[0, pl.ds(0, win)]], o_vmem)
            for k in range(1, TOP_K):
                pltpu.sync_copy(y_hbm.at[d_vmem.at[0, pl.ds(k * win, win)]], o_vmem, add=True)
        pltpu.emit_pipeline(
            window,
            grid=(n_tok // win,),
            in_specs=[pl.BlockSpec((1, TOP_K * win), lambda i: (i, 0))],
            out_specs=[pl.BlockSpec((win, n_sub, LANES), lambda i: (i, 0, 0))],
            core_axis_name=("core", "subcore"),
            dimension_semantics=(pltpu.PARALLEL,),
        )(d_hbm, o_hbm)

    return gather_tiles(ys_tiles, dest_windows)


def _expert_kernel(first_ref, nblk_ref, cnt_ref, nused_ref, xs_ref, rg_ref, wg_ref, wu_ref, wd_ref, ys_ref,
                   xbuf_ref, gbuf_ref, ybuf_ref, wgb_ref, wub_ref, wdb_ref, xsem_ref, ysem_ref):
    e = pl.program_id(0)
    n_used = nused_ref[0]
    d = wgb_ref.shape[0]
    blk_rows = xbuf_ref.shape[1]

    class _InCopy:
        def __init__(self, g, slot):
            src = xs_ref.at[pl.ds(pl.multiple_of(g * blk_rows, blk_rows), blk_rows)]
            gsrc = rg_ref.at[pl.ds(pl.multiple_of(g * MOE_ROWS, MOE_ROWS), MOE_ROWS)]
            self.copies = (pltpu.make_async_copy(src, xbuf_ref.at[slot], xsem_ref.at[slot]),
                           pltpu.make_async_copy(gsrc, gbuf_ref.at[slot], xsem_ref.at[slot]))

        def start(self):
            for cp in self.copies:
                cp.start()

        def wait(self):
            for cp in self.copies:
                cp.wait()

    x_copy = _InCopy

    def y_copy(g, slot):
        dst = ys_ref.at[pl.ds(pl.multiple_of(g * blk_rows, blk_rows), blk_rows)]
        return pltpu.make_async_copy(ybuf_ref.at[slot], dst, ysem_ref.at[slot])

    n_slot = xbuf_ref.shape[0]

    @pl.when(e == 0)
    def _():
        for g in range(n_slot - 1):
            @pl.when(g < n_used)
            def _(g=g):
                x_copy(g, g).start()

    @pl.when(nblk_ref[e] > 0)
    def _():
        wgb_ref[...] = wg_ref[0].astype(BF16)
        wub_ref[...] = wu_ref[0].astype(BF16)
        wdb_ref[...] = wd_ref[0].astype(BF16)

    def block(j, carry):
        g = first_ref[e] + j
        slot = g & (n_slot - 1)
        x_copy(g, slot).wait()

        @pl.when(g + n_slot - 1 < n_used)
        def _():
            x_copy(g + n_slot - 1, (g + n_slot - 1) & (n_slot - 1)).start()

        @pl.when(g >= n_slot)
        def _():
            y_copy(g - n_slot, slot).wait()

        x = jnp.concatenate(_load_token_tiles(xbuf_ref.at[slot], MOE_ROWS, d), axis=1)
        row = lax.broadcasted_iota(jnp.int32, (MOE_ROWS, 1), 0)
        xb = jnp.where(row < cnt_ref[e] - j * MOE_ROWS, x, 0.0).astype(BF16)
        hid = _silu(_dot(xb, wgb_ref[...])) * _dot(xb, wub_ref[...])
        y = _dot(hid.astype(BF16), wdb_ref[...]) * gbuf_ref[slot][:, 0:1]
        _store_token_tiles(ybuf_ref.at[slot], jnp.where(row < cnt_ref[e] - j * MOE_ROWS, y, 0.0))
        y_copy(g, slot).start()
        return carry
    lax.fori_loop(0, nblk_ref[e], block, 0)

    @pl.when(e == pl.num_programs(0) - 1)
    def _():
        for back in range(1, n_slot + 1):
            @pl.when(n_used >= back)
            def _(back=back):
                y_copy(n_used - back, (n_used - back) & (n_slot - 1)).wait()


def _experts(xs_tiles, row_gates, plan, w_gate, w_up, w_down):
    n_rows, n_sub, _ = xs_tiles.shape
    d = n_sub * LANES
    n_e, _, ff = w_gate.shape
    blk_rows = MOE_ROWS * n_sub
    hbm = pl.BlockSpec(memory_space=pl.ANY)
    grid_spec = pltpu.PrefetchScalarGridSpec(
        num_scalar_prefetch=4,
        grid=(n_e,),
        in_specs=[hbm, hbm,
                  pl.BlockSpec((1, d, ff), lambda e, *_: (e, 0, 0)),
                  pl.BlockSpec((1, d, ff), lambda e, *_: (e, 0, 0)),
                  pl.BlockSpec((1, ff, d), lambda e, *_: (e, 0, 0))],
        out_specs=hbm,
        scratch_shapes=[pltpu.VMEM((EXPERT_SLOTS, blk_rows, LANES), F32),
                        pltpu.VMEM((EXPERT_SLOTS, MOE_ROWS, GATE_LANES), F32),
                        pltpu.VMEM((EXPERT_SLOTS, blk_rows, LANES), F32),
                        pltpu.VMEM((d, ff), BF16), pltpu.VMEM((d, ff), BF16), pltpu.VMEM((ff, d), BF16),
                        pltpu.SemaphoreType.DMA((EXPERT_SLOTS,)), pltpu.SemaphoreType.DMA((EXPERT_SLOTS,))],
    )
    ys = pl.pallas_call(
        _expert_kernel,
        grid_spec=grid_spec,
        out_shape=jax.ShapeDtypeStruct((n_rows * n_sub, LANES), F32),
        compiler_params=_cparams(("arbitrary",)),
        name="moe_experts",
    )(*plan, xs_tiles.reshape(n_rows * n_sub, LANES), row_gates, w_gate, w_up, w_down)
    return ys.reshape(n_rows, n_sub, LANES)


def _combine_kernel(dest_ref, ys_ref, pre_ref, g2_ref, ln_g_ref, ln_b_ref, o_ref, buf_ref, sem_ref):
    tm, d = pre_ref.shape[1], pre_ref.shape[2]
    n_sub = d // LANES

    def row_copy(t, k, src_row):
        slot = pl.multiple_of((k * tm + t) * n_sub, n_sub)
        return pltpu.make_async_copy(ys_ref.at[src_row], buf_ref.at[pl.ds(slot, n_sub)], sem_ref.at[0])

    def issue(g, c):
        t0 = g * DMA_ISSUE_UNROLL
        rows = [[dest_ref[(t0 + j) * TOP_K + k] for k in range(TOP_K)] for j in range(DMA_ISSUE_UNROLL)]
        for j in range(DMA_ISSUE_UNROLL):
            for k in range(TOP_K):
                row_copy(t0 + j, k, rows[j][k]).start(priority=k % 2)
        return c
    lax.fori_loop(0, tm // DMA_ISSUE_UNROLL, issue, 0)

    def drain(t, c):
        for k in range(TOP_K):
            row_copy(t, k, dest_ref[t * TOP_K + k]).wait()
        return c
    lax.fori_loop(0, tm, drain, 0)

    pieces = []
    for s in range(n_sub):
        acc = None
        for k in range(TOP_K):
            term = buf_ref[pl.ds(k * tm * n_sub + s, tm, stride=n_sub), :]
            acc = term if acc is None else acc + term
        pieces.append(acc)
    routed = jnp.concatenate(pieces, axis=1)
    y = pre_ref[0] + g2_ref[0] * routed
    o_ref[0] = _layer_norm(y, ln_g_ref[...], ln_b_ref[...])


def _combine(ys, dest, pre, gate2, ln_g, ln_b, n_tiles):
    bsz, seq, d = pre.shape
    tm = min(DISPATCH_ROWS, seq)
    per_seq = seq // tm
    row = lambda i, *_: (i // per_seq, i % per_seq, 0)
    per_b = lambda i, *_: (i // per_seq, 0, 0)
    full = lambda *shape: pl.BlockSpec(shape, lambda i, *_: (0,) * len(shape))
    grid_spec = pltpu.PrefetchScalarGridSpec(
        num_scalar_prefetch=0,
        grid=(n_tiles,),
        in_specs=[pl.BlockSpec((TOP_K * tm,), lambda i: (i,), memory_space=pltpu.SMEM),
                  pl.BlockSpec(memory_space=pl.ANY),
                  pl.BlockSpec((1, tm, d), row), pl.BlockSpec((1, 1, d), per_b), full(1, d), full(1, d)],
        out_specs=pl.BlockSpec((1, tm, d), row),
        scratch_shapes=[pltpu.VMEM((TOP_K * tm * (d // LANES), LANES), F32), pltpu.SemaphoreType.DMA((1,))],
    )
    return pl.pallas_call(
        _combine_kernel,
        grid_spec=grid_spec,
        out_shape=jax.ShapeDtypeStruct((bsz, seq, d), F32),
        compiler_params=_cparams(("arbitrary",)),
        name="moe_combine_ln2",
    )(dest, ys, pre, gate2.reshape(bsz, 1, d), ln_g.reshape(1, d), ln_b.reshape(1, d))


def _final_ln_kernel(r_ref, pre_ref, g2_ref, ln_g_ref, ln_b_ref, *rest):
    o_ref = rest[-1]
    tm, d = pre_ref.shape[1], pre_ref.shape[2]
    routed = jnp.concatenate(_load_token_tiles(r_ref, tm, d), axis=1)
    y = pre_ref[0] + g2_ref[0] * routed
    o_ref[0] = _layer_norm(y, ln_g_ref[...], ln_b_ref[...])


def _final_ln(routed_tiles, pre, gate2, ln_g, ln_b, first_tile, partial=None):
    bsz, seq, d = pre.shape
    tm = min(DISPATCH_ROWS, seq)
    per_seq = seq // tm
    n_sub = d // LANES
    n_tiles = bsz * per_seq - first_tile
    row = lambda i: ((i + first_tile) // per_seq, (i + first_tile) % per_seq, 0)
    per_b = lambda i: ((i + first_tile) // per_seq, 0, 0)
    full = lambda *shape: pl.BlockSpec(shape, lambda i: (0,) * len(shape))
    in_specs = [pl.BlockSpec((tm * n_sub, LANES), lambda i: (i, 0)),
                pl.BlockSpec((1, tm, d), row), pl.BlockSpec((1, 1, d), per_b), full(1, d), full(1, d)]
    args = [routed_tiles.reshape(-1, LANES), pre, gate2.reshape(bsz, 1, d), ln_g.reshape(1, d),
            ln_b.reshape(1, d)]
    aliases = {}
    if partial is not None:
        in_specs.append(pl.BlockSpec(memory_space=pl.ANY))
        args.append(partial)
        aliases = {len(args) - 1: 0}
    return pl.pallas_call(
        _final_ln_kernel,
        grid=(n_tiles,),
        in_specs=in_specs,
        out_specs=pl.BlockSpec((1, tm, d), row),
        out_shape=jax.ShapeDtypeStruct((bsz, seq, d), F32),
        input_output_aliases=aliases,
        compiler_params=_cparams(("arbitrary",)),
        name="moe_final_ln2",
    )(*args)


def _dest_kernel(pstart_ref, idx_ref, rank_ref, dest_ref):
    idx = idx_ref[...]

    def body(e, acc):
        return acc + jnp.where(idx == e, pstart_ref[e], 0)
    dest_ref[...] = lax.fori_loop(0, pstart_ref.shape[0], body, rank_ref[...], unroll=8)


def _dest_rows(pstart, idx, rank):
    n_k, n_tok = idx.shape
    tn = min(4096, n_tok)
    grid_spec = pltpu.PrefetchScalarGridSpec(
        num_scalar_prefetch=1,
        grid=(n_tok // tn,),
        in_specs=[pl.BlockSpec((n_k, tn), lambda i, *_: (0, i))] * 2,
        out_specs=pl.BlockSpec((n_k, tn), lambda i, *_: (0, i)),
    )
    return pl.pallas_call(
        _dest_kernel,
        grid_spec=grid_spec,
        out_shape=jax.ShapeDtypeStruct((n_k, n_tok), jnp.int32),
        compiler_params=_cparams(("arbitrary",)),
        name="moe_dest",
    )(pstart, idx, rank)


def _moe_plan(idx, rank, cnt, n_tok):
    n_e = cnt.shape[0]
    padded = (cnt + MOE_ROWS - 1) // MOE_ROWS * MOE_ROWS
    pend = jnp.cumsum(padded).astype(jnp.int32)
    pstart = pend - padded
    dest_kt = _dest_rows(pstart, idx, rank)
    dest = dest_kt.T.reshape(-1)
    win = min(DISPATCH_WINDOW, n_tok)
    dest_windows = dest_kt.reshape(TOP_K, n_tok // win, win).transpose(1, 0, 2).reshape(n_tok // win, TOP_K * win)
    n_blk = n_tok * TOP_K // MOE_ROWS + n_e
    n_used = (pend[-1:] // MOE_ROWS).astype(jnp.int32)
    plan = ((pstart // MOE_ROWS).astype(jnp.int32), (padded // MOE_ROWS).astype(jnp.int32),
            cnt.astype(jnp.int32), n_used)
    return dest, dest_windows, plan, n_blk * MOE_ROWS


def kernel(x, c, rel_bias, w_ada, b_ada, w_in, ssm_a_re, ssm_a_im, ssm_b_re, ssm_b_im, ssm_c_re, ssm_c_im, ssm_d, ssm_log_dt, w_glu, b_glu, g_att, g_ssm, w_out, ln1_g, ln1_b, w_router, router_bias, w_e_gate, w_e_up, w_e_down, w_s_gate, w_s_up, w_s_down, ln2_g, ln2_b):
    bsz, seq, d = x.shape
    depth = w_ada.shape[0]
    alpha = (2 * depth) ** 0.25
    att_w = g_att.shape[-1]
    tables = _att_tables(rel_bias)
    for layer in range(depth):
        ada = _ada(c, w_ada[layer], b_ada[layer])
        shift1, scale1, gate1, shift2, scale2, gate2 = jnp.split(ada, 6, axis=-1)
        q, k, v, u = _inproj(x, scale1, shift1, w_in[layer], att_w)
        att = _attention(q, k, v, tables)
        ssm_prm = _ssm_params(ssm_a_re[layer], ssm_a_im[layer], ssm_b_re[layer], ssm_b_im[layer],
                              ssm_c_re[layer], ssm_c_im[layer], ssm_d[layer], ssm_log_dt[layer])
        ssm = _ssm(u, *ssm_prm, w_glu[layer], b_glu[layer])
        h2, h2_tiles, pre = _mix(x, att, ssm, g_att[layer], g_ssm[layer], w_out[layer], gate1,
                                 ln1_g[layer], ln1_b[layer], scale2, shift2, gate2, w_s_gate[layer],
                                 w_s_up[layer], w_s_down[layer], alpha)
        h2d = h2.reshape(bsz * seq, d)
        idx, gate, rank, cnt = _route(h2d, w_router[layer], router_bias[layer])
        dest, dest_windows, plan, n_rows = _moe_plan(idx, rank, cnt, bsz * seq)
        n_tok = bsz * seq
        gate_rows = jnp.broadcast_to(gate[:, :, None], (TOP_K, n_tok, GATE_LANES))
        xs, row_gates = _dispatch(h2_tiles.reshape(n_tok, d // LANES, LANES), gate_rows, dest_windows, n_rows)
        ys = _experts(xs, row_gates, plan, w_e_gate[layer], w_e_up[layer], w_e_down[layer])
        tm = min(DISPATCH_ROWS, seq)
        tc_tiles = min(COMBINE_TC_TILES, n_tok // tm)
        tc_tokens = tc_tiles * tm
        win = min(DISPATCH_WINDOW, n_tok)
        partial = (_combine(ys, dest, pre, gate2, ln2_g[layer], ln2_b[layer], tc_tiles)
                   if tc_tiles > 0 else None)
        routed = _gather_sum(ys, dest_windows[tc_tokens // win:], n_tok - tc_tokens)
        x = _final_ln(routed, pre, gate2, ln2_g[layer], ln2_b[layer], tc_tiles, partial)
    return x
```

```python
import functools
import math

import jax
import jax.numpy as jnp
import numpy as np
from jax import lax
from jax.experimental import pallas as pl
from jax.experimental.pallas import tpu as pltpu
from jax.experimental.pallas import tpu_sc as plsc

F32 = jnp.float32
BF16 = jnp.bfloat16

HEAD_DIM = 64
ATT_BLOCK = 128
PATTERNS = ((128, 1), (512, 4), (2048, 16))
N_BUCKETS = 32
MAX_DISTANCE = 2048
TOP_K = 8
N_EXPERT_GROUPS = 8
TOPK_GROUPS = 4
ROUTED_SCALE = 2.5
EPS = 1e-5
NEG_INF = -1e30
LOG2_E = math.log2(math.e)

LANES = 128
SUBLANES = 8
VMEM_LIMIT_BYTES = 56 * 1024 * 1024

PROJ_ROWS = 512
SSM_ROWS = 512
SSM_TAPS = 4
MIX_ROWS = 512
ROUTE_COLS = 512
MOE_ROWS = 256
DISPATCH_ROWS = 256
COMBINE_TC_TILES = 48
GATE_LANES = LANES
DISPATCH_WINDOW = 32
ATT_UNITS_PER_STEP = 8
DMA_ISSUE_UNROLL = 4
EXPERT_SLOTS = 4


def _cparams(sem, vmem=VMEM_LIMIT_BYTES):
    return pltpu.CompilerParams(dimension_semantics=sem, vmem_limit_bytes=vmem)


def _dot(a, b):
    return jnp.dot(a, b, preferred_element_type=F32)


def _dot_nt(a, b):
    return lax.dot_general(a, b, (((1,), (1,)), ((), ())), preferred_element_type=F32)


def _silu(x):
    return x * jax.nn.sigmoid(x)


def _store_token_tiles(ref, x):
    m, d = x.shape
    n_sub = d // LANES
    for s in range(n_sub):
        ref[pl.ds(s, m, stride=n_sub), :] = x[:, s * LANES:(s + 1) * LANES]


def _load_token_tiles(ref, m, d, row0=0):
    n_sub = d // LANES
    return [ref[pl.ds(row0 + s, m, stride=n_sub), :] for s in range(n_sub)]


def _split_bf16(x):
    hi = x.astype(BF16)
    lo = (x - hi.astype(F32)).astype(BF16)
    return hi, lo


def _ada_kernel(c_ref, w_ref, b_ref, o_ref):
    c = c_ref[...]
    a_hi, a_lo = _split_bf16(_silu(c))
    w_hi, w_lo = _split_bf16(w_ref[...])
    acc = _dot(a_hi, w_hi) + _dot(a_hi, w_lo) + _dot(a_lo, w_hi)
    o_ref[...] = acc + b_ref[...]


def _ada(c, w_ada, b_ada):
    bsz, d = c.shape
    n = w_ada.shape[1]
    rows = SUBLANES
    c_pad = jnp.zeros((rows, d), F32).at[:bsz].set(c)
    tn = 1024
    out = pl.pallas_call(
        _ada_kernel,
        grid=(n // tn,),
        in_specs=[pl.BlockSpec((rows, d), lambda j: (0, 0)),
                  pl.BlockSpec((d, tn), lambda j: (0, j)),
                  pl.BlockSpec((1, tn), lambda j: (0, j))],
        out_specs=pl.BlockSpec((rows, tn), lambda j: (0, j)),
        out_shape=jax.ShapeDtypeStruct((rows, n), F32),
        compiler_params=_cparams(("parallel",)),
        name="ada",
    )(c_pad, w_ada, b_ada.reshape(1, n))
    return out[:bsz]


def _inproj_kernel(x_ref, sc_ref, sh_ref, w_ref, q_ref, k_ref, v_ref, u_ref, *, att_w, q_scale):
    h = (x_ref[0] * (1.0 + sc_ref[0]) + sh_ref[0]).astype(BF16)
    q_ref[0] = _dot(h, w_ref[:, 0:att_w]) * q_scale
    k_ref[0] = _dot(h, w_ref[:, att_w:2 * att_w])
    v_ref[0] = _dot(h, w_ref[:, 2 * att_w:3 * att_w])
    u_ref[0] = _dot(h, w_ref[:, 3 * att_w:])


def _inproj(x, scale, shift, w_in, att_w):
    bsz, seq, d = x.shape
    n = w_in.shape[1]
    ssm_w = n - 3 * att_w
    ts = min(PROJ_ROWS, seq)
    row = lambda b, i: (b, i, 0)
    per_b = lambda b, i: (b, 0, 0)
    kern = functools.partial(_inproj_kernel, att_w=att_w, q_scale=HEAD_DIM ** -0.5 * LOG2_E)
    return pl.pallas_call(
        kern,
        grid=(bsz, seq // ts),
        in_specs=[pl.BlockSpec((1, ts, d), row),
                  pl.BlockSpec((1, 1, d), per_b),
                  pl.BlockSpec((1, 1, d), per_b),
                  pl.BlockSpec((d, n), lambda b, i: (0, 0))],
        out_specs=[pl.BlockSpec((1, ts, att_w), row)] * 3 + [pl.BlockSpec((1, ts, ssm_w), row)],
        out_shape=[jax.ShapeDtypeStruct((bsz, seq, att_w), F32)] * 3
                  + [jax.ShapeDtypeStruct((bsz, seq, ssm_w), F32)],
        compiler_params=_cparams(("parallel", "parallel")),
        name="inproj",
    )(x, scale.reshape(bsz, 1, d), shift.reshape(bsz, 1, d), w_in.astype(BF16))


def _t5_bucket_np(dist):
    exact = N_BUCKETS // 2
    large = exact + (np.log(np.maximum(dist, 1).astype(np.float64) / exact)
                     / math.log(MAX_DISTANCE / exact) * (N_BUCKETS - exact)).astype(np.int64)
    return np.where(dist < exact, dist, np.minimum(large, N_BUCKETS - 1))


def _att_tables(rel_bias):
    qi = np.arange(ATT_BLOCK)[:, None]
    ki = np.arange(2 * ATT_BLOCK)[None, :]
    rel = qi + ATT_BLOCK - ki
    tabs = []
    for window, dil in PATTERNS:
        band = (rel >= 0) & (rel <= window // dil)
        bucket = _t5_bucket_np(np.maximum(rel, 0) * dil)
        onehot = (bucket[:, :, None] == np.arange(N_BUCKETS)[None, None, :]).astype(np.float32)
        bias = jnp.einsum('qkb,bh->hqk', onehot, rel_bias.astype(F32),
                          precision=lax.Precision.HIGHEST)
        full = jnp.where(band[None], bias * LOG2_E, NEG_INF)
        first = jnp.concatenate([full[:, :, ATT_BLOCK:], jnp.full_like(full[:, :, ATT_BLOCK:], NEG_INF)], axis=-1)
        tabs.append(jnp.stack([full, first]))
    return jnp.stack(tabs)


def _att_kernel(q_ref, k_ref, v_ref, tab_ref, o_ref, m_ref, l_ref, *, seq):
    lane = lax.broadcasted_iota(jnp.int32, (ATT_BLOCK, LANES), 1)
    head0 = lane < HEAD_DIM

    def rows(ref, start, n, dil):
        if dil == 1:
            return ref[0, pl.ds(pl.multiple_of(start, ATT_BLOCK), n), :]
        return ref[0, pl.ds(start, n, stride=dil), :]

    def lane_mask(n_rows, h):
        m = lax.broadcasted_iota(jnp.int32, (n_rows, LANES), 1) < HEAD_DIM
        return m if h == 0 else ~m

    def batch(pi, dil, nb, res, blk0, head, init):
        q_rows = nb * ATT_BLOCK
        kv_blocks = nb if head else nb + 1
        q_start = res + dil * ATT_BLOCK * blk0
        kv_start = q_start if head else q_start - dil * ATT_BLOCK
        if dil == 1:
            sl = pl.ds(pl.multiple_of(q_start, ATT_BLOCK), q_rows)
        else:
            sl = pl.ds(q_start, q_rows, stride=dil)
        q = rows(q_ref, q_start, q_rows, dil)
        k = rows(k_ref, kv_start, kv_blocks * ATT_BLOCK, dil).astype(BF16)
        v = rows(v_ref, kv_start, kv_blocks * ATT_BLOCK, dil)
        if not init:
            m_old, l_old, o_old = m_ref[sl, :], l_ref[sl, :], o_ref[0, sl, :]
        qh = [jnp.where(lane_mask(q_rows, h), q, 0.0).astype(BF16) for h in range(2)]
        vh = [jnp.where(lane_mask(kv_blocks * ATT_BLOCK, h), v, 1.0).astype(BF16) for h in range(2)]
        o_parts, l_parts, m_parts = [], [], []
        for j in range(nb):
            first = 1 if (head and j == 0) else 0
            kb = j if (not head or j == 0) else j - 1
            ksl = slice(kb * ATT_BLOCK, (kb + 2) * ATT_BLOCK)
            qsl = slice(j * ATT_BLOCK, (j + 1) * ATT_BLOCK)
            outs, ms = [], []
            for h in range(2):
                s = _dot_nt(qh[h][qsl], k[ksl]) + tab_ref[pi, first, h]
                m = jnp.max(s, axis=1, keepdims=True)
                p = jnp.exp2(s - m).astype(BF16)
                outs.append(_dot(p, vh[h][ksl]))
                ms.append(m)
            o_parts.append(jnp.where(head0, outs[0], outs[1]))
            l_parts.append(pltpu.roll(jnp.where(head0, outs[1], outs[0]), HEAD_DIM, axis=1))
            m_parts.append(jnp.where(head0, ms[0], ms[1]))
        o = jnp.concatenate(o_parts, axis=0)
        l = jnp.concatenate(l_parts, axis=0)
        m = jnp.concatenate(m_parts, axis=0)
        if not init:
            m_new = jnp.maximum(m_old, m)
            a_old = jnp.exp2(m_old - m_new)
            a_cur = jnp.exp2(m - m_new)
            o = o_old * a_old + o * a_cur
            l = l_old * a_old + l * a_cur
            m = m_new
        o_ref[0, sl, :] = o
        m_ref[sl, :] = m
        l_ref[sl, :] = l

    n_units = seq // ATT_BLOCK
    order = sorted(range(len(PATTERNS)), key=lambda p: -PATTERNS[p][1])
    for pos, pi in enumerate(order):
        dil = PATTERNS[pi][1]
        log_d = dil.bit_length() - 1
        res_blocks = seq // (dil * ATT_BLOCK)
        nb = min(ATT_UNITS_PER_STEP, res_blocks)
        per_res = res_blocks // nb
        init = pos == 0

        res_step = 2 if (nb < ATT_UNITS_PER_STEP and dil % 2 == 0) else 1

        def head_body(r, carry, pi=pi, dil=dil, nb=nb, init=init, res_step=res_step):
            for dr in range(res_step):
                batch(pi, dil, nb, r * res_step + dr, 0, True, init)
            return carry
        lax.fori_loop(0, dil // res_step, head_body, 0)

        if per_res > 1:
            def tail_body(i, carry, pi=pi, dil=dil, nb=nb, log_d=log_d, init=init):
                res = i & (dil - 1)
                blk0 = nb * (1 + (i >> log_d))
                batch(pi, dil, nb, res, blk0, False, init)
                return carry
            lax.fori_loop(0, dil * (per_res - 1), tail_body, 0)

    def finish(i, carry):
        sl = pl.ds(pl.multiple_of(i * ATT_BLOCK, ATT_BLOCK), ATT_BLOCK)
        o_ref[0, sl, :] = o_ref[0, sl, :] / l_ref[sl, :]
        return carry
    lax.fori_loop(0, n_units, finish, 0)


def _attention(q, k, v, tables):
    bsz, seq, att_w = q.shape
    n_pairs = att_w // LANES
    blk = lambda b, p: (b, 0, p)
    qkv_spec = pl.BlockSpec((1, seq, LANES), blk)
    n_pat = tables.shape[0]
    tab_spec = pl.BlockSpec((n_pat, 2, 2, ATT_BLOCK, 2 * ATT_BLOCK), lambda b, p: (0, 0, p, 0, 0))
    return pl.pallas_call(
        functools.partial(_att_kernel, seq=seq),
        grid=(bsz, n_pairs),
        in_specs=[qkv_spec, qkv_spec, qkv_spec, tab_spec],
        out_specs=pl.BlockSpec((1, seq, LANES), blk),
        out_shape=jax.ShapeDtypeStruct((bsz, seq, att_w), F32),
        scratch_shapes=[pltpu.VMEM((seq, LANES), F32), pltpu.VMEM((seq, LANES), F32)],
        compiler_params=_cparams(("parallel", "parallel")),
        name="dilated_attention",
    )(q, k, v, tables)


def _ssm_params(a_re, a_im, b_re, b_im, c_re, c_im, d_skip, log_dt):
    n_g, n_p = a_re.shape
    n_c = b_re.shape[-1]
    n_state = n_g * n_p
    dt = jnp.exp(log_dt.astype(F32))[:, None]

    def a_pow(kk):
        mag = jnp.exp(kk * dt * a_re)
        ph = kk * dt * a_im
        return mag * jnp.cos(ph), mag * jnp.sin(ph)

    ab_re, ab_im = a_pow(1.0)
    nr, ni = ab_re - 1.0, ab_im
    den = a_re * a_re + a_im * a_im
    f_re = (nr * a_re + ni * a_im) / den
    f_im = (ni * a_re - nr * a_im) / den
    bb_re = f_re[:, :, None] * b_re - f_im[:, :, None] * b_im
    bb_im = f_re[:, :, None] * b_im + f_im[:, :, None] * b_re
    eye = jnp.eye(n_g, dtype=F32)
    w_re = jnp.einsum('gpc,gh->gchp', bb_re, eye).reshape(n_g * n_c, n_state)
    w_im = jnp.einsum('gpc,gh->gchp', bb_im, eye).reshape(n_g * n_c, n_state)
    o_re = jnp.einsum('gcp,gh->gphc', c_re, eye).reshape(n_state, n_g * n_c)
    o_im = jnp.einsum('gcp,gh->gphc', c_im, eye).reshape(n_state, n_g * n_c)
    n_kb = (n_g * n_c) // LANES
    sp = n_state // n_kb
    def tap(kb, j):
        p_re, p_im = a_pow(float(j))
        p_re = p_re.reshape(1, n_state)[:, kb * sp:(kb + 1) * sp]
        p_im = p_im.reshape(1, n_state)[:, kb * sp:(kb + 1) * sp]
        b_r = w_re[kb * LANES:(kb + 1) * LANES, kb * sp:(kb + 1) * sp]
        b_i = w_im[kb * LANES:(kb + 1) * LANES, kb * sp:(kb + 1) * sp]
        return jnp.concatenate([b_r * p_re - b_i * p_im, b_r * p_im + b_i * p_re], axis=1)
    w_in = jnp.stack([jnp.concatenate([tap(kb, j) for j in range(SSM_TAPS)], axis=0)
                      for kb in range(n_kb)])
    w_out = jnp.stack([jnp.concatenate([o_re[kb * sp:(kb + 1) * sp, kb * LANES:(kb + 1) * LANES],
                                        -o_im[kb * sp:(kb + 1) * sp, kb * LANES:(kb + 1) * LANES]], axis=0)
                       for kb in range(n_kb)])
    j = jnp.arange(SUBLANES, dtype=F32)[:, None]
    flat = lambda t: jnp.broadcast_to(t.reshape(1, n_state), (SUBLANES, n_state))
    coef = []
    for sh in (1, 2, 4):
        pr, pim = a_pow(float(sh))
        keep = (j >= sh).astype(F32)
        coef += [flat(pr) * keep, flat(pim) * keep]
    a_re_f = jnp.broadcast_to(a_re.reshape(1, n_state), (SUBLANES, n_state))
    a_im_f = jnp.broadcast_to(a_im.reshape(1, n_state), (SUBLANES, n_state))
    dt_f = jnp.broadcast_to(jnp.repeat(dt[:, 0], n_p).reshape(1, n_state), (SUBLANES, n_state))
    mag = jnp.exp((j + 1.0) * dt_f * a_re_f)
    ph = (j + 1.0) * dt_f * a_im_f
    coef += [mag * jnp.cos(ph), mag * jnp.sin(ph)]
    coef = jnp.stack(coef)
    return w_in.astype(BF16), w_out.astype(BF16), coef, d_skip.reshape(1, n_g * n_c).astype(F32)


def _gelu_tanh(x):
    return 0.5 * x * (1.0 + jnp.tanh(math.sqrt(2.0 / math.pi) * (x + 0.044715 * (x * x * x))))


def _ssm_kernel(u_ref, win_ref, wout_ref, coef_ref, d_ref, wglu_ref, bglu_ref, o_ref,
                hre_ref, him_ref, cre_ref, cim_ref, *, rows, n_state, col_w):
    n_kb = win_ref.shape[0]
    sp = n_state // n_kb

    @pl.when(pl.program_id(1) == 0)
    def _():
        cre_ref[...] = jnp.zeros_like(cre_ref)
        cim_ref[...] = jnp.zeros_like(cim_ref)

    u = u_ref[0]
    row_in_slab = lax.broadcasted_iota(jnp.int32, u.shape, 0) & (SUBLANES - 1)
    taps = [u.astype(BF16)] + [jnp.where(row_in_slab >= j, pltpu.roll(u, j, axis=0), 0.0).astype(BF16)
                               for j in range(1, SSM_TAPS)]
    for kb in range(n_kb):
        lhs = jnp.concatenate([t[:, kb * LANES:(kb + 1) * LANES] for t in taps], axis=1)
        bu = _dot(lhs, win_ref[kb])
        hre_ref[:, kb * sp:(kb + 1) * sp] = bu[:, :sp]
        him_ref[:, kb * sp:(kb + 1) * sp] = bu[:, sp:]

    n_slab = rows // SUBLANES
    for c0 in range(0, n_state, col_w):
        cs = slice(c0, c0 + col_w)

        def body(i, carry, cs=cs):
            c_re, c_im = carry
            sl = pl.ds(pl.multiple_of(i * SUBLANES, SUBLANES), SUBLANES)
            x_re = hre_ref[sl, cs]
            x_im = him_ref[sl, cs]
            for si, sh in enumerate((1, 2, 4)):
                if sh < SSM_TAPS:
                    continue
                p_re = coef_ref[2 * si, :, cs]
                p_im = coef_ref[2 * si + 1, :, cs]
                s_re = pltpu.roll(x_re, sh, axis=0)
                s_im = pltpu.roll(x_im, sh, axis=0)
                x_re, x_im = (x_re + p_re * s_re - p_im * s_im,
                              x_im + p_re * s_im + p_im * s_re)
            p_re = coef_ref[6, :, cs]
            p_im = coef_ref[7, :, cs]
            h_re = x_re + p_re * c_re - p_im * c_im
            h_im = x_im + p_re * c_im + p_im * c_re
            hre_ref[sl, cs] = h_re
            him_ref[sl, cs] = h_im
            last = slice(SUBLANES - 1, SUBLANES)
            return (jnp.broadcast_to(h_re[last, :], h_re.shape),
                    jnp.broadcast_to(h_im[last, :], h_im.shape))

        c_re, c_im = lax.fori_loop(0, n_slab, body, (cre_ref[:, cs], cim_ref[:, cs]), unroll=True)
        cre_ref[:, cs] = c_re
        cim_ref[:, cs] = c_im

    ys = []
    for kb in range(n_kb):
        hcat = jnp.concatenate([hre_ref[:, kb * sp:(kb + 1) * sp].astype(BF16),
                                him_ref[:, kb * sp:(kb + 1) * sp].astype(BF16)], axis=1)
        ys.append(_dot(hcat, wout_ref[kb]))
    y = jnp.concatenate(ys, axis=1) + d_ref[...] * u
    y = _gelu_tanh(y)
    z = _dot(y.astype(BF16), wglu_ref[...]) + bglu_ref[...]
    o_ref[0] = y * jax.nn.sigmoid(z)


def _ssm(u, w_in, w_out, coef, d_flat, w_glu, b_glu):
    bsz, seq, ssm_w = u.shape
    n_state = coef.shape[-1]
    rows = min(SSM_ROWS, seq)
    full = lambda *shape: pl.BlockSpec(shape, lambda b, i: (0,) * len(shape))
    kern = functools.partial(_ssm_kernel, rows=rows, n_state=n_state, col_w=4 * LANES)
    return pl.pallas_call(
        kern,
        grid=(bsz, seq // rows),
        in_specs=[pl.BlockSpec((1, rows, ssm_w), lambda b, i: (b, i, 0)),
                  full(*w_in.shape), full(*w_out.shape), full(*coef.shape), full(1, ssm_w),
                  full(ssm_w, ssm_w), full(1, ssm_w)],
        out_specs=pl.BlockSpec((1, rows, ssm_w), lambda b, i: (b, i, 0)),
        out_shape=jax.ShapeDtypeStruct((bsz, seq, ssm_w), F32),
        scratch_shapes=[pltpu.VMEM((rows, n_state), F32), pltpu.VMEM((rows, n_state), F32),
                        pltpu.VMEM((SUBLANES, n_state), F32), pltpu.VMEM((SUBLANES, n_state), F32)],
        compiler_params=_cparams(("parallel", "arbitrary")),
        name="s5_glu",
    )(u, w_in, w_out, coef, d_flat, w_glu.astype(BF16), b_glu.reshape(1, ssm_w))


def _layer_norm(y, g, b):
    mu = jnp.mean(y, axis=-1, keepdims=True)
    yc = y - mu
    var = jnp.mean(yc * yc, axis=-1, keepdims=True)
    return yc * lax.rsqrt(var + EPS) * g + b


def _rms_norm(y, g):
    return y * lax.rsqrt(jnp.mean(y * y, axis=-1, keepdims=True) + EPS) * g


def _mix_kernel(x_ref, att_ref, ssm_ref, gatt_ref, gssm_ref, wout_ref, g1_ref, ln_g_ref, ln_b_ref,
                sc_ref, sh_ref, g2_ref, wsg_ref, wsu_ref, wsd_ref, h_ref, ht_ref, pre_ref, *, alpha, att_w):
    a_n = _rms_norm(att_ref[0], gatt_ref[...]).astype(BF16)
    s_n = _rms_norm(ssm_ref[0], gssm_ref[...]).astype(BF16)
    mix = _dot(a_n, wout_ref[0:att_w, :]) + _dot(s_n, wout_ref[att_w:, :])
    x1 = _layer_norm(alpha * x_ref[0] + g1_ref[0] * mix, ln_g_ref[...], ln_b_ref[...])
    h = x1 * (1.0 + sc_ref[0]) + sh_ref[0]
    h_ref[0] = h
    _store_token_tiles(ht_ref, h)
    hb = h.astype(BF16)
    hid = _silu(_dot(hb, wsg_ref[...])) * _dot(hb, wsu_ref[...])
    shared = _dot(hid.astype(BF16), wsd_ref[...])
    pre_ref[0] = alpha * x1 + g2_ref[0] * shared


def _mix(x, att, ssm, g_att, g_ssm, w_out, gate1, ln_g, ln_b, scale2, shift2, gate2,
         w_s_gate, w_s_up, w_s_down, alpha):
    bsz, seq, d = x.shape
    att_w = att.shape[-1]
    ssm_w = ssm.shape[-1]
    ff = w_s_gate.shape[1]
    tm = min(MIX_ROWS, seq)
    n_sub = d // LANES
    row = lambda b, i: (b, i, 0)
    per_b = lambda b, i: (b, 0, 0)
    full = lambda *shape: pl.BlockSpec(shape, lambda b, i: (0,) * len(shape))
    vec = lambda t: t.reshape(bsz, 1, d)
    return pl.pallas_call(
        functools.partial(_mix_kernel, alpha=alpha, att_w=att_w),
        grid=(bsz, seq // tm),
        in_specs=[pl.BlockSpec((1, tm, d), row), pl.BlockSpec((1, tm, att_w), row),
                  pl.BlockSpec((1, tm, ssm_w), row),
                  full(1, att_w), full(1, ssm_w), full(att_w + ssm_w, d),
                  pl.BlockSpec((1, 1, d), per_b), full(1, d), full(1, d),
                  pl.BlockSpec((1, 1, d), per_b), pl.BlockSpec((1, 1, d), per_b),
                  pl.BlockSpec((1, 1, d), per_b),
                  full(d, ff), full(d, ff), full(ff, d)],
        out_specs=[pl.BlockSpec((1, tm, d), row),
                   pl.BlockSpec((tm * n_sub, LANES), lambda b, i: (b * (seq // tm) + i, 0)),
                   pl.BlockSpec((1, tm, d), row)],
        out_shape=[jax.ShapeDtypeStruct((bsz, seq, d), F32),
                   jax.ShapeDtypeStruct((bsz * seq * n_sub, LANES), F32),
                   jax.ShapeDtypeStruct((bsz, seq, d), F32)],
        compiler_params=_cparams(("parallel", "parallel")),
        name="mix_ln1_shared",
    )(x, att, ssm, g_att.reshape(1, att_w), g_ssm.reshape(1, ssm_w), w_out.astype(BF16),
      vec(gate1), ln_g.reshape(1, d), ln_b.reshape(1, d), vec(scale2), vec(shift2), vec(gate2),
      w_s_gate.astype(BF16), w_s_up.astype(BF16), w_s_down.astype(BF16))


def _route_kernel(h_ref, wrt_ref, wrt_lo_ref, bias_ref, tri_ref, idx_ref, gate_ref, rank_ref, cnt_ref, carry_ref):
    n_e = wrt_ref.shape[0]
    tm = h_ref.shape[0]
    per_group = n_e // N_EXPERT_GROUPS
    neg = -jnp.inf

    @pl.when(pl.program_id(0) == 0)
    def _():
        carry_ref[...] = jnp.zeros_like(carry_ref)

    h_hi, h_lo = _split_bf16(h_ref[...])
    logits = (_dot_nt(wrt_ref[...], h_hi) + _dot_nt(wrt_ref[...], h_lo)
              + _dot_nt(wrt_lo_ref[...], h_hi))
    scores = jax.nn.sigmoid(logits)
    biased = scores + bias_ref[...]

    g3 = biased.reshape(N_EXPERT_GROUPS, per_group, tm)
    ridx = lax.broadcasted_iota(jnp.int32, g3.shape, 1).astype(F32)
    m1 = jnp.max(g3, axis=1, keepdims=True)
    first = jnp.min(jnp.where(g3 == m1, ridx, float(per_group)), axis=1, keepdims=True)
    m2 = jnp.max(jnp.where(ridx == first, neg, g3), axis=1, keepdims=True)
    gs = m1 + m2

    gidx = lax.broadcasted_iota(jnp.int32, gs.shape, 0).astype(F32)
    ok = jnp.zeros_like(gs)
    cur = gs
    for _ in range(TOPK_GROUPS):
        mx = jnp.max(cur, axis=0, keepdims=True)
        fi = jnp.min(jnp.where(cur == mx, gidx, float(N_EXPERT_GROUPS)), axis=0, keepdims=True)
        hit = gidx == fi
        ok = jnp.where(hit, 1.0, ok)
        cur = jnp.where(hit, neg, cur)
    masked = jnp.where(ok > 0.5, g3, neg).reshape(n_e, tm)

    eidx = lax.broadcasted_iota(jnp.int32, (n_e, tm), 0).astype(F32)
    onehot = jnp.zeros((n_e, tm), F32)
    cur = masked
    sel_idx = []
    sel_gate = []
    for _ in range(TOP_K):
        mx = jnp.max(cur, axis=0, keepdims=True)
        fi = jnp.min(jnp.where(cur == mx, eidx, float(n_e)), axis=0, keepdims=True)
        hit = eidx == fi
        sel_idx.append(fi)
        sel_gate.append(jnp.sum(jnp.where(hit, scores, 0.0), axis=0, keepdims=True))
        onehot = jnp.where(hit, 1.0, onehot)
        cur = jnp.where(hit, neg, cur)
    idx = jnp.concatenate(sel_idx, axis=0)
    gate = jnp.concatenate(sel_gate, axis=0)
    gate = gate / jnp.sum(gate, axis=0, keepdims=True) * ROUTED_SCALE

    prior = _dot(onehot.astype(BF16), tri_ref[...]) + carry_ref[:, 0:1]
    ranks = [jnp.sum(jnp.where(eidx == sel_idx[k], prior, 0.0), axis=0, keepdims=True)
             for k in range(TOP_K)]
    rank = jnp.concatenate(ranks, axis=0)
    carry = carry_ref[...] + jnp.sum(onehot, axis=1, keepdims=True)
    carry_ref[...] = carry
    cnt_ref[...] = carry

    idx_ref[...] = idx.astype(jnp.int32)
    gate_ref[...] = gate
    rank_ref[...] = rank.astype(jnp.int32)


def _route(h2d, w_router, router_bias):
    n_tok, d = h2d.shape
    n_e = w_router.shape[1]
    tm = min(ROUTE_COLS, n_tok)
    tri = (np.arange(tm)[:, None] < np.arange(tm)[None, :]).astype(np.float32)
    w_hi, w_lo = _split_bf16(w_router.T.astype(F32))
    col = lambda i: (0, i)
    full = lambda *shape: pl.BlockSpec(shape, lambda i: (0,) * len(shape))
    idx, gate, rank, cnt = pl.pallas_call(
        _route_kernel,
        grid=(n_tok // tm,),
        in_specs=[pl.BlockSpec((tm, d), lambda i: (i, 0)), full(n_e, d), full(n_e, d), full(n_e, 1),
                  full(tm, tm)],
        out_specs=[pl.BlockSpec((TOP_K, tm), col), pl.BlockSpec((TOP_K, tm), col),
                   pl.BlockSpec((TOP_K, tm), col), full(n_e, LANES)],
        out_shape=[jax.ShapeDtypeStruct((TOP_K, n_tok), jnp.int32),
                   jax.ShapeDtypeStruct((TOP_K, n_tok), F32),
                   jax.ShapeDtypeStruct((TOP_K, n_tok), jnp.int32),
                   jax.ShapeDtypeStruct((n_e, LANES), F32)],
        scratch_shapes=[pltpu.VMEM((n_e, LANES), F32)],
        compiler_params=_cparams(("arbitrary",)),
        name="router_topk",
    )(h2d, w_hi, w_lo, router_bias.reshape(n_e, 1).astype(F32), jnp.asarray(tri, BF16))
    return idx, gate, rank, cnt[:, 0].astype(jnp.int32)


def _sc_mesh():
    return plsc.VectorSubcoreMesh(core_axis_name="core", subcore_axis_name="subcore")


def _dispatch(h_tiles, gate_rows, dest_windows, n_rows):
    n_tok, n_sub, _ = h_tiles.shape
    win = DISPATCH_WINDOW
    index_spec = pl.BlockSpec((1, TOP_K * win), lambda i: (i, 0))

    def scatter(body, src, src_spec, out_type):
        @pl.kernel(out_type=out_type, mesh=_sc_mesh(), scratch_types=[])
        def run(s_hbm, d_hbm, o_hbm):
            pltpu.emit_pipeline(
                functools.partial(body, o_hbm),
                grid=(n_tok // win,),
                in_specs=[src_spec, index_spec],
                out_specs=[],
                core_axis_name=("core", "subcore"),
                dimension_semantics=(pltpu.PARALLEL,),
            )(s_hbm, d_hbm)
        return run(src, dest_windows)

    def tile_window(o_hbm, x_vmem, d_vmem):
        for k in range(TOP_K):
            pltpu.sync_copy(x_vmem, o_hbm.at[d_vmem.at[0, pl.ds(k * win, win)]])

    def gate_window(o_hbm, g_vmem, d_vmem):
        for k in range(TOP_K):
            pltpu.sync_copy(g_vmem.at[k], o_hbm.at[d_vmem.at[0, pl.ds(k * win, win)]])

    xs = scatter(tile_window, h_tiles, pl.BlockSpec((win, n_sub, LANES), lambda i: (i, 0, 0)),
                 jax.ShapeDtypeStruct((n_rows, n_sub, LANES), h_tiles.dtype))
    row_gates = scatter(gate_window, gate_rows, pl.BlockSpec((TOP_K, win, GATE_LANES), lambda i: (0, i, 0)),
                        jax.ShapeDtypeStruct((n_rows, GATE_LANES), F32))
    return xs, row_gates


def _gather_sum(ys_tiles, dest_windows, n_tok):
    _, n_sub, _ = ys_tiles.shape
    win = DISPATCH_WINDOW

    @pl.kernel(out_type=jax.ShapeDtypeStruct((n_tok, n_sub, LANES), ys_tiles.dtype), mesh=_sc_mesh(),
               scratch_types=[])
    def gather_tiles(y_hbm, d_hbm, o_hbm):
        def window(d_vmem, o_vmem):
            pltpu.sync_copy(y_hbm.at[d_vmem.at[0, pl.ds(0, win)]], o_vmem)
            for k in range(1, TOP_K):
                pltpu.sync_copy(y_hbm.at[d_vmem.at[0, pl.ds(k * win, win)]], o_vmem, add=True)
        pltpu.emit_pipeline(
            window,
            grid=(n_tok // win,),
            in_specs=[pl.BlockSpec((1, TOP_K * win), lambda i: (i, 0))],
            out_specs=[pl.BlockSpec((win, n_sub, LANES), lambda i: (i, 0, 0))],
            core_axis_name=("core", "subcore"),
            dimension_semantics=(pltpu.PARALLEL,),
        )(d_hbm, o_hbm)

    return gather_tiles(ys_tiles, dest_windows)


def _expert_kernel(first_ref, nblk_ref, cnt_ref, nused_ref, xs_ref, rg_ref, wg_ref, wu_ref, wd_ref, ys_ref,
                   xbuf_ref, gbuf_ref, ybuf_ref, wgb_ref, wub_ref, wdb_ref, xsem_ref, ysem_ref):
    e = pl.program_id(0)
    n_used = nused_ref[0]
    d = wgb_ref.shape[0]
    blk_rows = xbuf_ref.shape[1]

    class _InCopy:
        def __init__(self, g, slot):
            src = xs_ref.at[pl.ds(pl.multiple_of(g * blk_rows, blk_rows), blk_rows)]
            gsrc = rg_ref.at[pl.ds(pl.multiple_of(g * MOE_ROWS, MOE_ROWS), MOE_ROWS)]
            self.copies = (pltpu.make_async_copy(src, xbuf_ref.at[slot], xsem_ref.at[slot]),
                           pltpu.make_async_copy(gsrc, gbuf_ref.at[slot], xsem_ref.at[slot]))

        def start(self):
            for cp in self.copies:
                cp.start()

        def wait(self):
            for cp in self.copies:
                cp.wait()

    x_copy = _InCopy

    def y_copy(g, slot):
        dst = ys_ref.at[pl.ds(pl.multiple_of(g * blk_rows, blk_rows), blk_rows)]
        return pltpu.make_async_copy(ybuf_ref.at[slot], dst, ysem_ref.at[slot])

    n_slot = xbuf_ref.shape[0]

    @pl.when(e == 0)
    def _():
        for g in range(n_slot - 1):
            @pl.when(g < n_used)
            def _(g=g):
                x_copy(g, g).start()

    @pl.when(nblk_ref[e] > 0)
    def _():
        wgb_ref[...] = wg_ref[0].astype(BF16)
        wub_ref[...] = wu_ref[0].astype(BF16)
        wdb_ref[...] = wd_ref[0].astype(BF16)

    def block(j, carry):
        g = first_ref[e] + j
        slot = g & (n_slot - 1)
        x_copy(g, slot).wait()

        @pl.when(g + n_slot - 1 < n_used)
        def _():
            x_copy(g + n_slot - 1, (g + n_slot - 1) & (n_slot - 1)).start()

        @pl.when(g >= n_slot)
        def _():
            y_copy(g - n_slot, slot).wait()

        x = jnp.concatenate(_load_token_tiles(xbuf_ref.at[slot], MOE_ROWS, d), axis=1)
        row = lax.broadcasted_iota(jnp.int32, (MOE_ROWS, 1), 0)
        xb = jnp.where(row < cnt_ref[e] - j * MOE_ROWS, x, 0.0).astype(BF16)
        hid = _silu(_dot(xb, wgb_ref[...])) * _dot(xb, wub_ref[...])
        y = _dot(hid.astype(BF16), wdb_ref[...]) * gbuf_ref[slot][:, 0:1]
        _store_token_tiles(ybuf_ref.at[slot], jnp.where(row < cnt_ref[e] - j * MOE_ROWS, y, 0.0))
        y_copy(g, slot).start()
        return carry
    lax.fori_loop(0, nblk_ref[e], block, 0)

    @pl.when(e == pl.num_programs(0) - 1)
    def _():
        for back in range(1, n_slot + 1):
            @pl.when(n_used >= back)
            def _(back=back):
                y_copy(n_used - back, (n_used - back) & (n_slot - 1)).wait()


def _experts(xs_tiles, row_gates, plan, w_gate, w_up, w_down):
    n_rows, n_sub, _ = xs_tiles.shape
    d = n_sub * LANES
    n_e, _, ff = w_gate.shape
    blk_rows = MOE_ROWS * n_sub
    hbm = pl.BlockSpec(memory_space=pl.ANY)
    grid_spec = pltpu.PrefetchScalarGridSpec(
        num_scalar_prefetch=4,
        grid=(n_e,),
        in_specs=[hbm, hbm,
                  pl.BlockSpec((1, d, ff), lambda e, *_: (e, 0, 0)),
                  pl.BlockSpec((1, d, ff), lambda e, *_: (e, 0, 0)),
                  pl.BlockSpec((1, ff, d), lambda e, *_: (e, 0, 0))],
        out_specs=hbm,
        scratch_shapes=[pltpu.VMEM((EXPERT_SLOTS, blk_rows, LANES), F32),
                        pltpu.VMEM((EXPERT_SLOTS, MOE_ROWS, GATE_LANES), F32),
                        pltpu.VMEM((EXPERT_SLOTS, blk_rows, LANES), F32),
                        pltpu.VMEM((d, ff), BF16), pltpu.VMEM((d, ff), BF16), pltpu.VMEM((ff, d), BF16),
                        pltpu.SemaphoreType.DMA((EXPERT_SLOTS,)), pltpu.SemaphoreType.DMA((EXPERT_SLOTS,))],
    )
    ys = pl.pallas_call(
        _expert_kernel,
        grid_spec=grid_spec,
        out_shape=jax.ShapeDtypeStruct((n_rows * n_sub, LANES), F32),
        compiler_params=_cparams(("arbitrary",)),
        name="moe_experts",
    )(*plan, xs_tiles.reshape(n_rows * n_sub, LANES), row_gates, w_gate, w_up, w_down)
    return ys.reshape(n_rows, n_sub, LANES)


def _combine_kernel(dest_ref, ys_ref, pre_ref, g2_ref, ln_g_ref, ln_b_ref, o_ref, buf_ref, sem_ref):
    tm, d = pre_ref.shape[1], pre_ref.shape[2]
    n_sub = d // LANES

    def row_copy(t, k, src_row):
        slot = pl.multiple_of((k * tm + t) * n_sub, n_sub)
        return pltpu.make_async_copy(ys_ref.at[src_row], buf_ref.at[pl.ds(slot, n_sub)], sem_ref.at[0])

    def issue(g, c):
        t0 = g * DMA_ISSUE_UNROLL
        rows = [[dest_ref[(t0 + j) * TOP_K + k] for k in range(TOP_K)] for j in range(DMA_ISSUE_UNROLL)]
        for j in range(DMA_ISSUE_UNROLL):
            for k in range(TOP_K):
                row_copy(t0 + j, k, rows[j][k]).start(priority=k % 2)
        return c
    lax.fori_loop(0, tm // DMA_ISSUE_UNROLL, issue, 0)

    def drain(t, c):
        for k in range(TOP_K):
            row_copy(t, k, dest_ref[t * TOP_K + k]).wait()
        return c
    lax.fori_loop(0, tm, drain, 0)

    pieces = []
    for s in range(n_sub):
        acc = None
        for k in range(TOP_K):
            term = buf_ref[pl.ds(k * tm * n_sub + s, tm, stride=n_sub), :]
            acc = term if acc is None else acc + term
        pieces.append(acc)
    routed = jnp.concatenate(pieces, axis=1)
    y = pre_ref[0] + g2_ref[0] * routed
    o_ref[0] = _layer_norm(y, ln_g_ref[...], ln_b_ref[...])


def _combine(ys, dest, pre, gate2, ln_g, ln_b, n_tiles):
    bsz, seq, d = pre.shape
    tm = min(DISPATCH_ROWS, seq)
    per_seq = seq // tm
    row = lambda i, *_: (i // per_seq, i % per_seq, 0)
    per_b = lambda i, *_: (i // per_seq, 0, 0)
    full = lambda *shape: pl.BlockSpec(shape, lambda i, *_: (0,) * len(shape))
    grid_spec = pltpu.PrefetchScalarGridSpec(
        num_scalar_prefetch=0,
        grid=(n_tiles,),
        in_specs=[pl.BlockSpec((TOP_K * tm,), lambda i: (i,), memory_space=pltpu.SMEM),
                  pl.BlockSpec(memory_space=pl.ANY),
                  pl.BlockSpec((1, tm, d), row), pl.BlockSpec((1, 1, d), per_b), full(1, d), full(1, d)],
        out_specs=pl.BlockSpec((1, tm, d), row),
        scratch_shapes=[pltpu.VMEM((TOP_K * tm * (d // LANES), LANES), F32), pltpu.SemaphoreType.DMA((1,))],
    )
    return pl.pallas_call(
        _combine_kernel,
        grid_spec=grid_spec,
        out_shape=jax.ShapeDtypeStruct((bsz, seq, d), F32),
        compiler_params=_cparams(("arbitrary",)),
        name="moe_combine_ln2",
    )(dest, ys, pre, gate2.reshape(bsz, 1, d), ln_g.reshape(1, d), ln_b.reshape(1, d))


def _final_ln_kernel(r_ref, pre_ref, g2_ref, ln_g_ref, ln_b_ref, *rest):
    o_ref = rest[-1]
    tm, d = pre_ref.shape[1], pre_ref.shape[2]
    routed = jnp.concatenate(_load_token_tiles(r_ref, tm, d), axis=1)
    y = pre_ref[0] + g2_ref[0] * routed
    o_ref[0] = _layer_norm(y, ln_g_ref[...], ln_b_ref[...])


def _final_ln(routed_tiles, pre, gate2, ln_g, ln_b, first_tile, partial=None):
    bsz, seq, d = pre.shape
    tm = min(DISPATCH_ROWS, seq)
    per_seq = seq // tm
    n_sub = d // LANES
    n_tiles = bsz * per_seq - first_tile
    row = lambda i: ((i + first_tile) // per_seq, (i + first_tile) % per_seq, 0)
    per_b = lambda i: ((i + first_tile) // per_seq, 0, 0)
    full = lambda *shape: pl.BlockSpec(shape, lambda i: (0,) * len(shape))
    in_specs = [pl.BlockSpec((tm * n_sub, LANES), lambda i: (i, 0)),
                pl.BlockSpec((1, tm, d), row), pl.BlockSpec((1, 1, d), per_b), full(1, d), full(1, d)]
    args = [routed_tiles.reshape(-1, LANES), pre, gate2.reshape(bsz, 1, d), ln_g.reshape(1, d),
            ln_b.reshape(1, d)]
    aliases = {}
    if partial is not None:
        in_specs.append(pl.BlockSpec(memory_space=pl.ANY))
        args.append(partial)
        aliases = {len(args) - 1: 0}
    return pl.pallas_call(
        _final_ln_kernel,
        grid=(n_tiles,),
        in_specs=in_specs,
        out_specs=pl.BlockSpec((1, tm, d), row),
        out_shape=jax.ShapeDtypeStruct((bsz, seq, d), F32),
        input_output_aliases=aliases,
        compiler_params=_cparams(("arbitrary",)),
        name="moe_final_ln2",
    )(*args)


def _dest_kernel(pstart_ref, idx_ref, rank_ref, dest_ref):
    idx = idx_ref[...]

    def body(e, acc):
        return acc + jnp.where(idx == e, pstart_ref[e], 0)
    dest_ref[...] = lax.fori_loop(0, pstart_ref.shape[0], body, rank_ref[...], unroll=8)


def _dest_rows(pstart, idx, rank):
    n_k, n_tok = idx.shape
    tn = min(4096, n_tok)
    grid_spec = pltpu.PrefetchScalarGridSpec(
        num_scalar_prefetch=1,
        grid=(n_tok // tn,),
        in_specs=[pl.BlockSpec((n_k, tn), lambda i, *_: (0, i))] * 2,
        out_specs=pl.BlockSpec((n_k, tn), lambda i, *_: (0, i)),
    )
    return pl.pallas_call(
        _dest_kernel,
        grid_spec=grid_spec,
        out_shape=jax.ShapeDtypeStruct((n_k, n_tok), jnp.int32),
        compiler_params=_cparams(("arbitrary",)),
        name="moe_dest",
    )(pstart, idx, rank)


def _moe_plan(idx, rank, cnt, n_tok):
    n_e = cnt.shape[0]
    padded = (cnt + MOE_ROWS - 1) // MOE_ROWS * MOE_ROWS
    pend = jnp.cumsum(padded).astype(jnp.int32)
    pstart = pend - padded
    dest_kt = _dest_rows(pstart, idx, rank)
    dest = dest_kt.T.reshape(-1)
    win = min(DISPATCH_WINDOW, n_tok)
    dest_windows = dest_kt.reshape(TOP_K, n_tok // win, win).transpose(1, 0, 2).reshape(n_tok // win, TOP_K * win)
    n_blk = n_tok * TOP_K // MOE_ROWS + n_e
    n_used = (pend[-1:] // MOE_ROWS).astype(jnp.int32)
    plan = ((pstart // MOE_ROWS).astype(jnp.int32), (padded // MOE_ROWS).astype(jnp.int32),
            cnt.astype(jnp.int32), n_used)
    return dest, dest_windows, plan, n_blk * MOE_ROWS


def kernel(x, c, rel_bias, w_ada, b_ada, w_in, ssm_a_re, ssm_a_im, ssm_b_re, ssm_b_im, ssm_c_re, ssm_c_im, ssm_d, ssm_log_dt, w_glu, b_glu, g_att, g_ssm, w_out, ln1_g, ln1_b, w_router, router_bias, w_e_gate, w_e_up, w_e_down, w_s_gate, w_s_up, w_s_down, ln2_g, ln2_b):
    bsz, seq, d = x.shape
    depth = w_ada.shape[0]
    alpha = (2 * depth) ** 0.25
    att_w = g_att.shape[-1]
    tables = _att_tables(rel_bias)
    for layer in range(depth):
        ada = _ada(c, w_ada[layer], b_ada[layer])
        shift1, scale1, gate1, shift2, scale2, gate2 = jnp.split(ada, 6, axis=-1)
        q, k, v, u = _inproj(x, scale1, shift1, w_in[layer], att_w)
        att = _attention(q, k, v, tables)
        ssm_prm = _ssm_params(ssm_a_re[layer], ssm_a_im[layer], ssm_b_re[layer], ssm_b_im[layer],
                              ssm_c_re[layer], ssm_c_im[layer], ssm_d[layer], ssm_log_dt[layer])
        ssm = _ssm(u, *ssm_prm, w_glu[layer], b_glu[layer])
        h2, h2_tiles, pre = _mix(x, att, ssm, g_att[layer], g_ssm[layer], w_out[layer], gate1,
                                 ln1_g[layer], ln1_b[layer], scale2, shift2, gate2, w_s_gate[layer],
                                 w_s_up[layer], w_s_down[layer], alpha)
        h2d = h2.reshape(bsz * seq, d)
        idx, gate, rank, cnt = _route(h2d, w_router[layer], router_bias[layer])
        dest, dest_windows, plan, n_rows = _moe_plan(idx, rank, cnt, bsz * seq)
        n_tok = bsz * seq
        gate_rows = jnp.broadcast_to(gate[:, :, None], (TOP_K, n_tok, GATE_LANES))
        xs, row_gates = _dispatch(h2_tiles.reshape(n_tok, d // LANES, LANES), gate_rows, dest_windows, n_rows)
        ys = _experts(xs, row_gates, plan, w_e_gate[layer], w_e_up[layer], w_e_down[layer])
        tm = min(DISPATCH_ROWS, seq)
        tc_tiles = min(COMBINE_TC_TILES, n_tok // tm)
        tc_tokens = tc_tiles * tm
        win = min(DISPATCH_WINDOW, n_tok)
        partial = (_combine(ys, dest, pre, gate2, ln2_g[layer], ln2_b[layer], tc_tiles)
                   if tc_tiles > 0 else None)
        routed = _gather_sum(ys, dest_windows[tc_tokens // win:], n_tok - tc_tokens)
        x = _final_ln(routed, pre, gate2, ln2_g[layer], ln2_b[layer], tc_tiles, partial)
    return x
```

```python
import functools
import math

import jax
import jax.numpy as jnp
import numpy as np
from jax import lax
from jax.experimental import pallas as pl
from jax.experimental.pallas import tpu as pltpu
from jax.experimental.pallas import tpu_sc as plsc

F32 = jnp.float32
BF16 = jnp.bfloat16

HEAD_DIM = 64
ATT_BLOCK = 128
PATTERNS = ((128, 1), (512, 4), (2048, 16))
N_BUCKETS = 32
MAX_DISTANCE = 2048
TOP_K = 8
N_EXPERT_GROUPS = 8
TOPK_GROUPS = 4
ROUTED_SCALE = 2.5
EPS = 1e-5
NEG_INF = -1e30
LOG2_E = math.log2(math.e)

LANES = 128
SUBLANES = 8
VMEM_LIMIT_BYTES = 56 * 1024 * 1024

PROJ_ROWS = 512
SSM_ROWS = 512
SSM_TAPS = 4
MIX_ROWS = 512
ROUTE_COLS = 512
MOE_ROWS = 256
DISPATCH_ROWS = 256
COMBINE_TC_TILES = 48
GATE_LANES = LANES
DISPATCH_WINDOW = 32
ATT_UNITS_PER_STEP = 8
DMA_ISSUE_UNROLL = 4
EXPERT_SLOTS = 4


def _cparams(sem, vmem=VMEM_LIMIT_BYTES):
    return pltpu.CompilerParams(dimension_semantics=sem, vmem_limit_bytes=vmem)


def _dot(a, b):
    return jnp.dot(a, b, preferred_element_type=F32)


def _dot_nt(a, b):
    return lax.dot_general(a, b, (((1,), (1,)), ((), ())), preferred_element_type=F32)


def _silu(x):
    return x * jax.nn.sigmoid(x)


def _store_token_tiles(ref, x):
    m, d = x.shape
    n_sub = d // LANES
    for s in range(n_sub):
        ref[pl.ds(s, m, stride=n_sub), :] = x[:, s * LANES:(s + 1) * LANES]


def _load_token_tiles(ref, m, d, row0=0):
    n_sub = d // LANES
    return [ref[pl.ds(row0 + s, m, stride=n_sub), :] for s in range(n_sub)]


def _split_bf16(x):
    hi = x.astype(BF16)
    lo = (x - hi.astype(F32)).astype(BF16)
    return hi, lo


def _ada_kernel(c_ref, w_ref, b_ref, o_ref):
    c = c_ref[...]
    a_hi, a_lo = _split_bf16(_silu(c))
    w_hi, w_lo = _split_bf16(w_ref[...])
    acc = _dot(a_hi, w_hi) + _dot(a_hi, w_lo) + _dot(a_lo, w_hi)
    o_ref[...] = acc + b_ref[...]


def _ada(c, w_ada, b_ada):
    bsz, d = c.shape
    n = w_ada.shape[1]
    rows = SUBLANES
    c_pad = jnp.zeros((rows, d), F32).at[:bsz].set(c)
    tn = 1024
    out = pl.pallas_call(
        _ada_kernel,
        grid=(n // tn,),
        in_specs=[pl.BlockSpec((rows, d), lambda j: (0, 0)),
                  pl.BlockSpec((d, tn), lambda j: (0, j)),
                  pl.BlockSpec((1, tn), lambda j: (0, j))],
        out_specs=pl.BlockSpec((rows, tn), lambda j: (0, j)),
        out_shape=jax.ShapeDtypeStruct((rows, n), F32),
        compiler_params=_cparams(("parallel",)),
        name="ada",
    )(c_pad, w_ada, b_ada.reshape(1, n))
    return out[:bsz]


def _inproj_kernel(x_ref, sc_ref, sh_ref, w_ref, q_ref, k_ref, v_ref, u_ref, *, att_w, q_scale):
    h = (x_ref[0] * (1.0 + sc_ref[0]) + sh_ref[0]).astype(BF16)
    q_ref[0] = _dot(h, w_ref[:, 0:att_w]) * q_scale
    k_ref[0] = _dot(h, w_ref[:, att_w:2 * att_w])
    v_ref[0] = _dot(h, w_ref[:, 2 * att_w:3 * att_w])
    u_ref[0] = _dot(h, w_ref[:, 3 * att_w:])


def _inproj(x, scale, shift, w_in, att_w):
    bsz, seq, d = x.shape
    n = w_in.shape[1]
    ssm_w = n - 3 * att_w
    ts = min(PROJ_ROWS, seq)
    row = lambda b, i: (b, i, 0)
    per_b = lambda b, i: (b, 0, 0)
    kern = functools.partial(_inproj_kernel, att_w=att_w, q_scale=HEAD_DIM ** -0.5 * LOG2_E)
    return pl.pallas_call(
        kern,
        grid=(bsz, seq // ts),
        in_specs=[pl.BlockSpec((1, ts, d), row),
                  pl.BlockSpec((1, 1, d), per_b),
                  pl.BlockSpec((1, 1, d), per_b),
                  pl.BlockSpec((d, n), lambda b, i: (0, 0))],
        out_specs=[pl.BlockSpec((1, ts, att_w), row)] * 3 + [pl.BlockSpec((1, ts, ssm_w), row)],
        out_shape=[jax.ShapeDtypeStruct((bsz, seq, att_w), F32)] * 3
                  + [jax.ShapeDtypeStruct((bsz, seq, ssm_w), F32)],
        compiler_params=_cparams(("parallel", "parallel")),
        name="inproj",
    )(x, scale.reshape(bsz, 1, d), shift.reshape(bsz, 1, d), w_in.astype(BF16))


def _t5_bucket_np(dist):
    exact = N_BUCKETS // 2
    large = exact + (np.log(np.maximum(dist, 1).astype(np.float64) / exact)
                     / math.log(MAX_DISTANCE / exact) * (N_BUCKETS - exact)).astype(np.int64)
    return np.where(dist < exact, dist, np.minimum(large, N_BUCKETS - 1))


def _att_tables(rel_bias):
    qi = np.arange(ATT_BLOCK)[:, None]
    ki = np.arange(2 * ATT_BLOCK)[None, :]
    rel = qi + ATT_BLOCK - ki
    tabs = []
    for window, dil in PATTERNS:
        band = (rel >= 0) & (rel <= window // dil)
        bucket = _t5_bucket_np(np.maximum(rel, 0) * dil)
        onehot = (bucket[:, :, None] == np.arange(N_BUCKETS)[None, None, :]).astype(np.float32)
        bias = jnp.einsum('qkb,bh->hqk', onehot, rel_bias.astype(F32),
                          precision=lax.Precision.HIGHEST)
        full = jnp.where(band[None], bias * LOG2_E, NEG_INF)
        first = jnp.concatenate([full[:, :, ATT_BLOCK:], jnp.full_like(full[:, :, ATT_BLOCK:], NEG_INF)], axis=-1)
        tabs.append(jnp.stack([full, first]))
    return jnp.stack(tabs)


def _att_kernel(q_ref, k_ref, v_ref, tab_ref, o_ref, m_ref, l_ref, *, seq):
    lane = lax.broadcasted_iota(jnp.int32, (ATT_BLOCK, LANES), 1)
    head0 = lane < HEAD_DIM

    def rows(ref, start, n, dil):
        if dil == 1:
            return ref[0, pl.ds(pl.multiple_of(start, ATT_BLOCK), n), :]
        return ref[0, pl.ds(start, n, stride=dil), :]

    def lane_mask(n_rows, h):
        m = lax.broadcasted_iota(jnp.int32, (n_rows, LANES), 1) < HEAD_DIM
        return m if h == 0 else ~m

    def batch(pi, dil, nb, res, blk0, head, init):
        q_rows = nb * ATT_BLOCK
        kv_blocks = nb if head else nb + 1
        q_start = res + dil * ATT_BLOCK * blk0
        kv_start = q_start if head else q_start - dil * ATT_BLOCK
        if dil == 1:
            sl = pl.ds(pl.multiple_of(q_start, ATT_BLOCK), q_rows)
        else:
            sl = pl.ds(q_start, q_rows, stride=dil)
        q = rows(q_ref, q_start, q_rows, dil)
        k = rows(k_ref, kv_start, kv_blocks * ATT_BLOCK, dil).astype(BF16)
        v = rows(v_ref, kv_start, kv_blocks * ATT_BLOCK, dil)
        if not init:
            m_old, l_old, o_old = m_ref[sl, :], l_ref[sl, :], o_ref[0, sl, :]
        qh = [jnp.where(lane_mask(q_rows, h), q, 0.0).astype(BF16) for h in range(2)]
        vh = [jnp.where(lane_mask(kv_blocks * ATT_BLOCK, h), v, 1.0).astype(BF16) for h in range(2)]
        o_parts, l_parts, m_parts = [], [], []
        for j in range(nb):
            first = 1 if (head and j == 0) else 0
            kb = j if (not head or j == 0) else j - 1
            ksl = slice(kb * ATT_BLOCK, (kb + 2) * ATT_BLOCK)
            qsl = slice(j * ATT_BLOCK, (j + 1) * ATT_BLOCK)
            outs, ms = [], []
            for h in range(2):
                s = _dot_nt(qh[h][qsl], k[ksl]) + tab_ref[pi, first, h]
                m = jnp.max(s, axis=1, keepdims=True)
                p = jnp.exp2(s - m).astype(BF16)
                outs.append(_dot(p, vh[h][ksl]))
                ms.append(m)
            o_parts.append(jnp.where(head0, outs[0], outs[1]))
            l_parts.append(pltpu.roll(jnp.where(head0, outs[1], outs[0]), HEAD_DIM, axis=1))
            m_parts.append(jnp.where(head0, ms[0], ms[1]))
        o = jnp.concatenate(o_parts, axis=0)
        l = jnp.concatenate(l_parts, axis=0)
        m = jnp.concatenate(m_parts, axis=0)
        if not init:
            m_new = jnp.maximum(m_old, m)
            a_old = jnp.exp2(m_old - m_new)
            a_cur = jnp.exp2(m - m_new)
            o = o_old * a_old + o * a_cur
            l = l_old * a_old + l * a_cur
            m = m_new
        o_ref[0, sl, :] = o
        m_ref[sl, :] = m
        l_ref[sl, :] = l

    n_units = seq // ATT_BLOCK
    order = sorted(range(len(PATTERNS)), key=lambda p: -PATTERNS[p][1])
    for pos, pi in enumerate(order):
        dil = PATTERNS[pi][1]
        log_d = dil.bit_length() - 1
        res_blocks = seq // (dil * ATT_BLOCK)
        nb = min(ATT_UNITS_PER_STEP, res_blocks)
        per_res = res_blocks // nb
        init = pos == 0

        res_step = 2 if (nb < ATT_UNITS_PER_STEP and dil % 2 == 0) else 1

        def head_body(r, carry, pi=pi, dil=dil, nb=nb, init=init, res_step=res_step):
            for dr in range(res_step):
                batch(pi, dil, nb, r * res_step + dr, 0, True, init)
            return carry
        lax.fori_loop(0, dil // res_step, head_body, 0)

        if per_res > 1:
            def tail_body(i, carry, pi=pi, dil=dil, nb=nb, log_d=log_d, init=init):
                res = i & (dil - 1)
                blk0 = nb * (1 + (i >> log_d))
                batch(pi, dil, nb, res, blk0, False, init)
                return carry
            lax.fori_loop(0, dil * (per_res - 1), tail_body, 0)

    def finish(i, carry):
        sl = pl.ds(pl.multiple_of(i * ATT_BLOCK, ATT_BLOCK), ATT_BLOCK)
        o_ref[0, sl, :] = o_ref[0, sl, :] / l_ref[sl, :]
        return carry
    lax.fori_loop(0, n_units, finish, 0)


def _attention(q, k, v, tables):
    bsz, seq, att_w = q.shape
    n_pairs = att_w // LANES
    blk = lambda b, p: (b, 0, p)
    qkv_spec = pl.BlockSpec((1, seq, LANES), blk)
    n_pat = tables.shape[0]
    tab_spec = pl.BlockSpec((n_pat, 2, 2, ATT_BLOCK, 2 * ATT_BLOCK), lambda b, p: (0, 0, p, 0, 0))
    return pl.pallas_call(
        functools.partial(_att_kernel, seq=seq),
        grid=(bsz, n_pairs),
        in_specs=[qkv_spec, qkv_spec, qkv_spec, tab_spec],
        out_specs=pl.BlockSpec((1, seq, LANES), blk),
        out_shape=jax.ShapeDtypeStruct((bsz, seq, att_w), F32),
        scratch_shapes=[pltpu.VMEM((seq, LANES), F32), pltpu.VMEM((seq, LANES), F32)],
        compiler_params=_cparams(("parallel", "parallel")),
        name="dilated_attention",
    )(q, k, v, tables)


def _ssm_params(a_re, a_im, b_re, b_im, c_re, c_im, d_skip, log_dt):
    n_g, n_p = a_re.shape
    n_c = b_re.shape[-1]
    n_state = n_g * n_p
    dt = jnp.exp(log_dt.astype(F32))[:, None]

    def a_pow(kk):
        mag = jnp.exp(kk * dt * a_re)
        ph = kk * dt * a_im
        return mag * jnp.cos(ph), mag * jnp.sin(ph)

    ab_re, ab_im = a_pow(1.0)
    nr, ni = ab_re - 1.0, ab_im
    den = a_re * a_re + a_im * a_im
    f_re = (nr * a_re + ni * a_im) / den
    f_im = (ni * a_re - nr * a_im) / den
    bb_re = f_re[:, :, None] * b_re - f_im[:, :, None] * b_im
    bb_im = f_re[:, :, None] * b_im + f_im[:, :, None] * b_re
    eye = jnp.eye(n_g, dtype=F32)
    w_re = jnp.einsum('gpc,gh->gchp', bb_re, eye).reshape(n_g * n_c, n_state)
    w_im = jnp.einsum('gpc,gh->gchp', bb_im, eye).reshape(n_g * n_c, n_state)
    o_re = jnp.einsum('gcp,gh->gphc', c_re, eye).reshape(n_state, n_g * n_c)
    o_im = jnp.einsum('gcp,gh->gphc', c_im, eye).reshape(n_state, n_g * n_c)
    n_kb = (n_g * n_c) // LANES
    sp = n_state // n_kb
    def tap(kb, j):
        p_re, p_im = a_pow(float(j))
        p_re = p_re.reshape(1, n_state)[:, kb * sp:(kb + 1) * sp]
        p_im = p_im.reshape(1, n_state)[:, kb * sp:(kb + 1) * sp]
        b_r = w_re[kb * LANES:(kb + 1) * LANES, kb * sp:(kb + 1) * sp]
        b_i = w_im[kb * LANES:(kb + 1) * LANES, kb * sp:(kb + 1) * sp]
        return jnp.concatenate([b_r * p_re - b_i * p_im, b_r * p_im + b_i * p_re], axis=1)
    w_in = jnp.stack([jnp.concatenate([tap(kb, j) for j in range(SSM_TAPS)], axis=0)
                      for kb in range(n_kb)])
    w_out = jnp.stack([jnp.concatenate([o_re[kb * sp:(kb + 1) * sp, kb * LANES:(kb + 1) * LANES],
                                        -o_im[kb * sp:(kb + 1) * sp, kb * LANES:(kb + 1) * LANES]], axis=0)
                       for kb in range(n_kb)])
    j = jnp.arange(SUBLANES, dtype=F32)[:, None]
    flat = lambda t: jnp.broadcast_to(t.reshape(1, n_state), (SUBLANES, n_state))
    coef = []
    for sh in (1, 2, 4):
        pr, pim = a_pow(float(sh))
        keep = (j >= sh).astype(F32)
        coef += [flat(pr) * keep, flat(pim) * keep]
    a_re_f = jnp.broadcast_to(a_re.reshape(1, n_state), (SUBLANES, n_state))
    a_im_f = jnp.broadcast_to(a_im.reshape(1, n_state), (SUBLANES, n_state))
    dt_f = jnp.broadcast_to(jnp.repeat(dt[:, 0], n_p).reshape(1, n_state), (SUBLANES, n_state))
    mag = jnp.exp((j + 1.0) * dt_f * a_re_f)
    ph = (j + 1.0) * dt_f * a_im_f
    coef += [mag * jnp.cos(ph), mag * jnp.sin(ph)]
    coef = jnp.stack(coef)
    return w_in.astype(BF16), w_out.astype(BF16), coef, d_skip.reshape(1, n_g * n_c).astype(F32)


def _gelu_tanh(x):
    return 0.5 * x * (1.0 + jnp.tanh(math.sqrt(2.0 / math.pi) * (x + 0.044715 * (x * x * x))))


def _ssm_kernel(u_ref, win_ref, wout_ref, coef_ref, d_ref, wglu_ref, bglu_ref, o_ref,
                hre_ref, him_ref, cre_ref, cim_ref, *, rows, n_state, col_w):
    n_kb = win_ref.shape[0]
    sp = n_state // n_kb

    @pl.when(pl.program_id(1) == 0)
    def _():
        cre_ref[...] = jnp.zeros_like(cre_ref)
        cim_ref[...] = jnp.zeros_like(cim_ref)

    u = u_ref[0]
    row_in_slab = lax.broadcasted_iota(jnp.int32, u.shape, 0) & (SUBLANES - 1)
    taps = [u.astype(BF16)] + [jnp.where(row_in_slab >= j, pltpu.roll(u, j, axis=0), 0.0).astype(BF16)
                               for j in range(1, SSM_TAPS)]
    for kb in range(n_kb):
        lhs = jnp.concatenate([t[:, kb * LANES:(kb + 1) * LANES] for t in taps], axis=1)
        bu = _dot(lhs, win_ref[kb])
        hre_ref[:, kb * sp:(kb + 1) * sp] = bu[:, :sp]
        him_ref[:, kb * sp:(kb + 1) * sp] = bu[:, sp:]

    n_slab = rows // SUBLANES
    for c0 in range(0, n_state, col_w):
        cs = slice(c0, c0 + col_w)

        def body(i, carry, cs=cs):
            c_re, c_im = carry
            sl = pl.ds(pl.multiple_of(i * SUBLANES, SUBLANES), SUBLANES)
            x_re = hre_ref[sl, cs]
            x_im = him_ref[sl, cs]
            for si, sh in enumerate((1, 2, 4)):
                if sh < SSM_TAPS:
                    continue
                p_re = coef_ref[2 * si, :, cs]
                p_im = coef_ref[2 * si + 1, :, cs]
                s_re = pltpu.roll(x_re, sh, axis=0)
                s_im = pltpu.roll(x_im, sh, axis=0)
                x_re, x_im = (x_re + p_re * s_re - p_im * s_im,
                              x_im + p_re * s_im + p_im * s_re)
            p_re = coef_ref[6, :, cs]
            p_im = coef_ref[7, :, cs]
            h_re = x_re + p_re * c_re - p_im * c_im
            h_im = x_im + p_re * c_im + p_im * c_re
            hre_ref[sl, cs] = h_re
            him_ref[sl, cs] = h_im
            last = slice(SUBLANES - 1, SUBLANES)
            return (jnp.broadcast_to(h_re[last, :], h_re.shape),
                    jnp.broadcast_to(h_im[last, :], h_im.shape))

        c_re, c_im = lax.fori_loop(0, n_slab, body, (cre_ref[:, cs], cim_ref[:, cs]), unroll=True)
        cre_ref[:, cs] = c_re
        cim_ref[:, cs] = c_im

    ys = []
    for kb in range(n_kb):
        hcat = jnp.concatenate([hre_ref[:, kb * sp:(kb + 1) * sp].astype(BF16),
                                him_ref[:, kb * sp:(kb + 1) * sp].astype(BF16)], axis=1)
        ys.append(_dot(hcat, wout_ref[kb]))
    y = jnp.concatenate(ys, axis=1) + d_ref[...] * u
    y = _gelu_tanh(y)
    z = _dot(y.astype(BF16), wglu_ref[...]) + bglu_ref[...]
    o_ref[0] = y * jax.nn.sigmoid(z)


def _ssm(u, w_in, w_out, coef, d_flat, w_glu, b_glu):
    bsz, seq, ssm_w = u.shape
    n_state = coef.shape[-1]
    rows = min(SSM_ROWS, seq)
    full = lambda *shape: pl.BlockSpec(shape, lambda b, i: (0,) * len(shape))
    kern = functools.partial(_ssm_kernel, rows=rows, n_state=n_state, col_w=4 * LANES)
    return pl.pallas_call(
        kern,
        grid=(bsz, seq // rows),
        in_specs=[pl.BlockSpec((1, rows, ssm_w), lambda b, i: (b, i, 0)),
                  full(*w_in.shape), full(*w_out.shape), full(*coef.shape), full(1, ssm_w),
                  full(ssm_w, ssm_w), full(1, ssm_w)],
        out_specs=pl.BlockSpec((1, rows, ssm_w), lambda b, i: (b, i, 0)),
        out_shape=jax.ShapeDtypeStruct((bsz, seq, ssm_w), F32),
        scratch_shapes=[pltpu.VMEM((rows, n_state), F32), pltpu.VMEM((rows, n_state), F32),
                        pltpu.VMEM((SUBLANES, n_state), F32), pltpu.VMEM((SUBLANES, n_state), F32)],
        compiler_params=_cparams(("parallel", "arbitrary")),
        name="s5_glu",
    )(u, w_in, w_out, coef, d_flat, w_glu.astype(BF16), b_glu.reshape(1, ssm_w))


def _layer_norm(y, g, b):
    mu = jnp.mean(y, axis=-1, keepdims=True)
    yc = y - mu
    var = jnp.mean(yc * yc, axis=-1, keepdims=True)
    return yc * lax.rsqrt(var + EPS) * g + b


def _rms_norm(y, g):
    return y * lax.rsqrt(jnp.mean(y * y, axis=-1, keepdims=True) + EPS) * g


def _mix_kernel(x_ref, att_ref, ssm_ref, gatt_ref, gssm_ref, wout_ref, g1_ref, ln_g_ref, ln_b_ref,
                sc_ref, sh_ref, g2_ref, wsg_ref, wsu_ref, wsd_ref, ht_ref, pre_ref, *, alpha, att_w):
    a_n = _rms_norm(att_ref[0], gatt_ref[...]).astype(BF16)
    s_n = _rms_norm(ssm_ref[0], gssm_ref[...]).astype(BF16)
    mix = _dot(a_n, wout_ref[0:att_w, :]) + _dot(s_n, wout_ref[att_w:, :])
    x1 = _layer_norm(alpha * x_ref[0] + g1_ref[0] * mix, ln_g_ref[...], ln_b_ref[...])
    h = x1 * (1.0 + sc_ref[0]) + sh_ref[0]
    _store_token_tiles(ht_ref, h)
    hb = h.astype(BF16)
    hid = _silu(_dot(hb, wsg_ref[...])) * _dot(hb, wsu_ref[...])
    shared = _dot(hid.astype(BF16), wsd_ref[...])
    pre_ref[0] = alpha * x1 + g2_ref[0] * shared


def _mix(x, att, ssm, g_att, g_ssm, w_out, gate1, ln_g, ln_b, scale2, shift2, gate2,
         w_s_gate, w_s_up, w_s_down, alpha):
    bsz, seq, d = x.shape
    att_w = att.shape[-1]
    ssm_w = ssm.shape[-1]
    ff = w_s_gate.shape[1]
    tm = min(MIX_ROWS, seq)
    n_sub = d // LANES
    row = lambda b, i: (b, i, 0)
    per_b = lambda b, i: (b, 0, 0)
    full = lambda *shape: pl.BlockSpec(shape, lambda b, i: (0,) * len(shape))
    vec = lambda t: t.reshape(bsz, 1, d)
    return pl.pallas_call(
        functools.partial(_mix_kernel, alpha=alpha, att_w=att_w),
        grid=(bsz, seq // tm),
        in_specs=[pl.BlockSpec((1, tm, d), row), pl.BlockSpec((1, tm, att_w), row),
                  pl.BlockSpec((1, tm, ssm_w), row),
                  full(1, att_w), full(1, ssm_w), full(att_w + ssm_w, d),
                  pl.BlockSpec((1, 1, d), per_b), full(1, d), full(1, d),
                  pl.BlockSpec((1, 1, d), per_b), pl.BlockSpec((1, 1, d), per_b),
                  pl.BlockSpec((1, 1, d), per_b),
                  full(d, ff), full(d, ff), full(ff, d)],
        out_specs=[pl.BlockSpec((tm * n_sub, LANES), lambda b, i: (b * (seq // tm) + i, 0)),
                   pl.BlockSpec((1, tm, d), row)],
        out_shape=[jax.ShapeDtypeStruct((bsz * seq * n_sub, LANES), F32),
                   jax.ShapeDtypeStruct((bsz, seq, d), F32)],
        compiler_params=_cparams(("parallel", "parallel")),
        name="mix_ln1_shared",
    )(x, att, ssm, g_att.reshape(1, att_w), g_ssm.reshape(1, ssm_w), w_out.astype(BF16),
      vec(gate1), ln_g.reshape(1, d), ln_b.reshape(1, d), vec(scale2), vec(shift2), vec(gate2),
      w_s_gate.astype(BF16), w_s_up.astype(BF16), w_s_down.astype(BF16))


def _route_kernel(h_ref, wrt_ref, wrt_lo_ref, bias_ref, tri_ref, idx_ref, gate_ref, rank_ref, cnt_ref, carry_ref):
    n_e, d = wrt_ref.shape
    tm = h_ref.shape[0] * LANES // d
    per_group = n_e // N_EXPERT_GROUPS
    neg = -jnp.inf

    @pl.when(pl.program_id(0) == 0)
    def _():
        carry_ref[...] = jnp.zeros_like(carry_ref)

    h_hi, h_lo = _split_bf16(jnp.concatenate(_load_token_tiles(h_ref, tm, d), axis=1))
    logits = (_dot_nt(wrt_ref[...], h_hi) + _dot_nt(wrt_ref[...], h_lo)
              + _dot_nt(wrt_lo_ref[...], h_hi))
    scores = jax.nn.sigmoid(logits)
    biased = scores + bias_ref[...]

    g3 = biased.reshape(N_EXPERT_GROUPS, per_group, tm)
    ridx = lax.broadcasted_iota(jnp.int32, g3.shape, 1).astype(F32)
    m1 = jnp.max(g3, axis=1, keepdims=True)
    first = jnp.min(jnp.where(g3 == m1, ridx, float(per_group)), axis=1, keepdims=True)
    m2 = jnp.max(jnp.where(ridx == first, neg, g3), axis=1, keepdims=True)
    gs = m1 + m2

    gidx = lax.broadcasted_iota(jnp.int32, gs.shape, 0).astype(F32)
    ok = jnp.zeros_like(gs)
    cur = gs
    for _ in range(TOPK_GROUPS):
        mx = jnp.max(cur, axis=0, keepdims=True)
        fi = jnp.min(jnp.where(cur == mx, gidx, float(N_EXPERT_GROUPS)), axis=0, keepdims=True)
        hit = gidx == fi
        ok = jnp.where(hit, 1.0, ok)
        cur = jnp.where(hit, neg, cur)
    masked = jnp.where(ok > 0.5, g3, neg).reshape(n_e, tm)

    eidx = lax.broadcasted_iota(jnp.int32, (n_e, tm), 0).astype(F32)
    onehot = jnp.zeros((n_e, tm), F32)
    cur = masked
    sel_idx = []
    sel_gate = []
    for _ in range(TOP_K):
        mx = jnp.max(cur, axis=0, keepdims=True)
        fi = jnp.min(jnp.where(cur == mx, eidx, float(n_e)), axis=0, keepdims=True)
        hit = eidx == fi
        sel_idx.append(fi)
        sel_gate.append(jnp.sum(jnp.where(hit, scores, 0.0), axis=0, keepdims=True))
        onehot = jnp.where(hit, 1.0, onehot)
        cur = jnp.where(hit, neg, cur)
    idx = jnp.concatenate(sel_idx, axis=0)
    gate = jnp.concatenate(sel_gate, axis=0)
    gate = gate / jnp.sum(gate, axis=0, keepdims=True) * ROUTED_SCALE

    prior = _dot(onehot.astype(BF16), tri_ref[...]) + carry_ref[:, 0:1]
    ranks = [jnp.sum(jnp.where(eidx == sel_idx[k], prior, 0.0), axis=0, keepdims=True)
             for k in range(TOP_K)]
    rank = jnp.concatenate(ranks, axis=0)
    carry = carry_ref[...] + jnp.sum(onehot, axis=1, keepdims=True)
    carry_ref[...] = carry
    cnt_ref[...] = carry

    idx_ref[...] = idx.astype(jnp.int32)
    gate_ref[...] = gate
    rank_ref[...] = rank.astype(jnp.int32)


def _route(h_tiles, w_router, router_bias):
    d, n_e = w_router.shape
    n_sub = d // LANES
    n_tok = h_tiles.shape[0] // n_sub
    tm = min(ROUTE_COLS, n_tok)
    tri = (np.arange(tm)[:, None] < np.arange(tm)[None, :]).astype(np.float32)
    w_hi, w_lo = _split_bf16(w_router.T.astype(F32))
    col = lambda i: (0, i)
    full = lambda *shape: pl.BlockSpec(shape, lambda i: (0,) * len(shape))
    idx, gate, rank, cnt = pl.pallas_call(
        _route_kernel,
        grid=(n_tok // tm,),
        in_specs=[pl.BlockSpec((tm * n_sub, LANES), lambda i: (i, 0)), full(n_e, d), full(n_e, d),
                  full(n_e, 1), full(tm, tm)],
        out_specs=[pl.BlockSpec((TOP_K, tm), col), pl.BlockSpec((TOP_K, tm), col),
                   pl.BlockSpec((TOP_K, tm), col), full(n_e, LANES)],
        out_shape=[jax.ShapeDtypeStruct((TOP_K, n_tok), jnp.int32),
                   jax.ShapeDtypeStruct((TOP_K, n_tok), F32),
                   jax.ShapeDtypeStruct((TOP_K, n_tok), jnp.int32),
                   jax.ShapeDtypeStruct((n_e, LANES), F32)],
        scratch_shapes=[pltpu.VMEM((n_e, LANES), F32)],
        compiler_params=_cparams(("arbitrary",)),
        name="router_topk",
    )(h_tiles, w_hi, w_lo, router_bias.reshape(n_e, 1).astype(F32), jnp.asarray(tri, BF16))
    return idx, gate, rank, cnt[:, 0].astype(jnp.int32)


def _sc_mesh():
    return plsc.VectorSubcoreMesh(core_axis_name="core", subcore_axis_name="subcore")


def _dispatch(h_tiles, gate_rows, dest_windows, n_rows):
    n_tok, n_sub, _ = h_tiles.shape
    win = DISPATCH_WINDOW
    index_spec = pl.BlockSpec((1, TOP_K * win), lambda i: (i, 0))

    def scatter(body, src, src_spec, out_type):
        @pl.kernel(out_type=out_type, mesh=_sc_mesh(), scratch_types=[])
        def run(s_hbm, d_hbm, o_hbm):
            pltpu.emit_pipeline(
                functools.partial(body, o_hbm),
                grid=(n_tok // win,),
                in_specs=[src_spec, index_spec],
                out_specs=[],
                core_axis_name=("core", "subcore"),
                dimension_semantics=(pltpu.PARALLEL,),
            )(s_hbm, d_hbm)
        return run(src, dest_windows)

    def tile_window(o_hbm, x_vmem, d_vmem):
        for k in range(TOP_K):
            pltpu.sync_copy(x_vmem, o_hbm.at[d_vmem.at[0, pl.ds(k * win, win)]])

    def gate_window(o_hbm, g_vmem, d_vmem):
        for k in range(TOP_K):
            pltpu.sync_copy(g_vmem.at[k], o_hbm.at[d_vmem.at[0, pl.ds(k * win, win)]])

    xs = scatter(tile_window, h_tiles, pl.BlockSpec((win, n_sub, LANES), lambda i: (i, 0, 0)),
                 jax.ShapeDtypeStruct((n_rows, n_sub, LANES), h_tiles.dtype))
    row_gates = scatter(gate_window, gate_rows, pl.BlockSpec((TOP_K, win, GATE_LANES), lambda i: (0, i, 0)),
                        jax.ShapeDtypeStruct((n_rows, GATE_LANES), F32))
    return xs, row_gates


def _gather_sum(ys_tiles, dest_windows, n_tok):
    _, n_sub, _ = ys_tiles.shape
    win = DISPATCH_WINDOW

    @pl.kernel(out_type=jax.ShapeDtypeStruct((n_tok, n_sub, LANES), ys_tiles.dtype), mesh=_sc_mesh(),
               scratch_types=[])
    def gather_tiles(y_hbm, d_hbm, o_hbm):
        def window(d_vmem, o_vmem):
            pltpu.sync_copy(y_hbm.at[d_vmem.at[0, pl.ds(0, win)]], o_vmem)
            for k in range(1, TOP_K):
                pltpu.sync_copy(y_hbm.at[d_vmem.at[0, pl.ds(k * win, win)]], o_vmem, add=True)
        pltpu.emit_pipeline(
            window,
            grid=(n_tok // win,),
            in_specs=[pl.BlockSpec((1, TOP_K * win), lambda i: (i, 0))],
            out_specs=[pl.BlockSpec((win, n_sub, LANES), lambda i: (i, 0, 0))],
            core_axis_name=("core", "subcore"),
            dimension_semantics=(pltpu.PARALLEL,),
        )(d_hbm, o_hbm)

    return gather_tiles(ys_tiles, dest_windows)


def _expert_kernel(first_ref, nblk_ref, cnt_ref, nused_ref, xs_ref, rg_ref, wg_ref, wu_ref, wd_ref, ys_ref,
                   xbuf_ref, gbuf_ref, ybuf_ref, wgb_ref, wub_ref, wdb_ref, xsem_ref, ysem_ref):
    e = pl.program_id(0)
    n_used = nused_ref[0]
    d = wgb_ref.shape[0]
    blk_rows = xbuf_ref.shape[1]

    class _InCopy:
        def __init__(self, g, slot):
            src = xs_ref.at[pl.ds(pl.multiple_of(g * blk_rows, blk_rows), blk_rows)]
            gsrc = rg_ref.at[pl.ds(pl.multiple_of(g * MOE_ROWS, MOE_ROWS), MOE_ROWS)]
            self.copies = (pltpu.make_async_copy(src, xbuf_ref.at[slot], xsem_ref.at[slot]),
                           pltpu.make_async_copy(gsrc, gbuf_ref.at[slot], xsem_ref.at[slot]))

        def start(self):
            for cp in self.copies:
                cp.start()

        def wait(self):
            for cp in self.copies:
                cp.wait()

    x_copy = _InCopy

    def y_copy(g, slot):
        dst = ys_ref.at[pl.ds(pl.multiple_of(g * blk_rows, blk_rows), blk_rows)]
        return pltpu.make_async_copy(ybuf_ref.at[slot], dst, ysem_ref.at[slot])

    n_slot = xbuf_ref.shape[0]

    @pl.when(e == 0)
    def _():
        for g in range(n_slot - 1):
            @pl.when(g < n_used)
            def _(g=g):
                x_copy(g, g).start()

    @pl.when(nblk_ref[e] > 0)
    def _():
        wgb_ref[...] = wg_ref[0].astype(BF16)
        wub_ref[...] = wu_ref[0].astype(BF16)
        wdb_ref[...] = wd_ref[0].astype(BF16)

    def block(j, carry):
        g = first_ref[e] + j
        slot = g & (n_slot - 1)
        x_copy(g, slot).wait()

        @pl.when(g + n_slot - 1 < n_used)
        def _():
            x_copy(g + n_slot - 1, (g + n_slot - 1) & (n_slot - 1)).start()

        @pl.when(g >= n_slot)
        def _():
            y_copy(g - n_slot, slot).wait()

        x = jnp.concatenate(_load_token_tiles(xbuf_ref.at[slot], MOE_ROWS, d), axis=1)
        row = lax.broadcasted_iota(jnp.int32, (MOE_ROWS, 1), 0)
        xb = jnp.where(row < cnt_ref[e] - j * MOE_ROWS, x, 0.0).astype(BF16)
        hid = _silu(_dot(xb, wgb_ref[...])) * _dot(xb, wub_ref[...])
        y = _dot(hid.astype(BF16), wdb_ref[...]) * gbuf_ref[slot][:, 0:1]
        _store_token_tiles(ybuf_ref.at[slot], jnp.where(row < cnt_ref[e] - j * MOE_ROWS, y, 0.0))
        y_copy(g, slot).start()
        return carry
    lax.fori_loop(0, nblk_ref[e], block, 0)

    @pl.when(e == pl.num_programs(0) - 1)
    def _():
        for back in range(1, n_slot + 1):
            @pl.when(n_used >= back)
            def _(back=back):
                y_copy(n_used - back, (n_used - back) & (n_slot - 1)).wait()


def _experts(xs_tiles, row_gates, plan, w_gate, w_up, w_down):
    n_rows, n_sub, _ = xs_tiles.shape
    d = n_sub * LANES
    n_e, _, ff = w_gate.shape
    blk_rows = MOE_ROWS * n_sub
    hbm = pl.BlockSpec(memory_space=pl.ANY)
    grid_spec = pltpu.PrefetchScalarGridSpec(
        num_scalar_prefetch=4,
        grid=(n_e,),
        in_specs=[hbm, hbm,
                  pl.BlockSpec((1, d, ff), lambda e, *_: (e, 0, 0)),
                  pl.BlockSpec((1, d, ff), lambda e, *_: (e, 0, 0)),
                  pl.BlockSpec((1, ff, d), lambda e, *_: (e, 0, 0))],
        out_specs=hbm,
        scratch_shapes=[pltpu.VMEM((EXPERT_SLOTS, blk_rows, LANES), F32),
                        pltpu.VMEM((EXPERT_SLOTS, MOE_ROWS, GATE_LANES), F32),
                        pltpu.VMEM((EXPERT_SLOTS, blk_rows, LANES), F32),
                        pltpu.VMEM((d, ff), BF16), pltpu.VMEM((d, ff), BF16), pltpu.VMEM((ff, d), BF16),
                        pltpu.SemaphoreType.DMA((EXPERT_SLOTS,)), pltpu.SemaphoreType.DMA((EXPERT_SLOTS,))],
    )
    ys = pl.pallas_call(
        _expert_kernel,
        grid_spec=grid_spec,
        out_shape=jax.ShapeDtypeStruct((n_rows * n_sub, LANES), F32),
        compiler_params=_cparams(("arbitrary",)),
        name="moe_experts",
    )(*plan, xs_tiles.reshape(n_rows * n_sub, LANES), row_gates, w_gate, w_up, w_down)
    return ys.reshape(n_rows, n_sub, LANES)


def _combine_kernel(dest_ref, ys_ref, pre_ref, g2_ref, ln_g_ref, ln_b_ref, o_ref, buf_ref, sem_ref):
    tm, d = pre_ref.shape[1], pre_ref.shape[2]
    n_sub = d // LANES

    def row_copy(t, k, src_row):
        slot = pl.multiple_of((k * tm + t) * n_sub, n_sub)
        return pltpu.make_async_copy(ys_ref.at[src_row], buf_ref.at[pl.ds(slot, n_sub)], sem_ref.at[0])

    def issue(g, c):
        t0 = g * DMA_ISSUE_UNROLL
        rows = [[dest_ref[(t0 + j) * TOP_K + k] for k in range(TOP_K)] for j in range(DMA_ISSUE_UNROLL)]
        for j in range(DMA_ISSUE_UNROLL):
            for k in range(TOP_K):
                row_copy(t0 + j, k, rows[j][k]).start(priority=k % 2)
        return c
    lax.fori_loop(0, tm // DMA_ISSUE_UNROLL, issue, 0)

    def drain(t, c):
        for k in range(TOP_K):
            row_copy(t, k, dest_ref[t * TOP_K + k]).wait()
        return c
    lax.fori_loop(0, tm, drain, 0)

    pieces = []
    for s in range(n_sub):
        acc = None
        for k in range(TOP_K):
            term = buf_ref[pl.ds(k * tm * n_sub + s, tm, stride=n_sub), :]
            acc = term if acc is None else acc + term
        pieces.append(acc)
    routed = jnp.concatenate(pieces, axis=1)
    y = pre_ref[0] + g2_ref[0] * routed
    o_ref[0] = _layer_norm(y, ln_g_ref[...], ln_b_ref[...])


def _combine(ys, dest, pre, gate2, ln_g, ln_b, n_tiles):
    bsz, seq, d = pre.shape
    tm = min(DISPATCH_ROWS, seq)
    per_seq = seq // tm
    row = lambda i, *_: (i // per_seq, i % per_seq, 0)
    per_b = lambda i, *_: (i // per_seq, 0, 0)
    full = lambda *shape: pl.BlockSpec(shape, lambda i, *_: (0,) * len(shape))
    grid_spec = pltpu.PrefetchScalarGridSpec(
        num_scalar_prefetch=0,
        grid=(n_tiles,),
        in_specs=[pl.BlockSpec((TOP_K * tm,), lambda i: (i,), memory_space=pltpu.SMEM),
                  pl.BlockSpec(memory_space=pl.ANY),
                  pl.BlockSpec((1, tm, d), row), pl.BlockSpec((1, 1, d), per_b), full(1, d), full(1, d)],
        out_specs=pl.BlockSpec((1, tm, d), row),
        scratch_shapes=[pltpu.VMEM((TOP_K * tm * (d // LANES), LANES), F32), pltpu.SemaphoreType.DMA((1,))],
    )
    return pl.pallas_call(
        _combine_kernel,
        grid_spec=grid_spec,
        out_shape=jax.ShapeDtypeStruct((bsz, seq, d), F32),
        compiler_params=_cparams(("arbitrary",)),
        name="moe_combine_ln2",
    )(dest, ys, pre, gate2.reshape(bsz, 1, d), ln_g.reshape(1, d), ln_b.reshape(1, d))


def _final_ln_kernel(r_ref, pre_ref, g2_ref, ln_g_ref, ln_b_ref, *rest):
    o_ref = rest[-1]
    tm, d = pre_ref.shape[1], pre_ref.shape[2]
    routed = jnp.concatenate(_load_token_tiles(r_ref, tm, d), axis=1)
    y = pre_ref[0] + g2_ref[0] * routed
    o_ref[0] = _layer_norm(y, ln_g_ref[...], ln_b_ref[...])


def _final_ln(routed_tiles, pre, gate2, ln_g, ln_b, first_tile, partial=None):
    bsz, seq, d = pre.shape
    tm = min(DISPATCH_ROWS, seq)
    per_seq = seq // tm
    n_sub = d // LANES
    n_tiles = bsz * per_seq - first_tile
    row = lambda i: ((i + first_tile) // per_seq, (i + first_tile) % per_seq, 0)
    per_b = lambda i: ((i + first_tile) // per_seq, 0, 0)
    full = lambda *shape: pl.BlockSpec(shape, lambda i: (0,) * len(shape))
    in_specs = [pl.BlockSpec((tm * n_sub, LANES), lambda i: (i, 0)),
                pl.BlockSpec((1, tm, d), row), pl.BlockSpec((1, 1, d), per_b), full(1, d), full(1, d)]
    args = [routed_tiles.reshape(-1, LANES), pre, gate2.reshape(bsz, 1, d), ln_g.reshape(1, d),
            ln_b.reshape(1, d)]
    aliases = {}
    if partial is not None:
        in_specs.append(pl.BlockSpec(memory_space=pl.ANY))
        args.append(partial)
        aliases = {len(args) - 1: 0}
    return pl.pallas_call(
        _final_ln_kernel,
        grid=(n_tiles,),
        in_specs=in_specs,
        out_specs=pl.BlockSpec((1, tm, d), row),
        out_shape=jax.ShapeDtypeStruct((bsz, seq, d), F32),
        input_output_aliases=aliases,
        compiler_params=_cparams(("arbitrary",)),
        name="moe_final_ln2",
    )(*args)


def _dest_kernel(pstart_ref, idx_ref, rank_ref, dest_ref):
    idx = idx_ref[...]

    def body(e, acc):
        return acc + jnp.where(idx == e, pstart_ref[e], 0)
    dest_ref[...] = lax.fori_loop(0, pstart_ref.shape[0], body, rank_ref[...], unroll=8)


def _dest_rows(pstart, idx, rank):
    n_k, n_tok = idx.shape
    tn = min(4096, n_tok)
    grid_spec = pltpu.PrefetchScalarGridSpec(
        num_scalar_prefetch=1,
        grid=(n_tok // tn,),
        in_specs=[pl.BlockSpec((n_k, tn), lambda i, *_: (0, i))] * 2,
        out_specs=pl.BlockSpec((n_k, tn), lambda i, *_: (0, i)),
    )
    return pl.pallas_call(
        _dest_kernel,
        grid_spec=grid_spec,
        out_shape=jax.ShapeDtypeStruct((n_k, n_tok), jnp.int32),
        compiler_params=_cparams(("arbitrary",)),
        name="moe_dest",
    )(pstart, idx, rank)


def _moe_plan(idx, rank, cnt, n_tok):
    n_e = cnt.shape[0]
    padded = (cnt + MOE_ROWS - 1) // MOE_ROWS * MOE_ROWS
    pend = jnp.cumsum(padded).astype(jnp.int32)
    pstart = pend - padded
    dest_kt = _dest_rows(pstart, idx, rank)
    dest = dest_kt.T.reshape(-1)
    win = min(DISPATCH_WINDOW, n_tok)
    dest_windows = dest_kt.reshape(TOP_K, n_tok // win, win).transpose(1, 0, 2).reshape(n_tok // win, TOP_K * win)
    n_blk = n_tok * TOP_K // MOE_ROWS + n_e
    n_used = (pend[-1:] // MOE_ROWS).astype(jnp.int32)
    plan = ((pstart // MOE_ROWS).astype(jnp.int32), (padded // MOE_ROWS).astype(jnp.int32),
            cnt.astype(jnp.int32), n_used)
    return dest, dest_windows, plan, n_blk * MOE_ROWS


def kernel(x, c, rel_bias, w_ada, b_ada, w_in, ssm_a_re, ssm_a_im, ssm_b_re, ssm_b_im, ssm_c_re, ssm_c_im, ssm_d, ssm_log_dt, w_glu, b_glu, g_att, g_ssm, w_out, ln1_g, ln1_b, w_router, router_bias, w_e_gate, w_e_up, w_e_down, w_s_gate, w_s_up, w_s_down, ln2_g, ln2_b):
    bsz, seq, d = x.shape
    depth = w_ada.shape[0]
    alpha = (2 * depth) ** 0.25
    att_w = g_att.shape[-1]
    tables = _att_tables(rel_bias)
    for layer in range(depth):
        ada = _ada(c, w_ada[layer], b_ada[layer])
        shift1, scale1, gate1, shift2, scale2, gate2 = jnp.split(ada, 6, axis=-1)
        q, k, v, u = _inproj(x, scale1, shift1, w_in[layer], att_w)
        att = _attention(q, k, v, tables)
        ssm_prm = _ssm_params(ssm_a_re[layer], ssm_a_im[layer], ssm_b_re[layer], ssm_b_im[layer],
                              ssm_c_re[layer], ssm_c_im[layer], ssm_d[layer], ssm_log_dt[layer])
        ssm = _ssm(u, *ssm_prm, w_glu[layer], b_glu[layer])
        h2_tiles, pre = _mix(x, att, ssm, g_att[layer], g_ssm[layer], w_out[layer], gate1,
                             ln1_g[layer], ln1_b[layer], scale2, shift2, gate2, w_s_gate[layer],
                             w_s_up[layer], w_s_down[layer], alpha)
        idx, gate, rank, cnt = _route(h2_tiles, w_router[layer], router_bias[layer])
        dest, dest_windows, plan, n_rows = _moe_plan(idx, rank, cnt, bsz * seq)
        n_tok = bsz * seq
        gate_rows = jnp.broadcast_to(gate[:, :, None], (TOP_K, n_tok, GATE_LANES))
        xs, row_gates = _dispatch(h2_tiles.reshape(n_tok, d // LANES, LANES), gate_rows, dest_windows, n_rows)
        ys = _experts(xs, row_gates, plan, w_e_gate[layer], w_e_up[layer], w_e_down[layer])
        tm = min(DISPATCH_ROWS, seq)
        tc_tiles = min(COMBINE_TC_TILES, n_tok // tm)
        tc_tokens = tc_tiles * tm
        win = min(DISPATCH_WINDOW, n_tok)
        partial = (_combine(ys, dest, pre, gate2, ln2_g[layer], ln2_b[layer], tc_tiles)
                   if tc_tiles > 0 else None)
        routed = _gather_sum(ys, dest_windows[tc_tokens // win:], n_tok - tc_tokens)
        x = _final_ln(routed, pre, gate2, ln2_g[layer], ln2_b[layer], tc_tiles, partial)
    return x
```
